```python
import jax, jax.numpy as jnp
from jax import lax
import numpy as np


D_MODEL = 1024
BATCH = 8
SEQ = 2048
DEPTH = 1

CHUNK = 64
Q_BLOCK = 128
D_A = D_MODEL // 2
HEAD_A = 64
H_A = D_A // HEAD_A
DECAY_RANK = 64
AAA_RANK = 64
D_B = D_MODEL // 2
HEAD_B = 64
H_B = D_B // HEAD_B
RWKV_COLS = 4 * D_A + DECAY_RANK + AAA_RANK
FOX_COLS = 4 * D_B + H_B
GATE_COLS = 2 * D_MODEL
IN_COLS = RWKV_COLS + FOX_COLS + GATE_COLS
RMS_EPS = 1e-6
LNX_EPS = 64e-5

kernel_name = 'hybrid_rwkv7_fox_gated_block'


def _rmsnorm(x, g):
    xf = x.astype(jnp.float32)
    y = xf * lax.rsqrt(jnp.mean(xf * xf, axis=-1, keepdims=True) + RMS_EPS)
    return (y * g.astype(jnp.float32)).astype(x.dtype)


def _token_shift(u):
    return jnp.pad(u, ((0, 0), (1, 0), (0, 0)))[:, :-1]


def _wkv7_scan(r, decay, k, v, a_vec, b_vec):
    B, S, H, N = r.shape

    def step(state, inp):
        r_t, w_t, k_t, v_t, a_t, b_t = inp
        sa = jnp.einsum('bhij,bhj->bhi', state, a_t)
        state = (state * w_t[:, :, None, :]
                 + sa[..., None] * b_t[:, :, None, :]
                 + v_t[..., None] * k_t[:, :, None, :])
        return state, jnp.einsum('bhij,bhj->bhi', state, r_t)

    xs = (jnp.moveaxis(r, 1, 0), jnp.moveaxis(decay, 1, 0), jnp.moveaxis(k, 1, 0),
          jnp.moveaxis(v, 1, 0), jnp.moveaxis(a_vec, 1, 0), jnp.moveaxis(b_vec, 1, 0))
    state0 = jnp.zeros((B, H, N, N), jnp.float32)
    _, y = lax.scan(step, state0, xs)
    return jnp.moveaxis(y, 0, 1)


def _rwkv7_mixer(u, mu, w_up, w0, a_up, a0, k_k, k_a, r_k, lnx_w, lnx_b):
    B, S, _ = u.shape
    u = u + (_token_shift(u) - u) * mu
    r, k, v, wd, ad, gate = jnp.split(
        u, [D_A, 2 * D_A, 3 * D_A, 3 * D_A + DECAY_RANK, 3 * D_A + DECAY_RANK + AAA_RANK], axis=-1)
    w = -jax.nn.softplus(-(w0 + jnp.tanh(wd) @ w_up)) - 0.5
    decay = jnp.exp(-jnp.exp(w.astype(jnp.float32)))
    a = jax.nn.sigmoid(a0 + ad @ a_up)
    heads = lambda t: t.reshape(B, S, H_A, HEAD_A).astype(jnp.float32)
    kk = heads(k * k_k)
    kk = kk / jnp.maximum(jnp.linalg.norm(kk, axis=-1, keepdims=True), 1e-12)
    k = k * (1.0 + (a - 1.0) * k_a)
    r_h, k_h, v_h, a_h = heads(r), heads(k), heads(v), heads(a)
    y = _wkv7_scan(r_h, heads(decay), k_h, v_h, -kk, kk * a_h)
    mean = jnp.mean(y, axis=-1, keepdims=True)
    var = jnp.mean(jnp.square(y - mean), axis=-1, keepdims=True)
    y = (y - mean) * lax.rsqrt(var + LNX_EPS)
    y = y * lnx_w.reshape(H_A, HEAD_A).astype(jnp.float32) + lnx_b.reshape(H_A, HEAD_A).astype(jnp.float32)
    bonus = jnp.sum(r_h * k_h * r_k.astype(jnp.float32), axis=-1, keepdims=True) * v_h
    y = (y + bonus).reshape(B, S, D_A).astype(u.dtype)
    return y * jax.nn.silu(gate)


def _fox_mixer(u, f_bias, q_norm_g, k_norm_g):
    B, S, _ = u.shape
    q, k, v, gate, f_logit = jnp.split(u, [D_B, 2 * D_B, 3 * D_B, 4 * D_B], axis=-1)
    to_bhsd = lambda t: jnp.transpose(t, (0, 2, 1, 3))
    q = to_bhsd(_rmsnorm(q.reshape(B, S, H_B, HEAD_B), q_norm_g))
    k = to_bhsd(_rmsnorm(k.reshape(B, S, H_B, HEAD_B), k_norm_g))
    v = to_bhsd(v.reshape(B, S, H_B, HEAD_B))
    log_f = jax.nn.log_sigmoid((f_logit + f_bias).astype(jnp.float32))
    cum = jnp.transpose(jnp.cumsum(log_f, axis=1), (0, 2, 1))
    scale = HEAD_B ** -0.5
    outs = []
    for i in range(S // Q_BLOCK):
        lo, hi = i * Q_BLOCK, (i + 1) * Q_BLOCK
        qb, kp, vp = q[:, :, lo:hi], k[:, :, :hi], v[:, :, :hi]
        logits = (jnp.einsum('bhqd,bhkd->bhqk', qb, kp).astype(jnp.float32) * scale
                  + cum[:, :, lo:hi, None] - cum[:, :, None, :hi])
        causal = (lo + jnp.arange(Q_BLOCK))[:, None] >= jnp.arange(hi)[None, :]
        logits = jnp.where(causal, logits, -jnp.inf)
        p = jax.nn.softmax(logits, axis=-1)
        outs.append(jnp.einsum('bhqk,bhkd->bhqd', p.astype(vp.dtype), vp))
    o = jnp.concatenate(outs, axis=2)
    o = jnp.transpose(o, (0, 2, 1, 3)).reshape(B, S, D_B)
    return o * jax.nn.silu(gate)


def _fwd_setup_inputs(seed: int = 0) -> dict:
    key = jax.random.key(seed)
    ks = jax.random.split(key, 20)
    nrm = lambda k, shape, s: jax.random.normal(k, shape, jnp.float32) * s
    unif = lambda k, shape, lo, hi: jax.random.uniform(k, shape, jnp.float32, lo, hi)
    return {
        'x': jax.random.normal(ks[0], (BATCH, SEQ, D_MODEL), jnp.float32),
        'norm_g': 1.0 + nrm(ks[1], (DEPTH, D_MODEL), 0.02),
        'w_in': nrm(ks[2], (DEPTH, D_MODEL, IN_COLS), D_MODEL ** -0.5),
        'shift_mu': unif(ks[3], (DEPTH, RWKV_COLS), 0.0, 1.0),
        'w_lora_up': nrm(ks[4], (DEPTH, DECAY_RANK, D_A), 0.1 * DECAY_RANK ** -0.5),
        'w0': unif(ks[5], (DEPTH, D_A), -6.0, -1.0),
        'a_lora_up': nrm(ks[6], (DEPTH, AAA_RANK, D_A), 0.1 * AAA_RANK ** -0.5),
        'a0': nrm(ks[7], (DEPTH, D_A), 0.5),
        'k_k': 0.85 + nrm(ks[8], (DEPTH, D_A), 0.02),
        'k_a': 1.0 + nrm(ks[9], (DEPTH, D_A), 0.02),
        'r_k': nrm(ks[10], (DEPTH, H_A, HEAD_A), 0.1),
        'lnx_w': 1.0 + nrm(ks[11], (DEPTH, D_A), 0.02),
        'lnx_b': nrm(ks[12], (DEPTH, D_A), 0.02),
        'f_bias': unif(ks[13], (DEPTH, H_B), 1.0, 5.0),
        'q_norm_g': 1.0 + nrm(ks[14], (DEPTH, HEAD_B), 0.02),
        'k_norm_g': 1.0 + nrm(ks[15], (DEPTH, HEAD_B), 0.02),
        'w_out_a': nrm(ks[16], (DEPTH, D_A, D_MODEL), D_A ** -0.5),
        'w_out_b': nrm(ks[17], (DEPTH, D_B, D_MODEL), D_B ** -0.5),
        'w_out': nrm(ks[18], (DEPTH, D_MODEL, D_MODEL), D_MODEL ** -0.5),
        'final_norm_g': 1.0 + nrm(ks[19], (D_MODEL,), 0.02),
    }


def _fwd_reference(x, norm_g, w_in, shift_mu, w_lora_up, w0, a_lora_up, a0, k_k, k_a, r_k,
              lnx_w, lnx_b, f_bias, q_norm_g, k_norm_g, w_out_a, w_out_b, w_out, final_norm_g):
    for l in range(DEPTH):
        h = _rmsnorm(x, norm_g[l])
        u = h @ w_in[l]
        u_a, u_b, u_g = jnp.split(u, [RWKV_COLS, RWKV_COLS + FOX_COLS], axis=-1)
        y_a = _rwkv7_mixer(u_a, shift_mu[l], w_lora_up[l], w0[l], a_lora_up[l], a0[l],
                           k_k[l], k_a[l], r_k[l], lnx_w[l], lnx_b[l]) @ w_out_a[l]
        y_b = _fox_mixer(u_b, f_bias[l], q_norm_g[l], k_norm_g[l]) @ w_out_b[l]
        g_a, g_b = jnp.split(u_g, 2, axis=-1)
        merged = jax.nn.sigmoid(g_a) * y_a + jax.nn.sigmoid(g_b) * y_b
        x = x + merged @ w_out[l]
    return _rmsnorm(x, final_norm_g)


import jax as _jax
import jax.numpy as _jnp

TWIN_FORMAT = 'train_step'
FWD_PARAMS = ['x', 'norm_g', 'w_in', 'shift_mu', 'w_lora_up', 'w0', 'a_lora_up', 'a0', 'k_k', 'k_a', 'r_k', 'lnx_w', 'lnx_b', 'f_bias', 'q_norm_g', 'k_norm_g', 'w_out_a', 'w_out_b', 'w_out', 'final_norm_g']
TWIN_WEIGHTS = ['norm_g', 'w_in', 'shift_mu', 'w_lora_up', 'w0', 'a_lora_up', 'a0', 'k_k', 'k_a', 'r_k', 'lnx_w', 'lnx_b', 'f_bias', 'q_norm_g', 'k_norm_g', 'w_out_a', 'w_out_b', 'w_out', 'final_norm_g']
TWIN_DIFF_INPUT = 'x'
TWIN_INPUTS = ['x', 'norm_g', 'w_in', 'shift_mu', 'w_lora_up', 'w0', 'a_lora_up', 'a0', 'k_k', 'k_a', 'r_k', 'lnx_w', 'lnx_b', 'f_bias', 'q_norm_g', 'k_norm_g', 'w_out_a', 'w_out_b', 'w_out', 'final_norm_g', 'loss_target', 'm_norm_g', 'm_w_in', 'm_shift_mu', 'm_w_lora_up', 'm_w0', 'm_a_lora_up', 'm_a0', 'm_k_k', 'm_k_a', 'm_r_k', 'm_lnx_w', 'm_lnx_b', 'm_f_bias', 'm_q_norm_g', 'm_k_norm_g', 'm_w_out_a', 'm_w_out_b', 'm_w_out', 'm_final_norm_g', 'v_norm_g', 'v_w_in', 'v_shift_mu', 'v_w_lora_up', 'v_w0', 'v_a_lora_up', 'v_a0', 'v_k_k', 'v_k_a', 'v_r_k', 'v_lnx_w', 'v_lnx_b', 'v_f_bias', 'v_q_norm_g', 'v_k_norm_g', 'v_w_out_a', 'v_w_out_b', 'v_w_out', 'v_final_norm_g']
TWIN_OUTPUTS = ['loss', 'grad_x', 'grad_norm_g', 'grad_w_in', 'grad_shift_mu', 'grad_w_lora_up', 'grad_w0', 'grad_a_lora_up', 'grad_a0', 'grad_k_k', 'grad_k_a', 'grad_r_k', 'grad_lnx_w', 'grad_lnx_b', 'grad_f_bias', 'grad_q_norm_g', 'grad_k_norm_g', 'grad_w_out_a', 'grad_w_out_b', 'grad_w_out', 'grad_final_norm_g', 'delta_norm_g', 'delta_w_in', 'delta_shift_mu', 'delta_w_lora_up', 'delta_w0', 'delta_a_lora_up', 'delta_a0', 'delta_k_k', 'delta_k_a', 'delta_r_k', 'delta_lnx_w', 'delta_lnx_b', 'delta_f_bias', 'delta_q_norm_g', 'delta_k_norm_g', 'delta_w_out_a', 'delta_w_out_b', 'delta_w_out', 'delta_final_norm_g', 'new_m_norm_g', 'new_m_w_in', 'new_m_shift_mu', 'new_m_w_lora_up', 'new_m_w0', 'new_m_a_lora_up', 'new_m_a0', 'new_m_k_k', 'new_m_k_a', 'new_m_r_k', 'new_m_lnx_w', 'new_m_lnx_b', 'new_m_f_bias', 'new_m_q_norm_g', 'new_m_k_norm_g', 'new_m_w_out_a', 'new_m_w_out_b', 'new_m_w_out', 'new_m_final_norm_g', 'new_v_norm_g', 'new_v_w_in', 'new_v_shift_mu', 'new_v_w_lora_up', 'new_v_w0', 'new_v_a_lora_up', 'new_v_a0', 'new_v_k_k', 'new_v_k_a', 'new_v_r_k', 'new_v_lnx_w', 'new_v_lnx_b', 'new_v_f_bias', 'new_v_q_norm_g', 'new_v_k_norm_g', 'new_v_w_out_a', 'new_v_w_out_b', 'new_v_w_out', 'new_v_final_norm_g']
TWIN_LEAF_KINDS = {'loss': 'loss', 'grad_x': 'grad_x', 'grad_norm_g': 'grad_w', 'grad_w_in': 'grad_w', 'grad_shift_mu': 'grad_w', 'grad_w_lora_up': 'grad_w', 'grad_w0': 'grad_w', 'grad_a_lora_up': 'grad_w', 'grad_a0': 'grad_w', 'grad_k_k': 'grad_w', 'grad_k_a': 'grad_w', 'grad_r_k': 'grad_w', 'grad_lnx_w': 'grad_w', 'grad_lnx_b': 'grad_w', 'grad_f_bias': 'grad_w', 'grad_q_norm_g': 'grad_w', 'grad_k_norm_g': 'grad_w', 'grad_w_out_a': 'grad_w', 'grad_w_out_b': 'grad_w', 'grad_w_out': 'grad_w', 'grad_final_norm_g': 'grad_w', 'delta_norm_g': 'delta_w', 'delta_w_in': 'delta_w', 'delta_shift_mu': 'delta_w', 'delta_w_lora_up': 'delta_w', 'delta_w0': 'delta_w', 'delta_a_lora_up': 'delta_w', 'delta_a0': 'delta_w', 'delta_k_k': 'delta_w', 'delta_k_a': 'delta_w', 'delta_r_k': 'delta_w', 'delta_lnx_w': 'delta_w', 'delta_lnx_b': 'delta_w', 'delta_f_bias': 'delta_w', 'delta_q_norm_g': 'delta_w', 'delta_k_norm_g': 'delta_w', 'delta_w_out_a': 'delta_w', 'delta_w_out_b': 'delta_w', 'delta_w_out': 'delta_w', 'delta_final_norm_g': 'delta_w', 'new_m_norm_g': 'new_m', 'new_m_w_in': 'new_m', 'new_m_shift_mu': 'new_m', 'new_m_w_lora_up': 'new_m', 'new_m_w0': 'new_m', 'new_m_a_lora_up': 'new_m', 'new_m_a0': 'new_m', 'new_m_k_k': 'new_m', 'new_m_k_a': 'new_m', 'new_m_r_k': 'new_m', 'new_m_lnx_w': 'new_m', 'new_m_lnx_b': 'new_m', 'new_m_f_bias': 'new_m', 'new_m_q_norm_g': 'new_m', 'new_m_k_norm_g': 'new_m', 'new_m_w_out_a': 'new_m', 'new_m_w_out_b': 'new_m', 'new_m_w_out': 'new_m', 'new_m_final_norm_g': 'new_m', 'new_v_norm_g': 'new_v', 'new_v_w_in': 'new_v', 'new_v_shift_mu': 'new_v', 'new_v_w_lora_up': 'new_v', 'new_v_w0': 'new_v', 'new_v_a_lora_up': 'new_v', 'new_v_a0': 'new_v', 'new_v_k_k': 'new_v', 'new_v_k_a': 'new_v', 'new_v_r_k': 'new_v', 'new_v_lnx_w': 'new_v', 'new_v_lnx_b': 'new_v', 'new_v_f_bias': 'new_v', 'new_v_q_norm_g': 'new_v', 'new_v_k_norm_g': 'new_v', 'new_v_w_out_a': 'new_v', 'new_v_w_out_b': 'new_v', 'new_v_w_out': 'new_v', 'new_v_final_norm_g': 'new_v'}


def _forward(args):
    return _fwd_reference(*[args[k] for k in FWD_PARAMS])


def _output_shape():
    out = _jax.eval_shape(lambda: _forward(_fwd_setup_inputs(0)))
    return out.shape, out.dtype

N_MICROBATCH = 1
ADAM_LR = 0.001
ADAM_B1 = 0.9
ADAM_B2 = 0.999
ADAM_EPS = 1e-08
ADAM_WD = 0.01
ADAM_STEP = 10
PER_EXAMPLE_BATCH_AXIS = {'x': 0, 'loss_target': 0}
SHARED_INPUTS = []
_WEIGHT_DTYPES = {'norm_g': _jnp.float32, 'w_in': _jnp.float32, 'shift_mu': _jnp.float32, 'w_lora_up': _jnp.float32, 'w0': _jnp.float32, 'a_lora_up': _jnp.float32, 'a0': _jnp.float32, 'k_k': _jnp.float32, 'k_a': _jnp.float32, 'r_k': _jnp.float32, 'lnx_w': _jnp.float32, 'lnx_b': _jnp.float32, 'f_bias': _jnp.float32, 'q_norm_g': _jnp.float32, 'k_norm_g': _jnp.float32, 'w_out_a': _jnp.float32, 'w_out_b': _jnp.float32, 'w_out': _jnp.float32, 'final_norm_g': _jnp.float32}
MOMENT_SCALE = {'norm_g': 7.510364e-02, 'w_in': 2.873863e-02, 'shift_mu': 7.609827e-02, 'w_lora_up': 2.087024e-03, 'w0': 1.613881e-02, 'a_lora_up': 1.744581e-02, 'a0': 2.003040e-02, 'k_k': 5.006206e-02, 'k_a': 5.452429e-02, 'r_k': 1.115010e-01, 'lnx_w': 4.762222e-02, 'lnx_b': 4.783090e-02, 'f_bias': 6.165195e-02, 'q_norm_g': 3.734431e-02, 'k_norm_g': 3.750917e-02, 'w_out_a': 3.193405e-02, 'w_out_b': 1.214899e-02, 'w_out': 3.367338e-02, 'final_norm_g': 1.598422e+01}


def _to_microbatches(a, axis):
    t = _jnp.moveaxis(a, axis, 0)
    t = t.reshape((N_MICROBATCH, t.shape[0] // N_MICROBATCH) + t.shape[1:])
    return _jnp.moveaxis(t, 1, axis + 1)


def setup_inputs(seed: int = 0) -> dict:
    inp = _fwd_setup_inputs(seed)
    key = _jax.random.fold_in(_jax.random.key(seed), 7919)
    shape, _ = _output_shape()
    out = dict(inp)
    out["loss_target"] = _jax.random.normal(_jax.random.fold_in(key, 0), shape, _jnp.float32)
    for i, name in enumerate(TWIN_WEIGHTS):
        w = inp[name].astype(_jnp.float32)
        if MOMENT_SCALE is None:
            s = _jnp.sqrt(_jnp.mean(_jnp.square(w)) + 1e-30)
        else:
            s = MOMENT_SCALE[name]
        km, kv = _jax.random.split(_jax.random.fold_in(key, i + 1))
        out[name] = w
        out["m_" + name] = s * _jax.random.normal(km, w.shape, _jnp.float32)
        out["v_" + name] = (s * s) * _jax.random.uniform(kv, w.shape, _jnp.float32, 0.5, 1.5)
    if N_MICROBATCH > 1:
        for name, axis in PER_EXAMPLE_BATCH_AXIS.items():
            out[name] = _to_microbatches(out[name], axis)
    return {'x': out['x'], 'norm_g': out['norm_g'], 'w_in': out['w_in'], 'shift_mu': out['shift_mu'], 'w_lora_up': out['w_lora_up'], 'w0': out['w0'], 'a_lora_up': out['a_lora_up'], 'a0': out['a0'], 'k_k': out['k_k'], 'k_a': out['k_a'], 'r_k': out['r_k'], 'lnx_w': out['lnx_w'], 'lnx_b': out['lnx_b'], 'f_bias': out['f_bias'], 'q_norm_g': out['q_norm_g'], 'k_norm_g': out['k_norm_g'], 'w_out_a': out['w_out_a'], 'w_out_b': out['w_out_b'], 'w_out': out['w_out'], 'final_norm_g': out['final_norm_g'], 'loss_target': out['loss_target'], 'm_norm_g': out['m_norm_g'], 'm_w_in': out['m_w_in'], 'm_shift_mu': out['m_shift_mu'], 'm_w_lora_up': out['m_w_lora_up'], 'm_w0': out['m_w0'], 'm_a_lora_up': out['m_a_lora_up'], 'm_a0': out['m_a0'], 'm_k_k': out['m_k_k'], 'm_k_a': out['m_k_a'], 'm_r_k': out['m_r_k'], 'm_lnx_w': out['m_lnx_w'], 'm_lnx_b': out['m_lnx_b'], 'm_f_bias': out['m_f_bias'], 'm_q_norm_g': out['m_q_norm_g'], 'm_k_norm_g': out['m_k_norm_g'], 'm_w_out_a': out['m_w_out_a'], 'm_w_out_b': out['m_w_out_b'], 'm_w_out': out['m_w_out'], 'm_final_norm_g': out['m_final_norm_g'], 'v_norm_g': out['v_norm_g'], 'v_w_in': out['v_w_in'], 'v_shift_mu': out['v_shift_mu'], 'v_w_lora_up': out['v_w_lora_up'], 'v_w0': out['v_w0'], 'v_a_lora_up': out['v_a_lora_up'], 'v_a0': out['v_a0'], 'v_k_k': out['v_k_k'], 'v_k_a': out['v_k_a'], 'v_r_k': out['v_r_k'], 'v_lnx_w': out['v_lnx_w'], 'v_lnx_b': out['v_lnx_b'], 'v_f_bias': out['v_f_bias'], 'v_q_norm_g': out['v_q_norm_g'], 'v_k_norm_g': out['v_k_norm_g'], 'v_w_out_a': out['v_w_out_a'], 'v_w_out_b': out['v_w_out_b'], 'v_w_out': out['v_w_out'], 'v_final_norm_g': out['v_final_norm_g']}


def _loss(weights, diff, rest, loss_target):
    with _jax.named_scope("forward"):
        args = {**rest, TWIN_DIFF_INPUT: diff, **{k: w.astype(_WEIGHT_DTYPES[k]) for k, w in weights.items()}}
        y = _forward(args)
    with _jax.named_scope("loss_head"):
        err = _jnp.square(y.astype(_jnp.float32) - loss_target)
        return 0.5 * _jnp.sum(_jnp.mean(err, axis=-1)) if err.ndim else 0.5 * err


def _adamw(w, g, m, v):
    m = ADAM_B1 * m + (1.0 - ADAM_B1) * g
    v = ADAM_B2 * v + (1.0 - ADAM_B2) * _jnp.square(g)
    m_hat = m / (1.0 - ADAM_B1 ** ADAM_STEP)
    v_hat = v / (1.0 - ADAM_B2 ** ADAM_STEP)
    delta = -ADAM_LR * (m_hat / (_jnp.sqrt(v_hat) + ADAM_EPS) + ADAM_WD * w)
    return delta, m, v


def reference(x, norm_g, w_in, shift_mu, w_lora_up, w0, a_lora_up, a0, k_k, k_a, r_k, lnx_w, lnx_b, f_bias, q_norm_g, k_norm_g, w_out_a, w_out_b, w_out, final_norm_g, loss_target, m_norm_g, m_w_in, m_shift_mu, m_w_lora_up, m_w0, m_a_lora_up, m_a0, m_k_k, m_k_a, m_r_k, m_lnx_w, m_lnx_b, m_f_bias, m_q_norm_g, m_k_norm_g, m_w_out_a, m_w_out_b, m_w_out, m_final_norm_g, v_norm_g, v_w_in, v_shift_mu, v_w_lora_up, v_w0, v_a_lora_up, v_a0, v_k_k, v_k_a, v_r_k, v_lnx_w, v_lnx_b, v_f_bias, v_q_norm_g, v_k_norm_g, v_w_out_a, v_w_out_b, v_w_out, v_final_norm_g):
    given = dict(x=x, norm_g=norm_g, w_in=w_in, shift_mu=shift_mu, w_lora_up=w_lora_up, w0=w0, a_lora_up=a_lora_up, a0=a0, k_k=k_k, k_a=k_a, r_k=r_k, lnx_w=lnx_w, lnx_b=lnx_b, f_bias=f_bias, q_norm_g=q_norm_g, k_norm_g=k_norm_g, w_out_a=w_out_a, w_out_b=w_out_b, w_out=w_out, final_norm_g=final_norm_g, loss_target=loss_target, m_norm_g=m_norm_g, m_w_in=m_w_in, m_shift_mu=m_shift_mu, m_w_lora_up=m_w_lora_up, m_w0=m_w0, m_a_lora_up=m_a_lora_up, m_a0=m_a0, m_k_k=m_k_k, m_k_a=m_k_a, m_r_k=m_r_k, m_lnx_w=m_lnx_w, m_lnx_b=m_lnx_b, m_f_bias=m_f_bias, m_q_norm_g=m_q_norm_g, m_k_norm_g=m_k_norm_g, m_w_out_a=m_w_out_a, m_w_out_b=m_w_out_b, m_w_out=m_w_out, m_final_norm_g=m_final_norm_g, v_norm_g=v_norm_g, v_w_in=v_w_in, v_shift_mu=v_shift_mu, v_w_lora_up=v_w_lora_up, v_w0=v_w0, v_a_lora_up=v_a_lora_up, v_a0=v_a0, v_k_k=v_k_k, v_k_a=v_k_a, v_r_k=v_r_k, v_lnx_w=v_lnx_w, v_lnx_b=v_lnx_b, v_f_bias=v_f_bias, v_q_norm_g=v_q_norm_g, v_k_norm_g=v_k_norm_g, v_w_out_a=v_w_out_a, v_w_out_b=v_w_out_b, v_w_out=v_w_out, v_final_norm_g=v_final_norm_g)
    weights = {n: given[n] for n in TWIN_WEIGHTS}
    shared = {n: given[n] for n in SHARED_INPUTS}
    per_example = {n: given[n] for n in ['x']}
    grad_fn = _jax.value_and_grad(_loss, argnums=(0, 1))

    def one_microbatch(ex, loss_target):
        ex = dict(ex)
        diff = ex.pop(TWIN_DIFF_INPUT)
        return grad_fn(weights, diff, {**shared, **ex}, loss_target)

    if N_MICROBATCH == 1:
        loss, (grad_w, grad_x) = one_microbatch(per_example, given["loss_target"])
    else:
        def body(carry, xs):
            loss_sum, grad_sum = carry
            l_k, (gw_k, gx_k) = one_microbatch(xs[0], xs[1])
            with _jax.named_scope("update"):
                return (loss_sum + l_k, _jax.tree.map(_jnp.add, grad_sum, gw_k)), gx_k

        init = (_jnp.zeros((), _jnp.float32), _jax.tree.map(_jnp.zeros_like, weights))
        (loss, grad_w), grad_x = _jax.lax.scan(body, init, (per_example, given["loss_target"]))
    with _jax.named_scope("update"):
        delta_w, new_m, new_v = {}, {}, {}
        for n in TWIN_WEIGHTS:
            delta_w[n], new_m[n], new_v[n] = _adamw(weights[n], grad_w[n], given["m_" + n], given["v_" + n])
    return (loss, grad_x, *[grad_w[n] for n in TWIN_WEIGHTS], *[delta_w[n] for n in TWIN_WEIGHTS],
            *[new_m[n] for n in TWIN_WEIGHTS], *[new_v[n] for n in TWIN_WEIGHTS])
```

```python
import functools
import math

import jax
import jax.numpy as jnp
from jax import lax
from jax.experimental import pallas as pl
from jax.experimental.pallas import tpu as pltpu

F32 = jnp.float32
BF16 = jnp.bfloat16
HI = lax.Precision.HIGHEST
MESH = pl.DeviceIdType.MESH

N_DEV = 8
D = 1024
H = 8
N = 64
DA = H * N
RANK = 64
NA = 4 * DA + 2 * RANK
NB = 4 * DA
NG = 2 * D
NF = 128
IN_COLS = NA + NB + H + NG
COLS_PER_DEV = IN_COLS // N_DEV
RMS_EPS = 1e-6
LNX_EPS = 64e-5
ATT_SCALE = N ** -0.5

ADAM_LR = 0.001
ADAM_B1 = 0.9
ADAM_B2 = 0.999
ADAM_EPS = 1e-08
ADAM_WD = 0.01
ADAM_STEP = 10

LANES = 128
WKV_CHUNK = 64
TOK_TILE = 256
HEAD_TILE = 128
ATT_TILE = 256
VMEM_LIMIT = 56 * 1024 * 1024

BIG_ROWS = (("w_in", D * COLS_PER_DEV // LANES), ("w_out_a", DA), ("w_out_b", DA), ("w_out", D),
            ("w_lora_up", RANK * N // LANES), ("a_lora_up", RANK * N // LANES))
BIG_TOTAL = sum(r for _, r in BIG_ROWS)
GATHER_ROWS = 8400
SMALL = (("norm_g", D), ("final_norm_g", D), ("shift_mu", NA), ("w0", DA), ("a0", DA), ("k_k", DA), ("k_a", DA),
         ("r_k", DA), ("lnx_w", DA), ("lnx_b", DA), ("q_norm_g", N), ("k_norm_g", N), ("f_bias", H))
SMALL_ROWS = 64
PACK_ROWS = 8464
ADAM_TILE = 368


def _params(*sem):
    return pltpu.CompilerParams(dimension_semantics=sem or None, vmem_limit_bytes=VMEM_LIMIT)


def _bdot(a, b):
    return jnp.dot(a.astype(BF16), b.astype(BF16), preferred_element_type=F32)


def _bdot_nt(a, b):
    return lax.dot_general(a.astype(BF16), b.astype(BF16), (((1,), (1,)), ((), ())), preferred_element_type=F32)


def _bdot_tn(a, b):
    return lax.dot_general(a.astype(BF16), b.astype(BF16), (((0,), (0,)), ((), ())), preferred_element_type=F32)


def _sigmoid(x):
    return 1.0 / (1.0 + jnp.exp(-x))


def _softplus(x):
    return jnp.maximum(x, 0.0) + jnp.log(1.0 + jnp.exp(-jnp.abs(x)))


def _heads(ref, col0):
    return jnp.stack([ref[:, col0 + N * h:col0 + N * (h + 1)] for h in range(H)])


def _store_heads(ref, col0, val):
    for h in range(H):
        ref[:, col0 + N * h:col0 + N * (h + 1)] = val[h]


def _lerp(c, s, mu):
    return c + (s - c) * mu


def _rwkv_pre(rc, rs, kc, ks, vc, vs, gc, gs, wdc, wds, adc, ads,
              mu_r, mu_k, mu_v, mu_g, mu_wd, mu_ad, w_up, w0, a_up, a0, k_k, k_a):
    r = _lerp(rc, rs, mu_r)
    k = _lerp(kc, ks, mu_k)
    v = _lerp(vc, vs, mu_v)
    g = _lerp(gc, gs, mu_g)
    wd = _lerp(wdc, wds, mu_wd)
    ad = _lerp(adc, ads, mu_ad)
    t = wd.shape[0]
    bdims = (((2,), (1,)), ((0,), (0,)))
    tw = jnp.broadcast_to(jnp.tanh(wd).astype(BF16)[None], (H, t, RANK))
    z = w0 + lax.dot_general(tw, w_up.astype(BF16), bdims, preferred_element_type=F32)
    w_raw = -_softplus(-z) - 0.5
    lw = -jnp.exp(w_raw)
    adb = jnp.broadcast_to(ad.astype(BF16)[None], (H, t, RANK))
    alr = _sigmoid(a0 + lax.dot_general(adb, a_up.astype(BF16), bdims, preferred_element_type=F32))
    kk = k * k_k
    kk = kk / jnp.maximum(jnp.sqrt(jnp.sum(kk * kk, axis=-1, keepdims=True)), 1e-12)
    k2 = k * (1.0 + (alr - 1.0) * k_a)
    return r, lw, k2, v, -kk, kk * alr, g


def _wkv_chunk(s0, r, lw, k, v, a, b):
    c = r.shape[1]
    row = lax.broadcasted_iota(jnp.int32, (c, c), 0)
    col = lax.broadcasted_iota(jnp.int32, (c, c), 1)
    incl = (row >= col)[None]
    strict = (row > col)[None]
    bmm = functools.partial(jnp.einsum, precision=HI, preferred_element_type=F32)
    cl = bmm("hts,hsn->htn", jnp.broadcast_to((row >= col).astype(F32)[None], (H, c, c)), lw)
    cl_last = bmm("hts,hsn->htn", jnp.ones((H, 1, c), F32), lw)
    gi = jnp.exp(-cl)
    at = a * jnp.exp(cl - lw)
    rt = r * jnp.exp(cl)
    bt = b * gi
    kt = k * gi
    a_ab = jnp.where(strict, bmm("htj,hsj->hts", at, bt), 0.0)
    a_ak = jnp.where(strict, bmm("htj,hsj->hts", at, kt), 0.0)
    a_rb = jnp.where(incl, bmm("htj,hsj->hts", rt, bt), 0.0)
    a_rk = jnp.where(incl, bmm("htj,hsj->hts", rt, kt), 0.0)
    rhs = bmm("htj,hij->hti", at, s0) + bmm("hts,hsi->hti", a_ak, v)
    p = (row == col).astype(F32)[None] + a_ab
    x = a_ab
    for _ in range(int(math.log2(c)) - 1):
        x = bmm("hts,hsu->htu", x, x)
        p = p + bmm("hts,hsu->htu", p, x)
    sa = bmm("hts,hsi->hti", p, rhs)
    y = bmm("htj,hij->hti", rt, s0) + bmm("hts,hsi->hti", a_rb, sa) + bmm("hts,hsi->hti", a_rk, v)
    s1 = (s0 + bmm("hti,htj->hij", sa, bt) + bmm("hti,htj->hij", v, kt)) * jnp.exp(cl_last)
    return y, s1


def _rwkv_post(y, r, k2, v, g, lnx_w, lnx_b, r_k):
    mean = jnp.mean(y, axis=-1, keepdims=True)
    yc = y - mean
    var = jnp.mean(yc * yc, axis=-1, keepdims=True)
    yn = yc * lax.rsqrt(var + LNX_EPS) * lnx_w + lnx_b
    bonus = jnp.sum(r * k2 * r_k, axis=-1, keepdims=True) * v
    return (yn + bonus) * (g * _sigmoid(g))


def _fox_pre(q, k, f, q_g, k_g, f_b):
    qn = q * lax.rsqrt(jnp.mean(q * q, axis=-1, keepdims=True) + RMS_EPS) * q_g
    kn = k * lax.rsqrt(jnp.mean(k * k, axis=-1, keepdims=True) + RMS_EPS) * k_g
    x = f + f_b
    return qn, kn, jnp.minimum(x, 0.0) - jnp.log(1.0 + jnp.exp(-jnp.abs(x)))


def _rms_fwd(x, g):
    s = x.shape[0]

    def body(x_ref, g_ref, h_ref):
        xv = x_ref[...]
        h_ref[...] = (xv * lax.rsqrt(jnp.mean(xv * xv, axis=-1, keepdims=True) + RMS_EPS) * g_ref[...]).astype(BF16)

    return pl.pallas_call(
        body, name="rms_fwd", grid=(s // TOK_TILE,),
        in_specs=[pl.BlockSpec((TOK_TILE, D), lambda i: (i, 0)), pl.BlockSpec((1, D), lambda i: (0, 0))],
        out_specs=pl.BlockSpec((TOK_TILE, D), lambda i: (i, 0)),
        out_shape=jax.ShapeDtypeStruct((s, D), BF16), compiler_params=_params("arbitrary"))(x, g)


def _proj(h, w, name):
    s, n = h.shape[0], w.shape[1]

    def body(h_ref, w_ref, o_ref):
        o_ref[...] = jnp.dot(h_ref[...], w_ref[...], preferred_element_type=F32)

    return pl.pallas_call(
        body, name=name, grid=(s // TOK_TILE,),
        in_specs=[pl.BlockSpec((TOK_TILE, D), lambda i: (i, 0)), pl.BlockSpec((D, n), lambda i: (0, 0))],
        out_specs=pl.BlockSpec((TOK_TILE, n), lambda i: (i, 0)),
        out_shape=jax.ShapeDtypeStruct((s, n), F32), compiler_params=_params("arbitrary"))(h, w)


def _proj_wgrad(h, du, name):
    s, n = du.shape

    def body(h_ref, du_ref, o_ref):
        @pl.when(pl.program_id(0) == 0)
        def _():
            o_ref[...] = jnp.zeros_like(o_ref)

        o_ref[...] += _bdot_tn(h_ref[...], du_ref[...])

    return pl.pallas_call(
        body, name=name, grid=(s // TOK_TILE,),
        in_specs=[pl.BlockSpec((TOK_TILE, D), lambda i: (i, 0)), pl.BlockSpec((TOK_TILE, n), lambda i: (i, 0))],
        out_specs=pl.BlockSpec((D, n), lambda i: (0, 0)),
        out_shape=jax.ShapeDtypeStruct((D, n), F32), compiler_params=_params("arbitrary"))(h, du)


def _proj_xgrad(x, g, dx2, dus, ws):
    s = x.shape[0]
    tile = HEAD_TILE
    k = len(dus)

    def body(*refs):
        x_ref, g_ref, dx2_ref = refs[:3]
        du_refs, w_refs = refs[3:3 + k], refs[3 + k:3 + 2 * k]
        dx_ref, dg_ref = refs[3 + 2 * k:]

        @pl.when(pl.program_id(0) == 0)
        def _():
            dg_ref[...] = jnp.zeros_like(dg_ref)

        dh = _bdot_nt(du_refs[0][...], w_refs[0][...])
        for du_ref, w_ref in zip(du_refs[1:], w_refs[1:]):
            dh += _bdot_nt(du_ref[...], w_ref[...])
        xv = x_ref[...]
        rs = lax.rsqrt(jnp.mean(xv * xv, axis=-1, keepdims=True) + RMS_EPS)
        xn = xv * rs
        dg_ref[...] += jnp.sum(dh * xn, axis=0, keepdims=True)
        dxn = dh * g_ref[...]
        dx_ref[...] = rs * (dxn - xn * jnp.mean(dxn * xn, axis=-1, keepdims=True)) + dx2_ref[...]

    tok = lambda n: pl.BlockSpec((tile, n), lambda i: (i, 0))
    fixed = lambda a: pl.BlockSpec(a.shape, lambda i: (0,) * a.ndim)
    return pl.pallas_call(
        body, name="proj_xgrad", grid=(s // tile,),
        in_specs=[tok(D), fixed(g), tok(D)] + [tok(du.shape[1]) for du in dus] + [fixed(w) for w in ws],
        out_specs=[tok(D), pl.BlockSpec((1, D), lambda i: (0, 0))],
        out_shape=[jax.ShapeDtypeStruct((s, D), F32), jax.ShapeDtypeStruct((1, D), F32)],
        compiler_params=_params("arbitrary"))(x, g, dx2, *dus, *ws)


def _tail(x, target, ya, o, ub, ug, w_oa, w_ob, w_o, fg):
    s = x.shape[0]
    tile = HEAD_TILE

    def body(x_ref, t_ref, ya_ref, o_ref, gb_ref, ug_ref, woa_ref, wob_ref, wo_ref, fg_ref,
             loss_ref, dfg_ref, dwo_ref, dwoa_ref, dwob_ref, dx2_ref, dya_ref, do_ref, dgb_ref, dug_ref):
        @pl.when(pl.program_id(0) == 0)
        def _():
            for r in (loss_ref, dfg_ref, dwo_ref, dwoa_ref, dwob_ref):
                r[...] = jnp.zeros_like(r)

        ya_v = ya_ref[...]
        gate_b = gb_ref[...]
        sg_b = _sigmoid(gate_b)
        silu_b = gate_b * sg_b
        o_v = jnp.concatenate([o_ref[h] for h in range(H)], axis=-1)
        yb_v = o_v * silu_b
        big_a = _bdot(ya_v, woa_ref[...])
        big_b = _bdot(yb_v, wob_ref[...])
        sa = _sigmoid(ug_ref[:, :D])
        sb = _sigmoid(ug_ref[:, D:])
        merged = sa * big_a + sb * big_b
        x2 = x_ref[...] + _bdot(merged, wo_ref[...])
        rs = lax.rsqrt(jnp.mean(x2 * x2, axis=-1, keepdims=True) + RMS_EPS)
        xn = x2 * rs
        err = xn * fg_ref[...] - t_ref[...]
        loss_ref[...] += (0.5 / D) * jnp.sum(err * err)
        dout = err * (1.0 / D)
        dfg_ref[...] += jnp.sum(dout * xn, axis=0, keepdims=True)
        dxn = dout * fg_ref[...]
        dx2 = rs * (dxn - xn * jnp.mean(dxn * xn, axis=-1, keepdims=True))
        dx2_ref[...] = dx2
        dwo_ref[...] += _bdot_tn(merged, dx2)
        dmerged = _bdot_nt(dx2, wo_ref[...])
        dbig_a = dmerged * sa
        dbig_b = dmerged * sb
        dug_ref[:, :D] = dmerged * big_a * sa * (1.0 - sa)
        dug_ref[:, D:] = dmerged * big_b * sb * (1.0 - sb)
        dwoa_ref[...] += _bdot_tn(ya_v, dbig_a)
        dwob_ref[...] += _bdot_tn(yb_v, dbig_b)
        dya_ref[...] = _bdot_nt(dbig_a, woa_ref[...])
        dyb = _bdot_nt(dbig_b, wob_ref[...])
        dgb_ref[...] = dyb * o_v * (sg_b * (1.0 + gate_b * (1.0 - sg_b)))
        _dov = dyb * silu_b
        for h in range(H):
            do_ref[h] = _dov[:, N * h:N * (h + 1)]

    tok = lambda n: pl.BlockSpec((tile, n), lambda i: (i, 0))
    hm = pl.BlockSpec((H, tile, N), lambda i: (0, i, 0))
    fixed = lambda shape: pl.BlockSpec(shape, lambda i: (0,) * len(shape))
    f32 = lambda *shape: jax.ShapeDtypeStruct(shape, F32)
    return pl.pallas_call(
        body, name="tail", grid=(s // tile,),
        in_specs=[tok(D), tok(D), tok(DA), hm, pl.BlockSpec((tile, DA), lambda i: (i, 3)), tok(NG),
                  fixed((DA, D)), fixed((DA, D)), fixed((D, D)), fixed((1, D))],
        out_specs=[fixed((1, 1)), fixed((1, D)), fixed((D, D)), fixed((DA, D)), fixed((DA, D)),
                   tok(D), tok(DA), hm, tok(DA), tok(NG)],
        out_shape=[f32(1, 1), f32(1, D), f32(D, D), f32(DA, D), f32(DA, D),
                   f32(s, D), f32(s, DA), f32(H, s, N), f32(s, DA), f32(s, NG)],
        compiler_params=_params("arbitrary"))(x, target, ya, o, ub, ug, w_oa, w_ob, w_o, fg)


_PRE_PARAM_SHAPES = ((H, 1, N),) * 4 + ((1, RANK),) * 2 + ((H, RANK, N), (H, 1, N), (H, RANK, N), (H, 1, N), (H, 1, N),
                                                              (H, 1, N))


def _pre_operands(ua_ref, prev_ref, first):
    cur = ua_ref[...]
    t = cur.shape[0]
    prev_row = jnp.where(first, 0.0, prev_ref[7:8, :])
    rows = lax.broadcasted_iota(jnp.int32, cur.shape, 0)
    sh = jnp.where(rows == 0, prev_row, pltpu.roll(cur, 1, axis=0))
    ops = []
    for c0 in (0, DA, 2 * DA, 3 * DA + 2 * RANK):
        ops.append(jnp.stack([cur[:, c0 + N * h:c0 + N * (h + 1)] for h in range(H)]))
        ops.append(jnp.stack([sh[:, c0 + N * h:c0 + N * (h + 1)] for h in range(H)]))
    for c0 in (3 * DA, 3 * DA + RANK):
        ops.append(cur[:, c0:c0 + RANK])
        ops.append(sh[:, c0:c0 + RANK])
    del t
    return ops


def _ua_specs(tile, order):
    blocks = tile // 8
    return [pl.BlockSpec((tile, NA), lambda i: (order(i), 0)),
            pl.BlockSpec((8, NA), lambda i: (jnp.maximum(order(i) * blocks - 1, 0), 0))]


def _rwkv_pre_fwd(ua, pre_params):
    s = ua.shape[0]
    tile = HEAD_TILE

    def body(ua_ref, prev_ref, *refs):
        p_refs, o_refs = refs[:len(pre_params)], refs[len(pre_params):]
        ops = _pre_operands(ua_ref, prev_ref, pl.program_id(0) == 0)
        outs = _rwkv_pre(*ops, *[p[...] for p in p_refs])
        for o_ref, val in zip(o_refs, outs):
            o_ref[...] = val

    hm = pl.BlockSpec((H, tile, N), lambda i: (0, i, 0))
    return pl.pallas_call(
        body, name="rwkv_pre_fwd", grid=(s // tile,),
        in_specs=_ua_specs(tile, lambda i: i) + [pl.BlockSpec(p.shape, lambda i, nd=p.ndim: (0,) * nd) for p in pre_params],
        out_specs=[hm] * 7, out_shape=[jax.ShapeDtypeStruct((H, s, N), F32)] * 7,
        compiler_params=_params("arbitrary"))(ua, ua, *pre_params)


def _rwkv_pre_bwd(ua, pre_params, cots):
    s = ua.shape[0]
    tile = HEAD_TILE
    nt = s // tile
    n_p = len(pre_params)

    def body(ua_ref, prev_ref, *refs):
        p_refs, c_refs = refs[:n_p], refs[n_p:n_p + 10]
        dua_ref = refs[n_p + 10]
        dp_refs = refs[n_p + 11:n_p + 11 + n_p]
        carry_ref = refs[-1]
        i = pl.program_id(0)

        @pl.when(i == 0)
        def _():
            carry_ref[...] = jnp.zeros_like(carry_ref)
            for r in dp_refs:
                r[...] = jnp.zeros_like(r)

        ops = _pre_operands(ua_ref, prev_ref, i == nt - 1)
        _, vjp = jax.vjp(_rwkv_pre, *ops, *[p[...] for p in p_refs])
        c = [r[...] for r in c_refs]
        grads = vjp((c[0] + c[1], c[2], c[3] + c[4], c[5] + c[6], c[7], c[8], c[9]))
        d_ops, d_par = grads[:12], grads[12:]
        for r, val in zip(dp_refs, d_par):
            r[...] += val
        d_cur = jnp.concatenate([d_ops[0][h] for h in range(H)] + [d_ops[2][h] for h in range(H)]
                                + [d_ops[4][h] for h in range(H)] + [d_ops[8], d_ops[10]]
                                + [d_ops[6][h] for h in range(H)], axis=-1)
        d_sh = jnp.concatenate([d_ops[1][h] for h in range(H)] + [d_ops[3][h] for h in range(H)]
                               + [d_ops[5][h] for h in range(H)] + [d_ops[9], d_ops[11]]
                               + [d_ops[7][h] for h in range(H)], axis=-1)
        rows = lax.broadcasted_iota(jnp.int32, d_sh.shape, 0)
        dua_ref[...] = d_cur + jnp.where(rows == tile - 1, carry_ref[...], pltpu.roll(d_sh, tile - 1, axis=0))
        carry_ref[...] = d_sh[0:1, :]

    rev = lambda i: nt - 1 - i
    hm = pl.BlockSpec((H, tile, N), lambda i: (0, rev(i), 0))
    fixed = [pl.BlockSpec(p.shape, lambda i, nd=p.ndim: (0,) * nd) for p in pre_params]
    return pl.pallas_call(
        body, name="rwkv_pre_bwd", grid=(nt,),
        in_specs=_ua_specs(tile, rev) + fixed + [hm] * 10,
        out_specs=[pl.BlockSpec((tile, NA), lambda i: (rev(i), 0))] + fixed,
        out_shape=[jax.ShapeDtypeStruct((s, NA), F32)] + [jax.ShapeDtypeStruct(p.shape, F32) for p in pre_params],
        scratch_shapes=[pltpu.VMEM((1, NA), F32)],
        compiler_params=_params("arbitrary"))(ua, ua, *pre_params, *cots)


def _wkv_fwd(seq):
    s = seq[0].shape[1]
    nc = s // WKV_CHUNK

    def body(r_ref, lw_ref, k_ref, v_ref, a_ref, b_ref, y_ref, ck_ref, state):
        @pl.when(pl.program_id(0) == 0)
        def _():
            state[...] = jnp.zeros_like(state)

        s0 = state[...]
        ck_ref[0] = s0
        y, s1 = _wkv_chunk(s0, r_ref[...], lw_ref[...], k_ref[...], v_ref[...], a_ref[...], b_ref[...])
        y_ref[...] = y
        state[...] = s1

    hm = pl.BlockSpec((H, WKV_CHUNK, N), lambda c: (0, c, 0))
    return pl.pallas_call(
        body, name="wkv_fwd", grid=(nc,), in_specs=[hm] * 6,
        out_specs=[hm, pl.BlockSpec((1, H, N, N), lambda c: (c, 0, 0, 0))],
        out_shape=[jax.ShapeDtypeStruct((H, s, N), F32), jax.ShapeDtypeStruct((nc, H, N, N), F32)],
        scratch_shapes=[pltpu.VMEM((H, N, N), F32)], compiler_params=_params("arbitrary"))(*seq)


def _wkv_bwd(seq, ckpt, dy):
    s = seq[0].shape[1]
    nc = s // WKV_CHUNK

    def body(r_ref, lw_ref, k_ref, v_ref, a_ref, b_ref, ck_ref, dy_ref, *refs):
        d_refs, dstate = refs[:6], refs[6]

        @pl.when(pl.program_id(0) == 0)
        def _():
            dstate[...] = jnp.zeros_like(dstate)

        _, vjp = jax.vjp(_wkv_chunk, ck_ref[0], r_ref[...], lw_ref[...], k_ref[...], v_ref[...], a_ref[...],
                         b_ref[...])
        grads = vjp((dy_ref[...], dstate[...]))
        dstate[...] = grads[0]
        for d_ref, val in zip(d_refs, grads[1:]):
            d_ref[...] = val

    hm = pl.BlockSpec((H, WKV_CHUNK, N), lambda c: (0, nc - 1 - c, 0))
    return pl.pallas_call(
        body, name="wkv_bwd", grid=(nc,),
        in_specs=[hm] * 6 + [pl.BlockSpec((1, H, N, N), lambda c: (nc - 1 - c, 0, 0, 0)), hm],
        out_specs=[hm] * 6, out_shape=[jax.ShapeDtypeStruct((H, s, N), F32)] * 6,
        scratch_shapes=[pltpu.VMEM((H, N, N), F32)], compiler_params=_params("arbitrary"))(*seq, ckpt, dy)


def _rwkv_post_fwd(y, r, k2, v, g, post_params):
    s = y.shape[1]
    tile = HEAD_TILE

    def body(y_ref, r_ref, k_ref, v_ref, g_ref, w_ref, b_ref, rk_ref, o_ref):
        out = _rwkv_post(y_ref[...], r_ref[...], k_ref[...], v_ref[...], g_ref[...], w_ref[...], b_ref[...],
                         rk_ref[...])
        o_ref[...] = jnp.concatenate([out[h] for h in range(H)], axis=-1)

    hm = pl.BlockSpec((H, tile, N), lambda i: (0, i, 0))
    par = pl.BlockSpec((H, 1, N), lambda i: (0, 0, 0))
    return pl.pallas_call(
        body, name="rwkv_post_fwd", grid=(s // tile,), in_specs=[hm] * 5 + [par] * 3,
        out_specs=pl.BlockSpec((tile, DA), lambda i: (i, 0)), out_shape=jax.ShapeDtypeStruct((s, DA), F32),
        compiler_params=_params("arbitrary"))(y, r, k2, v, g, *post_params)


def _rwkv_post_bwd(y, r, k2, v, g, post_params, dya):
    s = y.shape[1]
    tile = HEAD_TILE

    def body(y_ref, r_ref, k_ref, v_ref, g_ref, w_ref, b_ref, rk_ref, dya_ref, *d_refs):
        @pl.when(pl.program_id(0) == 0)
        def _():
            for ref in d_refs[5:]:
                ref[...] = jnp.zeros_like(ref)

        _, vjp = jax.vjp(_rwkv_post, y_ref[...], r_ref[...], k_ref[...], v_ref[...], g_ref[...], w_ref[...],
                         b_ref[...], rk_ref[...])
        grads = vjp(jnp.stack([dya_ref[:, N * h:N * (h + 1)] for h in range(H)]))
        for ref, val in zip(d_refs[:5], grads[:5]):
            ref[...] = val
        for ref, val in zip(d_refs[5:], grads[5:]):
            ref[...] += val

    hm = pl.BlockSpec((H, tile, N), lambda i: (0, i, 0))
    par = pl.BlockSpec((H, 1, N), lambda i: (0, 0, 0))
    return pl.pallas_call(
        body, name="rwkv_post_bwd", grid=(s // tile,),
        in_specs=[hm] * 5 + [par] * 3 + [pl.BlockSpec((tile, DA), lambda i: (i, 0))],
        out_specs=[hm] * 5 + [par] * 3,
        out_shape=[jax.ShapeDtypeStruct((H, s, N), F32)] * 5 + [jax.ShapeDtypeStruct((H, 1, N), F32)] * 3,
        compiler_params=_params("arbitrary"))(y, r, k2, v, g, *post_params, dya)


def _tri(t):
    return (lax.broadcasted_iota(jnp.int32, (t, t), 0) >= lax.broadcasted_iota(jnp.int32, (t, t), 1)).astype(F32)


def _fox_pre_fwd(ub, uf, q_g, k_g, f_b):
    s = ub.shape[0]
    tile = HEAD_TILE

    def body(ub_ref, uf_ref, qg_ref, kg_ref, fb_ref, q_ref, k_ref, v_ref, cum_ref, carry):
        @pl.when(pl.program_id(0) == 0)
        def _():
            carry[...] = jnp.zeros_like(carry)

        qn, kn, logf = _fox_pre(_heads(ub_ref, 0), _heads(ub_ref, DA), uf_ref[...], qg_ref[...], kg_ref[...],
                                fb_ref[...])
        q_ref[...] = qn
        k_ref[...] = kn
        v_ref[...] = _heads(ub_ref, 2 * DA)
        cum = jnp.dot(_tri(tile), logf, precision=HI, preferred_element_type=F32) + carry[...]
        cum_ref[...] = cum
        carry[...] = cum[tile - 1:tile, :]

    hm = pl.BlockSpec((H, tile, N), lambda i: (0, i, 0))
    fixed = lambda shape: pl.BlockSpec(shape, lambda i: (0,) * len(shape))
    return pl.pallas_call(
        body, name="fox_pre_fwd", grid=(s // tile,),
        in_specs=[pl.BlockSpec((tile, NB), lambda i: (i, 0)), pl.BlockSpec((tile, NF), lambda i: (i, 0)),
                  fixed((1, 1, N)), fixed((1, 1, N)), fixed((1, NF))],
        out_specs=[hm] * 3 + [pl.BlockSpec((tile, NF), lambda i: (i, 0))],
        out_shape=[jax.ShapeDtypeStruct((H, s, N), F32)] * 3 + [jax.ShapeDtypeStruct((s, NF), F32)],
        scratch_shapes=[pltpu.VMEM((1, NF), F32)], compiler_params=_params("arbitrary"))(ub, uf, q_g, k_g, f_b)


def _fox_pre_bwd(ub, uf, q_g, k_g, f_b, dqn, dkn, dvf, dgate, dcum_q, dcum_k):
    s = ub.shape[0]
    tile = HEAD_TILE
    nt = s // tile

    def body(ub_ref, uf_ref, qg_ref, kg_ref, fb_ref, dq_ref, dk_ref, dv_ref, dgate_ref, dcq_ref, dck_ref,
             dub_ref, duf_ref, dqg_ref, dkg_ref, dfb_ref, carry):
        @pl.when(pl.program_id(0) == 0)
        def _():
            carry[...] = jnp.zeros_like(carry)
            for ref in (dqg_ref, dkg_ref, dfb_ref):
                ref[...] = jnp.zeros_like(ref)

        dcum = dcq_ref[...] + dck_ref[...]
        dlogf = lax.dot_general(_tri(tile), dcum, (((0,), (0,)), ((), ())), precision=HI,
                                preferred_element_type=F32) + carry[...]
        carry[...] = dlogf[0:1, :]
        _, vjp = jax.vjp(_fox_pre, _heads(ub_ref, 0), _heads(ub_ref, DA), uf_ref[...], qg_ref[...], kg_ref[...],
                         fb_ref[...])
        d_q, d_k, d_f, d_qg, d_kg, d_fb = vjp((dq_ref[...], dk_ref[...], dlogf))
        _store_heads(dub_ref, 0, d_q)
        _store_heads(dub_ref, DA, d_k)
        _store_heads(dub_ref, 2 * DA, dv_ref[...])
        dub_ref[:, 3 * DA:] = dgate_ref[...]
        duf_ref[...] = d_f
        dqg_ref[...] += d_qg
        dkg_ref[...] += d_kg
        dfb_ref[...] += d_fb

    rev = lambda i: nt - 1 - i
    hm = pl.BlockSpec((H, tile, N), lambda i: (0, rev(i), 0))
    tok = lambda n: pl.BlockSpec((tile, n), lambda i: (rev(i), 0))
    fixed = lambda shape: pl.BlockSpec(shape, lambda i: (0,) * len(shape))
    return pl.pallas_call(
        body, name="fox_pre_bwd", grid=(nt,),
        in_specs=[tok(NB), tok(NF), fixed((1, 1, N)), fixed((1, 1, N)), fixed((1, NF)), hm, hm, hm, tok(DA), tok(NF),
                  tok(NF)],
        out_specs=[tok(NB), tok(NF), fixed((1, 1, N)), fixed((1, 1, N)), fixed((1, NF))],
        out_shape=[jax.ShapeDtypeStruct((s, NB), F32), jax.ShapeDtypeStruct((s, NF), F32),
                   jax.ShapeDtypeStruct((1, 1, N), F32), jax.ShapeDtypeStruct((1, 1, N), F32),
                   jax.ShapeDtypeStruct((1, NF), F32)],
        scratch_shapes=[pltpu.VMEM((1, NF), F32)],
        compiler_params=_params("arbitrary"))(ub, uf, q_g, k_g, f_b, dqn, dkn, dvf, dgate, dcum_q, dcum_k)


def _att_logits(q, k, cq, ck, qi):
    tq, sk = q.shape[0], k.shape[0]
    logits = _bdot_nt(q, k) * ATT_SCALE + cq - ck
    rows = qi * tq + lax.broadcasted_iota(jnp.int32, (tq, sk), 0)
    cols = lax.broadcasted_iota(jnp.int32, (tq, sk), 1)
    return logits, rows >= cols


def _fox_attn_fwd(q, k, v, cum_q, cum_k):
    s = q.shape[1]
    tq = ATT_TILE

    def body(q_ref, k_ref, v_ref, cq_ref, ck_ref, o_ref, lse_ref):
        logits, mask = _att_logits(q_ref[0], k_ref[0], cq_ref[0], ck_ref[0], pl.program_id(1))
        logits = jnp.where(mask, logits, -1e30)
        m = jnp.max(logits, axis=-1, keepdims=True)
        p = jnp.exp(logits - m)
        l = jnp.sum(p, axis=-1, keepdims=True)
        o_ref[0] = _bdot(p, v_ref[0]) / l
        lse_ref[0] = m + jnp.log(l)

    qb = pl.BlockSpec((1, tq, N), lambda h, i: (h, i, 0))
    kb = pl.BlockSpec((1, s, N), lambda h, i: (h, 0, 0))
    return pl.pallas_call(
        body, name="fox_attn_fwd", grid=(H, s // tq),
        in_specs=[qb, kb, kb, pl.BlockSpec((1, tq, 1), lambda h, i: (h, i, 0)),
                  pl.BlockSpec((1, 1, s), lambda h, i: (h, 0, 0))],
        out_specs=[qb, pl.BlockSpec((1, tq, 1), lambda h, i: (h, i, 0))],
        out_shape=[jax.ShapeDtypeStruct((H, s, N), F32), jax.ShapeDtypeStruct((H, s, 1), F32)],
        compiler_params=_params("arbitrary", "arbitrary"))(q, k, v, cum_q, cum_k)


def _fox_attn_bwd(q, k, v, cum_q, cum_k, o, lse, do):
    s = q.shape[1]
    tq = ATT_TILE

    def body(q_ref, k_ref, v_ref, cq_ref, ck_ref, o_ref, lse_ref, do_ref, dq_ref, dk_ref, dv_ref, dcq_ref, dck_ref):
        @pl.when(pl.program_id(1) == 0)
        def _():
            for ref in (dk_ref, dv_ref, dck_ref):
                ref[...] = jnp.zeros_like(ref)

        qv, kv, vv, dov = q_ref[0], k_ref[0], v_ref[0], do_ref[0]
        logits, mask = _att_logits(qv, kv, cq_ref[0], ck_ref[0], pl.program_id(1))
        p = jnp.where(mask, jnp.exp(jnp.where(mask, logits, -1e30) - lse_ref[0]), 0.0)
        dp = _bdot_nt(dov, vv)
        delta = jnp.sum(dov * o_ref[0], axis=-1, keepdims=True)
        ds = p * (dp - delta)
        dq_ref[0] = _bdot(ds, kv) * ATT_SCALE
        dk_ref[0] += _bdot_tn(ds, qv) * ATT_SCALE
        dv_ref[0] += _bdot_tn(p, dov)
        dcq_ref[0] = jnp.sum(ds, axis=-1, keepdims=True)
        dck_ref[0] -= jnp.sum(ds, axis=0, keepdims=True)

    qb = pl.BlockSpec((1, tq, N), lambda h, i: (h, i, 0))
    kb = pl.BlockSpec((1, s, N), lambda h, i: (h, 0, 0))
    cqb = pl.BlockSpec((1, tq, 1), lambda h, i: (h, i, 0))
    ckb = pl.BlockSpec((1, 1, s), lambda h, i: (h, 0, 0))
    f32 = lambda *shape: jax.ShapeDtypeStruct(shape, F32)
    return pl.pallas_call(
        body, name="fox_attn_bwd", grid=(H, s // tq),
        in_specs=[qb, kb, kb, cqb, ckb, qb, cqb, qb], out_specs=[qb, kb, kb, cqb, ckb],
        out_shape=[f32(H, s, N), f32(H, s, N), f32(H, s, N), f32(H, s, 1), f32(H, 1, s)],
        compiler_params=_params("arbitrary", "arbitrary"))(q, k, v, cum_q, cum_k, o, lse, do)


def _head_param(p):
    return p.reshape(H, 1, N)


def _local_step(x, target, w, p):
    mu = p["shift_mu"]
    pre_params = (_head_param(mu[:, 0:DA]), _head_param(mu[:, DA:2 * DA]), _head_param(mu[:, 2 * DA:3 * DA]),
                  _head_param(mu[:, 3 * DA + 2 * RANK:]), mu[:, 3 * DA:3 * DA + RANK],
                  mu[:, 3 * DA + RANK:3 * DA + 2 * RANK],
                  w["w_lora_up"].astype(F32), _head_param(p["w0"]), w["a_lora_up"].astype(F32), _head_param(p["a0"]),
                  _head_param(p["k_k"]), _head_param(p["k_a"]))
    post_params = (_head_param(p["lnx_w"]), _head_param(p["lnx_b"]), _head_param(p["r_k"]))
    q_g, k_g = p["q_norm_g"].reshape(1, 1, N), p["k_norm_g"].reshape(1, 1, N)
    f_b = jnp.pad(p["f_bias"], ((0, 0), (0, NF - H)))
    fg = p["final_norm_g"].reshape(1, D)

    h = _rms_fwd(x, p["norm_g"])
    ua = _proj(h, w["in_a"], "proj_a")
    ub = _proj(h, w["in_b"], "proj_b")
    ug = _proj(h, w["in_g"], "proj_g")
    uf = _proj(h, w["in_f"], "proj_f")
    r, lw, k2, v, av, bv, gg = _rwkv_pre_fwd(ua, pre_params)
    y, ckpt = _wkv_fwd((r, lw, k2, v, av, bv))
    ya = _rwkv_post_fwd(y, r, k2, v, gg, post_params)
    qn, kn, vf, cum = _fox_pre_fwd(ub, uf, q_g, k_g, f_b)
    cum_t = cum[:, :H].T
    cum_q, cum_k = cum_t[:, :, None], cum_t[:, None, :]
    o, lse = _fox_attn_fwd(qn, kn, vf, cum_q, cum_k)

    (loss, dfg, dwo, dwoa, dwob, dx2, dya, do, dgate_b, dug) = _tail(
        x, target, ya, o, ub, ug, w["w_out_a"], w["w_out_b"], w["w_out"], fg)
    dqn, dkn, dvf, dcq, dck = _fox_attn_bwd(qn, kn, vf, cum_q, cum_k, o, lse, do)
    pad_f = lambda a: jnp.pad(a.T, ((0, 0), (0, NF - H)))
    dub, duf, dqg, dkg, dfb = _fox_pre_bwd(ub, uf, q_g, k_g, f_b, dqn, dkn, dvf, dgate_b,
                                           pad_f(dcq[:, :, 0]), pad_f(dck[:, 0, :]))
    dy, dr_p, dk_p, dv_p, dgg, dlnw, dlnb, drk = _rwkv_post_bwd(y, r, k2, v, gg, post_params, dya)
    dr_s, dlw, dk_s, dv_s, dav, dbv = _wkv_bwd((r, lw, k2, v, av, bv), ckpt, dy)
    pre_out = _rwkv_pre_bwd(ua, pre_params, (dr_s, dr_p, dlw, dk_s, dk_p, dv_s, dv_p, dav, dbv, dgg))
    dua, dpre = pre_out[0], pre_out[1:]
    dws = [_proj_wgrad(h, du, name) for du, name in
           ((dua, "wgrad_a"), (dub, "wgrad_b"), (dug, "wgrad_g"), (duf, "wgrad_f"))]
    dx, dng = _proj_xgrad(x, p["norm_g"], dx2, (dua, dub, dug, duf), (w["in_a"], w["in_b"], w["in_g"], w["in_f"]))

    flat = lambda a: a.reshape(1, -1)
    dmu = jnp.concatenate([flat(dpre[0]), flat(dpre[1]), flat(dpre[2]), dpre[4], dpre[5], flat(dpre[3])], axis=1)
    grads = {
        "w_in": jnp.concatenate([dws[0], dws[1], dws[3][:, :H], dws[2]], axis=1),
        "w_out_a": dwoa, "w_out_b": dwob, "w_out": dwo, "w_lora_up": dpre[6], "a_lora_up": dpre[8],
        "norm_g": dng, "final_norm_g": dfg, "shift_mu": dmu, "w0": flat(dpre[7]), "a0": flat(dpre[9]),
        "k_k": flat(dpre[10]), "k_a": flat(dpre[11]), "r_k": flat(drk), "lnx_w": flat(dlnw), "lnx_b": flat(dlnb),
        "q_norm_g": flat(dqg), "k_norm_g": flat(dkg), "f_bias": dfb[:, :H],
    }
    return loss, dx, grads


def _position():
    return lax.axis_index("x"), lax.axis_index("y"), lax.axis_index("c")


def _all_gather(block, name):
    rows, dtype = block.shape[0], block.dtype

    def body(x_ref, out_ref, send_sems, recv_sems, local_sem):
        x, y, c = _position()
        me, sibling = (x, y, c), (x, y, 1 - c)
        chips = [(1 - x, y), (x, 1 - y), (1 - x, 1 - y)]

        def slot(px, py, pc):
            return out_ref.at[4 * px + 2 * py + pc]

        def copy(k, blk, to, src=None):
            return pltpu.make_async_remote_copy(
                src_ref=slot(*blk) if src is None else src, dst_ref=slot(*blk), send_sem=send_sems.at[k],
                recv_sem=recv_sems.at[k], device_id=to, device_id_type=MESH)

        mine = pltpu.make_async_copy(x_ref, slot(*me), local_sem)
        mine.start()
        first = [copy(0, me, sibling, src=x_ref)]
        first += [copy(1 + j, me, (*chip, c), src=x_ref) for j, chip in enumerate(chips)]
        for cp in first:
            cp.start()
        passed = [copy(4 + j, (*chip, c), sibling) for j, chip in enumerate(chips)]
        for j, chip in enumerate(chips):
            copy(1 + j, (*chip, c), me).wait_recv()
            passed[j].start()
        copy(0, sibling, me).wait_recv()
        for j, chip in enumerate(chips):
            copy(4 + j, (*chip, 1 - c), me).wait_recv()
        for cp in first + passed:
            cp.wait_send()
        mine.wait()

    return pl.pallas_call(
        body, name=name, out_shape=jax.ShapeDtypeStruct((N_DEV, rows, LANES), dtype),
        in_specs=[pl.BlockSpec(memory_space=pl.ANY)], out_specs=pl.BlockSpec(memory_space=pl.ANY),
        scratch_shapes=[pltpu.SemaphoreType.DMA((7,)), pltpu.SemaphoreType.DMA((7,)), pltpu.SemaphoreType.DMA(())],
    )(block)


def _exchange(slabs):
    rows = slabs.shape[1]

    def body(s_ref, out_ref, send_sems, recv_sems, local_sem):
        x, y, c = _position()
        me = 4 * x + 2 * y + c
        mine = pltpu.make_async_copy(s_ref.at[me], out_ref.at[me], local_sem)
        mine.start()
        copies = []
        for m in range(1, N_DEV):
            px, py, pc = x ^ (m >> 2), y ^ ((m >> 1) & 1), c ^ (m & 1)
            peer = 4 * px + 2 * py + pc
            copies.append((
                pltpu.make_async_remote_copy(src_ref=s_ref.at[peer], dst_ref=out_ref.at[me], send_sem=send_sems.at[m - 1],
                                             recv_sem=recv_sems.at[m - 1], device_id=(px, py, pc), device_id_type=MESH),
                pltpu.make_async_remote_copy(src_ref=s_ref.at[me], dst_ref=out_ref.at[peer], send_sem=send_sems.at[m - 1],
                                             recv_sem=recv_sems.at[m - 1], device_id=(px, py, pc), device_id_type=MESH)))
        for send, _ in copies:
            send.start()
        for _, recv in copies:
            recv.wait_recv()
        for send, _ in copies:
            send.wait_send()
        mine.wait()

    return pl.pallas_call(
        body, name="grad_exchange", out_shape=jax.ShapeDtypeStruct((N_DEV, rows, LANES), slabs.dtype),
        in_specs=[pl.BlockSpec(memory_space=pl.ANY)], out_specs=pl.BlockSpec(memory_space=pl.ANY),
        scratch_shapes=[pltpu.SemaphoreType.DMA((7,)), pltpu.SemaphoreType.DMA((7,)), pltpu.SemaphoreType.DMA(())],
    )(slabs)


def _sum_adamw(recv, w, m, v):
    rows = w.shape[0]

    def body(r_ref, w_ref, m_ref, v_ref, g_ref, d_ref, mo_ref, vo_ref):
        g = r_ref[0]
        for k in range(1, N_DEV):
            g = g + r_ref[k]
        m_new = ADAM_B1 * m_ref[...] + (1.0 - ADAM_B1) * g
        v_new = ADAM_B2 * v_ref[...] + (1.0 - ADAM_B2) * (g * g)
        m_hat = m_new / (1.0 - ADAM_B1 ** ADAM_STEP)
        v_hat = v_new / (1.0 - ADAM_B2 ** ADAM_STEP)
        g_ref[...] = g
        d_ref[...] = -ADAM_LR * (m_hat / (jnp.sqrt(v_hat) + ADAM_EPS) + ADAM_WD * w_ref[...])
        mo_ref[...] = m_new
        vo_ref[...] = v_new

    blk = pl.BlockSpec((ADAM_TILE, LANES), lambda i: (i, 0))
    return pl.pallas_call(
        body, name="sum_adamw", grid=(rows // ADAM_TILE,),
        in_specs=[pl.BlockSpec((N_DEV, ADAM_TILE, LANES), lambda i: (0, i, 0)), blk, blk, blk],
        out_specs=[blk] * 4, out_shape=[jax.ShapeDtypeStruct((rows, LANES), F32)] * 4,
        compiler_params=_params("arbitrary"))(recv, w, m, v)


_BIG_NAMES = tuple(n for n, _ in BIG_ROWS)
_SMALL_NAMES = tuple(n for n, _ in SMALL)


def _pack_shard(t, extra=None):
    big = [t[n].reshape(-1, LANES) for n in _BIG_NAMES]
    small = jnp.concatenate([t[n].reshape(-1) for n in _SMALL_NAMES] + ([extra.reshape(-1)] if extra is not None else []))
    small = jnp.pad(small, (0, SMALL_ROWS * LANES - small.shape[0])).reshape(SMALL_ROWS, LANES)
    pad = jnp.zeros((PACK_ROWS - BIG_TOTAL - SMALL_ROWS, LANES), F32)
    return jnp.concatenate(big + [small, pad], axis=0)


def _unpack_shard(packed, like):
    out, row = {}, 0
    for n, r in BIG_ROWS:
        out[n] = packed[row:row + r].reshape(like[n].shape)
        row += r
    flat, off = packed[row:row + SMALL_ROWS].reshape(-1), 0
    for n, size in SMALL:
        out[n] = flat[off:off + size].reshape(like[n].shape)
        off += size
    return out, flat[off]


def _pack_grad_slabs(g, loss):
    by_cols = lambda a: jnp.moveaxis(a.reshape(a.shape[0], N_DEV, -1), 1, 0).reshape(N_DEV, -1, LANES)
    big = [by_cols(g["w_in"]), by_cols(g["w_out_a"]), by_cols(g["w_out_b"]), g["w_out"].reshape(N_DEV, -1, LANES),
           g["w_lora_up"].reshape(N_DEV, -1, LANES), g["a_lora_up"].reshape(N_DEV, -1, LANES)]
    small = jnp.concatenate([g[n].reshape(-1) for n in _SMALL_NAMES] + [loss.reshape(-1)])
    small = jnp.pad(small, (0, SMALL_ROWS * LANES - small.shape[0])).reshape(1, SMALL_ROWS, LANES)
    small = jnp.broadcast_to(small, (N_DEV, SMALL_ROWS, LANES))
    pad = jnp.zeros((N_DEV, PACK_ROWS - BIG_TOTAL - SMALL_ROWS, LANES), F32)
    return jnp.concatenate(big + [small, pad], axis=1)


def _gather_weights(t):
    packed = jnp.concatenate([t[n].reshape(-1, LANES) for n in _BIG_NAMES]
                             + [jnp.zeros((GATHER_ROWS - BIG_TOTAL, LANES), F32)], axis=0).astype(BF16)
    g = _all_gather(packed, "weight_gather")
    parts, row = {}, 0
    for n, r in BIG_ROWS:
        parts[n] = g[:, row:row + r]
        row += r
    by_cols = lambda a, rows_: jnp.moveaxis(a.reshape(N_DEV, rows_, -1), 0, 1).reshape(rows_, -1)
    w_in = by_cols(parts["w_in"], D)
    return {
        "in_a": w_in[:, :NA], "in_b": w_in[:, NA:NA + NB],
        "in_f": jnp.pad(w_in[:, NA + NB:NA + NB + H], ((0, 0), (0, NF - H))), "in_g": w_in[:, NA + NB + H:],
        "w_out_a": by_cols(parts["w_out_a"], DA), "w_out_b": by_cols(parts["w_out_b"], DA),
        "w_out": parts["w_out"].reshape(D, D),
        "w_lora_up": parts["w_lora_up"].reshape(H, RANK, N), "a_lora_up": parts["a_lora_up"].reshape(H, RANK, N),
    }


def kernel(x, norm_g, w_in, shift_mu, w_lora_up, w0, a_lora_up, a0, k_k, k_a, r_k, lnx_w, lnx_b, f_bias, q_norm_g, k_norm_g, w_out_a, w_out_b, w_out, final_norm_g, loss_target, m_norm_g, m_w_in, m_shift_mu, m_w_lora_up, m_w0, m_a_lora_up, m_a0, m_k_k, m_k_a, m_r_k, m_lnx_w, m_lnx_b, m_f_bias, m_q_norm_g, m_k_norm_g, m_w_out_a, m_w_out_b, m_w_out, m_final_norm_g, v_norm_g, v_w_in, v_shift_mu, v_w_lora_up, v_w0, v_a_lora_up, v_a0, v_k_k, v_k_a, v_r_k, v_lnx_w, v_lnx_b, v_f_bias, v_q_norm_g, v_k_norm_g, v_w_out_a, v_w_out_b, v_w_out, v_final_norm_g):
    names = ("norm_g", "w_in", "shift_mu", "w_lora_up", "w0", "a_lora_up", "a0", "k_k", "k_a", "r_k", "lnx_w", "lnx_b",
             "f_bias", "q_norm_g", "k_norm_g", "w_out_a", "w_out_b", "w_out", "final_norm_g")
    weights = dict(zip(names, (norm_g, w_in, shift_mu, w_lora_up, w0, a_lora_up, a0, k_k, k_a, r_k, lnx_w, lnx_b,
                               f_bias, q_norm_g, k_norm_g, w_out_a, w_out_b, w_out, final_norm_g)))
    m_in = dict(zip(names, (m_norm_g, m_w_in, m_shift_mu, m_w_lora_up, m_w0, m_a_lora_up, m_a0, m_k_k, m_k_a, m_r_k,
                            m_lnx_w, m_lnx_b, m_f_bias, m_q_norm_g, m_k_norm_g, m_w_out_a, m_w_out_b, m_w_out,
                            m_final_norm_g)))
    v_in = dict(zip(names, (v_norm_g, v_w_in, v_shift_mu, v_w_lora_up, v_w0, v_a_lora_up, v_a0, v_k_k, v_k_a, v_r_k,
                            v_lnx_w, v_lnx_b, v_f_bias, v_q_norm_g, v_k_norm_g, v_w_out_a, v_w_out_b, v_w_out,
                            v_final_norm_g)))

    full = _gather_weights(weights)
    small = {n: weights[n].reshape(1, -1) for n in _SMALL_NAMES}
    loss, dx, grads = _local_step(x[0], loss_target[0], full, small)

    recv = _exchange(_pack_grad_slabs(grads, loss))
    zero = jnp.zeros((1,), F32)
    g_p, d_p, m_p, v_p = _sum_adamw(recv, _pack_shard(weights, zero), _pack_shard(m_in, zero), _pack_shard(v_in, zero))
    g_out, loss_sum = _unpack_shard(g_p, weights)
    d_out, _ = _unpack_shard(d_p, weights)
    m_out, _ = _unpack_shard(m_p, weights)
    v_out, _ = _unpack_shard(v_p, weights)
    return (loss_sum, dx[None], *[g_out[n] for n in names], *[d_out[n] for n in names],
            *[m_out[n] for n in names], *[v_out[n] for n in names])
```

```python
import functools
import math

import jax
import jax.numpy as jnp
from jax import lax
from jax.experimental import pallas as pl
from jax.experimental.pallas import tpu as pltpu

F32 = jnp.float32
BF16 = jnp.bfloat16
HI = lax.Precision.HIGHEST
MESH = pl.DeviceIdType.MESH

N_DEV = 8
D = 1024
H = 8
N = 64
DA = H * N
RANK = 64
NA = 4 * DA + 2 * RANK
NB = 4 * DA
NG = 2 * D
NF = 128
IN_COLS = NA + NB + H + NG
COLS_PER_DEV = IN_COLS // N_DEV
RMS_EPS = 1e-6
LNX_EPS = 64e-5
ATT_SCALE = N ** -0.5

ADAM_LR = 0.001
ADAM_B1 = 0.9
ADAM_B2 = 0.999
ADAM_EPS = 1e-08
ADAM_WD = 0.01
ADAM_STEP = 10

LANES = 128
WKV_CHUNK = 64
TOK_TILE = 256
HEAD_TILE = 128
ATT_TILE = 256
VMEM_LIMIT = 56 * 1024 * 1024

REST_ROWS = (("w_out_a", DA), ("w_out_b", DA), ("w_out", D), ("w_lora_up", RANK * N // LANES),
             ("a_lora_up", RANK * N // LANES))
REST_TOTAL = sum(r for _, r in REST_ROWS)
SMALL = (("norm_g", D), ("final_norm_g", D), ("shift_mu", NA), ("w0", DA), ("a0", DA), ("k_k", DA), ("k_a", DA),
         ("r_k", DA), ("lnx_w", DA), ("lnx_b", DA), ("q_norm_g", N), ("k_norm_g", N), ("f_bias", H))
SMALL_ROWS = 64
W_IN_ROW_TILE = 128
REST_ROW_TILE = 352


def _params(*sem):
    return pltpu.CompilerParams(dimension_semantics=sem or None, vmem_limit_bytes=VMEM_LIMIT)


def _bdot(a, b):
    return jnp.dot(a.astype(BF16), b.astype(BF16), preferred_element_type=F32)


def _bdot_nt(a, b):
    return lax.dot_general(a.astype(BF16), b.astype(BF16), (((1,), (1,)), ((), ())), preferred_element_type=F32)


def _bdot_tn(a, b):
    return lax.dot_general(a.astype(BF16), b.astype(BF16), (((0,), (0,)), ((), ())), preferred_element_type=F32)


def _sigmoid(x):
    return 1.0 / (1.0 + jnp.exp(-x))


def _softplus(x):
    return jnp.maximum(x, 0.0) + jnp.log(1.0 + jnp.exp(-jnp.abs(x)))


def _heads(ref, col0):
    return jnp.stack([ref[:, col0 + N * h:col0 + N * (h + 1)] for h in range(H)])


def _store_heads(ref, col0, val):
    for h in range(H):
        ref[:, col0 + N * h:col0 + N * (h + 1)] = val[h]


def _lerp(c, s, mu):
    return c + (s - c) * mu


def _rwkv_pre(rc, rs, kc, ks, vc, vs, gc, gs, wdc, wds, adc, ads,
              mu_r, mu_k, mu_v, mu_g, mu_wd, mu_ad, w_up, w0, a_up, a0, k_k, k_a):
    r = _lerp(rc, rs, mu_r)
    k = _lerp(kc, ks, mu_k)
    v = _lerp(vc, vs, mu_v)
    g = _lerp(gc, gs, mu_g)
    wd = _lerp(wdc, wds, mu_wd)
    ad = _lerp(adc, ads, mu_ad)
    t = wd.shape[0]
    bdims = (((2,), (1,)), ((0,), (0,)))
    tw = jnp.broadcast_to(jnp.tanh(wd).astype(BF16)[None], (H, t, RANK))
    z = w0 + lax.dot_general(tw, w_up.astype(BF16), bdims, preferred_element_type=F32)
    w_raw = -_softplus(-z) - 0.5
    lw = -jnp.exp(w_raw)
    row = lax.broadcasted_iota(jnp.int32, (t, t), 0)
    col = lax.broadcasted_iota(jnp.int32, (t, t), 1)
    same_chunk = ((row >= col) & (row // WKV_CHUNK == col // WKV_CHUNK)).astype(F32)
    cl = jnp.einsum("hts,hsn->htn", jnp.broadcast_to(same_chunk[None], (H, t, t)), lw, precision=HI,
                    preferred_element_type=F32)
    adb = jnp.broadcast_to(ad.astype(BF16)[None], (H, t, RANK))
    alr = _sigmoid(a0 + lax.dot_general(adb, a_up.astype(BF16), bdims, preferred_element_type=F32))
    kk = k * k_k
    kk = kk / jnp.maximum(jnp.sqrt(jnp.sum(kk * kk, axis=-1, keepdims=True)), 1e-12)
    k2 = k * (1.0 + (alr - 1.0) * k_a)
    return r, lw, cl, k2, v, -kk, kk * alr, g


_MM_DIMS = {"nn": (((2,), (1,)), ((0,), (0,))), "nt": (((2,), (2,)), ((0,), (0,))), "tn": (((1,), (1,)), ((0,), (0,)))}


def _split(x):
    hi = x.astype(BF16)
    return hi, (x - hi.astype(F32)).astype(BF16)


def _dot3(a, b, kind):
    ah, al = _split(a)
    bh, bl = _split(b)
    dot = functools.partial(lax.dot_general, dimension_numbers=_MM_DIMS[kind], preferred_element_type=F32)
    return dot(ah, bh) + (dot(ah, bl) + dot(al, bh))


@functools.partial(jax.custom_vjp, nondiff_argnums=(2,))
def _mm(a, b, kind):
    return _dot3(a, b, kind)


def _mm_fwd(a, b, kind):
    return _dot3(a, b, kind), (a, b)


def _mm_bwd(kind, res, ct):
    a, b = res
    if kind == "nn":
        return _mm(ct, b, "nt"), _mm(a, ct, "tn")
    if kind == "nt":
        return _mm(ct, b, "nn"), _mm(ct, a, "tn")
    return _mm(b, ct, "nt"), _mm(a, ct, "nn")


_mm.defvjp(_mm_fwd, _mm_bwd)


def _chunk_masks(c):
    row = lax.broadcasted_iota(jnp.int32, (c, c), 0)
    col = lax.broadcasted_iota(jnp.int32, (c, c), 1)
    return (row >= col)[None], (row > col)[None], (row == col).astype(F32)[None]


def _wkv_aab(lw, cl, a, b):
    _, strict, _ = _chunk_masks(a.shape[1])
    return jnp.where(strict, _mm(a * jnp.exp(cl - lw), b * jnp.exp(-cl), "nt"), 0.0)


def _tri_inverse(x):
    c = x.shape[1]
    p = _chunk_masks(c)[2] + x
    for _ in range(int(math.log2(c)) - 1):
        x = _mm(x, x, "nn")
        p = p + _mm(p, x, "nn")
    return p


def _wkv_apply(s0, r, lw, cl, k, v, a, b, p):
    c = r.shape[1]
    incl, strict, _ = _chunk_masks(c)
    gi = jnp.exp(-cl)
    at = a * jnp.exp(cl - lw)
    rt = r * jnp.exp(cl)
    bt = b * gi
    kt = k * gi
    a_ak = jnp.where(strict, _mm(at, kt, "nt"), 0.0)
    a_rb = jnp.where(incl, _mm(rt, bt, "nt"), 0.0)
    a_rk = jnp.where(incl, _mm(rt, kt, "nt"), 0.0)
    sa = _mm(p, _mm(at, s0, "nt") + _mm(a_ak, v, "nn"), "nn")
    y = _mm(rt, s0, "nt") + _mm(a_rb, sa, "nn") + _mm(a_rk, v, "nn")
    s1 = (s0 + _mm(sa, bt, "tn") + _mm(v, kt, "tn")) * jnp.exp(cl[:, c - 1:c, :])
    return y, s1


def _rwkv_post(y, r, k2, v, g, lnx_w, lnx_b, r_k):
    mean = jnp.mean(y, axis=-1, keepdims=True)
    yc = y - mean
    var = jnp.mean(yc * yc, axis=-1, keepdims=True)
    yn = yc * lax.rsqrt(var + LNX_EPS) * lnx_w + lnx_b
    bonus = jnp.sum(r * k2 * r_k, axis=-1, keepdims=True) * v
    return (yn + bonus) * (g * _sigmoid(g))


def _fox_pre(q, k, f, q_g, k_g, f_b):
    qn = q * lax.rsqrt(jnp.mean(q * q, axis=-1, keepdims=True) + RMS_EPS) * q_g
    kn = k * lax.rsqrt(jnp.mean(k * k, axis=-1, keepdims=True) + RMS_EPS) * k_g
    x = f + f_b
    return qn, kn, jnp.minimum(x, 0.0) - jnp.log(1.0 + jnp.exp(-jnp.abs(x)))


def _rms_fwd(x, g):
    s = x.shape[0]

    def body(x_ref, g_ref, h_ref):
        xv = x_ref[...]
        h_ref[...] = (xv * lax.rsqrt(jnp.mean(xv * xv, axis=-1, keepdims=True) + RMS_EPS) * g_ref[...]).astype(BF16)

    return pl.pallas_call(
        body, name="rms_fwd", grid=(s // TOK_TILE,),
        in_specs=[pl.BlockSpec((TOK_TILE, D), lambda i: (i, 0)), pl.BlockSpec((1, D), lambda i: (0, 0))],
        out_specs=pl.BlockSpec((TOK_TILE, D), lambda i: (i, 0)),
        out_shape=jax.ShapeDtypeStruct((s, D), BF16), compiler_params=_params("arbitrary"))(x, g)


def _proj(h, w, name):
    s, n = h.shape[0], w.shape[1]

    def body(h_ref, w_ref, o_ref):
        o_ref[...] = jnp.dot(h_ref[...], w_ref[...], preferred_element_type=F32)

    return pl.pallas_call(
        body, name=name, grid=(s // TOK_TILE,),
        in_specs=[pl.BlockSpec((TOK_TILE, D), lambda i: (i, 0)), pl.BlockSpec((D, n), lambda i: (0, 0))],
        out_specs=pl.BlockSpec((TOK_TILE, n), lambda i: (i, 0)),
        out_shape=jax.ShapeDtypeStruct((s, n), F32), compiler_params=_params("arbitrary"))(h, w)


def _proj_wgrad(h, du, name):
    s, n = du.shape

    def body(h_ref, du_ref, o_ref):
        @pl.when(pl.program_id(0) == 0)
        def _():
            o_ref[...] = jnp.zeros_like(o_ref)

        o_ref[...] += _bdot_tn(h_ref[...], du_ref[...])

    return pl.pallas_call(
        body, name=name, grid=(s // TOK_TILE,),
        in_specs=[pl.BlockSpec((TOK_TILE, D), lambda i: (i, 0)), pl.BlockSpec((TOK_TILE, n), lambda i: (i, 0))],
        out_specs=pl.BlockSpec((D, n), lambda i: (0, 0)),
        out_shape=jax.ShapeDtypeStruct((D, n), F32), compiler_params=_params("arbitrary"))(h, du)


def _proj_xgrad(x, g, dx2, dus, ws):
    s = x.shape[0]
    tile = HEAD_TILE
    k = len(dus)

    def body(*refs):
        x_ref, g_ref, dx2_ref = refs[:3]
        du_refs, w_refs = refs[3:3 + k], refs[3 + k:3 + 2 * k]
        dx_ref, dg_ref = refs[3 + 2 * k:]

        @pl.when(pl.program_id(0) == 0)
        def _():
            dg_ref[...] = jnp.zeros_like(dg_ref)

        dh = _bdot_nt(du_refs[0][...], w_refs[0][...])
        for du_ref, w_ref in zip(du_refs[1:], w_refs[1:]):
            dh += _bdot_nt(du_ref[...], w_ref[...])
        xv = x_ref[...]
        rs = lax.rsqrt(jnp.mean(xv * xv, axis=-1, keepdims=True) + RMS_EPS)
        xn = xv * rs
        dg_ref[...] += jnp.sum(dh * xn, axis=0, keepdims=True)
        dxn = dh * g_ref[...]
        dx_ref[...] = rs * (dxn - xn * jnp.mean(dxn * xn, axis=-1, keepdims=True)) + dx2_ref[...]

    tok = lambda n: pl.BlockSpec((tile, n), lambda i: (i, 0))
    fixed = lambda a: pl.BlockSpec(a.shape, lambda i: (0,) * a.ndim)
    return pl.pallas_call(
        body, name="proj_xgrad", grid=(s // tile,),
        in_specs=[tok(D), fixed(g), tok(D)] + [tok(du.shape[1]) for du in dus] + [fixed(w) for w in ws],
        out_specs=[tok(D), pl.BlockSpec((1, D), lambda i: (0, 0))],
        out_shape=[jax.ShapeDtypeStruct((s, D), F32), jax.ShapeDtypeStruct((1, D), F32)],
        compiler_params=_params("arbitrary"))(x, g, dx2, *dus, *ws)


def _tail(x, target, ya, o, ub, ug, w_oa, w_ob, w_o, fg):
    s = x.shape[0]
    tile = HEAD_TILE

    def body(x_ref, t_ref, ya_ref, o_ref, gb_ref, ug_ref, woa_ref, wob_ref, wo_ref, fg_ref,
             loss_ref, dfg_ref, dwo_ref, dwoa_ref, dwob_ref, dx2_ref, dya_ref, do_ref, dgb_ref, dug_ref):
        @pl.when(pl.program_id(0) == 0)
        def _():
            for r in (loss_ref, dfg_ref, dwo_ref, dwoa_ref, dwob_ref):
                r[...] = jnp.zeros_like(r)

        ya_v = ya_ref[...]
        gate_b = gb_ref[...]
        sg_b = _sigmoid(gate_b)
        silu_b = gate_b * sg_b
        o_v = jnp.concatenate([o_ref[h] for h in range(H)], axis=-1)
        yb_v = o_v * silu_b
        big_a = _bdot(ya_v, woa_ref[...])
        big_b = _bdot(yb_v, wob_ref[...])
        sa = _sigmoid(ug_ref[:, :D])
        sb = _sigmoid(ug_ref[:, D:])
        merged = sa * big_a + sb * big_b
        x2 = x_ref[...] + _bdot(merged, wo_ref[...])
        rs = lax.rsqrt(jnp.mean(x2 * x2, axis=-1, keepdims=True) + RMS_EPS)
        xn = x2 * rs
        err = xn * fg_ref[...] - t_ref[...]
        loss_ref[...] += (0.5 / D) * jnp.sum(err * err)
        dout = err * (1.0 / D)
        dfg_ref[...] += jnp.sum(dout * xn, axis=0, keepdims=True)
        dxn = dout * fg_ref[...]
        dx2 = rs * (dxn - xn * jnp.mean(dxn * xn, axis=-1, keepdims=True))
        dx2_ref[...] = dx2
        dwo_ref[...] += _bdot_tn(merged, dx2)
        dmerged = _bdot_nt(dx2, wo_ref[...])
        dbig_a = dmerged * sa
        dbig_b = dmerged * sb
        dug_ref[:, :D] = dmerged * big_a * sa * (1.0 - sa)
        dug_ref[:, D:] = dmerged * big_b * sb * (1.0 - sb)
        dwoa_ref[...] += _bdot_tn(ya_v, dbig_a)
        dwob_ref[...] += _bdot_tn(yb_v, dbig_b)
        dya_ref[...] = _bdot_nt(dbig_a, woa_ref[...])
        dyb = _bdot_nt(dbig_b, wob_ref[...])
        dgb_ref[...] = dyb * o_v * (sg_b * (1.0 + gate_b * (1.0 - sg_b)))
        _dov = dyb * silu_b
        for h in range(H):
            do_ref[h] = _dov[:, N * h:N * (h + 1)]

    tok = lambda n: pl.BlockSpec((tile, n), lambda i: (i, 0))
    hm = pl.BlockSpec((H, tile, N), lambda i: (0, i, 0))
    fixed = lambda shape: pl.BlockSpec(shape, lambda i: (0,) * len(shape))
    f32 = lambda *shape: jax.ShapeDtypeStruct(shape, F32)
    return pl.pallas_call(
        body, name="tail", grid=(s // tile,),
        in_specs=[tok(D), tok(D), tok(DA), hm, pl.BlockSpec((tile, DA), lambda i: (i, 3)), tok(NG),
                  fixed((DA, D)), fixed((DA, D)), fixed((D, D)), fixed((1, D))],
        out_specs=[fixed((1, 1)), fixed((1, D)), fixed((D, D)), fixed((DA, D)), fixed((DA, D)),
                   tok(D), tok(DA), hm, tok(DA), tok(NG)],
        out_shape=[f32(1, 1), f32(1, D), f32(D, D), f32(DA, D), f32(DA, D),
                   f32(s, D), f32(s, DA), f32(H, s, N), f32(s, DA), f32(s, NG)],
        compiler_params=_params("arbitrary"))(x, target, ya, o, ub, ug, w_oa, w_ob, w_o, fg)


_PRE_PARAM_SHAPES = ((H, 1, N),) * 4 + ((1, RANK),) * 2 + ((H, RANK, N), (H, 1, N), (H, RANK, N), (H, 1, N), (H, 1, N),
                                                              (H, 1, N))


def _pre_operands(ua_ref, prev_ref, first):
    cur = ua_ref[...]
    t = cur.shape[0]
    prev_row = jnp.where(first, 0.0, prev_ref[7:8, :])
    rows = lax.broadcasted_iota(jnp.int32, cur.shape, 0)
    sh = jnp.where(rows == 0, prev_row, pltpu.roll(cur, 1, axis=0))
    ops = []
    for c0 in (0, DA, 2 * DA, 3 * DA + 2 * RANK):
        ops.append(jnp.stack([cur[:, c0 + N * h:c0 + N * (h + 1)] for h in range(H)]))
        ops.append(jnp.stack([sh[:, c0 + N * h:c0 + N * (h + 1)] for h in range(H)]))
    for c0 in (3 * DA, 3 * DA + RANK):
        ops.append(cur[:, c0:c0 + RANK])
        ops.append(sh[:, c0:c0 + RANK])
    del t
    return ops


def _ua_specs(tile, order):
    blocks = tile // 8
    return [pl.BlockSpec((tile, NA), lambda i: (order(i), 0)),
            pl.BlockSpec((8, NA), lambda i: (jnp.maximum(order(i) * blocks - 1, 0), 0))]


def _rwkv_pre_fwd(ua, pre_params):
    s = ua.shape[0]
    tile = HEAD_TILE

    def body(ua_ref, prev_ref, *refs):
        p_refs, o_refs = refs[:len(pre_params)], refs[len(pre_params):]
        ops = _pre_operands(ua_ref, prev_ref, pl.program_id(0) == 0)
        outs = _rwkv_pre(*ops, *[p[...] for p in p_refs])
        for o_ref, val in zip(o_refs, outs):
            o_ref[...] = val

    hm = pl.BlockSpec((H, tile, N), lambda i: (0, i, 0))
    return pl.pallas_call(
        body, name="rwkv_pre_fwd", grid=(s // tile,),
        in_specs=_ua_specs(tile, lambda i: i) + [pl.BlockSpec(p.shape, lambda i, nd=p.ndim: (0,) * nd) for p in pre_params],
        out_specs=[hm] * 8, out_shape=[jax.ShapeDtypeStruct((H, s, N), F32)] * 8,
        compiler_params=_params("arbitrary"))(ua, ua, *pre_params)


def _rwkv_pre_bwd(ua, pre_params, cots):
    s = ua.shape[0]
    tile = HEAD_TILE
    nt = s // tile
    n_p = len(pre_params)

    def body(ua_ref, prev_ref, *refs):
        p_refs, c_refs = refs[:n_p], refs[n_p:n_p + 11]
        dua_ref = refs[n_p + 11]
        dp_refs = refs[n_p + 12:n_p + 12 + n_p]
        carry_ref = refs[-1]
        i = pl.program_id(0)

        @pl.when(i == 0)
        def _():
            carry_ref[...] = jnp.zeros_like(carry_ref)
            for r in dp_refs:
                r[...] = jnp.zeros_like(r)

        ops = _pre_operands(ua_ref, prev_ref, i == nt - 1)
        _, vjp = jax.vjp(_rwkv_pre, *ops, *[p[...] for p in p_refs])
        c = [r[...] for r in c_refs]
        grads = vjp((c[0] + c[1], c[2], c[3], c[4] + c[5], c[6] + c[7], c[8], c[9], c[10]))
        d_ops, d_par = grads[:12], grads[12:]
        for r, val in zip(dp_refs, d_par):
            r[...] += val
        d_cur = jnp.concatenate([d_ops[0][h] for h in range(H)] + [d_ops[2][h] for h in range(H)]
                                + [d_ops[4][h] for h in range(H)] + [d_ops[8], d_ops[10]]
                                + [d_ops[6][h] for h in range(H)], axis=-1)
        d_sh = jnp.concatenate([d_ops[1][h] for h in range(H)] + [d_ops[3][h] for h in range(H)]
                               + [d_ops[5][h] for h in range(H)] + [d_ops[9], d_ops[11]]
                               + [d_ops[7][h] for h in range(H)], axis=-1)
        rows = lax.broadcasted_iota(jnp.int32, d_sh.shape, 0)
        dua_ref[...] = d_cur + jnp.where(rows == tile - 1, carry_ref[...], pltpu.roll(d_sh, tile - 1, axis=0))
        carry_ref[...] = d_sh[0:1, :]

    rev = lambda i: nt - 1 - i
    hm = pl.BlockSpec((H, tile, N), lambda i: (0, rev(i), 0))
    fixed = [pl.BlockSpec(p.shape, lambda i, nd=p.ndim: (0,) * nd) for p in pre_params]
    return pl.pallas_call(
        body, name="rwkv_pre_bwd", grid=(nt,),
        in_specs=_ua_specs(tile, rev) + fixed + [hm] * 11,
        out_specs=[pl.BlockSpec((tile, NA), lambda i: (rev(i), 0))] + fixed,
        out_shape=[jax.ShapeDtypeStruct((s, NA), F32)] + [jax.ShapeDtypeStruct(p.shape, F32) for p in pre_params],
        scratch_shapes=[pltpu.VMEM((1, NA), F32)],
        compiler_params=_params("arbitrary"))(ua, ua, *pre_params, *cots)


def _wkv_fwd(seq):
    s = seq[0].shape[1]
    nc = s // WKV_CHUNK

    def body(r_ref, lw_ref, cl_ref, k_ref, v_ref, a_ref, b_ref, y_ref, ck_ref, p_ref, state):
        @pl.when(pl.program_id(0) == 0)
        def _():
            state[...] = jnp.zeros_like(state)

        s0 = state[...]
        ck_ref[0] = s0
        p = _tri_inverse(_wkv_aab(lw_ref[...], cl_ref[...], a_ref[...], b_ref[...]))
        p_ref[0] = p
        y, s1 = _wkv_apply(s0, r_ref[...], lw_ref[...], cl_ref[...], k_ref[...], v_ref[...], a_ref[...], b_ref[...], p)
        y_ref[...] = y
        state[...] = s1

    hm = pl.BlockSpec((H, WKV_CHUNK, N), lambda c: (0, c, 0))
    per_chunk = lambda m: pl.BlockSpec((1, H, m, m), lambda c: (c, 0, 0, 0))
    return pl.pallas_call(
        body, name="wkv_fwd", grid=(nc,), in_specs=[hm] * 7,
        out_specs=[hm, per_chunk(N), per_chunk(WKV_CHUNK)],
        out_shape=[jax.ShapeDtypeStruct((H, s, N), F32), jax.ShapeDtypeStruct((nc, H, N, N), F32),
                   jax.ShapeDtypeStruct((nc, H, WKV_CHUNK, WKV_CHUNK), F32)],
        scratch_shapes=[pltpu.VMEM((H, N, N), F32)], compiler_params=_params("arbitrary"))(*seq)


def _wkv_bwd(seq, ckpt, pinv, dy):
    s = seq[0].shape[1]
    nc = s // WKV_CHUNK

    def body(r_ref, lw_ref, cl_ref, k_ref, v_ref, a_ref, b_ref, ck_ref, p_ref, dy_ref, *refs):
        d_refs, dstate = refs[:7], refs[7]

        @pl.when(pl.program_id(0) == 0)
        def _():
            dstate[...] = jnp.zeros_like(dstate)

        p = p_ref[0]
        lw, cl, a, b = lw_ref[...], cl_ref[...], a_ref[...], b_ref[...]
        _, vjp = jax.vjp(_wkv_apply, ck_ref[0], r_ref[...], lw, cl, k_ref[...], v_ref[...], a, b, p)
        ds0, dr, dlw, dcl, dk, dv, da, db, dp = vjp((dy_ref[...], dstate[...]))
        dstate[...] = ds0
        _, vjp_x = jax.vjp(_wkv_aab, lw, cl, a, b)
        dlw2, dcl2, da2, db2 = vjp_x(_mm(_mm(p, dp, "tn"), p, "nt"))
        for d_ref, val in zip(d_refs, (dr, dlw + dlw2, dcl + dcl2, dk, dv, da + da2, db + db2)):
            d_ref[...] = val

    hm = pl.BlockSpec((H, WKV_CHUNK, N), lambda c: (0, nc - 1 - c, 0))
    per_chunk = lambda m: pl.BlockSpec((1, H, m, m), lambda c: (nc - 1 - c, 0, 0, 0))
    return pl.pallas_call(
        body, name="wkv_bwd", grid=(nc,),
        in_specs=[hm] * 7 + [per_chunk(N), per_chunk(WKV_CHUNK), hm],
        out_specs=[hm] * 7, out_shape=[jax.ShapeDtypeStruct((H, s, N), F32)] * 7,
        scratch_shapes=[pltpu.VMEM((H, N, N), F32)], compiler_params=_params("arbitrary"))(*seq, ckpt, pinv, dy)


def _rwkv_post_fwd(y, r, k2, v, g, post_params):
    s = y.shape[1]
    tile = HEAD_TILE

    def body(y_ref, r_ref, k_ref, v_ref, g_ref, w_ref, b_ref, rk_ref, o_ref):
        out = _rwkv_post(y_ref[...], r_ref[...], k_ref[...], v_ref[...], g_ref[...], w_ref[...], b_ref[...],
                         rk_ref[...])
        o_ref[...] = jnp.concatenate([out[h] for h in range(H)], axis=-1)

    hm = pl.BlockSpec((H, tile, N), lambda i: (0, i, 0))
    par = pl.BlockSpec((H, 1, N), lambda i: (0, 0, 0))
    return pl.pallas_call(
        body, name="rwkv_post_fwd", grid=(s // tile,), in_specs=[hm] * 5 + [par] * 3,
        out_specs=pl.BlockSpec((tile, DA), lambda i: (i, 0)), out_shape=jax.ShapeDtypeStruct((s, DA), F32),
        compiler_params=_params("arbitrary"))(y, r, k2, v, g, *post_params)


def _rwkv_post_bwd(y, r, k2, v, g, post_params, dya):
    s = y.shape[1]
    tile = HEAD_TILE

    def body(y_ref, r_ref, k_ref, v_ref, g_ref, w_ref, b_ref, rk_ref, dya_ref, *d_refs):
        @pl.when(pl.program_id(0) == 0)
        def _():
            for ref in d_refs[5:]:
                ref[...] = jnp.zeros_like(ref)

        _, vjp = jax.vjp(_rwkv_post, y_ref[...], r_ref[...], k_ref[...], v_ref[...], g_ref[...], w_ref[...],
                         b_ref[...], rk_ref[...])
        grads = vjp(jnp.stack([dya_ref[:, N * h:N * (h + 1)] for h in range(H)]))
        for ref, val in zip(d_refs[:5], grads[:5]):
            ref[...] = val
        for ref, val in zip(d_refs[5:], grads[5:]):
            ref[...] += val

    hm = pl.BlockSpec((H, tile, N), lambda i: (0, i, 0))
    par = pl.BlockSpec((H, 1, N), lambda i: (0, 0, 0))
    return pl.pallas_call(
        body, name="rwkv_post_bwd", grid=(s // tile,),
        in_specs=[hm] * 5 + [par] * 3 + [pl.BlockSpec((tile, DA), lambda i: (i, 0))],
        out_specs=[hm] * 5 + [par] * 3,
        out_shape=[jax.ShapeDtypeStruct((H, s, N), F32)] * 5 + [jax.ShapeDtypeStruct((H, 1, N), F32)] * 3,
        compiler_params=_params("arbitrary"))(y, r, k2, v, g, *post_params, dya)


def _tri(t):
    return (lax.broadcasted_iota(jnp.int32, (t, t), 0) >= lax.broadcasted_iota(jnp.int32, (t, t), 1)).astype(F32)


def _fox_pre_fwd(ub, uf, q_g, k_g, f_b):
    s = ub.shape[0]
    tile = HEAD_TILE

    def body(ub_ref, uf_ref, qg_ref, kg_ref, fb_ref, q_ref, k_ref, v_ref, cum_ref, carry):
        @pl.when(pl.program_id(0) == 0)
        def _():
            carry[...] = jnp.zeros_like(carry)

        qn, kn, logf = _fox_pre(_heads(ub_ref, 0), _heads(ub_ref, DA), uf_ref[...], qg_ref[...], kg_ref[...],
                                fb_ref[...])
        q_ref[...] = qn
        k_ref[...] = kn
        v_ref[...] = _heads(ub_ref, 2 * DA)
        cum = jnp.dot(_tri(tile), logf, precision=HI, preferred_element_type=F32) + carry[...]
        cum_ref[...] = cum
        carry[...] = cum[tile - 1:tile, :]

    hm = pl.BlockSpec((H, tile, N), lambda i: (0, i, 0))
    fixed = lambda shape: pl.BlockSpec(shape, lambda i: (0,) * len(shape))
    return pl.pallas_call(
        body, name="fox_pre_fwd", grid=(s // tile,),
        in_specs=[pl.BlockSpec((tile, NB), lambda i: (i, 0)), pl.BlockSpec((tile, NF), lambda i: (i, 0)),
                  fixed((1, 1, N)), fixed((1, 1, N)), fixed((1, NF))],
        out_specs=[hm] * 3 + [pl.BlockSpec((tile, NF), lambda i: (i, 0))],
        out_shape=[jax.ShapeDtypeStruct((H, s, N), F32)] * 3 + [jax.ShapeDtypeStruct((s, NF), F32)],
        scratch_shapes=[pltpu.VMEM((1, NF), F32)], compiler_params=_params("arbitrary"))(ub, uf, q_g, k_g, f_b)


def _fox_pre_bwd(ub, uf, q_g, k_g, f_b, dqn, dkn, dvf, dgate, dcum_q, dcum_k):
    s = ub.shape[0]
    tile = HEAD_TILE
    nt = s // tile

    def body(ub_ref, uf_ref, qg_ref, kg_ref, fb_ref, dq_ref, dk_ref, dv_ref, dgate_ref, dcq_ref, dck_ref,
             dub_ref, duf_ref, dqg_ref, dkg_ref, dfb_ref, carry):
        @pl.when(pl.program_id(0) == 0)
        def _():
            carry[...] = jnp.zeros_like(carry)
            for ref in (dqg_ref, dkg_ref, dfb_ref):
                ref[...] = jnp.zeros_like(ref)

        dcum = dcq_ref[...] + dck_ref[...]
        dlogf = lax.dot_general(_tri(tile), dcum, (((0,), (0,)), ((), ())), precision=HI,
                                preferred_element_type=F32) + carry[...]
        carry[...] = dlogf[0:1, :]
        _, vjp = jax.vjp(_fox_pre, _heads(ub_ref, 0), _heads(ub_ref, DA), uf_ref[...], qg_ref[...], kg_ref[...],
                         fb_ref[...])
        d_q, d_k, d_f, d_qg, d_kg, d_fb = vjp((dq_ref[...], dk_ref[...], dlogf))
        _store_heads(dub_ref, 0, d_q)
        _store_heads(dub_ref, DA, d_k)
        _store_heads(dub_ref, 2 * DA, dv_ref[...])
        dub_ref[:, 3 * DA:] = dgate_ref[...]
        duf_ref[...] = d_f
        dqg_ref[...] += d_qg
        dkg_ref[...] += d_kg
        dfb_ref[...] += d_fb

    rev = lambda i: nt - 1 - i
    hm = pl.BlockSpec((H, tile, N), lambda i: (0, rev(i), 0))
    tok = lambda n: pl.BlockSpec((tile, n), lambda i: (rev(i), 0))
    fixed = lambda shape: pl.BlockSpec(shape, lambda i: (0,) * len(shape))
    return pl.pallas_call(
        body, name="fox_pre_bwd", grid=(nt,),
        in_specs=[tok(NB), tok(NF), fixed((1, 1, N)), fixed((1, 1, N)), fixed((1, NF)), hm, hm, hm, tok(DA), tok(NF),
                  tok(NF)],
        out_specs=[tok(NB), tok(NF), fixed((1, 1, N)), fixed((1, 1, N)), fixed((1, NF))],
        out_shape=[jax.ShapeDtypeStruct((s, NB), F32), jax.ShapeDtypeStruct((s, NF), F32),
                   jax.ShapeDtypeStruct((1, 1, N), F32), jax.ShapeDtypeStruct((1, 1, N), F32),
                   jax.ShapeDtypeStruct((1, NF), F32)],
        scratch_shapes=[pltpu.VMEM((1, NF), F32)],
        compiler_params=_params("arbitrary"))(ub, uf, q_g, k_g, f_b, dqn, dkn, dvf, dgate, dcum_q, dcum_k)


def _att_block(q_bf, k_ref, ck_ref, cq, qi, j):
    t = ATT_TILE
    kj = k_ref[0, pl.ds(pl.multiple_of(j * t, t), t), :]
    logits = _bdot_nt(q_bf, kj) * ATT_SCALE + cq - ck_ref[0, j]
    rows = qi * t + lax.broadcasted_iota(jnp.int32, (t, t), 0)
    cols = j * t + lax.broadcasted_iota(jnp.int32, (t, t), 1)
    mask = rows >= cols
    return jnp.where(mask, logits, -1e30), mask


def _fox_attn_fwd(q, k, v, cum_q, cum_k):
    s = q.shape[1]
    t = ATT_TILE

    def body(q_ref, k_ref, v_ref, cq_ref, ck_ref, o_ref, lse_ref):
        qi = pl.program_id(1)
        q_bf, cq = q_ref[0].astype(BF16), cq_ref[0]

        def step(j, carry):
            m, l, acc = carry
            logits, _ = _att_block(q_bf, k_ref, ck_ref, cq, qi, j)
            m_new = jnp.maximum(m, jnp.max(logits, axis=-1, keepdims=True))
            alpha = jnp.exp(m - m_new)
            p = jnp.exp(logits - m_new)
            vj = v_ref[0, pl.ds(pl.multiple_of(j * t, t), t), :]
            return m_new, alpha * l + jnp.sum(p, axis=-1, keepdims=True), alpha * acc + _bdot(p, vj)

        init = (jnp.full((t, 1), -1e30, F32), jnp.zeros((t, 1), F32), jnp.zeros((t, N), F32))
        m, l, acc = lax.fori_loop(0, qi + 1, step, init)
        o_ref[0] = acc / l
        lse_ref[0] = m + jnp.log(l)

    qb = pl.BlockSpec((1, t, N), lambda h, i: (h, i, 0))
    kb = pl.BlockSpec((1, s, N), lambda h, i: (h, 0, 0))
    return pl.pallas_call(
        body, name="fox_attn_fwd", grid=(H, s // t),
        in_specs=[qb, kb, kb, pl.BlockSpec((1, t, 1), lambda h, i: (h, i, 0)),
                  pl.BlockSpec((1, s // t, 1, t), lambda h, i: (h, 0, 0, 0))],
        out_specs=[qb, pl.BlockSpec((1, t, 1), lambda h, i: (h, i, 0))],
        out_shape=[jax.ShapeDtypeStruct((H, s, N), F32), jax.ShapeDtypeStruct((H, s, 1), F32)],
        compiler_params=_params("arbitrary", "arbitrary"))(q, k, v, cum_q, cum_k)


def _fox_attn_bwd(q, k, v, cum_q, cum_k, o, lse, do):
    s = q.shape[1]
    t = ATT_TILE

    def body(q_ref, k_ref, v_ref, cq_ref, ck_ref, o_ref, lse_ref, do_ref, dq_ref, dk_ref, dv_ref, dcq_ref, dck_ref):
        qi = pl.program_id(1)

        @pl.when(qi == 0)
        def _():
            for ref in (dk_ref, dv_ref, dck_ref):
                ref[...] = jnp.zeros_like(ref)

        qv, dov, cq, lse_v = q_ref[0], do_ref[0], cq_ref[0], lse_ref[0]
        q_bf, do_bf = qv.astype(BF16), dov.astype(BF16)
        delta = jnp.sum(dov * o_ref[0], axis=-1, keepdims=True)

        def step(j, carry):
            dq, dcq = carry
            rows = pl.ds(pl.multiple_of(j * t, t), t)
            logits, mask = _att_block(q_bf, k_ref, ck_ref, cq, qi, j)
            p = jnp.where(mask, jnp.exp(logits - lse_v), 0.0)
            ds = p * (_bdot_nt(do_bf, v_ref[0, rows, :]) - delta)
            dk_ref[0, rows, :] += _bdot_tn(ds, q_bf) * ATT_SCALE
            dv_ref[0, rows, :] += _bdot_tn(p, do_bf)
            dck_ref[0, j] -= jnp.sum(ds, axis=0, keepdims=True)
            return dq + _bdot(ds, k_ref[0, rows, :]), dcq + jnp.sum(ds, axis=-1, keepdims=True)

        dq, dcq = lax.fori_loop(0, qi + 1, step, (jnp.zeros((t, N), F32), jnp.zeros((t, 1), F32)))
        dq_ref[0] = dq * ATT_SCALE
        dcq_ref[0] = dcq

    qb = pl.BlockSpec((1, t, N), lambda h, i: (h, i, 0))
    kb = pl.BlockSpec((1, s, N), lambda h, i: (h, 0, 0))
    cqb = pl.BlockSpec((1, t, 1), lambda h, i: (h, i, 0))
    ckb = pl.BlockSpec((1, s // t, 1, t), lambda h, i: (h, 0, 0, 0))
    f32 = lambda *shape: jax.ShapeDtypeStruct(shape, F32)
    return pl.pallas_call(
        body, name="fox_attn_bwd", grid=(H, s // t),
        in_specs=[qb, kb, kb, cqb, ckb, qb, cqb, qb], out_specs=[qb, kb, kb, cqb, ckb],
        out_shape=[f32(H, s, N), f32(H, s, N), f32(H, s, N), f32(H, s, 1), f32(H, s // t, 1, t)],
        compiler_params=_params("arbitrary", "arbitrary"))(q, k, v, cum_q, cum_k, o, lse, do)


def _head_param(p):
    return p.reshape(H, 1, N)


def _local_step(x, target, w, p):
    mu = p["shift_mu"]
    pre_params = (_head_param(mu[:, 0:DA]), _head_param(mu[:, DA:2 * DA]), _head_param(mu[:, 2 * DA:3 * DA]),
                  _head_param(mu[:, 3 * DA + 2 * RANK:]), mu[:, 3 * DA:3 * DA + RANK],
                  mu[:, 3 * DA + RANK:3 * DA + 2 * RANK],
                  w["w_lora_up"].astype(F32), _head_param(p["w0"]), w["a_lora_up"].astype(F32), _head_param(p["a0"]),
                  _head_param(p["k_k"]), _head_param(p["k_a"]))
    post_params = (_head_param(p["lnx_w"]), _head_param(p["lnx_b"]), _head_param(p["r_k"]))
    q_g, k_g = p["q_norm_g"].reshape(1, 1, N), p["k_norm_g"].reshape(1, 1, N)
    f_b = jnp.pad(p["f_bias"], ((0, 0), (0, NF - H)))
    fg = p["final_norm_g"].reshape(1, D)

    h = _rms_fwd(x, p["norm_g"])
    ua = _proj(h, w["in_a"], "proj_a")
    ub = _proj(h, w["in_b"], "proj_b")
    ug = _proj(h, w["in_g"], "proj_g")
    uf = _proj(h, w["in_f"], "proj_f")
    r, lw, cl, k2, v, av, bv, gg = _rwkv_pre_fwd(ua, pre_params)
    y, ckpt, pinv = _wkv_fwd((r, lw, cl, k2, v, av, bv))
    ya = _rwkv_post_fwd(y, r, k2, v, gg, post_params)
    qn, kn, vf, cum = _fox_pre_fwd(ub, uf, q_g, k_g, f_b)
    cum_t = cum[:, :H].T
    cum_q, cum_k = cum_t[:, :, None], cum_t.reshape(H, -1, 1, ATT_TILE)
    o, lse = _fox_attn_fwd(qn, kn, vf, cum_q, cum_k)

    (loss, dfg, dwo, dwoa, dwob, dx2, dya, do, dgate_b, dug) = _tail(
        x, target, ya, o, ub, ug, w["w_out_a"], w["w_out_b"], w["w_out"], fg)
    dqn, dkn, dvf, dcq, dck = _fox_attn_bwd(qn, kn, vf, cum_q, cum_k, o, lse, do)
    pad_f = lambda a: jnp.pad(a.T, ((0, 0), (0, NF - H)))
    dub, duf, dqg, dkg, dfb = _fox_pre_bwd(ub, uf, q_g, k_g, f_b, dqn, dkn, dvf, dgate_b,
                                           pad_f(dcq[:, :, 0]), pad_f(dck.reshape(H, -1)))
    dy, dr_p, dk_p, dv_p, dgg, dlnw, dlnb, drk = _rwkv_post_bwd(y, r, k2, v, gg, post_params, dya)
    dr_s, dlw, dcl, dk_s, dv_s, dav, dbv = _wkv_bwd((r, lw, cl, k2, v, av, bv), ckpt, pinv, dy)
    pre_out = _rwkv_pre_bwd(ua, pre_params, (dr_s, dr_p, dlw, dcl, dk_s, dk_p, dv_s, dv_p, dav, dbv, dgg))
    dua, dpre = pre_out[0], pre_out[1:]
    dws = [_proj_wgrad(h, du, name) for du, name in
           ((dua, "wgrad_a"), (dub, "wgrad_b"), (dug, "wgrad_g"), (duf, "wgrad_f"))]
    dx, dng = _proj_xgrad(x, p["norm_g"], dx2, (dua, dub, dug, duf), (w["in_a"], w["in_b"], w["in_g"], w["in_f"]))

    flat = lambda a: a.reshape(1, -1)
    dmu = jnp.concatenate([flat(dpre[0]), flat(dpre[1]), flat(dpre[2]), dpre[4], dpre[5], flat(dpre[3])], axis=1)
    grads = {
        "w_in_segments": (dws[0], dws[1], dws[2], dws[3]),
        "w_out_a": dwoa, "w_out_b": dwob, "w_out": dwo, "w_lora_up": dpre[6], "a_lora_up": dpre[8],
        "norm_g": dng, "final_norm_g": dfg, "shift_mu": dmu, "w0": flat(dpre[7]), "a0": flat(dpre[9]),
        "k_k": flat(dpre[10]), "k_a": flat(dpre[11]), "r_k": flat(drk), "lnx_w": flat(dlnw), "lnx_b": flat(dlnb),
        "q_norm_g": flat(dqg), "k_norm_g": flat(dkg), "f_bias": dfb[:, :H],
    }
    return loss, dx, grads


def _position():
    return lax.axis_index("x"), lax.axis_index("y"), lax.axis_index("c")


def _hbm_specs(n):
    return [pl.BlockSpec(memory_space=pl.ANY)] * n


def _all_gather(blocks, name):
    n = len(blocks)

    def body(*refs):
        x_refs, out_refs = refs[:n], refs[n:2 * n]
        send_sems, recv_sems, local_sems = refs[2 * n:]
        x, y, c = _position()
        me, sibling = (x, y, c), (x, y, 1 - c)
        chips = [(1 - x, y), (x, 1 - y), (1 - x, 1 - y)]

        def copy(a, k, blk, to, own=False):
            dst = out_refs[a].at[4 * blk[0] + 2 * blk[1] + blk[2]]
            return pltpu.make_async_remote_copy(
                src_ref=x_refs[a] if own else dst, dst_ref=dst, send_sem=send_sems.at[7 * a + k],
                recv_sem=recv_sems.at[7 * a + k], device_id=to, device_id_type=MESH)

        mine = [pltpu.make_async_copy(x_refs[a], out_refs[a].at[4 * x + 2 * y + c], local_sems.at[a]) for a in range(n)]
        for cp in mine:
            cp.start()
        first = []
        for a in range(n):
            first.append(copy(a, 0, me, sibling, own=True))
            first += [copy(a, 1 + j, me, (*chip, c), own=True) for j, chip in enumerate(chips)]
        for cp in first:
            cp.start()
        passed = []
        for j, chip in enumerate(chips):
            for a in range(n):
                copy(a, 1 + j, (*chip, c), me).wait_recv()
                passed.append(copy(a, 4 + j, (*chip, c), sibling))
                passed[-1].start()
        for a in range(n):
            copy(a, 0, sibling, me).wait_recv()
        for j, chip in enumerate(chips):
            for a in range(n):
                copy(a, 4 + j, (*chip, 1 - c), me).wait_recv()
        for cp in first + passed:
            cp.wait_send()
        for cp in mine:
            cp.wait()

    return pl.pallas_call(
        body, name=name, out_shape=[jax.ShapeDtypeStruct((N_DEV,) + b.shape, b.dtype) for b in blocks],
        in_specs=_hbm_specs(n), out_specs=_hbm_specs(n),
        scratch_shapes=[pltpu.SemaphoreType.DMA((7 * n,)), pltpu.SemaphoreType.DMA((7 * n,)),
                        pltpu.SemaphoreType.DMA((n,))],
    )(*blocks)


def _exchange(slabs, name):
    n = len(slabs)

    def body(*refs):
        s_refs, out_refs = refs[:n], refs[n:2 * n]
        send_sems, recv_sems, local_sems = refs[2 * n:]
        x, y, c = _position()
        me = 4 * x + 2 * y + c
        mine = [pltpu.make_async_copy(s_refs[a].at[me], out_refs[a].at[me], local_sems.at[a]) for a in range(n)]
        for cp in mine:
            cp.start()
        sends, recvs = [], []
        for m in range(1, N_DEV):
            px, py, pc = x ^ (m >> 2), y ^ ((m >> 1) & 1), c ^ (m & 1)
            peer = 4 * px + 2 * py + pc
            for a in range(n):
                sem = dict(send_sem=send_sems.at[7 * a + m - 1], recv_sem=recv_sems.at[7 * a + m - 1],
                           device_id=(px, py, pc), device_id_type=MESH)
                sends.append(pltpu.make_async_remote_copy(src_ref=s_refs[a].at[peer], dst_ref=out_refs[a].at[me], **sem))
                recvs.append(pltpu.make_async_remote_copy(src_ref=s_refs[a].at[me], dst_ref=out_refs[a].at[peer], **sem))
        for cp in sends:
            cp.start()
        for cp in recvs:
            cp.wait_recv()
        for cp in sends:
            cp.wait_send()
        for cp in mine:
            cp.wait()

    return pl.pallas_call(
        body, name=name, out_shape=[jax.ShapeDtypeStruct(s.shape, s.dtype) for s in slabs],
        in_specs=_hbm_specs(n), out_specs=_hbm_specs(n),
        scratch_shapes=[pltpu.SemaphoreType.DMA((7 * n,)), pltpu.SemaphoreType.DMA((7 * n,)),
                        pltpu.SemaphoreType.DMA((n,))],
    )(*slabs)


def _sum_adamw(recv, w, m, v, tile, name):
    rows, cols = w.shape

    def body(r_ref, w_ref, m_ref, v_ref, g_ref, d_ref, mo_ref, vo_ref):
        g = r_ref[0].astype(F32)
        for k in range(1, N_DEV):
            g = g + r_ref[k].astype(F32)
        m_new = ADAM_B1 * m_ref[...] + (1.0 - ADAM_B1) * g
        v_new = ADAM_B2 * v_ref[...] + (1.0 - ADAM_B2) * (g * g)
        m_hat = m_new / (1.0 - ADAM_B1 ** ADAM_STEP)
        v_hat = v_new / (1.0 - ADAM_B2 ** ADAM_STEP)
        g_ref[...] = g
        d_ref[...] = -ADAM_LR * (m_hat / (jnp.sqrt(v_hat) + ADAM_EPS) + ADAM_WD * w_ref[...])
        mo_ref[...] = m_new
        vo_ref[...] = v_new

    blk = pl.BlockSpec((tile, cols), lambda i: (i, 0))
    return pl.pallas_call(
        body, name=name, grid=(rows // tile,),
        in_specs=[pl.BlockSpec((N_DEV, tile, cols), lambda i: (0, i, 0)), blk, blk, blk],
        out_specs=[blk] * 4, out_shape=[jax.ShapeDtypeStruct((rows, cols), F32)] * 4,
        compiler_params=_params("arbitrary"))(recv, w, m, v)


_REST_NAMES = tuple(n for n, _ in REST_ROWS)
_SMALL_NAMES = tuple(n for n, _ in SMALL)


def _split_w_in(gathered):
    tile = W_IN_ROW_TILE

    def body(g_ref, a_ref, b_ref, gate_ref, f_ref):
        full = jnp.concatenate([g_ref[j] for j in range(N_DEV)], axis=1)
        a_ref[...] = full[:, :NA]
        b_ref[...] = full[:, NA:NA + NB]
        f_ref[...] = jnp.concatenate([full[:, NA + NB:NA + NB + H], jnp.zeros((tile, NF - H), BF16)], axis=1)
        gate_ref[...] = full[:, NA + NB + H:]

    seg = lambda n: pl.BlockSpec((tile, n), lambda i: (i, 0))
    return pl.pallas_call(
        body, name="split_w_in", grid=(D // tile,),
        in_specs=[pl.BlockSpec((N_DEV, tile, COLS_PER_DEV), lambda i: (0, i, 0))],
        out_specs=[seg(NA), seg(NB), seg(NG), seg(NF)],
        out_shape=[jax.ShapeDtypeStruct((D, n), BF16) for n in (NA, NB, NG, NF)],
        compiler_params=_params("arbitrary"))(gathered)


def _slab_w_in_grad(da, db, dg, df):
    tile = W_IN_ROW_TILE

    def body(a_ref, b_ref, gate_ref, f_ref, o_ref):
        full = jnp.concatenate([a_ref[...].astype(BF16), b_ref[...].astype(BF16), f_ref[:, :H].astype(BF16),
                                gate_ref[...].astype(BF16)], axis=1)
        for j in range(N_DEV):
            o_ref[j] = full[:, COLS_PER_DEV * j:COLS_PER_DEV * (j + 1)]

    seg = lambda n: pl.BlockSpec((tile, n), lambda i: (i, 0))
    return pl.pallas_call(
        body, name="slab_w_in_grad", grid=(D // tile,), in_specs=[seg(NA), seg(NB), seg(NG), seg(NF)],
        out_specs=pl.BlockSpec((N_DEV, tile, COLS_PER_DEV), lambda i: (0, i, 0)),
        out_shape=jax.ShapeDtypeStruct((N_DEV, D, COLS_PER_DEV), BF16),
        compiler_params=_params("arbitrary"))(da, db, dg, df)


def _pack_rest(t):
    return jnp.concatenate([t[n].reshape(-1, LANES) for n in _REST_NAMES], axis=0)


def _unpack_rest(packed, like):
    out, row = {}, 0
    for n, r in REST_ROWS:
        out[n] = packed[row:row + r].reshape(like[n].shape)
        row += r
    return out


def _pack_small(t, last):
    flat = jnp.concatenate([t[n].reshape(-1) for n in _SMALL_NAMES] + [last.reshape(-1)])
    return jnp.pad(flat, (0, SMALL_ROWS * LANES - flat.shape[0])).reshape(SMALL_ROWS, LANES)


def _unpack_small(packed, like):
    out, flat, off = {}, packed.reshape(-1), 0
    for n, size in SMALL:
        out[n] = flat[off:off + size].reshape(like[n].shape)
        off += size
    return out, flat[off]


def _gather_weights(t):
    w_in_all, rest_all = _all_gather([t["w_in"][0].astype(BF16), _pack_rest(t).astype(BF16)], "weight_gather")
    in_a, in_b, in_g, in_f = _split_w_in(w_in_all)
    parts, row = {}, 0
    for n, r in REST_ROWS:
        parts[n] = rest_all[:, row:row + r]
        row += r
    by_cols = lambda a: jnp.moveaxis(a, 0, 1).reshape(a.shape[1], -1)
    return {
        "in_a": in_a, "in_b": in_b, "in_g": in_g, "in_f": in_f,
        "w_out_a": by_cols(parts["w_out_a"]), "w_out_b": by_cols(parts["w_out_b"]),
        "w_out": parts["w_out"].reshape(D, D),
        "w_lora_up": parts["w_lora_up"].reshape(H, RANK, N), "a_lora_up": parts["a_lora_up"].reshape(H, RANK, N),
    }


def _grad_slabs(g, loss):
    by_cols = lambda a: jnp.moveaxis(a.reshape(a.shape[0], N_DEV, -1), 1, 0)
    rest = jnp.concatenate([by_cols(g["w_out_a"]), by_cols(g["w_out_b"]), g["w_out"].reshape(N_DEV, -1, LANES),
                            g["w_lora_up"].reshape(N_DEV, -1, LANES), g["a_lora_up"].reshape(N_DEV, -1, LANES)],
                           axis=1).astype(BF16)
    small = jnp.broadcast_to(_pack_small(g, loss)[None], (N_DEV, SMALL_ROWS, LANES))
    return [_slab_w_in_grad(*g["w_in_segments"]), rest, small]


def kernel(x, norm_g, w_in, shift_mu, w_lora_up, w0, a_lora_up, a0, k_k, k_a, r_k, lnx_w, lnx_b, f_bias, q_norm_g, k_norm_g, w_out_a, w_out_b, w_out, final_norm_g, loss_target, m_norm_g, m_w_in, m_shift_mu, m_w_lora_up, m_w0, m_a_lora_up, m_a0, m_k_k, m_k_a, m_r_k, m_lnx_w, m_lnx_b, m_f_bias, m_q_norm_g, m_k_norm_g, m_w_out_a, m_w_out_b, m_w_out, m_final_norm_g, v_norm_g, v_w_in, v_shift_mu, v_w_lora_up, v_w0, v_a_lora_up, v_a0, v_k_k, v_k_a, v_r_k, v_lnx_w, v_lnx_b, v_f_bias, v_q_norm_g, v_k_norm_g, v_w_out_a, v_w_out_b, v_w_out, v_final_norm_g):
    names = ("norm_g", "w_in", "shift_mu", "w_lora_up", "w0", "a_lora_up", "a0", "k_k", "k_a", "r_k", "lnx_w", "lnx_b",
             "f_bias", "q_norm_g", "k_norm_g", "w_out_a", "w_out_b", "w_out", "final_norm_g")
    weights = dict(zip(names, (norm_g, w_in, shift_mu, w_lora_up, w0, a_lora_up, a0, k_k, k_a, r_k, lnx_w, lnx_b,
                               f_bias, q_norm_g, k_norm_g, w_out_a, w_out_b, w_out, final_norm_g)))
    m_in = dict(zip(names, (m_norm_g, m_w_in, m_shift_mu, m_w_lora_up, m_w0, m_a_lora_up, m_a0, m_k_k, m_k_a, m_r_k,
                            m_lnx_w, m_lnx_b, m_f_bias, m_q_norm_g, m_k_norm_g, m_w_out_a, m_w_out_b, m_w_out,
                            m_final_norm_g)))
    v_in = dict(zip(names, (v_norm_g, v_w_in, v_shift_mu, v_w_lora_up, v_w0, v_a_lora_up, v_a0, v_k_k, v_k_a, v_r_k,
                            v_lnx_w, v_lnx_b, v_f_bias, v_q_norm_g, v_k_norm_g, v_w_out_a, v_w_out_b, v_w_out,
                            v_final_norm_g)))

    full = _gather_weights(weights)
    small = {n: weights[n].reshape(1, -1) for n in _SMALL_NAMES}
    loss, dx, grads = _local_step(x[0], loss_target[0], full, small)

    recv_w_in, recv_rest, recv_small = _exchange(_grad_slabs(grads, loss), "grad_exchange")
    zero = jnp.zeros((1,), F32)
    outs = [{}, {}, {}, {}]
    res = _sum_adamw(recv_w_in, w_in[0], m_w_in[0], v_w_in[0], W_IN_ROW_TILE, "adamw_w_in")
    for o, r in zip(outs, res):
        o["w_in"] = r[None]
    res = _sum_adamw(recv_rest, _pack_rest(weights), _pack_rest(m_in), _pack_rest(v_in), REST_ROW_TILE, "adamw_rest")
    for o, r in zip(outs, res):
        o.update(_unpack_rest(r, weights))
    res = _sum_adamw(recv_small, _pack_small(weights, zero), _pack_small(m_in, zero), _pack_small(v_in, zero),
                     SMALL_ROWS, "adamw_small")
    for o, r in zip(outs, res):
        o.update(_unpack_small(r, weights)[0])
    loss_sum = _unpack_small(res[0], weights)[1]
    return (loss_sum, dx[None], *[o[n] for o in outs for n in names])
```

```python
import functools
import math

import jax
import jax.numpy as jnp
from jax import lax
from jax.experimental import pallas as pl
from jax.experimental.pallas import tpu as pltpu

F32 = jnp.float32
BF16 = jnp.bfloat16
HI = lax.Precision.HIGHEST
MESH = pl.DeviceIdType.MESH

N_DEV = 8
D = 1024
H = 8
N = 64
DA = H * N
RANK = 64
NA = 4 * DA + 2 * RANK
NB = 4 * DA
NG = 2 * D
NF = 128
IN_COLS = NA + NB + H + NG
COLS_PER_DEV = IN_COLS // N_DEV
RMS_EPS = 1e-6
LNX_EPS = 64e-5
ATT_SCALE = N ** -0.5

ADAM_LR = 0.001
ADAM_B1 = 0.9
ADAM_B2 = 0.999
ADAM_EPS = 1e-08
ADAM_WD = 0.01
ADAM_STEP = 10

LANES = 128
WKV_CHUNK = 64
TOK_TILE = 256
HEAD_TILE = 128
ATT_TILE = 256
ATT_GROUPS = 4
VMEM_LIMIT = 56 * 1024 * 1024

REST_ROWS = (("w_out_a", DA), ("w_out_b", DA), ("w_out", D), ("w_lora_up", RANK * N // LANES),
             ("a_lora_up", RANK * N // LANES))
REST_TOTAL = sum(r for _, r in REST_ROWS)
SMALL = (("norm_g", D), ("final_norm_g", D), ("shift_mu", NA), ("w0", DA), ("a0", DA), ("k_k", DA), ("k_a", DA),
         ("r_k", DA), ("lnx_w", DA), ("lnx_b", DA), ("q_norm_g", N), ("k_norm_g", N), ("f_bias", H))
SMALL_ROWS = 64
W_IN_COL_TILE = 256
REST_ROW_TILE = 352


def _params(*sem):
    return pltpu.CompilerParams(dimension_semantics=sem or None, vmem_limit_bytes=VMEM_LIMIT)


def _bdot(a, b):
    return jnp.dot(a.astype(BF16), b.astype(BF16), preferred_element_type=F32)


def _bdot_nt(a, b):
    return lax.dot_general(a.astype(BF16), b.astype(BF16), (((1,), (1,)), ((), ())), preferred_element_type=F32)


def _bdot_tn(a, b):
    return lax.dot_general(a.astype(BF16), b.astype(BF16), (((0,), (0,)), ((), ())), preferred_element_type=F32)


def _sigmoid(x):
    return 1.0 / (1.0 + jnp.exp(-x))


def _softplus(x):
    return jnp.maximum(x, 0.0) + jnp.log(1.0 + jnp.exp(-jnp.abs(x)))


def _heads(ref, col0):
    return jnp.stack([ref[:, col0 + N * h:col0 + N * (h + 1)] for h in range(H)])


def _store_heads(ref, col0, val):
    for h in range(H):
        ref[:, col0 + N * h:col0 + N * (h + 1)] = val[h]


def _lerp(c, s, mu):
    return c + (s - c) * mu


def _rwkv_pre(rc, rs, kc, ks, vc, vs, gc, gs, wdc, wds, adc, ads,
              mu_r, mu_k, mu_v, mu_g, mu_wd, mu_ad, w_up, w0, a_up, a0, k_k, k_a):
    r = _lerp(rc, rs, mu_r)
    k = _lerp(kc, ks, mu_k)
    v = _lerp(vc, vs, mu_v)
    g = _lerp(gc, gs, mu_g)
    wd = _lerp(wdc, wds, mu_wd)
    ad = _lerp(adc, ads, mu_ad)
    t = wd.shape[0]
    bdims = (((2,), (1,)), ((0,), (0,)))
    tw = jnp.broadcast_to(jnp.tanh(wd).astype(BF16)[None], (H, t, RANK))
    z = w0 + lax.dot_general(tw, w_up.astype(BF16), bdims, preferred_element_type=F32)
    w_raw = -_softplus(-z) - 0.5
    lw = -jnp.exp(w_raw)
    row = lax.broadcasted_iota(jnp.int32, (t, t), 0)
    col = lax.broadcasted_iota(jnp.int32, (t, t), 1)
    same_chunk = ((row >= col) & (row // WKV_CHUNK == col // WKV_CHUNK)).astype(F32)
    cl = jnp.einsum("hts,hsn->htn", jnp.broadcast_to(same_chunk[None], (H, t, t)), lw, precision=HI,
                    preferred_element_type=F32)
    adb = jnp.broadcast_to(ad.astype(BF16)[None], (H, t, RANK))
    alr = _sigmoid(a0 + lax.dot_general(adb, a_up.astype(BF16), bdims, preferred_element_type=F32))
    kk = k * k_k
    kk = kk / jnp.maximum(jnp.sqrt(jnp.sum(kk * kk, axis=-1, keepdims=True)), 1e-12)
    k2 = k * (1.0 + (alr - 1.0) * k_a)
    return r, lw, cl, k2, v, -kk, kk * alr, g


_MM_DIMS = {"nn": (((2,), (1,)), ((0,), (0,))), "nt": (((2,), (2,)), ((0,), (0,))), "tn": (((1,), (1,)), ((0,), (0,)))}


def _split(x):
    hi = x.astype(BF16)
    return hi, (x - hi.astype(F32)).astype(BF16)


def _dot3(a, b, kind):
    ah, al = _split(a)
    bh, bl = _split(b)
    dot = functools.partial(lax.dot_general, dimension_numbers=_MM_DIMS[kind], preferred_element_type=F32)
    return dot(ah, bh) + (dot(ah, bl) + dot(al, bh))


@functools.partial(jax.custom_vjp, nondiff_argnums=(2,))
def _mm(a, b, kind):
    return _dot3(a, b, kind)


def _mm_fwd(a, b, kind):
    return _dot3(a, b, kind), (a, b)


def _mm_bwd(kind, res, ct):
    a, b = res
    if kind == "nn":
        return _mm(ct, b, "nt"), _mm(a, ct, "tn")
    if kind == "nt":
        return _mm(ct, b, "nn"), _mm(ct, a, "tn")
    return _mm(b, ct, "nt"), _mm(a, ct, "nn")


_mm.defvjp(_mm_fwd, _mm_bwd)


def _chunk_masks(c):
    row = lax.broadcasted_iota(jnp.int32, (c, c), 0)
    col = lax.broadcasted_iota(jnp.int32, (c, c), 1)
    return (row >= col)[None], (row > col)[None], (row == col).astype(F32)[None]


def _wkv_aab(lw, cl, a, b):
    _, strict, _ = _chunk_masks(a.shape[1])
    return jnp.where(strict, _mm(a * jnp.exp(cl - lw), b * jnp.exp(-cl), "nt"), 0.0)


def _tri_inverse(x):
    c = x.shape[1]
    p = _chunk_masks(c)[2] + x
    for _ in range(int(math.log2(c)) - 1):
        x = _mm(x, x, "nn")
        p = p + _mm(p, x, "nn")
    return p


def _wkv_apply(s0, r, lw, cl, k, v, a, b, p):
    c = r.shape[1]
    incl, strict, _ = _chunk_masks(c)
    gi = jnp.exp(-cl)
    at = a * jnp.exp(cl - lw)
    rt = r * jnp.exp(cl)
    bt = b * gi
    kt = k * gi
    a_ak = jnp.where(strict, _mm(at, kt, "nt"), 0.0)
    a_rb = jnp.where(incl, _mm(rt, bt, "nt"), 0.0)
    a_rk = jnp.where(incl, _mm(rt, kt, "nt"), 0.0)
    sa = _mm(p, _mm(at, s0, "nt") + _mm(a_ak, v, "nn"), "nn")
    y = _mm(rt, s0, "nt") + _mm(a_rb, sa, "nn") + _mm(a_rk, v, "nn")
    s1 = (s0 + _mm(sa, bt, "tn") + _mm(v, kt, "tn")) * jnp.exp(cl[:, c - 1:c, :])
    return y, s1


def _rwkv_post(y, r, k2, v, g, lnx_w, lnx_b, r_k):
    mean = jnp.mean(y, axis=-1, keepdims=True)
    yc = y - mean
    var = jnp.mean(yc * yc, axis=-1, keepdims=True)
    yn = yc * lax.rsqrt(var + LNX_EPS) * lnx_w + lnx_b
    bonus = jnp.sum(r * k2 * r_k, axis=-1, keepdims=True) * v
    return (yn + bonus) * (g * _sigmoid(g))


def _fox_pre(q, k, f, q_g, k_g, f_b):
    qn = q * lax.rsqrt(jnp.mean(q * q, axis=-1, keepdims=True) + RMS_EPS) * q_g
    kn = k * lax.rsqrt(jnp.mean(k * k, axis=-1, keepdims=True) + RMS_EPS) * k_g
    x = f + f_b
    return qn, kn, jnp.minimum(x, 0.0) - jnp.log(1.0 + jnp.exp(-jnp.abs(x)))


def _rms_fwd(x, g):
    s = x.shape[0]

    def body(x_ref, g_ref, h_ref):
        xv = x_ref[...]
        h_ref[...] = (xv * lax.rsqrt(jnp.mean(xv * xv, axis=-1, keepdims=True) + RMS_EPS) * g_ref[...]).astype(BF16)

    return pl.pallas_call(
        body, name="rms_fwd", grid=(s // TOK_TILE,),
        in_specs=[pl.BlockSpec((TOK_TILE, D), lambda i: (i, 0)), pl.BlockSpec((1, D), lambda i: (0, 0))],
        out_specs=pl.BlockSpec((TOK_TILE, D), lambda i: (i, 0)),
        out_shape=jax.ShapeDtypeStruct((s, D), BF16), compiler_params=_params("arbitrary"))(x, g)


def _proj(h, wt, name):
    s, n = h.shape[0], wt.shape[0]

    def body(h_ref, w_ref, o_ref):
        o_ref[...] = _bdot_nt(h_ref[...], w_ref[...])

    return pl.pallas_call(
        body, name=name, grid=(s // TOK_TILE,),
        in_specs=[pl.BlockSpec((TOK_TILE, D), lambda i: (i, 0)), pl.BlockSpec((n, D), lambda i: (0, 0))],
        out_specs=pl.BlockSpec((TOK_TILE, n), lambda i: (i, 0)),
        out_shape=jax.ShapeDtypeStruct((s, n), F32), compiler_params=_params("arbitrary"))(h, wt)


def _proj_wgrad(h, du, name):
    s, n = du.shape

    def body(h_ref, du_ref, o_ref):
        @pl.when(pl.program_id(0) == 0)
        def _():
            o_ref[...] = jnp.zeros_like(o_ref)

        o_ref[...] += _bdot_tn(du_ref[...], h_ref[...])

    return pl.pallas_call(
        body, name=name, grid=(s // TOK_TILE,),
        in_specs=[pl.BlockSpec((TOK_TILE, D), lambda i: (i, 0)), pl.BlockSpec((TOK_TILE, n), lambda i: (i, 0))],
        out_specs=pl.BlockSpec((n, D), lambda i: (0, 0)),
        out_shape=jax.ShapeDtypeStruct((n, D), F32), compiler_params=_params("arbitrary"))(h, du)


def _proj_xgrad(x, g, dx2, dus, ws):
    s = x.shape[0]
    tile = HEAD_TILE
    k = len(dus)

    def body(*refs):
        x_ref, g_ref, dx2_ref = refs[:3]
        du_refs, w_refs = refs[3:3 + k], refs[3 + k:3 + 2 * k]
        dx_ref, dg_ref = refs[3 + 2 * k:]

        @pl.when(pl.program_id(0) == 0)
        def _():
            dg_ref[...] = jnp.zeros_like(dg_ref)

        dh = _bdot(du_refs[0][...], w_refs[0][...])
        for du_ref, w_ref in zip(du_refs[1:], w_refs[1:]):
            dh += _bdot(du_ref[...], w_ref[...])
        xv = x_ref[...]
        rs = lax.rsqrt(jnp.mean(xv * xv, axis=-1, keepdims=True) + RMS_EPS)
        xn = xv * rs
        dg_ref[...] += jnp.sum(dh * xn, axis=0, keepdims=True)
        dxn = dh * g_ref[...]
        dx_ref[...] = rs * (dxn - xn * jnp.mean(dxn * xn, axis=-1, keepdims=True)) + dx2_ref[...]

    tok = lambda n: pl.BlockSpec((tile, n), lambda i: (i, 0))
    fixed = lambda a: pl.BlockSpec(a.shape, lambda i: (0,) * a.ndim)
    return pl.pallas_call(
        body, name="proj_xgrad", grid=(s // tile,),
        in_specs=[tok(D), fixed(g), tok(D)] + [tok(du.shape[1]) for du in dus] + [fixed(w) for w in ws],
        out_specs=[tok(D), pl.BlockSpec((1, D), lambda i: (0, 0))],
        out_shape=[jax.ShapeDtypeStruct((s, D), F32), jax.ShapeDtypeStruct((1, D), F32)],
        compiler_params=_params("arbitrary"))(x, g, dx2, *dus, *ws)


def _tail(x, target, ya, o, ub, ug, w_oa, w_ob, w_o, fg):
    s = x.shape[0]
    tile = HEAD_TILE

    def body(x_ref, t_ref, ya_ref, o_ref, gb_ref, ug_ref, woa_ref, wob_ref, wo_ref, fg_ref,
             loss_ref, dfg_ref, dwo_ref, dwoa_ref, dwob_ref, dx2_ref, dya_ref, do_ref, dgb_ref, dug_ref):
        @pl.when(pl.program_id(0) == 0)
        def _():
            for r in (loss_ref, dfg_ref, dwo_ref, dwoa_ref, dwob_ref):
                r[...] = jnp.zeros_like(r)

        ya_v = ya_ref[...]
        gate_b = gb_ref[...]
        sg_b = _sigmoid(gate_b)
        silu_b = gate_b * sg_b
        o_v = jnp.concatenate([o_ref[h] for h in range(H)], axis=-1)
        yb_v = o_v * silu_b
        big_a = _bdot(ya_v, woa_ref[...])
        big_b = _bdot(yb_v, wob_ref[...])
        sa = _sigmoid(ug_ref[:, :D])
        sb = _sigmoid(ug_ref[:, D:])
        merged = sa * big_a + sb * big_b
        x2 = x_ref[...] + _bdot(merged, wo_ref[...])
        rs = lax.rsqrt(jnp.mean(x2 * x2, axis=-1, keepdims=True) + RMS_EPS)
        xn = x2 * rs
        err = xn * fg_ref[...] - t_ref[...]
        loss_ref[...] += (0.5 / D) * jnp.sum(err * err)
        dout = err * (1.0 / D)
        dfg_ref[...] += jnp.sum(dout * xn, axis=0, keepdims=True)
        dxn = dout * fg_ref[...]
        dx2 = rs * (dxn - xn * jnp.mean(dxn * xn, axis=-1, keepdims=True))
        dx2_ref[...] = dx2
        dwo_ref[...] += _bdot_tn(merged, dx2)
        dmerged = _bdot_nt(dx2, wo_ref[...])
        dbig_a = dmerged * sa
        dbig_b = dmerged * sb
        dug_ref[:, :D] = dmerged * big_a * sa * (1.0 - sa)
        dug_ref[:, D:] = dmerged * big_b * sb * (1.0 - sb)
        dwoa_ref[...] += _bdot_tn(ya_v, dbig_a)
        dwob_ref[...] += _bdot_tn(yb_v, dbig_b)
        dya_ref[...] = _bdot_nt(dbig_a, woa_ref[...])
        dyb = _bdot_nt(dbig_b, wob_ref[...])
        dgb_ref[...] = dyb * o_v * (sg_b * (1.0 + gate_b * (1.0 - sg_b)))
        _dov = dyb * silu_b
        for h in range(H):
            do_ref[h] = _dov[:, N * h:N * (h + 1)]

    tok = lambda n: pl.BlockSpec((tile, n), lambda i: (i, 0))
    hm = pl.BlockSpec((H, tile, N), lambda i: (0, i, 0))
    fixed = lambda shape: pl.BlockSpec(shape, lambda i: (0,) * len(shape))
    f32 = lambda *shape: jax.ShapeDtypeStruct(shape, F32)
    return pl.pallas_call(
        body, name="tail", grid=(s // tile,),
        in_specs=[tok(D), tok(D), tok(DA), hm, pl.BlockSpec((tile, DA), lambda i: (i, 3)), tok(NG),
                  fixed((DA, D)), fixed((DA, D)), fixed((D, D)), fixed((1, D))],
        out_specs=[fixed((1, 1)), fixed((1, D)), fixed((D, D)), fixed((DA, D)), fixed((DA, D)),
                   tok(D), tok(DA), hm, tok(DA), tok(NG)],
        out_shape=[f32(1, 1), f32(1, D), f32(D, D), f32(DA, D), f32(DA, D),
                   f32(s, D), f32(s, DA), f32(H, s, N), f32(s, DA), f32(s, NG)],
        compiler_params=_params("arbitrary"))(x, target, ya, o, ub, ug, w_oa, w_ob, w_o, fg)


_PRE_PARAM_SHAPES = ((H, 1, N),) * 4 + ((1, RANK),) * 2 + ((H, RANK, N), (H, 1, N), (H, RANK, N), (H, 1, N), (H, 1, N),
                                                              (H, 1, N))


def _pre_operands(ua_ref, prev_ref, first):
    cur = ua_ref[...]
    t = cur.shape[0]
    prev_row = jnp.where(first, 0.0, prev_ref[7:8, :])
    rows = lax.broadcasted_iota(jnp.int32, cur.shape, 0)
    sh = jnp.where(rows == 0, prev_row, pltpu.roll(cur, 1, axis=0))
    ops = []
    for c0 in (0, DA, 2 * DA, 3 * DA + 2 * RANK):
        ops.append(jnp.stack([cur[:, c0 + N * h:c0 + N * (h + 1)] for h in range(H)]))
        ops.append(jnp.stack([sh[:, c0 + N * h:c0 + N * (h + 1)] for h in range(H)]))
    for c0 in (3 * DA, 3 * DA + RANK):
        ops.append(cur[:, c0:c0 + RANK])
        ops.append(sh[:, c0:c0 + RANK])
    del t
    return ops


def _ua_specs(tile, order):
    blocks = tile // 8
    return [pl.BlockSpec((tile, NA), lambda i: (order(i), 0)),
            pl.BlockSpec((8, NA), lambda i: (jnp.maximum(order(i) * blocks - 1, 0), 0))]


def _rwkv_pre_fwd(ua, pre_params):
    s = ua.shape[0]
    tile = HEAD_TILE

    def body(ua_ref, prev_ref, *refs):
        p_refs, o_refs = refs[:len(pre_params)], refs[len(pre_params):]
        ops = _pre_operands(ua_ref, prev_ref, pl.program_id(0) == 0)
        outs = _rwkv_pre(*ops, *[p[...] for p in p_refs])
        for o_ref, val in zip(o_refs, outs):
            o_ref[...] = val

    hm = pl.BlockSpec((H, tile, N), lambda i: (0, i, 0))
    return pl.pallas_call(
        body, name="rwkv_pre_fwd", grid=(s // tile,),
        in_specs=_ua_specs(tile, lambda i: i) + [pl.BlockSpec(p.shape, lambda i, nd=p.ndim: (0,) * nd) for p in pre_params],
        out_specs=[hm] * 8, out_shape=[jax.ShapeDtypeStruct((H, s, N), F32)] * 8,
        compiler_params=_params("arbitrary"))(ua, ua, *pre_params)


def _rwkv_pre_bwd(ua, pre_params, cots):
    s = ua.shape[0]
    tile = HEAD_TILE
    nt = s // tile
    n_p = len(pre_params)

    def body(ua_ref, prev_ref, *refs):
        p_refs, c_refs = refs[:n_p], refs[n_p:n_p + 11]
        dua_ref = refs[n_p + 11]
        dp_refs = refs[n_p + 12:n_p + 12 + n_p]
        carry_ref = refs[-1]
        i = pl.program_id(0)

        @pl.when(i == 0)
        def _():
            carry_ref[...] = jnp.zeros_like(carry_ref)
            for r in dp_refs:
                r[...] = jnp.zeros_like(r)

        ops = _pre_operands(ua_ref, prev_ref, i == nt - 1)
        _, vjp = jax.vjp(_rwkv_pre, *ops, *[p[...] for p in p_refs])
        c = [r[...] for r in c_refs]
        grads = vjp((c[0] + c[1], c[2], c[3], c[4] + c[5], c[6] + c[7], c[8], c[9], c[10]))
        d_ops, d_par = grads[:12], grads[12:]
        for r, val in zip(dp_refs, d_par):
            r[...] += val
        d_cur = jnp.concatenate([d_ops[0][h] for h in range(H)] + [d_ops[2][h] for h in range(H)]
                                + [d_ops[4][h] for h in range(H)] + [d_ops[8], d_ops[10]]
                                + [d_ops[6][h] for h in range(H)], axis=-1)
        d_sh = jnp.concatenate([d_ops[1][h] for h in range(H)] + [d_ops[3][h] for h in range(H)]
                               + [d_ops[5][h] for h in range(H)] + [d_ops[9], d_ops[11]]
                               + [d_ops[7][h] for h in range(H)], axis=-1)
        rows = lax.broadcasted_iota(jnp.int32, d_sh.shape, 0)
        dua_ref[...] = d_cur + jnp.where(rows == tile - 1, carry_ref[...], pltpu.roll(d_sh, tile - 1, axis=0))
        carry_ref[...] = d_sh[0:1, :]

    rev = lambda i: nt - 1 - i
    hm = pl.BlockSpec((H, tile, N), lambda i: (0, rev(i), 0))
    fixed = [pl.BlockSpec(p.shape, lambda i, nd=p.ndim: (0,) * nd) for p in pre_params]
    return pl.pallas_call(
        body, name="rwkv_pre_bwd", grid=(nt,),
        in_specs=_ua_specs(tile, rev) + fixed + [hm] * 11,
        out_specs=[pl.BlockSpec((tile, NA), lambda i: (rev(i), 0))] + fixed,
        out_shape=[jax.ShapeDtypeStruct((s, NA), F32)] + [jax.ShapeDtypeStruct(p.shape, F32) for p in pre_params],
        scratch_shapes=[pltpu.VMEM((1, NA), F32)],
        compiler_params=_params("arbitrary"))(ua, ua, *pre_params, *cots)


def _wkv_fwd(seq):
    s = seq[0].shape[1]
    nc = s // WKV_CHUNK

    def body(r_ref, lw_ref, cl_ref, k_ref, v_ref, a_ref, b_ref, y_ref, ck_ref, p_ref, state):
        @pl.when(pl.program_id(0) == 0)
        def _():
            state[...] = jnp.zeros_like(state)

        s0 = state[...]
        ck_ref[0] = s0
        p = _tri_inverse(_wkv_aab(lw_ref[...], cl_ref[...], a_ref[...], b_ref[...]))
        p_ref[0] = p
        y, s1 = _wkv_apply(s0, r_ref[...], lw_ref[...], cl_ref[...], k_ref[...], v_ref[...], a_ref[...], b_ref[...], p)
        y_ref[...] = y
        state[...] = s1

    hm = pl.BlockSpec((H, WKV_CHUNK, N), lambda c: (0, c, 0))
    per_chunk = lambda m: pl.BlockSpec((1, H, m, m), lambda c: (c, 0, 0, 0))
    return pl.pallas_call(
        body, name="wkv_fwd", grid=(nc,), in_specs=[hm] * 7,
        out_specs=[hm, per_chunk(N), per_chunk(WKV_CHUNK)],
        out_shape=[jax.ShapeDtypeStruct((H, s, N), F32), jax.ShapeDtypeStruct((nc, H, N, N), F32),
                   jax.ShapeDtypeStruct((nc, H, WKV_CHUNK, WKV_CHUNK), F32)],
        scratch_shapes=[pltpu.VMEM((H, N, N), F32)], compiler_params=_params("arbitrary"))(*seq)


def _wkv_bwd(seq, ckpt, pinv, dy):
    s = seq[0].shape[1]
    nc = s // WKV_CHUNK

    def body(r_ref, lw_ref, cl_ref, k_ref, v_ref, a_ref, b_ref, ck_ref, p_ref, dy_ref, *refs):
        d_refs, dstate = refs[:7], refs[7]

        @pl.when(pl.program_id(0) == 0)
        def _():
            dstate[...] = jnp.zeros_like(dstate)

        p = p_ref[0]
        lw, cl, a, b = lw_ref[...], cl_ref[...], a_ref[...], b_ref[...]
        _, vjp = jax.vjp(_wkv_apply, ck_ref[0], r_ref[...], lw, cl, k_ref[...], v_ref[...], a, b, p)
        ds0, dr, dlw, dcl, dk, dv, da, db, dp = vjp((dy_ref[...], dstate[...]))
        dstate[...] = ds0
        _, vjp_x = jax.vjp(_wkv_aab, lw, cl, a, b)
        dlw2, dcl2, da2, db2 = vjp_x(_mm(_mm(p, dp, "tn"), p, "nt"))
        for d_ref, val in zip(d_refs, (dr, dlw + dlw2, dcl + dcl2, dk, dv, da + da2, db + db2)):
            d_ref[...] = val

    hm = pl.BlockSpec((H, WKV_CHUNK, N), lambda c: (0, nc - 1 - c, 0))
    per_chunk = lambda m: pl.BlockSpec((1, H, m, m), lambda c: (nc - 1 - c, 0, 0, 0))
    return pl.pallas_call(
        body, name="wkv_bwd", grid=(nc,),
        in_specs=[hm] * 7 + [per_chunk(N), per_chunk(WKV_CHUNK), hm],
        out_specs=[hm] * 7, out_shape=[jax.ShapeDtypeStruct((H, s, N), F32)] * 7,
        scratch_shapes=[pltpu.VMEM((H, N, N), F32)], compiler_params=_params("arbitrary"))(*seq, ckpt, pinv, dy)


def _rwkv_post_fwd(y, r, k2, v, g, post_params):
    s = y.shape[1]
    tile = HEAD_TILE

    def body(y_ref, r_ref, k_ref, v_ref, g_ref, w_ref, b_ref, rk_ref, o_ref):
        out = _rwkv_post(y_ref[...], r_ref[...], k_ref[...], v_ref[...], g_ref[...], w_ref[...], b_ref[...],
                         rk_ref[...])
        o_ref[...] = jnp.concatenate([out[h] for h in range(H)], axis=-1)

    hm = pl.BlockSpec((H, tile, N), lambda i: (0, i, 0))
    par = pl.BlockSpec((H, 1, N), lambda i: (0, 0, 0))
    return pl.pallas_call(
        body, name="rwkv_post_fwd", grid=(s // tile,), in_specs=[hm] * 5 + [par] * 3,
        out_specs=pl.BlockSpec((tile, DA), lambda i: (i, 0)), out_shape=jax.ShapeDtypeStruct((s, DA), F32),
        compiler_params=_params("arbitrary"))(y, r, k2, v, g, *post_params)


def _rwkv_post_bwd(y, r, k2, v, g, post_params, dya):
    s = y.shape[1]
    tile = HEAD_TILE

    def body(y_ref, r_ref, k_ref, v_ref, g_ref, w_ref, b_ref, rk_ref, dya_ref, *d_refs):
        @pl.when(pl.program_id(0) == 0)
        def _():
            for ref in d_refs[5:]:
                ref[...] = jnp.zeros_like(ref)

        _, vjp = jax.vjp(_rwkv_post, y_ref[...], r_ref[...], k_ref[...], v_ref[...], g_ref[...], w_ref[...],
                         b_ref[...], rk_ref[...])
        grads = vjp(jnp.stack([dya_ref[:, N * h:N * (h + 1)] for h in range(H)]))
        for ref, val in zip(d_refs[:5], grads[:5]):
            ref[...] = val
        for ref, val in zip(d_refs[5:], grads[5:]):
            ref[...] += val

    hm = pl.BlockSpec((H, tile, N), lambda i: (0, i, 0))
    par = pl.BlockSpec((H, 1, N), lambda i: (0, 0, 0))
    return pl.pallas_call(
        body, name="rwkv_post_bwd", grid=(s // tile,),
        in_specs=[hm] * 5 + [par] * 3 + [pl.BlockSpec((tile, DA), lambda i: (i, 0))],
        out_specs=[hm] * 5 + [par] * 3,
        out_shape=[jax.ShapeDtypeStruct((H, s, N), F32)] * 5 + [jax.ShapeDtypeStruct((H, 1, N), F32)] * 3,
        compiler_params=_params("arbitrary"))(y, r, k2, v, g, *post_params, dya)


def _tri(t):
    return (lax.broadcasted_iota(jnp.int32, (t, t), 0) >= lax.broadcasted_iota(jnp.int32, (t, t), 1)).astype(F32)


def _fox_pre_fwd(ub, uf, q_g, k_g, f_b):
    s = ub.shape[0]
    tile = HEAD_TILE

    def body(ub_ref, uf_ref, qg_ref, kg_ref, fb_ref, q_ref, k_ref, v_ref, cum_ref, carry):
        @pl.when(pl.program_id(0) == 0)
        def _():
            carry[...] = jnp.zeros_like(carry)

        qn, kn, logf = _fox_pre(_heads(ub_ref, 0), _heads(ub_ref, DA), uf_ref[...], qg_ref[...], kg_ref[...],
                                fb_ref[...])
        q_ref[...] = qn
        k_ref[...] = kn
        v_ref[...] = _heads(ub_ref, 2 * DA)
        cum = jnp.dot(_tri(tile), logf, precision=HI, preferred_element_type=F32) + carry[...]
        cum_ref[...] = cum
        carry[...] = cum[tile - 1:tile, :]

    hm = pl.BlockSpec((H, tile, N), lambda i: (0, i, 0))
    fixed = lambda shape: pl.BlockSpec(shape, lambda i: (0,) * len(shape))
    return pl.pallas_call(
        body, name="fox_pre_fwd", grid=(s // tile,),
        in_specs=[pl.BlockSpec((tile, NB), lambda i: (i, 0)), pl.BlockSpec((tile, NF), lambda i: (i, 0)),
                  fixed((1, 1, N)), fixed((1, 1, N)), fixed((1, NF))],
        out_specs=[hm] * 3 + [pl.BlockSpec((tile, NF), lambda i: (i, 0))],
        out_shape=[jax.ShapeDtypeStruct((H, s, N), F32)] * 3 + [jax.ShapeDtypeStruct((s, NF), F32)],
        scratch_shapes=[pltpu.VMEM((1, NF), F32)], compiler_params=_params("arbitrary"))(ub, uf, q_g, k_g, f_b)


def _fox_pre_bwd(ub, uf, q_g, k_g, f_b, dqn, dkn, dvf, dgate, dcum_q, dcum_k):
    s = ub.shape[0]
    tile = HEAD_TILE
    nt = s // tile

    def body(ub_ref, uf_ref, qg_ref, kg_ref, fb_ref, dq_ref, dk_ref, dv_ref, dgate_ref, dcq_ref, dck_ref,
             dub_ref, duf_ref, dqg_ref, dkg_ref, dfb_ref, carry):
        @pl.when(pl.program_id(0) == 0)
        def _():
            carry[...] = jnp.zeros_like(carry)
            for ref in (dqg_ref, dkg_ref, dfb_ref):
                ref[...] = jnp.zeros_like(ref)

        dcum = dcq_ref[...] + dck_ref[...]
        dlogf = lax.dot_general(_tri(tile), dcum, (((0,), (0,)), ((), ())), precision=HI,
                                preferred_element_type=F32) + carry[...]
        carry[...] = dlogf[0:1, :]
        _, vjp = jax.vjp(_fox_pre, _heads(ub_ref, 0), _heads(ub_ref, DA), uf_ref[...], qg_ref[...], kg_ref[...],
                         fb_ref[...])
        d_q, d_k, d_f, d_qg, d_kg, d_fb = vjp((dq_ref[...], dk_ref[...], dlogf))
        _store_heads(dub_ref, 0, d_q)
        _store_heads(dub_ref, DA, d_k)
        _store_heads(dub_ref, 2 * DA, dv_ref[...])
        dub_ref[:, 3 * DA:] = dgate_ref[...]
        duf_ref[...] = d_f
        dqg_ref[...] += d_qg
        dkg_ref[...] += d_kg
        dfb_ref[...] += d_fb

    rev = lambda i: nt - 1 - i
    hm = pl.BlockSpec((H, tile, N), lambda i: (0, rev(i), 0))
    tok = lambda n: pl.BlockSpec((tile, n), lambda i: (rev(i), 0))
    fixed = lambda shape: pl.BlockSpec(shape, lambda i: (0,) * len(shape))
    return pl.pallas_call(
        body, name="fox_pre_bwd", grid=(nt,),
        in_specs=[tok(NB), tok(NF), fixed((1, 1, N)), fixed((1, 1, N)), fixed((1, NF)), hm, hm, hm, tok(DA), tok(NF),
                  tok(NF)],
        out_specs=[tok(NB), tok(NF), fixed((1, 1, N)), fixed((1, 1, N)), fixed((1, NF))],
        out_shape=[jax.ShapeDtypeStruct((s, NB), F32), jax.ShapeDtypeStruct((s, NF), F32),
                   jax.ShapeDtypeStruct((1, 1, N), F32), jax.ShapeDtypeStruct((1, 1, N), F32),
                   jax.ShapeDtypeStruct((1, NF), F32)],
        scratch_shapes=[pltpu.VMEM((1, NF), F32)],
        compiler_params=_params("arbitrary"))(ub, uf, q_g, k_g, f_b, dqn, dkn, dvf, dgate, dcum_q, dcum_k)


def _att_groups(s):
    blocks = s // ATT_TILE
    per = max(1, blocks // ATT_GROUPS)
    return per, blocks // per


def _att_logits(q_bf, k, cq, ck, qi):
    tq, sk = q_bf.shape[0], k.shape[0]
    logits = _bdot_nt(q_bf, k) * ATT_SCALE + cq - ck
    rows = qi * tq + lax.broadcasted_iota(jnp.int32, (tq, sk), 0)
    mask = rows >= lax.broadcasted_iota(jnp.int32, (tq, sk), 1)
    return jnp.where(mask, logits, -1e30), mask


def _fox_attn_fwd(q, k, v, cum_q, cum_k):
    s = q.shape[1]
    t = ATT_TILE
    per, groups = _att_groups(s)

    def body(q_ref, k_ref, v_ref, cq_ref, ck_ref, o_ref, lse_ref):
        qi = pl.program_id(1)
        for g in range(groups):
            @pl.when(qi // per == g)
            def _(n=(g + 1) * per * t):
                logits, _ = _att_logits(q_ref[0].astype(BF16), k_ref[0, :n, :], cq_ref[0], ck_ref[0, :, :n], qi)
                m = jnp.max(logits, axis=-1, keepdims=True)
                p = jnp.exp(logits - m)
                l = jnp.sum(p, axis=-1, keepdims=True)
                o_ref[0] = _bdot(p, v_ref[0, :n, :]) / l
                lse_ref[0] = m + jnp.log(l)

    qb = pl.BlockSpec((1, t, N), lambda h, i: (h, i, 0))
    kb = pl.BlockSpec((1, s, N), lambda h, i: (h, 0, 0))
    return pl.pallas_call(
        body, name="fox_attn_fwd", grid=(H, s // t),
        in_specs=[qb, kb, kb, pl.BlockSpec((1, t, 1), lambda h, i: (h, i, 0)),
                  pl.BlockSpec((1, 1, s), lambda h, i: (h, 0, 0))],
        out_specs=[qb, pl.BlockSpec((1, t, 1), lambda h, i: (h, i, 0))],
        out_shape=[jax.ShapeDtypeStruct((H, s, N), F32), jax.ShapeDtypeStruct((H, s, 1), F32)],
        compiler_params=_params("arbitrary", "arbitrary"))(q, k, v, cum_q, cum_k)


def _fox_attn_bwd(q, k, v, cum_q, cum_k, o, lse, do):
    s = q.shape[1]
    t = ATT_TILE
    per, groups = _att_groups(s)

    def body(q_ref, k_ref, v_ref, cq_ref, ck_ref, o_ref, lse_ref, do_ref, dq_ref, dk_ref, dv_ref, dcq_ref, dck_ref):
        qi = pl.program_id(1)

        @pl.when(qi == 0)
        def _():
            for ref in (dk_ref, dv_ref, dck_ref):
                ref[...] = jnp.zeros_like(ref)

        for g in range(groups):
            @pl.when(qi // per == g)
            def _(n=(g + 1) * per * t):
                q_bf, do_bf = q_ref[0].astype(BF16), do_ref[0].astype(BF16)
                kv = k_ref[0, :n, :]
                logits, mask = _att_logits(q_bf, kv, cq_ref[0], ck_ref[0, :, :n], qi)
                p = jnp.where(mask, jnp.exp(logits - lse_ref[0]), 0.0)
                delta = jnp.sum(do_ref[0] * o_ref[0], axis=-1, keepdims=True)
                ds = p * (_bdot_nt(do_bf, v_ref[0, :n, :]) - delta)
                dq_ref[0] = _bdot(ds, kv) * ATT_SCALE
                dk_ref[0, :n, :] += _bdot_tn(ds, q_bf) * ATT_SCALE
                dv_ref[0, :n, :] += _bdot_tn(p, do_bf)
                dcq_ref[0] = jnp.sum(ds, axis=-1, keepdims=True)
                dck_ref[0, :, :n] -= jnp.sum(ds, axis=0, keepdims=True)

    qb = pl.BlockSpec((1, t, N), lambda h, i: (h, i, 0))
    kb = pl.BlockSpec((1, s, N), lambda h, i: (h, 0, 0))
    cqb = pl.BlockSpec((1, t, 1), lambda h, i: (h, i, 0))
    ckb = pl.BlockSpec((1, 1, s), lambda h, i: (h, 0, 0))
    f32 = lambda *shape: jax.ShapeDtypeStruct(shape, F32)
    return pl.pallas_call(
        body, name="fox_attn_bwd", grid=(H, s // t),
        in_specs=[qb, kb, kb, cqb, ckb, qb, cqb, qb], out_specs=[qb, kb, kb, cqb, ckb],
        out_shape=[f32(H, s, N), f32(H, s, N), f32(H, s, N), f32(H, s, 1), f32(H, 1, s)],
        compiler_params=_params("arbitrary", "arbitrary"))(q, k, v, cum_q, cum_k, o, lse, do)


def _head_param(p):
    return p.reshape(H, 1, N)


def _local_step(x, target, w, p):
    mu = p["shift_mu"]
    pre_params = (_head_param(mu[:, 0:DA]), _head_param(mu[:, DA:2 * DA]), _head_param(mu[:, 2 * DA:3 * DA]),
                  _head_param(mu[:, 3 * DA + 2 * RANK:]), mu[:, 3 * DA:3 * DA + RANK],
                  mu[:, 3 * DA + RANK:3 * DA + 2 * RANK],
                  w["w_lora_up"].astype(F32), _head_param(p["w0"]), w["a_lora_up"].astype(F32), _head_param(p["a0"]),
                  _head_param(p["k_k"]), _head_param(p["k_a"]))
    post_params = (_head_param(p["lnx_w"]), _head_param(p["lnx_b"]), _head_param(p["r_k"]))
    q_g, k_g = p["q_norm_g"].reshape(1, 1, N), p["k_norm_g"].reshape(1, 1, N)
    f_b = jnp.pad(p["f_bias"], ((0, 0), (0, NF - H)))
    fg = p["final_norm_g"].reshape(1, D)

    h = _rms_fwd(x, p["norm_g"])
    ua = _proj(h, w["in_a"], "proj_a")
    ub = _proj(h, w["in_b"], "proj_b")
    ug = _proj(h, w["in_g"], "proj_g")
    uf = _proj(h, w["in_f"], "proj_f")
    r, lw, cl, k2, v, av, bv, gg = _rwkv_pre_fwd(ua, pre_params)
    y, ckpt, pinv = _wkv_fwd((r, lw, cl, k2, v, av, bv))
    ya = _rwkv_post_fwd(y, r, k2, v, gg, post_params)
    qn, kn, vf, cum = _fox_pre_fwd(ub, uf, q_g, k_g, f_b)
    cum_t = cum[:, :H].T
    cum_q, cum_k = cum_t[:, :, None], cum_t[:, None, :]
    o, lse = _fox_attn_fwd(qn, kn, vf, cum_q, cum_k)

    (loss, dfg, dwo, dwoa, dwob, dx2, dya, do, dgate_b, dug) = _tail(
        x, target, ya, o, ub, ug, w["w_out_a"], w["w_out_b"], w["w_out"], fg)
    dqn, dkn, dvf, dcq, dck = _fox_attn_bwd(qn, kn, vf, cum_q, cum_k, o, lse, do)
    pad_f = lambda a: jnp.pad(a.T, ((0, 0), (0, NF - H)))
    dub, duf, dqg, dkg, dfb = _fox_pre_bwd(ub, uf, q_g, k_g, f_b, dqn, dkn, dvf, dgate_b,
                                           pad_f(dcq[:, :, 0]), pad_f(dck.reshape(H, -1)))
    dy, dr_p, dk_p, dv_p, dgg, dlnw, dlnb, drk = _rwkv_post_bwd(y, r, k2, v, gg, post_params, dya)
    dr_s, dlw, dcl, dk_s, dv_s, dav, dbv = _wkv_bwd((r, lw, cl, k2, v, av, bv), ckpt, pinv, dy)
    pre_out = _rwkv_pre_bwd(ua, pre_params, (dr_s, dr_p, dlw, dcl, dk_s, dk_p, dv_s, dv_p, dav, dbv, dgg))
    dua, dpre = pre_out[0], pre_out[1:]
    dws = [_proj_wgrad(h, du, name) for du, name in
           ((dua, "wgrad_a"), (dub, "wgrad_b"), (dug, "wgrad_g"), (duf, "wgrad_f"))]
    dx, dng = _proj_xgrad(x, p["norm_g"], dx2, (dua, dub, dug, duf), (w["in_a"], w["in_b"], w["in_g"], w["in_f"]))

    flat = lambda a: a.reshape(1, -1)
    dmu = jnp.concatenate([flat(dpre[0]), flat(dpre[1]), flat(dpre[2]), dpre[4], dpre[5], flat(dpre[3])], axis=1)
    grads = {
        "w_in_segments": (dws[0], dws[1], dws[2], dws[3]),
        "w_out_a": dwoa, "w_out_b": dwob, "w_out": dwo, "w_lora_up": dpre[6], "a_lora_up": dpre[8],
        "norm_g": dng, "final_norm_g": dfg, "shift_mu": dmu, "w0": flat(dpre[7]), "a0": flat(dpre[9]),
        "k_k": flat(dpre[10]), "k_a": flat(dpre[11]), "r_k": flat(drk), "lnx_w": flat(dlnw), "lnx_b": flat(dlnb),
        "q_norm_g": flat(dqg), "k_norm_g": flat(dkg), "f_bias": dfb[:, :H],
    }
    return loss, dx, grads


def _position():
    return lax.axis_index("x"), lax.axis_index("y"), lax.axis_index("c")


def _hbm_specs(n):
    return [pl.BlockSpec(memory_space=pl.ANY)] * n


def _all_gather(blocks, name):
    n = len(blocks)

    def body(*refs):
        x_refs, out_refs = refs[:n], refs[n:2 * n]
        send_sems, recv_sems, local_sems = refs[2 * n:]
        x, y, c = _position()
        me, sibling = (x, y, c), (x, y, 1 - c)
        chips = [(1 - x, y), (x, 1 - y), (1 - x, 1 - y)]

        def copy(a, k, blk, to, own=False):
            dst = out_refs[a].at[4 * blk[0] + 2 * blk[1] + blk[2]]
            return pltpu.make_async_remote_copy(
                src_ref=x_refs[a] if own else dst, dst_ref=dst, send_sem=send_sems.at[7 * a + k],
                recv_sem=recv_sems.at[7 * a + k], device_id=to, device_id_type=MESH)

        mine = [pltpu.make_async_copy(x_refs[a], out_refs[a].at[4 * x + 2 * y + c], local_sems.at[a]) for a in range(n)]
        for cp in mine:
            cp.start()
        first = []
        for a in range(n):
            first.append(copy(a, 0, me, sibling, own=True))
            first += [copy(a, 1 + j, me, (*chip, c), own=True) for j, chip in enumerate(chips)]
        for cp in first:
            cp.start()
        passed = []
        for j, chip in enumerate(chips):
            for a in range(n):
                copy(a, 1 + j, (*chip, c), me).wait_recv()
                passed.append(copy(a, 4 + j, (*chip, c), sibling))
                passed[-1].start()
        for a in range(n):
            copy(a, 0, sibling, me).wait_recv()
        for j, chip in enumerate(chips):
            for a in range(n):
                copy(a, 4 + j, (*chip, 1 - c), me).wait_recv()
        for cp in first + passed:
            cp.wait_send()
        for cp in mine:
            cp.wait()

    return pl.pallas_call(
        body, name=name, out_shape=[jax.ShapeDtypeStruct((N_DEV,) + b.shape, b.dtype) for b in blocks],
        in_specs=_hbm_specs(n), out_specs=_hbm_specs(n),
        scratch_shapes=[pltpu.SemaphoreType.DMA((7 * n,)), pltpu.SemaphoreType.DMA((7 * n,)),
                        pltpu.SemaphoreType.DMA((n,))],
    )(*blocks)


def _exchange(slabs, name):
    n = len(slabs)

    def body(*refs):
        s_refs, out_refs = refs[:n], refs[n:2 * n]
        send_sems, recv_sems, local_sems = refs[2 * n:]
        x, y, c = _position()
        me = 4 * x + 2 * y + c
        mine = [pltpu.make_async_copy(s_refs[a].at[me], out_refs[a].at[me], local_sems.at[a]) for a in range(n)]
        for cp in mine:
            cp.start()
        sends, recvs = [], []
        for m in range(1, N_DEV):
            px, py, pc = x ^ (m >> 2), y ^ ((m >> 1) & 1), c ^ (m & 1)
            peer = 4 * px + 2 * py + pc
            for a in range(n):
                sem = dict(send_sem=send_sems.at[7 * a + m - 1], recv_sem=recv_sems.at[7 * a + m - 1],
                           device_id=(px, py, pc), device_id_type=MESH)
                sends.append(pltpu.make_async_remote_copy(src_ref=s_refs[a].at[peer], dst_ref=out_refs[a].at[me], **sem))
                recvs.append(pltpu.make_async_remote_copy(src_ref=s_refs[a].at[me], dst_ref=out_refs[a].at[peer], **sem))
        for cp in sends:
            cp.start()
        for cp in recvs:
            cp.wait_recv()
        for cp in sends:
            cp.wait_send()
        for cp in mine:
            cp.wait()

    return pl.pallas_call(
        body, name=name, out_shape=[jax.ShapeDtypeStruct(s.shape, s.dtype) for s in slabs],
        in_specs=_hbm_specs(n), out_specs=_hbm_specs(n),
        scratch_shapes=[pltpu.SemaphoreType.DMA((7 * n,)), pltpu.SemaphoreType.DMA((7 * n,)),
                        pltpu.SemaphoreType.DMA((n,))],
    )(*slabs)


def _sum_adamw(recv, w, m, v, tile, name, axis=0):
    rows, cols = w.shape

    def body(r_ref, w_ref, m_ref, v_ref, g_ref, d_ref, mo_ref, vo_ref):
        g = r_ref[0].astype(F32)
        for k in range(1, N_DEV):
            g = g + r_ref[k].astype(F32)
        m_new = ADAM_B1 * m_ref[...] + (1.0 - ADAM_B1) * g
        v_new = ADAM_B2 * v_ref[...] + (1.0 - ADAM_B2) * (g * g)
        m_hat = m_new / (1.0 - ADAM_B1 ** ADAM_STEP)
        v_hat = v_new / (1.0 - ADAM_B2 ** ADAM_STEP)
        g_ref[...] = g
        d_ref[...] = -ADAM_LR * (m_hat / (jnp.sqrt(v_hat) + ADAM_EPS) + ADAM_WD * w_ref[...])
        mo_ref[...] = m_new
        vo_ref[...] = v_new

    if axis == 0:
        blk = pl.BlockSpec((tile, cols), lambda i: (i, 0))
        rblk = pl.BlockSpec((N_DEV, tile, cols), lambda i: (0, i, 0))
    else:
        blk = pl.BlockSpec((rows, tile), lambda i: (0, i))
        rblk = pl.BlockSpec((N_DEV, rows, tile), lambda i: (0, 0, i))
    return pl.pallas_call(
        body, name=name, grid=((rows if axis == 0 else cols) // tile,),
        in_specs=[rblk, blk, blk, blk],
        out_specs=[blk] * 4, out_shape=[jax.ShapeDtypeStruct((rows, cols), F32)] * 4,
        compiler_params=_params("arbitrary"))(recv, w, m, v)


_REST_NAMES = tuple(n for n, _ in REST_ROWS)
_SMALL_NAMES = tuple(n for n, _ in SMALL)


def _pack_rest(t):
    return jnp.concatenate([t[n].reshape(-1, LANES) for n in _REST_NAMES], axis=0)


def _unpack_rest(packed, like):
    out, row = {}, 0
    for n, r in REST_ROWS:
        out[n] = packed[row:row + r].reshape(like[n].shape)
        row += r
    return out


def _pack_small(t, last):
    flat = jnp.concatenate([t[n].reshape(-1) for n in _SMALL_NAMES] + [last.reshape(-1)])
    return jnp.pad(flat, (0, SMALL_ROWS * LANES - flat.shape[0])).reshape(SMALL_ROWS, LANES)


def _unpack_small(packed, like):
    out, flat, off = {}, packed.reshape(-1), 0
    for n, size in SMALL:
        out[n] = flat[off:off + size].reshape(like[n].shape)
        off += size
    return out, flat[off]


def _gather_weights(t):
    w_in_all, rest_all = _all_gather([t["w_in"][0].T.astype(BF16), _pack_rest(t).astype(BF16)], "weight_gather")
    wt = w_in_all.reshape(IN_COLS, D)
    in_a, in_b, in_g = wt[:NA], wt[NA:NA + NB], wt[NA + NB + H:]
    in_f = jnp.pad(wt[NA + NB:NA + NB + H], ((0, NF - H), (0, 0)))
    parts, row = {}, 0
    for n, r in REST_ROWS:
        parts[n] = rest_all[:, row:row + r]
        row += r
    by_cols = lambda a: jnp.moveaxis(a, 0, 1).reshape(a.shape[1], -1)
    return {
        "in_a": in_a, "in_b": in_b, "in_g": in_g, "in_f": in_f,
        "w_out_a": by_cols(parts["w_out_a"]), "w_out_b": by_cols(parts["w_out_b"]),
        "w_out": parts["w_out"].reshape(D, D),
        "w_lora_up": parts["w_lora_up"].reshape(H, RANK, N), "a_lora_up": parts["a_lora_up"].reshape(H, RANK, N),
    }


def _grad_slabs(g, loss):
    by_cols = lambda a: jnp.moveaxis(a.reshape(a.shape[0], N_DEV, -1), 1, 0)
    da, db, dg, df = g["w_in_segments"]
    w_in_t = jnp.concatenate([da, db, df[:H], dg], axis=0).astype(BF16).reshape(N_DEV, COLS_PER_DEV, D)
    rest = jnp.concatenate([by_cols(g["w_out_a"]), by_cols(g["w_out_b"]), g["w_out"].reshape(N_DEV, -1, LANES),
                            g["w_lora_up"].reshape(N_DEV, -1, LANES), g["a_lora_up"].reshape(N_DEV, -1, LANES)],
                           axis=1).astype(BF16)
    small = jnp.broadcast_to(_pack_small(g, loss)[None], (N_DEV, SMALL_ROWS, LANES))
    return [w_in_t, rest, small]


def kernel(x, norm_g, w_in, shift_mu, w_lora_up, w0, a_lora_up, a0, k_k, k_a, r_k, lnx_w, lnx_b, f_bias, q_norm_g, k_norm_g, w_out_a, w_out_b, w_out, final_norm_g, loss_target, m_norm_g, m_w_in, m_shift_mu, m_w_lora_up, m_w0, m_a_lora_up, m_a0, m_k_k, m_k_a, m_r_k, m_lnx_w, m_lnx_b, m_f_bias, m_q_norm_g, m_k_norm_g, m_w_out_a, m_w_out_b, m_w_out, m_final_norm_g, v_norm_g, v_w_in, v_shift_mu, v_w_lora_up, v_w0, v_a_lora_up, v_a0, v_k_k, v_k_a, v_r_k, v_lnx_w, v_lnx_b, v_f_bias, v_q_norm_g, v_k_norm_g, v_w_out_a, v_w_out_b, v_w_out, v_final_norm_g):
    names = ("norm_g", "w_in", "shift_mu", "w_lora_up", "w0", "a_lora_up", "a0", "k_k", "k_a", "r_k", "lnx_w", "lnx_b",
             "f_bias", "q_norm_g", "k_norm_g", "w_out_a", "w_out_b", "w_out", "final_norm_g")
    weights = dict(zip(names, (norm_g, w_in, shift_mu, w_lora_up, w0, a_lora_up, a0, k_k, k_a, r_k, lnx_w, lnx_b,
                               f_bias, q_norm_g, k_norm_g, w_out_a, w_out_b, w_out, final_norm_g)))
    m_in = dict(zip(names, (m_norm_g, m_w_in, m_shift_mu, m_w_lora_up, m_w0, m_a_lora_up, m_a0, m_k_k, m_k_a, m_r_k,
                            m_lnx_w, m_lnx_b, m_f_bias, m_q_norm_g, m_k_norm_g, m_w_out_a, m_w_out_b, m_w_out,
                            m_final_norm_g)))
    v_in = dict(zip(names, (v_norm_g, v_w_in, v_shift_mu, v_w_lora_up, v_w0, v_a_lora_up, v_a0, v_k_k, v_k_a, v_r_k,
                            v_lnx_w, v_lnx_b, v_f_bias, v_q_norm_g, v_k_norm_g, v_w_out_a, v_w_out_b, v_w_out,
                            v_final_norm_g)))

    full = _gather_weights(weights)
    small = {n: weights[n].reshape(1, -1) for n in _SMALL_NAMES}
    loss, dx, grads = _local_step(x[0], loss_target[0], full, small)

    recv_w_in, recv_rest, recv_small = _exchange(_grad_slabs(grads, loss), "grad_exchange")
    zero = jnp.zeros((1,), F32)
    outs = [{}, {}, {}, {}]
    res = _sum_adamw(recv_w_in, w_in[0].T, m_w_in[0].T, v_w_in[0].T, W_IN_COL_TILE, "adamw_w_in", axis=1)
    for o, r in zip(outs, res):
        o["w_in"] = r.T[None]
    res = _sum_adamw(recv_rest, _pack_rest(weights), _pack_rest(m_in), _pack_rest(v_in), REST_ROW_TILE, "adamw_rest")
    for o, r in zip(outs, res):
        o.update(_unpack_rest(r, weights))
    res = _sum_adamw(recv_small, _pack_small(weights, zero), _pack_small(m_in, zero), _pack_small(v_in, zero),
                     SMALL_ROWS, "adamw_small")
    for o, r in zip(outs, res):
        o.update(_unpack_small(r, weights)[0])
    loss_sum = _unpack_small(res[0], weights)[1]
    return (loss_sum, dx[None], *[o[n] for o in outs for n in names])
```

```python
import functools
import math

import jax
import jax.numpy as jnp
from jax import lax
from jax.experimental import pallas as pl
from jax.experimental.pallas import tpu as pltpu

F32 = jnp.float32
BF16 = jnp.bfloat16
HI = lax.Precision.HIGHEST
MESH = pl.DeviceIdType.MESH

N_DEV = 8
D = 1024
H = 8
N = 64
DA = H * N
RANK = 64
NA = 4 * DA + 2 * RANK
NB = 4 * DA
NG = 2 * D
NF = 128
IN_COLS = NA + NB + H + NG
COLS_PER_DEV = IN_COLS // N_DEV
RMS_EPS = 1e-6
LNX_EPS = 64e-5
ATT_SCALE = N ** -0.5

ADAM_LR = 0.001
ADAM_B1 = 0.9
ADAM_B2 = 0.999
ADAM_EPS = 1e-08
ADAM_WD = 0.01
ADAM_STEP = 10

LANES = 128
WKV_CHUNK = 64
TOK_TILE = 256
HEAD_TILE = 128
ATT_TILE = 256
ATT_GROUPS = 4
VMEM_LIMIT = 56 * 1024 * 1024

REST_ROWS = (("w_out_a", DA), ("w_out_b", DA), ("w_out", D), ("w_lora_up", RANK * N // LANES),
             ("a_lora_up", RANK * N // LANES))
REST_TOTAL = sum(r for _, r in REST_ROWS)
SMALL = (("norm_g", D), ("final_norm_g", D), ("shift_mu", NA), ("w0", DA), ("a0", DA), ("k_k", DA), ("k_a", DA),
         ("r_k", DA), ("lnx_w", DA), ("lnx_b", DA), ("q_norm_g", N), ("k_norm_g", N), ("f_bias", H))
SMALL_ROWS = 64
W_IN_COL_TILE = 256
REST_ROW_TILE = 352


def _params(*sem):
    return pltpu.CompilerParams(dimension_semantics=sem or None, vmem_limit_bytes=VMEM_LIMIT)


def _bdot(a, b):
    return jnp.dot(a.astype(BF16), b.astype(BF16), preferred_element_type=F32)


def _bdot_nt(a, b):
    return lax.dot_general(a.astype(BF16), b.astype(BF16), (((1,), (1,)), ((), ())), preferred_element_type=F32)


def _bdot_tn(a, b):
    return lax.dot_general(a.astype(BF16), b.astype(BF16), (((0,), (0,)), ((), ())), preferred_element_type=F32)


def _sigmoid(x):
    return 1.0 / (1.0 + jnp.exp(-x))


def _softplus(x):
    return jnp.maximum(x, 0.0) + jnp.log(1.0 + jnp.exp(-jnp.abs(x)))


def _heads(ref, col0):
    return jnp.stack([ref[:, col0 + N * h:col0 + N * (h + 1)] for h in range(H)])


def _store_heads(ref, col0, val):
    for h in range(H):
        ref[:, col0 + N * h:col0 + N * (h + 1)] = val[h]


def _lerp(c, s, mu):
    return c + (s - c) * mu


def _rwkv_pre(rc, rs, kc, ks, vc, vs, gc, gs, wdc, wds, adc, ads,
              mu_r, mu_k, mu_v, mu_g, mu_wd, mu_ad, w_up, w0, a_up, a0, k_k, k_a):
    r = _lerp(rc, rs, mu_r)
    k = _lerp(kc, ks, mu_k)
    v = _lerp(vc, vs, mu_v)
    g = _lerp(gc, gs, mu_g)
    wd = _lerp(wdc, wds, mu_wd)
    ad = _lerp(adc, ads, mu_ad)
    t = wd.shape[0]
    bdims = (((2,), (1,)), ((0,), (0,)))
    tw = jnp.broadcast_to(jnp.tanh(wd).astype(BF16)[None], (H, t, RANK))
    z = w0 + lax.dot_general(tw, w_up.astype(BF16), bdims, preferred_element_type=F32)
    w_raw = -_softplus(-z) - 0.5
    lw = -jnp.exp(w_raw)
    row = lax.broadcasted_iota(jnp.int32, (t, t), 0)
    col = lax.broadcasted_iota(jnp.int32, (t, t), 1)
    same_chunk = ((row >= col) & (row // WKV_CHUNK == col // WKV_CHUNK)).astype(F32)
    cl = jnp.einsum("hts,hsn->htn", jnp.broadcast_to(same_chunk[None], (H, t, t)), lw, precision=HI,
                    preferred_element_type=F32)
    adb = jnp.broadcast_to(ad.astype(BF16)[None], (H, t, RANK))
    alr = _sigmoid(a0 + lax.dot_general(adb, a_up.astype(BF16), bdims, preferred_element_type=F32))
    kk = k * k_k
    kk = kk / jnp.maximum(jnp.sqrt(jnp.sum(kk * kk, axis=-1, keepdims=True)), 1e-12)
    k2 = k * (1.0 + (alr - 1.0) * k_a)
    return r, lw, cl, k2, v, -kk, kk * alr, g


_MM_DIMS = {"nn": (((2,), (1,)), ((0,), (0,))), "nt": (((2,), (2,)), ((0,), (0,))), "tn": (((1,), (1,)), ((0,), (0,)))}


def _split(x):
    hi = x.astype(BF16)
    return hi, (x - hi.astype(F32)).astype(BF16)


def _dot3(a, b, kind):
    ah, al = _split(a)
    bh, bl = _split(b)
    dot = functools.partial(lax.dot_general, dimension_numbers=_MM_DIMS[kind], preferred_element_type=F32)
    return dot(ah, bh) + (dot(ah, bl) + dot(al, bh))


@functools.partial(jax.custom_vjp, nondiff_argnums=(2,))
def _mm(a, b, kind):
    return _dot3(a, b, kind)


def _mm_fwd(a, b, kind):
    return _dot3(a, b, kind), (a, b)


def _mm_bwd(kind, res, ct):
    a, b = res
    if kind == "nn":
        return _mm(ct, b, "nt"), _mm(a, ct, "tn")
    if kind == "nt":
        return _mm(ct, b, "nn"), _mm(ct, a, "tn")
    return _mm(b, ct, "nt"), _mm(a, ct, "nn")


_mm.defvjp(_mm_fwd, _mm_bwd)


def _chunk_masks(c):
    row = lax.broadcasted_iota(jnp.int32, (c, c), 0)
    col = lax.broadcasted_iota(jnp.int32, (c, c), 1)
    return (row >= col)[None], (row > col)[None], (row == col).astype(F32)[None]


def _wkv_aab(lw, cl, a, b):
    _, strict, _ = _chunk_masks(a.shape[1])
    return jnp.where(strict, _mm(a * jnp.exp(cl - lw), b * jnp.exp(-cl), "nt"), 0.0)


def _tri_inverse(x):
    c = x.shape[1]
    p = _chunk_masks(c)[2] + x
    for _ in range(int(math.log2(c)) - 1):
        x = _mm(x, x, "nn")
        p = p + _mm(p, x, "nn")
    return p


def _wkv_apply(s0, r, lw, cl, k, v, a, b, p):
    c = r.shape[1]
    incl, strict, _ = _chunk_masks(c)
    gi = jnp.exp(-cl)
    left = jnp.concatenate([a * jnp.exp(cl - lw), r * jnp.exp(cl)], axis=1)
    right = jnp.concatenate([b * gi, k * gi], axis=1)
    m = _mm(left, right, "nt")
    z0 = _mm(left, s0, "nt")
    a_ak = jnp.where(strict, m[:, :c, c:], 0.0)
    row = lax.broadcasted_iota(jnp.int32, (c, 2 * c), 0)
    col = lax.broadcasted_iota(jnp.int32, (c, 2 * c), 1)
    a_r = jnp.where((row >= col % c)[None], m[:, c:, :], 0.0)
    sa = _mm(p, z0[:, :c] + _mm(a_ak, v, "nn"), "nn")
    sa_v = jnp.concatenate([sa, v], axis=1)
    y = z0[:, c:] + _mm(a_r, sa_v, "nn")
    s1 = (s0 + _mm(sa_v, right, "tn")) * jnp.exp(cl[:, c - 1:c, :])
    return y, s1


def _rwkv_post(y, r, k2, v, g, lnx_w, lnx_b, r_k):
    mean = jnp.mean(y, axis=-1, keepdims=True)
    yc = y - mean
    var = jnp.mean(yc * yc, axis=-1, keepdims=True)
    yn = yc * lax.rsqrt(var + LNX_EPS) * lnx_w + lnx_b
    bonus = jnp.sum(r * k2 * r_k, axis=-1, keepdims=True) * v
    return (yn + bonus) * (g * _sigmoid(g))


def _fox_pre(q, k, f, q_g, k_g, f_b):
    qn = q * lax.rsqrt(jnp.mean(q * q, axis=-1, keepdims=True) + RMS_EPS) * q_g
    kn = k * lax.rsqrt(jnp.mean(k * k, axis=-1, keepdims=True) + RMS_EPS) * k_g
    x = f + f_b
    return qn, kn, jnp.minimum(x, 0.0) - jnp.log(1.0 + jnp.exp(-jnp.abs(x)))


def _rms_fwd(x, g):
    s = x.shape[0]

    def body(x_ref, g_ref, h_ref):
        xv = x_ref[...]
        h_ref[...] = (xv * lax.rsqrt(jnp.mean(xv * xv, axis=-1, keepdims=True) + RMS_EPS) * g_ref[...]).astype(BF16)

    return pl.pallas_call(
        body, name="rms_fwd", grid=(s // TOK_TILE,),
        in_specs=[pl.BlockSpec((TOK_TILE, D), lambda i: (i, 0)), pl.BlockSpec((1, D), lambda i: (0, 0))],
        out_specs=pl.BlockSpec((TOK_TILE, D), lambda i: (i, 0)),
        out_shape=jax.ShapeDtypeStruct((s, D), BF16), compiler_params=_params("arbitrary"))(x, g)


def _proj(h, wt, name):
    s, n = h.shape[0], wt.shape[0]

    def body(h_ref, w_ref, o_ref):
        o_ref[...] = _bdot_nt(h_ref[...], w_ref[...])

    return pl.pallas_call(
        body, name=name, grid=(s // TOK_TILE,),
        in_specs=[pl.BlockSpec((TOK_TILE, D), lambda i: (i, 0)), pl.BlockSpec((n, D), lambda i: (0, 0))],
        out_specs=pl.BlockSpec((TOK_TILE, n), lambda i: (i, 0)),
        out_shape=jax.ShapeDtypeStruct((s, n), F32), compiler_params=_params("arbitrary"))(h, wt)


def _proj_wgrad(h, du, name):
    s, n = du.shape

    def body(h_ref, du_ref, o_ref):
        @pl.when(pl.program_id(0) == 0)
        def _():
            o_ref[...] = jnp.zeros_like(o_ref)

        o_ref[...] += _bdot_tn(du_ref[...], h_ref[...])

    return pl.pallas_call(
        body, name=name, grid=(s // TOK_TILE,),
        in_specs=[pl.BlockSpec((TOK_TILE, D), lambda i: (i, 0)), pl.BlockSpec((TOK_TILE, n), lambda i: (i, 0))],
        out_specs=pl.BlockSpec((n, D), lambda i: (0, 0)),
        out_shape=jax.ShapeDtypeStruct((n, D), F32), compiler_params=_params("arbitrary"))(h, du)


def _proj_xgrad(x, g, dx2, dus, ws):
    s = x.shape[0]
    tile = HEAD_TILE
    k = len(dus)

    def body(*refs):
        x_ref, g_ref, dx2_ref = refs[:3]
        du_refs, w_refs = refs[3:3 + k], refs[3 + k:3 + 2 * k]
        dx_ref, dg_ref = refs[3 + 2 * k:]

        @pl.when(pl.program_id(0) == 0)
        def _():
            dg_ref[...] = jnp.zeros_like(dg_ref)

        dh = _bdot(du_refs[0][...], w_refs[0][...])
        for du_ref, w_ref in zip(du_refs[1:], w_refs[1:]):
            dh += _bdot(du_ref[...], w_ref[...])
        xv = x_ref[...]
        rs = lax.rsqrt(jnp.mean(xv * xv, axis=-1, keepdims=True) + RMS_EPS)
        xn = xv * rs
        dg_ref[...] += jnp.sum(dh * xn, axis=0, keepdims=True)
        dxn = dh * g_ref[...]
        dx_ref[...] = rs * (dxn - xn * jnp.mean(dxn * xn, axis=-1, keepdims=True)) + dx2_ref[...]

    tok = lambda n: pl.BlockSpec((tile, n), lambda i: (i, 0))
    fixed = lambda a: pl.BlockSpec(a.shape, lambda i: (0,) * a.ndim)
    return pl.pallas_call(
        body, name="proj_xgrad", grid=(s // tile,),
        in_specs=[tok(D), fixed(g), tok(D)] + [tok(du.shape[1]) for du in dus] + [fixed(w) for w in ws],
        out_specs=[tok(D), pl.BlockSpec((1, D), lambda i: (0, 0))],
        out_shape=[jax.ShapeDtypeStruct((s, D), F32), jax.ShapeDtypeStruct((1, D), F32)],
        compiler_params=_params("arbitrary"))(x, g, dx2, *dus, *ws)


def _tail(x, target, ya, o, ub, ug, w_oa, w_ob, w_o, fg):
    s = x.shape[0]
    tile = TOK_TILE

    def body(x_ref, t_ref, ya_ref, o_ref, gb_ref, ug_ref, woa_ref, wob_ref, wo_ref, fg_ref,
             loss_ref, dfg_ref, dwo_ref, dwoa_ref, dwob_ref, dx2_ref, dya_ref, do_ref, dgb_ref, dug_ref):
        @pl.when(pl.program_id(0) == 0)
        def _():
            for r in (loss_ref, dfg_ref, dwo_ref, dwoa_ref, dwob_ref):
                r[...] = jnp.zeros_like(r)

        ya_v = ya_ref[...]
        gate_b = gb_ref[...]
        sg_b = _sigmoid(gate_b)
        silu_b = gate_b * sg_b
        o_v = jnp.concatenate([o_ref[h] for h in range(H)], axis=-1)
        yb_v = o_v * silu_b
        big_a = _bdot(ya_v, woa_ref[...])
        big_b = _bdot(yb_v, wob_ref[...])
        sa = _sigmoid(ug_ref[:, :D])
        sb = _sigmoid(ug_ref[:, D:])
        merged = sa * big_a + sb * big_b
        x2 = x_ref[...] + _bdot(merged, wo_ref[...])
        rs = lax.rsqrt(jnp.mean(x2 * x2, axis=-1, keepdims=True) + RMS_EPS)
        xn = x2 * rs
        err = xn * fg_ref[...] - t_ref[...]
        loss_ref[...] += (0.5 / D) * jnp.sum(err * err)
        dout = err * (1.0 / D)
        dfg_ref[...] += jnp.sum(dout * xn, axis=0, keepdims=True)
        dxn = dout * fg_ref[...]
        dx2 = rs * (dxn - xn * jnp.mean(dxn * xn, axis=-1, keepdims=True))
        dx2_ref[...] = dx2
        dwo_ref[...] += _bdot_tn(merged, dx2)
        dmerged = _bdot_nt(dx2, wo_ref[...])
        dbig_a = dmerged * sa
        dbig_b = dmerged * sb
        dug_ref[:, :D] = dmerged * big_a * sa * (1.0 - sa)
        dug_ref[:, D:] = dmerged * big_b * sb * (1.0 - sb)
        dwoa_ref[...] += _bdot_tn(ya_v, dbig_a)
        dwob_ref[...] += _bdot_tn(yb_v, dbig_b)
        dya_ref[...] = _bdot_nt(dbig_a, woa_ref[...])
        dyb = _bdot_nt(dbig_b, wob_ref[...])
        dgb_ref[...] = dyb * o_v * (sg_b * (1.0 + gate_b * (1.0 - sg_b)))
        _dov = dyb * silu_b
        for h in range(H):
            do_ref[h] = _dov[:, N * h:N * (h + 1)]

    tok = lambda n: pl.BlockSpec((tile, n), lambda i: (i, 0))
    hm = pl.BlockSpec((H, tile, N), lambda i: (0, i, 0))
    fixed = lambda shape: pl.BlockSpec(shape, lambda i: (0,) * len(shape))
    f32 = lambda *shape: jax.ShapeDtypeStruct(shape, F32)
    return pl.pallas_call(
        body, name="tail", grid=(s // tile,),
        in_specs=[tok(D), tok(D), tok(DA), hm, pl.BlockSpec((tile, DA), lambda i: (i, 3)), tok(NG),
                  fixed((DA, D)), fixed((DA, D)), fixed((D, D)), fixed((1, D))],
        out_specs=[fixed((1, 1)), fixed((1, D)), fixed((D, D)), fixed((DA, D)), fixed((DA, D)),
                   tok(D), tok(DA), hm, tok(DA), tok(NG)],
        out_shape=[f32(1, 1), f32(1, D), f32(D, D), f32(DA, D), f32(DA, D),
                   f32(s, D), f32(s, DA), f32(H, s, N), f32(s, DA), f32(s, NG)],
        compiler_params=_params("arbitrary"))(x, target, ya, o, ub, ug, w_oa, w_ob, w_o, fg)


_PRE_PARAM_SHAPES = ((H, 1, N),) * 4 + ((1, RANK),) * 2 + ((H, RANK, N), (H, 1, N), (H, RANK, N), (H, 1, N), (H, 1, N),
                                                              (H, 1, N))


def _pre_operands(ua_ref, prev_ref, first):
    cur = ua_ref[...]
    t = cur.shape[0]
    prev_row = jnp.where(first, 0.0, prev_ref[7:8, :])
    rows = lax.broadcasted_iota(jnp.int32, cur.shape, 0)
    sh = jnp.where(rows == 0, prev_row, pltpu.roll(cur, 1, axis=0))
    ops = []
    for c0 in (0, DA, 2 * DA, 3 * DA + 2 * RANK):
        ops.append(jnp.stack([cur[:, c0 + N * h:c0 + N * (h + 1)] for h in range(H)]))
        ops.append(jnp.stack([sh[:, c0 + N * h:c0 + N * (h + 1)] for h in range(H)]))
    for c0 in (3 * DA, 3 * DA + RANK):
        ops.append(cur[:, c0:c0 + RANK])
        ops.append(sh[:, c0:c0 + RANK])
    del t
    return ops


def _ua_specs(tile, order):
    blocks = tile // 8
    return [pl.BlockSpec((tile, NA), lambda i: (order(i), 0)),
            pl.BlockSpec((8, NA), lambda i: (jnp.maximum(order(i) * blocks - 1, 0), 0))]


def _rwkv_pre_fwd(ua, pre_params):
    s = ua.shape[0]
    tile = HEAD_TILE

    def body(ua_ref, prev_ref, *refs):
        p_refs, o_refs = refs[:len(pre_params)], refs[len(pre_params):]
        ops = _pre_operands(ua_ref, prev_ref, pl.program_id(0) == 0)
        outs = _rwkv_pre(*ops, *[p[...] for p in p_refs])
        for o_ref, val in zip(o_refs, outs):
            o_ref[...] = val

    hm = pl.BlockSpec((H, tile, N), lambda i: (0, i, 0))
    return pl.pallas_call(
        body, name="rwkv_pre_fwd", grid=(s // tile,),
        in_specs=_ua_specs(tile, lambda i: i) + [pl.BlockSpec(p.shape, lambda i, nd=p.ndim: (0,) * nd) for p in pre_params],
        out_specs=[hm] * 8, out_shape=[jax.ShapeDtypeStruct((H, s, N), F32)] * 8,
        compiler_params=_params("arbitrary"))(ua, ua, *pre_params)


def _rwkv_pre_bwd(ua, pre_params, cots):
    s = ua.shape[0]
    tile = HEAD_TILE
    nt = s // tile
    n_p = len(pre_params)

    def body(ua_ref, prev_ref, *refs):
        p_refs, c_refs = refs[:n_p], refs[n_p:n_p + 11]
        dua_ref = refs[n_p + 11]
        dp_refs = refs[n_p + 12:n_p + 12 + n_p]
        carry_ref = refs[-1]
        i = pl.program_id(0)

        @pl.when(i == 0)
        def _():
            carry_ref[...] = jnp.zeros_like(carry_ref)
            for r in dp_refs:
                r[...] = jnp.zeros_like(r)

        ops = _pre_operands(ua_ref, prev_ref, i == nt - 1)
        _, vjp = jax.vjp(_rwkv_pre, *ops, *[p[...] for p in p_refs])
        c = [r[...] for r in c_refs]
        grads = vjp((c[0] + c[1], c[2], c[3], c[4] + c[5], c[6] + c[7], c[8], c[9], c[10]))
        d_ops, d_par = grads[:12], grads[12:]
        for r, val in zip(dp_refs, d_par):
            r[...] += val
        d_cur = jnp.concatenate([d_ops[0][h] for h in range(H)] + [d_ops[2][h] for h in range(H)]
                                + [d_ops[4][h] for h in range(H)] + [d_ops[8], d_ops[10]]
                                + [d_ops[6][h] for h in range(H)], axis=-1)
        d_sh = jnp.concatenate([d_ops[1][h] for h in range(H)] + [d_ops[3][h] for h in range(H)]
                               + [d_ops[5][h] for h in range(H)] + [d_ops[9], d_ops[11]]
                               + [d_ops[7][h] for h in range(H)], axis=-1)
        rows = lax.broadcasted_iota(jnp.int32, d_sh.shape, 0)
        dua_ref[...] = d_cur + jnp.where(rows == tile - 1, carry_ref[...], pltpu.roll(d_sh, tile - 1, axis=0))
        carry_ref[...] = d_sh[0:1, :]

    rev = lambda i: nt - 1 - i
    hm = pl.BlockSpec((H, tile, N), lambda i: (0, rev(i), 0))
    fixed = [pl.BlockSpec(p.shape, lambda i, nd=p.ndim: (0,) * nd) for p in pre_params]
    return pl.pallas_call(
        body, name="rwkv_pre_bwd", grid=(nt,),
        in_specs=_ua_specs(tile, rev) + fixed + [hm] * 11,
        out_specs=[pl.BlockSpec((tile, NA), lambda i: (rev(i), 0))] + fixed,
        out_shape=[jax.ShapeDtypeStruct((s, NA), F32)] + [jax.ShapeDtypeStruct(p.shape, F32) for p in pre_params],
        scratch_shapes=[pltpu.VMEM((1, NA), F32)],
        compiler_params=_params("arbitrary"))(ua, ua, *pre_params, *cots)


def _wkv_fwd(seq):
    s = seq[0].shape[1]
    nc = s // WKV_CHUNK

    def body(r_ref, lw_ref, cl_ref, k_ref, v_ref, a_ref, b_ref, y_ref, ck_ref, p_ref, state):
        @pl.when(pl.program_id(0) == 0)
        def _():
            state[...] = jnp.zeros_like(state)

        s0 = state[...]
        ck_ref[0] = s0
        p = _tri_inverse(_wkv_aab(lw_ref[...], cl_ref[...], a_ref[...], b_ref[...]))
        p_ref[0] = p
        y, s1 = _wkv_apply(s0, r_ref[...], lw_ref[...], cl_ref[...], k_ref[...], v_ref[...], a_ref[...], b_ref[...], p)
        y_ref[...] = y
        state[...] = s1

    hm = pl.BlockSpec((H, WKV_CHUNK, N), lambda c: (0, c, 0))
    per_chunk = lambda m: pl.BlockSpec((1, H, m, m), lambda c: (c, 0, 0, 0))
    return pl.pallas_call(
        body, name="wkv_fwd", grid=(nc,), in_specs=[hm] * 7,
        out_specs=[hm, per_chunk(N), per_chunk(WKV_CHUNK)],
        out_shape=[jax.ShapeDtypeStruct((H, s, N), F32), jax.ShapeDtypeStruct((nc, H, N, N), F32),
                   jax.ShapeDtypeStruct((nc, H, WKV_CHUNK, WKV_CHUNK), F32)],
        scratch_shapes=[pltpu.VMEM((H, N, N), F32)], compiler_params=_params("arbitrary"))(*seq)


def _wkv_bwd(seq, ckpt, pinv, dy):
    s = seq[0].shape[1]
    nc = s // WKV_CHUNK

    def body(r_ref, lw_ref, cl_ref, k_ref, v_ref, a_ref, b_ref, ck_ref, p_ref, dy_ref, *refs):
        d_refs, dstate = refs[:7], refs[7]

        @pl.when(pl.program_id(0) == 0)
        def _():
            dstate[...] = jnp.zeros_like(dstate)

        p = p_ref[0]
        lw, cl, a, b = lw_ref[...], cl_ref[...], a_ref[...], b_ref[...]
        _, vjp = jax.vjp(_wkv_apply, ck_ref[0], r_ref[...], lw, cl, k_ref[...], v_ref[...], a, b, p)
        ds0, dr, dlw, dcl, dk, dv, da, db, dp = vjp((dy_ref[...], dstate[...]))
        dstate[...] = ds0
        _, vjp_x = jax.vjp(_wkv_aab, lw, cl, a, b)
        dlw2, dcl2, da2, db2 = vjp_x(_mm(_mm(p, dp, "tn"), p, "nt"))
        for d_ref, val in zip(d_refs, (dr, dlw + dlw2, dcl + dcl2, dk, dv, da + da2, db + db2)):
            d_ref[...] = val

    hm = pl.BlockSpec((H, WKV_CHUNK, N), lambda c: (0, nc - 1 - c, 0))
    per_chunk = lambda m: pl.BlockSpec((1, H, m, m), lambda c: (nc - 1 - c, 0, 0, 0))
    return pl.pallas_call(
        body, name="wkv_bwd", grid=(nc,),
        in_specs=[hm] * 7 + [per_chunk(N), per_chunk(WKV_CHUNK), hm],
        out_specs=[hm] * 7, out_shape=[jax.ShapeDtypeStruct((H, s, N), F32)] * 7,
        scratch_shapes=[pltpu.VMEM((H, N, N), F32)], compiler_params=_params("arbitrary"))(*seq, ckpt, pinv, dy)


def _rwkv_post_fwd(y, r, k2, v, g, post_params):
    s = y.shape[1]
    tile = HEAD_TILE

    def body(y_ref, r_ref, k_ref, v_ref, g_ref, w_ref, b_ref, rk_ref, o_ref):
        out = _rwkv_post(y_ref[...], r_ref[...], k_ref[...], v_ref[...], g_ref[...], w_ref[...], b_ref[...],
                         rk_ref[...])
        o_ref[...] = jnp.concatenate([out[h] for h in range(H)], axis=-1)

    hm = pl.BlockSpec((H, tile, N), lambda i: (0, i, 0))
    par = pl.BlockSpec((H, 1, N), lambda i: (0, 0, 0))
    return pl.pallas_call(
        body, name="rwkv_post_fwd", grid=(s // tile,), in_specs=[hm] * 5 + [par] * 3,
        out_specs=pl.BlockSpec((tile, DA), lambda i: (i, 0)), out_shape=jax.ShapeDtypeStruct((s, DA), F32),
        compiler_params=_params("arbitrary"))(y, r, k2, v, g, *post_params)


def _rwkv_post_bwd(y, r, k2, v, g, post_params, dya):
    s = y.shape[1]
    tile = HEAD_TILE

    def body(y_ref, r_ref, k_ref, v_ref, g_ref, w_ref, b_ref, rk_ref, dya_ref, *d_refs):
        @pl.when(pl.program_id(0) == 0)
        def _():
            for ref in d_refs[5:]:
                ref[...] = jnp.zeros_like(ref)

        _, vjp = jax.vjp(_rwkv_post, y_ref[...], r_ref[...], k_ref[...], v_ref[...], g_ref[...], w_ref[...],
                         b_ref[...], rk_ref[...])
        grads = vjp(jnp.stack([dya_ref[:, N * h:N * (h + 1)] for h in range(H)]))
        for ref, val in zip(d_refs[:5], grads[:5]):
            ref[...] = val
        for ref, val in zip(d_refs[5:], grads[5:]):
            ref[...] += val

    hm = pl.BlockSpec((H, tile, N), lambda i: (0, i, 0))
    par = pl.BlockSpec((H, 1, N), lambda i: (0, 0, 0))
    return pl.pallas_call(
        body, name="rwkv_post_bwd", grid=(s // tile,),
        in_specs=[hm] * 5 + [par] * 3 + [pl.BlockSpec((tile, DA), lambda i: (i, 0))],
        out_specs=[hm] * 5 + [par] * 3,
        out_shape=[jax.ShapeDtypeStruct((H, s, N), F32)] * 5 + [jax.ShapeDtypeStruct((H, 1, N), F32)] * 3,
        compiler_params=_params("arbitrary"))(y, r, k2, v, g, *post_params, dya)


def _tri(t):
    return (lax.broadcasted_iota(jnp.int32, (t, t), 0) >= lax.broadcasted_iota(jnp.int32, (t, t), 1)).astype(F32)


def _fox_pre_fwd(ub, uf, q_g, k_g, f_b):
    s = ub.shape[0]
    tile = HEAD_TILE

    def body(ub_ref, uf_ref, qg_ref, kg_ref, fb_ref, q_ref, k_ref, v_ref, cum_ref, carry):
        @pl.when(pl.program_id(0) == 0)
        def _():
            carry[...] = jnp.zeros_like(carry)

        qn, kn, logf = _fox_pre(_heads(ub_ref, 0), _heads(ub_ref, DA), uf_ref[...], qg_ref[...], kg_ref[...],
                                fb_ref[...])
        q_ref[...] = qn
        k_ref[...] = kn
        v_ref[...] = _heads(ub_ref, 2 * DA)
        cum = jnp.dot(_tri(tile), logf, precision=HI, preferred_element_type=F32) + carry[...]
        cum_ref[...] = cum
        carry[...] = cum[tile - 1:tile, :]

    hm = pl.BlockSpec((H, tile, N), lambda i: (0, i, 0))
    fixed = lambda shape: pl.BlockSpec(shape, lambda i: (0,) * len(shape))
    return pl.pallas_call(
        body, name="fox_pre_fwd", grid=(s // tile,),
        in_specs=[pl.BlockSpec((tile, NB), lambda i: (i, 0)), pl.BlockSpec((tile, NF), lambda i: (i, 0)),
                  fixed((1, 1, N)), fixed((1, 1, N)), fixed((1, NF))],
        out_specs=[hm] * 3 + [pl.BlockSpec((tile, NF), lambda i: (i, 0))],
        out_shape=[jax.ShapeDtypeStruct((H, s, N), F32)] * 3 + [jax.ShapeDtypeStruct((s, NF), F32)],
        scratch_shapes=[pltpu.VMEM((1, NF), F32)], compiler_params=_params("arbitrary"))(ub, uf, q_g, k_g, f_b)


def _fox_pre_bwd(ub, uf, q_g, k_g, f_b, dqn, dkn, dvf, dgate, dcum_q, dcum_k):
    s = ub.shape[0]
    tile = HEAD_TILE
    nt = s // tile

    def body(ub_ref, uf_ref, qg_ref, kg_ref, fb_ref, dq_ref, dk_ref, dv_ref, dgate_ref, dcq_ref, dck_ref,
             dub_ref, duf_ref, dqg_ref, dkg_ref, dfb_ref, carry):
        @pl.when(pl.program_id(0) == 0)
        def _():
            carry[...] = jnp.zeros_like(carry)
            for ref in (dqg_ref, dkg_ref, dfb_ref):
                ref[...] = jnp.zeros_like(ref)

        dcum = dcq_ref[...] + dck_ref[...]
        dlogf = lax.dot_general(_tri(tile), dcum, (((0,), (0,)), ((), ())), precision=HI,
                                preferred_element_type=F32) + carry[...]
        carry[...] = dlogf[0:1, :]
        _, vjp = jax.vjp(_fox_pre, _heads(ub_ref, 0), _heads(ub_ref, DA), uf_ref[...], qg_ref[...], kg_ref[...],
                         fb_ref[...])
        d_q, d_k, d_f, d_qg, d_kg, d_fb = vjp((dq_ref[...], dk_ref[...], dlogf))
        _store_heads(dub_ref, 0, d_q)
        _store_heads(dub_ref, DA, d_k)
        _store_heads(dub_ref, 2 * DA, dv_ref[...])
        dub_ref[:, 3 * DA:] = dgate_ref[...]
        duf_ref[...] = d_f
        dqg_ref[...] += d_qg
        dkg_ref[...] += d_kg
        dfb_ref[...] += d_fb

    rev = lambda i: nt - 1 - i
    hm = pl.BlockSpec((H, tile, N), lambda i: (0, rev(i), 0))
    tok = lambda n: pl.BlockSpec((tile, n), lambda i: (rev(i), 0))
    fixed = lambda shape: pl.BlockSpec(shape, lambda i: (0,) * len(shape))
    return pl.pallas_call(
        body, name="fox_pre_bwd", grid=(nt,),
        in_specs=[tok(NB), tok(NF), fixed((1, 1, N)), fixed((1, 1, N)), fixed((1, NF)), hm, hm, hm, tok(DA), tok(NF),
                  tok(NF)],
        out_specs=[tok(NB), tok(NF), fixed((1, 1, N)), fixed((1, 1, N)), fixed((1, NF))],
        out_shape=[jax.ShapeDtypeStruct((s, NB), F32), jax.ShapeDtypeStruct((s, NF), F32),
                   jax.ShapeDtypeStruct((1, 1, N), F32), jax.ShapeDtypeStruct((1, 1, N), F32),
                   jax.ShapeDtypeStruct((1, NF), F32)],
        scratch_shapes=[pltpu.VMEM((1, NF), F32)],
        compiler_params=_params("arbitrary"))(ub, uf, q_g, k_g, f_b, dqn, dkn, dvf, dgate, dcum_q, dcum_k)


def _att_groups(s):
    blocks = s // ATT_TILE
    per = max(1, blocks // ATT_GROUPS)
    return per, blocks // per


def _att_logits(q_bf, k, cq, ck, qi):
    tq, sk = q_bf.shape[0], k.shape[0]
    logits = _bdot_nt(q_bf, k) * ATT_SCALE + cq - ck
    rows = qi * tq + lax.broadcasted_iota(jnp.int32, (tq, sk), 0)
    mask = rows >= lax.broadcasted_iota(jnp.int32, (tq, sk), 1)
    return jnp.where(mask, logits, -1e30), mask


def _fox_attn_fwd(q, k, v, cum_q, cum_k):
    s = q.shape[1]
    t = ATT_TILE
    per, groups = _att_groups(s)

    def body(q_ref, k_ref, v_ref, cq_ref, ck_ref, o_ref, lse_ref):
        qi = pl.program_id(1)
        for g in range(groups):
            @pl.when(qi // per == g)
            def _(n=(g + 1) * per * t):
                logits, _ = _att_logits(q_ref[0].astype(BF16), k_ref[0, :n, :], cq_ref[0], ck_ref[0, :, :n], qi)
                m = jnp.max(logits, axis=-1, keepdims=True)
                p = jnp.exp(logits - m)
                l = jnp.sum(p, axis=-1, keepdims=True)
                o_ref[0] = _bdot(p, v_ref[0, :n, :]) / l
                lse_ref[0] = m + jnp.log(l)

    qb = pl.BlockSpec((1, t, N), lambda h, i: (h, i, 0))
    kb = pl.BlockSpec((1, s, N), lambda h, i: (h, 0, 0))
    return pl.pallas_call(
        body, name="fox_attn_fwd", grid=(H, s // t),
        in_specs=[qb, kb, kb, pl.BlockSpec((1, t, 1), lambda h, i: (h, i, 0)),
                  pl.BlockSpec((1, 1, s), lambda h, i: (h, 0, 0))],
        out_specs=[qb, pl.BlockSpec((1, t, 1), lambda h, i: (h, i, 0))],
        out_shape=[jax.ShapeDtypeStruct((H, s, N), F32), jax.ShapeDtypeStruct((H, s, 1), F32)],
        compiler_params=_params("arbitrary", "arbitrary"))(q, k, v, cum_q, cum_k)


def _fox_attn_bwd(q, k, v, cum_q, cum_k, o, lse, do):
    s = q.shape[1]
    t = ATT_TILE
    per, groups = _att_groups(s)

    def body(q_ref, k_ref, v_ref, cq_ref, ck_ref, o_ref, lse_ref, do_ref, dq_ref, dk_ref, dv_ref, dcq_ref, dck_ref):
        qi = pl.program_id(1)

        @pl.when(qi == 0)
        def _():
            for ref in (dk_ref, dv_ref, dck_ref):
                ref[...] = jnp.zeros_like(ref)

        for g in range(groups):
            @pl.when(qi // per == g)
            def _(n=(g + 1) * per * t):
                q_bf, do_bf = q_ref[0].astype(BF16), do_ref[0].astype(BF16)
                kv = k_ref[0, :n, :]
                logits, mask = _att_logits(q_bf, kv, cq_ref[0], ck_ref[0, :, :n], qi)
                p = jnp.where(mask, jnp.exp(logits - lse_ref[0]), 0.0)
                delta = jnp.sum(do_ref[0] * o_ref[0], axis=-1, keepdims=True)
                ds = p * (_bdot_nt(do_bf, v_ref[0, :n, :]) - delta)
                dq_ref[0] = _bdot(ds, kv) * ATT_SCALE
                dk_ref[0, :n, :] += _bdot_tn(ds, q_bf) * ATT_SCALE
                dv_ref[0, :n, :] += _bdot_tn(p, do_bf)
                dcq_ref[0] = jnp.sum(ds, axis=-1, keepdims=True)
                dck_ref[0, :, :n] -= jnp.sum(ds, axis=0, keepdims=True)

    qb = pl.BlockSpec((1, t, N), lambda h, i: (h, i, 0))
    kb = pl.BlockSpec((1, s, N), lambda h, i: (h, 0, 0))
    cqb = pl.BlockSpec((1, t, 1), lambda h, i: (h, i, 0))
    ckb = pl.BlockSpec((1, 1, s), lambda h, i: (h, 0, 0))
    f32 = lambda *shape: jax.ShapeDtypeStruct(shape, F32)
    return pl.pallas_call(
        body, name="fox_attn_bwd", grid=(H, s // t),
        in_specs=[qb, kb, kb, cqb, ckb, qb, cqb, qb], out_specs=[qb, kb, kb, cqb, ckb],
        out_shape=[f32(H, s, N), f32(H, s, N), f32(H, s, N), f32(H, s, 1), f32(H, 1, s)],
        compiler_params=_params("arbitrary", "arbitrary"))(q, k, v, cum_q, cum_k, o, lse, do)


def _head_param(p):
    return p.reshape(H, 1, N)


def _local_step(x, target, w, p):
    mu = p["shift_mu"]
    pre_params = (_head_param(mu[:, 0:DA]), _head_param(mu[:, DA:2 * DA]), _head_param(mu[:, 2 * DA:3 * DA]),
                  _head_param(mu[:, 3 * DA + 2 * RANK:]), mu[:, 3 * DA:3 * DA + RANK],
                  mu[:, 3 * DA + RANK:3 * DA + 2 * RANK],
                  w["w_lora_up"].astype(F32), _head_param(p["w0"]), w["a_lora_up"].astype(F32), _head_param(p["a0"]),
                  _head_param(p["k_k"]), _head_param(p["k_a"]))
    post_params = (_head_param(p["lnx_w"]), _head_param(p["lnx_b"]), _head_param(p["r_k"]))
    q_g, k_g = p["q_norm_g"].reshape(1, 1, N), p["k_norm_g"].reshape(1, 1, N)
    f_b = jnp.pad(p["f_bias"], ((0, 0), (0, NF - H)))
    fg = p["final_norm_g"].reshape(1, D)

    h = _rms_fwd(x, p["norm_g"])
    ua = _proj(h, w["in_a"], "proj_a")
    ub = _proj(h, w["in_b"], "proj_b")
    ug = _proj(h, w["in_g"], "proj_g")
    uf = _proj(h, w["in_f"], "proj_f")
    r, lw, cl, k2, v, av, bv, gg = _rwkv_pre_fwd(ua, pre_params)
    y, ckpt, pinv = _wkv_fwd((r, lw, cl, k2, v, av, bv))
    ya = _rwkv_post_fwd(y, r, k2, v, gg, post_params)
    qn, kn, vf, cum = _fox_pre_fwd(ub, uf, q_g, k_g, f_b)
    cum_t = cum[:, :H].T
    cum_q, cum_k = cum_t[:, :, None], cum_t[:, None, :]
    o, lse = _fox_attn_fwd(qn, kn, vf, cum_q, cum_k)

    (loss, dfg, dwo, dwoa, dwob, dx2, dya, do, dgate_b, dug) = _tail(
        x, target, ya, o, ub, ug, w["w_out_a"], w["w_out_b"], w["w_out"], fg)
    dqn, dkn, dvf, dcq, dck = _fox_attn_bwd(qn, kn, vf, cum_q, cum_k, o, lse, do)
    pad_f = lambda a: jnp.pad(a.T, ((0, 0), (0, NF - H)))
    dub, duf, dqg, dkg, dfb = _fox_pre_bwd(ub, uf, q_g, k_g, f_b, dqn, dkn, dvf, dgate_b,
                                           pad_f(dcq[:, :, 0]), pad_f(dck.reshape(H, -1)))
    dy, dr_p, dk_p, dv_p, dgg, dlnw, dlnb, drk = _rwkv_post_bwd(y, r, k2, v, gg, post_params, dya)
    dr_s, dlw, dcl, dk_s, dv_s, dav, dbv = _wkv_bwd((r, lw, cl, k2, v, av, bv), ckpt, pinv, dy)
    pre_out = _rwkv_pre_bwd(ua, pre_params, (dr_s, dr_p, dlw, dcl, dk_s, dk_p, dv_s, dv_p, dav, dbv, dgg))
    dua, dpre = pre_out[0], pre_out[1:]
    dws = [_proj_wgrad(h, du, name) for du, name in
           ((dua, "wgrad_a"), (dub, "wgrad_b"), (dug, "wgrad_g"), (duf, "wgrad_f"))]
    dx, dng = _proj_xgrad(x, p["norm_g"], dx2, (dua, dub, dug, duf), (w["in_a"], w["in_b"], w["in_g"], w["in_f"]))

    flat = lambda a: a.reshape(1, -1)
    dmu = jnp.concatenate([flat(dpre[0]), flat(dpre[1]), flat(dpre[2]), dpre[4], dpre[5], flat(dpre[3])], axis=1)
    grads = {
        "w_in_segments": (dws[0], dws[1], dws[2], dws[3]),
        "w_out_a": dwoa, "w_out_b": dwob, "w_out": dwo, "w_lora_up": dpre[6], "a_lora_up": dpre[8],
        "norm_g": dng, "final_norm_g": dfg, "shift_mu": dmu, "w0": flat(dpre[7]), "a0": flat(dpre[9]),
        "k_k": flat(dpre[10]), "k_a": flat(dpre[11]), "r_k": flat(drk), "lnx_w": flat(dlnw), "lnx_b": flat(dlnb),
        "q_norm_g": flat(dqg), "k_norm_g": flat(dkg), "f_bias": dfb[:, :H],
    }
    return loss, dx, grads


def _position():
    return lax.axis_index("x"), lax.axis_index("y"), lax.axis_index("c")


def _hbm_specs(n):
    return [pl.BlockSpec(memory_space=pl.ANY)] * n


def _all_gather(blocks, name):
    n = len(blocks)

    def body(*refs):
        x_refs, out_refs = refs[:n], refs[n:2 * n]
        send_sems, recv_sems, local_sems = refs[2 * n:]
        x, y, c = _position()
        me, sibling = (x, y, c), (x, y, 1 - c)
        chips = [(1 - x, y), (x, 1 - y), (1 - x, 1 - y)]

        def copy(a, k, blk, to, own=False):
            dst = out_refs[a].at[4 * blk[0] + 2 * blk[1] + blk[2]]
            return pltpu.make_async_remote_copy(
                src_ref=x_refs[a] if own else dst, dst_ref=dst, send_sem=send_sems.at[7 * a + k],
                recv_sem=recv_sems.at[7 * a + k], device_id=to, device_id_type=MESH)

        mine = [pltpu.make_async_copy(x_refs[a], out_refs[a].at[4 * x + 2 * y + c], local_sems.at[a]) for a in range(n)]
        for cp in mine:
            cp.start()
        first = []
        for a in range(n):
            first.append(copy(a, 0, me, sibling, own=True))
            first += [copy(a, 1 + j, me, (*chip, c), own=True) for j, chip in enumerate(chips)]
        for cp in first:
            cp.start()
        passed = []
        for j, chip in enumerate(chips):
            for a in range(n):
                copy(a, 1 + j, (*chip, c), me).wait_recv()
                passed.append(copy(a, 4 + j, (*chip, c), sibling))
                passed[-1].start()
        for a in range(n):
            copy(a, 0, sibling, me).wait_recv()
        for j, chip in enumerate(chips):
            for a in range(n):
                copy(a, 4 + j, (*chip, 1 - c), me).wait_recv()
        for cp in first + passed:
            cp.wait_send()
        for cp in mine:
            cp.wait()

    return pl.pallas_call(
        body, name=name, out_shape=[jax.ShapeDtypeStruct((N_DEV,) + b.shape, b.dtype) for b in blocks],
        in_specs=_hbm_specs(n), out_specs=_hbm_specs(n),
        scratch_shapes=[pltpu.SemaphoreType.DMA((7 * n,)), pltpu.SemaphoreType.DMA((7 * n,)),
                        pltpu.SemaphoreType.DMA((n,))],
    )(*blocks)


def _exchange(slabs, name):
    n = len(slabs)

    def body(*refs):
        s_refs, out_refs = refs[:n], refs[n:2 * n]
        send_sems, recv_sems, local_sems = refs[2 * n:]
        x, y, c = _position()
        me = 4 * x + 2 * y + c
        mine = [pltpu.make_async_copy(s_refs[a].at[me], out_refs[a].at[me], local_sems.at[a]) for a in range(n)]
        for cp in mine:
            cp.start()
        sends, recvs = [], []
        for m in range(1, N_DEV):
            px, py, pc = x ^ (m >> 2), y ^ ((m >> 1) & 1), c ^ (m & 1)
            peer = 4 * px + 2 * py + pc
            for a in range(n):
                sem = dict(send_sem=send_sems.at[7 * a + m - 1], recv_sem=recv_sems.at[7 * a + m - 1],
                           device_id=(px, py, pc), device_id_type=MESH)
                sends.append(pltpu.make_async_remote_copy(src_ref=s_refs[a].at[peer], dst_ref=out_refs[a].at[me], **sem))
                recvs.append(pltpu.make_async_remote_copy(src_ref=s_refs[a].at[me], dst_ref=out_refs[a].at[peer], **sem))
        for cp in sends:
            cp.start()
        for cp in recvs:
            cp.wait_recv()
        for cp in sends:
            cp.wait_send()
        for cp in mine:
            cp.wait()

    return pl.pallas_call(
        body, name=name, out_shape=[jax.ShapeDtypeStruct(s.shape, s.dtype) for s in slabs],
        in_specs=_hbm_specs(n), out_specs=_hbm_specs(n),
        scratch_shapes=[pltpu.SemaphoreType.DMA((7 * n,)), pltpu.SemaphoreType.DMA((7 * n,)),
                        pltpu.SemaphoreType.DMA((n,))],
    )(*slabs)


def _sum_adamw(recv, w, m, v, tile, name, axis=0):
    rows, cols = w.shape

    def body(r_ref, w_ref, m_ref, v_ref, g_ref, d_ref, mo_ref, vo_ref):
        g = r_ref[0].astype(F32)
        for k in range(1, N_DEV):
            g = g + r_ref[k].astype(F32)
        m_new = ADAM_B1 * m_ref[...] + (1.0 - ADAM_B1) * g
        v_new = ADAM_B2 * v_ref[...] + (1.0 - ADAM_B2) * (g * g)
        m_hat = m_new / (1.0 - ADAM_B1 ** ADAM_STEP)
        v_hat = v_new / (1.0 - ADAM_B2 ** ADAM_STEP)
        g_ref[...] = g
        d_ref[...] = -ADAM_LR * (m_hat / (jnp.sqrt(v_hat) + ADAM_EPS) + ADAM_WD * w_ref[...])
        mo_ref[...] = m_new
        vo_ref[...] = v_new

    if axis == 0:
        blk = pl.BlockSpec((tile, cols), lambda i: (i, 0))
        rblk = pl.BlockSpec((N_DEV, tile, cols), lambda i: (0, i, 0))
    else:
        blk = pl.BlockSpec((rows, tile), lambda i: (0, i))
        rblk = pl.BlockSpec((N_DEV, rows, tile), lambda i: (0, 0, i))
    return pl.pallas_call(
        body, name=name, grid=((rows if axis == 0 else cols) // tile,),
        in_specs=[rblk, blk, blk, blk],
        out_specs=[blk] * 4, out_shape=[jax.ShapeDtypeStruct((rows, cols), F32)] * 4,
        compiler_params=_params("arbitrary"))(recv, w, m, v)


_REST_NAMES = tuple(n for n, _ in REST_ROWS)
_SMALL_NAMES = tuple(n for n, _ in SMALL)
_WT_SEGMENTS = ((0, NA), (NA, NB), (NA + NB + H, NG), (NA + NB, H))


def _split_wt(gathered):
    tile = W_IN_COL_TILE

    def body(g_ref, *o_refs):
        full = jnp.concatenate([g_ref[j] for j in range(N_DEV)], axis=0)
        for o_ref, (row, n) in zip(o_refs, _WT_SEGMENTS):
            seg = full[row:row + n]
            if n < o_ref.shape[0]:
                seg = jnp.concatenate([seg, jnp.zeros((o_ref.shape[0] - n, tile), BF16)], axis=0)
            o_ref[...] = seg

    sizes = (NA, NB, NG, NF)
    return pl.pallas_call(
        body, name="split_wt", grid=(D // tile,),
        in_specs=[pl.BlockSpec((N_DEV, COLS_PER_DEV, tile), lambda i: (0, 0, i))],
        out_specs=[pl.BlockSpec((n, tile), lambda i: (0, i)) for n in sizes],
        out_shape=[jax.ShapeDtypeStruct((n, D), BF16) for n in sizes],
        compiler_params=_params("arbitrary"))(gathered)


def _slab_wt_grad(segments):
    tile = W_IN_COL_TILE

    def body(*refs):
        seg_refs, o_ref = refs[:4], refs[4]
        for j in range(N_DEV):
            lo, hi = COLS_PER_DEV * j, COLS_PER_DEV * (j + 1)
            parts = []
            for ref, (row, n) in sorted(zip(seg_refs, _WT_SEGMENTS), key=lambda t: t[1][0]):
                first, last = max(lo, row), min(hi, row + n)
                if first < last:
                    parts.append(ref[first - row:last - row, :])
            o_ref[j] = (parts[0] if len(parts) == 1 else jnp.concatenate(parts, axis=0)).astype(BF16)

    return pl.pallas_call(
        body, name="slab_wt_grad", grid=(D // tile,),
        in_specs=[pl.BlockSpec((s.shape[0], tile), lambda i: (0, i)) for s in segments],
        out_specs=pl.BlockSpec((N_DEV, COLS_PER_DEV, tile), lambda i: (0, 0, i)),
        out_shape=jax.ShapeDtypeStruct((N_DEV, COLS_PER_DEV, D), BF16),
        compiler_params=_params("arbitrary"))(*segments)


def _pack_rest(t):
    return jnp.concatenate([t[n].reshape(-1, LANES) for n in _REST_NAMES], axis=0)


def _unpack_rest(packed, like):
    out, row = {}, 0
    for n, r in REST_ROWS:
        out[n] = packed[row:row + r].reshape(like[n].shape)
        row += r
    return out


def _pack_small(t, last):
    flat = jnp.concatenate([t[n].reshape(-1) for n in _SMALL_NAMES] + [last.reshape(-1)])
    return jnp.pad(flat, (0, SMALL_ROWS * LANES - flat.shape[0])).reshape(SMALL_ROWS, LANES)


def _unpack_small(packed, like):
    out, flat, off = {}, packed.reshape(-1), 0
    for n, size in SMALL:
        out[n] = flat[off:off + size].reshape(like[n].shape)
        off += size
    return out, flat[off]


def _gather_weights(t):
    w_in_all, rest_all = _all_gather([t["w_in"][0].T.astype(BF16), _pack_rest(t).astype(BF16)], "weight_gather")
    in_a, in_b, in_g, in_f = _split_wt(w_in_all)
    parts, row = {}, 0
    for n, r in REST_ROWS:
        parts[n] = rest_all[:, row:row + r]
        row += r
    by_cols = lambda a: jnp.moveaxis(a, 0, 1).reshape(a.shape[1], -1)
    return {
        "in_a": in_a, "in_b": in_b, "in_g": in_g, "in_f": in_f,
        "w_out_a": by_cols(parts["w_out_a"]), "w_out_b": by_cols(parts["w_out_b"]),
        "w_out": parts["w_out"].reshape(D, D),
        "w_lora_up": parts["w_lora_up"].reshape(H, RANK, N), "a_lora_up": parts["a_lora_up"].reshape(H, RANK, N),
    }


def _grad_slabs(g, loss):
    by_cols = lambda a: jnp.moveaxis(a.reshape(a.shape[0], N_DEV, -1), 1, 0)
    w_in_t = _slab_wt_grad(g["w_in_segments"])
    rest = jnp.concatenate([by_cols(g["w_out_a"]), by_cols(g["w_out_b"]), g["w_out"].reshape(N_DEV, -1, LANES),
                            g["w_lora_up"].reshape(N_DEV, -1, LANES), g["a_lora_up"].reshape(N_DEV, -1, LANES)],
                           axis=1).astype(BF16)
    small = jnp.broadcast_to(_pack_small(g, loss)[None], (N_DEV, SMALL_ROWS, LANES))
    return [w_in_t, rest, small]


def kernel(x, norm_g, w_in, shift_mu, w_lora_up, w0, a_lora_up, a0, k_k, k_a, r_k, lnx_w, lnx_b, f_bias, q_norm_g, k_norm_g, w_out_a, w_out_b, w_out, final_norm_g, loss_target, m_norm_g, m_w_in, m_shift_mu, m_w_lora_up, m_w0, m_a_lora_up, m_a0, m_k_k, m_k_a, m_r_k, m_lnx_w, m_lnx_b, m_f_bias, m_q_norm_g, m_k_norm_g, m_w_out_a, m_w_out_b, m_w_out, m_final_norm_g, v_norm_g, v_w_in, v_shift_mu, v_w_lora_up, v_w0, v_a_lora_up, v_a0, v_k_k, v_k_a, v_r_k, v_lnx_w, v_lnx_b, v_f_bias, v_q_norm_g, v_k_norm_g, v_w_out_a, v_w_out_b, v_w_out, v_final_norm_g):
    names = ("norm_g", "w_in", "shift_mu", "w_lora_up", "w0", "a_lora_up", "a0", "k_k", "k_a", "r_k", "lnx_w", "lnx_b",
             "f_bias", "q_norm_g", "k_norm_g", "w_out_a", "w_out_b", "w_out", "final_norm_g")
    weights = dict(zip(names, (norm_g, w_in, shift_mu, w_lora_up, w0, a_lora_up, a0, k_k, k_a, r_k, lnx_w, lnx_b,
                               f_bias, q_norm_g, k_norm_g, w_out_a, w_out_b, w_out, final_norm_g)))
    m_in = dict(zip(names, (m_norm_g, m_w_in, m_shift_mu, m_w_lora_up, m_w0, m_a_lora_up, m_a0, m_k_k, m_k_a, m_r_k,
                            m_lnx_w, m_lnx_b, m_f_bias, m_q_norm_g, m_k_norm_g, m_w_out_a, m_w_out_b, m_w_out,
                            m_final_norm_g)))
    v_in = dict(zip(names, (v_norm_g, v_w_in, v_shift_mu, v_w_lora_up, v_w0, v_a_lora_up, v_a0, v_k_k, v_k_a, v_r_k,
                            v_lnx_w, v_lnx_b, v_f_bias, v_q_norm_g, v_k_norm_g, v_w_out_a, v_w_out_b, v_w_out,
                            v_final_norm_g)))

    full = _gather_weights(weights)
    small = {n: weights[n].reshape(1, -1) for n in _SMALL_NAMES}
    loss, dx, grads = _local_step(x[0], loss_target[0], full, small)

    recv_w_in, recv_rest, recv_small = _exchange(_grad_slabs(grads, loss), "grad_exchange")
    zero = jnp.zeros((1,), F32)
    outs = [{}, {}, {}, {}]
    res = _sum_adamw(recv_w_in, w_in[0].T, m_w_in[0].T, v_w_in[0].T, W_IN_COL_TILE, "adamw_w_in", axis=1)
    for o, r in zip(outs, res):
        o["w_in"] = r.T[None]
    res = _sum_adamw(recv_rest, _pack_rest(weights), _pack_rest(m_in), _pack_rest(v_in), REST_ROW_TILE, "adamw_rest")
    for o, r in zip(outs, res):
        o.update(_unpack_rest(r, weights))
    res = _sum_adamw(recv_small, _pack_small(weights, zero), _pack_small(m_in, zero), _pack_small(v_in, zero),
                     SMALL_ROWS, "adamw_small")
    for o, r in zip(outs, res):
        o.update(_unpack_small(r, weights)[0])
    loss_sum = _unpack_small(res[0], weights)[1]
    return (loss_sum, dx[None], *[o[n] for o in outs for n in names])
```

```python
import functools
import math

import jax
import jax.numpy as jnp
from jax import lax
from jax.experimental import pallas as pl
from jax.experimental.pallas import tpu as pltpu

F32 = jnp.float32
BF16 = jnp.bfloat16
HI = lax.Precision.HIGHEST
MESH = pl.DeviceIdType.MESH

N_DEV = 8
D = 1024
H = 8
N = 64
DA = H * N
RANK = 64
NA = 4 * DA + 2 * RANK
NB = 4 * DA
NG = 2 * D
NF = 128
IN_COLS = NA + NB + H + NG
COLS_PER_DEV = IN_COLS // N_DEV
RMS_EPS = 1e-6
LNX_EPS = 64e-5
ATT_SCALE = N ** -0.5

ADAM_LR = 0.001
ADAM_B1 = 0.9
ADAM_B2 = 0.999
ADAM_EPS = 1e-08
ADAM_WD = 0.01
ADAM_STEP = 10

LANES = 128
WKV_CHUNK = 64
TOK_TILE = 256
HEAD_TILE = 128
ATT_TILE = 256
ATT_GROUPS = 4
VMEM_LIMIT = 56 * 1024 * 1024

SMALL_SLOTS = {"final_norm_g": (0, D), "shift_mu": (D, NA), "w0": (3200, DA), "a0": (3712, DA), "k_k": (4224, DA),
               "k_a": (4736, DA), "r_k": (5248, DA), "lnx_w": (5760, DA), "lnx_b": (6272, DA), "q_norm_g": (6784, N),
               "k_norm_g": (6912, N), "f_bias": (7040, H)}
LOSS_SLOT = 7168
SMALL_LEN = 7296
W_IN_COL_TILE = 256
EARLY_FROM = -(-NA // COLS_PER_DEV)


def _params(*sem):
    return pltpu.CompilerParams(dimension_semantics=sem or None, vmem_limit_bytes=VMEM_LIMIT)


def _bdot(a, b):
    return jnp.dot(a.astype(BF16), b.astype(BF16), preferred_element_type=F32)


def _bdot_nt(a, b):
    return lax.dot_general(a.astype(BF16), b.astype(BF16), (((1,), (1,)), ((), ())), preferred_element_type=F32)


def _bdot_tn(a, b):
    return lax.dot_general(a.astype(BF16), b.astype(BF16), (((0,), (0,)), ((), ())), preferred_element_type=F32)


def _sigmoid(x):
    return 1.0 / (1.0 + jnp.exp(-x))


def _softplus(x):
    return jnp.maximum(x, 0.0) + jnp.log(1.0 + jnp.exp(-jnp.abs(x)))


def _heads(ref, col0):
    return jnp.stack([ref[:, col0 + N * h:col0 + N * (h + 1)] for h in range(H)])


def _store_heads(ref, col0, val):
    for h in range(H):
        ref[:, col0 + N * h:col0 + N * (h + 1)] = val[h]


def _lerp(c, s, mu):
    return c + (s - c) * mu


def _rwkv_pre(rc, rs, kc, ks, vc, vs, gc, gs, wdc, wds, adc, ads,
              mu_r, mu_k, mu_v, mu_g, mu_wd, mu_ad, w_up, w0, a_up, a0, k_k, k_a):
    r = _lerp(rc, rs, mu_r)
    k = _lerp(kc, ks, mu_k)
    v = _lerp(vc, vs, mu_v)
    g = _lerp(gc, gs, mu_g)
    wd = _lerp(wdc, wds, mu_wd)
    ad = _lerp(adc, ads, mu_ad)
    t = wd.shape[0]
    bdims = (((2,), (1,)), ((0,), (0,)))
    tw = jnp.broadcast_to(jnp.tanh(wd).astype(BF16)[None], (H, t, RANK))
    z = w0 + lax.dot_general(tw, w_up.astype(BF16), bdims, preferred_element_type=F32)
    w_raw = -_softplus(-z) - 0.5
    lw = -jnp.exp(w_raw)
    row = lax.broadcasted_iota(jnp.int32, (t, t), 0)
    col = lax.broadcasted_iota(jnp.int32, (t, t), 1)
    same_chunk = ((row >= col) & (row // WKV_CHUNK == col // WKV_CHUNK)).astype(F32)
    cl = jnp.einsum("hts,hsn->htn", jnp.broadcast_to(same_chunk[None], (H, t, t)), lw, precision=HI,
                    preferred_element_type=F32)
    adb = jnp.broadcast_to(ad.astype(BF16)[None], (H, t, RANK))
    alr = _sigmoid(a0 + lax.dot_general(adb, a_up.astype(BF16), bdims, preferred_element_type=F32))
    kk = k * k_k
    kk = kk / jnp.maximum(jnp.sqrt(jnp.sum(kk * kk, axis=-1, keepdims=True)), 1e-12)
    k2 = k * (1.0 + (alr - 1.0) * k_a)
    return r, lw, cl, k2, v, -kk, kk * alr, g


_MM_DIMS = {"nn": (((2,), (1,)), ((0,), (0,))), "nt": (((2,), (2,)), ((0,), (0,))), "tn": (((1,), (1,)), ((0,), (0,)))}


def _split(x):
    hi = x.astype(BF16)
    return hi, (x - hi.astype(F32)).astype(BF16)


def _dot3(a, b, kind):
    ah, al = _split(a)
    bh, bl = _split(b)
    dot = functools.partial(lax.dot_general, dimension_numbers=_MM_DIMS[kind], preferred_element_type=F32)
    return dot(ah, bh) + (dot(ah, bl) + dot(al, bh))


@functools.partial(jax.custom_vjp, nondiff_argnums=(2,))
def _mm(a, b, kind):
    return _dot3(a, b, kind)


def _mm_fwd(a, b, kind):
    return _dot3(a, b, kind), (a, b)


def _mm_bwd(kind, res, ct):
    a, b = res
    if kind == "nn":
        return _mm(ct, b, "nt"), _mm(a, ct, "tn")
    if kind == "nt":
        return _mm(ct, b, "nn"), _mm(ct, a, "tn")
    return _mm(b, ct, "nt"), _mm(a, ct, "nn")


_mm.defvjp(_mm_fwd, _mm_bwd)


def _chunk_masks(c):
    row = lax.broadcasted_iota(jnp.int32, (c, c), 0)
    col = lax.broadcasted_iota(jnp.int32, (c, c), 1)
    return (row >= col)[None], (row > col)[None], (row == col).astype(F32)[None]


def _wkv_aab(lw, cl, a, b):
    _, strict, _ = _chunk_masks(a.shape[1])
    return jnp.where(strict, _mm(a * jnp.exp(cl - lw), b * jnp.exp(-cl), "nt"), 0.0)


def _tri_inverse(x):
    c = x.shape[1]
    p = _chunk_masks(c)[2] + x
    for _ in range(int(math.log2(c)) - 1):
        x = _mm(x, x, "nn")
        p = p + _mm(p, x, "nn")
    return p


def _wkv_apply(s0, r, lw, cl, k, v, a, b, p):
    c = r.shape[1]
    incl, strict, _ = _chunk_masks(c)
    gi = jnp.exp(-cl)
    left = jnp.concatenate([a * jnp.exp(cl - lw), r * jnp.exp(cl)], axis=1)
    right = jnp.concatenate([b * gi, k * gi], axis=1)
    m = _mm(left, right, "nt")
    z0 = _mm(left, s0, "nt")
    a_ak = jnp.where(strict, m[:, :c, c:], 0.0)
    row = lax.broadcasted_iota(jnp.int32, (c, 2 * c), 0)
    col = lax.broadcasted_iota(jnp.int32, (c, 2 * c), 1)
    a_r = jnp.where((row >= col % c)[None], m[:, c:, :], 0.0)
    sa = _mm(p, z0[:, :c] + _mm(a_ak, v, "nn"), "nn")
    sa_v = jnp.concatenate([sa, v], axis=1)
    y = z0[:, c:] + _mm(a_r, sa_v, "nn")
    s1 = (s0 + _mm(sa_v, right, "tn")) * jnp.exp(cl[:, c - 1:c, :])
    return y, s1


def _rwkv_post(y, r, k2, v, g, lnx_w, lnx_b, r_k):
    mean = jnp.mean(y, axis=-1, keepdims=True)
    yc = y - mean
    var = jnp.mean(yc * yc, axis=-1, keepdims=True)
    yn = yc * lax.rsqrt(var + LNX_EPS) * lnx_w + lnx_b
    bonus = jnp.sum(r * k2 * r_k, axis=-1, keepdims=True) * v
    return (yn + bonus) * (g * _sigmoid(g))


def _fox_pre(q, k, f, q_g, k_g, f_b):
    qn = q * lax.rsqrt(jnp.mean(q * q, axis=-1, keepdims=True) + RMS_EPS) * q_g
    kn = k * lax.rsqrt(jnp.mean(k * k, axis=-1, keepdims=True) + RMS_EPS) * k_g
    x = f + f_b
    return qn, kn, jnp.minimum(x, 0.0) - jnp.log(1.0 + jnp.exp(-jnp.abs(x)))


def _rms_fwd(x, g):
    s = x.shape[0]

    def body(x_ref, g_ref, h_ref):
        xv = x_ref[...]
        h_ref[...] = (xv * lax.rsqrt(jnp.mean(xv * xv, axis=-1, keepdims=True) + RMS_EPS) * g_ref[...]).astype(BF16)

    return pl.pallas_call(
        body, name="rms_fwd", grid=(s // TOK_TILE,),
        in_specs=[pl.BlockSpec((TOK_TILE, D), lambda i: (i, 0)), pl.BlockSpec((1, D), lambda i: (0, 0))],
        out_specs=pl.BlockSpec((TOK_TILE, D), lambda i: (i, 0)),
        out_shape=jax.ShapeDtypeStruct((s, D), BF16), compiler_params=_params("arbitrary"))(x, g)


def _proj(h, wt, name):
    s, n = h.shape[0], wt.shape[0]

    def body(h_ref, w_ref, o_ref):
        o_ref[...] = _bdot_nt(h_ref[...], w_ref[...])

    return pl.pallas_call(
        body, name=name, grid=(s // TOK_TILE,),
        in_specs=[pl.BlockSpec((TOK_TILE, D), lambda i: (i, 0)), pl.BlockSpec((n, D), lambda i: (0, 0))],
        out_specs=pl.BlockSpec((TOK_TILE, n), lambda i: (i, 0)),
        out_shape=jax.ShapeDtypeStruct((s, n), F32), compiler_params=_params("arbitrary"))(h, wt)


def _proj_wgrad(h, du, name):
    s, n = du.shape

    def body(h_ref, du_ref, o_ref):
        @pl.when(pl.program_id(0) == 0)
        def _():
            o_ref[...] = jnp.zeros_like(o_ref)

        o_ref[...] += _bdot_tn(du_ref[...], h_ref[...])

    return pl.pallas_call(
        body, name=name, grid=(s // TOK_TILE,),
        in_specs=[pl.BlockSpec((TOK_TILE, D), lambda i: (i, 0)), pl.BlockSpec((TOK_TILE, n), lambda i: (i, 0))],
        out_specs=pl.BlockSpec((n, D), lambda i: (0, 0)),
        out_shape=jax.ShapeDtypeStruct((n, D), F32), compiler_params=_params("arbitrary"))(h, du)


def _proj_xgrad(x, g, dx2, dus, ws, slabs, owners, landing):
    s = x.shape[0]
    tile = HEAD_TILE
    k = len(dus)
    nx = len(slabs)
    carried = [a for a in range(nx) if landing[a] is not None]
    n_in = 3 + 2 * k + nx + len(carried)

    def body(*refs):
        x_ref, g_ref, dx2_ref = refs[:3]
        du_refs, w_refs = refs[3:3 + k], refs[3 + k:3 + 2 * k]
        src_refs = refs[3 + 2 * k:3 + 2 * k + nx]
        dx_ref, dg_ref = refs[n_in:n_in + 2]
        dst_refs = refs[n_in + 2:n_in + 2 + nx]
        start, wait = _exchange_ops(src_refs, dst_refs, owners, refs[n_in + 2 + nx:])

        @pl.when(pl.program_id(0) == 0)
        def _():
            dg_ref[...] = jnp.zeros_like(dg_ref)
            start()

        dh = _bdot(du_refs[0][...], w_refs[0][...])
        for du_ref, w_ref in zip(du_refs[1:], w_refs[1:]):
            dh += _bdot(du_ref[...], w_ref[...])
        xv = x_ref[...]
        rs = lax.rsqrt(jnp.mean(xv * xv, axis=-1, keepdims=True) + RMS_EPS)
        xn = xv * rs
        dg_ref[...] += jnp.sum(dh * xn, axis=0, keepdims=True)
        dxn = dh * g_ref[...]
        dx_ref[...] = rs * (dxn - xn * jnp.mean(dxn * xn, axis=-1, keepdims=True)) + dx2_ref[...]

        @pl.when(pl.program_id(0) == s // tile - 1)
        def _():
            wait()

    tok = lambda n: pl.BlockSpec((tile, n), lambda i: (i, 0))
    fixed = lambda a: pl.BlockSpec(a.shape, lambda i: (0,) * a.ndim)
    out = pl.pallas_call(
        body, name="proj_xgrad", grid=(s // tile,),
        in_specs=([tok(D), fixed(g), tok(D)] + [tok(du.shape[1]) for du in dus] + [fixed(w) for w in ws]
                  + _hbm_specs(nx + len(carried))),
        out_specs=[tok(D), pl.BlockSpec((1, D), lambda i: (0, 0))] + _hbm_specs(nx),
        out_shape=[jax.ShapeDtypeStruct((s, D), F32), jax.ShapeDtypeStruct((1, D), F32)] + _received_shapes(slabs),
        input_output_aliases={3 + 2 * k + nx + i: 2 + a for i, a in enumerate(carried)},
        scratch_shapes=_exchange_scratch(nx),
        compiler_params=_params("arbitrary"))(x, g, dx2, *dus, *ws, *slabs, *[landing[a] for a in carried])
    return out[0], out[1], out[2:]


def _tail(x, target, ya, o, ub, ug, w_oa, w_ob, w_o, fg):
    s = x.shape[0]
    tile = TOK_TILE

    def body(x_ref, t_ref, ya_ref, o_ref, gb_ref, ug_ref, woa_ref, wob_ref, wo_ref, fg_ref,
             loss_ref, dfg_ref, dwo_ref, dwoa_ref, dwob_ref, dx2_ref, dya_ref, do_ref, dgb_ref, dug_ref):
        @pl.when(pl.program_id(0) == 0)
        def _():
            for r in (loss_ref, dfg_ref, dwo_ref, dwoa_ref, dwob_ref):
                r[...] = jnp.zeros_like(r)

        ya_v = ya_ref[...]
        gate_b = gb_ref[...]
        sg_b = _sigmoid(gate_b)
        silu_b = gate_b * sg_b
        o_v = jnp.concatenate([o_ref[h] for h in range(H)], axis=-1)
        yb_v = o_v * silu_b
        big_a = _bdot(ya_v, woa_ref[...])
        big_b = _bdot(yb_v, wob_ref[...])
        sa = _sigmoid(ug_ref[:, :D])
        sb = _sigmoid(ug_ref[:, D:])
        merged = sa * big_a + sb * big_b
        x2 = x_ref[...] + _bdot(merged, wo_ref[...])
        rs = lax.rsqrt(jnp.mean(x2 * x2, axis=-1, keepdims=True) + RMS_EPS)
        xn = x2 * rs
        err = xn * fg_ref[...] - t_ref[...]
        loss_ref[...] += (0.5 / D) * jnp.sum(err * err)
        dout = err * (1.0 / D)
        dfg_ref[...] += jnp.sum(dout * xn, axis=0, keepdims=True)
        dxn = dout * fg_ref[...]
        dx2 = rs * (dxn - xn * jnp.mean(dxn * xn, axis=-1, keepdims=True))
        dx2_ref[...] = dx2
        dwo_ref[...] += _bdot_tn(merged, dx2)
        dmerged = _bdot_nt(dx2, wo_ref[...])
        dbig_a = dmerged * sa
        dbig_b = dmerged * sb
        dug_ref[:, :D] = dmerged * big_a * sa * (1.0 - sa)
        dug_ref[:, D:] = dmerged * big_b * sb * (1.0 - sb)
        dwoa_ref[...] += _bdot_tn(ya_v, dbig_a)
        dwob_ref[...] += _bdot_tn(yb_v, dbig_b)
        dya_ref[...] = _bdot_nt(dbig_a, woa_ref[...])
        dyb = _bdot_nt(dbig_b, wob_ref[...])
        dgb_ref[...] = dyb * o_v * (sg_b * (1.0 + gate_b * (1.0 - sg_b)))
        _dov = dyb * silu_b
        for h in range(H):
            do_ref[h] = _dov[:, N * h:N * (h + 1)]

    tok = lambda n: pl.BlockSpec((tile, n), lambda i: (i, 0))
    hm = pl.BlockSpec((H, tile, N), lambda i: (0, i, 0))
    fixed = lambda shape: pl.BlockSpec(shape, lambda i: (0,) * len(shape))
    f32 = lambda *shape: jax.ShapeDtypeStruct(shape, F32)
    return pl.pallas_call(
        body, name="tail", grid=(s // tile,),
        in_specs=[tok(D), tok(D), tok(DA), hm, pl.BlockSpec((tile, DA), lambda i: (i, 3)), tok(NG),
                  fixed((DA, D)), fixed((DA, D)), fixed((D, D)), fixed((1, D))],
        out_specs=[fixed((1, 1)), fixed((1, D)), fixed((D, D)), fixed((DA, D)), fixed((DA, D)),
                   tok(D), tok(DA), hm, tok(DA), tok(NG)],
        out_shape=[f32(1, 1), f32(1, D), f32(D, D), f32(DA, D), f32(DA, D),
                   f32(s, D), f32(s, DA), f32(H, s, N), f32(s, DA), f32(s, NG)],
        compiler_params=_params("arbitrary"))(x, target, ya, o, ub, ug, w_oa, w_ob, w_o, fg)


_PRE_PARAM_SHAPES = ((H, 1, N),) * 4 + ((1, RANK),) * 2 + ((H, RANK, N), (H, 1, N), (H, RANK, N), (H, 1, N), (H, 1, N),
                                                              (H, 1, N))


def _pre_operands(ua_ref, prev_ref, first):
    cur = ua_ref[...]
    t = cur.shape[0]
    prev_row = jnp.where(first, 0.0, prev_ref[7:8, :])
    rows = lax.broadcasted_iota(jnp.int32, cur.shape, 0)
    sh = jnp.where(rows == 0, prev_row, pltpu.roll(cur, 1, axis=0))
    ops = []
    for c0 in (0, DA, 2 * DA, 3 * DA + 2 * RANK):
        ops.append(jnp.stack([cur[:, c0 + N * h:c0 + N * (h + 1)] for h in range(H)]))
        ops.append(jnp.stack([sh[:, c0 + N * h:c0 + N * (h + 1)] for h in range(H)]))
    for c0 in (3 * DA, 3 * DA + RANK):
        ops.append(cur[:, c0:c0 + RANK])
        ops.append(sh[:, c0:c0 + RANK])
    del t
    return ops


def _ua_specs(tile, order):
    blocks = tile // 8
    return [pl.BlockSpec((tile, NA), lambda i: (order(i), 0)),
            pl.BlockSpec((8, NA), lambda i: (jnp.maximum(order(i) * blocks - 1, 0), 0))]


def _rwkv_pre_fwd(ua, pre_params):
    s = ua.shape[0]
    tile = HEAD_TILE

    def body(ua_ref, prev_ref, *refs):
        p_refs, o_refs = refs[:len(pre_params)], refs[len(pre_params):]
        ops = _pre_operands(ua_ref, prev_ref, pl.program_id(0) == 0)
        outs = _rwkv_pre(*ops, *[p[...] for p in p_refs])
        for o_ref, val in zip(o_refs, outs):
            o_ref[...] = val

    hm = pl.BlockSpec((H, tile, N), lambda i: (0, i, 0))
    return pl.pallas_call(
        body, name="rwkv_pre_fwd", grid=(s // tile,),
        in_specs=_ua_specs(tile, lambda i: i) + [pl.BlockSpec(p.shape, lambda i, nd=p.ndim: (0,) * nd) for p in pre_params],
        out_specs=[hm] * 8, out_shape=[jax.ShapeDtypeStruct((H, s, N), F32)] * 8,
        compiler_params=_params("arbitrary"))(ua, ua, *pre_params)


def _rwkv_pre_bwd(ua, pre_params, cots):
    s = ua.shape[0]
    tile = HEAD_TILE
    nt = s // tile
    n_p = len(pre_params)

    def body(ua_ref, prev_ref, *refs):
        p_refs, c_refs = refs[:n_p], refs[n_p:n_p + 11]
        dua_ref = refs[n_p + 11]
        dp_refs = refs[n_p + 12:n_p + 12 + n_p]
        carry_ref = refs[-1]
        i = pl.program_id(0)

        @pl.when(i == 0)
        def _():
            carry_ref[...] = jnp.zeros_like(carry_ref)
            for r in dp_refs:
                r[...] = jnp.zeros_like(r)

        ops = _pre_operands(ua_ref, prev_ref, i == nt - 1)
        _, vjp = jax.vjp(_rwkv_pre, *ops, *[p[...] for p in p_refs])
        c = [r[...] for r in c_refs]
        grads = vjp((c[0] + c[1], c[2], c[3], c[4] + c[5], c[6] + c[7], c[8], c[9], c[10]))
        d_ops, d_par = grads[:12], grads[12:]
        for r, val in zip(dp_refs, d_par):
            r[...] += val
        d_cur = jnp.concatenate([d_ops[0][h] for h in range(H)] + [d_ops[2][h] for h in range(H)]
                                + [d_ops[4][h] for h in range(H)] + [d_ops[8], d_ops[10]]
                                + [d_ops[6][h] for h in range(H)], axis=-1)
        d_sh = jnp.concatenate([d_ops[1][h] for h in range(H)] + [d_ops[3][h] for h in range(H)]
                               + [d_ops[5][h] for h in range(H)] + [d_ops[9], d_ops[11]]
                               + [d_ops[7][h] for h in range(H)], axis=-1)
        rows = lax.broadcasted_iota(jnp.int32, d_sh.shape, 0)
        dua_ref[...] = d_cur + jnp.where(rows == tile - 1, carry_ref[...], pltpu.roll(d_sh, tile - 1, axis=0))
        carry_ref[...] = d_sh[0:1, :]

    rev = lambda i: nt - 1 - i
    hm = pl.BlockSpec((H, tile, N), lambda i: (0, rev(i), 0))
    fixed = [pl.BlockSpec(p.shape, lambda i, nd=p.ndim: (0,) * nd) for p in pre_params]
    return pl.pallas_call(
        body, name="rwkv_pre_bwd", grid=(nt,),
        in_specs=_ua_specs(tile, rev) + fixed + [hm] * 11,
        out_specs=[pl.BlockSpec((tile, NA), lambda i: (rev(i), 0))] + fixed,
        out_shape=[jax.ShapeDtypeStruct((s, NA), F32)] + [jax.ShapeDtypeStruct(p.shape, F32) for p in pre_params],
        scratch_shapes=[pltpu.VMEM((1, NA), F32)],
        compiler_params=_params("arbitrary"))(ua, ua, *pre_params, *cots)


def _wkv_fwd(seq):
    s = seq[0].shape[1]
    nc = s // WKV_CHUNK

    def body(r_ref, lw_ref, cl_ref, k_ref, v_ref, a_ref, b_ref, y_ref, ck_ref, p_ref, state):
        @pl.when(pl.program_id(0) == 0)
        def _():
            state[...] = jnp.zeros_like(state)

        s0 = state[...]
        ck_ref[0] = s0
        p = _tri_inverse(_wkv_aab(lw_ref[...], cl_ref[...], a_ref[...], b_ref[...]))
        p_ref[0] = p
        y, s1 = _wkv_apply(s0, r_ref[...], lw_ref[...], cl_ref[...], k_ref[...], v_ref[...], a_ref[...], b_ref[...], p)
        y_ref[...] = y
        state[...] = s1

    hm = pl.BlockSpec((H, WKV_CHUNK, N), lambda c: (0, c, 0))
    per_chunk = lambda m: pl.BlockSpec((1, H, m, m), lambda c: (c, 0, 0, 0))
    return pl.pallas_call(
        body, name="wkv_fwd", grid=(nc,), in_specs=[hm] * 7,
        out_specs=[hm, per_chunk(N), per_chunk(WKV_CHUNK)],
        out_shape=[jax.ShapeDtypeStruct((H, s, N), F32), jax.ShapeDtypeStruct((nc, H, N, N), F32),
                   jax.ShapeDtypeStruct((nc, H, WKV_CHUNK, WKV_CHUNK), F32)],
        scratch_shapes=[pltpu.VMEM((H, N, N), F32)], compiler_params=_params("arbitrary"))(*seq)


def _wkv_bwd(seq, ckpt, pinv, dy, slabs, owners):
    s = seq[0].shape[1]
    nc = s // WKV_CHUNK
    nx = len(slabs)

    def body(r_ref, lw_ref, cl_ref, k_ref, v_ref, a_ref, b_ref, ck_ref, p_ref, dy_ref, *refs):
        src_refs, d_refs, dst_refs = refs[:nx], refs[nx:nx + 7], refs[nx + 7:2 * nx + 7]
        dstate = refs[2 * nx + 7]
        start, wait = _exchange_ops(src_refs, dst_refs, owners, refs[2 * nx + 8:])

        @pl.when(pl.program_id(0) == 0)
        def _():
            dstate[...] = jnp.zeros_like(dstate)
            start()

        p = p_ref[0]
        lw, cl, a, b = lw_ref[...], cl_ref[...], a_ref[...], b_ref[...]
        _, vjp = jax.vjp(_wkv_apply, ck_ref[0], r_ref[...], lw, cl, k_ref[...], v_ref[...], a, b, p)
        ds0, dr, dlw, dcl, dk, dv, da, db, dp = vjp((dy_ref[...], dstate[...]))
        dstate[...] = ds0
        _, vjp_x = jax.vjp(_wkv_aab, lw, cl, a, b)
        dlw2, dcl2, da2, db2 = vjp_x(_mm(_mm(p, dp, "tn"), p, "nt"))
        for d_ref, val in zip(d_refs, (dr, dlw + dlw2, dcl + dcl2, dk, dv, da + da2, db + db2)):
            d_ref[...] = val

        @pl.when(pl.program_id(0) == nc - 1)
        def _():
            wait()

    hm = pl.BlockSpec((H, WKV_CHUNK, N), lambda c: (0, nc - 1 - c, 0))
    per_chunk = lambda m: pl.BlockSpec((1, H, m, m), lambda c: (nc - 1 - c, 0, 0, 0))
    out = pl.pallas_call(
        body, name="wkv_bwd", grid=(nc,),
        in_specs=[hm] * 7 + [per_chunk(N), per_chunk(WKV_CHUNK), hm] + _hbm_specs(nx),
        out_specs=[hm] * 7 + _hbm_specs(nx),
        out_shape=[jax.ShapeDtypeStruct((H, s, N), F32)] * 7 + _received_shapes(slabs),
        scratch_shapes=[pltpu.VMEM((H, N, N), F32)] + _exchange_scratch(nx),
        compiler_params=_params("arbitrary"))(*seq, ckpt, pinv, dy, *slabs)
    return out[:7], out[7:]


def _rwkv_post_fwd(y, r, k2, v, g, post_params):
    s = y.shape[1]
    tile = HEAD_TILE

    def body(y_ref, r_ref, k_ref, v_ref, g_ref, w_ref, b_ref, rk_ref, o_ref):
        out = _rwkv_post(y_ref[...], r_ref[...], k_ref[...], v_ref[...], g_ref[...], w_ref[...], b_ref[...],
                         rk_ref[...])
        o_ref[...] = jnp.concatenate([out[h] for h in range(H)], axis=-1)

    hm = pl.BlockSpec((H, tile, N), lambda i: (0, i, 0))
    par = pl.BlockSpec((H, 1, N), lambda i: (0, 0, 0))
    return pl.pallas_call(
        body, name="rwkv_post_fwd", grid=(s // tile,), in_specs=[hm] * 5 + [par] * 3,
        out_specs=pl.BlockSpec((tile, DA), lambda i: (i, 0)), out_shape=jax.ShapeDtypeStruct((s, DA), F32),
        compiler_params=_params("arbitrary"))(y, r, k2, v, g, *post_params)


def _rwkv_post_bwd(y, r, k2, v, g, post_params, dya):
    s = y.shape[1]
    tile = HEAD_TILE

    def body(y_ref, r_ref, k_ref, v_ref, g_ref, w_ref, b_ref, rk_ref, dya_ref, *d_refs):
        @pl.when(pl.program_id(0) == 0)
        def _():
            for ref in d_refs[5:]:
                ref[...] = jnp.zeros_like(ref)

        _, vjp = jax.vjp(_rwkv_post, y_ref[...], r_ref[...], k_ref[...], v_ref[...], g_ref[...], w_ref[...],
                         b_ref[...], rk_ref[...])
        grads = vjp(jnp.stack([dya_ref[:, N * h:N * (h + 1)] for h in range(H)]))
        for ref, val in zip(d_refs[:5], grads[:5]):
            ref[...] = val
        for ref, val in zip(d_refs[5:], grads[5:]):
            ref[...] += val

    hm = pl.BlockSpec((H, tile, N), lambda i: (0, i, 0))
    par = pl.BlockSpec((H, 1, N), lambda i: (0, 0, 0))
    return pl.pallas_call(
        body, name="rwkv_post_bwd", grid=(s // tile,),
        in_specs=[hm] * 5 + [par] * 3 + [pl.BlockSpec((tile, DA), lambda i: (i, 0))],
        out_specs=[hm] * 5 + [par] * 3,
        out_shape=[jax.ShapeDtypeStruct((H, s, N), F32)] * 5 + [jax.ShapeDtypeStruct((H, 1, N), F32)] * 3,
        compiler_params=_params("arbitrary"))(y, r, k2, v, g, *post_params, dya)


def _tri(t):
    return (lax.broadcasted_iota(jnp.int32, (t, t), 0) >= lax.broadcasted_iota(jnp.int32, (t, t), 1)).astype(F32)


def _fox_pre_fwd(ub, uf, q_g, k_g, f_b):
    s = ub.shape[0]
    tile = HEAD_TILE

    def body(ub_ref, uf_ref, qg_ref, kg_ref, fb_ref, q_ref, k_ref, v_ref, cum_ref, carry):
        @pl.when(pl.program_id(0) == 0)
        def _():
            carry[...] = jnp.zeros_like(carry)

        qn, kn, logf = _fox_pre(_heads(ub_ref, 0), _heads(ub_ref, DA), uf_ref[...], qg_ref[...], kg_ref[...],
                                fb_ref[...])
        q_ref[...] = qn
        k_ref[...] = kn
        v_ref[...] = _heads(ub_ref, 2 * DA)
        cum = jnp.dot(_tri(tile), logf, precision=HI, preferred_element_type=F32) + carry[...]
        cum_ref[...] = cum
        carry[...] = cum[tile - 1:tile, :]

    hm = pl.BlockSpec((H, tile, N), lambda i: (0, i, 0))
    fixed = lambda shape: pl.BlockSpec(shape, lambda i: (0,) * len(shape))
    return pl.pallas_call(
        body, name="fox_pre_fwd", grid=(s // tile,),
        in_specs=[pl.BlockSpec((tile, NB), lambda i: (i, 0)), pl.BlockSpec((tile, NF), lambda i: (i, 0)),
                  fixed((1, 1, N)), fixed((1, 1, N)), fixed((1, NF))],
        out_specs=[hm] * 3 + [pl.BlockSpec((tile, NF), lambda i: (i, 0))],
        out_shape=[jax.ShapeDtypeStruct((H, s, N), F32)] * 3 + [jax.ShapeDtypeStruct((s, NF), F32)],
        scratch_shapes=[pltpu.VMEM((1, NF), F32)], compiler_params=_params("arbitrary"))(ub, uf, q_g, k_g, f_b)


def _fox_pre_bwd(ub, uf, q_g, k_g, f_b, dqn, dkn, dvf, dgate, dcum_q, dcum_k):
    s = ub.shape[0]
    tile = HEAD_TILE
    nt = s // tile

    def body(ub_ref, uf_ref, qg_ref, kg_ref, fb_ref, dq_ref, dk_ref, dv_ref, dgate_ref, dcq_ref, dck_ref,
             dub_ref, duf_ref, dqg_ref, dkg_ref, dfb_ref, carry):
        @pl.when(pl.program_id(0) == 0)
        def _():
            carry[...] = jnp.zeros_like(carry)
            for ref in (dqg_ref, dkg_ref, dfb_ref):
                ref[...] = jnp.zeros_like(ref)

        dcum = dcq_ref[...] + dck_ref[...]
        dlogf = lax.dot_general(_tri(tile), dcum, (((0,), (0,)), ((), ())), precision=HI,
                                preferred_element_type=F32) + carry[...]
        carry[...] = dlogf[0:1, :]
        _, vjp = jax.vjp(_fox_pre, _heads(ub_ref, 0), _heads(ub_ref, DA), uf_ref[...], qg_ref[...], kg_ref[...],
                         fb_ref[...])
        d_q, d_k, d_f, d_qg, d_kg, d_fb = vjp((dq_ref[...], dk_ref[...], dlogf))
        _store_heads(dub_ref, 0, d_q)
        _store_heads(dub_ref, DA, d_k)
        _store_heads(dub_ref, 2 * DA, dv_ref[...])
        dub_ref[:, 3 * DA:] = dgate_ref[...]
        duf_ref[...] = d_f
        dqg_ref[...] += d_qg
        dkg_ref[...] += d_kg
        dfb_ref[...] += d_fb

    rev = lambda i: nt - 1 - i
    hm = pl.BlockSpec((H, tile, N), lambda i: (0, rev(i), 0))
    tok = lambda n: pl.BlockSpec((tile, n), lambda i: (rev(i), 0))
    fixed = lambda shape: pl.BlockSpec(shape, lambda i: (0,) * len(shape))
    return pl.pallas_call(
        body, name="fox_pre_bwd", grid=(nt,),
        in_specs=[tok(NB), tok(NF), fixed((1, 1, N)), fixed((1, 1, N)), fixed((1, NF)), hm, hm, hm, tok(DA), tok(NF),
                  tok(NF)],
        out_specs=[tok(NB), tok(NF), fixed((1, 1, N)), fixed((1, 1, N)), fixed((1, NF))],
        out_shape=[jax.ShapeDtypeStruct((s, NB), F32), jax.ShapeDtypeStruct((s, NF), F32),
                   jax.ShapeDtypeStruct((1, 1, N), F32), jax.ShapeDtypeStruct((1, 1, N), F32),
                   jax.ShapeDtypeStruct((1, NF), F32)],
        scratch_shapes=[pltpu.VMEM((1, NF), F32)],
        compiler_params=_params("arbitrary"))(ub, uf, q_g, k_g, f_b, dqn, dkn, dvf, dgate, dcum_q, dcum_k)


def _att_groups(s):
    blocks = s // ATT_TILE
    per = max(1, blocks // ATT_GROUPS)
    return per, blocks // per


def _att_logits(q_bf, k, cq, ck, qi):
    tq, sk = q_bf.shape[0], k.shape[0]
    logits = _bdot_nt(q_bf, k) * ATT_SCALE + cq - ck
    rows = qi * tq + lax.broadcasted_iota(jnp.int32, (tq, sk), 0)
    mask = rows >= lax.broadcasted_iota(jnp.int32, (tq, sk), 1)
    return jnp.where(mask, logits, -1e30), mask


def _fox_attn_fwd(q, k, v, cum_q, cum_k):
    s = q.shape[1]
    t = ATT_TILE
    per, groups = _att_groups(s)

    def body(q_ref, k_ref, v_ref, cq_ref, ck_ref, o_ref, lse_ref):
        qi = pl.program_id(1)
        for g in range(groups):
            @pl.when(qi // per == g)
            def _(n=(g + 1) * per * t):
                logits, _ = _att_logits(q_ref[0].astype(BF16), k_ref[0, :n, :], cq_ref[0], ck_ref[0, :, :n], qi)
                m = jnp.max(logits, axis=-1, keepdims=True)
                p = jnp.exp(logits - m)
                l = jnp.sum(p, axis=-1, keepdims=True)
                o_ref[0] = _bdot(p, v_ref[0, :n, :]) / l
                lse_ref[0] = m + jnp.log(l)

    qb = pl.BlockSpec((1, t, N), lambda h, i: (h, i, 0))
    kb = pl.BlockSpec((1, s, N), lambda h, i: (h, 0, 0))
    return pl.pallas_call(
        body, name="fox_attn_fwd", grid=(H, s // t),
        in_specs=[qb, kb, kb, pl.BlockSpec((1, t, 1), lambda h, i: (h, i, 0)),
                  pl.BlockSpec((1, 1, s), lambda h, i: (h, 0, 0))],
        out_specs=[qb, pl.BlockSpec((1, t, 1), lambda h, i: (h, i, 0))],
        out_shape=[jax.ShapeDtypeStruct((H, s, N), F32), jax.ShapeDtypeStruct((H, s, 1), F32)],
        compiler_params=_params("arbitrary", "arbitrary"))(q, k, v, cum_q, cum_k)


def _fox_attn_bwd(q, k, v, cum_q, cum_k, o, lse, do):
    s = q.shape[1]
    t = ATT_TILE
    per, groups = _att_groups(s)

    def body(q_ref, k_ref, v_ref, cq_ref, ck_ref, o_ref, lse_ref, do_ref, dq_ref, dk_ref, dv_ref, dcq_ref, dck_ref):
        qi = pl.program_id(1)

        @pl.when(qi == 0)
        def _():
            for ref in (dk_ref, dv_ref, dck_ref):
                ref[...] = jnp.zeros_like(ref)

        for g in range(groups):
            @pl.when(qi // per == g)
            def _(n=(g + 1) * per * t):
                q_bf, do_bf = q_ref[0].astype(BF16), do_ref[0].astype(BF16)
                kv = k_ref[0, :n, :]
                logits, mask = _att_logits(q_bf, kv, cq_ref[0], ck_ref[0, :, :n], qi)
                p = jnp.where(mask, jnp.exp(logits - lse_ref[0]), 0.0)
                delta = jnp.sum(do_ref[0] * o_ref[0], axis=-1, keepdims=True)
                ds = p * (_bdot_nt(do_bf, v_ref[0, :n, :]) - delta)
                dq_ref[0] = _bdot(ds, kv) * ATT_SCALE
                dk_ref[0, :n, :] += _bdot_tn(ds, q_bf) * ATT_SCALE
                dv_ref[0, :n, :] += _bdot_tn(p, do_bf)
                dcq_ref[0] = jnp.sum(ds, axis=-1, keepdims=True)
                dck_ref[0, :, :n] -= jnp.sum(ds, axis=0, keepdims=True)

    qb = pl.BlockSpec((1, t, N), lambda h, i: (h, i, 0))
    kb = pl.BlockSpec((1, s, N), lambda h, i: (h, 0, 0))
    cqb = pl.BlockSpec((1, t, 1), lambda h, i: (h, i, 0))
    ckb = pl.BlockSpec((1, 1, s), lambda h, i: (h, 0, 0))
    f32 = lambda *shape: jax.ShapeDtypeStruct(shape, F32)
    return pl.pallas_call(
        body, name="fox_attn_bwd", grid=(H, s // t),
        in_specs=[qb, kb, kb, cqb, ckb, qb, cqb, qb], out_specs=[qb, kb, kb, cqb, ckb],
        out_shape=[f32(H, s, N), f32(H, s, N), f32(H, s, N), f32(H, s, 1), f32(H, 1, s)],
        compiler_params=_params("arbitrary", "arbitrary"))(q, k, v, cum_q, cum_k, o, lse, do)


def _head_param(p):
    return p.reshape(H, 1, N)


def _local_step(x, target, w, p):
    mu = p["shift_mu"]
    pre_params = (_head_param(mu[:, 0:DA]), _head_param(mu[:, DA:2 * DA]), _head_param(mu[:, 2 * DA:3 * DA]),
                  _head_param(mu[:, 3 * DA + 2 * RANK:]), mu[:, 3 * DA:3 * DA + RANK],
                  mu[:, 3 * DA + RANK:3 * DA + 2 * RANK],
                  w["w_lora_up"].astype(F32), _head_param(p["w0"]), w["a_lora_up"].astype(F32), _head_param(p["a0"]),
                  _head_param(p["k_k"]), _head_param(p["k_a"]))
    post_params = (_head_param(p["lnx_w"]), _head_param(p["lnx_b"]), _head_param(p["r_k"]))
    q_g, k_g = p["q_norm_g"].reshape(1, 1, N), p["k_norm_g"].reshape(1, 1, N)
    f_b = jnp.pad(p["f_bias"], ((0, 0), (0, NF - H)))
    fg = p["final_norm_g"].reshape(1, D)

    h = _rms_fwd(x, p["norm_g"])
    ua = _proj(h, w["in_a"], "proj_a")
    ub = _proj(h, w["in_b"], "proj_b")
    ug = _proj(h, w["in_g"], "proj_g")
    uf = _proj(h, w["in_f"], "proj_f")
    r, lw, cl, k2, v, av, bv, gg = _rwkv_pre_fwd(ua, pre_params)
    y, ckpt, pinv = _wkv_fwd((r, lw, cl, k2, v, av, bv))
    ya = _rwkv_post_fwd(y, r, k2, v, gg, post_params)
    qn, kn, vf, cum = _fox_pre_fwd(ub, uf, q_g, k_g, f_b)
    cum_t = cum[:, :H].T
    cum_q, cum_k = cum_t[:, :, None], cum_t[:, None, :]
    o, lse = _fox_attn_fwd(qn, kn, vf, cum_q, cum_k)

    (loss, dfg, dwo, dwoa, dwob, dx2, dya, do, dgate_b, dug) = _tail(
        x, target, ya, o, ub, ug, w["w_out_a"], w["w_out_b"], w["w_out"], fg)
    dqn, dkn, dvf, dcq, dck = _fox_attn_bwd(qn, kn, vf, cum_q, cum_k, o, lse, do)
    pad_f = lambda a: jnp.pad(a.T, ((0, 0), (0, NF - H)))
    dub, duf, dqg, dkg, dfb = _fox_pre_bwd(ub, uf, q_g, k_g, f_b, dqn, dkn, dvf, dgate_b,
                                           pad_f(dcq[:, :, 0]), pad_f(dck.reshape(H, -1)))
    dwt_b, dwt_g, dwt_f = (_proj_wgrad(h, du, name) for du, name in ((dub, "wgrad_b"), (dug, "wgrad_g"), (duf, "wgrad_f")))
    dy, dr_p, dk_p, dv_p, dgg, dlnw, dlnb, drk = _rwkv_post_bwd(y, r, k2, v, gg, post_params, dya)

    everyone = (0, N_DEV)
    early = _slab_wt_grad((dwt_b, dwt_g, dwt_f), (_WT_SEGMENTS[1], _WT_SEGMENTS[2], _WT_SEGMENTS[3]), EARLY_FROM, N_DEV,
                          "slab_wt_early")
    (dr_s, dlw, dcl, dk_s, dv_s, dav, dbv), (recv_wt, recv_woa, recv_wob, recv_wo) = _wkv_bwd(
        (r, lw, cl, k2, v, av, bv), ckpt, pinv, dy,
        (early, _col_slabs(dwoa), _col_slabs(dwob), dwo.astype(BF16).reshape(N_DEV, D // N_DEV, D)),
        ((EARLY_FROM, N_DEV), everyone, everyone, everyone))
    pre_out = _rwkv_pre_bwd(ua, pre_params, (dr_s, dr_p, dlw, dcl, dk_s, dk_p, dv_s, dv_p, dav, dbv, dgg))
    dua, dpre = pre_out[0], pre_out[1:]
    dwt_a = _proj_wgrad(h, dua, "wgrad_a")

    flat = lambda a: a.reshape(1, -1)
    small = {
        "final_norm_g": dfg, "w0": dpre[7], "a0": dpre[9], "k_k": dpre[10], "k_a": dpre[11], "r_k": drk, "lnx_w": dlnw,
        "lnx_b": dlnb, "q_norm_g": dqg, "k_norm_g": dkg, "f_bias": dfb[:, :H],
        "shift_mu": jnp.concatenate([flat(dpre[0]), flat(dpre[1]), flat(dpre[2]), dpre[4], dpre[5], flat(dpre[3])], axis=1),
    }
    late = _slab_wt_grad((dwt_a, dwt_b), (_WT_SEGMENTS[0], _WT_SEGMENTS[1]), 0, EARLY_FROM, "slab_wt_late")
    loras = jnp.stack([dpre[6], dpre[8]], axis=1).astype(BF16)
    dx, dng, (recv_wt, recv_lora, recv_small) = _proj_xgrad(
        x, p["norm_g"], dx2, (dua, dub, dug, duf), (w["in_a"], w["in_b"], w["in_g"], w["in_f"]),
        (late, loras, _pack_small(small, loss)), ((0, EARLY_FROM), everyone, everyone), (recv_wt, None, None))
    return dx, dng, recv_wt, (recv_woa, recv_wob, recv_wo, recv_lora), recv_small


def _position():
    return lax.axis_index("x"), lax.axis_index("y"), lax.axis_index("c")


def _hbm_specs(n):
    return [pl.BlockSpec(memory_space=pl.ANY)] * n


def _all_gather(blocks, name):
    n = len(blocks)

    def body(*refs):
        x_refs, out_refs = refs[:n], refs[n:2 * n]
        send_sems, recv_sems, local_sems = refs[2 * n:]
        x, y, c = _position()
        me, sibling = (x, y, c), (x, y, 1 - c)
        chips = [(1 - x, y), (x, 1 - y), (1 - x, 1 - y)]

        def copy(a, k, blk, to, own=False):
            dst = out_refs[a].at[4 * blk[0] + 2 * blk[1] + blk[2]]
            return pltpu.make_async_remote_copy(
                src_ref=x_refs[a] if own else dst, dst_ref=dst, send_sem=send_sems.at[7 * a + k],
                recv_sem=recv_sems.at[7 * a + k], device_id=to, device_id_type=MESH)

        mine = [pltpu.make_async_copy(x_refs[a], out_refs[a].at[4 * x + 2 * y + c], local_sems.at[a]) for a in range(n)]
        for cp in mine:
            cp.start()
        first = []
        for a in range(n):
            first.append(copy(a, 0, me, sibling, own=True))
            first += [copy(a, 1 + j, me, (*chip, c), own=True) for j, chip in enumerate(chips)]
        for cp in first:
            cp.start()
        passed = []
        for j, chip in enumerate(chips):
            for a in range(n):
                copy(a, 1 + j, (*chip, c), me).wait_recv()
                passed.append(copy(a, 4 + j, (*chip, c), sibling))
                passed[-1].start()
        for a in range(n):
            copy(a, 0, sibling, me).wait_recv()
        for j, chip in enumerate(chips):
            for a in range(n):
                copy(a, 4 + j, (*chip, 1 - c), me).wait_recv()
        for cp in first + passed:
            cp.wait_send()
        for cp in mine:
            cp.wait()

    return pl.pallas_call(
        body, name=name, out_shape=[jax.ShapeDtypeStruct((N_DEV,) + b.shape, b.dtype) for b in blocks],
        in_specs=_hbm_specs(n), out_specs=_hbm_specs(n),
        scratch_shapes=[pltpu.SemaphoreType.DMA((7 * n,)), pltpu.SemaphoreType.DMA((7 * n,)),
                        pltpu.SemaphoreType.DMA((n,))],
    )(*blocks)


def _received_shapes(slabs):
    return [jax.ShapeDtypeStruct((N_DEV,) + s.shape[1:], s.dtype) for s in slabs]


def _exchange_scratch(n):
    return [pltpu.SemaphoreType.DMA((7 * n,)), pltpu.SemaphoreType.DMA((7 * n,)), pltpu.SemaphoreType.DMA((n,))]


def _exchange_ops(src_refs, dst_refs, owners, sems):
    send_sems, recv_sems, local_sems = sems
    n = len(src_refs)

    def guarded(a, dev, fn):
        lo, hi = owners[a]
        if (lo, hi) == (0, N_DEV):
            fn()
        else:
            pl.when((dev >= lo) & (dev < hi))(fn)

    def src(a, dev):
        ref = src_refs[a]
        return ref.at[0] if ref.shape[0] == 1 else ref.at[dev - owners[a][0]]

    def run(sending, waiting):
        x, y, c = _position()
        me = 4 * x + 2 * y + c
        for a in range(n):
            local = lambda a=a: pltpu.make_async_copy(src(a, me), dst_refs[a].at[me], local_sems.at[a])
            if sending:
                guarded(a, me, lambda local=local: local().start())
            for m in range(1, N_DEV):
                px, py, pc = x ^ (m >> 2), y ^ ((m >> 1) & 1), c ^ (m & 1)
                peer = 4 * px + 2 * py + pc
                sem = dict(send_sem=send_sems.at[7 * a + m - 1], recv_sem=recv_sems.at[7 * a + m - 1],
                           device_id=(px, py, pc), device_id_type=MESH)
                send = lambda a=a, peer=peer, sem=sem: pltpu.make_async_remote_copy(
                    src_ref=src(a, peer), dst_ref=dst_refs[a].at[me], **sem)
                recv = lambda a=a, peer=peer, sem=sem: pltpu.make_async_remote_copy(
                    src_ref=src(a, me), dst_ref=dst_refs[a].at[peer], **sem)
                if sending:
                    guarded(a, peer, lambda send=send: send().start())
                if waiting:
                    guarded(a, me, lambda recv=recv: recv().wait_recv())
                    guarded(a, peer, lambda send=send: send().wait_send())
            if waiting:
                guarded(a, me, lambda local=local: local().wait())

    return functools.partial(run, True, False), functools.partial(run, False, True)


def _sum_slabs(r_ref):
    g = r_ref[0].astype(F32)
    for k in range(1, N_DEV):
        g = g + r_ref[k].astype(F32)
    return g


def _adamw(g, w, m, v):
    m_new = ADAM_B1 * m + (1.0 - ADAM_B1) * g
    v_new = ADAM_B2 * v + (1.0 - ADAM_B2) * (g * g)
    m_hat = m_new / (1.0 - ADAM_B1 ** ADAM_STEP)
    v_hat = v_new / (1.0 - ADAM_B2 ** ADAM_STEP)
    return g, -ADAM_LR * (m_hat / (jnp.sqrt(v_hat) + ADAM_EPS) + ADAM_WD * w), m_new, v_new


def _adamw_w_in(recv, w, m, v, slabs, owners):
    rows, cols = w.shape
    tile = W_IN_COL_TILE
    nx = len(slabs)

    def body(r_ref, w_ref, m_ref, v_ref, *refs):
        src_refs, o_refs, dst_refs = refs[:nx], refs[nx:nx + 4], refs[nx + 4:2 * nx + 4]
        start, wait = _exchange_ops(src_refs, dst_refs, owners, refs[2 * nx + 4:])

        @pl.when(pl.program_id(0) == 0)
        def _():
            start()

        for o_ref, val in zip(o_refs, _adamw(_sum_slabs(r_ref), w_ref[...], m_ref[...], v_ref[...])):
            o_ref[...] = val

        @pl.when(pl.program_id(0) == cols // tile - 1)
        def _():
            wait()

    blk = pl.BlockSpec((rows, tile), lambda i: (0, i))
    out = pl.pallas_call(
        body, name="adamw_w_in", grid=(cols // tile,),
        in_specs=[pl.BlockSpec((N_DEV, rows, tile), lambda i: (0, 0, i)), blk, blk, blk] + _hbm_specs(nx),
        out_specs=[blk] * 4 + _hbm_specs(nx),
        out_shape=[jax.ShapeDtypeStruct((rows, cols), F32)] * 4 + _received_shapes(slabs),
        scratch_shapes=_exchange_scratch(nx), compiler_params=_params("arbitrary"))(recv, w, m, v, *slabs)
    return out[:4], out[4:]


def _adamw_misc(recvs, recv_small, recv_norm, params):
    names = list(params)
    flat = [a for n in names for a in params[n]]

    def body(woa_ref, wob_ref, wo_ref, lora_ref, small_ref, norm_ref, *refs):
        p_refs, o_refs = refs[:len(flat)], refs[len(flat):]
        g_small = _sum_slabs(small_ref)
        g_lora = _sum_slabs(lora_ref)
        grads = {"w_out_a": _sum_slabs(woa_ref), "w_out_b": _sum_slabs(wob_ref), "w_out": _sum_slabs(wo_ref),
                 "w_lora_up": g_lora[0], "a_lora_up": g_lora[1], "norm_g": _sum_slabs(norm_ref)}
        for n, (off, size) in SMALL_SLOTS.items():
            grads[n] = g_small[:, off:off + size]
        for i, n in enumerate(names):
            w_ref, m_ref, v_ref = p_refs[3 * i:3 * i + 3]
            for o_ref, val in zip(o_refs[4 * i:4 * i + 4], _adamw(grads[n], w_ref[...], m_ref[...], v_ref[...])):
                o_ref[...] = val
        o_refs[-1][...] = g_small[:, LOSS_SLOT:LOSS_SLOT + 1]

    out = pl.pallas_call(
        body, name="adamw_misc",
        out_shape=[jax.ShapeDtypeStruct(params[n][0].shape, F32) for n in names for _ in range(4)]
        + [jax.ShapeDtypeStruct((1, 1), F32)],
        compiler_params=_params())(*recvs, recv_small, recv_norm, *flat)
    return {n: out[4 * i:4 * i + 4] for i, n in enumerate(names)}, out[-1]


_WT_SEGMENTS = ((0, NA), (NA, NB), (NA + NB + H, NG), (NA + NB, H))


def _split_wt(gathered):
    tile = W_IN_COL_TILE

    def body(g_ref, *o_refs):
        full = jnp.concatenate([g_ref[j] for j in range(N_DEV)], axis=0)
        for o_ref, (row, n) in zip(o_refs, _WT_SEGMENTS):
            seg = full[row:row + n]
            if n < o_ref.shape[0]:
                seg = jnp.concatenate([seg, jnp.zeros((o_ref.shape[0] - n, tile), BF16)], axis=0)
            o_ref[...] = seg

    sizes = (NA, NB, NG, NF)
    return pl.pallas_call(
        body, name="split_wt", grid=(D // tile,),
        in_specs=[pl.BlockSpec((N_DEV, COLS_PER_DEV, tile), lambda i: (0, 0, i))],
        out_specs=[pl.BlockSpec((n, tile), lambda i: (0, i)) for n in sizes],
        out_shape=[jax.ShapeDtypeStruct((n, D), BF16) for n in sizes],
        compiler_params=_params("arbitrary"))(gathered)


def _slab_wt_grad(segments, seg_rows, dev_lo, dev_hi, name):
    tile = W_IN_COL_TILE
    k = len(segments)

    def body(*refs):
        seg_refs, o_ref = refs[:k], refs[k]
        for j in range(dev_lo, dev_hi):
            lo, hi = COLS_PER_DEV * j, COLS_PER_DEV * (j + 1)
            parts = []
            for ref, (row, n) in sorted(zip(seg_refs, seg_rows), key=lambda t: t[1][0]):
                first, last = max(lo, row), min(hi, row + n)
                if first < last:
                    parts.append(ref[first - row:last - row, :])
            o_ref[j - dev_lo] = (parts[0] if len(parts) == 1 else jnp.concatenate(parts, axis=0)).astype(BF16)

    return pl.pallas_call(
        body, name=name, grid=(D // tile,),
        in_specs=[pl.BlockSpec((s.shape[0], tile), lambda i: (0, i)) for s in segments],
        out_specs=pl.BlockSpec((dev_hi - dev_lo, COLS_PER_DEV, tile), lambda i: (0, 0, i)),
        out_shape=jax.ShapeDtypeStruct((dev_hi - dev_lo, COLS_PER_DEV, D), BF16),
        compiler_params=_params("arbitrary"))(*segments)


def _by_cols(a):
    return jnp.moveaxis(a, 0, 1).reshape(a.shape[1], -1)


def _col_slabs(a):
    return jnp.moveaxis(a.reshape(a.shape[0], N_DEV, -1), 1, 0).astype(BF16)


def _pack_small(grads, loss):
    pieces, at = [], 0
    for n, (off, size) in list(SMALL_SLOTS.items()) + [("loss", (LOSS_SLOT, 1))]:
        pieces += [jnp.zeros((off - at,), F32), (loss if n == "loss" else grads[n]).reshape(-1)]
        at = off + size
    return jnp.concatenate(pieces + [jnp.zeros((SMALL_LEN - at,), F32)]).reshape(1, 1, SMALL_LEN)


def _gather_weights(t):
    cast = lambda a: a.astype(BF16)
    loras = jnp.stack([t["w_lora_up"][0], t["a_lora_up"][0]])
    wt, woa, wob, wo, lora = _all_gather(
        [cast(t["w_in"][0].T), cast(t["w_out_a"][0]), cast(t["w_out_b"][0]), cast(t["w_out"][0]), cast(loras)],
        "weight_gather")
    in_a, in_b, in_g, in_f = _split_wt(wt)
    return {"in_a": in_a, "in_b": in_b, "in_g": in_g, "in_f": in_f, "w_out_a": _by_cols(woa), "w_out_b": _by_cols(wob),
            "w_out": wo.reshape(D, D), "w_lora_up": lora[:, 0], "a_lora_up": lora[:, 1]}


def kernel(x, norm_g, w_in, shift_mu, w_lora_up, w0, a_lora_up, a0, k_k, k_a, r_k, lnx_w, lnx_b, f_bias, q_norm_g, k_norm_g, w_out_a, w_out_b, w_out, final_norm_g, loss_target, m_norm_g, m_w_in, m_shift_mu, m_w_lora_up, m_w0, m_a_lora_up, m_a0, m_k_k, m_k_a, m_r_k, m_lnx_w, m_lnx_b, m_f_bias, m_q_norm_g, m_k_norm_g, m_w_out_a, m_w_out_b, m_w_out, m_final_norm_g, v_norm_g, v_w_in, v_shift_mu, v_w_lora_up, v_w0, v_a_lora_up, v_a0, v_k_k, v_k_a, v_r_k, v_lnx_w, v_lnx_b, v_f_bias, v_q_norm_g, v_k_norm_g, v_w_out_a, v_w_out_b, v_w_out, v_final_norm_g):
    names = ("norm_g", "w_in", "shift_mu", "w_lora_up", "w0", "a_lora_up", "a0", "k_k", "k_a", "r_k", "lnx_w", "lnx_b",
             "f_bias", "q_norm_g", "k_norm_g", "w_out_a", "w_out_b", "w_out", "final_norm_g")
    weights = dict(zip(names, (norm_g, w_in, shift_mu, w_lora_up, w0, a_lora_up, a0, k_k, k_a, r_k, lnx_w, lnx_b,
                               f_bias, q_norm_g, k_norm_g, w_out_a, w_out_b, w_out, final_norm_g)))
    m_in = dict(zip(names, (m_norm_g, m_w_in, m_shift_mu, m_w_lora_up, m_w0, m_a_lora_up, m_a0, m_k_k, m_k_a, m_r_k,
                            m_lnx_w, m_lnx_b, m_f_bias, m_q_norm_g, m_k_norm_g, m_w_out_a, m_w_out_b, m_w_out,
                            m_final_norm_g)))
    v_in = dict(zip(names, (v_norm_g, v_w_in, v_shift_mu, v_w_lora_up, v_w0, v_a_lora_up, v_a0, v_k_k, v_k_a, v_r_k,
                            v_lnx_w, v_lnx_b, v_f_bias, v_q_norm_g, v_k_norm_g, v_w_out_a, v_w_out_b, v_w_out,
                            v_final_norm_g)))

    matrices = ("w_out_a", "w_out_b", "w_out", "w_lora_up", "a_lora_up")
    as_2d = lambda n, a: a[0] if n in matrices else a.reshape(1, -1)

    full = _gather_weights(weights)
    dx, dng, recv_wt, recvs, recv_small = _local_step(
        x[0], loss_target[0], full, {n: as_2d(n, weights[n]) for n in ("norm_g",) + tuple(SMALL_SLOTS)})

    res, (recv_norm,) = _adamw_w_in(recv_wt, w_in[0].T, m_w_in[0].T, v_w_in[0].T, (dng[None],), ((0, N_DEV),))
    outs = {"w_in": [r.T[None] for r in res]}
    misc = [n for n in names if n != "w_in"]
    res, loss_sum = _adamw_misc(recvs, recv_small, recv_norm,
                                {n: tuple(as_2d(n, t[n]) for t in (weights, m_in, v_in)) for n in misc})
    for n in misc:
        outs[n] = [r.reshape(weights[n].shape) for r in res[n]]
    return (loss_sum.reshape(()), dx[None], *[outs[n][i] for i in range(4) for n in names])
```

```python
import functools
import math

import jax
import jax.numpy as jnp
from jax import lax
from jax.experimental import pallas as pl
from jax.experimental.pallas import tpu as pltpu

F32 = jnp.float32
BF16 = jnp.bfloat16
HI = lax.Precision.HIGHEST
MESH = pl.DeviceIdType.MESH

N_DEV = 8
D = 1024
H = 8
N = 64
DA = H * N
RANK = 64
NA = 4 * DA + 2 * RANK
NB = 4 * DA
NG = 2 * D
NF = 128
IN_COLS = NA + NB + H + NG
COLS_PER_DEV = IN_COLS // N_DEV
RMS_EPS = 1e-6
LNX_EPS = 64e-5
ATT_SCALE = N ** -0.5

ADAM_LR = 0.001
ADAM_B1 = 0.9
ADAM_B2 = 0.999
ADAM_EPS = 1e-08
ADAM_WD = 0.01
ADAM_STEP = 10

LANES = 128
WKV_CHUNK = 64
TOK_TILE = 256
HEAD_TILE = 128
ATT_TILE = 256
ATT_GROUPS = 8
VMEM_LIMIT = 56 * 1024 * 1024

SMALL_SLOTS = {"final_norm_g": (0, D), "shift_mu": (D, NA), "w0": (3200, DA), "a0": (3712, DA), "k_k": (4224, DA),
               "k_a": (4736, DA), "r_k": (5248, DA), "lnx_w": (5760, DA), "lnx_b": (6272, DA), "q_norm_g": (6784, N),
               "k_norm_g": (6912, N), "f_bias": (7040, H)}
LOSS_SLOT = 7168
SMALL_LEN = 7296
W_IN_COL_TILE = 256
EARLY_FROM = -(-NA // COLS_PER_DEV)


def _params(*sem):
    return pltpu.CompilerParams(dimension_semantics=sem or None, vmem_limit_bytes=VMEM_LIMIT)


def _bdot(a, b):
    return jnp.dot(a.astype(BF16), b.astype(BF16), preferred_element_type=F32)


def _bdot_nt(a, b):
    return lax.dot_general(a.astype(BF16), b.astype(BF16), (((1,), (1,)), ((), ())), preferred_element_type=F32)


def _bdot_tn(a, b):
    return lax.dot_general(a.astype(BF16), b.astype(BF16), (((0,), (0,)), ((), ())), preferred_element_type=F32)


def _sigmoid(x):
    return 1.0 / (1.0 + jnp.exp(-x))


def _softplus(x):
    return jnp.maximum(x, 0.0) + jnp.log(1.0 + jnp.exp(-jnp.abs(x)))


def _heads(ref, col0):
    return jnp.stack([ref[:, col0 + N * h:col0 + N * (h + 1)] for h in range(H)])


def _store_heads(ref, col0, val):
    for h in range(H):
        ref[:, col0 + N * h:col0 + N * (h + 1)] = val[h]


def _lerp(c, s, mu):
    return c + (s - c) * mu


def _rwkv_pre(rc, rs, kc, ks, vc, vs, gc, gs, wdc, wds, adc, ads,
              mu_r, mu_k, mu_v, mu_g, mu_wd, mu_ad, w_up, w0, a_up, a0, k_k, k_a):
    r = _lerp(rc, rs, mu_r)
    k = _lerp(kc, ks, mu_k)
    v = _lerp(vc, vs, mu_v)
    g = _lerp(gc, gs, mu_g)
    wd = _lerp(wdc, wds, mu_wd)
    ad = _lerp(adc, ads, mu_ad)
    t = wd.shape[0]
    bdims = (((2,), (1,)), ((0,), (0,)))
    tw = jnp.broadcast_to(jnp.tanh(wd).astype(BF16)[None], (H, t, RANK))
    z = w0 + lax.dot_general(tw, w_up.astype(BF16), bdims, preferred_element_type=F32)
    w_raw = -_softplus(-z) - 0.5
    lw = -jnp.exp(w_raw)
    row = lax.broadcasted_iota(jnp.int32, (t, t), 0)
    col = lax.broadcasted_iota(jnp.int32, (t, t), 1)
    same_chunk = ((row >= col) & (row // WKV_CHUNK == col // WKV_CHUNK)).astype(F32)
    cl = jnp.einsum("hts,hsn->htn", jnp.broadcast_to(same_chunk[None], (H, t, t)), lw, precision=HI,
                    preferred_element_type=F32)
    adb = jnp.broadcast_to(ad.astype(BF16)[None], (H, t, RANK))
    alr = _sigmoid(a0 + lax.dot_general(adb, a_up.astype(BF16), bdims, preferred_element_type=F32))
    kk = k * k_k
    kk = kk / jnp.maximum(jnp.sqrt(jnp.sum(kk * kk, axis=-1, keepdims=True)), 1e-12)
    k2 = k * (1.0 + (alr - 1.0) * k_a)
    return r, lw, cl, k2, v, -kk, kk * alr, g


_MM_DIMS = {"nn": (((2,), (1,)), ((0,), (0,))), "nt": (((2,), (2,)), ((0,), (0,))), "tn": (((1,), (1,)), ((0,), (0,)))}


def _split(x):
    hi = x.astype(BF16)
    return hi, (x - hi.astype(F32)).astype(BF16)


def _dot3(a, b, kind):
    ah, al = _split(a)
    bh, bl = _split(b)
    dot = functools.partial(lax.dot_general, dimension_numbers=_MM_DIMS[kind], preferred_element_type=F32)
    return dot(ah, bh) + (dot(ah, bl) + dot(al, bh))


@functools.partial(jax.custom_vjp, nondiff_argnums=(2,))
def _mm(a, b, kind):
    return _dot3(a, b, kind)


def _mm_fwd(a, b, kind):
    return _dot3(a, b, kind), (a, b)


def _dot1(a, b, kind):
    return lax.dot_general(a.astype(BF16), b.astype(BF16), dimension_numbers=_MM_DIMS[kind], preferred_element_type=F32)


def _mm_bwd(kind, res, ct):
    a, b = res
    if kind == "nn":
        return _dot1(ct, b, "nt"), _dot1(a, ct, "tn")
    if kind == "nt":
        return _dot1(ct, b, "nn"), _dot1(ct, a, "tn")
    return _dot1(b, ct, "nt"), _dot1(a, ct, "nn")


_mm.defvjp(_mm_fwd, _mm_bwd)


def _chunk_masks(c):
    row = lax.broadcasted_iota(jnp.int32, (c, c), 0)
    col = lax.broadcasted_iota(jnp.int32, (c, c), 1)
    return (row >= col)[None], (row > col)[None], (row == col).astype(F32)[None]


def _wkv_aab(lw, cl, a, b):
    _, strict, _ = _chunk_masks(a.shape[1])
    return jnp.where(strict, _mm(a * jnp.exp(cl - lw), b * jnp.exp(-cl), "nt"), 0.0)


def _tri_inverse(x):
    c = x.shape[1]
    p = _chunk_masks(c)[2] + x
    for _ in range(int(math.log2(c)) - 1):
        x = _mm(x, x, "nn")
        p = p + _mm(p, x, "nn")
    return p


def _wkv_apply(s0, r, lw, cl, k, v, a, b, p):
    c = r.shape[1]
    incl, strict, _ = _chunk_masks(c)
    gi = jnp.exp(-cl)
    left = jnp.concatenate([a * jnp.exp(cl - lw), r * jnp.exp(cl)], axis=1)
    right = jnp.concatenate([b * gi, k * gi], axis=1)
    m = _mm(left, right, "nt")
    z0 = _mm(left, s0, "nt")
    a_ak = jnp.where(strict, m[:, :c, c:], 0.0)
    row = lax.broadcasted_iota(jnp.int32, (c, 2 * c), 0)
    col = lax.broadcasted_iota(jnp.int32, (c, 2 * c), 1)
    a_r = jnp.where((row >= col % c)[None], m[:, c:, :], 0.0)
    sa = _mm(p, z0[:, :c] + _mm(a_ak, v, "nn"), "nn")
    sa_v = jnp.concatenate([sa, v], axis=1)
    y = z0[:, c:] + _mm(a_r, sa_v, "nn")
    s1 = (s0 + _mm(sa_v, right, "tn")) * jnp.exp(cl[:, c - 1:c, :])
    return y, s1


def _rwkv_post(y, r, k2, v, g, lnx_w, lnx_b, r_k):
    mean = jnp.mean(y, axis=-1, keepdims=True)
    yc = y - mean
    var = jnp.mean(yc * yc, axis=-1, keepdims=True)
    yn = yc * lax.rsqrt(var + LNX_EPS) * lnx_w + lnx_b
    bonus = jnp.sum(r * k2 * r_k, axis=-1, keepdims=True) * v
    return (yn + bonus) * (g * _sigmoid(g))


def _fox_pre(q, k, f, q_g, k_g, f_b):
    qn = q * lax.rsqrt(jnp.mean(q * q, axis=-1, keepdims=True) + RMS_EPS) * q_g
    kn = k * lax.rsqrt(jnp.mean(k * k, axis=-1, keepdims=True) + RMS_EPS) * k_g
    x = f + f_b
    return qn, kn, jnp.minimum(x, 0.0) - jnp.log(1.0 + jnp.exp(-jnp.abs(x)))


def _rms_fwd(x, g):
    s = x.shape[0]

    def body(x_ref, g_ref, h_ref):
        xv = x_ref[...]
        h_ref[...] = (xv * lax.rsqrt(jnp.mean(xv * xv, axis=-1, keepdims=True) + RMS_EPS) * g_ref[...]).astype(BF16)

    return pl.pallas_call(
        body, name="rms_fwd", grid=(s // TOK_TILE,),
        in_specs=[pl.BlockSpec((TOK_TILE, D), lambda i: (i, 0)), pl.BlockSpec((1, D), lambda i: (0, 0))],
        out_specs=pl.BlockSpec((TOK_TILE, D), lambda i: (i, 0)),
        out_shape=jax.ShapeDtypeStruct((s, D), BF16), compiler_params=_params("arbitrary"))(x, g)


def _proj(h, wt, name):
    s, n = h.shape[0], wt.shape[0]

    def body(h_ref, w_ref, o_ref):
        o_ref[...] = _bdot_nt(h_ref[...], w_ref[...])

    return pl.pallas_call(
        body, name=name, grid=(s // TOK_TILE,),
        in_specs=[pl.BlockSpec((TOK_TILE, D), lambda i: (i, 0)), pl.BlockSpec((n, D), lambda i: (0, 0))],
        out_specs=pl.BlockSpec((TOK_TILE, n), lambda i: (i, 0)),
        out_shape=jax.ShapeDtypeStruct((s, n), F32), compiler_params=_params("arbitrary"))(h, wt)


def _proj_wgrad(h, du, name):
    s, n = du.shape

    def body(h_ref, du_ref, o_ref):
        @pl.when(pl.program_id(0) == 0)
        def _():
            o_ref[...] = jnp.zeros_like(o_ref)

        o_ref[...] += _bdot_tn(du_ref[...], h_ref[...])

    return pl.pallas_call(
        body, name=name, grid=(s // TOK_TILE,),
        in_specs=[pl.BlockSpec((TOK_TILE, D), lambda i: (i, 0)), pl.BlockSpec((TOK_TILE, n), lambda i: (i, 0))],
        out_specs=pl.BlockSpec((n, D), lambda i: (0, 0)),
        out_shape=jax.ShapeDtypeStruct((n, D), F32), compiler_params=_params("arbitrary"))(h, du)


def _proj_xgrad(x, g, dx2, dus, ws, slabs, owners, landing):
    s = x.shape[0]
    tile = HEAD_TILE
    k = len(dus)
    nx = len(slabs)
    carried = [a for a in range(nx) if landing[a] is not None]
    n_in = 3 + 2 * k + nx + len(carried)

    def body(*refs):
        x_ref, g_ref, dx2_ref = refs[:3]
        du_refs, w_refs = refs[3:3 + k], refs[3 + k:3 + 2 * k]
        src_refs = refs[3 + 2 * k:3 + 2 * k + nx]
        dx_ref, dg_ref = refs[n_in:n_in + 2]
        dst_refs = refs[n_in + 2:n_in + 2 + nx]
        start, wait = _exchange_ops(src_refs, dst_refs, owners, refs[n_in + 2 + nx:])

        @pl.when(pl.program_id(0) == 0)
        def _():
            dg_ref[...] = jnp.zeros_like(dg_ref)
            start()

        dh = _bdot(du_refs[0][...], w_refs[0][...])
        for du_ref, w_ref in zip(du_refs[1:], w_refs[1:]):
            dh += _bdot(du_ref[...], w_ref[...])
        xv = x_ref[...]
        rs = lax.rsqrt(jnp.mean(xv * xv, axis=-1, keepdims=True) + RMS_EPS)
        xn = xv * rs
        dg_ref[...] += jnp.sum(dh * xn, axis=0, keepdims=True)
        dxn = dh * g_ref[...]
        dx_ref[...] = rs * (dxn - xn * jnp.mean(dxn * xn, axis=-1, keepdims=True)) + dx2_ref[...]

        @pl.when(pl.program_id(0) == s // tile - 1)
        def _():
            wait()

    tok = lambda n: pl.BlockSpec((tile, n), lambda i: (i, 0))
    fixed = lambda a: pl.BlockSpec(a.shape, lambda i: (0,) * a.ndim)
    out = pl.pallas_call(
        body, name="proj_xgrad", grid=(s // tile,),
        in_specs=([tok(D), fixed(g), tok(D)] + [tok(du.shape[1]) for du in dus] + [fixed(w) for w in ws]
                  + _hbm_specs(nx + len(carried))),
        out_specs=[tok(D), pl.BlockSpec((1, D), lambda i: (0, 0))] + _hbm_specs(nx),
        out_shape=[jax.ShapeDtypeStruct((s, D), F32), jax.ShapeDtypeStruct((1, D), F32)] + _received_shapes(slabs),
        input_output_aliases={3 + 2 * k + nx + i: 2 + a for i, a in enumerate(carried)},
        scratch_shapes=_exchange_scratch(nx),
        compiler_params=_params("arbitrary"))(x, g, dx2, *dus, *ws, *slabs, *[landing[a] for a in carried])
    return out[0], out[1], out[2:]


def _tail(x, target, ya, o, ub, ug, w_oa, w_ob, w_o, fg):
    s = x.shape[0]
    tile = TOK_TILE

    def body(x_ref, t_ref, ya_ref, o_ref, gb_ref, ug_ref, woa_ref, wob_ref, wo_ref, fg_ref,
             loss_ref, dfg_ref, dwo_ref, dwoa_ref, dwob_ref, dx2_ref, dya_ref, do_ref, dgb_ref, dug_ref):
        @pl.when(pl.program_id(0) == 0)
        def _():
            for r in (loss_ref, dfg_ref, dwo_ref, dwoa_ref, dwob_ref):
                r[...] = jnp.zeros_like(r)

        ya_v = ya_ref[...]
        gate_b = gb_ref[...]
        sg_b = _sigmoid(gate_b)
        silu_b = gate_b * sg_b
        o_v = jnp.concatenate([o_ref[h] for h in range(H)], axis=-1)
        yb_v = o_v * silu_b
        big_a = _bdot(ya_v, woa_ref[...])
        big_b = _bdot(yb_v, wob_ref[...])
        sa = _sigmoid(ug_ref[:, :D])
        sb = _sigmoid(ug_ref[:, D:])
        merged = sa * big_a + sb * big_b
        x2 = x_ref[...] + _bdot(merged, wo_ref[...])
        rs = lax.rsqrt(jnp.mean(x2 * x2, axis=-1, keepdims=True) + RMS_EPS)
        xn = x2 * rs
        err = xn * fg_ref[...] - t_ref[...]
        loss_ref[...] += (0.5 / D) * jnp.sum(err * err)
        dout = err * (1.0 / D)
        dfg_ref[...] += jnp.sum(dout * xn, axis=0, keepdims=True)
        dxn = dout * fg_ref[...]
        dx2 = rs * (dxn - xn * jnp.mean(dxn * xn, axis=-1, keepdims=True))
        dx2_ref[...] = dx2
        dwo_ref[...] += _bdot_tn(merged, dx2)
        dmerged = _bdot_nt(dx2, wo_ref[...])
        dbig_a = dmerged * sa
        dbig_b = dmerged * sb
        dug_ref[:, :D] = dmerged * big_a * sa * (1.0 - sa)
        dug_ref[:, D:] = dmerged * big_b * sb * (1.0 - sb)
        dwoa_ref[...] += _bdot_tn(ya_v, dbig_a)
        dwob_ref[...] += _bdot_tn(yb_v, dbig_b)
        dya_ref[...] = _bdot_nt(dbig_a, woa_ref[...])
        dyb = _bdot_nt(dbig_b, wob_ref[...])
        dgb_ref[...] = dyb * o_v * (sg_b * (1.0 + gate_b * (1.0 - sg_b)))
        _dov = dyb * silu_b
        for h in range(H):
            do_ref[h] = _dov[:, N * h:N * (h + 1)]

    tok = lambda n: pl.BlockSpec((tile, n), lambda i: (i, 0))
    hm = pl.BlockSpec((H, tile, N), lambda i: (0, i, 0))
    fixed = lambda shape: pl.BlockSpec(shape, lambda i: (0,) * len(shape))
    f32 = lambda *shape: jax.ShapeDtypeStruct(shape, F32)
    return pl.pallas_call(
        body, name="tail", grid=(s // tile,),
        in_specs=[tok(D), tok(D), tok(DA), hm, pl.BlockSpec((tile, DA), lambda i: (i, 3)), tok(NG),
                  fixed((DA, D)), fixed((DA, D)), fixed((D, D)), fixed((1, D))],
        out_specs=[fixed((1, 1)), fixed((1, D)), fixed((D, D)), fixed((DA, D)), fixed((DA, D)),
                   tok(D), tok(DA), hm, tok(DA), tok(NG)],
        out_shape=[f32(1, 1), f32(1, D), f32(D, D), f32(DA, D), f32(DA, D),
                   f32(s, D), f32(s, DA), f32(H, s, N), f32(s, DA), f32(s, NG)],
        compiler_params=_params("arbitrary"))(x, target, ya, o, ub, ug, w_oa, w_ob, w_o, fg)


_PRE_PARAM_SHAPES = ((H, 1, N),) * 4 + ((1, RANK),) * 2 + ((H, RANK, N), (H, 1, N), (H, RANK, N), (H, 1, N), (H, 1, N),
                                                              (H, 1, N))


def _pre_operands(ua_ref, prev_ref, first):
    cur = ua_ref[...]
    t = cur.shape[0]
    prev_row = jnp.where(first, 0.0, prev_ref[7:8, :])
    rows = lax.broadcasted_iota(jnp.int32, cur.shape, 0)
    sh = jnp.where(rows == 0, prev_row, pltpu.roll(cur, 1, axis=0))
    ops = []
    for c0 in (0, DA, 2 * DA, 3 * DA + 2 * RANK):
        ops.append(jnp.stack([cur[:, c0 + N * h:c0 + N * (h + 1)] for h in range(H)]))
        ops.append(jnp.stack([sh[:, c0 + N * h:c0 + N * (h + 1)] for h in range(H)]))
    for c0 in (3 * DA, 3 * DA + RANK):
        ops.append(cur[:, c0:c0 + RANK])
        ops.append(sh[:, c0:c0 + RANK])
    del t
    return ops


def _ua_specs(tile, order):
    blocks = tile // 8
    return [pl.BlockSpec((tile, NA), lambda i: (order(i), 0)),
            pl.BlockSpec((8, NA), lambda i: (jnp.maximum(order(i) * blocks - 1, 0), 0))]


def _rwkv_pre_fwd(ua, pre_params):
    s = ua.shape[0]
    tile = HEAD_TILE

    def body(ua_ref, prev_ref, *refs):
        p_refs, o_refs = refs[:len(pre_params)], refs[len(pre_params):]
        ops = _pre_operands(ua_ref, prev_ref, pl.program_id(0) == 0)
        outs = _rwkv_pre(*ops, *[p[...] for p in p_refs])
        for o_ref, val in zip(o_refs, outs):
            o_ref[...] = val

    hm = pl.BlockSpec((H, tile, N), lambda i: (0, i, 0))
    return pl.pallas_call(
        body, name="rwkv_pre_fwd", grid=(s // tile,),
        in_specs=_ua_specs(tile, lambda i: i) + [pl.BlockSpec(p.shape, lambda i, nd=p.ndim: (0,) * nd) for p in pre_params],
        out_specs=[hm] * 8, out_shape=[jax.ShapeDtypeStruct((H, s, N), F32)] * 8,
        compiler_params=_params("arbitrary"))(ua, ua, *pre_params)


def _rwkv_pre_bwd(ua, pre_params, cots):
    s = ua.shape[0]
    tile = HEAD_TILE
    nt = s // tile
    n_p = len(pre_params)

    def body(ua_ref, prev_ref, *refs):
        p_refs, c_refs = refs[:n_p], refs[n_p:n_p + 11]
        dua_ref = refs[n_p + 11]
        dp_refs = refs[n_p + 12:n_p + 12 + n_p]
        carry_ref = refs[-1]
        i = pl.program_id(0)

        @pl.when(i == 0)
        def _():
            carry_ref[...] = jnp.zeros_like(carry_ref)
            for r in dp_refs:
                r[...] = jnp.zeros_like(r)

        ops = _pre_operands(ua_ref, prev_ref, i == nt - 1)
        _, vjp = jax.vjp(_rwkv_pre, *ops, *[p[...] for p in p_refs])
        c = [r[...] for r in c_refs]
        grads = vjp((c[0] + c[1], c[2], c[3], c[4] + c[5], c[6] + c[7], c[8], c[9], c[10]))
        d_ops, d_par = grads[:12], grads[12:]
        for r, val in zip(dp_refs, d_par):
            r[...] += val
        d_cur = jnp.concatenate([d_ops[0][h] for h in range(H)] + [d_ops[2][h] for h in range(H)]
                                + [d_ops[4][h] for h in range(H)] + [d_ops[8], d_ops[10]]
                                + [d_ops[6][h] for h in range(H)], axis=-1)
        d_sh = jnp.concatenate([d_ops[1][h] for h in range(H)] + [d_ops[3][h] for h in range(H)]
                               + [d_ops[5][h] for h in range(H)] + [d_ops[9], d_ops[11]]
                               + [d_ops[7][h] for h in range(H)], axis=-1)
        rows = lax.broadcasted_iota(jnp.int32, d_sh.shape, 0)
        dua_ref[...] = d_cur + jnp.where(rows == tile - 1, carry_ref[...], pltpu.roll(d_sh, tile - 1, axis=0))
        carry_ref[...] = d_sh[0:1, :]

    rev = lambda i: nt - 1 - i
    hm = pl.BlockSpec((H, tile, N), lambda i: (0, rev(i), 0))
    fixed = [pl.BlockSpec(p.shape, lambda i, nd=p.ndim: (0,) * nd) for p in pre_params]
    return pl.pallas_call(
        body, name="rwkv_pre_bwd", grid=(nt,),
        in_specs=_ua_specs(tile, rev) + fixed + [hm] * 11,
        out_specs=[pl.BlockSpec((tile, NA), lambda i: (rev(i), 0))] + fixed,
        out_shape=[jax.ShapeDtypeStruct((s, NA), F32)] + [jax.ShapeDtypeStruct(p.shape, F32) for p in pre_params],
        scratch_shapes=[pltpu.VMEM((1, NA), F32)],
        compiler_params=_params("arbitrary"))(ua, ua, *pre_params, *cots)


def _wkv_fwd(seq):
    s = seq[0].shape[1]
    nc = s // WKV_CHUNK

    def body(r_ref, lw_ref, cl_ref, k_ref, v_ref, a_ref, b_ref, y_ref, ck_ref, p_ref, state):
        @pl.when(pl.program_id(0) == 0)
        def _():
            state[...] = jnp.zeros_like(state)

        s0 = state[...]
        ck_ref[0] = s0
        p = _tri_inverse(_wkv_aab(lw_ref[...], cl_ref[...], a_ref[...], b_ref[...]))
        p_ref[0] = p
        y, s1 = _wkv_apply(s0, r_ref[...], lw_ref[...], cl_ref[...], k_ref[...], v_ref[...], a_ref[...], b_ref[...], p)
        y_ref[...] = y
        state[...] = s1

    hm = pl.BlockSpec((H, WKV_CHUNK, N), lambda c: (0, c, 0))
    per_chunk = lambda m: pl.BlockSpec((1, H, m, m), lambda c: (c, 0, 0, 0))
    return pl.pallas_call(
        body, name="wkv_fwd", grid=(nc,), in_specs=[hm] * 7,
        out_specs=[hm, per_chunk(N), per_chunk(WKV_CHUNK)],
        out_shape=[jax.ShapeDtypeStruct((H, s, N), F32), jax.ShapeDtypeStruct((nc, H, N, N), F32),
                   jax.ShapeDtypeStruct((nc, H, WKV_CHUNK, WKV_CHUNK), F32)],
        scratch_shapes=[pltpu.VMEM((H, N, N), F32)], compiler_params=_params("arbitrary"))(*seq)


def _wkv_bwd(seq, ckpt, pinv, dy, slabs, owners):
    s = seq[0].shape[1]
    nc = s // WKV_CHUNK
    nx = len(slabs)

    def body(r_ref, lw_ref, cl_ref, k_ref, v_ref, a_ref, b_ref, ck_ref, p_ref, dy_ref, *refs):
        src_refs, d_refs, dst_refs = refs[:nx], refs[nx:nx + 7], refs[nx + 7:2 * nx + 7]
        dstate = refs[2 * nx + 7]
        start, wait = _exchange_ops(src_refs, dst_refs, owners, refs[2 * nx + 8:])

        @pl.when(pl.program_id(0) == 0)
        def _():
            dstate[...] = jnp.zeros_like(dstate)
            start()

        p = p_ref[0]
        lw, cl, a, b = lw_ref[...], cl_ref[...], a_ref[...], b_ref[...]
        _, vjp = jax.vjp(_wkv_apply, ck_ref[0], r_ref[...], lw, cl, k_ref[...], v_ref[...], a, b, p)
        ds0, dr, dlw, dcl, dk, dv, da, db, dp = vjp((dy_ref[...], dstate[...]))
        dstate[...] = ds0
        _, vjp_x = jax.vjp(_wkv_aab, lw, cl, a, b)
        dlw2, dcl2, da2, db2 = vjp_x(_mm(_mm(p, dp, "tn"), p, "nt"))
        for d_ref, val in zip(d_refs, (dr, dlw + dlw2, dcl + dcl2, dk, dv, da + da2, db + db2)):
            d_ref[...] = val

        @pl.when(pl.program_id(0) == nc - 1)
        def _():
            wait()

    hm = pl.BlockSpec((H, WKV_CHUNK, N), lambda c: (0, nc - 1 - c, 0))
    per_chunk = lambda m: pl.BlockSpec((1, H, m, m), lambda c: (nc - 1 - c, 0, 0, 0))
    out = pl.pallas_call(
        body, name="wkv_bwd", grid=(nc,),
        in_specs=[hm] * 7 + [per_chunk(N), per_chunk(WKV_CHUNK), hm] + _hbm_specs(nx),
        out_specs=[hm] * 7 + _hbm_specs(nx),
        out_shape=[jax.ShapeDtypeStruct((H, s, N), F32)] * 7 + _received_shapes(slabs),
        scratch_shapes=[pltpu.VMEM((H, N, N), F32)] + _exchange_scratch(nx),
        compiler_params=_params("arbitrary"))(*seq, ckpt, pinv, dy, *slabs)
    return out[:7], out[7:]


def _rwkv_post_fwd(y, r, k2, v, g, post_params):
    s = y.shape[1]
    tile = HEAD_TILE

    def body(y_ref, r_ref, k_ref, v_ref, g_ref, w_ref, b_ref, rk_ref, o_ref):
        out = _rwkv_post(y_ref[...], r_ref[...], k_ref[...], v_ref[...], g_ref[...], w_ref[...], b_ref[...],
                         rk_ref[...])
        o_ref[...] = jnp.concatenate([out[h] for h in range(H)], axis=-1)

    hm = pl.BlockSpec((H, tile, N), lambda i: (0, i, 0))
    par = pl.BlockSpec((H, 1, N), lambda i: (0, 0, 0))
    return pl.pallas_call(
        body, name="rwkv_post_fwd", grid=(s // tile,), in_specs=[hm] * 5 + [par] * 3,
        out_specs=pl.BlockSpec((tile, DA), lambda i: (i, 0)), out_shape=jax.ShapeDtypeStruct((s, DA), F32),
        compiler_params=_params("arbitrary"))(y, r, k2, v, g, *post_params)


def _rwkv_post_bwd(y, r, k2, v, g, post_params, dya):
    s = y.shape[1]
    tile = HEAD_TILE

    def body(y_ref, r_ref, k_ref, v_ref, g_ref, w_ref, b_ref, rk_ref, dya_ref, *d_refs):
        @pl.when(pl.program_id(0) == 0)
        def _():
            for ref in d_refs[5:]:
                ref[...] = jnp.zeros_like(ref)

        _, vjp = jax.vjp(_rwkv_post, y_ref[...], r_ref[...], k_ref[...], v_ref[...], g_ref[...], w_ref[...],
                         b_ref[...], rk_ref[...])
        grads = vjp(jnp.stack([dya_ref[:, N * h:N * (h + 1)] for h in range(H)]))
        for ref, val in zip(d_refs[:5], grads[:5]):
            ref[...] = val
        for ref, val in zip(d_refs[5:], grads[5:]):
            ref[...] += val

    hm = pl.BlockSpec((H, tile, N), lambda i: (0, i, 0))
    par = pl.BlockSpec((H, 1, N), lambda i: (0, 0, 0))
    return pl.pallas_call(
        body, name="rwkv_post_bwd", grid=(s // tile,),
        in_specs=[hm] * 5 + [par] * 3 + [pl.BlockSpec((tile, DA), lambda i: (i, 0))],
        out_specs=[hm] * 5 + [par] * 3,
        out_shape=[jax.ShapeDtypeStruct((H, s, N), F32)] * 5 + [jax.ShapeDtypeStruct((H, 1, N), F32)] * 3,
        compiler_params=_params("arbitrary"))(y, r, k2, v, g, *post_params, dya)


def _tri(t):
    return (lax.broadcasted_iota(jnp.int32, (t, t), 0) >= lax.broadcasted_iota(jnp.int32, (t, t), 1)).astype(F32)


def _fox_pre_fwd(ub, uf, q_g, k_g, f_b):
    s = ub.shape[0]
    tile = HEAD_TILE

    def body(ub_ref, uf_ref, qg_ref, kg_ref, fb_ref, q_ref, k_ref, v_ref, cum_ref, carry):
        @pl.when(pl.program_id(0) == 0)
        def _():
            carry[...] = jnp.zeros_like(carry)

        qn, kn, logf = _fox_pre(_heads(ub_ref, 0), _heads(ub_ref, DA), uf_ref[...], qg_ref[...], kg_ref[...],
                                fb_ref[...])
        q_ref[...] = qn
        k_ref[...] = kn
        v_ref[...] = _heads(ub_ref, 2 * DA)
        cum = jnp.dot(_tri(tile), logf, precision=HI, preferred_element_type=F32) + carry[...]
        cum_ref[...] = cum
        carry[...] = cum[tile - 1:tile, :]

    hm = pl.BlockSpec((H, tile, N), lambda i: (0, i, 0))
    fixed = lambda shape: pl.BlockSpec(shape, lambda i: (0,) * len(shape))
    return pl.pallas_call(
        body, name="fox_pre_fwd", grid=(s // tile,),
        in_specs=[pl.BlockSpec((tile, NB), lambda i: (i, 0)), pl.BlockSpec((tile, NF), lambda i: (i, 0)),
                  fixed((1, 1, N)), fixed((1, 1, N)), fixed((1, NF))],
        out_specs=[hm] * 3 + [pl.BlockSpec((tile, NF), lambda i: (i, 0))],
        out_shape=[jax.ShapeDtypeStruct((H, s, N), F32)] * 3 + [jax.ShapeDtypeStruct((s, NF), F32)],
        scratch_shapes=[pltpu.VMEM((1, NF), F32)], compiler_params=_params("arbitrary"))(ub, uf, q_g, k_g, f_b)


def _fox_pre_bwd(ub, uf, q_g, k_g, f_b, dqn, dkn, dvf, dgate, dcum_q, dcum_k):
    s = ub.shape[0]
    tile = HEAD_TILE
    nt = s // tile

    def body(ub_ref, uf_ref, qg_ref, kg_ref, fb_ref, dq_ref, dk_ref, dv_ref, dgate_ref, dcq_ref, dck_ref,
             dub_ref, duf_ref, dqg_ref, dkg_ref, dfb_ref, carry):
        @pl.when(pl.program_id(0) == 0)
        def _():
            carry[...] = jnp.zeros_like(carry)
            for ref in (dqg_ref, dkg_ref, dfb_ref):
                ref[...] = jnp.zeros_like(ref)

        dcum = dcq_ref[...] + dck_ref[...]
        dlogf = lax.dot_general(_tri(tile), dcum, (((0,), (0,)), ((), ())), precision=HI,
                                preferred_element_type=F32) + carry[...]
        carry[...] = dlogf[0:1, :]
        _, vjp = jax.vjp(_fox_pre, _heads(ub_ref, 0), _heads(ub_ref, DA), uf_ref[...], qg_ref[...], kg_ref[...],
                         fb_ref[...])
        d_q, d_k, d_f, d_qg, d_kg, d_fb = vjp((dq_ref[...], dk_ref[...], dlogf))
        _store_heads(dub_ref, 0, d_q)
        _store_heads(dub_ref, DA, d_k)
        _store_heads(dub_ref, 2 * DA, dv_ref[...])
        dub_ref[:, 3 * DA:] = dgate_ref[...]
        duf_ref[...] = d_f
        dqg_ref[...] += d_qg
        dkg_ref[...] += d_kg
        dfb_ref[...] += d_fb

    rev = lambda i: nt - 1 - i
    hm = pl.BlockSpec((H, tile, N), lambda i: (0, rev(i), 0))
    tok = lambda n: pl.BlockSpec((tile, n), lambda i: (rev(i), 0))
    fixed = lambda shape: pl.BlockSpec(shape, lambda i: (0,) * len(shape))
    return pl.pallas_call(
        body, name="fox_pre_bwd", grid=(nt,),
        in_specs=[tok(NB), tok(NF), fixed((1, 1, N)), fixed((1, 1, N)), fixed((1, NF)), hm, hm, hm, tok(DA), tok(NF),
                  tok(NF)],
        out_specs=[tok(NB), tok(NF), fixed((1, 1, N)), fixed((1, 1, N)), fixed((1, NF))],
        out_shape=[jax.ShapeDtypeStruct((s, NB), F32), jax.ShapeDtypeStruct((s, NF), F32),
                   jax.ShapeDtypeStruct((1, 1, N), F32), jax.ShapeDtypeStruct((1, 1, N), F32),
                   jax.ShapeDtypeStruct((1, NF), F32)],
        scratch_shapes=[pltpu.VMEM((1, NF), F32)],
        compiler_params=_params("arbitrary"))(ub, uf, q_g, k_g, f_b, dqn, dkn, dvf, dgate, dcum_q, dcum_k)


def _att_groups(s):
    blocks = s // ATT_TILE
    per = max(1, blocks // ATT_GROUPS)
    return per, blocks // per


def _att_logits(q_bf, k, cq, ck, qi):
    tq, sk = q_bf.shape[0], k.shape[0]
    logits = _bdot_nt(q_bf, k) * ATT_SCALE + cq - ck
    rows = qi * tq + lax.broadcasted_iota(jnp.int32, (tq, sk), 0)
    mask = rows >= lax.broadcasted_iota(jnp.int32, (tq, sk), 1)
    return jnp.where(mask, logits, -1e30), mask


def _fox_attn_fwd(q, k, v, cum_q, cum_k):
    s = q.shape[1]
    t = ATT_TILE
    per, groups = _att_groups(s)

    def body(q_ref, k_ref, v_ref, cq_ref, ck_ref, o_ref, lse_ref):
        qi = pl.program_id(1)
        for g in range(groups):
            @pl.when(qi // per == g)
            def _(n=(g + 1) * per * t):
                logits, _ = _att_logits(q_ref[0].astype(BF16), k_ref[0, :n, :], cq_ref[0], ck_ref[0, :, :n], qi)
                m = jnp.max(logits, axis=-1, keepdims=True)
                p = jnp.exp(logits - m)
                l = jnp.sum(p, axis=-1, keepdims=True)
                o_ref[0] = _bdot(p, v_ref[0, :n, :]) / l
                lse_ref[0] = m + jnp.log(l)

    qb = pl.BlockSpec((1, t, N), lambda h, i: (h, i, 0))
    kb = pl.BlockSpec((1, s, N), lambda h, i: (h, 0, 0))
    return pl.pallas_call(
        body, name="fox_attn_fwd", grid=(H, s // t),
        in_specs=[qb, kb, kb, pl.BlockSpec((1, t, 1), lambda h, i: (h, i, 0)),
                  pl.BlockSpec((1, 1, s), lambda h, i: (h, 0, 0))],
        out_specs=[qb, pl.BlockSpec((1, t, 1), lambda h, i: (h, i, 0))],
        out_shape=[jax.ShapeDtypeStruct((H, s, N), F32), jax.ShapeDtypeStruct((H, s, 1), F32)],
        compiler_params=_params("arbitrary", "arbitrary"))(q, k, v, cum_q, cum_k)


def _fox_attn_bwd(q, k, v, cum_q, cum_k, o, lse, do):
    s = q.shape[1]
    t = ATT_TILE
    per, groups = _att_groups(s)

    def body(q_ref, k_ref, v_ref, cq_ref, ck_ref, o_ref, lse_ref, do_ref, dq_ref, dk_ref, dv_ref, dcq_ref, dck_ref):
        qi = pl.program_id(1)

        @pl.when(qi == 0)
        def _():
            for ref in (dk_ref, dv_ref, dck_ref):
                ref[...] = jnp.zeros_like(ref)

        for g in range(groups):
            @pl.when(qi // per == g)
            def _(n=(g + 1) * per * t):
                q_bf, do_bf = q_ref[0].astype(BF16), do_ref[0].astype(BF16)
                kv = k_ref[0, :n, :]
                logits, mask = _att_logits(q_bf, kv, cq_ref[0], ck_ref[0, :, :n], qi)
                p = jnp.where(mask, jnp.exp(logits - lse_ref[0]), 0.0)
                delta = jnp.sum(do_ref[0] * o_ref[0], axis=-1, keepdims=True)
                ds = p * (_bdot_nt(do_bf, v_ref[0, :n, :]) - delta)
                dq_ref[0] = _bdot(ds, kv) * ATT_SCALE
                dk_ref[0, :n, :] += _bdot_tn(ds, q_bf) * ATT_SCALE
                dv_ref[0, :n, :] += _bdot_tn(p, do_bf)
                dcq_ref[0] = jnp.sum(ds, axis=-1, keepdims=True)
                dck_ref[0, :, :n] -= jnp.sum(ds, axis=0, keepdims=True)

    qb = pl.BlockSpec((1, t, N), lambda h, i: (h, i, 0))
    kb = pl.BlockSpec((1, s, N), lambda h, i: (h, 0, 0))
    cqb = pl.BlockSpec((1, t, 1), lambda h, i: (h, i, 0))
    ckb = pl.BlockSpec((1, 1, s), lambda h, i: (h, 0, 0))
    f32 = lambda *shape: jax.ShapeDtypeStruct(shape, F32)
    return pl.pallas_call(
        body, name="fox_attn_bwd", grid=(H, s // t),
        in_specs=[qb, kb, kb, cqb, ckb, qb, cqb, qb], out_specs=[qb, kb, kb, cqb, ckb],
        out_shape=[f32(H, s, N), f32(H, s, N), f32(H, s, N), f32(H, s, 1), f32(H, 1, s)],
        compiler_params=_params("arbitrary", "arbitrary"))(q, k, v, cum_q, cum_k, o, lse, do)


def _head_param(p):
    return p.reshape(H, 1, N)


def _local_step(x, target, w, p):
    mu = p["shift_mu"]
    pre_params = (_head_param(mu[:, 0:DA]), _head_param(mu[:, DA:2 * DA]), _head_param(mu[:, 2 * DA:3 * DA]),
                  _head_param(mu[:, 3 * DA + 2 * RANK:]), mu[:, 3 * DA:3 * DA + RANK],
                  mu[:, 3 * DA + RANK:3 * DA + 2 * RANK],
                  w["w_lora_up"].astype(F32), _head_param(p["w0"]), w["a_lora_up"].astype(F32), _head_param(p["a0"]),
                  _head_param(p["k_k"]), _head_param(p["k_a"]))
    post_params = (_head_param(p["lnx_w"]), _head_param(p["lnx_b"]), _head_param(p["r_k"]))
    q_g, k_g = p["q_norm_g"].reshape(1, 1, N), p["k_norm_g"].reshape(1, 1, N)
    f_b = jnp.pad(p["f_bias"], ((0, 0), (0, NF - H)))
    fg = p["final_norm_g"].reshape(1, D)

    h = _rms_fwd(x, p["norm_g"])
    ua = _proj(h, w["in_a"], "proj_a")
    ub = _proj(h, w["in_b"], "proj_b")
    ug = _proj(h, w["in_g"], "proj_g")
    uf = _proj(h, w["in_f"], "proj_f")
    r, lw, cl, k2, v, av, bv, gg = _rwkv_pre_fwd(ua, pre_params)
    y, ckpt, pinv = _wkv_fwd((r, lw, cl, k2, v, av, bv))
    ya = _rwkv_post_fwd(y, r, k2, v, gg, post_params)
    qn, kn, vf, cum = _fox_pre_fwd(ub, uf, q_g, k_g, f_b)
    cum_t = cum[:, :H].T
    cum_q, cum_k = cum_t[:, :, None], cum_t[:, None, :]
    o, lse = _fox_attn_fwd(qn, kn, vf, cum_q, cum_k)

    (loss, dfg, dwo, dwoa, dwob, dx2, dya, do, dgate_b, dug) = _tail(
        x, target, ya, o, ub, ug, w["w_out_a"], w["w_out_b"], w["w_out"], fg)
    dqn, dkn, dvf, dcq, dck = _fox_attn_bwd(qn, kn, vf, cum_q, cum_k, o, lse, do)
    pad_f = lambda a: jnp.pad(a.T, ((0, 0), (0, NF - H)))
    dub, duf, dqg, dkg, dfb = _fox_pre_bwd(ub, uf, q_g, k_g, f_b, dqn, dkn, dvf, dgate_b,
                                           pad_f(dcq[:, :, 0]), pad_f(dck.reshape(H, -1)))
    dwt_b, dwt_g, dwt_f = (_proj_wgrad(h, du, name) for du, name in ((dub, "wgrad_b"), (dug, "wgrad_g"), (duf, "wgrad_f")))
    dy, dr_p, dk_p, dv_p, dgg, dlnw, dlnb, drk = _rwkv_post_bwd(y, r, k2, v, gg, post_params, dya)

    everyone = (0, N_DEV)
    early = _slab_wt_grad((dwt_b, dwt_g, dwt_f), (_WT_SEGMENTS[1], _WT_SEGMENTS[2], _WT_SEGMENTS[3]), EARLY_FROM, N_DEV,
                          "slab_wt_early")
    (dr_s, dlw, dcl, dk_s, dv_s, dav, dbv), (recv_wt, recv_woa, recv_wob, recv_wo) = _wkv_bwd(
        (r, lw, cl, k2, v, av, bv), ckpt, pinv, dy,
        (early, _col_slabs(dwoa), _col_slabs(dwob), dwo.astype(BF16).reshape(N_DEV, D // N_DEV, D)),
        ((EARLY_FROM, N_DEV), everyone, everyone, everyone))
    pre_out = _rwkv_pre_bwd(ua, pre_params, (dr_s, dr_p, dlw, dcl, dk_s, dk_p, dv_s, dv_p, dav, dbv, dgg))
    dua, dpre = pre_out[0], pre_out[1:]
    dwt_a = _proj_wgrad(h, dua, "wgrad_a")

    flat = lambda a: a.reshape(1, -1)
    small = {
        "final_norm_g": dfg, "w0": dpre[7], "a0": dpre[9], "k_k": dpre[10], "k_a": dpre[11], "r_k": drk, "lnx_w": dlnw,
        "lnx_b": dlnb, "q_norm_g": dqg, "k_norm_g": dkg, "f_bias": dfb[:, :H],
        "shift_mu": jnp.concatenate([flat(dpre[0]), flat(dpre[1]), flat(dpre[2]), dpre[4], dpre[5], flat(dpre[3])], axis=1),
    }
    late = _slab_wt_grad((dwt_a, dwt_b), (_WT_SEGMENTS[0], _WT_SEGMENTS[1]), 0, EARLY_FROM, "slab_wt_late")
    loras = jnp.stack([dpre[6], dpre[8]], axis=1).astype(BF16)
    dx, dng, (recv_wt, recv_lora, recv_small) = _proj_xgrad(
        x, p["norm_g"], dx2, (dua, dub, dug, duf), (w["in_a"], w["in_b"], w["in_g"], w["in_f"]),
        (late, loras, _pack_small(small, loss)), ((0, EARLY_FROM), everyone, everyone), (recv_wt, None, None))
    return dx, dng, recv_wt, (recv_woa, recv_wob, recv_wo, recv_lora), recv_small


def _position():
    return lax.axis_index("x"), lax.axis_index("y"), lax.axis_index("c")


def _hbm_specs(n):
    return [pl.BlockSpec(memory_space=pl.ANY)] * n


def _all_gather(blocks, name):
    n = len(blocks)

    def body(*refs):
        x_refs, out_refs = refs[:n], refs[n:2 * n]
        send_sems, recv_sems, local_sems = refs[2 * n:]
        x, y, c = _position()
        me, sibling = (x, y, c), (x, y, 1 - c)
        chips = [(1 - x, y), (x, 1 - y), (1 - x, 1 - y)]

        def copy(a, k, blk, to, own=False):
            dst = out_refs[a].at[4 * blk[0] + 2 * blk[1] + blk[2]]
            return pltpu.make_async_remote_copy(
                src_ref=x_refs[a] if own else dst, dst_ref=dst, send_sem=send_sems.at[7 * a + k],
                recv_sem=recv_sems.at[7 * a + k], device_id=to, device_id_type=MESH)

        mine = [pltpu.make_async_copy(x_refs[a], out_refs[a].at[4 * x + 2 * y + c], local_sems.at[a]) for a in range(n)]
        for cp in mine:
            cp.start()
        first = []
        for a in range(n):
            first.append(copy(a, 0, me, sibling, own=True))
            first += [copy(a, 1 + j, me, (*chip, c), own=True) for j, chip in enumerate(chips)]
        for cp in first:
            cp.start()
        passed = []
        for j, chip in enumerate(chips):
            for a in range(n):
                copy(a, 1 + j, (*chip, c), me).wait_recv()
                passed.append(copy(a, 4 + j, (*chip, c), sibling))
                passed[-1].start()
        for a in range(n):
            copy(a, 0, sibling, me).wait_recv()
        for j, chip in enumerate(chips):
            for a in range(n):
                copy(a, 4 + j, (*chip, 1 - c), me).wait_recv()
        for cp in first + passed:
            cp.wait_send()
        for cp in mine:
            cp.wait()

    return pl.pallas_call(
        body, name=name, out_shape=[jax.ShapeDtypeStruct((N_DEV,) + b.shape, b.dtype) for b in blocks],
        in_specs=_hbm_specs(n), out_specs=_hbm_specs(n),
        scratch_shapes=[pltpu.SemaphoreType.DMA((7 * n,)), pltpu.SemaphoreType.DMA((7 * n,)),
                        pltpu.SemaphoreType.DMA((n,))],
    )(*blocks)


def _received_shapes(slabs):
    return [jax.ShapeDtypeStruct((N_DEV,) + s.shape[1:], s.dtype) for s in slabs]


def _exchange_scratch(n):
    return [pltpu.SemaphoreType.DMA((7 * n,)), pltpu.SemaphoreType.DMA((7 * n,)), pltpu.SemaphoreType.DMA((n,))]


def _exchange_ops(src_refs, dst_refs, owners, sems):
    send_sems, recv_sems, local_sems = sems
    n = len(src_refs)

    def guarded(a, dev, fn):
        lo, hi = owners[a]
        if (lo, hi) == (0, N_DEV):
            fn()
        else:
            pl.when((dev >= lo) & (dev < hi))(fn)

    def src(a, dev):
        ref = src_refs[a]
        return ref.at[0] if ref.shape[0] == 1 else ref.at[dev - owners[a][0]]

    def run(sending, waiting):
        x, y, c = _position()
        me = 4 * x + 2 * y + c
        for a in range(n):
            local = lambda a=a: pltpu.make_async_copy(src(a, me), dst_refs[a].at[me], local_sems.at[a])
            if sending:
                guarded(a, me, lambda local=local: local().start())
            for m in range(1, N_DEV):
                px, py, pc = x ^ (m >> 2), y ^ ((m >> 1) & 1), c ^ (m & 1)
                peer = 4 * px + 2 * py + pc
                sem = dict(send_sem=send_sems.at[7 * a + m - 1], recv_sem=recv_sems.at[7 * a + m - 1],
                           device_id=(px, py, pc), device_id_type=MESH)
                send = lambda a=a, peer=peer, sem=sem: pltpu.make_async_remote_copy(
                    src_ref=src(a, peer), dst_ref=dst_refs[a].at[me], **sem)
                recv = lambda a=a, peer=peer, sem=sem: pltpu.make_async_remote_copy(
                    src_ref=src(a, me), dst_ref=dst_refs[a].at[peer], **sem)
                if sending:
                    guarded(a, peer, lambda send=send: send().start())
                if waiting:
                    guarded(a, me, lambda recv=recv: recv().wait_recv())
                    guarded(a, peer, lambda send=send: send().wait_send())
            if waiting:
                guarded(a, me, lambda local=local: local().wait())

    return functools.partial(run, True, False), functools.partial(run, False, True)


def _sum_slabs(r_ref):
    g = r_ref[0].astype(F32)
    for k in range(1, N_DEV):
        g = g + r_ref[k].astype(F32)
    return g


def _adamw(g, w, m, v):
    m_new = ADAM_B1 * m + (1.0 - ADAM_B1) * g
    v_new = ADAM_B2 * v + (1.0 - ADAM_B2) * (g * g)
    m_hat = m_new / (1.0 - ADAM_B1 ** ADAM_STEP)
    v_hat = v_new / (1.0 - ADAM_B2 ** ADAM_STEP)
    return g, -ADAM_LR * (m_hat / (jnp.sqrt(v_hat) + ADAM_EPS) + ADAM_WD * w), m_new, v_new


def _adamw_w_in(recv, w, m, v, slabs, owners):
    rows, cols = w.shape
    tile = W_IN_COL_TILE
    nx = len(slabs)

    def body(r_ref, w_ref, m_ref, v_ref, *refs):
        src_refs, o_refs, dst_refs = refs[:nx], refs[nx:nx + 4], refs[nx + 4:2 * nx + 4]
        start, wait = _exchange_ops(src_refs, dst_refs, owners, refs[2 * nx + 4:])

        @pl.when(pl.program_id(0) == 0)
        def _():
            start()

        for o_ref, val in zip(o_refs, _adamw(_sum_slabs(r_ref), w_ref[...], m_ref[...], v_ref[...])):
            o_ref[...] = val

        @pl.when(pl.program_id(0) == cols // tile - 1)
        def _():
            wait()

    blk = pl.BlockSpec((rows, tile), lambda i: (0, i))
    out = pl.pallas_call(
        body, name="adamw_w_in", grid=(cols // tile,),
        in_specs=[pl.BlockSpec((N_DEV, rows, tile), lambda i: (0, 0, i)), blk, blk, blk] + _hbm_specs(nx),
        out_specs=[blk] * 4 + _hbm_specs(nx),
        out_shape=[jax.ShapeDtypeStruct((rows, cols), F32)] * 4 + _received_shapes(slabs),
        scratch_shapes=_exchange_scratch(nx), compiler_params=_params("arbitrary"))(recv, w, m, v, *slabs)
    return out[:4], out[4:]


def _adamw_misc(recvs, recv_small, recv_norm, params):
    names = list(params)
    flat = [a for n in names for a in params[n]]

    def body(woa_ref, wob_ref, wo_ref, lora_ref, small_ref, norm_ref, *refs):
        p_refs, o_refs = refs[:len(flat)], refs[len(flat):]
        g_small = _sum_slabs(small_ref)
        g_lora = _sum_slabs(lora_ref)
        grads = {"w_out_a": _sum_slabs(woa_ref), "w_out_b": _sum_slabs(wob_ref), "w_out": _sum_slabs(wo_ref),
                 "w_lora_up": g_lora[0], "a_lora_up": g_lora[1], "norm_g": _sum_slabs(norm_ref)}
        for n, (off, size) in SMALL_SLOTS.items():
            grads[n] = g_small[:, off:off + size]
        for i, n in enumerate(names):
            w_ref, m_ref, v_ref = p_refs[3 * i:3 * i + 3]
            for o_ref, val in zip(o_refs[4 * i:4 * i + 4], _adamw(grads[n], w_ref[...], m_ref[...], v_ref[...])):
                o_ref[...] = val
        o_refs[-1][...] = g_small[:, LOSS_SLOT:LOSS_SLOT + 1]

    out = pl.pallas_call(
        body, name="adamw_misc",
        out_shape=[jax.ShapeDtypeStruct(params[n][0].shape, F32) for n in names for _ in range(4)]
        + [jax.ShapeDtypeStruct((1, 1), F32)],
        compiler_params=_params())(*recvs, recv_small, recv_norm, *flat)
    return {n: out[4 * i:4 * i + 4] for i, n in enumerate(names)}, out[-1]


_WT_SEGMENTS = ((0, NA), (NA, NB), (NA + NB + H, NG), (NA + NB, H))


def _split_wt(gathered):
    tile = W_IN_COL_TILE

    def body(g_ref, *o_refs):
        full = jnp.concatenate([g_ref[j] for j in range(N_DEV)], axis=0)
        for o_ref, (row, n) in zip(o_refs, _WT_SEGMENTS):
            seg = full[row:row + n]
            if n < o_ref.shape[0]:
                seg = jnp.concatenate([seg, jnp.zeros((o_ref.shape[0] - n, tile), BF16)], axis=0)
            o_ref[...] = seg

    sizes = (NA, NB, NG, NF)
    return pl.pallas_call(
        body, name="split_wt", grid=(D // tile,),
        in_specs=[pl.BlockSpec((N_DEV, COLS_PER_DEV, tile), lambda i: (0, 0, i))],
        out_specs=[pl.BlockSpec((n, tile), lambda i: (0, i)) for n in sizes],
        out_shape=[jax.ShapeDtypeStruct((n, D), BF16) for n in sizes],
        compiler_params=_params("arbitrary"))(gathered)


def _slab_wt_grad(segments, seg_rows, dev_lo, dev_hi, name):
    tile = W_IN_COL_TILE
    k = len(segments)

    def body(*refs):
        seg_refs, o_ref = refs[:k], refs[k]
        for j in range(dev_lo, dev_hi):
            lo, hi = COLS_PER_DEV * j, COLS_PER_DEV * (j + 1)
            parts = []
            for ref, (row, n) in sorted(zip(seg_refs, seg_rows), key=lambda t: t[1][0]):
                first, last = max(lo, row), min(hi, row + n)
                if first < last:
                    parts.append(ref[first - row:last - row, :])
            o_ref[j - dev_lo] = (parts[0] if len(parts) == 1 else jnp.concatenate(parts, axis=0)).astype(BF16)

    return pl.pallas_call(
        body, name=name, grid=(D // tile,),
        in_specs=[pl.BlockSpec((s.shape[0], tile), lambda i: (0, i)) for s in segments],
        out_specs=pl.BlockSpec((dev_hi - dev_lo, COLS_PER_DEV, tile), lambda i: (0, 0, i)),
        out_shape=jax.ShapeDtypeStruct((dev_hi - dev_lo, COLS_PER_DEV, D), BF16),
        compiler_params=_params("arbitrary"))(*segments)


def _by_cols(a):
    return jnp.moveaxis(a, 0, 1).reshape(a.shape[1], -1)


def _col_slabs(a):
    return jnp.moveaxis(a.reshape(a.shape[0], N_DEV, -1), 1, 0).astype(BF16)


def _pack_small(grads, loss):
    pieces, at = [], 0
    for n, (off, size) in list(SMALL_SLOTS.items()) + [("loss", (LOSS_SLOT, 1))]:
        pieces += [jnp.zeros((off - at,), F32), (loss if n == "loss" else grads[n]).reshape(-1)]
        at = off + size
    return jnp.concatenate(pieces + [jnp.zeros((SMALL_LEN - at,), F32)]).reshape(1, 1, SMALL_LEN)


def _gather_weights(t):
    cast = lambda a: a.astype(BF16)
    loras = jnp.stack([t["w_lora_up"][0], t["a_lora_up"][0]])
    wt, woa, wob, wo, lora = _all_gather(
        [cast(t["w_in"][0].T), cast(t["w_out_a"][0]), cast(t["w_out_b"][0]), cast(t["w_out"][0]), cast(loras)],
        "weight_gather")
    in_a, in_b, in_g, in_f = _split_wt(wt)
    return {"in_a": in_a, "in_b": in_b, "in_g": in_g, "in_f": in_f, "w_out_a": _by_cols(woa), "w_out_b": _by_cols(wob),
            "w_out": wo.reshape(D, D), "w_lora_up": lora[:, 0], "a_lora_up": lora[:, 1]}


def kernel(x, norm_g, w_in, shift_mu, w_lora_up, w0, a_lora_up, a0, k_k, k_a, r_k, lnx_w, lnx_b, f_bias, q_norm_g, k_norm_g, w_out_a, w_out_b, w_out, final_norm_g, loss_target, m_norm_g, m_w_in, m_shift_mu, m_w_lora_up, m_w0, m_a_lora_up, m_a0, m_k_k, m_k_a, m_r_k, m_lnx_w, m_lnx_b, m_f_bias, m_q_norm_g, m_k_norm_g, m_w_out_a, m_w_out_b, m_w_out, m_final_norm_g, v_norm_g, v_w_in, v_shift_mu, v_w_lora_up, v_w0, v_a_lora_up, v_a0, v_k_k, v_k_a, v_r_k, v_lnx_w, v_lnx_b, v_f_bias, v_q_norm_g, v_k_norm_g, v_w_out_a, v_w_out_b, v_w_out, v_final_norm_g):
    names = ("norm_g", "w_in", "shift_mu", "w_lora_up", "w0", "a_lora_up", "a0", "k_k", "k_a", "r_k", "lnx_w", "lnx_b",
             "f_bias", "q_norm_g", "k_norm_g", "w_out_a", "w_out_b", "w_out", "final_norm_g")
    weights = dict(zip(names, (norm_g, w_in, shift_mu, w_lora_up, w0, a_lora_up, a0, k_k, k_a, r_k, lnx_w, lnx_b,
                               f_bias, q_norm_g, k_norm_g, w_out_a, w_out_b, w_out, final_norm_g)))
    m_in = dict(zip(names, (m_norm_g, m_w_in, m_shift_mu, m_w_lora_up, m_w0, m_a_lora_up, m_a0, m_k_k, m_k_a, m_r_k,
                            m_lnx_w, m_lnx_b, m_f_bias, m_q_norm_g, m_k_norm_g, m_w_out_a, m_w_out_b, m_w_out,
                            m_final_norm_g)))
    v_in = dict(zip(names, (v_norm_g, v_w_in, v_shift_mu, v_w_lora_up, v_w0, v_a_lora_up, v_a0, v_k_k, v_k_a, v_r_k,
                            v_lnx_w, v_lnx_b, v_f_bias, v_q_norm_g, v_k_norm_g, v_w_out_a, v_w_out_b, v_w_out,
                            v_final_norm_g)))

    matrices = ("w_out_a", "w_out_b", "w_out", "w_lora_up", "a_lora_up")
    as_2d = lambda n, a: a[0] if n in matrices else a.reshape(1, -1)

    full = _gather_weights(weights)
    dx, dng, recv_wt, recvs, recv_small = _local_step(
        x[0], loss_target[0], full, {n: as_2d(n, weights[n]) for n in ("norm_g",) + tuple(SMALL_SLOTS)})

    res, (recv_norm,) = _adamw_w_in(recv_wt, w_in[0].T, m_w_in[0].T, v_w_in[0].T, (dng[None],), ((0, N_DEV),))
    outs = {"w_in": [r.T[None] for r in res]}
    misc = [n for n in names if n != "w_in"]
    res, loss_sum = _adamw_misc(recvs, recv_small, recv_norm,
                                {n: tuple(as_2d(n, t[n]) for t in (weights, m_in, v_in)) for n in misc})
    for n in misc:
        outs[n] = [r.reshape(weights[n].shape) for r in res[n]]
    return (loss_sum.reshape(()), dx[None], *[outs[n][i] for i in range(4) for n in names])
```

```python
import functools
import math

import jax
import jax.numpy as jnp
from jax import lax
from jax.experimental import pallas as pl
from jax.experimental.pallas import tpu as pltpu

F32 = jnp.float32
BF16 = jnp.bfloat16
HI = lax.Precision.HIGHEST
MESH = pl.DeviceIdType.MESH

N_DEV = 8
D = 1024
H = 8
N = 64
DA = H * N
RANK = 64
NA = 4 * DA + 2 * RANK
NB = 4 * DA
NG = 2 * D
NF = 128
IN_COLS = NA + NB + H + NG
COLS_PER_DEV = IN_COLS // N_DEV
RMS_EPS = 1e-6
LNX_EPS = 64e-5
ATT_SCALE = N ** -0.5

ADAM_LR = 0.001
ADAM_B1 = 0.9
ADAM_B2 = 0.999
ADAM_EPS = 1e-08
ADAM_WD = 0.01
ADAM_STEP = 10

LANES = 128
WKV_CHUNK = 64
TOK_TILE = 256
HEAD_TILE = 128
ATT_TILE = 256
ATT_GROUPS = 8
VMEM_LIMIT = 56 * 1024 * 1024

SMALL_SLOTS = {"final_norm_g": (0, D), "shift_mu": (D, NA), "w0": (3200, DA), "a0": (3712, DA), "k_k": (4224, DA),
               "k_a": (4736, DA), "r_k": (5248, DA), "lnx_w": (5760, DA), "lnx_b": (6272, DA), "q_norm_g": (6784, N),
               "k_norm_g": (6912, N), "f_bias": (7040, H)}
LOSS_SLOT = 7168
SMALL_LEN = 7296
W_IN_COL_TILE = 256
EARLY_FROM = -(-NA // COLS_PER_DEV)


def _params(*sem):
    return pltpu.CompilerParams(dimension_semantics=sem or None, vmem_limit_bytes=VMEM_LIMIT)


def _bdot(a, b):
    return jnp.dot(a.astype(BF16), b.astype(BF16), preferred_element_type=F32)


def _bdot_nt(a, b):
    return lax.dot_general(a.astype(BF16), b.astype(BF16), (((1,), (1,)), ((), ())), preferred_element_type=F32)


def _bdot_tn(a, b):
    return lax.dot_general(a.astype(BF16), b.astype(BF16), (((0,), (0,)), ((), ())), preferred_element_type=F32)


def _sigmoid(x):
    return 1.0 / (1.0 + jnp.exp(-x))


def _softplus(x):
    return jnp.maximum(x, 0.0) + jnp.log(1.0 + jnp.exp(-jnp.abs(x)))


def _heads(ref, col0):
    return jnp.stack([ref[:, col0 + N * h:col0 + N * (h + 1)] for h in range(H)])


def _store_heads(ref, col0, val):
    for h in range(H):
        ref[:, col0 + N * h:col0 + N * (h + 1)] = val[h]


def _lerp(c, s, mu):
    return c + (s - c) * mu


def _rwkv_pre(rc, rs, kc, ks, vc, vs, gc, gs, wdc, wds, adc, ads,
              mu_r, mu_k, mu_v, mu_g, mu_wd, mu_ad, w_up, w0, a_up, a0, k_k, k_a):
    r = _lerp(rc, rs, mu_r)
    k = _lerp(kc, ks, mu_k)
    v = _lerp(vc, vs, mu_v)
    g = _lerp(gc, gs, mu_g)
    wd = _lerp(wdc, wds, mu_wd)
    ad = _lerp(adc, ads, mu_ad)
    t = wd.shape[0]
    bdims = (((2,), (1,)), ((0,), (0,)))
    tw = jnp.broadcast_to(jnp.tanh(wd).astype(BF16)[None], (H, t, RANK))
    z = w0 + lax.dot_general(tw, w_up.astype(BF16), bdims, preferred_element_type=F32)
    w_raw = -_softplus(-z) - 0.5
    lw = -jnp.exp(w_raw)
    row = lax.broadcasted_iota(jnp.int32, (t, t), 0)
    col = lax.broadcasted_iota(jnp.int32, (t, t), 1)
    same_chunk = ((row >= col) & (row // WKV_CHUNK == col // WKV_CHUNK)).astype(F32)
    cl = jnp.einsum("hts,hsn->htn", jnp.broadcast_to(same_chunk[None], (H, t, t)), lw, precision=HI,
                    preferred_element_type=F32)
    adb = jnp.broadcast_to(ad.astype(BF16)[None], (H, t, RANK))
    alr = _sigmoid(a0 + lax.dot_general(adb, a_up.astype(BF16), bdims, preferred_element_type=F32))
    kk = k * k_k
    kk = kk / jnp.maximum(jnp.sqrt(jnp.sum(kk * kk, axis=-1, keepdims=True)), 1e-12)
    k2 = k * (1.0 + (alr - 1.0) * k_a)
    return r, lw, cl, k2, v, -kk, kk * alr, g


_MM_DIMS = {"nn": (((2,), (1,)), ((0,), (0,))), "nt": (((2,), (2,)), ((0,), (0,))), "tn": (((1,), (1,)), ((0,), (0,)))}


def _split(x):
    hi = x.astype(BF16)
    return hi, (x - hi.astype(F32)).astype(BF16)


def _dot3(a, b, kind):
    ah, al = _split(a)
    bh, bl = _split(b)
    dot = functools.partial(lax.dot_general, dimension_numbers=_MM_DIMS[kind], preferred_element_type=F32)
    return dot(ah, bh) + (dot(ah, bl) + dot(al, bh))


@functools.partial(jax.custom_vjp, nondiff_argnums=(2,))
def _mm(a, b, kind):
    return _dot3(a, b, kind)


def _mm_fwd(a, b, kind):
    return _dot3(a, b, kind), (a, b)


def _dot1(a, b, kind):
    return lax.dot_general(a.astype(BF16), b.astype(BF16), dimension_numbers=_MM_DIMS[kind], preferred_element_type=F32)


def _mm_bwd(kind, res, ct):
    a, b = res
    if kind == "nn":
        return _dot1(ct, b, "nt"), _dot1(a, ct, "tn")
    if kind == "nt":
        return _dot1(ct, b, "nn"), _dot1(ct, a, "tn")
    return _dot1(b, ct, "nt"), _dot1(a, ct, "nn")


_mm.defvjp(_mm_fwd, _mm_bwd)


def _chunk_masks(c):
    row = lax.broadcasted_iota(jnp.int32, (c, c), 0)
    col = lax.broadcasted_iota(jnp.int32, (c, c), 1)
    return (row >= col)[None], (row > col)[None], (row == col).astype(F32)[None]


def _wkv_aab(lw, cl, a, b):
    _, strict, _ = _chunk_masks(a.shape[1])
    return jnp.where(strict, _mm(a * jnp.exp(cl - lw), b * jnp.exp(-cl), "nt"), 0.0)


def _tri_inverse(x):
    c = x.shape[1]
    p = _chunk_masks(c)[2] + x
    for _ in range(int(math.log2(c)) - 1):
        x = _mm(x, x, "nn")
        p = p + _mm(p, x, "nn")
    return p


def _wkv_apply(s0, r, lw, cl, k, v, a, b, p):
    c = r.shape[1]
    incl, strict, _ = _chunk_masks(c)
    gi = jnp.exp(-cl)
    left = jnp.concatenate([a * jnp.exp(cl - lw), r * jnp.exp(cl)], axis=1)
    right = jnp.concatenate([b * gi, k * gi], axis=1)
    m = _mm(left, right, "nt")
    z0 = _mm(left, s0, "nt")
    a_ak = jnp.where(strict, m[:, :c, c:], 0.0)
    row = lax.broadcasted_iota(jnp.int32, (c, 2 * c), 0)
    col = lax.broadcasted_iota(jnp.int32, (c, 2 * c), 1)
    a_r = jnp.where((row >= col % c)[None], m[:, c:, :], 0.0)
    sa = _mm(p, z0[:, :c] + _mm(a_ak, v, "nn"), "nn")
    sa_v = jnp.concatenate([sa, v], axis=1)
    y = z0[:, c:] + _mm(a_r, sa_v, "nn")
    s1 = (s0 + _mm(sa_v, right, "tn")) * jnp.exp(cl[:, c - 1:c, :])
    return y, s1


def _rwkv_post(y, r, k2, v, g, lnx_w, lnx_b, r_k):
    mean = jnp.mean(y, axis=-1, keepdims=True)
    yc = y - mean
    var = jnp.mean(yc * yc, axis=-1, keepdims=True)
    yn = yc * lax.rsqrt(var + LNX_EPS) * lnx_w + lnx_b
    bonus = jnp.sum(r * k2 * r_k, axis=-1, keepdims=True) * v
    return (yn + bonus) * (g * _sigmoid(g))


def _fox_pre(q, k, f, q_g, k_g, f_b):
    qn = q * lax.rsqrt(jnp.mean(q * q, axis=-1, keepdims=True) + RMS_EPS) * q_g
    kn = k * lax.rsqrt(jnp.mean(k * k, axis=-1, keepdims=True) + RMS_EPS) * k_g
    x = f + f_b
    return qn, kn, jnp.minimum(x, 0.0) - jnp.log(1.0 + jnp.exp(-jnp.abs(x)))


def _rms_fwd(x, g):
    s = x.shape[0]

    def body(x_ref, g_ref, h_ref):
        xv = x_ref[...]
        h_ref[...] = (xv * lax.rsqrt(jnp.mean(xv * xv, axis=-1, keepdims=True) + RMS_EPS) * g_ref[...]).astype(BF16)

    return pl.pallas_call(
        body, name="rms_fwd", grid=(s // TOK_TILE,),
        in_specs=[pl.BlockSpec((TOK_TILE, D), lambda i: (i, 0)), pl.BlockSpec((1, D), lambda i: (0, 0))],
        out_specs=pl.BlockSpec((TOK_TILE, D), lambda i: (i, 0)),
        out_shape=jax.ShapeDtypeStruct((s, D), BF16), compiler_params=_params("arbitrary"))(x, g)


def _proj(h, wt, name):
    s, n = h.shape[0], wt.shape[0]

    def body(h_ref, w_ref, o_ref):
        o_ref[...] = _bdot_nt(h_ref[...], w_ref[...])

    return pl.pallas_call(
        body, name=name, grid=(s // TOK_TILE,),
        in_specs=[pl.BlockSpec((TOK_TILE, D), lambda i: (i, 0)), pl.BlockSpec((n, D), lambda i: (0, 0))],
        out_specs=pl.BlockSpec((TOK_TILE, n), lambda i: (i, 0)),
        out_shape=jax.ShapeDtypeStruct((s, n), F32), compiler_params=_params("arbitrary"))(h, wt)


def _proj_wgrad(h, du, name):
    s, n = du.shape

    def body(h_ref, du_ref, o_ref):
        @pl.when(pl.program_id(0) == 0)
        def _():
            o_ref[...] = jnp.zeros_like(o_ref)

        o_ref[...] += _bdot_tn(du_ref[...], h_ref[...])

    return pl.pallas_call(
        body, name=name, grid=(s // TOK_TILE,),
        in_specs=[pl.BlockSpec((TOK_TILE, D), lambda i: (i, 0)), pl.BlockSpec((TOK_TILE, n), lambda i: (i, 0))],
        out_specs=pl.BlockSpec((n, D), lambda i: (0, 0)),
        out_shape=jax.ShapeDtypeStruct((n, D), F32), compiler_params=_params("arbitrary"))(h, du)


def _proj_xgrad(x, g, dx2, dus, ws, slabs, owners):
    s = x.shape[0]
    tile = HEAD_TILE
    k = len(dus)
    nx = len(slabs)
    n_in = 3 + 2 * k + nx

    def body(*refs):
        x_ref, g_ref, dx2_ref = refs[:3]
        du_refs, w_refs = refs[3:3 + k], refs[3 + k:3 + 2 * k]
        src_refs = refs[3 + 2 * k:3 + 2 * k + nx]
        dx_ref, dg_ref = refs[n_in:n_in + 2]
        dst_refs = refs[n_in + 2:n_in + 2 + nx]
        start, wait = _exchange_ops(src_refs, dst_refs, owners, refs[n_in + 2 + nx:])

        @pl.when(pl.program_id(0) == 0)
        def _():
            dg_ref[...] = jnp.zeros_like(dg_ref)
            start()

        dh = _bdot(du_refs[0][...], w_refs[0][...])
        for du_ref, w_ref in zip(du_refs[1:], w_refs[1:]):
            dh += _bdot(du_ref[...], w_ref[...])
        xv = x_ref[...]
        rs = lax.rsqrt(jnp.mean(xv * xv, axis=-1, keepdims=True) + RMS_EPS)
        xn = xv * rs
        dg_ref[...] += jnp.sum(dh * xn, axis=0, keepdims=True)
        dxn = dh * g_ref[...]
        dx_ref[...] = rs * (dxn - xn * jnp.mean(dxn * xn, axis=-1, keepdims=True)) + dx2_ref[...]

        @pl.when(pl.program_id(0) == s // tile - 1)
        def _():
            wait()

    tok = lambda n: pl.BlockSpec((tile, n), lambda i: (i, 0))
    fixed = lambda a: pl.BlockSpec(a.shape, lambda i: (0,) * a.ndim)
    out = pl.pallas_call(
        body, name="proj_xgrad", grid=(s // tile,),
        in_specs=([tok(D), fixed(g), tok(D)] + [tok(du.shape[1]) for du in dus] + [fixed(w) for w in ws]
                  + _hbm_specs(nx)),
        out_specs=[tok(D), pl.BlockSpec((1, D), lambda i: (0, 0))] + _hbm_specs(nx),
        out_shape=[jax.ShapeDtypeStruct((s, D), F32), jax.ShapeDtypeStruct((1, D), F32)] + _received_shapes(slabs, owners),
        scratch_shapes=_exchange_scratch(nx),
        compiler_params=_params("arbitrary"))(x, g, dx2, *dus, *ws, *slabs)
    return out[0], out[1], out[2:]


def _tail(x, target, ya, o, ub, ug, w_oa, w_ob, w_o, fg):
    s = x.shape[0]
    tile = TOK_TILE

    def body(x_ref, t_ref, ya_ref, o_ref, gb_ref, ug_ref, woa_ref, wob_ref, wo_ref, fg_ref,
             loss_ref, dfg_ref, dwo_ref, dwoa_ref, dwob_ref, dx2_ref, dya_ref, do_ref, dgb_ref, dug_ref):
        @pl.when(pl.program_id(0) == 0)
        def _():
            for r in (loss_ref, dfg_ref, dwo_ref, dwoa_ref, dwob_ref):
                r[...] = jnp.zeros_like(r)

        ya_v = ya_ref[...]
        gate_b = gb_ref[...]
        sg_b = _sigmoid(gate_b)
        silu_b = gate_b * sg_b
        o_v = jnp.concatenate([o_ref[h] for h in range(H)], axis=-1)
        yb_v = o_v * silu_b
        big_a = _bdot(ya_v, woa_ref[...])
        big_b = _bdot(yb_v, wob_ref[...])
        sa = _sigmoid(ug_ref[:, :D])
        sb = _sigmoid(ug_ref[:, D:])
        merged = sa * big_a + sb * big_b
        x2 = x_ref[...] + _bdot(merged, wo_ref[...])
        rs = lax.rsqrt(jnp.mean(x2 * x2, axis=-1, keepdims=True) + RMS_EPS)
        xn = x2 * rs
        err = xn * fg_ref[...] - t_ref[...]
        loss_ref[...] += (0.5 / D) * jnp.sum(err * err)
        dout = err * (1.0 / D)
        dfg_ref[...] += jnp.sum(dout * xn, axis=0, keepdims=True)
        dxn = dout * fg_ref[...]
        dx2 = rs * (dxn - xn * jnp.mean(dxn * xn, axis=-1, keepdims=True))
        dx2_ref[...] = dx2
        dwo_ref[...] += _bdot_tn(merged, dx2)
        dmerged = _bdot_nt(dx2, wo_ref[...])
        dbig_a = dmerged * sa
        dbig_b = dmerged * sb
        dug_ref[:, :D] = dmerged * big_a * sa * (1.0 - sa)
        dug_ref[:, D:] = dmerged * big_b * sb * (1.0 - sb)
        dwoa_ref[...] += _bdot_tn(ya_v, dbig_a)
        dwob_ref[...] += _bdot_tn(yb_v, dbig_b)
        dya_ref[...] = _bdot_nt(dbig_a, woa_ref[...])
        dyb = _bdot_nt(dbig_b, wob_ref[...])
        dgb_ref[...] = dyb * o_v * (sg_b * (1.0 + gate_b * (1.0 - sg_b)))
        _dov = dyb * silu_b
        for h in range(H):
            do_ref[h] = _dov[:, N * h:N * (h + 1)]

    tok = lambda n: pl.BlockSpec((tile, n), lambda i: (i, 0))
    hm = pl.BlockSpec((H, tile, N), lambda i: (0, i, 0))
    fixed = lambda shape: pl.BlockSpec(shape, lambda i: (0,) * len(shape))
    f32 = lambda *shape: jax.ShapeDtypeStruct(shape, F32)
    return pl.pallas_call(
        body, name="tail", grid=(s // tile,),
        in_specs=[tok(D), tok(D), tok(DA), hm, pl.BlockSpec((tile, DA), lambda i: (i, 3)), tok(NG),
                  fixed((DA, D)), fixed((DA, D)), fixed((D, D)), fixed((1, D))],
        out_specs=[fixed((1, 1)), fixed((1, D)), fixed((D, D)), fixed((DA, D)), fixed((DA, D)),
                   tok(D), tok(DA), hm, tok(DA), tok(NG)],
        out_shape=[f32(1, 1), f32(1, D), f32(D, D), f32(DA, D), f32(DA, D),
                   f32(s, D), f32(s, DA), f32(H, s, N), f32(s, DA), f32(s, NG)],
        compiler_params=_params("arbitrary"))(x, target, ya, o, ub, ug, w_oa, w_ob, w_o, fg)


_PRE_PARAM_SHAPES = ((H, 1, N),) * 4 + ((1, RANK),) * 2 + ((H, RANK, N), (H, 1, N), (H, RANK, N), (H, 1, N), (H, 1, N),
                                                              (H, 1, N))


def _pre_operands(ua_ref, prev_ref, first):
    cur = ua_ref[...]
    t = cur.shape[0]
    prev_row = jnp.where(first, 0.0, prev_ref[7:8, :])
    rows = lax.broadcasted_iota(jnp.int32, cur.shape, 0)
    sh = jnp.where(rows == 0, prev_row, pltpu.roll(cur, 1, axis=0))
    ops = []
    for c0 in (0, DA, 2 * DA, 3 * DA + 2 * RANK):
        ops.append(jnp.stack([cur[:, c0 + N * h:c0 + N * (h + 1)] for h in range(H)]))
        ops.append(jnp.stack([sh[:, c0 + N * h:c0 + N * (h + 1)] for h in range(H)]))
    for c0 in (3 * DA, 3 * DA + RANK):
        ops.append(cur[:, c0:c0 + RANK])
        ops.append(sh[:, c0:c0 + RANK])
    del t
    return ops


def _ua_specs(tile, order):
    blocks = tile // 8
    return [pl.BlockSpec((tile, NA), lambda i: (order(i), 0)),
            pl.BlockSpec((8, NA), lambda i: (jnp.maximum(order(i) * blocks - 1, 0), 0))]


def _rwkv_pre_fwd(ua, pre_params):
    s = ua.shape[0]
    tile = HEAD_TILE

    def body(ua_ref, prev_ref, *refs):
        p_refs, o_refs = refs[:len(pre_params)], refs[len(pre_params):]
        ops = _pre_operands(ua_ref, prev_ref, pl.program_id(0) == 0)
        outs = _rwkv_pre(*ops, *[p[...] for p in p_refs])
        for o_ref, val in zip(o_refs, outs):
            o_ref[...] = val

    hm = pl.BlockSpec((H, tile, N), lambda i: (0, i, 0))
    return pl.pallas_call(
        body, name="rwkv_pre_fwd", grid=(s // tile,),
        in_specs=_ua_specs(tile, lambda i: i) + [pl.BlockSpec(p.shape, lambda i, nd=p.ndim: (0,) * nd) for p in pre_params],
        out_specs=[hm] * 8, out_shape=[jax.ShapeDtypeStruct((H, s, N), F32)] * 8,
        compiler_params=_params("arbitrary"))(ua, ua, *pre_params)


def _rwkv_pre_bwd(ua, pre_params, cots):
    s = ua.shape[0]
    tile = HEAD_TILE
    nt = s // tile
    n_p = len(pre_params)

    def body(ua_ref, prev_ref, *refs):
        p_refs, c_refs = refs[:n_p], refs[n_p:n_p + 11]
        dua_ref = refs[n_p + 11]
        dp_refs = refs[n_p + 12:n_p + 12 + n_p]
        carry_ref = refs[-1]
        i = pl.program_id(0)

        @pl.when(i == 0)
        def _():
            carry_ref[...] = jnp.zeros_like(carry_ref)
            for r in dp_refs:
                r[...] = jnp.zeros_like(r)

        ops = _pre_operands(ua_ref, prev_ref, i == nt - 1)
        _, vjp = jax.vjp(_rwkv_pre, *ops, *[p[...] for p in p_refs])
        c = [r[...] for r in c_refs]
        grads = vjp((c[0] + c[1], c[2], c[3], c[4] + c[5], c[6] + c[7], c[8], c[9], c[10]))
        d_ops, d_par = grads[:12], grads[12:]
        for r, val in zip(dp_refs, d_par):
            r[...] += val
        d_cur = jnp.concatenate([d_ops[0][h] for h in range(H)] + [d_ops[2][h] for h in range(H)]
                                + [d_ops[4][h] for h in range(H)] + [d_ops[8], d_ops[10]]
                                + [d_ops[6][h] for h in range(H)], axis=-1)
        d_sh = jnp.concatenate([d_ops[1][h] for h in range(H)] + [d_ops[3][h] for h in range(H)]
                               + [d_ops[5][h] for h in range(H)] + [d_ops[9], d_ops[11]]
                               + [d_ops[7][h] for h in range(H)], axis=-1)
        rows = lax.broadcasted_iota(jnp.int32, d_sh.shape, 0)
        dua_ref[...] = d_cur + jnp.where(rows == tile - 1, carry_ref[...], pltpu.roll(d_sh, tile - 1, axis=0))
        carry_ref[...] = d_sh[0:1, :]

    rev = lambda i: nt - 1 - i
    hm = pl.BlockSpec((H, tile, N), lambda i: (0, rev(i), 0))
    fixed = [pl.BlockSpec(p.shape, lambda i, nd=p.ndim: (0,) * nd) for p in pre_params]
    return pl.pallas_call(
        body, name="rwkv_pre_bwd", grid=(nt,),
        in_specs=_ua_specs(tile, rev) + fixed + [hm] * 11,
        out_specs=[pl.BlockSpec((tile, NA), lambda i: (rev(i), 0))] + fixed,
        out_shape=[jax.ShapeDtypeStruct((s, NA), F32)] + [jax.ShapeDtypeStruct(p.shape, F32) for p in pre_params],
        scratch_shapes=[pltpu.VMEM((1, NA), F32)],
        compiler_params=_params("arbitrary"))(ua, ua, *pre_params, *cots)


def _wkv_fwd(seq):
    s = seq[0].shape[1]
    nc = s // WKV_CHUNK

    def body(r_ref, lw_ref, cl_ref, k_ref, v_ref, a_ref, b_ref, y_ref, ck_ref, p_ref, state):
        @pl.when(pl.program_id(0) == 0)
        def _():
            state[...] = jnp.zeros_like(state)

        s0 = state[...]
        ck_ref[0] = s0
        p = _tri_inverse(_wkv_aab(lw_ref[...], cl_ref[...], a_ref[...], b_ref[...]))
        p_ref[0] = p
        y, s1 = _wkv_apply(s0, r_ref[...], lw_ref[...], cl_ref[...], k_ref[...], v_ref[...], a_ref[...], b_ref[...], p)
        y_ref[...] = y
        state[...] = s1

    hm = pl.BlockSpec((H, WKV_CHUNK, N), lambda c: (0, c, 0))
    per_chunk = lambda m: pl.BlockSpec((1, H, m, m), lambda c: (c, 0, 0, 0))
    return pl.pallas_call(
        body, name="wkv_fwd", grid=(nc,), in_specs=[hm] * 7,
        out_specs=[hm, per_chunk(N), per_chunk(WKV_CHUNK)],
        out_shape=[jax.ShapeDtypeStruct((H, s, N), F32), jax.ShapeDtypeStruct((nc, H, N, N), F32),
                   jax.ShapeDtypeStruct((nc, H, WKV_CHUNK, WKV_CHUNK), F32)],
        scratch_shapes=[pltpu.VMEM((H, N, N), F32)], compiler_params=_params("arbitrary"))(*seq)


def _wkv_bwd(seq, ckpt, pinv, dy, slabs, owners):
    s = seq[0].shape[1]
    nc = s // WKV_CHUNK
    nx = len(slabs)

    def body(r_ref, lw_ref, cl_ref, k_ref, v_ref, a_ref, b_ref, ck_ref, p_ref, dy_ref, *refs):
        src_refs, d_refs, dst_refs = refs[:nx], refs[nx:nx + 7], refs[nx + 7:2 * nx + 7]
        dstate = refs[2 * nx + 7]
        start, wait = _exchange_ops(src_refs, dst_refs, owners, refs[2 * nx + 8:])

        @pl.when(pl.program_id(0) == 0)
        def _():
            dstate[...] = jnp.zeros_like(dstate)
            start()

        p = p_ref[0]
        lw, cl, a, b = lw_ref[...], cl_ref[...], a_ref[...], b_ref[...]
        _, vjp = jax.vjp(_wkv_apply, ck_ref[0], r_ref[...], lw, cl, k_ref[...], v_ref[...], a, b, p)
        ds0, dr, dlw, dcl, dk, dv, da, db, dp = vjp((dy_ref[...], dstate[...]))
        dstate[...] = ds0
        _, vjp_x = jax.vjp(_wkv_aab, lw, cl, a, b)
        dlw2, dcl2, da2, db2 = vjp_x(_mm(_mm(p, dp, "tn"), p, "nt"))
        for d_ref, val in zip(d_refs, (dr, dlw + dlw2, dcl + dcl2, dk, dv, da + da2, db + db2)):
            d_ref[...] = val

        @pl.when(pl.program_id(0) == nc - 1)
        def _():
            wait()

    hm = pl.BlockSpec((H, WKV_CHUNK, N), lambda c: (0, nc - 1 - c, 0))
    per_chunk = lambda m: pl.BlockSpec((1, H, m, m), lambda c: (nc - 1 - c, 0, 0, 0))
    out = pl.pallas_call(
        body, name="wkv_bwd", grid=(nc,),
        in_specs=[hm] * 7 + [per_chunk(N), per_chunk(WKV_CHUNK), hm] + _hbm_specs(nx),
        out_specs=[hm] * 7 + _hbm_specs(nx),
        out_shape=[jax.ShapeDtypeStruct((H, s, N), F32)] * 7 + _received_shapes(slabs, owners),
        scratch_shapes=[pltpu.VMEM((H, N, N), F32)] + _exchange_scratch(nx),
        compiler_params=_params("arbitrary"))(*seq, ckpt, pinv, dy, *slabs)
    return out[:7], out[7:]


def _rwkv_post_fwd(y, r, k2, v, g, post_params):
    s = y.shape[1]
    tile = HEAD_TILE

    def body(y_ref, r_ref, k_ref, v_ref, g_ref, w_ref, b_ref, rk_ref, o_ref):
        out = _rwkv_post(y_ref[...], r_ref[...], k_ref[...], v_ref[...], g_ref[...], w_ref[...], b_ref[...],
                         rk_ref[...])
        o_ref[...] = jnp.concatenate([out[h] for h in range(H)], axis=-1)

    hm = pl.BlockSpec((H, tile, N), lambda i: (0, i, 0))
    par = pl.BlockSpec((H, 1, N), lambda i: (0, 0, 0))
    return pl.pallas_call(
        body, name="rwkv_post_fwd", grid=(s // tile,), in_specs=[hm] * 5 + [par] * 3,
        out_specs=pl.BlockSpec((tile, DA), lambda i: (i, 0)), out_shape=jax.ShapeDtypeStruct((s, DA), F32),
        compiler_params=_params("arbitrary"))(y, r, k2, v, g, *post_params)


def _rwkv_post_bwd(y, r, k2, v, g, post_params, dya):
    s = y.shape[1]
    tile = HEAD_TILE

    def body(y_ref, r_ref, k_ref, v_ref, g_ref, w_ref, b_ref, rk_ref, dya_ref, *d_refs):
        @pl.when(pl.program_id(0) == 0)
        def _():
            for ref in d_refs[5:]:
                ref[...] = jnp.zeros_like(ref)

        _, vjp = jax.vjp(_rwkv_post, y_ref[...], r_ref[...], k_ref[...], v_ref[...], g_ref[...], w_ref[...],
                         b_ref[...], rk_ref[...])
        grads = vjp(jnp.stack([dya_ref[:, N * h:N * (h + 1)] for h in range(H)]))
        for ref, val in zip(d_refs[:5], grads[:5]):
            ref[...] = val
        for ref, val in zip(d_refs[5:], grads[5:]):
            ref[...] += val

    hm = pl.BlockSpec((H, tile, N), lambda i: (0, i, 0))
    par = pl.BlockSpec((H, 1, N), lambda i: (0, 0, 0))
    return pl.pallas_call(
        body, name="rwkv_post_bwd", grid=(s // tile,),
        in_specs=[hm] * 5 + [par] * 3 + [pl.BlockSpec((tile, DA), lambda i: (i, 0))],
        out_specs=[hm] * 5 + [par] * 3,
        out_shape=[jax.ShapeDtypeStruct((H, s, N), F32)] * 5 + [jax.ShapeDtypeStruct((H, 1, N), F32)] * 3,
        compiler_params=_params("arbitrary"))(y, r, k2, v, g, *post_params, dya)


def _tri(t):
    return (lax.broadcasted_iota(jnp.int32, (t, t), 0) >= lax.broadcasted_iota(jnp.int32, (t, t), 1)).astype(F32)


def _fox_pre_fwd(ub, uf, q_g, k_g, f_b):
    s = ub.shape[0]
    tile = HEAD_TILE

    def body(ub_ref, uf_ref, qg_ref, kg_ref, fb_ref, q_ref, k_ref, v_ref, cum_ref, carry):
        @pl.when(pl.program_id(0) == 0)
        def _():
            carry[...] = jnp.zeros_like(carry)

        qn, kn, logf = _fox_pre(_heads(ub_ref, 0), _heads(ub_ref, DA), uf_ref[...], qg_ref[...], kg_ref[...],
                                fb_ref[...])
        q_ref[...] = qn
        k_ref[...] = kn
        v_ref[...] = _heads(ub_ref, 2 * DA)
        cum = jnp.dot(_tri(tile), logf, precision=HI, preferred_element_type=F32) + carry[...]
        cum_ref[...] = cum
        carry[...] = cum[tile - 1:tile, :]

    hm = pl.BlockSpec((H, tile, N), lambda i: (0, i, 0))
    fixed = lambda shape: pl.BlockSpec(shape, lambda i: (0,) * len(shape))
    return pl.pallas_call(
        body, name="fox_pre_fwd", grid=(s // tile,),
        in_specs=[pl.BlockSpec((tile, NB), lambda i: (i, 0)), pl.BlockSpec((tile, NF), lambda i: (i, 0)),
                  fixed((1, 1, N)), fixed((1, 1, N)), fixed((1, NF))],
        out_specs=[hm] * 3 + [pl.BlockSpec((tile, NF), lambda i: (i, 0))],
        out_shape=[jax.ShapeDtypeStruct((H, s, N), F32)] * 3 + [jax.ShapeDtypeStruct((s, NF), F32)],
        scratch_shapes=[pltpu.VMEM((1, NF), F32)], compiler_params=_params("arbitrary"))(ub, uf, q_g, k_g, f_b)


def _fox_pre_bwd(ub, uf, q_g, k_g, f_b, dqn, dkn, dvf, dgate, dcum_q, dcum_k):
    s = ub.shape[0]
    tile = HEAD_TILE
    nt = s // tile

    def body(ub_ref, uf_ref, qg_ref, kg_ref, fb_ref, dq_ref, dk_ref, dv_ref, dgate_ref, dcq_ref, dck_ref,
             dub_ref, duf_ref, dqg_ref, dkg_ref, dfb_ref, carry):
        @pl.when(pl.program_id(0) == 0)
        def _():
            carry[...] = jnp.zeros_like(carry)
            for ref in (dqg_ref, dkg_ref, dfb_ref):
                ref[...] = jnp.zeros_like(ref)

        dcum = dcq_ref[...] + dck_ref[...]
        dlogf = lax.dot_general(_tri(tile), dcum, (((0,), (0,)), ((), ())), precision=HI,
                                preferred_element_type=F32) + carry[...]
        carry[...] = dlogf[0:1, :]
        _, vjp = jax.vjp(_fox_pre, _heads(ub_ref, 0), _heads(ub_ref, DA), uf_ref[...], qg_ref[...], kg_ref[...],
                         fb_ref[...])
        d_q, d_k, d_f, d_qg, d_kg, d_fb = vjp((dq_ref[...], dk_ref[...], dlogf))
        _store_heads(dub_ref, 0, d_q)
        _store_heads(dub_ref, DA, d_k)
        _store_heads(dub_ref, 2 * DA, dv_ref[...])
        dub_ref[:, 3 * DA:] = dgate_ref[...]
        duf_ref[...] = d_f
        dqg_ref[...] += d_qg
        dkg_ref[...] += d_kg
        dfb_ref[...] += d_fb

    rev = lambda i: nt - 1 - i
    hm = pl.BlockSpec((H, tile, N), lambda i: (0, rev(i), 0))
    tok = lambda n: pl.BlockSpec((tile, n), lambda i: (rev(i), 0))
    fixed = lambda shape: pl.BlockSpec(shape, lambda i: (0,) * len(shape))
    return pl.pallas_call(
        body, name="fox_pre_bwd", grid=(nt,),
        in_specs=[tok(NB), tok(NF), fixed((1, 1, N)), fixed((1, 1, N)), fixed((1, NF)), hm, hm, hm, tok(DA), tok(NF),
                  tok(NF)],
        out_specs=[tok(NB), tok(NF), fixed((1, 1, N)), fixed((1, 1, N)), fixed((1, NF))],
        out_shape=[jax.ShapeDtypeStruct((s, NB), F32), jax.ShapeDtypeStruct((s, NF), F32),
                   jax.ShapeDtypeStruct((1, 1, N), F32), jax.ShapeDtypeStruct((1, 1, N), F32),
                   jax.ShapeDtypeStruct((1, NF), F32)],
        scratch_shapes=[pltpu.VMEM((1, NF), F32)],
        compiler_params=_params("arbitrary"))(ub, uf, q_g, k_g, f_b, dqn, dkn, dvf, dgate, dcum_q, dcum_k)


def _att_groups(s):
    blocks = s // ATT_TILE
    per = max(1, blocks // ATT_GROUPS)
    return per, blocks // per


def _att_logits(q_bf, k, cq, ck, qi):
    tq, sk = q_bf.shape[0], k.shape[0]
    logits = _bdot_nt(q_bf, k) * ATT_SCALE + cq - ck
    rows = qi * tq + lax.broadcasted_iota(jnp.int32, (tq, sk), 0)
    mask = rows >= lax.broadcasted_iota(jnp.int32, (tq, sk), 1)
    return jnp.where(mask, logits, -1e30), mask


def _fox_attn_fwd(q, k, v, cum_q, cum_k):
    s = q.shape[1]
    t = ATT_TILE
    per, groups = _att_groups(s)

    def body(q_ref, k_ref, v_ref, cq_ref, ck_ref, o_ref, lse_ref):
        qi = pl.program_id(1)
        for g in range(groups):
            @pl.when(qi // per == g)
            def _(n=(g + 1) * per * t):
                logits, _ = _att_logits(q_ref[0].astype(BF16), k_ref[0, :n, :], cq_ref[0], ck_ref[0, :, :n], qi)
                m = jnp.max(logits, axis=-1, keepdims=True)
                p = jnp.exp(logits - m)
                l = jnp.sum(p, axis=-1, keepdims=True)
                o_ref[0] = _bdot(p, v_ref[0, :n, :]) / l
                lse_ref[0] = m + jnp.log(l)

    qb = pl.BlockSpec((1, t, N), lambda h, i: (h, i, 0))
    kb = pl.BlockSpec((1, s, N), lambda h, i: (h, 0, 0))
    return pl.pallas_call(
        body, name="fox_attn_fwd", grid=(H, s // t),
        in_specs=[qb, kb, kb, pl.BlockSpec((1, t, 1), lambda h, i: (h, i, 0)),
                  pl.BlockSpec((1, 1, s), lambda h, i: (h, 0, 0))],
        out_specs=[qb, pl.BlockSpec((1, t, 1), lambda h, i: (h, i, 0))],
        out_shape=[jax.ShapeDtypeStruct((H, s, N), F32), jax.ShapeDtypeStruct((H, s, 1), F32)],
        compiler_params=_params("arbitrary", "arbitrary"))(q, k, v, cum_q, cum_k)


def _fox_attn_bwd(q, k, v, cum_q, cum_k, o, lse, do, slabs, owners):
    s = q.shape[1]
    t = ATT_TILE
    per, groups = _att_groups(s)
    nx = len(slabs)

    def body(q_ref, k_ref, v_ref, cq_ref, ck_ref, o_ref, lse_ref, do_ref, *refs):
        src_refs, (dq_ref, dk_ref, dv_ref, dcq_ref, dck_ref) = refs[:nx], refs[nx:nx + 5]
        start, wait = _exchange_ops(src_refs, refs[nx + 5:2 * nx + 5], owners, refs[2 * nx + 5:])
        qi = pl.program_id(1)

        @pl.when((pl.program_id(0) == 0) & (qi == 0))
        def _():
            start()

        @pl.when(qi == 0)
        def _():
            for ref in (dk_ref, dv_ref, dck_ref):
                ref[...] = jnp.zeros_like(ref)

        for g in range(groups):
            @pl.when(qi // per == g)
            def _(n=(g + 1) * per * t):
                q_bf, do_bf = q_ref[0].astype(BF16), do_ref[0].astype(BF16)
                kv = k_ref[0, :n, :]
                logits, mask = _att_logits(q_bf, kv, cq_ref[0], ck_ref[0, :, :n], qi)
                p = jnp.where(mask, jnp.exp(logits - lse_ref[0]), 0.0)
                delta = jnp.sum(do_ref[0] * o_ref[0], axis=-1, keepdims=True)
                ds = p * (_bdot_nt(do_bf, v_ref[0, :n, :]) - delta)
                dq_ref[0] = _bdot(ds, kv) * ATT_SCALE
                dk_ref[0, :n, :] += _bdot_tn(ds, q_bf) * ATT_SCALE
                dv_ref[0, :n, :] += _bdot_tn(p, do_bf)
                dcq_ref[0] = jnp.sum(ds, axis=-1, keepdims=True)
                dck_ref[0, :, :n] -= jnp.sum(ds, axis=0, keepdims=True)

        @pl.when((pl.program_id(0) == H - 1) & (qi == s // t - 1))
        def _():
            wait()

    qb = pl.BlockSpec((1, t, N), lambda h, i: (h, i, 0))
    kb = pl.BlockSpec((1, s, N), lambda h, i: (h, 0, 0))
    cqb = pl.BlockSpec((1, t, 1), lambda h, i: (h, i, 0))
    ckb = pl.BlockSpec((1, 1, s), lambda h, i: (h, 0, 0))
    f32 = lambda *shape: jax.ShapeDtypeStruct(shape, F32)
    out = pl.pallas_call(
        body, name="fox_attn_bwd", grid=(H, s // t),
        in_specs=[qb, kb, kb, cqb, ckb, qb, cqb, qb] + _hbm_specs(nx), out_specs=[qb, kb, kb, cqb, ckb] + _hbm_specs(nx),
        out_shape=[f32(H, s, N), f32(H, s, N), f32(H, s, N), f32(H, s, 1), f32(H, 1, s)]
        + _received_shapes(slabs, owners),
        scratch_shapes=_exchange_scratch(nx),
        compiler_params=_params("arbitrary", "arbitrary"))(q, k, v, cum_q, cum_k, o, lse, do, *slabs)
    return out[:5], out[5:]


def _head_param(p):
    return p.reshape(H, 1, N)


def _local_step(x, target, w, p):
    mu = p["shift_mu"]
    pre_params = (_head_param(mu[:, 0:DA]), _head_param(mu[:, DA:2 * DA]), _head_param(mu[:, 2 * DA:3 * DA]),
                  _head_param(mu[:, 3 * DA + 2 * RANK:]), mu[:, 3 * DA:3 * DA + RANK],
                  mu[:, 3 * DA + RANK:3 * DA + 2 * RANK],
                  w["w_lora_up"].astype(F32), _head_param(p["w0"]), w["a_lora_up"].astype(F32), _head_param(p["a0"]),
                  _head_param(p["k_k"]), _head_param(p["k_a"]))
    post_params = (_head_param(p["lnx_w"]), _head_param(p["lnx_b"]), _head_param(p["r_k"]))
    q_g, k_g = p["q_norm_g"].reshape(1, 1, N), p["k_norm_g"].reshape(1, 1, N)
    f_b = jnp.pad(p["f_bias"], ((0, 0), (0, NF - H)))
    fg = p["final_norm_g"].reshape(1, D)

    h = _rms_fwd(x, p["norm_g"])
    ua = _proj(h, w["in_a"], "proj_a")
    ub = _proj(h, w["in_b"], "proj_b")
    ug = _proj(h, w["in_g"], "proj_g")
    uf = _proj(h, w["in_f"], "proj_f")
    r, lw, cl, k2, v, av, bv, gg = _rwkv_pre_fwd(ua, pre_params)
    y, ckpt, pinv = _wkv_fwd((r, lw, cl, k2, v, av, bv))
    ya = _rwkv_post_fwd(y, r, k2, v, gg, post_params)
    qn, kn, vf, cum = _fox_pre_fwd(ub, uf, q_g, k_g, f_b)
    cum_t = cum[:, :H].T
    cum_q, cum_k = cum_t[:, :, None], cum_t[:, None, :]
    o, lse = _fox_attn_fwd(qn, kn, vf, cum_q, cum_k)

    (loss, dfg, dwo, dwoa, dwob, dx2, dya, do, dgate_b, dug) = _tail(
        x, target, ya, o, ub, ug, w["w_out_a"], w["w_out_b"], w["w_out"], fg)
    everyone = (0, N_DEV)
    (dqn, dkn, dvf, dcq, dck), (recv_woa, recv_wob, recv_wo) = _fox_attn_bwd(
        qn, kn, vf, cum_q, cum_k, o, lse, do,
        (_col_slabs(dwoa), _col_slabs(dwob), dwo.astype(BF16).reshape(N_DEV, D // N_DEV, D)), (everyone,) * 3)
    pad_f = lambda a: jnp.pad(a.T, ((0, 0), (0, NF - H)))
    dub, duf, dqg, dkg, dfb = _fox_pre_bwd(ub, uf, q_g, k_g, f_b, dqn, dkn, dvf, dgate_b,
                                           pad_f(dcq[:, :, 0]), pad_f(dck.reshape(H, -1)))
    dwt_b, dwt_g, dwt_f = (_proj_wgrad(h, du, name) for du, name in ((dub, "wgrad_b"), (dug, "wgrad_g"), (duf, "wgrad_f")))
    dy, dr_p, dk_p, dv_p, dgg, dlnw, dlnb, drk = _rwkv_post_bwd(y, r, k2, v, gg, post_params, dya)

    early = _slab_wt_grad((dwt_b, dwt_g, dwt_f), (_WT_SEGMENTS[1], _WT_SEGMENTS[2], _WT_SEGMENTS[3]), EARLY_FROM, N_DEV,
                          "slab_wt_early")
    (dr_s, dlw, dcl, dk_s, dv_s, dav, dbv), (recv_early,) = _wkv_bwd(
        (r, lw, cl, k2, v, av, bv), ckpt, pinv, dy, (early,), ((EARLY_FROM, N_DEV),))
    pre_out = _rwkv_pre_bwd(ua, pre_params, (dr_s, dr_p, dlw, dcl, dk_s, dk_p, dv_s, dv_p, dav, dbv, dgg))
    dua, dpre = pre_out[0], pre_out[1:]
    dwt_a = _proj_wgrad(h, dua, "wgrad_a")

    flat = lambda a: a.reshape(1, -1)
    small = {
        "final_norm_g": dfg, "w0": dpre[7], "a0": dpre[9], "k_k": dpre[10], "k_a": dpre[11], "r_k": drk, "lnx_w": dlnw,
        "lnx_b": dlnb, "q_norm_g": dqg, "k_norm_g": dkg, "f_bias": dfb[:, :H],
        "shift_mu": jnp.concatenate([flat(dpre[0]), flat(dpre[1]), flat(dpre[2]), dpre[4], dpre[5], flat(dpre[3])], axis=1),
    }
    late = _slab_wt_grad((dwt_a, dwt_b), (_WT_SEGMENTS[0], _WT_SEGMENTS[1]), 0, EARLY_FROM, "slab_wt_late")
    late = _chip_sums(late, _pair_swap(late, 0, "pair_swap_late"), 0, "chip_sums_late")
    loras = jnp.stack([dpre[6], dpre[8]], axis=1).astype(BF16)
    dx, dng, (recv_late, recv_lora, recv_small) = _proj_xgrad(
        x, p["norm_g"], dx2, (dua, dub, dug, duf), (w["in_a"], w["in_b"], w["in_g"], w["in_f"]),
        (late, loras, _pack_small(small, loss)), ((0, EARLY_FROM, "chips"), everyone, everyone))
    return dx, dng, (recv_early, recv_late), (recv_woa, recv_wob, recv_wo, recv_lora), recv_small


def _position():
    return lax.axis_index("x"), lax.axis_index("y"), lax.axis_index("c")


def _hbm_specs(n):
    return [pl.BlockSpec(memory_space=pl.ANY)] * n


def _all_gather(blocks, name):
    n = len(blocks)

    def body(*refs):
        x_refs, out_refs = refs[:n], refs[n:2 * n]
        send_sems, recv_sems, local_sems = refs[2 * n:]
        x, y, c = _position()
        me, sibling = (x, y, c), (x, y, 1 - c)
        chips = [(1 - x, y), (x, 1 - y), (1 - x, 1 - y)]

        def copy(a, k, blk, to, own=False):
            dst = out_refs[a].at[4 * blk[0] + 2 * blk[1] + blk[2]]
            return pltpu.make_async_remote_copy(
                src_ref=x_refs[a] if own else dst, dst_ref=dst, send_sem=send_sems.at[7 * a + k],
                recv_sem=recv_sems.at[7 * a + k], device_id=to, device_id_type=MESH)

        mine = [pltpu.make_async_copy(x_refs[a], out_refs[a].at[4 * x + 2 * y + c], local_sems.at[a]) for a in range(n)]
        for cp in mine:
            cp.start()
        first = []
        for a in range(n):
            first.append(copy(a, 0, me, sibling, own=True))
            first += [copy(a, 1 + j, me, (*chip, c), own=True) for j, chip in enumerate(chips)]
        for cp in first:
            cp.start()
        passed = []
        for j, chip in enumerate(chips):
            for a in range(n):
                copy(a, 1 + j, (*chip, c), me).wait_recv()
                passed.append(copy(a, 4 + j, (*chip, c), sibling))
                passed[-1].start()
        for a in range(n):
            copy(a, 0, sibling, me).wait_recv()
        for j, chip in enumerate(chips):
            for a in range(n):
                copy(a, 4 + j, (*chip, 1 - c), me).wait_recv()
        for cp in first + passed:
            cp.wait_send()
        for cp in mine:
            cp.wait()

    return pl.pallas_call(
        body, name=name, out_shape=[jax.ShapeDtypeStruct((N_DEV,) + b.shape, b.dtype) for b in blocks],
        in_specs=_hbm_specs(n), out_specs=_hbm_specs(n),
        scratch_shapes=[pltpu.SemaphoreType.DMA((7 * n,)), pltpu.SemaphoreType.DMA((7 * n,)),
                        pltpu.SemaphoreType.DMA((n,))],
    )(*blocks)


def _received_shapes(slabs, owners):
    return [jax.ShapeDtypeStruct((N_DEV // 2 if len(o) == 3 else N_DEV,) + s.shape[1:], s.dtype)
            for s, o in zip(slabs, owners)]


def _pair_swap(slabs, lo, name):
    n = slabs.shape[0]

    def body(s_ref, p_ref, send_sems, recv_sems):
        x, y, c = _position()

        def copy(i):
            return pltpu.make_async_remote_copy(src_ref=s_ref.at[i], dst_ref=p_ref.at[i], send_sem=send_sems.at[i],
                                                recv_sem=recv_sems.at[i], device_id=(x, y, 1 - c), device_id_type=MESH)

        def on_side(side):
            return [i for i in range(n) if (lo + i) % 2 == side]

        for side in (0, 1):
            @pl.when(c != side)
            def _(side=side):
                for i in on_side(side):
                    copy(i).start()
        for side in (0, 1):
            @pl.when(c == side)
            def _(side=side):
                for i in on_side(side):
                    copy(i).wait_recv()

            @pl.when(c != side)
            def _(side=side):
                for i in on_side(side):
                    copy(i).wait_send()

    return pl.pallas_call(
        body, name=name, out_shape=jax.ShapeDtypeStruct(slabs.shape, slabs.dtype),
        in_specs=_hbm_specs(1), out_specs=_hbm_specs(1)[0],
        scratch_shapes=[pltpu.SemaphoreType.DMA((n,)), pltpu.SemaphoreType.DMA((n,))])(slabs)


def _chip_sums(slabs, swapped, lo, name):
    n, rows, cols = slabs.shape
    tile = W_IN_COL_TILE

    def body(s_ref, p_ref, o_ref):
        c = lax.axis_index("c")
        for i in range(n):
            @pl.when(c == (lo + i) % 2)
            def _(i=i):
                o_ref[i] = (s_ref[i].astype(F32) + p_ref[i].astype(F32)).astype(BF16)

    blk = pl.BlockSpec((n, rows, tile), lambda j: (0, 0, j))
    return pl.pallas_call(
        body, name=name, grid=(cols // tile,), in_specs=[blk, blk], out_specs=blk,
        out_shape=jax.ShapeDtypeStruct(slabs.shape, BF16), compiler_params=_params("arbitrary"))(slabs, swapped)


def _exchange_scratch(n):
    return [pltpu.SemaphoreType.DMA((7 * n,)), pltpu.SemaphoreType.DMA((7 * n,)), pltpu.SemaphoreType.DMA((n,))]


def _exchange_ops(src_refs, dst_refs, owners, sems):
    send_sems, recv_sems, local_sems = sems
    n = len(src_refs)

    def guarded(a, dev, fn):
        lo, hi = owners[a][:2]
        if (lo, hi) == (0, N_DEV):
            fn()
        else:
            pl.when((dev >= lo) & (dev < hi))(fn)

    def src(a, dev):
        ref = src_refs[a]
        return ref.at[0] if ref.shape[0] == 1 else ref.at[dev - owners[a][0]]

    def run(sending, waiting):
        x, y, c = _position()
        me = 4 * x + 2 * y + c
        for a in range(n):
            by_chip = len(owners[a]) == 3
            slot = (lambda qx, qy, qc: 2 * qx + qy) if by_chip else (lambda qx, qy, qc: 4 * qx + 2 * qy + qc)
            mine = slot(x, y, c)
            local = lambda a=a, mine=mine: pltpu.make_async_copy(src(a, me), dst_refs[a].at[mine], local_sems.at[a])
            if sending:
                guarded(a, me, lambda local=local: local().start())
            for m in range(2, N_DEV, 2) if by_chip else range(1, N_DEV):
                px, py, pc = x ^ (m >> 2), y ^ ((m >> 1) & 1), c ^ (m & 1)
                peer = 4 * px + 2 * py + pc
                theirs = slot(px, py, pc)
                sem = dict(send_sem=send_sems.at[7 * a + m - 1], recv_sem=recv_sems.at[7 * a + m - 1],
                           device_id=(px, py, pc), device_id_type=MESH)
                send = lambda a=a, peer=peer, sem=sem, mine=mine: pltpu.make_async_remote_copy(
                    src_ref=src(a, peer), dst_ref=dst_refs[a].at[mine], **sem)
                recv = lambda a=a, sem=sem, theirs=theirs: pltpu.make_async_remote_copy(
                    src_ref=src(a, me), dst_ref=dst_refs[a].at[theirs], **sem)
                if sending:
                    guarded(a, peer, lambda send=send: send().start())
                if waiting:
                    guarded(a, me, lambda recv=recv: recv().wait_recv())
                    guarded(a, peer, lambda send=send: send().wait_send())
            if waiting:
                guarded(a, me, lambda local=local: local().wait())

    return functools.partial(run, True, False), functools.partial(run, False, True)


def _sum_slabs(r_ref):
    g = r_ref[0].astype(F32)
    for k in range(1, r_ref.shape[0]):
        g = g + r_ref[k].astype(F32)
    return g


def _adamw(g, w, m, v):
    m_new = ADAM_B1 * m + (1.0 - ADAM_B1) * g
    v_new = ADAM_B2 * v + (1.0 - ADAM_B2) * (g * g)
    m_hat = m_new / (1.0 - ADAM_B1 ** ADAM_STEP)
    v_hat = v_new / (1.0 - ADAM_B2 ** ADAM_STEP)
    return g, -ADAM_LR * (m_hat / (jnp.sqrt(v_hat) + ADAM_EPS) + ADAM_WD * w), m_new, v_new


def _adamw_w_in(recv_early, recv_late, w, m, v, slabs, owners):
    rows, cols = w.shape
    tile = W_IN_COL_TILE
    nx = len(slabs)

    def body(early_ref, late_ref, w_ref, m_ref, v_ref, *refs):
        src_refs, o_refs, dst_refs = refs[:nx], refs[nx:nx + 4], refs[nx + 4:2 * nx + 4]
        start, wait = _exchange_ops(src_refs, dst_refs, owners, refs[2 * nx + 4:])
        x, y, c = _position()
        early_owner = 4 * x + 2 * y + c >= EARLY_FROM

        @pl.when(pl.program_id(0) == 0)
        def _():
            start()

        def update(g):
            for o_ref, val in zip(o_refs, _adamw(g, w_ref[...], m_ref[...], v_ref[...])):
                o_ref[...] = val

        pl.when(early_owner)(lambda: update(_sum_slabs(early_ref)))
        pl.when(jnp.logical_not(early_owner))(lambda: update(_sum_slabs(late_ref)))

        @pl.when(pl.program_id(0) == cols // tile - 1)
        def _():
            wait()

    blk = pl.BlockSpec((rows, tile), lambda i: (0, i))
    slots = lambda r: pl.BlockSpec((r.shape[0], rows, tile), lambda i: (0, 0, i))
    out = pl.pallas_call(
        body, name="adamw_w_in", grid=(cols // tile,),
        in_specs=[slots(recv_early), slots(recv_late), blk, blk, blk] + _hbm_specs(nx),
        out_specs=[blk] * 4 + _hbm_specs(nx),
        out_shape=[jax.ShapeDtypeStruct((rows, cols), F32)] * 4 + _received_shapes(slabs, owners),
        scratch_shapes=_exchange_scratch(nx),
        compiler_params=_params("arbitrary"))(recv_early, recv_late, w, m, v, *slabs)
    return out[:4], out[4:]


def _adamw_misc(recvs, recv_small, recv_norm, params):
    names = list(params)
    flat = [a for n in names for a in params[n]]

    def body(woa_ref, wob_ref, wo_ref, lora_ref, small_ref, norm_ref, *refs):
        p_refs, o_refs = refs[:len(flat)], refs[len(flat):]
        g_small = _sum_slabs(small_ref)
        g_lora = _sum_slabs(lora_ref)
        grads = {"w_out_a": _sum_slabs(woa_ref), "w_out_b": _sum_slabs(wob_ref), "w_out": _sum_slabs(wo_ref),
                 "w_lora_up": g_lora[0], "a_lora_up": g_lora[1], "norm_g": _sum_slabs(norm_ref)}
        for n, (off, size) in SMALL_SLOTS.items():
            grads[n] = g_small[:, off:off + size]
        for i, n in enumerate(names):
            w_ref, m_ref, v_ref = p_refs[3 * i:3 * i + 3]
            for o_ref, val in zip(o_refs[4 * i:4 * i + 4], _adamw(grads[n], w_ref[...], m_ref[...], v_ref[...])):
                o_ref[...] = val
        o_refs[-1][...] = g_small[:, LOSS_SLOT:LOSS_SLOT + 1]

    out = pl.pallas_call(
        body, name="adamw_misc",
        out_shape=[jax.ShapeDtypeStruct(params[n][0].shape, F32) for n in names for _ in range(4)]
        + [jax.ShapeDtypeStruct((1, 1), F32)],
        compiler_params=_params())(*recvs, recv_small, recv_norm, *flat)
    return {n: out[4 * i:4 * i + 4] for i, n in enumerate(names)}, out[-1]


_WT_SEGMENTS = ((0, NA), (NA, NB), (NA + NB + H, NG), (NA + NB, H))


def _split_wt(gathered):
    tile = W_IN_COL_TILE

    def body(g_ref, *o_refs):
        full = jnp.concatenate([g_ref[j] for j in range(N_DEV)], axis=0)
        for o_ref, (row, n) in zip(o_refs, _WT_SEGMENTS):
            seg = full[row:row + n]
            if n < o_ref.shape[0]:
                seg = jnp.concatenate([seg, jnp.zeros((o_ref.shape[0] - n, tile), BF16)], axis=0)
            o_ref[...] = seg

    sizes = (NA, NB, NG, NF)
    return pl.pallas_call(
        body, name="split_wt", grid=(D // tile,),
        in_specs=[pl.BlockSpec((N_DEV, COLS_PER_DEV, tile), lambda i: (0, 0, i))],
        out_specs=[pl.BlockSpec((n, tile), lambda i: (0, i)) for n in sizes],
        out_shape=[jax.ShapeDtypeStruct((n, D), BF16) for n in sizes],
        compiler_params=_params("arbitrary"))(gathered)


def _slab_wt_grad(segments, seg_rows, dev_lo, dev_hi, name):
    tile = W_IN_COL_TILE
    k = len(segments)

    def body(*refs):
        seg_refs, o_ref = refs[:k], refs[k]
        for j in range(dev_lo, dev_hi):
            lo, hi = COLS_PER_DEV * j, COLS_PER_DEV * (j + 1)
            parts = []
            for ref, (row, n) in sorted(zip(seg_refs, seg_rows), key=lambda t: t[1][0]):
                first, last = max(lo, row), min(hi, row + n)
                if first < last:
                    parts.append(ref[first - row:last - row, :])
            o_ref[j - dev_lo] = (parts[0] if len(parts) == 1 else jnp.concatenate(parts, axis=0)).astype(BF16)

    return pl.pallas_call(
        body, name=name, grid=(D // tile,),
        in_specs=[pl.BlockSpec((s.shape[0], tile), lambda i: (0, i)) for s in segments],
        out_specs=pl.BlockSpec((dev_hi - dev_lo, COLS_PER_DEV, tile), lambda i: (0, 0, i)),
        out_shape=jax.ShapeDtypeStruct((dev_hi - dev_lo, COLS_PER_DEV, D), BF16),
        compiler_params=_params("arbitrary"))(*segments)


def _by_cols(a):
    return jnp.moveaxis(a, 0, 1).reshape(a.shape[1], -1)


def _col_slabs(a):
    return jnp.moveaxis(a.reshape(a.shape[0], N_DEV, -1), 1, 0).astype(BF16)


def _pack_small(grads, loss):
    pieces, at = [], 0
    for n, (off, size) in list(SMALL_SLOTS.items()) + [("loss", (LOSS_SLOT, 1))]:
        pieces += [jnp.zeros((off - at,), F32), (loss if n == "loss" else grads[n]).reshape(-1)]
        at = off + size
    return jnp.concatenate(pieces + [jnp.zeros((SMALL_LEN - at,), F32)]).reshape(1, 1, SMALL_LEN)


def _gather_weights(t):
    cast = lambda a: a.astype(BF16)
    loras = jnp.stack([t["w_lora_up"][0], t["a_lora_up"][0]])
    wt, woa, wob, wo, lora = _all_gather(
        [cast(t["w_in"][0].T), cast(t["w_out_a"][0]), cast(t["w_out_b"][0]), cast(t["w_out"][0]), cast(loras)],
        "weight_gather")
    in_a, in_b, in_g, in_f = _split_wt(wt)
    return {"in_a": in_a, "in_b": in_b, "in_g": in_g, "in_f": in_f, "w_out_a": _by_cols(woa), "w_out_b": _by_cols(wob),
            "w_out": wo.reshape(D, D), "w_lora_up": lora[:, 0], "a_lora_up": lora[:, 1]}


def kernel(x, norm_g, w_in, shift_mu, w_lora_up, w0, a_lora_up, a0, k_k, k_a, r_k, lnx_w, lnx_b, f_bias, q_norm_g, k_norm_g, w_out_a, w_out_b, w_out, final_norm_g, loss_target, m_norm_g, m_w_in, m_shift_mu, m_w_lora_up, m_w0, m_a_lora_up, m_a0, m_k_k, m_k_a, m_r_k, m_lnx_w, m_lnx_b, m_f_bias, m_q_norm_g, m_k_norm_g, m_w_out_a, m_w_out_b, m_w_out, m_final_norm_g, v_norm_g, v_w_in, v_shift_mu, v_w_lora_up, v_w0, v_a_lora_up, v_a0, v_k_k, v_k_a, v_r_k, v_lnx_w, v_lnx_b, v_f_bias, v_q_norm_g, v_k_norm_g, v_w_out_a, v_w_out_b, v_w_out, v_final_norm_g):
    names = ("norm_g", "w_in", "shift_mu", "w_lora_up", "w0", "a_lora_up", "a0", "k_k", "k_a", "r_k", "lnx_w", "lnx_b",
             "f_bias", "q_norm_g", "k_norm_g", "w_out_a", "w_out_b", "w_out", "final_norm_g")
    weights = dict(zip(names, (norm_g, w_in, shift_mu, w_lora_up, w0, a_lora_up, a0, k_k, k_a, r_k, lnx_w, lnx_b,
                               f_bias, q_norm_g, k_norm_g, w_out_a, w_out_b, w_out, final_norm_g)))
    m_in = dict(zip(names, (m_norm_g, m_w_in, m_shift_mu, m_w_lora_up, m_w0, m_a_lora_up, m_a0, m_k_k, m_k_a, m_r_k,
                            m_lnx_w, m_lnx_b, m_f_bias, m_q_norm_g, m_k_norm_g, m_w_out_a, m_w_out_b, m_w_out,
                            m_final_norm_g)))
    v_in = dict(zip(names, (v_norm_g, v_w_in, v_shift_mu, v_w_lora_up, v_w0, v_a_lora_up, v_a0, v_k_k, v_k_a, v_r_k,
                            v_lnx_w, v_lnx_b, v_f_bias, v_q_norm_g, v_k_norm_g, v_w_out_a, v_w_out_b, v_w_out,
                            v_final_norm_g)))

    matrices = ("w_out_a", "w_out_b", "w_out", "w_lora_up", "a_lora_up")
    as_2d = lambda n, a: a[0] if n in matrices else a.reshape(1, -1)

    full = _gather_weights(weights)
    dx, dng, recv_wt, recvs, recv_small = _local_step(
        x[0], loss_target[0], full, {n: as_2d(n, weights[n]) for n in ("norm_g",) + tuple(SMALL_SLOTS)})

    res, (recv_norm,) = _adamw_w_in(*recv_wt, w_in[0].T, m_w_in[0].T, v_w_in[0].T, (dng[None],), ((0, N_DEV),))
    outs = {"w_in": [r.T[None] for r in res]}
    misc = [n for n in names if n != "w_in"]
    res, loss_sum = _adamw_misc(recvs, recv_small, recv_norm,
                                {n: tuple(as_2d(n, t[n]) for t in (weights, m_in, v_in)) for n in misc})
    for n in misc:
        outs[n] = [r.reshape(weights[n].shape) for r in res[n]]
    return (loss_sum.reshape(()), dx[None], *[outs[n][i] for i in range(4) for n in names])
```

```python
import functools
import math

import jax
import jax.numpy as jnp
from jax import lax
from jax.experimental import pallas as pl
from jax.experimental.pallas import tpu as pltpu

F32 = jnp.float32
BF16 = jnp.bfloat16
HI = lax.Precision.HIGHEST
MESH = pl.DeviceIdType.MESH

N_DEV = 8
D = 1024
H = 8
N = 64
DA = H * N
RANK = 64
NA = 4 * DA + 2 * RANK
NB = 4 * DA
NG = 2 * D
NF = 128
IN_COLS = NA + NB + H + NG
COLS_PER_DEV = IN_COLS // N_DEV
RMS_EPS = 1e-6
LNX_EPS = 64e-5
ATT_SCALE = N ** -0.5

ADAM_LR = 0.001
ADAM_B1 = 0.9
ADAM_B2 = 0.999
ADAM_EPS = 1e-08
ADAM_WD = 0.01
ADAM_STEP = 10

LANES = 128
WKV_CHUNK = 64
TOK_TILE = 256
HEAD_TILE = 128
ATT_TILE = 256
ATT_GROUPS = 8
VMEM_LIMIT = 56 * 1024 * 1024

SMALL_SLOTS = {"final_norm_g": (0, D), "shift_mu": (D, NA), "w0": (3200, DA), "a0": (3712, DA), "k_k": (4224, DA),
               "k_a": (4736, DA), "r_k": (5248, DA), "lnx_w": (5760, DA), "lnx_b": (6272, DA), "q_norm_g": (6784, N),
               "k_norm_g": (6912, N), "f_bias": (7040, H)}
LOSS_SLOT = 7168
SMALL_LEN = 7296
W_IN_COL_TILE = 256
EARLY_FROM = -(-NA // COLS_PER_DEV)


def _params(*sem):
    return pltpu.CompilerParams(dimension_semantics=sem or None, vmem_limit_bytes=VMEM_LIMIT)


def _bdot(a, b):
    return jnp.dot(a.astype(BF16), b.astype(BF16), preferred_element_type=F32)


def _bdot_nt(a, b):
    return lax.dot_general(a.astype(BF16), b.astype(BF16), (((1,), (1,)), ((), ())), preferred_element_type=F32)


def _bdot_tn(a, b):
    return lax.dot_general(a.astype(BF16), b.astype(BF16), (((0,), (0,)), ((), ())), preferred_element_type=F32)


def _sigmoid(x):
    return 1.0 / (1.0 + jnp.exp(-x))


def _softplus(x):
    return jnp.maximum(x, 0.0) + jnp.log(1.0 + jnp.exp(-jnp.abs(x)))


def _heads(ref, col0):
    return jnp.stack([ref[:, col0 + N * h:col0 + N * (h + 1)] for h in range(H)])


def _store_heads(ref, col0, val):
    for h in range(H):
        ref[:, col0 + N * h:col0 + N * (h + 1)] = val[h]


def _lerp(c, s, mu):
    return c + (s - c) * mu


def _rwkv_pre(rc, rs, kc, ks, vc, vs, gc, gs, wdc, wds, adc, ads,
              mu_r, mu_k, mu_v, mu_g, mu_wd, mu_ad, w_up, w0, a_up, a0, k_k, k_a):
    r = _lerp(rc, rs, mu_r)
    k = _lerp(kc, ks, mu_k)
    v = _lerp(vc, vs, mu_v)
    g = _lerp(gc, gs, mu_g)
    wd = _lerp(wdc, wds, mu_wd)
    ad = _lerp(adc, ads, mu_ad)
    t = wd.shape[0]
    bdims = (((2,), (1,)), ((0,), (0,)))
    tw = jnp.broadcast_to(jnp.tanh(wd).astype(BF16)[None], (H, t, RANK))
    z = w0 + lax.dot_general(tw, w_up.astype(BF16), bdims, preferred_element_type=F32)
    w_raw = -_softplus(-z) - 0.5
    lw = -jnp.exp(w_raw)
    row = lax.broadcasted_iota(jnp.int32, (t, t), 0)
    col = lax.broadcasted_iota(jnp.int32, (t, t), 1)
    same_chunk = ((row >= col) & (row // WKV_CHUNK == col // WKV_CHUNK)).astype(F32)
    cl = jnp.einsum("hts,hsn->htn", jnp.broadcast_to(same_chunk[None], (H, t, t)), lw, precision=HI,
                    preferred_element_type=F32)
    adb = jnp.broadcast_to(ad.astype(BF16)[None], (H, t, RANK))
    alr = _sigmoid(a0 + lax.dot_general(adb, a_up.astype(BF16), bdims, preferred_element_type=F32))
    kk = k * k_k
    kk = kk / jnp.maximum(jnp.sqrt(jnp.sum(kk * kk, axis=-1, keepdims=True)), 1e-12)
    k2 = k * (1.0 + (alr - 1.0) * k_a)
    return r, lw, cl, k2, v, -kk, kk * alr, g


_MM_DIMS = {"nn": (((2,), (1,)), ((0,), (0,))), "nt": (((2,), (2,)), ((0,), (0,))), "tn": (((1,), (1,)), ((0,), (0,)))}


def _split(x):
    hi = x.astype(BF16)
    return hi, (x - hi.astype(F32)).astype(BF16)


def _dot3(a, b, kind):
    ah, al = _split(a)
    bh, bl = _split(b)
    dot = functools.partial(lax.dot_general, dimension_numbers=_MM_DIMS[kind], preferred_element_type=F32)
    return dot(ah, bh) + (dot(ah, bl) + dot(al, bh))


@functools.partial(jax.custom_vjp, nondiff_argnums=(2,))
def _mm(a, b, kind):
    return _dot3(a, b, kind)


def _mm_fwd(a, b, kind):
    return _dot3(a, b, kind), (a, b)


def _dot1(a, b, kind):
    return lax.dot_general(a.astype(BF16), b.astype(BF16), dimension_numbers=_MM_DIMS[kind], preferred_element_type=F32)


def _mm_bwd(kind, res, ct):
    a, b = res
    if kind == "nn":
        return _dot1(ct, b, "nt"), _dot1(a, ct, "tn")
    if kind == "nt":
        return _dot1(ct, b, "nn"), _dot1(ct, a, "tn")
    return _dot1(b, ct, "nt"), _dot1(a, ct, "nn")


_mm.defvjp(_mm_fwd, _mm_bwd)


def _chunk_masks(c):
    row = lax.broadcasted_iota(jnp.int32, (c, c), 0)
    col = lax.broadcasted_iota(jnp.int32, (c, c), 1)
    return (row >= col)[None], (row > col)[None], (row == col).astype(F32)[None]


def _wkv_aab(lw, cl, a, b):
    _, strict, _ = _chunk_masks(a.shape[1])
    return jnp.where(strict, _mm(a * jnp.exp(cl - lw), b * jnp.exp(-cl), "nt"), 0.0)


def _tri_inverse(x):
    c = x.shape[1]
    p = _chunk_masks(c)[2] + x
    for _ in range(int(math.log2(c)) - 1):
        x = _mm(x, x, "nn")
        p = p + _mm(p, x, "nn")
    return p


def _wkv_apply(s0, r, lw, cl, k, v, a, b, p):
    c = r.shape[1]
    incl, strict, _ = _chunk_masks(c)
    gi = jnp.exp(-cl)
    left = jnp.concatenate([a * jnp.exp(cl - lw), r * jnp.exp(cl)], axis=1)
    right = jnp.concatenate([b * gi, k * gi], axis=1)
    m = _mm(left, right, "nt")
    z0 = _mm(left, s0, "nt")
    a_ak = jnp.where(strict, m[:, :c, c:], 0.0)
    row = lax.broadcasted_iota(jnp.int32, (c, 2 * c), 0)
    col = lax.broadcasted_iota(jnp.int32, (c, 2 * c), 1)
    a_r = jnp.where((row >= col % c)[None], m[:, c:, :], 0.0)
    sa = _mm(p, z0[:, :c] + _mm(a_ak, v, "nn"), "nn")
    sa_v = jnp.concatenate([sa, v], axis=1)
    y = z0[:, c:] + _mm(a_r, sa_v, "nn")
    s1 = (s0 + _mm(sa_v, right, "tn")) * jnp.exp(cl[:, c - 1:c, :])
    return y, s1


def _rwkv_post(y, r, k2, v, g, lnx_w, lnx_b, r_k):
    mean = jnp.mean(y, axis=-1, keepdims=True)
    yc = y - mean
    var = jnp.mean(yc * yc, axis=-1, keepdims=True)
    yn = yc * lax.rsqrt(var + LNX_EPS) * lnx_w + lnx_b
    bonus = jnp.sum(r * k2 * r_k, axis=-1, keepdims=True) * v
    return (yn + bonus) * (g * _sigmoid(g))


def _fox_pre(q, k, f, q_g, k_g, f_b):
    qn = q * lax.rsqrt(jnp.mean(q * q, axis=-1, keepdims=True) + RMS_EPS) * q_g
    kn = k * lax.rsqrt(jnp.mean(k * k, axis=-1, keepdims=True) + RMS_EPS) * k_g
    x = f + f_b
    return qn, kn, jnp.minimum(x, 0.0) - jnp.log(1.0 + jnp.exp(-jnp.abs(x)))


def _rms_fwd(x, g):
    s = x.shape[0]

    def body(x_ref, g_ref, h_ref):
        xv = x_ref[...]
        h_ref[...] = (xv * lax.rsqrt(jnp.mean(xv * xv, axis=-1, keepdims=True) + RMS_EPS) * g_ref[...]).astype(BF16)

    return pl.pallas_call(
        body, name="rms_fwd", grid=(s // TOK_TILE,),
        in_specs=[pl.BlockSpec((TOK_TILE, D), lambda i: (i, 0)), pl.BlockSpec((1, D), lambda i: (0, 0))],
        out_specs=pl.BlockSpec((TOK_TILE, D), lambda i: (i, 0)),
        out_shape=jax.ShapeDtypeStruct((s, D), BF16), compiler_params=_params("arbitrary"))(x, g)


def _proj(h, wt, name):
    s, n = h.shape[0], wt.shape[0]

    def body(h_ref, w_ref, o_ref):
        o_ref[...] = _bdot_nt(h_ref[...], w_ref[...])

    return pl.pallas_call(
        body, name=name, grid=(s // TOK_TILE,),
        in_specs=[pl.BlockSpec((TOK_TILE, D), lambda i: (i, 0)), pl.BlockSpec((n, D), lambda i: (0, 0))],
        out_specs=pl.BlockSpec((TOK_TILE, n), lambda i: (i, 0)),
        out_shape=jax.ShapeDtypeStruct((s, n), F32), compiler_params=_params("arbitrary"))(h, wt)


def _proj_wgrad(h, du, name):
    s, n = du.shape

    def body(h_ref, du_ref, o_ref):
        @pl.when(pl.program_id(0) == 0)
        def _():
            o_ref[...] = jnp.zeros_like(o_ref)

        o_ref[...] += _bdot_tn(du_ref[...], h_ref[...])

    return pl.pallas_call(
        body, name=name, grid=(s // TOK_TILE,),
        in_specs=[pl.BlockSpec((TOK_TILE, D), lambda i: (i, 0)), pl.BlockSpec((TOK_TILE, n), lambda i: (i, 0))],
        out_specs=pl.BlockSpec((n, D), lambda i: (0, 0)),
        out_shape=jax.ShapeDtypeStruct((n, D), F32), compiler_params=_params("arbitrary"))(h, du)


def _proj_xgrad(x, g, dx2, dus, ws, slabs, owners):
    s = x.shape[0]
    tile = HEAD_TILE
    k = len(dus)
    nx = len(slabs)
    n_in = 3 + 2 * k + nx

    def body(*refs):
        x_ref, g_ref, dx2_ref = refs[:3]
        du_refs, w_refs = refs[3:3 + k], refs[3 + k:3 + 2 * k]
        src_refs = refs[3 + 2 * k:3 + 2 * k + nx]
        dx_ref, dg_ref = refs[n_in:n_in + 2]
        dst_refs = refs[n_in + 2:n_in + 2 + nx]
        start, wait = _exchange_ops(src_refs, dst_refs, owners, refs[n_in + 2 + nx:])

        @pl.when(pl.program_id(0) == 0)
        def _():
            dg_ref[...] = jnp.zeros_like(dg_ref)
            start()

        dh = _bdot(du_refs[0][...], w_refs[0][...])
        for du_ref, w_ref in zip(du_refs[1:], w_refs[1:]):
            dh += _bdot(du_ref[...], w_ref[...])
        xv = x_ref[...]
        rs = lax.rsqrt(jnp.mean(xv * xv, axis=-1, keepdims=True) + RMS_EPS)
        xn = xv * rs
        dg_ref[...] += jnp.sum(dh * xn, axis=0, keepdims=True)
        dxn = dh * g_ref[...]
        dx_ref[...] = rs * (dxn - xn * jnp.mean(dxn * xn, axis=-1, keepdims=True)) + dx2_ref[...]

        @pl.when(pl.program_id(0) == s // tile - 1)
        def _():
            wait()

    tok = lambda n: pl.BlockSpec((tile, n), lambda i: (i, 0))
    fixed = lambda a: pl.BlockSpec(a.shape, lambda i: (0,) * a.ndim)
    out = pl.pallas_call(
        body, name="proj_xgrad", grid=(s // tile,),
        in_specs=([tok(D), fixed(g), tok(D)] + [tok(du.shape[1]) for du in dus] + [fixed(w) for w in ws]
                  + _hbm_specs(nx)),
        out_specs=[tok(D), pl.BlockSpec((1, D), lambda i: (0, 0))] + _hbm_specs(nx),
        out_shape=[jax.ShapeDtypeStruct((s, D), F32), jax.ShapeDtypeStruct((1, D), F32)] + _received_shapes(slabs, owners),
        scratch_shapes=_exchange_scratch(nx),
        compiler_params=_params("arbitrary"))(x, g, dx2, *dus, *ws, *slabs)
    return out[0], out[1], out[2:]


def _tail(x, target, ya, o, ub, ug, w_oa, w_ob, w_o, fg):
    s = x.shape[0]
    tile = TOK_TILE

    def body(x_ref, t_ref, ya_ref, o_ref, gb_ref, ug_ref, woa_ref, wob_ref, wo_ref, fg_ref,
             loss_ref, dfg_ref, dwo_ref, dwoa_ref, dwob_ref, dx2_ref, dya_ref, do_ref, dgb_ref, dug_ref):
        @pl.when(pl.program_id(0) == 0)
        def _():
            for r in (loss_ref, dfg_ref, dwo_ref, dwoa_ref, dwob_ref):
                r[...] = jnp.zeros_like(r)

        ya_v = ya_ref[...]
        gate_b = gb_ref[...]
        sg_b = _sigmoid(gate_b)
        silu_b = gate_b * sg_b
        o_v = jnp.concatenate([o_ref[h] for h in range(H)], axis=-1)
        yb_v = o_v * silu_b
        big_a = _bdot(ya_v, woa_ref[...])
        big_b = _bdot(yb_v, wob_ref[...])
        sa = _sigmoid(ug_ref[:, :D])
        sb = _sigmoid(ug_ref[:, D:])
        merged = sa * big_a + sb * big_b
        x2 = x_ref[...] + _bdot(merged, wo_ref[...])
        rs = lax.rsqrt(jnp.mean(x2 * x2, axis=-1, keepdims=True) + RMS_EPS)
        xn = x2 * rs
        err = xn * fg_ref[...] - t_ref[...]
        loss_ref[...] += (0.5 / D) * jnp.sum(err * err)
        dout = err * (1.0 / D)
        dfg_ref[...] += jnp.sum(dout * xn, axis=0, keepdims=True)
        dxn = dout * fg_ref[...]
        dx2 = rs * (dxn - xn * jnp.mean(dxn * xn, axis=-1, keepdims=True))
        dx2_ref[...] = dx2
        dwo_ref[...] += _bdot_tn(merged, dx2)
        dmerged = _bdot_nt(dx2, wo_ref[...])
        dbig_a = dmerged * sa
        dbig_b = dmerged * sb
        dug_ref[:, :D] = dmerged * big_a * sa * (1.0 - sa)
        dug_ref[:, D:] = dmerged * big_b * sb * (1.0 - sb)
        dwoa_ref[...] += _bdot_tn(ya_v, dbig_a)
        dwob_ref[...] += _bdot_tn(yb_v, dbig_b)
        dya_ref[...] = _bdot_nt(dbig_a, woa_ref[...])
        dyb = _bdot_nt(dbig_b, wob_ref[...])
        dgb_ref[...] = dyb * o_v * (sg_b * (1.0 + gate_b * (1.0 - sg_b)))
        _dov = dyb * silu_b
        for h in range(H):
            do_ref[h] = _dov[:, N * h:N * (h + 1)]

    tok = lambda n: pl.BlockSpec((tile, n), lambda i: (i, 0))
    hm = pl.BlockSpec((H, tile, N), lambda i: (0, i, 0))
    fixed = lambda shape: pl.BlockSpec(shape, lambda i: (0,) * len(shape))
    f32 = lambda *shape: jax.ShapeDtypeStruct(shape, F32)
    return pl.pallas_call(
        body, name="tail", grid=(s // tile,),
        in_specs=[tok(D), tok(D), tok(DA), hm, pl.BlockSpec((tile, DA), lambda i: (i, 3)), tok(NG),
                  fixed((DA, D)), fixed((DA, D)), fixed((D, D)), fixed((1, D))],
        out_specs=[fixed((1, 1)), fixed((1, D)), fixed((D, D)), fixed((DA, D)), fixed((DA, D)),
                   tok(D), tok(DA), hm, tok(DA), tok(NG)],
        out_shape=[f32(1, 1), f32(1, D), f32(D, D), f32(DA, D), f32(DA, D),
                   f32(s, D), f32(s, DA), f32(H, s, N), f32(s, DA), f32(s, NG)],
        compiler_params=_params("arbitrary"))(x, target, ya, o, ub, ug, w_oa, w_ob, w_o, fg)


_PRE_PARAM_SHAPES = ((H, 1, N),) * 4 + ((1, RANK),) * 2 + ((H, RANK, N), (H, 1, N), (H, RANK, N), (H, 1, N), (H, 1, N),
                                                              (H, 1, N))


def _pre_operands(ua_ref, prev_ref, first):
    cur = ua_ref[...]
    t = cur.shape[0]
    prev_row = jnp.where(first, 0.0, prev_ref[7:8, :])
    rows = lax.broadcasted_iota(jnp.int32, cur.shape, 0)
    sh = jnp.where(rows == 0, prev_row, pltpu.roll(cur, 1, axis=0))
    ops = []
    for c0 in (0, DA, 2 * DA, 3 * DA + 2 * RANK):
        ops.append(jnp.stack([cur[:, c0 + N * h:c0 + N * (h + 1)] for h in range(H)]))
        ops.append(jnp.stack([sh[:, c0 + N * h:c0 + N * (h + 1)] for h in range(H)]))
    for c0 in (3 * DA, 3 * DA + RANK):
        ops.append(cur[:, c0:c0 + RANK])
        ops.append(sh[:, c0:c0 + RANK])
    del t
    return ops


def _ua_specs(tile, order):
    blocks = tile // 8
    return [pl.BlockSpec((tile, NA), lambda i: (order(i), 0)),
            pl.BlockSpec((8, NA), lambda i: (jnp.maximum(order(i) * blocks - 1, 0), 0))]


def _rwkv_pre_fwd(ua, pre_params):
    s = ua.shape[0]
    tile = HEAD_TILE

    def body(ua_ref, prev_ref, *refs):
        p_refs, o_refs = refs[:len(pre_params)], refs[len(pre_params):]
        ops = _pre_operands(ua_ref, prev_ref, pl.program_id(0) == 0)
        outs = _rwkv_pre(*ops, *[p[...] for p in p_refs])
        for o_ref, val in zip(o_refs, outs):
            o_ref[...] = val

    hm = pl.BlockSpec((H, tile, N), lambda i: (0, i, 0))
    return pl.pallas_call(
        body, name="rwkv_pre_fwd", grid=(s // tile,),
        in_specs=_ua_specs(tile, lambda i: i) + [pl.BlockSpec(p.shape, lambda i, nd=p.ndim: (0,) * nd) for p in pre_params],
        out_specs=[hm] * 8, out_shape=[jax.ShapeDtypeStruct((H, s, N), F32)] * 8,
        compiler_params=_params("arbitrary"))(ua, ua, *pre_params)


def _rwkv_pre_bwd(ua, pre_params, cots):
    s = ua.shape[0]
    tile = HEAD_TILE
    nt = s // tile
    n_p = len(pre_params)

    def body(ua_ref, prev_ref, *refs):
        p_refs, c_refs = refs[:n_p], refs[n_p:n_p + 11]
        dua_ref = refs[n_p + 11]
        dp_refs = refs[n_p + 12:n_p + 12 + n_p]
        carry_ref = refs[-1]
        i = pl.program_id(0)

        @pl.when(i == 0)
        def _():
            carry_ref[...] = jnp.zeros_like(carry_ref)
            for r in dp_refs:
                r[...] = jnp.zeros_like(r)

        ops = _pre_operands(ua_ref, prev_ref, i == nt - 1)
        _, vjp = jax.vjp(_rwkv_pre, *ops, *[p[...] for p in p_refs])
        c = [r[...] for r in c_refs]
        grads = vjp((c[0] + c[1], c[2], c[3], c[4] + c[5], c[6] + c[7], c[8], c[9], c[10]))
        d_ops, d_par = grads[:12], grads[12:]
        for r, val in zip(dp_refs, d_par):
            r[...] += val
        d_cur = jnp.concatenate([d_ops[0][h] for h in range(H)] + [d_ops[2][h] for h in range(H)]
                                + [d_ops[4][h] for h in range(H)] + [d_ops[8], d_ops[10]]
                                + [d_ops[6][h] for h in range(H)], axis=-1)
        d_sh = jnp.concatenate([d_ops[1][h] for h in range(H)] + [d_ops[3][h] for h in range(H)]
                               + [d_ops[5][h] for h in range(H)] + [d_ops[9], d_ops[11]]
                               + [d_ops[7][h] for h in range(H)], axis=-1)
        rows = lax.broadcasted_iota(jnp.int32, d_sh.shape, 0)
        dua_ref[...] = d_cur + jnp.where(rows == tile - 1, carry_ref[...], pltpu.roll(d_sh, tile - 1, axis=0))
        carry_ref[...] = d_sh[0:1, :]

    rev = lambda i: nt - 1 - i
    hm = pl.BlockSpec((H, tile, N), lambda i: (0, rev(i), 0))
    fixed = [pl.BlockSpec(p.shape, lambda i, nd=p.ndim: (0,) * nd) for p in pre_params]
    return pl.pallas_call(
        body, name="rwkv_pre_bwd", grid=(nt,),
        in_specs=_ua_specs(tile, rev) + fixed + [hm] * 11,
        out_specs=[pl.BlockSpec((tile, NA), lambda i: (rev(i), 0))] + fixed,
        out_shape=[jax.ShapeDtypeStruct((s, NA), F32)] + [jax.ShapeDtypeStruct(p.shape, F32) for p in pre_params],
        scratch_shapes=[pltpu.VMEM((1, NA), F32)],
        compiler_params=_params("arbitrary"))(ua, ua, *pre_params, *cots)


def _wkv_fwd(seq):
    s = seq[0].shape[1]
    nc = s // WKV_CHUNK

    def body(r_ref, lw_ref, cl_ref, k_ref, v_ref, a_ref, b_ref, y_ref, ck_ref, p_ref, state):
        @pl.when(pl.program_id(0) == 0)
        def _():
            state[...] = jnp.zeros_like(state)

        s0 = state[...]
        ck_ref[0] = s0
        p = _tri_inverse(_wkv_aab(lw_ref[...], cl_ref[...], a_ref[...], b_ref[...]))
        p_ref[0] = p
        y, s1 = _wkv_apply(s0, r_ref[...], lw_ref[...], cl_ref[...], k_ref[...], v_ref[...], a_ref[...], b_ref[...], p)
        y_ref[...] = y
        state[...] = s1

    hm = pl.BlockSpec((H, WKV_CHUNK, N), lambda c: (0, c, 0))
    per_chunk = lambda m: pl.BlockSpec((1, H, m, m), lambda c: (c, 0, 0, 0))
    return pl.pallas_call(
        body, name="wkv_fwd", grid=(nc,), in_specs=[hm] * 7,
        out_specs=[hm, per_chunk(N), per_chunk(WKV_CHUNK)],
        out_shape=[jax.ShapeDtypeStruct((H, s, N), F32), jax.ShapeDtypeStruct((nc, H, N, N), F32),
                   jax.ShapeDtypeStruct((nc, H, WKV_CHUNK, WKV_CHUNK), F32)],
        scratch_shapes=[pltpu.VMEM((H, N, N), F32)], compiler_params=_params("arbitrary"))(*seq)


def _wkv_bwd(seq, ckpt, pinv, dy, slabs, owners):
    s = seq[0].shape[1]
    nc = s // WKV_CHUNK
    nx = len(slabs)

    def body(r_ref, lw_ref, cl_ref, k_ref, v_ref, a_ref, b_ref, ck_ref, p_ref, dy_ref, *refs):
        src_refs, d_refs, dst_refs = refs[:nx], refs[nx:nx + 7], refs[nx + 7:2 * nx + 7]
        dstate = refs[2 * nx + 7]
        start, wait = _exchange_ops(src_refs, dst_refs, owners, refs[2 * nx + 8:])

        @pl.when(pl.program_id(0) == 0)
        def _():
            dstate[...] = jnp.zeros_like(dstate)
            start()

        p = p_ref[0]
        lw, cl, a, b = lw_ref[...], cl_ref[...], a_ref[...], b_ref[...]
        _, vjp = jax.vjp(_wkv_apply, ck_ref[0], r_ref[...], lw, cl, k_ref[...], v_ref[...], a, b, p)
        ds0, dr, dlw, dcl, dk, dv, da, db, dp = vjp((dy_ref[...], dstate[...]))
        dstate[...] = ds0
        _, vjp_x = jax.vjp(_wkv_aab, lw, cl, a, b)
        dlw2, dcl2, da2, db2 = vjp_x(_mm(_mm(p, dp, "tn"), p, "nt"))
        for d_ref, val in zip(d_refs, (dr, dlw + dlw2, dcl + dcl2, dk, dv, da + da2, db + db2)):
            d_ref[...] = val

        @pl.when(pl.program_id(0) == nc - 1)
        def _():
            wait()

    hm = pl.BlockSpec((H, WKV_CHUNK, N), lambda c: (0, nc - 1 - c, 0))
    per_chunk = lambda m: pl.BlockSpec((1, H, m, m), lambda c: (nc - 1 - c, 0, 0, 0))
    out = pl.pallas_call(
        body, name="wkv_bwd", grid=(nc,),
        in_specs=[hm] * 7 + [per_chunk(N), per_chunk(WKV_CHUNK), hm] + _hbm_specs(nx),
        out_specs=[hm] * 7 + _hbm_specs(nx),
        out_shape=[jax.ShapeDtypeStruct((H, s, N), F32)] * 7 + _received_shapes(slabs, owners),
        scratch_shapes=[pltpu.VMEM((H, N, N), F32)] + _exchange_scratch(nx),
        compiler_params=_params("arbitrary"))(*seq, ckpt, pinv, dy, *slabs)
    return out[:7], out[7:]


def _rwkv_post_fwd(y, r, k2, v, g, post_params):
    s = y.shape[1]
    tile = HEAD_TILE

    def body(y_ref, r_ref, k_ref, v_ref, g_ref, w_ref, b_ref, rk_ref, o_ref):
        out = _rwkv_post(y_ref[...], r_ref[...], k_ref[...], v_ref[...], g_ref[...], w_ref[...], b_ref[...],
                         rk_ref[...])
        o_ref[...] = jnp.concatenate([out[h] for h in range(H)], axis=-1)

    hm = pl.BlockSpec((H, tile, N), lambda i: (0, i, 0))
    par = pl.BlockSpec((H, 1, N), lambda i: (0, 0, 0))
    return pl.pallas_call(
        body, name="rwkv_post_fwd", grid=(s // tile,), in_specs=[hm] * 5 + [par] * 3,
        out_specs=pl.BlockSpec((tile, DA), lambda i: (i, 0)), out_shape=jax.ShapeDtypeStruct((s, DA), F32),
        compiler_params=_params("arbitrary"))(y, r, k2, v, g, *post_params)


def _rwkv_post_bwd(y, r, k2, v, g, post_params, dya, slabs, lo):
    s = y.shape[1]
    tile = HEAD_TILE

    def body(y_ref, r_ref, k_ref, v_ref, g_ref, w_ref, b_ref, rk_ref, dya_ref, s_ref, *refs):
        d_refs, p_ref = refs[:8], refs[8]
        start, wait = _pair_swap_ops(s_ref, p_ref, lo, refs[9:])

        @pl.when(pl.program_id(0) == 0)
        def _():
            for ref in d_refs[5:]:
                ref[...] = jnp.zeros_like(ref)
            start()

        _, vjp = jax.vjp(_rwkv_post, y_ref[...], r_ref[...], k_ref[...], v_ref[...], g_ref[...], w_ref[...],
                         b_ref[...], rk_ref[...])
        grads = vjp(jnp.stack([dya_ref[:, N * h:N * (h + 1)] for h in range(H)]))
        for ref, val in zip(d_refs[:5], grads[:5]):
            ref[...] = val
        for ref, val in zip(d_refs[5:], grads[5:]):
            ref[...] += val

        @pl.when(pl.program_id(0) == s // tile - 1)
        def _():
            wait()

    hm = pl.BlockSpec((H, tile, N), lambda i: (0, i, 0))
    par = pl.BlockSpec((H, 1, N), lambda i: (0, 0, 0))
    return pl.pallas_call(
        body, name="rwkv_post_bwd", grid=(s // tile,),
        in_specs=[hm] * 5 + [par] * 3 + [pl.BlockSpec((tile, DA), lambda i: (i, 0))] + _hbm_specs(1),
        out_specs=[hm] * 5 + [par] * 3 + _hbm_specs(1),
        out_shape=[jax.ShapeDtypeStruct((H, s, N), F32)] * 5 + [jax.ShapeDtypeStruct((H, 1, N), F32)] * 3
        + [jax.ShapeDtypeStruct(slabs.shape, slabs.dtype)],
        scratch_shapes=_pair_swap_scratch(slabs.shape[0]),
        compiler_params=_params("arbitrary"))(y, r, k2, v, g, *post_params, dya, slabs)


def _tri(t):
    return (lax.broadcasted_iota(jnp.int32, (t, t), 0) >= lax.broadcasted_iota(jnp.int32, (t, t), 1)).astype(F32)


def _fox_pre_fwd(ub, uf, q_g, k_g, f_b):
    s = ub.shape[0]
    tile = HEAD_TILE

    def body(ub_ref, uf_ref, qg_ref, kg_ref, fb_ref, q_ref, k_ref, v_ref, cum_ref, carry):
        @pl.when(pl.program_id(0) == 0)
        def _():
            carry[...] = jnp.zeros_like(carry)

        qn, kn, logf = _fox_pre(_heads(ub_ref, 0), _heads(ub_ref, DA), uf_ref[...], qg_ref[...], kg_ref[...],
                                fb_ref[...])
        q_ref[...] = qn
        k_ref[...] = kn
        v_ref[...] = _heads(ub_ref, 2 * DA)
        cum = jnp.dot(_tri(tile), logf, precision=HI, preferred_element_type=F32) + carry[...]
        cum_ref[...] = cum
        carry[...] = cum[tile - 1:tile, :]

    hm = pl.BlockSpec((H, tile, N), lambda i: (0, i, 0))
    fixed = lambda shape: pl.BlockSpec(shape, lambda i: (0,) * len(shape))
    return pl.pallas_call(
        body, name="fox_pre_fwd", grid=(s // tile,),
        in_specs=[pl.BlockSpec((tile, NB), lambda i: (i, 0)), pl.BlockSpec((tile, NF), lambda i: (i, 0)),
                  fixed((1, 1, N)), fixed((1, 1, N)), fixed((1, NF))],
        out_specs=[hm] * 3 + [pl.BlockSpec((tile, NF), lambda i: (i, 0))],
        out_shape=[jax.ShapeDtypeStruct((H, s, N), F32)] * 3 + [jax.ShapeDtypeStruct((s, NF), F32)],
        scratch_shapes=[pltpu.VMEM((1, NF), F32)], compiler_params=_params("arbitrary"))(ub, uf, q_g, k_g, f_b)


def _fox_pre_bwd(ub, uf, q_g, k_g, f_b, dqn, dkn, dvf, dgate, dcum_q, dcum_k):
    s = ub.shape[0]
    tile = HEAD_TILE
    nt = s // tile

    def body(ub_ref, uf_ref, qg_ref, kg_ref, fb_ref, dq_ref, dk_ref, dv_ref, dgate_ref, dcq_ref, dck_ref,
             dub_ref, duf_ref, dqg_ref, dkg_ref, dfb_ref, carry):
        @pl.when(pl.program_id(0) == 0)
        def _():
            carry[...] = jnp.zeros_like(carry)
            for ref in (dqg_ref, dkg_ref, dfb_ref):
                ref[...] = jnp.zeros_like(ref)

        dcum = dcq_ref[...] + dck_ref[...]
        dlogf = lax.dot_general(_tri(tile), dcum, (((0,), (0,)), ((), ())), precision=HI,
                                preferred_element_type=F32) + carry[...]
        carry[...] = dlogf[0:1, :]
        _, vjp = jax.vjp(_fox_pre, _heads(ub_ref, 0), _heads(ub_ref, DA), uf_ref[...], qg_ref[...], kg_ref[...],
                         fb_ref[...])
        d_q, d_k, d_f, d_qg, d_kg, d_fb = vjp((dq_ref[...], dk_ref[...], dlogf))
        _store_heads(dub_ref, 0, d_q)
        _store_heads(dub_ref, DA, d_k)
        _store_heads(dub_ref, 2 * DA, dv_ref[...])
        dub_ref[:, 3 * DA:] = dgate_ref[...]
        duf_ref[...] = d_f
        dqg_ref[...] += d_qg
        dkg_ref[...] += d_kg
        dfb_ref[...] += d_fb

    rev = lambda i: nt - 1 - i
    hm = pl.BlockSpec((H, tile, N), lambda i: (0, rev(i), 0))
    tok = lambda n: pl.BlockSpec((tile, n), lambda i: (rev(i), 0))
    fixed = lambda shape: pl.BlockSpec(shape, lambda i: (0,) * len(shape))
    return pl.pallas_call(
        body, name="fox_pre_bwd", grid=(nt,),
        in_specs=[tok(NB), tok(NF), fixed((1, 1, N)), fixed((1, 1, N)), fixed((1, NF)), hm, hm, hm, tok(DA), tok(NF),
                  tok(NF)],
        out_specs=[tok(NB), tok(NF), fixed((1, 1, N)), fixed((1, 1, N)), fixed((1, NF))],
        out_shape=[jax.ShapeDtypeStruct((s, NB), F32), jax.ShapeDtypeStruct((s, NF), F32),
                   jax.ShapeDtypeStruct((1, 1, N), F32), jax.ShapeDtypeStruct((1, 1, N), F32),
                   jax.ShapeDtypeStruct((1, NF), F32)],
        scratch_shapes=[pltpu.VMEM((1, NF), F32)],
        compiler_params=_params("arbitrary"))(ub, uf, q_g, k_g, f_b, dqn, dkn, dvf, dgate, dcum_q, dcum_k)


def _att_groups(s):
    blocks = s // ATT_TILE
    per = max(1, blocks // ATT_GROUPS)
    return per, blocks // per


def _att_parts(n, width):
    return ([(0, n - width, False)] if n > width else []) + [(n - width, n, True)]


def _att_scores(q_bf, k_ref, ck_ref, lo, hi, masked, row_offset):
    scores = _bdot_nt(q_bf, k_ref[0, lo:hi, :]) - ck_ref[0, :, lo:hi]
    if masked:
        rows = row_offset + lax.broadcasted_iota(jnp.int32, scores.shape, 0)
        scores = jnp.where(rows >= lax.broadcasted_iota(jnp.int32, scores.shape, 1), scores, -1e30)
    return scores


def _fox_attn_fwd(q, k, v, cum_q, cum_k):
    s = q.shape[1]
    t = ATT_TILE
    per, groups = _att_groups(s)

    def body(q_ref, k_ref, v_ref, cq_ref, ck_ref, o_ref, lse_ref):
        qi = pl.program_id(1)
        for g in range(groups):
            @pl.when(qi // per == g)
            def _(g=g):
                q_bf = (q_ref[0] * ATT_SCALE).astype(BF16)
                parts = _att_parts((g + 1) * per * t, per * t)
                scores = [_att_scores(q_bf, k_ref, ck_ref, lo, hi, masked, (qi - g * per) * t)
                          for lo, hi, masked in parts]
                m = functools.reduce(jnp.maximum, [jnp.max(sc, axis=-1, keepdims=True) for sc in scores])
                l, acc = 0.0, 0.0
                for sc, (lo, hi, _) in zip(scores, parts):
                    p = jnp.exp(sc - m)
                    l += jnp.sum(p, axis=-1, keepdims=True)
                    acc += _bdot(p, v_ref[0, lo:hi, :])
                o_ref[0] = acc / l
                lse_ref[0] = m + jnp.log(l) + cq_ref[0]

    qb = pl.BlockSpec((1, t, N), lambda h, i: (h, i, 0))
    kb = pl.BlockSpec((1, s, N), lambda h, i: (h, 0, 0))
    return pl.pallas_call(
        body, name="fox_attn_fwd", grid=(H, s // t),
        in_specs=[qb, kb, kb, pl.BlockSpec((1, t, 1), lambda h, i: (h, i, 0)),
                  pl.BlockSpec((1, 1, s), lambda h, i: (h, 0, 0))],
        out_specs=[qb, pl.BlockSpec((1, t, 1), lambda h, i: (h, i, 0))],
        out_shape=[jax.ShapeDtypeStruct((H, s, N), F32), jax.ShapeDtypeStruct((H, s, 1), F32)],
        compiler_params=_params("arbitrary", "arbitrary"))(q, k, v, cum_q, cum_k)


def _fox_attn_bwd(q, k, v, cum_q, cum_k, o, lse, do, slabs, owners):
    s = q.shape[1]
    t = ATT_TILE
    per, groups = _att_groups(s)
    nx = len(slabs)

    def body(q_ref, k_ref, v_ref, cq_ref, ck_ref, o_ref, lse_ref, do_ref, *refs):
        src_refs, (dq_ref, dk_ref, dv_ref, dcq_ref, dck_ref) = refs[:nx], refs[nx:nx + 5]
        start, wait = _exchange_ops(src_refs, refs[nx + 5:2 * nx + 5], owners, refs[2 * nx + 5:])
        qi = pl.program_id(1)

        @pl.when((pl.program_id(0) == 0) & (qi == 0))
        def _():
            start()

        @pl.when(qi == 0)
        def _():
            for ref in (dk_ref, dv_ref, dck_ref):
                ref[...] = jnp.zeros_like(ref)

        for g in range(groups):
            @pl.when(qi // per == g)
            def _(g=g):
                q_bf, do_bf = (q_ref[0] * ATT_SCALE).astype(BF16), do_ref[0].astype(BF16)
                row_term = cq_ref[0] - lse_ref[0]
                delta = jnp.sum(do_ref[0] * o_ref[0], axis=-1, keepdims=True)
                dq, dcq = 0.0, 0.0
                for lo, hi, masked in _att_parts((g + 1) * per * t, per * t):
                    p = jnp.exp(_att_scores(q_bf, k_ref, ck_ref, lo, hi, masked, (qi - g * per) * t) + row_term)
                    ds = p * (_bdot_nt(do_bf, v_ref[0, lo:hi, :]) - delta)
                    dq += _bdot(ds, k_ref[0, lo:hi, :])
                    dcq += jnp.sum(ds, axis=-1, keepdims=True)
                    dk_ref[0, lo:hi, :] += _bdot_tn(ds, q_bf)
                    dv_ref[0, lo:hi, :] += _bdot_tn(p, do_bf)
                    dck_ref[0, :, lo:hi] -= jnp.sum(ds, axis=0, keepdims=True)
                dq_ref[0] = dq * ATT_SCALE
                dcq_ref[0] = dcq

        @pl.when((pl.program_id(0) == H - 1) & (qi == s // t - 1))
        def _():
            wait()

    qb = pl.BlockSpec((1, t, N), lambda h, i: (h, i, 0))
    kb = pl.BlockSpec((1, s, N), lambda h, i: (h, 0, 0))
    cqb = pl.BlockSpec((1, t, 1), lambda h, i: (h, i, 0))
    ckb = pl.BlockSpec((1, 1, s), lambda h, i: (h, 0, 0))
    f32 = lambda *shape: jax.ShapeDtypeStruct(shape, F32)
    out = pl.pallas_call(
        body, name="fox_attn_bwd", grid=(H, s // t),
        in_specs=[qb, kb, kb, cqb, ckb, qb, cqb, qb] + _hbm_specs(nx), out_specs=[qb, kb, kb, cqb, ckb] + _hbm_specs(nx),
        out_shape=[f32(H, s, N), f32(H, s, N), f32(H, s, N), f32(H, s, 1), f32(H, 1, s)]
        + _received_shapes(slabs, owners),
        scratch_shapes=_exchange_scratch(nx),
        compiler_params=_params("arbitrary", "arbitrary"))(q, k, v, cum_q, cum_k, o, lse, do, *slabs)
    return out[:5], out[5:]


def _head_param(p):
    return p.reshape(H, 1, N)


def _local_step(x, target, w, p):
    mu = p["shift_mu"]
    pre_params = (_head_param(mu[:, 0:DA]), _head_param(mu[:, DA:2 * DA]), _head_param(mu[:, 2 * DA:3 * DA]),
                  _head_param(mu[:, 3 * DA + 2 * RANK:]), mu[:, 3 * DA:3 * DA + RANK],
                  mu[:, 3 * DA + RANK:3 * DA + 2 * RANK],
                  w["w_lora_up"].astype(F32), _head_param(p["w0"]), w["a_lora_up"].astype(F32), _head_param(p["a0"]),
                  _head_param(p["k_k"]), _head_param(p["k_a"]))
    post_params = (_head_param(p["lnx_w"]), _head_param(p["lnx_b"]), _head_param(p["r_k"]))
    q_g, k_g = p["q_norm_g"].reshape(1, 1, N), p["k_norm_g"].reshape(1, 1, N)
    f_b = jnp.pad(p["f_bias"], ((0, 0), (0, NF - H)))
    fg = p["final_norm_g"].reshape(1, D)

    h = _rms_fwd(x, p["norm_g"])
    ua = _proj(h, w["in_a"], "proj_a")
    ub = _proj(h, w["in_b"], "proj_b")
    ug = _proj(h, w["in_g"], "proj_g")
    uf = _proj(h, w["in_f"], "proj_f")
    r, lw, cl, k2, v, av, bv, gg = _rwkv_pre_fwd(ua, pre_params)
    y, ckpt, pinv = _wkv_fwd((r, lw, cl, k2, v, av, bv))
    ya = _rwkv_post_fwd(y, r, k2, v, gg, post_params)
    qn, kn, vf, cum = _fox_pre_fwd(ub, uf, q_g, k_g, f_b)
    cum_t = cum[:, :H].T
    cum_q, cum_k = cum_t[:, :, None], cum_t[:, None, :]
    o, lse = _fox_attn_fwd(qn, kn, vf, cum_q, cum_k)

    (loss, dfg, dwo, dwoa, dwob, dx2, dya, do, dgate_b, dug) = _tail(
        x, target, ya, o, ub, ug, w["w_out_a"], w["w_out_b"], w["w_out"], fg)
    everyone = (0, N_DEV)
    (dqn, dkn, dvf, dcq, dck), (recv_woa, recv_wob, recv_wo) = _fox_attn_bwd(
        qn, kn, vf, cum_q, cum_k, o, lse, do,
        (_col_slabs(dwoa), _col_slabs(dwob), dwo.astype(BF16).reshape(N_DEV, D // N_DEV, D)), (everyone,) * 3)
    pad_f = lambda a: jnp.pad(a.T, ((0, 0), (0, NF - H)))
    dub, duf, dqg, dkg, dfb = _fox_pre_bwd(ub, uf, q_g, k_g, f_b, dqn, dkn, dvf, dgate_b,
                                           pad_f(dcq[:, :, 0]), pad_f(dck.reshape(H, -1)))
    dwt_b, dwt_g, dwt_f = (_proj_wgrad(h, du, name) for du, name in ((dub, "wgrad_b"), (dug, "wgrad_g"), (duf, "wgrad_f")))
    early = _slab_wt_grad((dwt_b, dwt_g, dwt_f), (_WT_SEGMENTS[1], _WT_SEGMENTS[2], _WT_SEGMENTS[3]), EARLY_FROM, N_DEV,
                          "slab_wt_early")
    dy, dr_p, dk_p, dv_p, dgg, dlnw, dlnb, drk, handed = _rwkv_post_bwd(y, r, k2, v, gg, post_params, dya, early,
                                                                          EARLY_FROM)
    early = _chip_sums(early, handed, EARLY_FROM, "chip_sums_early")
    (dr_s, dlw, dcl, dk_s, dv_s, dav, dbv), (recv_early,) = _wkv_bwd(
        (r, lw, cl, k2, v, av, bv), ckpt, pinv, dy, (early,), ((EARLY_FROM, N_DEV, "chips"),))
    pre_out = _rwkv_pre_bwd(ua, pre_params, (dr_s, dr_p, dlw, dcl, dk_s, dk_p, dv_s, dv_p, dav, dbv, dgg))
    dua, dpre = pre_out[0], pre_out[1:]
    dwt_a = _proj_wgrad(h, dua, "wgrad_a")

    flat = lambda a: a.reshape(1, -1)
    small = {
        "final_norm_g": dfg, "w0": dpre[7], "a0": dpre[9], "k_k": dpre[10], "k_a": dpre[11], "r_k": drk, "lnx_w": dlnw,
        "lnx_b": dlnb, "q_norm_g": dqg, "k_norm_g": dkg, "f_bias": dfb[:, :H],
        "shift_mu": jnp.concatenate([flat(dpre[0]), flat(dpre[1]), flat(dpre[2]), dpre[4], dpre[5], flat(dpre[3])], axis=1),
    }
    late = _slab_wt_grad((dwt_a, dwt_b), (_WT_SEGMENTS[0], _WT_SEGMENTS[1]), 0, EARLY_FROM, "slab_wt_late")
    late = _chip_sums(late, _pair_swap(late, 0, "pair_swap_late"), 0, "chip_sums_late")
    loras = jnp.stack([dpre[6], dpre[8]], axis=1).astype(BF16)
    dx, dng, (recv_late, recv_lora, recv_small) = _proj_xgrad(
        x, p["norm_g"], dx2, (dua, dub, dug, duf), (w["in_a"], w["in_b"], w["in_g"], w["in_f"]),
        (late, loras, _pack_small(small, loss)), ((0, EARLY_FROM, "chips"), everyone, everyone))
    return dx, dng, (recv_early, recv_late), (recv_woa, recv_wob, recv_wo, recv_lora), recv_small


def _position():
    return lax.axis_index("x"), lax.axis_index("y"), lax.axis_index("c")


def _hbm_specs(n):
    return [pl.BlockSpec(memory_space=pl.ANY)] * n


def _all_gather(blocks, name):
    n = len(blocks)

    def body(*refs):
        x_refs, out_refs = refs[:n], refs[n:2 * n]
        send_sems, recv_sems, local_sems = refs[2 * n:]
        x, y, c = _position()
        me, sibling = (x, y, c), (x, y, 1 - c)
        chips = [(1 - x, y), (x, 1 - y), (1 - x, 1 - y)]

        def copy(a, k, blk, to, own=False):
            dst = out_refs[a].at[4 * blk[0] + 2 * blk[1] + blk[2]]
            return pltpu.make_async_remote_copy(
                src_ref=x_refs[a] if own else dst, dst_ref=dst, send_sem=send_sems.at[7 * a + k],
                recv_sem=recv_sems.at[7 * a + k], device_id=to, device_id_type=MESH)

        mine = [pltpu.make_async_copy(x_refs[a], out_refs[a].at[4 * x + 2 * y + c], local_sems.at[a]) for a in range(n)]
        for cp in mine:
            cp.start()
        first = []
        for a in range(n):
            first.append(copy(a, 0, me, sibling, own=True))
            first += [copy(a, 1 + j, me, (*chip, c), own=True) for j, chip in enumerate(chips)]
        for cp in first:
            cp.start()
        passed = []
        for j, chip in enumerate(chips):
            for a in range(n):
                copy(a, 1 + j, (*chip, c), me).wait_recv()
                passed.append(copy(a, 4 + j, (*chip, c), sibling))
                passed[-1].start()
        for a in range(n):
            copy(a, 0, sibling, me).wait_recv()
        for j, chip in enumerate(chips):
            for a in range(n):
                copy(a, 4 + j, (*chip, 1 - c), me).wait_recv()
        for cp in first + passed:
            cp.wait_send()
        for cp in mine:
            cp.wait()

    return pl.pallas_call(
        body, name=name, out_shape=[jax.ShapeDtypeStruct((N_DEV,) + b.shape, b.dtype) for b in blocks],
        in_specs=_hbm_specs(n), out_specs=_hbm_specs(n),
        scratch_shapes=[pltpu.SemaphoreType.DMA((7 * n,)), pltpu.SemaphoreType.DMA((7 * n,)),
                        pltpu.SemaphoreType.DMA((n,))],
    )(*blocks)


def _received_shapes(slabs, owners):
    return [jax.ShapeDtypeStruct((N_DEV // 2 if len(o) == 3 else N_DEV,) + s.shape[1:], s.dtype)
            for s, o in zip(slabs, owners)]


def _pair_swap_scratch(n):
    return [pltpu.SemaphoreType.DMA((n,)), pltpu.SemaphoreType.DMA((n,))]


def _pair_swap_ops(s_ref, p_ref, lo, sems):
    send_sems, recv_sems = sems
    n = s_ref.shape[0]

    def run(sending):
        x, y, c = _position()
        for side in (0, 1):
            mine = [pltpu.make_async_remote_copy(src_ref=s_ref.at[i], dst_ref=p_ref.at[i], send_sem=send_sems.at[i],
                                                 recv_sem=recv_sems.at[i], device_id=(x, y, 1 - c), device_id_type=MESH)
                    for i in range(n) if (lo + i) % 2 == side]

            @pl.when(c != side)
            def _():
                for cp in mine:
                    cp.start() if sending else cp.wait_send()

            if not sending:
                @pl.when(c == side)
                def _():
                    for cp in mine:
                        cp.wait_recv()

    return functools.partial(run, True), functools.partial(run, False)


def _pair_swap(slabs, lo, name):
    n = slabs.shape[0]

    def body(s_ref, p_ref, *sems):
        start, wait = _pair_swap_ops(s_ref, p_ref, lo, sems)
        start()
        wait()

    return pl.pallas_call(
        body, name=name, out_shape=jax.ShapeDtypeStruct(slabs.shape, slabs.dtype),
        in_specs=_hbm_specs(1), out_specs=_hbm_specs(1)[0], scratch_shapes=_pair_swap_scratch(n))(slabs)


def _chip_sums(slabs, swapped, lo, name):
    n, rows, cols = slabs.shape
    tile = W_IN_COL_TILE

    def body(s_ref, p_ref, o_ref):
        c = lax.axis_index("c")
        for i in range(n):
            @pl.when(c == (lo + i) % 2)
            def _(i=i):
                o_ref[i] = (s_ref[i].astype(F32) + p_ref[i].astype(F32)).astype(BF16)

    blk = pl.BlockSpec((n, rows, tile), lambda j: (0, 0, j))
    return pl.pallas_call(
        body, name=name, grid=(cols // tile,), in_specs=[blk, blk], out_specs=blk,
        out_shape=jax.ShapeDtypeStruct(slabs.shape, BF16), compiler_params=_params("arbitrary"))(slabs, swapped)


def _exchange_scratch(n):
    return [pltpu.SemaphoreType.DMA((7 * n,)), pltpu.SemaphoreType.DMA((7 * n,)), pltpu.SemaphoreType.DMA((n,))]


def _exchange_ops(src_refs, dst_refs, owners, sems):
    send_sems, recv_sems, local_sems = sems
    n = len(src_refs)

    def guarded(a, dev, fn):
        lo, hi = owners[a][:2]
        if (lo, hi) == (0, N_DEV):
            fn()
        else:
            pl.when((dev >= lo) & (dev < hi))(fn)

    def src(a, dev):
        ref = src_refs[a]
        return ref.at[0] if ref.shape[0] == 1 else ref.at[dev - owners[a][0]]

    def run(sending, waiting):
        x, y, c = _position()
        me = 4 * x + 2 * y + c
        for a in range(n):
            by_chip = len(owners[a]) == 3
            slot = (lambda qx, qy, qc: 2 * qx + qy) if by_chip else (lambda qx, qy, qc: 4 * qx + 2 * qy + qc)
            mine = slot(x, y, c)
            local = lambda a=a, mine=mine: pltpu.make_async_copy(src(a, me), dst_refs[a].at[mine], local_sems.at[a])
            if sending:
                guarded(a, me, lambda local=local: local().start())
            for m in range(2, N_DEV, 2) if by_chip else range(1, N_DEV):
                px, py, pc = x ^ (m >> 2), y ^ ((m >> 1) & 1), c ^ (m & 1)
                peer = 4 * px + 2 * py + pc
                theirs = slot(px, py, pc)
                sem = dict(send_sem=send_sems.at[7 * a + m - 1], recv_sem=recv_sems.at[7 * a + m - 1],
                           device_id=(px, py, pc), device_id_type=MESH)
                send = lambda a=a, peer=peer, sem=sem, mine=mine: pltpu.make_async_remote_copy(
                    src_ref=src(a, peer), dst_ref=dst_refs[a].at[mine], **sem)
                recv = lambda a=a, sem=sem, theirs=theirs: pltpu.make_async_remote_copy(
                    src_ref=src(a, me), dst_ref=dst_refs[a].at[theirs], **sem)
                if sending:
                    guarded(a, peer, lambda send=send: send().start())
                if waiting:
                    guarded(a, me, lambda recv=recv: recv().wait_recv())
                    guarded(a, peer, lambda send=send: send().wait_send())
            if waiting:
                guarded(a, me, lambda local=local: local().wait())

    return functools.partial(run, True, False), functools.partial(run, False, True)


def _sum_slabs(r_ref):
    g = r_ref[0].astype(F32)
    for k in range(1, r_ref.shape[0]):
        g = g + r_ref[k].astype(F32)
    return g


def _adamw(g, w, m, v):
    m_new = ADAM_B1 * m + (1.0 - ADAM_B1) * g
    v_new = ADAM_B2 * v + (1.0 - ADAM_B2) * (g * g)
    m_hat = m_new / (1.0 - ADAM_B1 ** ADAM_STEP)
    v_hat = v_new / (1.0 - ADAM_B2 ** ADAM_STEP)
    return g, -ADAM_LR * (m_hat / (jnp.sqrt(v_hat) + ADAM_EPS) + ADAM_WD * w), m_new, v_new


def _adamw_w_in(recv_early, recv_late, w, m, v, slabs, owners):
    rows, cols = w.shape
    tile = W_IN_COL_TILE
    nx = len(slabs)

    def body(early_ref, late_ref, w_ref, m_ref, v_ref, *refs):
        src_refs, o_refs, dst_refs = refs[:nx], refs[nx:nx + 4], refs[nx + 4:2 * nx + 4]
        start, wait = _exchange_ops(src_refs, dst_refs, owners, refs[2 * nx + 4:])
        x, y, c = _position()
        early_owner = 4 * x + 2 * y + c >= EARLY_FROM

        @pl.when(pl.program_id(0) == 0)
        def _():
            start()

        def update(g):
            for o_ref, val in zip(o_refs, _adamw(g, w_ref[...], m_ref[...], v_ref[...])):
                o_ref[...] = val

        pl.when(early_owner)(lambda: update(_sum_slabs(early_ref)))
        pl.when(jnp.logical_not(early_owner))(lambda: update(_sum_slabs(late_ref)))

        @pl.when(pl.program_id(0) == cols // tile - 1)
        def _():
            wait()

    blk = pl.BlockSpec((rows, tile), lambda i: (0, i))
    slots = lambda r: pl.BlockSpec((r.shape[0], rows, tile), lambda i: (0, 0, i))
    out = pl.pallas_call(
        body, name="adamw_w_in", grid=(cols // tile,),
        in_specs=[slots(recv_early), slots(recv_late), blk, blk, blk] + _hbm_specs(nx),
        out_specs=[blk] * 4 + _hbm_specs(nx),
        out_shape=[jax.ShapeDtypeStruct((rows, cols), F32)] * 4 + _received_shapes(slabs, owners),
        scratch_shapes=_exchange_scratch(nx),
        compiler_params=_params("arbitrary"))(recv_early, recv_late, w, m, v, *slabs)
    return out[:4], out[4:]


def _adamw_misc(recvs, recv_small, recv_norm, params):
    names = list(params)
    flat = [a for n in names for a in params[n]]

    def body(woa_ref, wob_ref, wo_ref, lora_ref, small_ref, norm_ref, *refs):
        p_refs, o_refs = refs[:len(flat)], refs[len(flat):]
        g_small = _sum_slabs(small_ref)
        g_lora = _sum_slabs(lora_ref)
        grads = {"w_out_a": _sum_slabs(woa_ref), "w_out_b": _sum_slabs(wob_ref), "w_out": _sum_slabs(wo_ref),
                 "w_lora_up": g_lora[0], "a_lora_up": g_lora[1], "norm_g": _sum_slabs(norm_ref)}
        for n, (off, size) in SMALL_SLOTS.items():
            grads[n] = g_small[:, off:off + size]
        for i, n in enumerate(names):
            w_ref, m_ref, v_ref = p_refs[3 * i:3 * i + 3]
            for o_ref, val in zip(o_refs[4 * i:4 * i + 4], _adamw(grads[n], w_ref[...], m_ref[...], v_ref[...])):
                o_ref[...] = val
        o_refs[-1][...] = g_small[:, LOSS_SLOT:LOSS_SLOT + 1]

    out = pl.pallas_call(
        body, name="adamw_misc",
        out_shape=[jax.ShapeDtypeStruct(params[n][0].shape, F32) for n in names for _ in range(4)]
        + [jax.ShapeDtypeStruct((1, 1), F32)],
        compiler_params=_params())(*recvs, recv_small, recv_norm, *flat)
    return {n: out[4 * i:4 * i + 4] for i, n in enumerate(names)}, out[-1]


_WT_SEGMENTS = ((0, NA), (NA, NB), (NA + NB + H, NG), (NA + NB, H))


def _split_wt(gathered):
    tile = W_IN_COL_TILE

    def body(g_ref, *o_refs):
        full = jnp.concatenate([g_ref[j] for j in range(N_DEV)], axis=0)
        for o_ref, (row, n) in zip(o_refs, _WT_SEGMENTS):
            seg = full[row:row + n]
            if n < o_ref.shape[0]:
                seg = jnp.concatenate([seg, jnp.zeros((o_ref.shape[0] - n, tile), BF16)], axis=0)
            o_ref[...] = seg

    sizes = (NA, NB, NG, NF)
    return pl.pallas_call(
        body, name="split_wt", grid=(D // tile,),
        in_specs=[pl.BlockSpec((N_DEV, COLS_PER_DEV, tile), lambda i: (0, 0, i))],
        out_specs=[pl.BlockSpec((n, tile), lambda i: (0, i)) for n in sizes],
        out_shape=[jax.ShapeDtypeStruct((n, D), BF16) for n in sizes],
        compiler_params=_params("arbitrary"))(gathered)


def _slab_wt_grad(segments, seg_rows, dev_lo, dev_hi, name):
    tile = W_IN_COL_TILE
    k = len(segments)

    def body(*refs):
        seg_refs, o_ref = refs[:k], refs[k]
        for j in range(dev_lo, dev_hi):
            lo, hi = COLS_PER_DEV * j, COLS_PER_DEV * (j + 1)
            parts = []
            for ref, (row, n) in sorted(zip(seg_refs, seg_rows), key=lambda t: t[1][0]):
                first, last = max(lo, row), min(hi, row + n)
                if first < last:
                    parts.append(ref[first - row:last - row, :])
            o_ref[j - dev_lo] = (parts[0] if len(parts) == 1 else jnp.concatenate(parts, axis=0)).astype(BF16)

    return pl.pallas_call(
        body, name=name, grid=(D // tile,),
        in_specs=[pl.BlockSpec((s.shape[0], tile), lambda i: (0, i)) for s in segments],
        out_specs=pl.BlockSpec((dev_hi - dev_lo, COLS_PER_DEV, tile), lambda i: (0, 0, i)),
        out_shape=jax.ShapeDtypeStruct((dev_hi - dev_lo, COLS_PER_DEV, D), BF16),
        compiler_params=_params("arbitrary"))(*segments)


def _by_cols(a):
    return jnp.moveaxis(a, 0, 1).reshape(a.shape[1], -1)


def _col_slabs(a):
    return jnp.moveaxis(a.reshape(a.shape[0], N_DEV, -1), 1, 0).astype(BF16)


def _pack_small(grads, loss):
    pieces, at = [], 0
    for n, (off, size) in list(SMALL_SLOTS.items()) + [("loss", (LOSS_SLOT, 1))]:
        pieces += [jnp.zeros((off - at,), F32), (loss if n == "loss" else grads[n]).reshape(-1)]
        at = off + size
    return jnp.concatenate(pieces + [jnp.zeros((SMALL_LEN - at,), F32)]).reshape(1, 1, SMALL_LEN)


def _gather_weights(t):
    cast = lambda a: a.astype(BF16)
    loras = jnp.stack([t["w_lora_up"][0], t["a_lora_up"][0]])
    wt, woa, wob, wo, lora = _all_gather(
        [cast(t["w_in"][0].T), cast(t["w_out_a"][0]), cast(t["w_out_b"][0]), cast(t["w_out"][0]), cast(loras)],
        "weight_gather")
    in_a, in_b, in_g, in_f = _split_wt(wt)
    return {"in_a": in_a, "in_b": in_b, "in_g": in_g, "in_f": in_f, "w_out_a": _by_cols(woa), "w_out_b": _by_cols(wob),
            "w_out": wo.reshape(D, D), "w_lora_up": lora[:, 0], "a_lora_up": lora[:, 1]}


def kernel(x, norm_g, w_in, shift_mu, w_lora_up, w0, a_lora_up, a0, k_k, k_a, r_k, lnx_w, lnx_b, f_bias, q_norm_g, k_norm_g, w_out_a, w_out_b, w_out, final_norm_g, loss_target, m_norm_g, m_w_in, m_shift_mu, m_w_lora_up, m_w0, m_a_lora_up, m_a0, m_k_k, m_k_a, m_r_k, m_lnx_w, m_lnx_b, m_f_bias, m_q_norm_g, m_k_norm_g, m_w_out_a, m_w_out_b, m_w_out, m_final_norm_g, v_norm_g, v_w_in, v_shift_mu, v_w_lora_up, v_w0, v_a_lora_up, v_a0, v_k_k, v_k_a, v_r_k, v_lnx_w, v_lnx_b, v_f_bias, v_q_norm_g, v_k_norm_g, v_w_out_a, v_w_out_b, v_w_out, v_final_norm_g):
    names = ("norm_g", "w_in", "shift_mu", "w_lora_up", "w0", "a_lora_up", "a0", "k_k", "k_a", "r_k", "lnx_w", "lnx_b",
             "f_bias", "q_norm_g", "k_norm_g", "w_out_a", "w_out_b", "w_out", "final_norm_g")
    weights = dict(zip(names, (norm_g, w_in, shift_mu, w_lora_up, w0, a_lora_up, a0, k_k, k_a, r_k, lnx_w, lnx_b,
                               f_bias, q_norm_g, k_norm_g, w_out_a, w_out_b, w_out, final_norm_g)))
    m_in = dict(zip(names, (m_norm_g, m_w_in, m_shift_mu, m_w_lora_up, m_w0, m_a_lora_up, m_a0, m_k_k, m_k_a, m_r_k,
                            m_lnx_w, m_lnx_b, m_f_bias, m_q_norm_g, m_k_norm_g, m_w_out_a, m_w_out_b, m_w_out,
                            m_final_norm_g)))
    v_in = dict(zip(names, (v_norm_g, v_w_in, v_shift_mu, v_w_lora_up, v_w0, v_a_lora_up, v_a0, v_k_k, v_k_a, v_r_k,
                            v_lnx_w, v_lnx_b, v_f_bias, v_q_norm_g, v_k_norm_g, v_w_out_a, v_w_out_b, v_w_out,
                            v_final_norm_g)))

    matrices = ("w_out_a", "w_out_b", "w_out", "w_lora_up", "a_lora_up")
    as_2d = lambda n, a: a[0] if n in matrices else a.reshape(1, -1)

    full = _gather_weights(weights)
    dx, dng, recv_wt, recvs, recv_small = _local_step(
        x[0], loss_target[0], full, {n: as_2d(n, weights[n]) for n in ("norm_g",) + tuple(SMALL_SLOTS)})

    res, (recv_norm,) = _adamw_w_in(*recv_wt, w_in[0].T, m_w_in[0].T, v_w_in[0].T, (dng[None],), ((0, N_DEV),))
    outs = {"w_in": [r.T[None] for r in res]}
    misc = [n for n in names if n != "w_in"]
    res, loss_sum = _adamw_misc(recvs, recv_small, recv_norm,
                                {n: tuple(as_2d(n, t[n]) for t in (weights, m_in, v_in)) for n in misc})
    for n in misc:
        outs[n] = [r.reshape(weights[n].shape) for r in res[n]]
    return (loss_sum.reshape(()), dx[None], *[outs[n][i] for i in range(4) for n in names])
```

```python
import functools
import math

import jax
import jax.numpy as jnp
from jax import lax
from jax.experimental import pallas as pl
from jax.experimental.pallas import tpu as pltpu

F32 = jnp.float32
BF16 = jnp.bfloat16
HI = lax.Precision.HIGHEST
MESH = pl.DeviceIdType.MESH

N_DEV = 8
D = 1024
H = 8
N = 64
DA = H * N
RANK = 64
NA = 4 * DA + 2 * RANK
NB = 4 * DA
NG = 2 * D
NF = 128
IN_COLS = NA + NB + H + NG
COLS_PER_DEV = IN_COLS // N_DEV
RMS_EPS = 1e-6
LNX_EPS = 64e-5
ATT_SCALE = N ** -0.5

ADAM_LR = 0.001
ADAM_B1 = 0.9
ADAM_B2 = 0.999
ADAM_EPS = 1e-08
ADAM_WD = 0.01
ADAM_STEP = 10

LANES = 128
WKV_CHUNK = 64
TOK_TILE = 256
HEAD_TILE = 128
ATT_TILE = 256
ATT_GROUPS = 8
VMEM_LIMIT = 56 * 1024 * 1024

SMALL_SLOTS = {"final_norm_g": (0, D), "shift_mu": (D, NA), "w0": (3200, DA), "a0": (3712, DA), "k_k": (4224, DA),
               "k_a": (4736, DA), "r_k": (5248, DA), "lnx_w": (5760, DA), "lnx_b": (6272, DA), "q_norm_g": (6784, N),
               "k_norm_g": (6912, N), "f_bias": (7040, H)}
LOSS_SLOT = 7168
SMALL_LEN = 7296
W_IN_COL_TILE = 256
EARLY_FROM = -(-NA // COLS_PER_DEV)


def _params(*sem):
    return pltpu.CompilerParams(dimension_semantics=sem or None, vmem_limit_bytes=VMEM_LIMIT)


def _bdot(a, b):
    return jnp.dot(a.astype(BF16), b.astype(BF16), preferred_element_type=F32)


def _bdot_nt(a, b):
    return lax.dot_general(a.astype(BF16), b.astype(BF16), (((1,), (1,)), ((), ())), preferred_element_type=F32)


def _bdot_tn(a, b):
    return lax.dot_general(a.astype(BF16), b.astype(BF16), (((0,), (0,)), ((), ())), preferred_element_type=F32)


def _sigmoid(x):
    return 1.0 / (1.0 + jnp.exp(-x))


def _softplus(x):
    return jnp.maximum(x, 0.0) + jnp.log(1.0 + jnp.exp(-jnp.abs(x)))


def _heads(ref, col0):
    return jnp.stack([ref[:, col0 + N * h:col0 + N * (h + 1)] for h in range(H)])


def _store_heads(ref, col0, val):
    for h in range(H):
        ref[:, col0 + N * h:col0 + N * (h + 1)] = val[h]


def _lerp(c, s, mu):
    return c + (s - c) * mu


def _rwkv_pre(rc, rs, kc, ks, vc, vs, gc, gs, wdc, wds, adc, ads,
              mu_r, mu_k, mu_v, mu_g, mu_wd, mu_ad, w_up, w0, a_up, a0, k_k, k_a):
    r = _lerp(rc, rs, mu_r)
    k = _lerp(kc, ks, mu_k)
    v = _lerp(vc, vs, mu_v)
    g = _lerp(gc, gs, mu_g)
    wd = _lerp(wdc, wds, mu_wd)
    ad = _lerp(adc, ads, mu_ad)
    t = wd.shape[0]
    bdims = (((2,), (1,)), ((0,), (0,)))
    tw = jnp.broadcast_to(jnp.tanh(wd).astype(BF16)[None], (H, t, RANK))
    z = w0 + lax.dot_general(tw, w_up.astype(BF16), bdims, preferred_element_type=F32)
    w_raw = -_softplus(-z) - 0.5
    lw = -jnp.exp(w_raw)
    row = lax.broadcasted_iota(jnp.int32, (t, t), 0)
    col = lax.broadcasted_iota(jnp.int32, (t, t), 1)
    same_chunk = ((row >= col) & (row // WKV_CHUNK == col // WKV_CHUNK)).astype(F32)
    cl = jnp.einsum("hts,hsn->htn", jnp.broadcast_to(same_chunk[None], (H, t, t)), lw, precision=HI,
                    preferred_element_type=F32)
    adb = jnp.broadcast_to(ad.astype(BF16)[None], (H, t, RANK))
    alr = _sigmoid(a0 + lax.dot_general(adb, a_up.astype(BF16), bdims, preferred_element_type=F32))
    kk = k * k_k
    kk = kk / jnp.maximum(jnp.sqrt(jnp.sum(kk * kk, axis=-1, keepdims=True)), 1e-12)
    k2 = k * (1.0 + (alr - 1.0) * k_a)
    return r, lw, cl, k2, v, -kk, kk * alr, g


_MM_DIMS = {"nn": (((2,), (1,)), ((0,), (0,))), "nt": (((2,), (2,)), ((0,), (0,))), "tn": (((1,), (1,)), ((0,), (0,)))}


def _split(x):
    hi = x.astype(BF16)
    return hi, (x - hi.astype(F32)).astype(BF16)


def _dot3(a, b, kind):
    ah, al = _split(a)
    bh, bl = _split(b)
    dot = functools.partial(lax.dot_general, dimension_numbers=_MM_DIMS[kind], preferred_element_type=F32)
    return dot(ah, bh) + (dot(ah, bl) + dot(al, bh))


def _dot1(a, b, kind):
    return lax.dot_general(a.astype(BF16), b.astype(BF16), dimension_numbers=_MM_DIMS[kind], preferred_element_type=F32)


@functools.partial(jax.custom_vjp, nondiff_argnums=(2, 3))
def _mm(a, b, kind, fine=True):
    return _dot3(a, b, kind) if fine else _dot1(a, b, kind)


def _mm_fwd(a, b, kind, fine):
    return _mm(a, b, kind, fine), (a, b)


def _mm_bwd(kind, fine, res, ct):
    a, b = res
    if kind == "nn":
        return _dot1(ct, b, "nt"), _dot1(a, ct, "tn")
    if kind == "nt":
        return _dot1(ct, b, "nn"), _dot1(ct, a, "tn")
    return _dot1(b, ct, "nt"), _dot1(a, ct, "nn")


_mm.defvjp(_mm_fwd, _mm_bwd)


def _chunk_masks(c):
    row = lax.broadcasted_iota(jnp.int32, (c, c), 0)
    col = lax.broadcasted_iota(jnp.int32, (c, c), 1)
    return (row >= col)[None], (row > col)[None], (row == col).astype(F32)[None]


def _wkv_aab(fine, lw, cl, a, b):
    _, strict, _ = _chunk_masks(a.shape[1])
    return jnp.where(strict, _mm(a * jnp.exp(cl - lw), b * jnp.exp(-cl), "nt", fine), 0.0)


def _tri_inverse(x):
    c = x.shape[1]
    p = _chunk_masks(c)[2] + x
    for _ in range(int(math.log2(c)) - 1):
        x = _dot1(x, x, "nn")
        p = p + _dot1(p, x, "nn")
    return p


def _wkv_apply(fine, s0, r, lw, cl, k, v, a, b, p):
    c = r.shape[1]
    incl, strict, _ = _chunk_masks(c)
    mm = functools.partial(_mm, fine=fine)
    gi = jnp.exp(-cl)
    left = jnp.concatenate([a * jnp.exp(cl - lw), r * jnp.exp(cl)], axis=1)
    right = jnp.concatenate([b * gi, k * gi], axis=1)
    m = mm(left, right, "nt")
    z0 = mm(left, s0, "nt")
    a_ak = jnp.where(strict, m[:, :c, c:], 0.0)
    row = lax.broadcasted_iota(jnp.int32, (c, 2 * c), 0)
    col = lax.broadcasted_iota(jnp.int32, (c, 2 * c), 1)
    a_r = jnp.where((row >= col % c)[None], m[:, c:, :], 0.0)
    sa = mm(p, z0[:, :c] + mm(a_ak, v, "nn"), "nn")
    sa_v = jnp.concatenate([sa, v], axis=1)
    y = z0[:, c:] + mm(a_r, sa_v, "nn")
    s1 = (s0 + mm(sa_v, right, "tn")) * jnp.exp(cl[:, c - 1:c, :])
    return y, s1


def _rwkv_post(y, r, k2, v, g, lnx_w, lnx_b, r_k):
    mean = jnp.mean(y, axis=-1, keepdims=True)
    yc = y - mean
    var = jnp.mean(yc * yc, axis=-1, keepdims=True)
    yn = yc * lax.rsqrt(var + LNX_EPS) * lnx_w + lnx_b
    bonus = jnp.sum(r * k2 * r_k, axis=-1, keepdims=True) * v
    return (yn + bonus) * (g * _sigmoid(g))


def _fox_pre(q, k, f, q_g, k_g, f_b):
    qn = q * lax.rsqrt(jnp.mean(q * q, axis=-1, keepdims=True) + RMS_EPS) * q_g
    kn = k * lax.rsqrt(jnp.mean(k * k, axis=-1, keepdims=True) + RMS_EPS) * k_g
    x = f + f_b
    return qn, kn, jnp.minimum(x, 0.0) - jnp.log(1.0 + jnp.exp(-jnp.abs(x)))


def _rms_fwd(x, g):
    s = x.shape[0]

    def body(x_ref, g_ref, h_ref):
        xv = x_ref[...]
        h_ref[...] = (xv * lax.rsqrt(jnp.mean(xv * xv, axis=-1, keepdims=True) + RMS_EPS) * g_ref[...]).astype(BF16)

    return pl.pallas_call(
        body, name="rms_fwd", grid=(s // TOK_TILE,),
        in_specs=[pl.BlockSpec((TOK_TILE, D), lambda i: (i, 0)), pl.BlockSpec((1, D), lambda i: (0, 0))],
        out_specs=pl.BlockSpec((TOK_TILE, D), lambda i: (i, 0)),
        out_shape=jax.ShapeDtypeStruct((s, D), BF16), compiler_params=_params("arbitrary"))(x, g)


def _proj(h, wt, name):
    s, n = h.shape[0], wt.shape[0]

    def body(h_ref, w_ref, o_ref):
        o_ref[...] = _bdot_nt(h_ref[...], w_ref[...])

    return pl.pallas_call(
        body, name=name, grid=(s // TOK_TILE,),
        in_specs=[pl.BlockSpec((TOK_TILE, D), lambda i: (i, 0)), pl.BlockSpec((n, D), lambda i: (0, 0))],
        out_specs=pl.BlockSpec((TOK_TILE, n), lambda i: (i, 0)),
        out_shape=jax.ShapeDtypeStruct((s, n), F32), compiler_params=_params("arbitrary"))(h, wt)


def _proj_wgrad(h, du, name):
    s, n = du.shape

    def body(h_ref, du_ref, o_ref):
        @pl.when(pl.program_id(0) == 0)
        def _():
            o_ref[...] = jnp.zeros_like(o_ref)

        o_ref[...] += _bdot_tn(du_ref[...], h_ref[...])

    return pl.pallas_call(
        body, name=name, grid=(s // TOK_TILE,),
        in_specs=[pl.BlockSpec((TOK_TILE, D), lambda i: (i, 0)), pl.BlockSpec((TOK_TILE, n), lambda i: (i, 0))],
        out_specs=pl.BlockSpec((n, D), lambda i: (0, 0)),
        out_shape=jax.ShapeDtypeStruct((n, D), F32), compiler_params=_params("arbitrary"))(h, du)


def _proj_xgrad(x, g, dx2, dus, ws, slabs, owners):
    s = x.shape[0]
    tile = HEAD_TILE
    k = len(dus)
    nx = len(slabs)
    n_in = 3 + 2 * k + nx

    def body(*refs):
        x_ref, g_ref, dx2_ref = refs[:3]
        du_refs, w_refs = refs[3:3 + k], refs[3 + k:3 + 2 * k]
        src_refs = refs[3 + 2 * k:3 + 2 * k + nx]
        dx_ref, dg_ref = refs[n_in:n_in + 2]
        dst_refs = refs[n_in + 2:n_in + 2 + nx]
        start, wait = _exchange_ops(src_refs, dst_refs, owners, refs[n_in + 2 + nx:])

        @pl.when(pl.program_id(0) == 0)
        def _():
            dg_ref[...] = jnp.zeros_like(dg_ref)
            start()

        dh = _bdot(du_refs[0][...], w_refs[0][...])
        for du_ref, w_ref in zip(du_refs[1:], w_refs[1:]):
            dh += _bdot(du_ref[...], w_ref[...])
        xv = x_ref[...]
        rs = lax.rsqrt(jnp.mean(xv * xv, axis=-1, keepdims=True) + RMS_EPS)
        xn = xv * rs
        dg_ref[...] += jnp.sum(dh * xn, axis=0, keepdims=True)
        dxn = dh * g_ref[...]
        dx_ref[...] = rs * (dxn - xn * jnp.mean(dxn * xn, axis=-1, keepdims=True)) + dx2_ref[...]

        @pl.when(pl.program_id(0) == s // tile - 1)
        def _():
            wait()

    tok = lambda n: pl.BlockSpec((tile, n), lambda i: (i, 0))
    fixed = lambda a: pl.BlockSpec(a.shape, lambda i: (0,) * a.ndim)
    out = pl.pallas_call(
        body, name="proj_xgrad", grid=(s // tile,),
        in_specs=([tok(D), fixed(g), tok(D)] + [tok(du.shape[1]) for du in dus] + [fixed(w) for w in ws]
                  + _hbm_specs(nx)),
        out_specs=[tok(D), pl.BlockSpec((1, D), lambda i: (0, 0))] + _hbm_specs(nx),
        out_shape=[jax.ShapeDtypeStruct((s, D), F32), jax.ShapeDtypeStruct((1, D), F32)] + _received_shapes(slabs, owners),
        scratch_shapes=_exchange_scratch(nx),
        compiler_params=_params("arbitrary"))(x, g, dx2, *dus, *ws, *slabs)
    return out[0], out[1], out[2:]


def _tail(x, target, ya, o, ub, ug, w_oa, w_ob, w_o, fg):
    s = x.shape[0]
    tile = TOK_TILE

    def body(x_ref, t_ref, ya_ref, o_ref, gb_ref, ug_ref, woa_ref, wob_ref, wo_ref, fg_ref,
             loss_ref, dfg_ref, dwo_ref, dwoa_ref, dwob_ref, dx2_ref, dya_ref, do_ref, dgb_ref, dug_ref):
        @pl.when(pl.program_id(0) == 0)
        def _():
            for r in (loss_ref, dfg_ref, dwo_ref, dwoa_ref, dwob_ref):
                r[...] = jnp.zeros_like(r)

        ya_v = ya_ref[...]
        gate_b = gb_ref[...]
        sg_b = _sigmoid(gate_b)
        silu_b = gate_b * sg_b
        o_v = jnp.concatenate([o_ref[h] for h in range(H)], axis=-1)
        yb_v = o_v * silu_b
        big_a = _bdot(ya_v, woa_ref[...])
        big_b = _bdot(yb_v, wob_ref[...])
        sa = _sigmoid(ug_ref[:, :D])
        sb = _sigmoid(ug_ref[:, D:])
        merged = sa * big_a + sb * big_b
        x2 = x_ref[...] + _bdot(merged, wo_ref[...])
        rs = lax.rsqrt(jnp.mean(x2 * x2, axis=-1, keepdims=True) + RMS_EPS)
        xn = x2 * rs
        err = xn * fg_ref[...] - t_ref[...]
        loss_ref[...] += (0.5 / D) * jnp.sum(err * err)
        dout = err * (1.0 / D)
        dfg_ref[...] += jnp.sum(dout * xn, axis=0, keepdims=True)
        dxn = dout * fg_ref[...]
        dx2 = rs * (dxn - xn * jnp.mean(dxn * xn, axis=-1, keepdims=True))
        dx2_ref[...] = dx2
        dwo_ref[...] += _bdot_tn(merged, dx2)
        dmerged = _bdot_nt(dx2, wo_ref[...])
        dbig_a = dmerged * sa
        dbig_b = dmerged * sb
        dug_ref[:, :D] = dmerged * big_a * sa * (1.0 - sa)
        dug_ref[:, D:] = dmerged * big_b * sb * (1.0 - sb)
        dwoa_ref[...] += _bdot_tn(ya_v, dbig_a)
        dwob_ref[...] += _bdot_tn(yb_v, dbig_b)
        dya_ref[...] = _bdot_nt(dbig_a, woa_ref[...])
        dyb = _bdot_nt(dbig_b, wob_ref[...])
        dgb_ref[...] = dyb * o_v * (sg_b * (1.0 + gate_b * (1.0 - sg_b)))
        _dov = dyb * silu_b
        for h in range(H):
            do_ref[h] = _dov[:, N * h:N * (h + 1)]

    tok = lambda n: pl.BlockSpec((tile, n), lambda i: (i, 0))
    hm = pl.BlockSpec((H, tile, N), lambda i: (0, i, 0))
    fixed = lambda shape: pl.BlockSpec(shape, lambda i: (0,) * len(shape))
    f32 = lambda *shape: jax.ShapeDtypeStruct(shape, F32)
    return pl.pallas_call(
        body, name="tail", grid=(s // tile,),
        in_specs=[tok(D), tok(D), tok(DA), hm, pl.BlockSpec((tile, DA), lambda i: (i, 3)), tok(NG),
                  fixed((DA, D)), fixed((DA, D)), fixed((D, D)), fixed((1, D))],
        out_specs=[fixed((1, 1)), fixed((1, D)), fixed((D, D)), fixed((DA, D)), fixed((DA, D)),
                   tok(D), tok(DA), hm, tok(DA), tok(NG)],
        out_shape=[f32(1, 1), f32(1, D), f32(D, D), f32(DA, D), f32(DA, D),
                   f32(s, D), f32(s, DA), f32(H, s, N), f32(s, DA), f32(s, NG)],
        compiler_params=_params("arbitrary"))(x, target, ya, o, ub, ug, w_oa, w_ob, w_o, fg)


_PRE_PARAM_SHAPES = ((H, 1, N),) * 4 + ((1, RANK),) * 2 + ((H, RANK, N), (H, 1, N), (H, RANK, N), (H, 1, N), (H, 1, N),
                                                              (H, 1, N))


def _pre_operands(ua_ref, prev_ref, first):
    cur = ua_ref[...]
    t = cur.shape[0]
    prev_row = jnp.where(first, 0.0, prev_ref[7:8, :])
    rows = lax.broadcasted_iota(jnp.int32, cur.shape, 0)
    sh = jnp.where(rows == 0, prev_row, pltpu.roll(cur, 1, axis=0))
    ops = []
    for c0 in (0, DA, 2 * DA, 3 * DA + 2 * RANK):
        ops.append(jnp.stack([cur[:, c0 + N * h:c0 + N * (h + 1)] for h in range(H)]))
        ops.append(jnp.stack([sh[:, c0 + N * h:c0 + N * (h + 1)] for h in range(H)]))
    for c0 in (3 * DA, 3 * DA + RANK):
        ops.append(cur[:, c0:c0 + RANK])
        ops.append(sh[:, c0:c0 + RANK])
    del t
    return ops


def _ua_specs(tile, order):
    blocks = tile // 8
    return [pl.BlockSpec((tile, NA), lambda i: (order(i), 0)),
            pl.BlockSpec((8, NA), lambda i: (jnp.maximum(order(i) * blocks - 1, 0), 0))]


def _rwkv_pre_fwd(ua, pre_params):
    s = ua.shape[0]
    tile = HEAD_TILE

    def body(ua_ref, prev_ref, *refs):
        p_refs, o_refs = refs[:len(pre_params)], refs[len(pre_params):]
        ops = _pre_operands(ua_ref, prev_ref, pl.program_id(0) == 0)
        outs = _rwkv_pre(*ops, *[p[...] for p in p_refs])
        for o_ref, val in zip(o_refs, outs):
            o_ref[...] = val

    hm = pl.BlockSpec((H, tile, N), lambda i: (0, i, 0))
    return pl.pallas_call(
        body, name="rwkv_pre_fwd", grid=(s // tile,),
        in_specs=_ua_specs(tile, lambda i: i) + [pl.BlockSpec(p.shape, lambda i, nd=p.ndim: (0,) * nd) for p in pre_params],
        out_specs=[hm] * 8, out_shape=[jax.ShapeDtypeStruct((H, s, N), F32)] * 8,
        compiler_params=_params("arbitrary"))(ua, ua, *pre_params)


def _rwkv_pre_bwd(ua, pre_params, cots):
    s = ua.shape[0]
    tile = HEAD_TILE
    nt = s // tile
    n_p = len(pre_params)

    def body(ua_ref, prev_ref, *refs):
        p_refs, c_refs = refs[:n_p], refs[n_p:n_p + 11]
        dua_ref = refs[n_p + 11]
        dp_refs = refs[n_p + 12:n_p + 12 + n_p]
        carry_ref = refs[-1]
        i = pl.program_id(0)

        @pl.when(i == 0)
        def _():
            carry_ref[...] = jnp.zeros_like(carry_ref)
            for r in dp_refs:
                r[...] = jnp.zeros_like(r)

        ops = _pre_operands(ua_ref, prev_ref, i == nt - 1)
        _, vjp = jax.vjp(_rwkv_pre, *ops, *[p[...] for p in p_refs])
        c = [r[...] for r in c_refs]
        grads = vjp((c[0] + c[1], c[2], c[3], c[4] + c[5], c[6] + c[7], c[8], c[9], c[10]))
        d_ops, d_par = grads[:12], grads[12:]
        for r, val in zip(dp_refs, d_par):
            r[...] += val
        d_cur = jnp.concatenate([d_ops[0][h] for h in range(H)] + [d_ops[2][h] for h in range(H)]
                                + [d_ops[4][h] for h in range(H)] + [d_ops[8], d_ops[10]]
                                + [d_ops[6][h] for h in range(H)], axis=-1)
        d_sh = jnp.concatenate([d_ops[1][h] for h in range(H)] + [d_ops[3][h] for h in range(H)]
                               + [d_ops[5][h] for h in range(H)] + [d_ops[9], d_ops[11]]
                               + [d_ops[7][h] for h in range(H)], axis=-1)
        rows = lax.broadcasted_iota(jnp.int32, d_sh.shape, 0)
        dua_ref[...] = d_cur + jnp.where(rows == tile - 1, carry_ref[...], pltpu.roll(d_sh, tile - 1, axis=0))
        carry_ref[...] = d_sh[0:1, :]

    rev = lambda i: nt - 1 - i
    hm = pl.BlockSpec((H, tile, N), lambda i: (0, rev(i), 0))
    fixed = [pl.BlockSpec(p.shape, lambda i, nd=p.ndim: (0,) * nd) for p in pre_params]
    return pl.pallas_call(
        body, name="rwkv_pre_bwd", grid=(nt,),
        in_specs=_ua_specs(tile, rev) + fixed + [hm] * 11,
        out_specs=[pl.BlockSpec((tile, NA), lambda i: (rev(i), 0))] + fixed,
        out_shape=[jax.ShapeDtypeStruct((s, NA), F32)] + [jax.ShapeDtypeStruct(p.shape, F32) for p in pre_params],
        scratch_shapes=[pltpu.VMEM((1, NA), F32)],
        compiler_params=_params("arbitrary"))(ua, ua, *pre_params, *cots)


def _wkv_fwd(seq):
    s = seq[0].shape[1]
    nc = s // WKV_CHUNK

    def body(r_ref, lw_ref, cl_ref, k_ref, v_ref, a_ref, b_ref, y_ref, ck_ref, p_ref, state):
        @pl.when(pl.program_id(0) == 0)
        def _():
            state[...] = jnp.zeros_like(state)

        s0 = state[...]
        ck_ref[0] = s0
        p = _tri_inverse(_wkv_aab(True, lw_ref[...], cl_ref[...], a_ref[...], b_ref[...]))
        p_ref[0] = p
        y, s1 = _wkv_apply(True, s0, r_ref[...], lw_ref[...], cl_ref[...], k_ref[...], v_ref[...], a_ref[...],
                           b_ref[...], p)
        y_ref[...] = y
        state[...] = s1

    hm = pl.BlockSpec((H, WKV_CHUNK, N), lambda c: (0, c, 0))
    per_chunk = lambda m: pl.BlockSpec((1, H, m, m), lambda c: (c, 0, 0, 0))
    return pl.pallas_call(
        body, name="wkv_fwd", grid=(nc,), in_specs=[hm] * 7,
        out_specs=[hm, per_chunk(N), per_chunk(WKV_CHUNK)],
        out_shape=[jax.ShapeDtypeStruct((H, s, N), F32), jax.ShapeDtypeStruct((nc, H, N, N), F32),
                   jax.ShapeDtypeStruct((nc, H, WKV_CHUNK, WKV_CHUNK), F32)],
        scratch_shapes=[pltpu.VMEM((H, N, N), F32)], compiler_params=_params("arbitrary"))(*seq)


def _wkv_bwd(seq, ckpt, pinv, dy, slabs, owners):
    s = seq[0].shape[1]
    nc = s // WKV_CHUNK
    nx = len(slabs)

    def body(r_ref, lw_ref, cl_ref, k_ref, v_ref, a_ref, b_ref, ck_ref, p_ref, dy_ref, *refs):
        src_refs, d_refs, dst_refs = refs[:nx], refs[nx:nx + 7], refs[nx + 7:2 * nx + 7]
        dstate = refs[2 * nx + 7]
        start, wait = _exchange_ops(src_refs, dst_refs, owners, refs[2 * nx + 8:])

        @pl.when(pl.program_id(0) == 0)
        def _():
            dstate[...] = jnp.zeros_like(dstate)
            start()

        p = p_ref[0]
        lw, cl, a, b = lw_ref[...], cl_ref[...], a_ref[...], b_ref[...]
        _, vjp = jax.vjp(functools.partial(_wkv_apply, False), ck_ref[0], r_ref[...], lw, cl, k_ref[...], v_ref[...],
                         a, b, p)
        ds0, dr, dlw, dcl, dk, dv, da, db, dp = vjp((dy_ref[...], dstate[...]))
        dstate[...] = ds0
        _, vjp_x = jax.vjp(functools.partial(_wkv_aab, False), lw, cl, a, b)
        dlw2, dcl2, da2, db2 = vjp_x(_dot1(_dot1(p, dp, "tn"), p, "nt"))
        for d_ref, val in zip(d_refs, (dr, dlw + dlw2, dcl + dcl2, dk, dv, da + da2, db + db2)):
            d_ref[...] = val

        @pl.when(pl.program_id(0) == nc - 1)
        def _():
            wait()

    hm = pl.BlockSpec((H, WKV_CHUNK, N), lambda c: (0, nc - 1 - c, 0))
    per_chunk = lambda m: pl.BlockSpec((1, H, m, m), lambda c: (nc - 1 - c, 0, 0, 0))
    out = pl.pallas_call(
        body, name="wkv_bwd", grid=(nc,),
        in_specs=[hm] * 7 + [per_chunk(N), per_chunk(WKV_CHUNK), hm] + _hbm_specs(nx),
        out_specs=[hm] * 7 + _hbm_specs(nx),
        out_shape=[jax.ShapeDtypeStruct((H, s, N), F32)] * 7 + _received_shapes(slabs, owners),
        scratch_shapes=[pltpu.VMEM((H, N, N), F32)] + _exchange_scratch(nx),
        compiler_params=_params("arbitrary"))(*seq, ckpt, pinv, dy, *slabs)
    return out[:7], out[7:]


def _rwkv_post_fwd(y, r, k2, v, g, post_params):
    s = y.shape[1]
    tile = HEAD_TILE

    def body(y_ref, r_ref, k_ref, v_ref, g_ref, w_ref, b_ref, rk_ref, o_ref):
        out = _rwkv_post(y_ref[...], r_ref[...], k_ref[...], v_ref[...], g_ref[...], w_ref[...], b_ref[...],
                         rk_ref[...])
        o_ref[...] = jnp.concatenate([out[h] for h in range(H)], axis=-1)

    hm = pl.BlockSpec((H, tile, N), lambda i: (0, i, 0))
    par = pl.BlockSpec((H, 1, N), lambda i: (0, 0, 0))
    return pl.pallas_call(
        body, name="rwkv_post_fwd", grid=(s // tile,), in_specs=[hm] * 5 + [par] * 3,
        out_specs=pl.BlockSpec((tile, DA), lambda i: (i, 0)), out_shape=jax.ShapeDtypeStruct((s, DA), F32),
        compiler_params=_params("arbitrary"))(y, r, k2, v, g, *post_params)


def _rwkv_post_bwd(y, r, k2, v, g, post_params, dya, slabs, lo):
    s = y.shape[1]
    tile = HEAD_TILE

    def body(y_ref, r_ref, k_ref, v_ref, g_ref, w_ref, b_ref, rk_ref, dya_ref, s_ref, *refs):
        d_refs, p_ref = refs[:8], refs[8]
        start, wait = _pair_swap_ops(s_ref, p_ref, lo, refs[9:])

        @pl.when(pl.program_id(0) == 0)
        def _():
            for ref in d_refs[5:]:
                ref[...] = jnp.zeros_like(ref)
            start()

        _, vjp = jax.vjp(_rwkv_post, y_ref[...], r_ref[...], k_ref[...], v_ref[...], g_ref[...], w_ref[...],
                         b_ref[...], rk_ref[...])
        grads = vjp(jnp.stack([dya_ref[:, N * h:N * (h + 1)] for h in range(H)]))
        for ref, val in zip(d_refs[:5], grads[:5]):
            ref[...] = val
        for ref, val in zip(d_refs[5:], grads[5:]):
            ref[...] += val

        @pl.when(pl.program_id(0) == s // tile - 1)
        def _():
            wait()

    hm = pl.BlockSpec((H, tile, N), lambda i: (0, i, 0))
    par = pl.BlockSpec((H, 1, N), lambda i: (0, 0, 0))
    return pl.pallas_call(
        body, name="rwkv_post_bwd", grid=(s // tile,),
        in_specs=[hm] * 5 + [par] * 3 + [pl.BlockSpec((tile, DA), lambda i: (i, 0))] + _hbm_specs(1),
        out_specs=[hm] * 5 + [par] * 3 + _hbm_specs(1),
        out_shape=[jax.ShapeDtypeStruct((H, s, N), F32)] * 5 + [jax.ShapeDtypeStruct((H, 1, N), F32)] * 3
        + [jax.ShapeDtypeStruct(slabs.shape, slabs.dtype)],
        scratch_shapes=_pair_swap_scratch(slabs.shape[0]),
        compiler_params=_params("arbitrary"))(y, r, k2, v, g, *post_params, dya, slabs)


def _tri(t):
    return (lax.broadcasted_iota(jnp.int32, (t, t), 0) >= lax.broadcasted_iota(jnp.int32, (t, t), 1)).astype(F32)


def _fox_pre_fwd(ub, uf, q_g, k_g, f_b):
    s = ub.shape[0]
    tile = HEAD_TILE

    def body(ub_ref, uf_ref, qg_ref, kg_ref, fb_ref, q_ref, k_ref, v_ref, cum_ref, carry):
        @pl.when(pl.program_id(0) == 0)
        def _():
            carry[...] = jnp.zeros_like(carry)

        qn, kn, logf = _fox_pre(_heads(ub_ref, 0), _heads(ub_ref, DA), uf_ref[...], qg_ref[...], kg_ref[...],
                                fb_ref[...])
        q_ref[...] = qn
        k_ref[...] = kn
        v_ref[...] = _heads(ub_ref, 2 * DA)
        cum = jnp.dot(_tri(tile), logf, precision=HI, preferred_element_type=F32) + carry[...]
        cum_ref[...] = cum
        carry[...] = cum[tile - 1:tile, :]

    hm = pl.BlockSpec((H, tile, N), lambda i: (0, i, 0))
    fixed = lambda shape: pl.BlockSpec(shape, lambda i: (0,) * len(shape))
    return pl.pallas_call(
        body, name="fox_pre_fwd", grid=(s // tile,),
        in_specs=[pl.BlockSpec((tile, NB), lambda i: (i, 0)), pl.BlockSpec((tile, NF), lambda i: (i, 0)),
                  fixed((1, 1, N)), fixed((1, 1, N)), fixed((1, NF))],
        out_specs=[hm] * 3 + [pl.BlockSpec((tile, NF), lambda i: (i, 0))],
        out_shape=[jax.ShapeDtypeStruct((H, s, N), F32)] * 3 + [jax.ShapeDtypeStruct((s, NF), F32)],
        scratch_shapes=[pltpu.VMEM((1, NF), F32)], compiler_params=_params("arbitrary"))(ub, uf, q_g, k_g, f_b)


def _fox_pre_bwd(ub, uf, q_g, k_g, f_b, dqn, dkn, dvf, dgate, dcum_q, dcum_k):
    s = ub.shape[0]
    tile = HEAD_TILE
    nt = s // tile

    def body(ub_ref, uf_ref, qg_ref, kg_ref, fb_ref, dq_ref, dk_ref, dv_ref, dgate_ref, dcq_ref, dck_ref,
             dub_ref, duf_ref, dqg_ref, dkg_ref, dfb_ref, carry):
        @pl.when(pl.program_id(0) == 0)
        def _():
            carry[...] = jnp.zeros_like(carry)
            for ref in (dqg_ref, dkg_ref, dfb_ref):
                ref[...] = jnp.zeros_like(ref)

        dcum = dcq_ref[...] + dck_ref[...]
        dlogf = lax.dot_general(_tri(tile), dcum, (((0,), (0,)), ((), ())), precision=HI,
                                preferred_element_type=F32) + carry[...]
        carry[...] = dlogf[0:1, :]
        _, vjp = jax.vjp(_fox_pre, _heads(ub_ref, 0), _heads(ub_ref, DA), uf_ref[...], qg_ref[...], kg_ref[...],
                         fb_ref[...])
        d_q, d_k, d_f, d_qg, d_kg, d_fb = vjp((dq_ref[...], dk_ref[...], dlogf))
        _store_heads(dub_ref, 0, d_q)
        _store_heads(dub_ref, DA, d_k)
        _store_heads(dub_ref, 2 * DA, dv_ref[...])
        dub_ref[:, 3 * DA:] = dgate_ref[...]
        duf_ref[...] = d_f
        dqg_ref[...] += d_qg
        dkg_ref[...] += d_kg
        dfb_ref[...] += d_fb

    rev = lambda i: nt - 1 - i
    hm = pl.BlockSpec((H, tile, N), lambda i: (0, rev(i), 0))
    tok = lambda n: pl.BlockSpec((tile, n), lambda i: (rev(i), 0))
    fixed = lambda shape: pl.BlockSpec(shape, lambda i: (0,) * len(shape))
    return pl.pallas_call(
        body, name="fox_pre_bwd", grid=(nt,),
        in_specs=[tok(NB), tok(NF), fixed((1, 1, N)), fixed((1, 1, N)), fixed((1, NF)), hm, hm, hm, tok(DA), tok(NF),
                  tok(NF)],
        out_specs=[tok(NB), tok(NF), fixed((1, 1, N)), fixed((1, 1, N)), fixed((1, NF))],
        out_shape=[jax.ShapeDtypeStruct((s, NB), F32), jax.ShapeDtypeStruct((s, NF), F32),
                   jax.ShapeDtypeStruct((1, 1, N), F32), jax.ShapeDtypeStruct((1, 1, N), F32),
                   jax.ShapeDtypeStruct((1, NF), F32)],
        scratch_shapes=[pltpu.VMEM((1, NF), F32)],
        compiler_params=_params("arbitrary"))(ub, uf, q_g, k_g, f_b, dqn, dkn, dvf, dgate, dcum_q, dcum_k)


def _att_groups(s):
    blocks = s // ATT_TILE
    per = max(1, blocks // ATT_GROUPS)
    return per, blocks // per


def _att_parts(n, width):
    return ([(0, n - width, False)] if n > width else []) + [(n - width, n, True)]


def _att_scores(q_bf, k_ref, ck_ref, lo, hi, masked, row_offset):
    scores = _bdot_nt(q_bf, k_ref[0, lo:hi, :]) - ck_ref[0, :, lo:hi]
    if masked:
        rows = row_offset + lax.broadcasted_iota(jnp.int32, scores.shape, 0)
        scores = jnp.where(rows >= lax.broadcasted_iota(jnp.int32, scores.shape, 1), scores, -1e30)
    return scores


def _fox_attn_fwd(q, k, v, cum_q, cum_k):
    s = q.shape[1]
    t = ATT_TILE
    per, groups = _att_groups(s)

    def body(q_ref, k_ref, v_ref, cq_ref, ck_ref, o_ref, lse_ref):
        qi = pl.program_id(1)
        for g in range(groups):
            @pl.when(qi // per == g)
            def _(g=g):
                q_bf = (q_ref[0] * ATT_SCALE).astype(BF16)
                parts = _att_parts((g + 1) * per * t, per * t)
                scores = [_att_scores(q_bf, k_ref, ck_ref, lo, hi, masked, (qi - g * per) * t)
                          for lo, hi, masked in parts]
                m = functools.reduce(jnp.maximum, [jnp.max(sc, axis=-1, keepdims=True) for sc in scores])
                l, acc = 0.0, 0.0
                for sc, (lo, hi, _) in zip(scores, parts):
                    p = jnp.exp(sc - m)
                    l += jnp.sum(p, axis=-1, keepdims=True)
                    acc += _bdot(p, v_ref[0, lo:hi, :])
                o_ref[0] = acc / l
                lse_ref[0] = m + jnp.log(l) + cq_ref[0]

    qb = pl.BlockSpec((1, t, N), lambda h, i: (h, i, 0))
    kb = pl.BlockSpec((1, s, N), lambda h, i: (h, 0, 0))
    return pl.pallas_call(
        body, name="fox_attn_fwd", grid=(H, s // t),
        in_specs=[qb, kb, kb, pl.BlockSpec((1, t, 1), lambda h, i: (h, i, 0)),
                  pl.BlockSpec((1, 1, s), lambda h, i: (h, 0, 0))],
        out_specs=[qb, pl.BlockSpec((1, t, 1), lambda h, i: (h, i, 0))],
        out_shape=[jax.ShapeDtypeStruct((H, s, N), F32), jax.ShapeDtypeStruct((H, s, 1), F32)],
        compiler_params=_params("arbitrary", "arbitrary"))(q, k, v, cum_q, cum_k)


def _fox_attn_bwd(q, k, v, cum_q, cum_k, o, lse, do, slabs, owners):
    s = q.shape[1]
    t = ATT_TILE
    per, groups = _att_groups(s)
    nx = len(slabs)

    def body(q_ref, k_ref, v_ref, cq_ref, ck_ref, o_ref, lse_ref, do_ref, *refs):
        src_refs, (dq_ref, dk_ref, dv_ref, dcq_ref, dck_ref) = refs[:nx], refs[nx:nx + 5]
        start, wait = _exchange_ops(src_refs, refs[nx + 5:2 * nx + 5], owners, refs[2 * nx + 5:])
        qi = pl.program_id(1)

        @pl.when((pl.program_id(0) == 0) & (qi == 0))
        def _():
            start()

        @pl.when(qi == 0)
        def _():
            for ref in (dk_ref, dv_ref, dck_ref):
                ref[...] = jnp.zeros_like(ref)

        for g in range(groups):
            @pl.when(qi // per == g)
            def _(g=g):
                q_bf, do_bf = (q_ref[0] * ATT_SCALE).astype(BF16), do_ref[0].astype(BF16)
                row_term = cq_ref[0] - lse_ref[0]
                delta = jnp.sum(do_ref[0] * o_ref[0], axis=-1, keepdims=True)
                dq, dcq = 0.0, 0.0
                for lo, hi, masked in _att_parts((g + 1) * per * t, per * t):
                    p = jnp.exp(_att_scores(q_bf, k_ref, ck_ref, lo, hi, masked, (qi - g * per) * t) + row_term)
                    ds = p * (_bdot_nt(do_bf, v_ref[0, lo:hi, :]) - delta)
                    dq += _bdot(ds, k_ref[0, lo:hi, :])
                    dcq += jnp.sum(ds, axis=-1, keepdims=True)
                    dk_ref[0, lo:hi, :] += _bdot_tn(ds, q_bf)
                    dv_ref[0, lo:hi, :] += _bdot_tn(p, do_bf)
                    dck_ref[0, :, lo:hi] -= jnp.sum(ds, axis=0, keepdims=True)
                dq_ref[0] = dq * ATT_SCALE
                dcq_ref[0] = dcq

        @pl.when((pl.program_id(0) == H - 1) & (qi == s // t - 1))
        def _():
            wait()

    qb = pl.BlockSpec((1, t, N), lambda h, i: (h, i, 0))
    kb = pl.BlockSpec((1, s, N), lambda h, i: (h, 0, 0))
    cqb = pl.BlockSpec((1, t, 1), lambda h, i: (h, i, 0))
    ckb = pl.BlockSpec((1, 1, s), lambda h, i: (h, 0, 0))
    f32 = lambda *shape: jax.ShapeDtypeStruct(shape, F32)
    out = pl.pallas_call(
        body, name="fox_attn_bwd", grid=(H, s // t),
        in_specs=[qb, kb, kb, cqb, ckb, qb, cqb, qb] + _hbm_specs(nx), out_specs=[qb, kb, kb, cqb, ckb] + _hbm_specs(nx),
        out_shape=[f32(H, s, N), f32(H, s, N), f32(H, s, N), f32(H, s, 1), f32(H, 1, s)]
        + _received_shapes(slabs, owners),
        scratch_shapes=_exchange_scratch(nx),
        compiler_params=_params("arbitrary", "arbitrary"))(q, k, v, cum_q, cum_k, o, lse, do, *slabs)
    return out[:5], out[5:]


def _head_param(p):
    return p.reshape(H, 1, N)


def _local_step(x, target, w, p):
    mu = p["shift_mu"]
    pre_params = (_head_param(mu[:, 0:DA]), _head_param(mu[:, DA:2 * DA]), _head_param(mu[:, 2 * DA:3 * DA]),
                  _head_param(mu[:, 3 * DA + 2 * RANK:]), mu[:, 3 * DA:3 * DA + RANK],
                  mu[:, 3 * DA + RANK:3 * DA + 2 * RANK],
                  w["w_lora_up"].astype(F32), _head_param(p["w0"]), w["a_lora_up"].astype(F32), _head_param(p["a0"]),
                  _head_param(p["k_k"]), _head_param(p["k_a"]))
    post_params = (_head_param(p["lnx_w"]), _head_param(p["lnx_b"]), _head_param(p["r_k"]))
    q_g, k_g = p["q_norm_g"].reshape(1, 1, N), p["k_norm_g"].reshape(1, 1, N)
    f_b = jnp.pad(p["f_bias"], ((0, 0), (0, NF - H)))
    fg = p["final_norm_g"].reshape(1, D)

    h = _rms_fwd(x, p["norm_g"])
    ua = _proj(h, w["in_a"], "proj_a")
    ub = _proj(h, w["in_b"], "proj_b")
    ug = _proj(h, w["in_g"], "proj_g")
    uf = _proj(h, w["in_f"], "proj_f")
    r, lw, cl, k2, v, av, bv, gg = _rwkv_pre_fwd(ua, pre_params)
    y, ckpt, pinv = _wkv_fwd((r, lw, cl, k2, v, av, bv))
    ya = _rwkv_post_fwd(y, r, k2, v, gg, post_params)
    qn, kn, vf, cum = _fox_pre_fwd(ub, uf, q_g, k_g, f_b)
    cum_t = cum[:, :H].T
    cum_q, cum_k = cum_t[:, :, None], cum_t[:, None, :]
    o, lse = _fox_attn_fwd(qn, kn, vf, cum_q, cum_k)

    (loss, dfg, dwo, dwoa, dwob, dx2, dya, do, dgate_b, dug) = _tail(
        x, target, ya, o, ub, ug, w["w_out_a"], w["w_out_b"], w["w_out"], fg)
    everyone = (0, N_DEV)
    (dqn, dkn, dvf, dcq, dck), (recv_woa, recv_wob, recv_wo) = _fox_attn_bwd(
        qn, kn, vf, cum_q, cum_k, o, lse, do,
        (_col_slabs(dwoa), _col_slabs(dwob), dwo.astype(BF16).reshape(N_DEV, D // N_DEV, D)), (everyone,) * 3)
    pad_f = lambda a: jnp.pad(a.T, ((0, 0), (0, NF - H)))
    dub, duf, dqg, dkg, dfb = _fox_pre_bwd(ub, uf, q_g, k_g, f_b, dqn, dkn, dvf, dgate_b,
                                           pad_f(dcq[:, :, 0]), pad_f(dck.reshape(H, -1)))
    dwt_b, dwt_g, dwt_f = (_proj_wgrad(h, du, name) for du, name in ((dub, "wgrad_b"), (dug, "wgrad_g"), (duf, "wgrad_f")))
    early = _slab_wt_grad((dwt_b, dwt_g, dwt_f), (_WT_SEGMENTS[1], _WT_SEGMENTS[2], _WT_SEGMENTS[3]), EARLY_FROM, N_DEV,
                          "slab_wt_early")
    dy, dr_p, dk_p, dv_p, dgg, dlnw, dlnb, drk, handed = _rwkv_post_bwd(y, r, k2, v, gg, post_params, dya, early,
                                                                          EARLY_FROM)
    early = _chip_sums(early, handed, EARLY_FROM, "chip_sums_early")
    (dr_s, dlw, dcl, dk_s, dv_s, dav, dbv), (recv_early,) = _wkv_bwd(
        (r, lw, cl, k2, v, av, bv), ckpt, pinv, dy, (early,), ((EARLY_FROM, N_DEV, "chips"),))
    pre_out = _rwkv_pre_bwd(ua, pre_params, (dr_s, dr_p, dlw, dcl, dk_s, dk_p, dv_s, dv_p, dav, dbv, dgg))
    dua, dpre = pre_out[0], pre_out[1:]
    dwt_a = _proj_wgrad(h, dua, "wgrad_a")

    flat = lambda a: a.reshape(1, -1)
    small = {
        "final_norm_g": dfg, "w0": dpre[7], "a0": dpre[9], "k_k": dpre[10], "k_a": dpre[11], "r_k": drk, "lnx_w": dlnw,
        "lnx_b": dlnb, "q_norm_g": dqg, "k_norm_g": dkg, "f_bias": dfb[:, :H],
        "shift_mu": jnp.concatenate([flat(dpre[0]), flat(dpre[1]), flat(dpre[2]), dpre[4], dpre[5], flat(dpre[3])], axis=1),
    }
    late = _slab_wt_grad((dwt_a, dwt_b), (_WT_SEGMENTS[0], _WT_SEGMENTS[1]), 0, EARLY_FROM, "slab_wt_late")
    late = _chip_sums(late, _pair_swap(late, 0, "pair_swap_late"), 0, "chip_sums_late")
    loras = jnp.stack([dpre[6], dpre[8]], axis=1).astype(BF16)
    dx, dng, (recv_late, recv_lora, recv_small) = _proj_xgrad(
        x, p["norm_g"], dx2, (dua, dub, dug, duf), (w["in_a"], w["in_b"], w["in_g"], w["in_f"]),
        (late, loras, _pack_small(small, loss)), ((0, EARLY_FROM, "chips"), everyone, everyone))
    return dx, dng, (recv_early, recv_late), (recv_woa, recv_wob, recv_wo, recv_lora), recv_small


def _position():
    return lax.axis_index("x"), lax.axis_index("y"), lax.axis_index("c")


def _hbm_specs(n):
    return [pl.BlockSpec(memory_space=pl.ANY)] * n


def _all_gather(blocks, name):
    n = len(blocks)

    def body(*refs):
        x_refs, out_refs = refs[:n], refs[n:2 * n]
        send_sems, recv_sems, local_sems = refs[2 * n:]
        x, y, c = _position()
        me, sibling = (x, y, c), (x, y, 1 - c)
        chips = [(1 - x, y), (x, 1 - y), (1 - x, 1 - y)]

        def copy(a, k, blk, to, own=False):
            dst = out_refs[a].at[4 * blk[0] + 2 * blk[1] + blk[2]]
            return pltpu.make_async_remote_copy(
                src_ref=x_refs[a] if own else dst, dst_ref=dst, send_sem=send_sems.at[7 * a + k],
                recv_sem=recv_sems.at[7 * a + k], device_id=to, device_id_type=MESH)

        mine = [pltpu.make_async_copy(x_refs[a], out_refs[a].at[4 * x + 2 * y + c], local_sems.at[a]) for a in range(n)]
        for cp in mine:
            cp.start()
        first = []
        for a in range(n):
            first.append(copy(a, 0, me, sibling, own=True))
            first += [copy(a, 1 + j, me, (*chip, c), own=True) for j, chip in enumerate(chips)]
        for cp in first:
            cp.start()
        passed = []
        for j, chip in enumerate(chips):
            for a in range(n):
                copy(a, 1 + j, (*chip, c), me).wait_recv()
                passed.append(copy(a, 4 + j, (*chip, c), sibling))
                passed[-1].start()
        for a in range(n):
            copy(a, 0, sibling, me).wait_recv()
        for j, chip in enumerate(chips):
            for a in range(n):
                copy(a, 4 + j, (*chip, 1 - c), me).wait_recv()
        for cp in first + passed:
            cp.wait_send()
        for cp in mine:
            cp.wait()

    return pl.pallas_call(
        body, name=name, out_shape=[jax.ShapeDtypeStruct((N_DEV,) + b.shape, b.dtype) for b in blocks],
        in_specs=_hbm_specs(n), out_specs=_hbm_specs(n),
        scratch_shapes=[pltpu.SemaphoreType.DMA((7 * n,)), pltpu.SemaphoreType.DMA((7 * n,)),
                        pltpu.SemaphoreType.DMA((n,))],
    )(*blocks)


def _received_shapes(slabs, owners):
    return [jax.ShapeDtypeStruct((N_DEV // 2 if len(o) == 3 else N_DEV,) + s.shape[1:], s.dtype)
            for s, o in zip(slabs, owners)]


def _pair_swap_scratch(n):
    return [pltpu.SemaphoreType.DMA((n,)), pltpu.SemaphoreType.DMA((n,))]


def _pair_swap_ops(s_ref, p_ref, lo, sems):
    send_sems, recv_sems = sems
    n = s_ref.shape[0]

    def run(sending):
        x, y, c = _position()
        for side in (0, 1):
            mine = [pltpu.make_async_remote_copy(src_ref=s_ref.at[i], dst_ref=p_ref.at[i], send_sem=send_sems.at[i],
                                                 recv_sem=recv_sems.at[i], device_id=(x, y, 1 - c), device_id_type=MESH)
                    for i in range(n) if (lo + i) % 2 == side]

            @pl.when(c != side)
            def _():
                for cp in mine:
                    cp.start() if sending else cp.wait_send()

            if not sending:
                @pl.when(c == side)
                def _():
                    for cp in mine:
                        cp.wait_recv()

    return functools.partial(run, True), functools.partial(run, False)


def _pair_swap(slabs, lo, name):
    n = slabs.shape[0]

    def body(s_ref, p_ref, *sems):
        start, wait = _pair_swap_ops(s_ref, p_ref, lo, sems)
        start()
        wait()

    return pl.pallas_call(
        body, name=name, out_shape=jax.ShapeDtypeStruct(slabs.shape, slabs.dtype),
        in_specs=_hbm_specs(1), out_specs=_hbm_specs(1)[0], scratch_shapes=_pair_swap_scratch(n))(slabs)


def _chip_sums(slabs, swapped, lo, name):
    n, rows, cols = slabs.shape
    tile = W_IN_COL_TILE

    def body(s_ref, p_ref, o_ref):
        c = lax.axis_index("c")
        for i in range(n):
            @pl.when(c == (lo + i) % 2)
            def _(i=i):
                o_ref[i] = (s_ref[i].astype(F32) + p_ref[i].astype(F32)).astype(BF16)

    blk = pl.BlockSpec((n, rows, tile), lambda j: (0, 0, j))
    return pl.pallas_call(
        body, name=name, grid=(cols // tile,), in_specs=[blk, blk], out_specs=blk,
        out_shape=jax.ShapeDtypeStruct(slabs.shape, BF16), compiler_params=_params("arbitrary"))(slabs, swapped)


def _exchange_scratch(n):
    return [pltpu.SemaphoreType.DMA((7 * n,)), pltpu.SemaphoreType.DMA((7 * n,)), pltpu.SemaphoreType.DMA((n,))]


def _exchange_ops(src_refs, dst_refs, owners, sems):
    send_sems, recv_sems, local_sems = sems
    n = len(src_refs)

    def guarded(a, dev, fn):
        lo, hi = owners[a][:2]
        if (lo, hi) == (0, N_DEV):
            fn()
        else:
            pl.when((dev >= lo) & (dev < hi))(fn)

    def src(a, dev):
        ref = src_refs[a]
        return ref.at[0] if ref.shape[0] == 1 else ref.at[dev - owners[a][0]]

    def run(sending, waiting):
        x, y, c = _position()
        me = 4 * x + 2 * y + c
        for a in range(n):
            by_chip = len(owners[a]) == 3
            slot = (lambda qx, qy, qc: 2 * qx + qy) if by_chip else (lambda qx, qy, qc: 4 * qx + 2 * qy + qc)
            mine = slot(x, y, c)
            local = lambda a=a, mine=mine: pltpu.make_async_copy(src(a, me), dst_refs[a].at[mine], local_sems.at[a])
            if sending:
                guarded(a, me, lambda local=local: local().start())
            for m in range(2, N_DEV, 2) if by_chip else range(1, N_DEV):
                px, py, pc = x ^ (m >> 2), y ^ ((m >> 1) & 1), c ^ (m & 1)
                peer = 4 * px + 2 * py + pc
                theirs = slot(px, py, pc)
                sem = dict(send_sem=send_sems.at[7 * a + m - 1], recv_sem=recv_sems.at[7 * a + m - 1],
                           device_id=(px, py, pc), device_id_type=MESH)
                send = lambda a=a, peer=peer, sem=sem, mine=mine: pltpu.make_async_remote_copy(
                    src_ref=src(a, peer), dst_ref=dst_refs[a].at[mine], **sem)
                recv = lambda a=a, sem=sem, theirs=theirs: pltpu.make_async_remote_copy(
                    src_ref=src(a, me), dst_ref=dst_refs[a].at[theirs], **sem)
                if sending:
                    guarded(a, peer, lambda send=send: send().start())
                if waiting:
                    guarded(a, me, lambda recv=recv: recv().wait_recv())
                    guarded(a, peer, lambda send=send: send().wait_send())
            if waiting:
                guarded(a, me, lambda local=local: local().wait())

    return functools.partial(run, True, False), functools.partial(run, False, True)


def _sum_slabs(r_ref):
    g = r_ref[0].astype(F32)
    for k in range(1, r_ref.shape[0]):
        g = g + r_ref[k].astype(F32)
    return g


def _adamw(g, w, m, v):
    m_new = ADAM_B1 * m + (1.0 - ADAM_B1) * g
    v_new = ADAM_B2 * v + (1.0 - ADAM_B2) * (g * g)
    m_hat = m_new / (1.0 - ADAM_B1 ** ADAM_STEP)
    v_hat = v_new / (1.0 - ADAM_B2 ** ADAM_STEP)
    return g, -ADAM_LR * (m_hat / (jnp.sqrt(v_hat) + ADAM_EPS) + ADAM_WD * w), m_new, v_new


def _adamw_w_in(recv_early, recv_late, w, m, v, slabs, owners):
    rows, cols = w.shape
    tile = W_IN_COL_TILE
    nx = len(slabs)

    def body(early_ref, late_ref, w_ref, m_ref, v_ref, *refs):
        src_refs, o_refs, dst_refs = refs[:nx], refs[nx:nx + 4], refs[nx + 4:2 * nx + 4]
        start, wait = _exchange_ops(src_refs, dst_refs, owners, refs[2 * nx + 4:])
        x, y, c = _position()
        early_owner = 4 * x + 2 * y + c >= EARLY_FROM

        @pl.when(pl.program_id(0) == 0)
        def _():
            start()

        def update(g):
            for o_ref, val in zip(o_refs, _adamw(g, w_ref[...], m_ref[...], v_ref[...])):
                o_ref[...] = val

        pl.when(early_owner)(lambda: update(_sum_slabs(early_ref)))
        pl.when(jnp.logical_not(early_owner))(lambda: update(_sum_slabs(late_ref)))

        @pl.when(pl.program_id(0) == cols // tile - 1)
        def _():
            wait()

    blk = pl.BlockSpec((rows, tile), lambda i: (0, i))
    slots = lambda r: pl.BlockSpec((r.shape[0], rows, tile), lambda i: (0, 0, i))
    out = pl.pallas_call(
        body, name="adamw_w_in", grid=(cols // tile,),
        in_specs=[slots(recv_early), slots(recv_late), blk, blk, blk] + _hbm_specs(nx),
        out_specs=[blk] * 4 + _hbm_specs(nx),
        out_shape=[jax.ShapeDtypeStruct((rows, cols), F32)] * 4 + _received_shapes(slabs, owners),
        scratch_shapes=_exchange_scratch(nx),
        compiler_params=_params("arbitrary"))(recv_early, recv_late, w, m, v, *slabs)
    return out[:4], out[4:]


def _adamw_misc(recvs, recv_small, recv_norm, params):
    names = list(params)
    flat = [a for n in names for a in params[n]]

    def body(woa_ref, wob_ref, wo_ref, lora_ref, small_ref, norm_ref, *refs):
        p_refs, o_refs = refs[:len(flat)], refs[len(flat):]
        g_small = _sum_slabs(small_ref)
        g_lora = _sum_slabs(lora_ref)
        grads = {"w_out_a": _sum_slabs(woa_ref), "w_out_b": _sum_slabs(wob_ref), "w_out": _sum_slabs(wo_ref),
                 "w_lora_up": g_lora[0], "a_lora_up": g_lora[1], "norm_g": _sum_slabs(norm_ref)}
        for n, (off, size) in SMALL_SLOTS.items():
            grads[n] = g_small[:, off:off + size]
        for i, n in enumerate(names):
            w_ref, m_ref, v_ref = p_refs[3 * i:3 * i + 3]
            for o_ref, val in zip(o_refs[4 * i:4 * i + 4], _adamw(grads[n], w_ref[...], m_ref[...], v_ref[...])):
                o_ref[...] = val
        o_refs[-1][...] = g_small[:, LOSS_SLOT:LOSS_SLOT + 1]

    out = pl.pallas_call(
        body, name="adamw_misc",
        out_shape=[jax.ShapeDtypeStruct(params[n][0].shape, F32) for n in names for _ in range(4)]
        + [jax.ShapeDtypeStruct((1, 1), F32)],
        compiler_params=_params())(*recvs, recv_small, recv_norm, *flat)
    return {n: out[4 * i:4 * i + 4] for i, n in enumerate(names)}, out[-1]


_WT_SEGMENTS = ((0, NA), (NA, NB), (NA + NB + H, NG), (NA + NB, H))


def _split_wt(gathered):
    tile = W_IN_COL_TILE

    def body(g_ref, *o_refs):
        full = jnp.concatenate([g_ref[j] for j in range(N_DEV)], axis=0)
        for o_ref, (row, n) in zip(o_refs, _WT_SEGMENTS):
            seg = full[row:row + n]
            if n < o_ref.shape[0]:
                seg = jnp.concatenate([seg, jnp.zeros((o_ref.shape[0] - n, tile), BF16)], axis=0)
            o_ref[...] = seg

    sizes = (NA, NB, NG, NF)
    return pl.pallas_call(
        body, name="split_wt", grid=(D // tile,),
        in_specs=[pl.BlockSpec((N_DEV, COLS_PER_DEV, tile), lambda i: (0, 0, i))],
        out_specs=[pl.BlockSpec((n, tile), lambda i: (0, i)) for n in sizes],
        out_shape=[jax.ShapeDtypeStruct((n, D), BF16) for n in sizes],
        compiler_params=_params("arbitrary"))(gathered)


def _slab_wt_grad(segments, seg_rows, dev_lo, dev_hi, name):
    tile = W_IN_COL_TILE
    k = len(segments)

    def body(*refs):
        seg_refs, o_ref = refs[:k], refs[k]
        for j in range(dev_lo, dev_hi):
            lo, hi = COLS_PER_DEV * j, COLS_PER_DEV * (j + 1)
            parts = []
            for ref, (row, n) in sorted(zip(seg_refs, seg_rows), key=lambda t: t[1][0]):
                first, last = max(lo, row), min(hi, row + n)
                if first < last:
                    parts.append(ref[first - row:last - row, :])
            o_ref[j - dev_lo] = (parts[0] if len(parts) == 1 else jnp.concatenate(parts, axis=0)).astype(BF16)

    return pl.pallas_call(
        body, name=name, grid=(D // tile,),
        in_specs=[pl.BlockSpec((s.shape[0], tile), lambda i: (0, i)) for s in segments],
        out_specs=pl.BlockSpec((dev_hi - dev_lo, COLS_PER_DEV, tile), lambda i: (0, 0, i)),
        out_shape=jax.ShapeDtypeStruct((dev_hi - dev_lo, COLS_PER_DEV, D), BF16),
        compiler_params=_params("arbitrary"))(*segments)


def _by_cols(a):
    return jnp.moveaxis(a, 0, 1).reshape(a.shape[1], -1)


def _col_slabs(a):
    return jnp.moveaxis(a.reshape(a.shape[0], N_DEV, -1), 1, 0).astype(BF16)


def _pack_small(grads, loss):
    pieces, at = [], 0
    for n, (off, size) in list(SMALL_SLOTS.items()) + [("loss", (LOSS_SLOT, 1))]:
        pieces += [jnp.zeros((off - at,), F32), (loss if n == "loss" else grads[n]).reshape(-1)]
        at = off + size
    return jnp.concatenate(pieces + [jnp.zeros((SMALL_LEN - at,), F32)]).reshape(1, 1, SMALL_LEN)


def _gather_weights(t):
    cast = lambda a: a.astype(BF16)
    loras = jnp.stack([t["w_lora_up"][0], t["a_lora_up"][0]])
    wt, woa, wob, wo, lora = _all_gather(
        [cast(t["w_in"][0].T), cast(t["w_out_a"][0]), cast(t["w_out_b"][0]), cast(t["w_out"][0]), cast(loras)],
        "weight_gather")
    in_a, in_b, in_g, in_f = _split_wt(wt)
    return {"in_a": in_a, "in_b": in_b, "in_g": in_g, "in_f": in_f, "w_out_a": _by_cols(woa), "w_out_b": _by_cols(wob),
            "w_out": wo.reshape(D, D), "w_lora_up": lora[:, 0], "a_lora_up": lora[:, 1]}


def kernel(x, norm_g, w_in, shift_mu, w_lora_up, w0, a_lora_up, a0, k_k, k_a, r_k, lnx_w, lnx_b, f_bias, q_norm_g, k_norm_g, w_out_a, w_out_b, w_out, final_norm_g, loss_target, m_norm_g, m_w_in, m_shift_mu, m_w_lora_up, m_w0, m_a_lora_up, m_a0, m_k_k, m_k_a, m_r_k, m_lnx_w, m_lnx_b, m_f_bias, m_q_norm_g, m_k_norm_g, m_w_out_a, m_w_out_b, m_w_out, m_final_norm_g, v_norm_g, v_w_in, v_shift_mu, v_w_lora_up, v_w0, v_a_lora_up, v_a0, v_k_k, v_k_a, v_r_k, v_lnx_w, v_lnx_b, v_f_bias, v_q_norm_g, v_k_norm_g, v_w_out_a, v_w_out_b, v_w_out, v_final_norm_g):
    names = ("norm_g", "w_in", "shift_mu", "w_lora_up", "w0", "a_lora_up", "a0", "k_k", "k_a", "r_k", "lnx_w", "lnx_b",
             "f_bias", "q_norm_g", "k_norm_g", "w_out_a", "w_out_b", "w_out", "final_norm_g")
    weights = dict(zip(names, (norm_g, w_in, shift_mu, w_lora_up, w0, a_lora_up, a0, k_k, k_a, r_k, lnx_w, lnx_b,
                               f_bias, q_norm_g, k_norm_g, w_out_a, w_out_b, w_out, final_norm_g)))
    m_in = dict(zip(names, (m_norm_g, m_w_in, m_shift_mu, m_w_lora_up, m_w0, m_a_lora_up, m_a0, m_k_k, m_k_a, m_r_k,
                            m_lnx_w, m_lnx_b, m_f_bias, m_q_norm_g, m_k_norm_g, m_w_out_a, m_w_out_b, m_w_out,
                            m_final_norm_g)))
    v_in = dict(zip(names, (v_norm_g, v_w_in, v_shift_mu, v_w_lora_up, v_w0, v_a_lora_up, v_a0, v_k_k, v_k_a, v_r_k,
                            v_lnx_w, v_lnx_b, v_f_bias, v_q_norm_g, v_k_norm_g, v_w_out_a, v_w_out_b, v_w_out,
                            v_final_norm_g)))

    matrices = ("w_out_a", "w_out_b", "w_out", "w_lora_up", "a_lora_up")
    as_2d = lambda n, a: a[0] if n in matrices else a.reshape(1, -1)

    full = _gather_weights(weights)
    dx, dng, recv_wt, recvs, recv_small = _local_step(
        x[0], loss_target[0], full, {n: as_2d(n, weights[n]) for n in ("norm_g",) + tuple(SMALL_SLOTS)})

    res, (recv_norm,) = _adamw_w_in(*recv_wt, w_in[0].T, m_w_in[0].T, v_w_in[0].T, (dng[None],), ((0, N_DEV),))
    outs = {"w_in": [r.T[None] for r in res]}
    misc = [n for n in names if n != "w_in"]
    res, loss_sum = _adamw_misc(recvs, recv_small, recv_norm,
                                {n: tuple(as_2d(n, t[n]) for t in (weights, m_in, v_in)) for n in misc})
    for n in misc:
        outs[n] = [r.reshape(weights[n].shape) for r in res[n]]
    return (loss_sum.reshape(()), dx[None], *[outs[n][i] for i in range(4) for n in names])
```

```python
import functools
import math

import jax
import jax.numpy as jnp
from jax import lax
from jax.experimental import pallas as pl
from jax.experimental.pallas import tpu as pltpu

F32 = jnp.float32
BF16 = jnp.bfloat16
HI = lax.Precision.HIGHEST
MESH = pl.DeviceIdType.MESH

N_DEV = 8
D = 1024
H = 8
N = 64
DA = H * N
RANK = 64
NA = 4 * DA + 2 * RANK
NB = 4 * DA
NG = 2 * D
NF = 128
IN_COLS = NA + NB + H + NG
COLS_PER_DEV = IN_COLS // N_DEV
RMS_EPS = 1e-6
LNX_EPS = 64e-5
ATT_SCALE = N ** -0.5

ADAM_LR = 0.001
ADAM_B1 = 0.9
ADAM_B2 = 0.999
ADAM_EPS = 1e-08
ADAM_WD = 0.01
ADAM_STEP = 10

LANES = 128
WKV_CHUNK = 64
TOK_TILE = 256
HEAD_TILE = 128
ATT_TILE = 256
ATT_GROUPS = 8
VMEM_LIMIT = 56 * 1024 * 1024

SMALL_SLOTS = {"final_norm_g": (0, D), "shift_mu": (D, NA), "w0": (3200, DA), "a0": (3712, DA), "k_k": (4224, DA),
               "k_a": (4736, DA), "r_k": (5248, DA), "lnx_w": (5760, DA), "lnx_b": (6272, DA), "q_norm_g": (6784, N),
               "k_norm_g": (6912, N), "f_bias": (7040, H)}
LOSS_SLOT = 7168
SMALL_LEN = 7296
W_IN_COL_TILE = 256
EARLY_FROM = -(-NA // COLS_PER_DEV)


def _params(*sem):
    return pltpu.CompilerParams(dimension_semantics=sem or None, vmem_limit_bytes=VMEM_LIMIT)


def _bdot(a, b):
    return jnp.dot(a.astype(BF16), b.astype(BF16), preferred_element_type=F32)


def _bdot_nt(a, b):
    return lax.dot_general(a.astype(BF16), b.astype(BF16), (((1,), (1,)), ((), ())), preferred_element_type=F32)


def _bdot_tn(a, b):
    return lax.dot_general(a.astype(BF16), b.astype(BF16), (((0,), (0,)), ((), ())), preferred_element_type=F32)


def _sigmoid(x):
    return 1.0 / (1.0 + jnp.exp(-x))


def _softplus(x):
    return jnp.maximum(x, 0.0) + jnp.log(1.0 + jnp.exp(-jnp.abs(x)))


def _heads(ref, col0):
    return jnp.stack([ref[:, col0 + N * h:col0 + N * (h + 1)] for h in range(H)])


def _store_heads(ref, col0, val):
    for h in range(H):
        ref[:, col0 + N * h:col0 + N * (h + 1)] = val[h]


def _lerp(c, s, mu):
    return c + (s - c) * mu


def _rwkv_pre(rc, rs, kc, ks, vc, vs, gc, gs, wdc, wds, adc, ads,
              mu_r, mu_k, mu_v, mu_g, mu_wd, mu_ad, w_up, w0, a_up, a0, k_k, k_a):
    r = _lerp(rc, rs, mu_r)
    k = _lerp(kc, ks, mu_k)
    v = _lerp(vc, vs, mu_v)
    g = _lerp(gc, gs, mu_g)
    wd = _lerp(wdc, wds, mu_wd)
    ad = _lerp(adc, ads, mu_ad)
    t = wd.shape[0]
    bdims = (((2,), (1,)), ((0,), (0,)))
    tw = jnp.broadcast_to(jnp.tanh(wd).astype(BF16)[None], (H, t, RANK))
    z = w0 + lax.dot_general(tw, w_up.astype(BF16), bdims, preferred_element_type=F32)
    w_raw = -_softplus(-z) - 0.5
    lw = -jnp.exp(w_raw)
    row = lax.broadcasted_iota(jnp.int32, (t, t), 0)
    col = lax.broadcasted_iota(jnp.int32, (t, t), 1)
    same_chunk = ((row >= col) & (row // WKV_CHUNK == col // WKV_CHUNK)).astype(F32)
    cl = jnp.einsum("hts,hsn->htn", jnp.broadcast_to(same_chunk[None], (H, t, t)), lw, precision=HI,
                    preferred_element_type=F32)
    adb = jnp.broadcast_to(ad.astype(BF16)[None], (H, t, RANK))
    alr = _sigmoid(a0 + lax.dot_general(adb, a_up.astype(BF16), bdims, preferred_element_type=F32))
    kk = k * k_k
    kk = kk / jnp.maximum(jnp.sqrt(jnp.sum(kk * kk, axis=-1, keepdims=True)), 1e-12)
    k2 = k * (1.0 + (alr - 1.0) * k_a)
    return r, lw, cl, k2, v, -kk, kk * alr, g


_MM_DIMS = {"nn": (((2,), (1,)), ((0,), (0,))), "nt": (((2,), (2,)), ((0,), (0,))), "tn": (((1,), (1,)), ((0,), (0,)))}


def _split(x):
    hi = x.astype(BF16)
    return hi, (x - hi.astype(F32)).astype(BF16)


def _dot3(a, b, kind):
    ah, al = _split(a)
    bh, bl = _split(b)
    dot = functools.partial(lax.dot_general, dimension_numbers=_MM_DIMS[kind], preferred_element_type=F32)
    return dot(ah, bh) + (dot(ah, bl) + dot(al, bh))


def _dot1(a, b, kind):
    return lax.dot_general(a.astype(BF16), b.astype(BF16), dimension_numbers=_MM_DIMS[kind], preferred_element_type=F32)


@functools.partial(jax.custom_vjp, nondiff_argnums=(2, 3))
def _mm(a, b, kind, fine=True):
    return _dot3(a, b, kind) if fine else _dot1(a, b, kind)


def _mm_fwd(a, b, kind, fine):
    return _mm(a, b, kind, fine), (a, b)


def _mm_bwd(kind, fine, res, ct):
    a, b = res
    if kind == "nn":
        return _dot1(ct, b, "nt"), _dot1(a, ct, "tn")
    if kind == "nt":
        return _dot1(ct, b, "nn"), _dot1(ct, a, "tn")
    return _dot1(b, ct, "nt"), _dot1(a, ct, "nn")


_mm.defvjp(_mm_fwd, _mm_bwd)


def _chunk_masks(c):
    row = lax.broadcasted_iota(jnp.int32, (c, c), 0)
    col = lax.broadcasted_iota(jnp.int32, (c, c), 1)
    return (row >= col)[None], (row > col)[None], (row == col).astype(F32)[None]


def _wkv_aab(fine, lw, cl, a, b):
    _, strict, _ = _chunk_masks(a.shape[1])
    return jnp.where(strict, _mm(a * jnp.exp(cl - lw), b * jnp.exp(-cl), "nt", fine), 0.0)


def _tri_inverse(x):
    c = x.shape[1]
    p = _chunk_masks(c)[2] + x
    for _ in range(int(math.log2(c)) - 1):
        x = _dot1(x, x, "nn")
        p = p + _dot1(p, x, "nn")
    return p


def _wkv_apply(fine, s0, r, lw, cl, k, v, a, b, p):
    c = r.shape[1]
    incl, strict, _ = _chunk_masks(c)
    mm = functools.partial(_mm, fine=fine)
    gi = jnp.exp(-cl)
    left = jnp.concatenate([a * jnp.exp(cl - lw), r * jnp.exp(cl)], axis=1)
    right = jnp.concatenate([b * gi, k * gi], axis=1)
    m = mm(left, right, "nt")
    z0 = mm(left, s0, "nt")
    a_ak = jnp.where(strict, m[:, :c, c:], 0.0)
    row = lax.broadcasted_iota(jnp.int32, (c, 2 * c), 0)
    col = lax.broadcasted_iota(jnp.int32, (c, 2 * c), 1)
    a_r = jnp.where((row >= col % c)[None], m[:, c:, :], 0.0)
    sa = mm(p, z0[:, :c] + mm(a_ak, v, "nn"), "nn")
    sa_v = jnp.concatenate([sa, v], axis=1)
    y = z0[:, c:] + mm(a_r, sa_v, "nn")
    s1 = (s0 + mm(sa_v, right, "tn")) * jnp.exp(cl[:, c - 1:c, :])
    return y, s1


def _rwkv_post(y, r, k2, v, g, lnx_w, lnx_b, r_k):
    mean = jnp.mean(y, axis=-1, keepdims=True)
    yc = y - mean
    var = jnp.mean(yc * yc, axis=-1, keepdims=True)
    yn = yc * lax.rsqrt(var + LNX_EPS) * lnx_w + lnx_b
    bonus = jnp.sum(r * k2 * r_k, axis=-1, keepdims=True) * v
    return (yn + bonus) * (g * _sigmoid(g))


def _fox_pre(q, k, f, q_g, k_g, f_b):
    qn = q * lax.rsqrt(jnp.mean(q * q, axis=-1, keepdims=True) + RMS_EPS) * q_g
    kn = k * lax.rsqrt(jnp.mean(k * k, axis=-1, keepdims=True) + RMS_EPS) * k_g
    x = f + f_b
    return qn, kn, jnp.minimum(x, 0.0) - jnp.log(1.0 + jnp.exp(-jnp.abs(x)))


def _rms_fwd(x, g):
    s = x.shape[0]

    def body(x_ref, g_ref, h_ref):
        xv = x_ref[...]
        h_ref[...] = (xv * lax.rsqrt(jnp.mean(xv * xv, axis=-1, keepdims=True) + RMS_EPS) * g_ref[...]).astype(BF16)

    return pl.pallas_call(
        body, name="rms_fwd", grid=(s // TOK_TILE,),
        in_specs=[pl.BlockSpec((TOK_TILE, D), lambda i: (i, 0)), pl.BlockSpec((1, D), lambda i: (0, 0))],
        out_specs=pl.BlockSpec((TOK_TILE, D), lambda i: (i, 0)),
        out_shape=jax.ShapeDtypeStruct((s, D), BF16), compiler_params=_params("arbitrary"))(x, g)


def _proj(h, wt, name):
    s, n = h.shape[0], wt.shape[0]

    def body(h_ref, w_ref, o_ref):
        o_ref[...] = _bdot_nt(h_ref[...], w_ref[...])

    return pl.pallas_call(
        body, name=name, grid=(s // TOK_TILE,),
        in_specs=[pl.BlockSpec((TOK_TILE, D), lambda i: (i, 0)), pl.BlockSpec((n, D), lambda i: (0, 0))],
        out_specs=pl.BlockSpec((TOK_TILE, n), lambda i: (i, 0)),
        out_shape=jax.ShapeDtypeStruct((s, n), F32), compiler_params=_params("arbitrary"))(h, wt)


def _proj_wgrad(h, du, name):
    s, n = du.shape

    def body(h_ref, du_ref, o_ref):
        @pl.when(pl.program_id(0) == 0)
        def _():
            o_ref[...] = jnp.zeros_like(o_ref)

        o_ref[...] += _bdot_tn(du_ref[...], h_ref[...])

    return pl.pallas_call(
        body, name=name, grid=(s // TOK_TILE,),
        in_specs=[pl.BlockSpec((TOK_TILE, D), lambda i: (i, 0)), pl.BlockSpec((TOK_TILE, n), lambda i: (i, 0))],
        out_specs=pl.BlockSpec((n, D), lambda i: (0, 0)),
        out_shape=jax.ShapeDtypeStruct((n, D), F32), compiler_params=_params("arbitrary"))(h, du)


def _proj_xgrad(x, g, dx2, dus, ws, slabs, owners):
    s = x.shape[0]
    tile = HEAD_TILE
    k = len(dus)
    nx = len(slabs)
    n_in = 3 + 2 * k + nx

    def body(*refs):
        x_ref, g_ref, dx2_ref = refs[:3]
        du_refs, w_refs = refs[3:3 + k], refs[3 + k:3 + 2 * k]
        src_refs = refs[3 + 2 * k:3 + 2 * k + nx]
        dx_ref, dg_ref = refs[n_in:n_in + 2]
        dst_refs = refs[n_in + 2:n_in + 2 + nx]
        start, wait = _exchange_ops(src_refs, dst_refs, owners, refs[n_in + 2 + nx:])

        @pl.when(pl.program_id(0) == 0)
        def _():
            dg_ref[...] = jnp.zeros_like(dg_ref)
            start()

        dh = _bdot(du_refs[0][...], w_refs[0][...])
        for du_ref, w_ref in zip(du_refs[1:], w_refs[1:]):
            dh += _bdot(du_ref[...], w_ref[...])
        xv = x_ref[...]
        rs = lax.rsqrt(jnp.mean(xv * xv, axis=-1, keepdims=True) + RMS_EPS)
        xn = xv * rs
        dg_ref[...] += jnp.sum(dh * xn, axis=0, keepdims=True)
        dxn = dh * g_ref[...]
        dx_ref[...] = rs * (dxn - xn * jnp.mean(dxn * xn, axis=-1, keepdims=True)) + dx2_ref[...]

        @pl.when(pl.program_id(0) == s // tile - 1)
        def _():
            wait()

    tok = lambda n: pl.BlockSpec((tile, n), lambda i: (i, 0))
    fixed = lambda a: pl.BlockSpec(a.shape, lambda i: (0,) * a.ndim)
    out = pl.pallas_call(
        body, name="proj_xgrad", grid=(s // tile,),
        in_specs=([tok(D), fixed(g), tok(D)] + [tok(du.shape[1]) for du in dus] + [fixed(w) for w in ws]
                  + _hbm_specs(nx)),
        out_specs=[tok(D), pl.BlockSpec((1, D), lambda i: (0, 0))] + _hbm_specs(nx),
        out_shape=[jax.ShapeDtypeStruct((s, D), F32), jax.ShapeDtypeStruct((1, D), F32)] + _received_shapes(slabs, owners),
        scratch_shapes=_exchange_scratch(nx),
        compiler_params=_params("arbitrary"))(x, g, dx2, *dus, *ws, *slabs)
    return out[0], out[1], out[2:]


def _tail(x, target, ya, o, ub, ug, w_oa, w_ob, w_o, fg):
    s = x.shape[0]
    tile = TOK_TILE

    def body(x_ref, t_ref, ya_ref, o_ref, gb_ref, ug_ref, woa_ref, wob_ref, wo_ref, fg_ref,
             loss_ref, dfg_ref, dwo_ref, dwoa_ref, dwob_ref, dx2_ref, dya_ref, do_ref, dgb_ref, dug_ref):
        @pl.when(pl.program_id(0) == 0)
        def _():
            for r in (loss_ref, dfg_ref, dwo_ref, dwoa_ref, dwob_ref):
                r[...] = jnp.zeros_like(r)

        ya_v = ya_ref[...]
        gate_b = gb_ref[...]
        sg_b = _sigmoid(gate_b)
        silu_b = gate_b * sg_b
        o_v = jnp.concatenate([o_ref[h] for h in range(H)], axis=-1)
        yb_v = o_v * silu_b
        big_a = _bdot(ya_v, woa_ref[...])
        big_b = _bdot(yb_v, wob_ref[...])
        sa = _sigmoid(ug_ref[:, :D])
        sb = _sigmoid(ug_ref[:, D:])
        merged = sa * big_a + sb * big_b
        x2 = x_ref[...] + _bdot(merged, wo_ref[...])
        rs = lax.rsqrt(jnp.mean(x2 * x2, axis=-1, keepdims=True) + RMS_EPS)
        xn = x2 * rs
        err = xn * fg_ref[...] - t_ref[...]
        loss_ref[...] += (0.5 / D) * jnp.sum(err * err)
        dout = err * (1.0 / D)
        dfg_ref[...] += jnp.sum(dout * xn, axis=0, keepdims=True)
        dxn = dout * fg_ref[...]
        dx2 = rs * (dxn - xn * jnp.mean(dxn * xn, axis=-1, keepdims=True))
        dx2_ref[...] = dx2
        dwo_ref[...] += _bdot_tn(merged, dx2)
        dmerged = _bdot_nt(dx2, wo_ref[...])
        dbig_a = dmerged * sa
        dbig_b = dmerged * sb
        dug_ref[:, :D] = dmerged * big_a * sa * (1.0 - sa)
        dug_ref[:, D:] = dmerged * big_b * sb * (1.0 - sb)
        dwoa_ref[...] += _bdot_tn(ya_v, dbig_a)
        dwob_ref[...] += _bdot_tn(yb_v, dbig_b)
        dya_ref[...] = _bdot_nt(dbig_a, woa_ref[...])
        dyb = _bdot_nt(dbig_b, wob_ref[...])
        dgb_ref[...] = dyb * o_v * (sg_b * (1.0 + gate_b * (1.0 - sg_b)))
        _dov = dyb * silu_b
        for h in range(H):
            do_ref[h] = _dov[:, N * h:N * (h + 1)]

    tok = lambda n: pl.BlockSpec((tile, n), lambda i: (i, 0))
    hm = pl.BlockSpec((H, tile, N), lambda i: (0, i, 0))
    fixed = lambda shape: pl.BlockSpec(shape, lambda i: (0,) * len(shape))
    f32 = lambda *shape: jax.ShapeDtypeStruct(shape, F32)
    return pl.pallas_call(
        body, name="tail", grid=(s // tile,),
        in_specs=[tok(D), tok(D), tok(DA), hm, pl.BlockSpec((tile, DA), lambda i: (i, 3)), tok(NG),
                  fixed((DA, D)), fixed((DA, D)), fixed((D, D)), fixed((1, D))],
        out_specs=[fixed((1, 1)), fixed((1, D)), fixed((D, D)), fixed((DA, D)), fixed((DA, D)),
                   tok(D), tok(DA), hm, tok(DA), tok(NG)],
        out_shape=[f32(1, 1), f32(1, D), f32(D, D), f32(DA, D), f32(DA, D),
                   f32(s, D), f32(s, DA), f32(H, s, N), f32(s, DA), f32(s, NG)],
        compiler_params=_params("arbitrary"))(x, target, ya, o, ub, ug, w_oa, w_ob, w_o, fg)


_PRE_PARAM_SHAPES = ((H, 1, N),) * 4 + ((1, RANK),) * 2 + ((H, RANK, N), (H, 1, N), (H, RANK, N), (H, 1, N), (H, 1, N),
                                                              (H, 1, N))


def _pre_operands(ua_ref, prev_ref, first):
    cur = ua_ref[...]
    t = cur.shape[0]
    prev_row = jnp.where(first, 0.0, prev_ref[7:8, :])
    rows = lax.broadcasted_iota(jnp.int32, cur.shape, 0)
    sh = jnp.where(rows == 0, prev_row, pltpu.roll(cur, 1, axis=0))
    ops = []
    for c0 in (0, DA, 2 * DA, 3 * DA + 2 * RANK):
        ops.append(jnp.stack([cur[:, c0 + N * h:c0 + N * (h + 1)] for h in range(H)]))
        ops.append(jnp.stack([sh[:, c0 + N * h:c0 + N * (h + 1)] for h in range(H)]))
    for c0 in (3 * DA, 3 * DA + RANK):
        ops.append(cur[:, c0:c0 + RANK])
        ops.append(sh[:, c0:c0 + RANK])
    del t
    return ops


def _ua_specs(tile, order):
    blocks = tile // 8
    return [pl.BlockSpec((tile, NA), lambda i: (order(i), 0)),
            pl.BlockSpec((8, NA), lambda i: (jnp.maximum(order(i) * blocks - 1, 0), 0))]


def _rwkv_pre_fwd(ua, pre_params):
    s = ua.shape[0]
    tile = HEAD_TILE

    def body(ua_ref, prev_ref, *refs):
        p_refs, o_refs = refs[:len(pre_params)], refs[len(pre_params):]
        ops = _pre_operands(ua_ref, prev_ref, pl.program_id(0) == 0)
        outs = _rwkv_pre(*ops, *[p[...] for p in p_refs])
        for o_ref, val in zip(o_refs, outs):
            o_ref[...] = val

    hm = pl.BlockSpec((H, tile, N), lambda i: (0, i, 0))
    return pl.pallas_call(
        body, name="rwkv_pre_fwd", grid=(s // tile,),
        in_specs=_ua_specs(tile, lambda i: i) + [pl.BlockSpec(p.shape, lambda i, nd=p.ndim: (0,) * nd) for p in pre_params],
        out_specs=[hm] * 8, out_shape=[jax.ShapeDtypeStruct((H, s, N), F32)] * 8,
        compiler_params=_params("arbitrary"))(ua, ua, *pre_params)


def _rwkv_pre_bwd(ua, pre_params, cots):
    s = ua.shape[0]
    tile = HEAD_TILE
    nt = s // tile
    n_p = len(pre_params)

    def body(ua_ref, prev_ref, *refs):
        p_refs, c_refs = refs[:n_p], refs[n_p:n_p + 11]
        dua_ref = refs[n_p + 11]
        dp_refs = refs[n_p + 12:n_p + 12 + n_p]
        carry_ref = refs[-1]
        i = pl.program_id(0)

        @pl.when(i == 0)
        def _():
            carry_ref[...] = jnp.zeros_like(carry_ref)
            for r in dp_refs:
                r[...] = jnp.zeros_like(r)

        ops = _pre_operands(ua_ref, prev_ref, i == nt - 1)
        _, vjp = jax.vjp(_rwkv_pre, *ops, *[p[...] for p in p_refs])
        c = [r[...] for r in c_refs]
        grads = vjp((c[0] + c[1], c[2], c[3], c[4] + c[5], c[6] + c[7], c[8], c[9], c[10]))
        d_ops, d_par = grads[:12], grads[12:]
        for r, val in zip(dp_refs, d_par):
            r[...] += val
        d_cur = jnp.concatenate([d_ops[0][h] for h in range(H)] + [d_ops[2][h] for h in range(H)]
                                + [d_ops[4][h] for h in range(H)] + [d_ops[8], d_ops[10]]
                                + [d_ops[6][h] for h in range(H)], axis=-1)
        d_sh = jnp.concatenate([d_ops[1][h] for h in range(H)] + [d_ops[3][h] for h in range(H)]
                               + [d_ops[5][h] for h in range(H)] + [d_ops[9], d_ops[11]]
                               + [d_ops[7][h] for h in range(H)], axis=-1)
        rows = lax.broadcasted_iota(jnp.int32, d_sh.shape, 0)
        dua_ref[...] = d_cur + jnp.where(rows == tile - 1, carry_ref[...], pltpu.roll(d_sh, tile - 1, axis=0))
        carry_ref[...] = d_sh[0:1, :]

    rev = lambda i: nt - 1 - i
    hm = pl.BlockSpec((H, tile, N), lambda i: (0, rev(i), 0))
    fixed = [pl.BlockSpec(p.shape, lambda i, nd=p.ndim: (0,) * nd) for p in pre_params]
    return pl.pallas_call(
        body, name="rwkv_pre_bwd", grid=(nt,),
        in_specs=_ua_specs(tile, rev) + fixed + [hm] * 11,
        out_specs=[pl.BlockSpec((tile, NA), lambda i: (rev(i), 0))] + fixed,
        out_shape=[jax.ShapeDtypeStruct((s, NA), F32)] + [jax.ShapeDtypeStruct(p.shape, F32) for p in pre_params],
        scratch_shapes=[pltpu.VMEM((1, NA), F32)],
        compiler_params=_params("arbitrary"))(ua, ua, *pre_params, *cots)


def _wkv_fwd(seq):
    s = seq[0].shape[1]
    nc = s // WKV_CHUNK

    def body(r_ref, lw_ref, cl_ref, k_ref, v_ref, a_ref, b_ref, y_ref, ck_ref, p_ref, state):
        @pl.when(pl.program_id(0) == 0)
        def _():
            state[...] = jnp.zeros_like(state)

        s0 = state[...]
        ck_ref[0] = s0
        p = _tri_inverse(_wkv_aab(True, lw_ref[...], cl_ref[...], a_ref[...], b_ref[...]))
        p_ref[0] = p
        y, s1 = _wkv_apply(True, s0, r_ref[...], lw_ref[...], cl_ref[...], k_ref[...], v_ref[...], a_ref[...],
                           b_ref[...], p)
        y_ref[...] = y
        state[...] = s1

    hm = pl.BlockSpec((H, WKV_CHUNK, N), lambda c: (0, c, 0))
    per_chunk = lambda m: pl.BlockSpec((1, H, m, m), lambda c: (c, 0, 0, 0))
    return pl.pallas_call(
        body, name="wkv_fwd", grid=(nc,), in_specs=[hm] * 7,
        out_specs=[hm, per_chunk(N), per_chunk(WKV_CHUNK)],
        out_shape=[jax.ShapeDtypeStruct((H, s, N), F32), jax.ShapeDtypeStruct((nc, H, N, N), F32),
                   jax.ShapeDtypeStruct((nc, H, WKV_CHUNK, WKV_CHUNK), F32)],
        scratch_shapes=[pltpu.VMEM((H, N, N), F32)], compiler_params=_params("arbitrary"))(*seq)


def _wkv_bwd(seq, ckpt, pinv, dy, slabs, owners):
    s = seq[0].shape[1]
    nc = s // WKV_CHUNK
    nx = len(slabs)

    def body(r_ref, lw_ref, cl_ref, k_ref, v_ref, a_ref, b_ref, ck_ref, p_ref, dy_ref, *refs):
        src_refs, d_refs, dst_refs = refs[:nx], refs[nx:nx + 7], refs[nx + 7:2 * nx + 7]
        dstate = refs[2 * nx + 7]
        start, wait = _exchange_ops(src_refs, dst_refs, owners, refs[2 * nx + 8:])

        @pl.when(pl.program_id(0) == 0)
        def _():
            dstate[...] = jnp.zeros_like(dstate)
            start()

        p = p_ref[0]
        lw, cl, a, b = lw_ref[...], cl_ref[...], a_ref[...], b_ref[...]
        _, vjp = jax.vjp(functools.partial(_wkv_apply, False), ck_ref[0], r_ref[...], lw, cl, k_ref[...], v_ref[...],
                         a, b, p)
        ds0, dr, dlw, dcl, dk, dv, da, db, dp = vjp((dy_ref[...], dstate[...]))
        dstate[...] = ds0
        _, vjp_x = jax.vjp(functools.partial(_wkv_aab, False), lw, cl, a, b)
        dlw2, dcl2, da2, db2 = vjp_x(_dot1(_dot1(p, dp, "tn"), p, "nt"))
        for d_ref, val in zip(d_refs, (dr, dlw + dlw2, dcl + dcl2, dk, dv, da + da2, db + db2)):
            d_ref[...] = val

        @pl.when(pl.program_id(0) == nc - 1)
        def _():
            wait()

    hm = pl.BlockSpec((H, WKV_CHUNK, N), lambda c: (0, nc - 1 - c, 0))
    per_chunk = lambda m: pl.BlockSpec((1, H, m, m), lambda c: (nc - 1 - c, 0, 0, 0))
    out = pl.pallas_call(
        body, name="wkv_bwd", grid=(nc,),
        in_specs=[hm] * 7 + [per_chunk(N), per_chunk(WKV_CHUNK), hm] + _hbm_specs(nx),
        out_specs=[hm] * 7 + _hbm_specs(nx),
        out_shape=[jax.ShapeDtypeStruct((H, s, N), F32)] * 7 + _received_shapes(slabs, owners),
        scratch_shapes=[pltpu.VMEM((H, N, N), F32)] + _exchange_scratch(nx),
        compiler_params=_params("arbitrary"))(*seq, ckpt, pinv, dy, *slabs)
    return out[:7], out[7:]


def _rwkv_post_fwd(y, r, k2, v, g, post_params):
    s = y.shape[1]
    tile = HEAD_TILE

    def body(y_ref, r_ref, k_ref, v_ref, g_ref, w_ref, b_ref, rk_ref, o_ref):
        out = _rwkv_post(y_ref[...], r_ref[...], k_ref[...], v_ref[...], g_ref[...], w_ref[...], b_ref[...],
                         rk_ref[...])
        o_ref[...] = jnp.concatenate([out[h] for h in range(H)], axis=-1)

    hm = pl.BlockSpec((H, tile, N), lambda i: (0, i, 0))
    par = pl.BlockSpec((H, 1, N), lambda i: (0, 0, 0))
    return pl.pallas_call(
        body, name="rwkv_post_fwd", grid=(s // tile,), in_specs=[hm] * 5 + [par] * 3,
        out_specs=pl.BlockSpec((tile, DA), lambda i: (i, 0)), out_shape=jax.ShapeDtypeStruct((s, DA), F32),
        compiler_params=_params("arbitrary"))(y, r, k2, v, g, *post_params)


def _rwkv_post_bwd(y, r, k2, v, g, post_params, dya, slabs, lo):
    s = y.shape[1]
    tile = HEAD_TILE

    def body(y_ref, r_ref, k_ref, v_ref, g_ref, w_ref, b_ref, rk_ref, dya_ref, s_ref, *refs):
        d_refs, p_ref = refs[:8], refs[8]
        start, wait = _pair_swap_ops(s_ref, p_ref, lo, refs[9:])

        @pl.when(pl.program_id(0) == 0)
        def _():
            for ref in d_refs[5:]:
                ref[...] = jnp.zeros_like(ref)
            start()

        _, vjp = jax.vjp(_rwkv_post, y_ref[...], r_ref[...], k_ref[...], v_ref[...], g_ref[...], w_ref[...],
                         b_ref[...], rk_ref[...])
        grads = vjp(jnp.stack([dya_ref[:, N * h:N * (h + 1)] for h in range(H)]))
        for ref, val in zip(d_refs[:5], grads[:5]):
            ref[...] = val
        for ref, val in zip(d_refs[5:], grads[5:]):
            ref[...] += val

        @pl.when(pl.program_id(0) == s // tile - 1)
        def _():
            wait()

    hm = pl.BlockSpec((H, tile, N), lambda i: (0, i, 0))
    par = pl.BlockSpec((H, 1, N), lambda i: (0, 0, 0))
    return pl.pallas_call(
        body, name="rwkv_post_bwd", grid=(s // tile,),
        in_specs=[hm] * 5 + [par] * 3 + [pl.BlockSpec((tile, DA), lambda i: (i, 0))] + _hbm_specs(1),
        out_specs=[hm] * 5 + [par] * 3 + _hbm_specs(1),
        out_shape=[jax.ShapeDtypeStruct((H, s, N), F32)] * 5 + [jax.ShapeDtypeStruct((H, 1, N), F32)] * 3
        + [jax.ShapeDtypeStruct(slabs.shape, slabs.dtype)],
        scratch_shapes=_pair_swap_scratch(slabs.shape[0]),
        compiler_params=_params("arbitrary"))(y, r, k2, v, g, *post_params, dya, slabs)


def _tri(t):
    return (lax.broadcasted_iota(jnp.int32, (t, t), 0) >= lax.broadcasted_iota(jnp.int32, (t, t), 1)).astype(F32)


def _fox_pre_fwd(ub, uf, q_g, k_g, f_b):
    s = ub.shape[0]
    tile = HEAD_TILE

    def body(ub_ref, uf_ref, qg_ref, kg_ref, fb_ref, q_ref, k_ref, v_ref, cum_ref, carry):
        @pl.when(pl.program_id(0) == 0)
        def _():
            carry[...] = jnp.zeros_like(carry)

        qn, kn, logf = _fox_pre(_heads(ub_ref, 0), _heads(ub_ref, DA), uf_ref[...], qg_ref[...], kg_ref[...],
                                fb_ref[...])
        q_ref[...] = qn
        k_ref[...] = kn
        v_ref[...] = _heads(ub_ref, 2 * DA)
        cum = jnp.dot(_tri(tile), logf, precision=HI, preferred_element_type=F32) + carry[...]
        cum_ref[...] = cum
        carry[...] = cum[tile - 1:tile, :]

    hm = pl.BlockSpec((H, tile, N), lambda i: (0, i, 0))
    fixed = lambda shape: pl.BlockSpec(shape, lambda i: (0,) * len(shape))
    return pl.pallas_call(
        body, name="fox_pre_fwd", grid=(s // tile,),
        in_specs=[pl.BlockSpec((tile, NB), lambda i: (i, 0)), pl.BlockSpec((tile, NF), lambda i: (i, 0)),
                  fixed((1, 1, N)), fixed((1, 1, N)), fixed((1, NF))],
        out_specs=[hm] * 3 + [pl.BlockSpec((tile, NF), lambda i: (i, 0))],
        out_shape=[jax.ShapeDtypeStruct((H, s, N), F32)] * 3 + [jax.ShapeDtypeStruct((s, NF), F32)],
        scratch_shapes=[pltpu.VMEM((1, NF), F32)], compiler_params=_params("arbitrary"))(ub, uf, q_g, k_g, f_b)


def _fox_pre_bwd(ub, uf, q_g, k_g, f_b, dqn, dkn, dvf, dgate, dcum_q, dcum_k):
    s = ub.shape[0]
    tile = HEAD_TILE
    nt = s // tile

    def body(ub_ref, uf_ref, qg_ref, kg_ref, fb_ref, dq_ref, dk_ref, dv_ref, dgate_ref, dcq_ref, dck_ref,
             dub_ref, duf_ref, dqg_ref, dkg_ref, dfb_ref, carry):
        @pl.when(pl.program_id(0) == 0)
        def _():
            carry[...] = jnp.zeros_like(carry)
            for ref in (dqg_ref, dkg_ref, dfb_ref):
                ref[...] = jnp.zeros_like(ref)

        dcum = dcq_ref[...] + dck_ref[...]
        dlogf = lax.dot_general(_tri(tile), dcum, (((0,), (0,)), ((), ())), precision=HI,
                                preferred_element_type=F32) + carry[...]
        carry[...] = dlogf[0:1, :]
        _, vjp = jax.vjp(_fox_pre, _heads(ub_ref, 0), _heads(ub_ref, DA), uf_ref[...], qg_ref[...], kg_ref[...],
                         fb_ref[...])
        d_q, d_k, d_f, d_qg, d_kg, d_fb = vjp((dq_ref[...], dk_ref[...], dlogf))
        _store_heads(dub_ref, 0, d_q)
        _store_heads(dub_ref, DA, d_k)
        _store_heads(dub_ref, 2 * DA, dv_ref[...])
        dub_ref[:, 3 * DA:] = dgate_ref[...]
        duf_ref[...] = d_f
        dqg_ref[...] += d_qg
        dkg_ref[...] += d_kg
        dfb_ref[...] += d_fb

    rev = lambda i: nt - 1 - i
    hm = pl.BlockSpec((H, tile, N), lambda i: (0, rev(i), 0))
    tok = lambda n: pl.BlockSpec((tile, n), lambda i: (rev(i), 0))
    fixed = lambda shape: pl.BlockSpec(shape, lambda i: (0,) * len(shape))
    return pl.pallas_call(
        body, name="fox_pre_bwd", grid=(nt,),
        in_specs=[tok(NB), tok(NF), fixed((1, 1, N)), fixed((1, 1, N)), fixed((1, NF)), hm, hm, hm, tok(DA), tok(NF),
                  tok(NF)],
        out_specs=[tok(NB), tok(NF), fixed((1, 1, N)), fixed((1, 1, N)), fixed((1, NF))],
        out_shape=[jax.ShapeDtypeStruct((s, NB), F32), jax.ShapeDtypeStruct((s, NF), F32),
                   jax.ShapeDtypeStruct((1, 1, N), F32), jax.ShapeDtypeStruct((1, 1, N), F32),
                   jax.ShapeDtypeStruct((1, NF), F32)],
        scratch_shapes=[pltpu.VMEM((1, NF), F32)],
        compiler_params=_params("arbitrary"))(ub, uf, q_g, k_g, f_b, dqn, dkn, dvf, dgate, dcum_q, dcum_k)


def _att_groups(s):
    blocks = s // ATT_TILE
    per = max(1, blocks // ATT_GROUPS)
    return per, blocks // per


def _att_parts(n, width):
    return ([(0, n - width, False)] if n > width else []) + [(n - width, n, True)]


def _att_scores(q_bf, k_ref, ck_ref, lo, hi, masked, row_offset):
    scores = _bdot_nt(q_bf, k_ref[0, lo:hi, :]) - ck_ref[0, :, lo:hi]
    if masked:
        rows = row_offset + lax.broadcasted_iota(jnp.int32, scores.shape, 0)
        scores = jnp.where(rows >= lax.broadcasted_iota(jnp.int32, scores.shape, 1), scores, -1e30)
    return scores


def _fox_attn_fwd(q, k, v, cum_q, cum_k):
    s = q.shape[1]
    t = ATT_TILE
    per, groups = _att_groups(s)

    def body(q_ref, k_ref, v_ref, cq_ref, ck_ref, o_ref, lse_ref):
        qi = pl.program_id(1)
        for g in range(groups):
            @pl.when(qi // per == g)
            def _(g=g):
                q_bf = (q_ref[0] * ATT_SCALE).astype(BF16)
                parts = _att_parts((g + 1) * per * t, per * t)
                scores = [_att_scores(q_bf, k_ref, ck_ref, lo, hi, masked, (qi - g * per) * t)
                          for lo, hi, masked in parts]
                m = functools.reduce(jnp.maximum, [jnp.max(sc, axis=-1, keepdims=True) for sc in scores])
                l, acc = 0.0, 0.0
                for sc, (lo, hi, _) in zip(scores, parts):
                    p = jnp.exp(sc - m)
                    l += jnp.sum(p, axis=-1, keepdims=True)
                    acc += _bdot(p, v_ref[0, lo:hi, :])
                o_ref[0] = acc / l
                lse_ref[0] = m + jnp.log(l) + cq_ref[0]

    qb = pl.BlockSpec((1, t, N), lambda h, i: (h, i, 0))
    kb = pl.BlockSpec((1, s, N), lambda h, i: (h, 0, 0))
    return pl.pallas_call(
        body, name="fox_attn_fwd", grid=(H, s // t),
        in_specs=[qb, kb, kb, pl.BlockSpec((1, t, 1), lambda h, i: (h, i, 0)),
                  pl.BlockSpec((1, 1, s), lambda h, i: (h, 0, 0))],
        out_specs=[qb, pl.BlockSpec((1, t, 1), lambda h, i: (h, i, 0))],
        out_shape=[jax.ShapeDtypeStruct((H, s, N), F32), jax.ShapeDtypeStruct((H, s, 1), F32)],
        compiler_params=_params("arbitrary", "arbitrary"))(q, k, v, cum_q, cum_k)


def _fox_attn_bwd(q, k, v, cum_q, cum_k, o, lse, do, slabs, owners):
    s = q.shape[1]
    t = ATT_TILE
    per, groups = _att_groups(s)
    nx = len(slabs)

    def body(q_ref, k_ref, v_ref, cq_ref, ck_ref, o_ref, lse_ref, do_ref, *refs):
        src_refs, (dq_ref, dk_ref, dv_ref, dcq_ref, dck_ref) = refs[:nx], refs[nx:nx + 5]
        start, wait = _exchange_ops(src_refs, refs[nx + 5:2 * nx + 5], owners, refs[2 * nx + 5:])
        qi = pl.program_id(1)

        @pl.when((pl.program_id(0) == 0) & (qi == 0))
        def _():
            start()

        @pl.when(qi == 0)
        def _():
            for ref in (dk_ref, dv_ref, dck_ref):
                ref[...] = jnp.zeros_like(ref)

        for g in range(groups):
            @pl.when(qi // per == g)
            def _(g=g):
                q_bf, do_bf = (q_ref[0] * ATT_SCALE).astype(BF16), do_ref[0].astype(BF16)
                row_term = cq_ref[0] - lse_ref[0]
                delta = jnp.sum(do_ref[0] * o_ref[0], axis=-1, keepdims=True)
                dq, dcq = 0.0, 0.0
                for lo, hi, masked in _att_parts((g + 1) * per * t, per * t):
                    p = jnp.exp(_att_scores(q_bf, k_ref, ck_ref, lo, hi, masked, (qi - g * per) * t) + row_term)
                    ds = p * (_bdot_nt(do_bf, v_ref[0, lo:hi, :]) - delta)
                    dq += _bdot(ds, k_ref[0, lo:hi, :])
                    dcq += jnp.sum(ds, axis=-1, keepdims=True)
                    dk_ref[0, lo:hi, :] += _bdot_tn(ds, q_bf)
                    dv_ref[0, lo:hi, :] += _bdot_tn(p, do_bf)
                    dck_ref[0, :, lo:hi] -= jnp.sum(ds, axis=0, keepdims=True)
                dq_ref[0] = dq * ATT_SCALE
                dcq_ref[0] = dcq

        @pl.when((pl.program_id(0) == H - 1) & (qi == s // t - 1))
        def _():
            wait()

    qb = pl.BlockSpec((1, t, N), lambda h, i: (h, i, 0))
    kb = pl.BlockSpec((1, s, N), lambda h, i: (h, 0, 0))
    cqb = pl.BlockSpec((1, t, 1), lambda h, i: (h, i, 0))
    ckb = pl.BlockSpec((1, 1, s), lambda h, i: (h, 0, 0))
    f32 = lambda *shape: jax.ShapeDtypeStruct(shape, F32)
    out = pl.pallas_call(
        body, name="fox_attn_bwd", grid=(H, s // t),
        in_specs=[qb, kb, kb, cqb, ckb, qb, cqb, qb] + _hbm_specs(nx), out_specs=[qb, kb, kb, cqb, ckb] + _hbm_specs(nx),
        out_shape=[f32(H, s, N), f32(H, s, N), f32(H, s, N), f32(H, s, 1), f32(H, 1, s)]
        + _received_shapes(slabs, owners),
        scratch_shapes=_exchange_scratch(nx),
        compiler_params=_params("arbitrary", "arbitrary"))(q, k, v, cum_q, cum_k, o, lse, do, *slabs)
    return out[:5], out[5:]


def _head_param(p):
    return p.reshape(H, 1, N)


def _local_step(x, target, w, p):
    mu = p["shift_mu"]
    pre_params = (_head_param(mu[:, 0:DA]), _head_param(mu[:, DA:2 * DA]), _head_param(mu[:, 2 * DA:3 * DA]),
                  _head_param(mu[:, 3 * DA + 2 * RANK:]), mu[:, 3 * DA:3 * DA + RANK],
                  mu[:, 3 * DA + RANK:3 * DA + 2 * RANK],
                  w["w_lora_up"].astype(F32), _head_param(p["w0"]), w["a_lora_up"].astype(F32), _head_param(p["a0"]),
                  _head_param(p["k_k"]), _head_param(p["k_a"]))
    post_params = (_head_param(p["lnx_w"]), _head_param(p["lnx_b"]), _head_param(p["r_k"]))
    q_g, k_g = p["q_norm_g"].reshape(1, 1, N), p["k_norm_g"].reshape(1, 1, N)
    f_b = jnp.pad(p["f_bias"], ((0, 0), (0, NF - H)))
    fg = p["final_norm_g"].reshape(1, D)

    h = _rms_fwd(x, p["norm_g"])
    ua = _proj(h, w["in_a"], "proj_a")
    ub = _proj(h, w["in_b"], "proj_b")
    ug = _proj(h, w["in_g"], "proj_g")
    uf = _proj(h, w["in_f"], "proj_f")
    r, lw, cl, k2, v, av, bv, gg = _rwkv_pre_fwd(ua, pre_params)
    y, ckpt, pinv = _wkv_fwd((r, lw, cl, k2, v, av, bv))
    ya = _rwkv_post_fwd(y, r, k2, v, gg, post_params)
    qn, kn, vf, cum = _fox_pre_fwd(ub, uf, q_g, k_g, f_b)
    cum_t = cum[:, :H].T
    cum_q, cum_k = cum_t[:, :, None], cum_t[:, None, :]
    o, lse = _fox_attn_fwd(qn, kn, vf, cum_q, cum_k)

    (loss, dfg, dwo, dwoa, dwob, dx2, dya, do, dgate_b, dug) = _tail(
        x, target, ya, o, ub, ug, w["w_out_a"], w["w_out_b"], w["w_out"], fg)
    everyone = (0, N_DEV)
    (dqn, dkn, dvf, dcq, dck), (recv_woa, recv_wob, recv_wo) = _fox_attn_bwd(
        qn, kn, vf, cum_q, cum_k, o, lse, do,
        (_col_slabs(dwoa), _col_slabs(dwob), dwo.astype(BF16).reshape(N_DEV, D // N_DEV, D)), (everyone,) * 3)
    pad_f = lambda a: jnp.pad(a.T, ((0, 0), (0, NF - H)))
    dub, duf, dqg, dkg, dfb = _fox_pre_bwd(ub, uf, q_g, k_g, f_b, dqn, dkn, dvf, dgate_b,
                                           pad_f(dcq[:, :, 0]), pad_f(dck.reshape(H, -1)))
    dwt_b, dwt_g, dwt_f = (_proj_wgrad(h, du, name) for du, name in ((dub, "wgrad_b"), (dug, "wgrad_g"), (duf, "wgrad_f")))
    early = _slab_wt_grad((dwt_b, dwt_g, dwt_f), (_WT_SEGMENTS[1], _WT_SEGMENTS[2], _WT_SEGMENTS[3]), EARLY_FROM, N_DEV,
                          "slab_wt_early")
    dy, dr_p, dk_p, dv_p, dgg, dlnw, dlnb, drk, handed = _rwkv_post_bwd(y, r, k2, v, gg, post_params, dya, early,
                                                                          EARLY_FROM)
    early = _chip_sums(early, handed, EARLY_FROM, "chip_sums_early")
    (dr_s, dlw, dcl, dk_s, dv_s, dav, dbv), (recv_early,) = _wkv_bwd(
        (r, lw, cl, k2, v, av, bv), ckpt, pinv, dy, (early,), ((EARLY_FROM, N_DEV, "chips"),))
    pre_out = _rwkv_pre_bwd(ua, pre_params, (dr_s, dr_p, dlw, dcl, dk_s, dk_p, dv_s, dv_p, dav, dbv, dgg))
    dua, dpre = pre_out[0], pre_out[1:]
    dwt_a = _proj_wgrad(h, dua, "wgrad_a")

    flat = lambda a: a.reshape(1, -1)
    small = {
        "final_norm_g": dfg, "w0": dpre[7], "a0": dpre[9], "k_k": dpre[10], "k_a": dpre[11], "r_k": drk, "lnx_w": dlnw,
        "lnx_b": dlnb, "q_norm_g": dqg, "k_norm_g": dkg, "f_bias": dfb[:, :H],
        "shift_mu": jnp.concatenate([flat(dpre[0]), flat(dpre[1]), flat(dpre[2]), dpre[4], dpre[5], flat(dpre[3])], axis=1),
    }
    late = _slab_wt_grad((dwt_a, dwt_b), (_WT_SEGMENTS[0], _WT_SEGMENTS[1]), 0, EARLY_FROM, "slab_wt_late")
    late = _chip_sums(late, _pair_swap(late, 0, "pair_swap_late"), 0, "chip_sums_late")
    loras = jnp.stack([dpre[6], dpre[8]], axis=1).astype(BF16)
    dx, dng, (recv_late, recv_lora, recv_small) = _proj_xgrad(
        x, p["norm_g"], dx2, (dua, dub, dug, duf), (w["in_a"], w["in_b"], w["in_g"], w["in_f"]),
        (late, loras, _pack_small(small, loss)), ((0, EARLY_FROM, "chips"), everyone, everyone))
    return dx, dng, (recv_early, recv_late), (recv_woa, recv_wob, recv_wo, recv_lora), recv_small


def _position():
    return lax.axis_index("x"), lax.axis_index("y"), lax.axis_index("c")


def _hbm_specs(n):
    return [pl.BlockSpec(memory_space=pl.ANY)] * n


BIG_GATHER_COPIES = 13


def _all_gather(big, blocks, name):
    n = len(blocks)

    def body(*refs):
        big_ref, x_refs = refs[0], refs[1:1 + n]
        big_out, out_refs = refs[1 + n], refs[2 + n:2 + 2 * n]
        send_sems, recv_sems, local_sems = refs[2 + 2 * n:]
        x, y, c = _position()
        me, sibling = (x, y, c), (x, y, 1 - c)
        chips = [(1 - x, y), (x, 1 - y), (1 - x, 1 - y)]
        x_nbr, y_nbr, diag = chips
        cols = big_ref.shape[1] // 2

        def part(ref, h):
            return ref if h is None else ref.at[:, pl.ds(h * cols, cols)]

        def landed(chip, core, h):
            return part(big_out.at[4 * chip[0] + 2 * chip[1] + core], h)

        def big_copy(k, src, dst, to):
            return pltpu.make_async_remote_copy(src_ref=src, dst_ref=dst, send_sem=send_sems.at[7 * n + k],
                                                recv_sem=recv_sems.at[7 * n + k], device_id=to, device_id_type=MESH)

        def arrival(k, chip, core, h):
            dst = landed(chip, core, h)
            return big_copy(k, dst, dst, me)

        def pass_on(k, chip, h, to):
            src = landed(chip, c, h)
            return big_copy(k, src, src, to)

        big_mine = pltpu.make_async_copy(big_ref, landed((x, y), c, None), local_sems.at[n])
        big_mine.start()
        here = (x, y)
        big_sent = [big_copy(0, big_ref, landed(here, c, None), sibling),
                    big_copy(1, part(big_ref, 0), landed(here, c, 0), (*x_nbr, c)),
                    big_copy(2, part(big_ref, 1), landed(here, c, 1), (*y_nbr, c)),
                    big_copy(3, part(big_ref, 1), landed(here, c, 1), (*x_nbr, c)),
                    big_copy(4, part(big_ref, 0), landed(here, c, 0), (*y_nbr, c))]
        for cp in big_sent:
            cp.start()

        def copy(a, k, blk, to, own=False):
            dst = out_refs[a].at[4 * blk[0] + 2 * blk[1] + blk[2]]
            return pltpu.make_async_remote_copy(
                src_ref=x_refs[a] if own else dst, dst_ref=dst, send_sem=send_sems.at[7 * a + k],
                recv_sem=recv_sems.at[7 * a + k], device_id=to, device_id_type=MESH)

        mine = [pltpu.make_async_copy(x_refs[a], out_refs[a].at[4 * x + 2 * y + c], local_sems.at[a]) for a in range(n)]
        for cp in mine:
            cp.start()
        first = []
        for a in range(n):
            first.append(copy(a, 0, me, sibling, own=True))
            first += [copy(a, 1 + j, me, (*chip, c), own=True) for j, chip in enumerate(chips)]
        for cp in first:
            cp.start()

        big_steps = [(1, x_nbr, 0, (*y_nbr, c), 5, 7), (2, y_nbr, 1, (*x_nbr, c), 6, 8), (3, x_nbr, 1, None, None, 9),
                     (4, y_nbr, 0, None, None, 10), (5, diag, 0, None, None, 11), (6, diag, 1, None, None, 12)]
        for k, chip, h, onward, k_onward, k_sibling in big_steps:
            arrival(k, chip, c, h).wait_recv()
            if onward is not None:
                big_sent.append(pass_on(k_onward, chip, h, onward))
                big_sent[-1].start()
            big_sent.append(pass_on(k_sibling, chip, h, sibling))
            big_sent[-1].start()

        passed = []
        for j, chip in enumerate(chips):
            for a in range(n):
                copy(a, 1 + j, (*chip, c), me).wait_recv()
                passed.append(copy(a, 4 + j, (*chip, c), sibling))
                passed[-1].start()
        for a in range(n):
            copy(a, 0, sibling, me).wait_recv()
        for j, chip in enumerate(chips):
            for a in range(n):
                copy(a, 4 + j, (*chip, 1 - c), me).wait_recv()
        arrival(0, here, 1 - c, None).wait_recv()
        for k, chip, h, _, _, k_sibling in big_steps:
            arrival(k_sibling, chip, 1 - c, h).wait_recv()
        for cp in first + passed + big_sent:
            cp.wait_send()
        for cp in mine + [big_mine]:
            cp.wait()

    everything = [big] + list(blocks)
    return pl.pallas_call(
        body, name=name, out_shape=[jax.ShapeDtypeStruct((N_DEV,) + b.shape, b.dtype) for b in everything],
        in_specs=_hbm_specs(n + 1), out_specs=_hbm_specs(n + 1),
        scratch_shapes=[pltpu.SemaphoreType.DMA((7 * n + BIG_GATHER_COPIES,)),
                        pltpu.SemaphoreType.DMA((7 * n + BIG_GATHER_COPIES,)), pltpu.SemaphoreType.DMA((n + 1,))],
    )(*everything)


def _received_shapes(slabs, owners):
    return [jax.ShapeDtypeStruct((N_DEV // 2 if len(o) == 3 else N_DEV,) + s.shape[1:], s.dtype)
            for s, o in zip(slabs, owners)]


def _pair_swap_scratch(n):
    return [pltpu.SemaphoreType.DMA((n,)), pltpu.SemaphoreType.DMA((n,))]


def _pair_swap_ops(s_ref, p_ref, lo, sems):
    send_sems, recv_sems = sems
    n = s_ref.shape[0]

    def run(sending):
        x, y, c = _position()
        for side in (0, 1):
            mine = [pltpu.make_async_remote_copy(src_ref=s_ref.at[i], dst_ref=p_ref.at[i], send_sem=send_sems.at[i],
                                                 recv_sem=recv_sems.at[i], device_id=(x, y, 1 - c), device_id_type=MESH)
                    for i in range(n) if (lo + i) % 2 == side]

            @pl.when(c != side)
            def _():
                for cp in mine:
                    cp.start() if sending else cp.wait_send()

            if not sending:
                @pl.when(c == side)
                def _():
                    for cp in mine:
                        cp.wait_recv()

    return functools.partial(run, True), functools.partial(run, False)


def _pair_swap(slabs, lo, name):
    n = slabs.shape[0]

    def body(s_ref, p_ref, *sems):
        start, wait = _pair_swap_ops(s_ref, p_ref, lo, sems)
        start()
        wait()

    return pl.pallas_call(
        body, name=name, out_shape=jax.ShapeDtypeStruct(slabs.shape, slabs.dtype),
        in_specs=_hbm_specs(1), out_specs=_hbm_specs(1)[0], scratch_shapes=_pair_swap_scratch(n))(slabs)


def _chip_sums(slabs, swapped, lo, name):
    n, rows, cols = slabs.shape
    tile = W_IN_COL_TILE

    def body(s_ref, p_ref, o_ref):
        c = lax.axis_index("c")
        for i in range(n):
            @pl.when(c == (lo + i) % 2)
            def _(i=i):
                o_ref[i] = (s_ref[i].astype(F32) + p_ref[i].astype(F32)).astype(BF16)

    blk = pl.BlockSpec((n, rows, tile), lambda j: (0, 0, j))
    return pl.pallas_call(
        body, name=name, grid=(cols // tile,), in_specs=[blk, blk], out_specs=blk,
        out_shape=jax.ShapeDtypeStruct(slabs.shape, BF16), compiler_params=_params("arbitrary"))(slabs, swapped)


def _exchange_scratch(n):
    return [pltpu.SemaphoreType.DMA((7 * n,)), pltpu.SemaphoreType.DMA((7 * n,)), pltpu.SemaphoreType.DMA((n,))]


def _exchange_ops(src_refs, dst_refs, owners, sems):
    send_sems, recv_sems, local_sems = sems
    n = len(src_refs)

    def guarded(a, dev, fn):
        lo, hi = owners[a][:2]
        if (lo, hi) == (0, N_DEV):
            fn()
        else:
            pl.when((dev >= lo) & (dev < hi))(fn)

    def src(a, dev):
        ref = src_refs[a]
        return ref.at[0] if ref.shape[0] == 1 else ref.at[dev - owners[a][0]]

    def run(sending, waiting):
        x, y, c = _position()
        me = 4 * x + 2 * y + c
        for a in range(n):
            by_chip = len(owners[a]) == 3
            slot = (lambda qx, qy, qc: 2 * qx + qy) if by_chip else (lambda qx, qy, qc: 4 * qx + 2 * qy + qc)
            mine = slot(x, y, c)
            local = lambda a=a, mine=mine: pltpu.make_async_copy(src(a, me), dst_refs[a].at[mine], local_sems.at[a])
            if sending:
                guarded(a, me, lambda local=local: local().start())
            for m in range(2, N_DEV, 2) if by_chip else range(1, N_DEV):
                px, py, pc = x ^ (m >> 2), y ^ ((m >> 1) & 1), c ^ (m & 1)
                peer = 4 * px + 2 * py + pc
                theirs = slot(px, py, pc)
                sem = dict(send_sem=send_sems.at[7 * a + m - 1], recv_sem=recv_sems.at[7 * a + m - 1],
                           device_id=(px, py, pc), device_id_type=MESH)
                send = lambda a=a, peer=peer, sem=sem, mine=mine: pltpu.make_async_remote_copy(
                    src_ref=src(a, peer), dst_ref=dst_refs[a].at[mine], **sem)
                recv = lambda a=a, sem=sem, theirs=theirs: pltpu.make_async_remote_copy(
                    src_ref=src(a, me), dst_ref=dst_refs[a].at[theirs], **sem)
                if sending:
                    guarded(a, peer, lambda send=send: send().start())
                if waiting:
                    guarded(a, me, lambda recv=recv: recv().wait_recv())
                    guarded(a, peer, lambda send=send: send().wait_send())
            if waiting:
                guarded(a, me, lambda local=local: local().wait())

    return functools.partial(run, True, False), functools.partial(run, False, True)


def _sum_slabs(r_ref):
    g = r_ref[0].astype(F32)
    for k in range(1, r_ref.shape[0]):
        g = g + r_ref[k].astype(F32)
    return g


def _adamw(g, w, m, v):
    m_new = ADAM_B1 * m + (1.0 - ADAM_B1) * g
    v_new = ADAM_B2 * v + (1.0 - ADAM_B2) * (g * g)
    m_hat = m_new / (1.0 - ADAM_B1 ** ADAM_STEP)
    v_hat = v_new / (1.0 - ADAM_B2 ** ADAM_STEP)
    return g, -ADAM_LR * (m_hat / (jnp.sqrt(v_hat) + ADAM_EPS) + ADAM_WD * w), m_new, v_new


def _adamw_w_in(recv_early, recv_late, w, m, v, slabs, owners):
    rows, cols = w.shape
    tile = W_IN_COL_TILE
    nx = len(slabs)

    def body(early_ref, late_ref, w_ref, m_ref, v_ref, *refs):
        src_refs, o_refs, dst_refs = refs[:nx], refs[nx:nx + 4], refs[nx + 4:2 * nx + 4]
        start, wait = _exchange_ops(src_refs, dst_refs, owners, refs[2 * nx + 4:])
        x, y, c = _position()
        early_owner = 4 * x + 2 * y + c >= EARLY_FROM

        @pl.when(pl.program_id(0) == 0)
        def _():
            start()

        def update(g):
            for o_ref, val in zip(o_refs, _adamw(g, w_ref[...], m_ref[...], v_ref[...])):
                o_ref[...] = val

        pl.when(early_owner)(lambda: update(_sum_slabs(early_ref)))
        pl.when(jnp.logical_not(early_owner))(lambda: update(_sum_slabs(late_ref)))

        @pl.when(pl.program_id(0) == cols // tile - 1)
        def _():
            wait()

    blk = pl.BlockSpec((rows, tile), lambda i: (0, i))
    slots = lambda r: pl.BlockSpec((r.shape[0], rows, tile), lambda i: (0, 0, i))
    out = pl.pallas_call(
        body, name="adamw_w_in", grid=(cols // tile,),
        in_specs=[slots(recv_early), slots(recv_late), blk, blk, blk] + _hbm_specs(nx),
        out_specs=[blk] * 4 + _hbm_specs(nx),
        out_shape=[jax.ShapeDtypeStruct((rows, cols), F32)] * 4 + _received_shapes(slabs, owners),
        scratch_shapes=_exchange_scratch(nx),
        compiler_params=_params("arbitrary"))(recv_early, recv_late, w, m, v, *slabs)
    return out[:4], out[4:]


def _adamw_misc(recvs, recv_small, recv_norm, params):
    names = list(params)
    flat = [a for n in names for a in params[n]]

    def body(woa_ref, wob_ref, wo_ref, lora_ref, small_ref, norm_ref, *refs):
        p_refs, o_refs = refs[:len(flat)], refs[len(flat):]
        g_small = _sum_slabs(small_ref)
        g_lora = _sum_slabs(lora_ref)
        grads = {"w_out_a": _sum_slabs(woa_ref), "w_out_b": _sum_slabs(wob_ref), "w_out": _sum_slabs(wo_ref),
                 "w_lora_up": g_lora[0], "a_lora_up": g_lora[1], "norm_g": _sum_slabs(norm_ref)}
        for n, (off, size) in SMALL_SLOTS.items():
            grads[n] = g_small[:, off:off + size]
        for i, n in enumerate(names):
            w_ref, m_ref, v_ref = p_refs[3 * i:3 * i + 3]
            for o_ref, val in zip(o_refs[4 * i:4 * i + 4], _adamw(grads[n], w_ref[...], m_ref[...], v_ref[...])):
                o_ref[...] = val
        o_refs[-1][...] = g_small[:, LOSS_SLOT:LOSS_SLOT + 1]

    out = pl.pallas_call(
        body, name="adamw_misc",
        out_shape=[jax.ShapeDtypeStruct(params[n][0].shape, F32) for n in names for _ in range(4)]
        + [jax.ShapeDtypeStruct((1, 1), F32)],
        compiler_params=_params())(*recvs, recv_small, recv_norm, *flat)
    return {n: out[4 * i:4 * i + 4] for i, n in enumerate(names)}, out[-1]


_WT_SEGMENTS = ((0, NA), (NA, NB), (NA + NB + H, NG), (NA + NB, H))


def _split_wt(gathered):
    tile = W_IN_COL_TILE

    def body(g_ref, *o_refs):
        full = jnp.concatenate([g_ref[j] for j in range(N_DEV)], axis=0)
        for o_ref, (row, n) in zip(o_refs, _WT_SEGMENTS):
            seg = full[row:row + n]
            if n < o_ref.shape[0]:
                seg = jnp.concatenate([seg, jnp.zeros((o_ref.shape[0] - n, tile), BF16)], axis=0)
            o_ref[...] = seg

    sizes = (NA, NB, NG, NF)
    return pl.pallas_call(
        body, name="split_wt", grid=(D // tile,),
        in_specs=[pl.BlockSpec((N_DEV, COLS_PER_DEV, tile), lambda i: (0, 0, i))],
        out_specs=[pl.BlockSpec((n, tile), lambda i: (0, i)) for n in sizes],
        out_shape=[jax.ShapeDtypeStruct((n, D), BF16) for n in sizes],
        compiler_params=_params("arbitrary"))(gathered)


def _slab_wt_grad(segments, seg_rows, dev_lo, dev_hi, name):
    tile = W_IN_COL_TILE
    k = len(segments)

    def body(*refs):
        seg_refs, o_ref = refs[:k], refs[k]
        for j in range(dev_lo, dev_hi):
            lo, hi = COLS_PER_DEV * j, COLS_PER_DEV * (j + 1)
            parts = []
            for ref, (row, n) in sorted(zip(seg_refs, seg_rows), key=lambda t: t[1][0]):
                first, last = max(lo, row), min(hi, row + n)
                if first < last:
                    parts.append(ref[first - row:last - row, :])
            o_ref[j - dev_lo] = (parts[0] if len(parts) == 1 else jnp.concatenate(parts, axis=0)).astype(BF16)

    return pl.pallas_call(
        body, name=name, grid=(D // tile,),
        in_specs=[pl.BlockSpec((s.shape[0], tile), lambda i: (0, i)) for s in segments],
        out_specs=pl.BlockSpec((dev_hi - dev_lo, COLS_PER_DEV, tile), lambda i: (0, 0, i)),
        out_shape=jax.ShapeDtypeStruct((dev_hi - dev_lo, COLS_PER_DEV, D), BF16),
        compiler_params=_params("arbitrary"))(*segments)


def _by_cols(a):
    return jnp.moveaxis(a, 0, 1).reshape(a.shape[1], -1)


def _col_slabs(a):
    return jnp.moveaxis(a.reshape(a.shape[0], N_DEV, -1), 1, 0).astype(BF16)


def _pack_small(grads, loss):
    pieces, at = [], 0
    for n, (off, size) in list(SMALL_SLOTS.items()) + [("loss", (LOSS_SLOT, 1))]:
        pieces += [jnp.zeros((off - at,), F32), (loss if n == "loss" else grads[n]).reshape(-1)]
        at = off + size
    return jnp.concatenate(pieces + [jnp.zeros((SMALL_LEN - at,), F32)]).reshape(1, 1, SMALL_LEN)


def _gather_weights(t):
    cast = lambda a: a.astype(BF16)
    loras = jnp.stack([t["w_lora_up"][0], t["a_lora_up"][0]])
    wt, woa, wob, wo, lora = _all_gather(
        cast(t["w_in"][0].T), [cast(t["w_out_a"][0]), cast(t["w_out_b"][0]), cast(t["w_out"][0]), cast(loras)],
        "weight_gather")
    in_a, in_b, in_g, in_f = _split_wt(wt)
    return {"in_a": in_a, "in_b": in_b, "in_g": in_g, "in_f": in_f, "w_out_a": _by_cols(woa), "w_out_b": _by_cols(wob),
            "w_out": wo.reshape(D, D), "w_lora_up": lora[:, 0], "a_lora_up": lora[:, 1]}


def kernel(x, norm_g, w_in, shift_mu, w_lora_up, w0, a_lora_up, a0, k_k, k_a, r_k, lnx_w, lnx_b, f_bias, q_norm_g, k_norm_g, w_out_a, w_out_b, w_out, final_norm_g, loss_target, m_norm_g, m_w_in, m_shift_mu, m_w_lora_up, m_w0, m_a_lora_up, m_a0, m_k_k, m_k_a, m_r_k, m_lnx_w, m_lnx_b, m_f_bias, m_q_norm_g, m_k_norm_g, m_w_out_a, m_w_out_b, m_w_out, m_final_norm_g, v_norm_g, v_w_in, v_shift_mu, v_w_lora_up, v_w0, v_a_lora_up, v_a0, v_k_k, v_k_a, v_r_k, v_lnx_w, v_lnx_b, v_f_bias, v_q_norm_g, v_k_norm_g, v_w_out_a, v_w_out_b, v_w_out, v_final_norm_g):
    names = ("norm_g", "w_in", "shift_mu", "w_lora_up", "w0", "a_lora_up", "a0", "k_k", "k_a", "r_k", "lnx_w", "lnx_b",
             "f_bias", "q_norm_g", "k_norm_g", "w_out_a", "w_out_b", "w_out", "final_norm_g")
    weights = dict(zip(names, (norm_g, w_in, shift_mu, w_lora_up, w0, a_lora_up, a0, k_k, k_a, r_k, lnx_w, lnx_b,
                               f_bias, q_norm_g, k_norm_g, w_out_a, w_out_b, w_out, final_norm_g)))
    m_in = dict(zip(names, (m_norm_g, m_w_in, m_shift_mu, m_w_lora_up, m_w0, m_a_lora_up, m_a0, m_k_k, m_k_a, m_r_k,
                            m_lnx_w, m_lnx_b, m_f_bias, m_q_norm_g, m_k_norm_g, m_w_out_a, m_w_out_b, m_w_out,
                            m_final_norm_g)))
    v_in = dict(zip(names, (v_norm_g, v_w_in, v_shift_mu, v_w_lora_up, v_w0, v_a_lora_up, v_a0, v_k_k, v_k_a, v_r_k,
                            v_lnx_w, v_lnx_b, v_f_bias, v_q_norm_g, v_k_norm_g, v_w_out_a, v_w_out_b, v_w_out,
                            v_final_norm_g)))

    matrices = ("w_out_a", "w_out_b", "w_out", "w_lora_up", "a_lora_up")
    as_2d = lambda n, a: a[0] if n in matrices else a.reshape(1, -1)

    full = _gather_weights(weights)
    dx, dng, recv_wt, recvs, recv_small = _local_step(
        x[0], loss_target[0], full, {n: as_2d(n, weights[n]) for n in ("norm_g",) + tuple(SMALL_SLOTS)})

    res, (recv_norm,) = _adamw_w_in(*recv_wt, w_in[0].T, m_w_in[0].T, v_w_in[0].T, (dng[None],), ((0, N_DEV),))
    outs = {"w_in": [r.T[None] for r in res]}
    misc = [n for n in names if n != "w_in"]
    res, loss_sum = _adamw_misc(recvs, recv_small, recv_norm,
                                {n: tuple(as_2d(n, t[n]) for t in (weights, m_in, v_in)) for n in misc})
    for n in misc:
        outs[n] = [r.reshape(weights[n].shape) for r in res[n]]
    return (loss_sum.reshape(()), dx[None], *[outs[n][i] for i in range(4) for n in names])
```

```python
import functools
import math

import jax
import jax.numpy as jnp
from jax import lax
from jax.experimental import pallas as pl
from jax.experimental.pallas import tpu as pltpu

F32 = jnp.float32
BF16 = jnp.bfloat16
HI = lax.Precision.HIGHEST
MESH = pl.DeviceIdType.MESH

N_DEV = 8
D = 1024
H = 8
N = 64
DA = H * N
RANK = 64
NA = 4 * DA + 2 * RANK
NB = 4 * DA
NG = 2 * D
NF = 128
IN_COLS = NA + NB + H + NG
COLS_PER_DEV = IN_COLS // N_DEV
RMS_EPS = 1e-6
LNX_EPS = 64e-5
ATT_SCALE = N ** -0.5

ADAM_LR = 0.001
ADAM_B1 = 0.9
ADAM_B2 = 0.999
ADAM_EPS = 1e-08
ADAM_WD = 0.01
ADAM_STEP = 10

LANES = 128
WKV_CHUNK = 64
TOK_TILE = 256
HEAD_TILE = 128
ATT_TILE = 256
ATT_GROUPS = 8
VMEM_LIMIT = 56 * 1024 * 1024

SMALL_SLOTS = {"final_norm_g": (0, D), "shift_mu": (D, NA), "w0": (3200, DA), "a0": (3712, DA), "k_k": (4224, DA),
               "k_a": (4736, DA), "r_k": (5248, DA), "lnx_w": (5760, DA), "lnx_b": (6272, DA), "q_norm_g": (6784, N),
               "k_norm_g": (6912, N), "f_bias": (7040, H)}
LOSS_SLOT = 7168
SMALL_LEN = 7296
W_IN_COL_TILE = 256
EARLY_FROM = -(-NA // COLS_PER_DEV)


def _params(*sem):
    return pltpu.CompilerParams(dimension_semantics=sem or None, vmem_limit_bytes=VMEM_LIMIT)


def _bdot(a, b):
    return jnp.dot(a.astype(BF16), b.astype(BF16), preferred_element_type=F32)


def _bdot_nt(a, b):
    return lax.dot_general(a.astype(BF16), b.astype(BF16), (((1,), (1,)), ((), ())), preferred_element_type=F32)


def _bdot_tn(a, b):
    return lax.dot_general(a.astype(BF16), b.astype(BF16), (((0,), (0,)), ((), ())), preferred_element_type=F32)


def _sigmoid(x):
    return 1.0 / (1.0 + jnp.exp(-x))


def _softplus(x):
    return jnp.maximum(x, 0.0) + jnp.log(1.0 + jnp.exp(-jnp.abs(x)))


def _heads(ref, col0):
    return jnp.stack([ref[:, col0 + N * h:col0 + N * (h + 1)] for h in range(H)])


def _store_heads(ref, col0, val):
    for h in range(H):
        ref[:, col0 + N * h:col0 + N * (h + 1)] = val[h]


def _lerp(c, s, mu):
    return c + (s - c) * mu


def _head_sums(x):
    low = lax.broadcasted_iota(jnp.int32, (x.shape[0], LANES), 1) < N
    out = []
    for p in range(x.shape[1] // LANES):
        pair = x[:, LANES * p:LANES * (p + 1)]
        first = jnp.sum(jnp.where(low, pair, 0.0), axis=-1, keepdims=True)
        second = jnp.sum(jnp.where(low, 0.0, pair), axis=-1, keepdims=True)
        out.append(jnp.where(low, first, second))
    return jnp.concatenate(out, axis=-1)


def _to_heads(x):
    return [x[:, N * h:N * (h + 1)] for h in range(H)]


def _from_heads(ref):
    return jnp.concatenate([ref[h] for h in range(H)], axis=-1)


def _rwkv_pre(rc, rs, kc, ks, vc, vs, gc, gs, wdc, wds, adc, ads,
              mu_r, mu_k, mu_v, mu_g, mu_wd, mu_ad, w_up, w0, a_up, a0, k_k, k_a):
    r = _lerp(rc, rs, mu_r)
    k = _lerp(kc, ks, mu_k)
    v = _lerp(vc, vs, mu_v)
    g = _lerp(gc, gs, mu_g)
    wd = _lerp(wdc, wds, mu_wd)
    ad = _lerp(adc, ads, mu_ad)
    t = wd.shape[0]
    w_raw = -_softplus(-(w0 + _bdot(jnp.tanh(wd), w_up))) - 0.5
    lw = -jnp.exp(w_raw)
    row = lax.broadcasted_iota(jnp.int32, (t, t), 0)
    col = lax.broadcasted_iota(jnp.int32, (t, t), 1)
    same_chunk = ((row >= col) & (row // WKV_CHUNK == col // WKV_CHUNK)).astype(F32)
    cl = jnp.dot(same_chunk, lw, precision=HI, preferred_element_type=F32)
    alr = _sigmoid(a0 + _bdot(ad, a_up))
    kk = k * k_k
    kk = kk / jnp.maximum(jnp.sqrt(_head_sums(kk * kk)), 1e-12)
    k2 = k * (1.0 + (alr - 1.0) * k_a)
    return r, lw, cl, k2, v, -kk, kk * alr, g


_MM_DIMS = {"nn": (((2,), (1,)), ((0,), (0,))), "nt": (((2,), (2,)), ((0,), (0,))), "tn": (((1,), (1,)), ((0,), (0,)))}


def _split(x):
    hi = x.astype(BF16)
    return hi, (x - hi.astype(F32)).astype(BF16)


def _dot3(a, b, kind):
    ah, al = _split(a)
    bh, bl = _split(b)
    dot = functools.partial(lax.dot_general, dimension_numbers=_MM_DIMS[kind], preferred_element_type=F32)
    return dot(ah, bh) + (dot(ah, bl) + dot(al, bh))


def _dot1(a, b, kind):
    return lax.dot_general(a.astype(BF16), b.astype(BF16), dimension_numbers=_MM_DIMS[kind], preferred_element_type=F32)


@functools.partial(jax.custom_vjp, nondiff_argnums=(2, 3))
def _mm(a, b, kind, fine=True):
    return _dot3(a, b, kind) if fine else _dot1(a, b, kind)


def _mm_fwd(a, b, kind, fine):
    return _mm(a, b, kind, fine), (a, b)


def _mm_bwd(kind, fine, res, ct):
    a, b = res
    if kind == "nn":
        return _dot1(ct, b, "nt"), _dot1(a, ct, "tn")
    if kind == "nt":
        return _dot1(ct, b, "nn"), _dot1(ct, a, "tn")
    return _dot1(b, ct, "nt"), _dot1(a, ct, "nn")


_mm.defvjp(_mm_fwd, _mm_bwd)


def _chunk_masks(c):
    row = lax.broadcasted_iota(jnp.int32, (c, c), 0)
    col = lax.broadcasted_iota(jnp.int32, (c, c), 1)
    return (row >= col)[None], (row > col)[None], (row == col).astype(F32)[None]


def _wkv_aab(fine, lw, cl, a, b):
    _, strict, _ = _chunk_masks(a.shape[1])
    return jnp.where(strict, _mm(a * jnp.exp(cl - lw), b * jnp.exp(-cl), "nt", fine), 0.0)


def _tri_inverse(x):
    c = x.shape[1]
    p = _chunk_masks(c)[2] + x
    for _ in range(int(math.log2(c)) - 1):
        x = _dot1(x, x, "nn")
        p = p + _dot1(p, x, "nn")
    return p


def _wkv_apply(fine, s0, r, lw, cl, k, v, a, b, p):
    c = r.shape[1]
    incl, strict, _ = _chunk_masks(c)
    mm = functools.partial(_mm, fine=fine)
    gi = jnp.exp(-cl)
    left = jnp.concatenate([a * jnp.exp(cl - lw), r * jnp.exp(cl)], axis=1)
    right = jnp.concatenate([b * gi, k * gi], axis=1)
    m = mm(left, right, "nt")
    z0 = mm(left, s0, "nt")
    a_ak = jnp.where(strict, m[:, :c, c:], 0.0)
    row = lax.broadcasted_iota(jnp.int32, (c, 2 * c), 0)
    col = lax.broadcasted_iota(jnp.int32, (c, 2 * c), 1)
    a_r = jnp.where((row >= col % c)[None], m[:, c:, :], 0.0)
    sa = mm(p, z0[:, :c] + mm(a_ak, v, "nn"), "nn")
    sa_v = jnp.concatenate([sa, v], axis=1)
    y = z0[:, c:] + mm(a_r, sa_v, "nn")
    s1 = (s0 + mm(sa_v, right, "tn")) * jnp.exp(cl[:, c - 1:c, :])
    return y, s1


def _rwkv_post(y, r, k2, v, g, lnx_w, lnx_b, r_k):
    yc = y - _head_sums(y) * (1.0 / N)
    var = _head_sums(yc * yc) * (1.0 / N)
    yn = yc * lax.rsqrt(var + LNX_EPS) * lnx_w + lnx_b
    bonus = _head_sums(r * k2 * r_k) * v
    return (yn + bonus) * (g * _sigmoid(g))


def _fox_pre(q, k, f, q_g, k_g, f_b):
    qn = q * lax.rsqrt(_head_sums(q * q) * (1.0 / N) + RMS_EPS) * q_g
    kn = k * lax.rsqrt(_head_sums(k * k) * (1.0 / N) + RMS_EPS) * k_g
    x = f + f_b
    return qn, kn, jnp.minimum(x, 0.0) - jnp.log(1.0 + jnp.exp(-jnp.abs(x)))


def _rms_fwd(x, g):
    s = x.shape[0]

    def body(x_ref, g_ref, h_ref):
        xv = x_ref[...]
        h_ref[...] = (xv * lax.rsqrt(jnp.mean(xv * xv, axis=-1, keepdims=True) + RMS_EPS) * g_ref[...]).astype(BF16)

    return pl.pallas_call(
        body, name="rms_fwd", grid=(s // TOK_TILE,),
        in_specs=[pl.BlockSpec((TOK_TILE, D), lambda i: (i, 0)), pl.BlockSpec((1, D), lambda i: (0, 0))],
        out_specs=pl.BlockSpec((TOK_TILE, D), lambda i: (i, 0)),
        out_shape=jax.ShapeDtypeStruct((s, D), BF16), compiler_params=_params("arbitrary"))(x, g)


def _proj(h, wt, name):
    s, n = h.shape[0], wt.shape[0]

    def body(h_ref, w_ref, o_ref):
        o_ref[...] = _bdot_nt(h_ref[...], w_ref[...])

    return pl.pallas_call(
        body, name=name, grid=(s // TOK_TILE,),
        in_specs=[pl.BlockSpec((TOK_TILE, D), lambda i: (i, 0)), pl.BlockSpec((n, D), lambda i: (0, 0))],
        out_specs=pl.BlockSpec((TOK_TILE, n), lambda i: (i, 0)),
        out_shape=jax.ShapeDtypeStruct((s, n), F32), compiler_params=_params("arbitrary"))(h, wt)


def _proj_wgrad(h, du, name):
    s, n = du.shape

    def body(h_ref, du_ref, o_ref):
        @pl.when(pl.program_id(0) == 0)
        def _():
            o_ref[...] = jnp.zeros_like(o_ref)

        o_ref[...] += _bdot_tn(du_ref[...], h_ref[...])

    return pl.pallas_call(
        body, name=name, grid=(s // TOK_TILE,),
        in_specs=[pl.BlockSpec((TOK_TILE, D), lambda i: (i, 0)), pl.BlockSpec((TOK_TILE, n), lambda i: (i, 0))],
        out_specs=pl.BlockSpec((n, D), lambda i: (0, 0)),
        out_shape=jax.ShapeDtypeStruct((n, D), F32), compiler_params=_params("arbitrary"))(h, du)


def _proj_xgrad(x, g, dx2, dus, ws, slabs, owners):
    s = x.shape[0]
    tile = HEAD_TILE
    k = len(dus)
    nx = len(slabs)
    n_in = 3 + 2 * k + nx

    def body(*refs):
        x_ref, g_ref, dx2_ref = refs[:3]
        du_refs, w_refs = refs[3:3 + k], refs[3 + k:3 + 2 * k]
        src_refs = refs[3 + 2 * k:3 + 2 * k + nx]
        dx_ref, dg_ref = refs[n_in:n_in + 2]
        dst_refs = refs[n_in + 2:n_in + 2 + nx]
        start, wait = _exchange_ops(src_refs, dst_refs, owners, refs[n_in + 2 + nx:])

        @pl.when(pl.program_id(0) == 0)
        def _():
            dg_ref[...] = jnp.zeros_like(dg_ref)
            start()

        dh = _bdot(du_refs[0][...], w_refs[0][...])
        for du_ref, w_ref in zip(du_refs[1:], w_refs[1:]):
            dh += _bdot(du_ref[...], w_ref[...])
        xv = x_ref[...]
        rs = lax.rsqrt(jnp.mean(xv * xv, axis=-1, keepdims=True) + RMS_EPS)
        xn = xv * rs
        dg_ref[...] += jnp.sum(dh * xn, axis=0, keepdims=True)
        dxn = dh * g_ref[...]
        dx_ref[...] = rs * (dxn - xn * jnp.mean(dxn * xn, axis=-1, keepdims=True)) + dx2_ref[...]

        @pl.when(pl.program_id(0) == s // tile - 1)
        def _():
            wait()

    tok = lambda n: pl.BlockSpec((tile, n), lambda i: (i, 0))
    fixed = lambda a: pl.BlockSpec(a.shape, lambda i: (0,) * a.ndim)
    out = pl.pallas_call(
        body, name="proj_xgrad", grid=(s // tile,),
        in_specs=([tok(D), fixed(g), tok(D)] + [tok(du.shape[1]) for du in dus] + [fixed(w) for w in ws]
                  + _hbm_specs(nx)),
        out_specs=[tok(D), pl.BlockSpec((1, D), lambda i: (0, 0))] + _hbm_specs(nx),
        out_shape=[jax.ShapeDtypeStruct((s, D), F32), jax.ShapeDtypeStruct((1, D), F32)] + _received_shapes(slabs, owners),
        scratch_shapes=_exchange_scratch(nx),
        compiler_params=_params("arbitrary"))(x, g, dx2, *dus, *ws, *slabs)
    return out[0], out[1], out[2:]


def _tail(x, target, ya, o, ub, ug, w_oa, w_ob, w_o, fg):
    s = x.shape[0]
    tile = TOK_TILE

    def body(x_ref, t_ref, ya_ref, o_ref, gb_ref, ug_ref, woa_ref, wob_ref, wo_ref, fg_ref,
             loss_ref, dfg_ref, dwo_ref, dwoa_ref, dwob_ref, dx2_ref, dya_ref, do_ref, dgb_ref, dug_ref):
        @pl.when(pl.program_id(0) == 0)
        def _():
            for r in (loss_ref, dfg_ref, dwo_ref, dwoa_ref, dwob_ref):
                r[...] = jnp.zeros_like(r)

        ya_v = ya_ref[...]
        gate_b = gb_ref[...]
        sg_b = _sigmoid(gate_b)
        silu_b = gate_b * sg_b
        o_v = jnp.concatenate([o_ref[h] for h in range(H)], axis=-1)
        yb_v = o_v * silu_b
        big_a = _bdot(ya_v, woa_ref[...])
        big_b = _bdot(yb_v, wob_ref[...])
        sa = _sigmoid(ug_ref[:, :D])
        sb = _sigmoid(ug_ref[:, D:])
        merged = sa * big_a + sb * big_b
        x2 = x_ref[...] + _bdot(merged, wo_ref[...])
        rs = lax.rsqrt(jnp.mean(x2 * x2, axis=-1, keepdims=True) + RMS_EPS)
        xn = x2 * rs
        err = xn * fg_ref[...] - t_ref[...]
        loss_ref[...] += (0.5 / D) * jnp.sum(err * err)
        dout = err * (1.0 / D)
        dfg_ref[...] += jnp.sum(dout * xn, axis=0, keepdims=True)
        dxn = dout * fg_ref[...]
        dx2 = rs * (dxn - xn * jnp.mean(dxn * xn, axis=-1, keepdims=True))
        dx2_ref[...] = dx2
        dwo_ref[...] += _bdot_tn(merged, dx2)
        dmerged = _bdot_nt(dx2, wo_ref[...])
        dbig_a = dmerged * sa
        dbig_b = dmerged * sb
        dug_ref[:, :D] = dmerged * big_a * sa * (1.0 - sa)
        dug_ref[:, D:] = dmerged * big_b * sb * (1.0 - sb)
        dwoa_ref[...] += _bdot_tn(ya_v, dbig_a)
        dwob_ref[...] += _bdot_tn(yb_v, dbig_b)
        dya_ref[...] = _bdot_nt(dbig_a, woa_ref[...])
        dyb = _bdot_nt(dbig_b, wob_ref[...])
        dgb_ref[...] = dyb * o_v * (sg_b * (1.0 + gate_b * (1.0 - sg_b)))
        _dov = dyb * silu_b
        for h in range(H):
            do_ref[h] = _dov[:, N * h:N * (h + 1)]

    tok = lambda n: pl.BlockSpec((tile, n), lambda i: (i, 0))
    hm = pl.BlockSpec((H, tile, N), lambda i: (0, i, 0))
    fixed = lambda shape: pl.BlockSpec(shape, lambda i: (0,) * len(shape))
    f32 = lambda *shape: jax.ShapeDtypeStruct(shape, F32)
    return pl.pallas_call(
        body, name="tail", grid=(s // tile,),
        in_specs=[tok(D), tok(D), tok(DA), hm, pl.BlockSpec((tile, DA), lambda i: (i, 3)), tok(NG),
                  fixed((DA, D)), fixed((DA, D)), fixed((D, D)), fixed((1, D))],
        out_specs=[fixed((1, 1)), fixed((1, D)), fixed((D, D)), fixed((DA, D)), fixed((DA, D)),
                   tok(D), tok(DA), hm, tok(DA), tok(NG)],
        out_shape=[f32(1, 1), f32(1, D), f32(D, D), f32(DA, D), f32(DA, D),
                   f32(s, D), f32(s, DA), f32(H, s, N), f32(s, DA), f32(s, NG)],
        compiler_params=_params("arbitrary"))(x, target, ya, o, ub, ug, w_oa, w_ob, w_o, fg)


def _pre_operands(ua_ref, prev_ref, first):
    cur = ua_ref[...]
    t = cur.shape[0]
    prev_row = jnp.where(first, 0.0, prev_ref[7:8, :])
    rows = lax.broadcasted_iota(jnp.int32, cur.shape, 0)
    sh = jnp.where(rows == 0, prev_row, pltpu.roll(cur, 1, axis=0))
    ops = []
    for c0, n in ((0, DA), (DA, DA), (2 * DA, DA), (3 * DA + 2 * RANK, DA), (3 * DA, RANK), (3 * DA + RANK, RANK)):
        ops += [cur[:, c0:c0 + n], sh[:, c0:c0 + n]]
    del t
    return ops


def _ua_specs(tile, order):
    blocks = tile // 8
    return [pl.BlockSpec((tile, NA), lambda i: (order(i), 0)),
            pl.BlockSpec((8, NA), lambda i: (jnp.maximum(order(i) * blocks - 1, 0), 0))]


def _rwkv_pre_fwd(ua, pre_params):
    s = ua.shape[0]
    tile = HEAD_TILE

    def body(ua_ref, prev_ref, *refs):
        p_refs, o_refs = refs[:len(pre_params)], refs[len(pre_params):]
        ops = _pre_operands(ua_ref, prev_ref, pl.program_id(0) == 0)
        outs = _rwkv_pre(*ops, *[p[...] for p in p_refs])
        for o_ref, val in zip(o_refs, outs):
            for h, col in enumerate(_to_heads(val)):
                o_ref[h] = col

    hm = pl.BlockSpec((H, tile, N), lambda i: (0, i, 0))
    return pl.pallas_call(
        body, name="rwkv_pre_fwd", grid=(s // tile,),
        in_specs=_ua_specs(tile, lambda i: i) + [pl.BlockSpec(p.shape, lambda i, nd=p.ndim: (0,) * nd) for p in pre_params],
        out_specs=[hm] * 8, out_shape=[jax.ShapeDtypeStruct((H, s, N), F32)] * 8,
        compiler_params=_params("arbitrary"))(ua, ua, *pre_params)


def _rwkv_pre_bwd(ua, pre_params, cots):
    s = ua.shape[0]
    tile = HEAD_TILE
    nt = s // tile
    n_p = len(pre_params)

    def body(ua_ref, prev_ref, *refs):
        p_refs, c_refs = refs[:n_p], refs[n_p:n_p + 11]
        dua_ref = refs[n_p + 11]
        dp_refs = refs[n_p + 12:n_p + 12 + n_p]
        carry_ref = refs[-1]
        i = pl.program_id(0)

        @pl.when(i == 0)
        def _():
            carry_ref[...] = jnp.zeros_like(carry_ref)
            for r in dp_refs:
                r[...] = jnp.zeros_like(r)

        ops = _pre_operands(ua_ref, prev_ref, i == nt - 1)
        _, vjp = jax.vjp(_rwkv_pre, *ops, *[p[...] for p in p_refs])
        c = [_from_heads(r) for r in c_refs]
        grads = vjp((c[0] + c[1], c[2], c[3], c[4] + c[5], c[6] + c[7], c[8], c[9], c[10]))
        d_ops, d_par = grads[:12], grads[12:]
        for r, val in zip(dp_refs, d_par):
            r[...] += val
        d_cur = jnp.concatenate([d_ops[0], d_ops[2], d_ops[4], d_ops[8], d_ops[10], d_ops[6]], axis=-1)
        d_sh = jnp.concatenate([d_ops[1], d_ops[3], d_ops[5], d_ops[9], d_ops[11], d_ops[7]], axis=-1)
        rows = lax.broadcasted_iota(jnp.int32, d_sh.shape, 0)
        dua_ref[...] = d_cur + jnp.where(rows == tile - 1, carry_ref[...], pltpu.roll(d_sh, tile - 1, axis=0))
        carry_ref[...] = d_sh[0:1, :]

    rev = lambda i: nt - 1 - i
    hm = pl.BlockSpec((H, tile, N), lambda i: (0, rev(i), 0))
    fixed = [pl.BlockSpec(p.shape, lambda i, nd=p.ndim: (0,) * nd) for p in pre_params]
    return pl.pallas_call(
        body, name="rwkv_pre_bwd", grid=(nt,),
        in_specs=_ua_specs(tile, rev) + fixed + [hm] * 11,
        out_specs=[pl.BlockSpec((tile, NA), lambda i: (rev(i), 0))] + fixed,
        out_shape=[jax.ShapeDtypeStruct((s, NA), F32)] + [jax.ShapeDtypeStruct(p.shape, F32) for p in pre_params],
        scratch_shapes=[pltpu.VMEM((1, NA), F32)],
        compiler_params=_params("arbitrary"))(ua, ua, *pre_params, *cots)


def _wkv_fwd(seq):
    s = seq[0].shape[1]
    nc = s // WKV_CHUNK

    def body(r_ref, lw_ref, cl_ref, k_ref, v_ref, a_ref, b_ref, y_ref, ck_ref, p_ref, state):
        @pl.when(pl.program_id(0) == 0)
        def _():
            state[...] = jnp.zeros_like(state)

        s0 = state[...]
        ck_ref[0] = s0
        p = _tri_inverse(_wkv_aab(True, lw_ref[...], cl_ref[...], a_ref[...], b_ref[...]))
        p_ref[0] = p
        y, s1 = _wkv_apply(True, s0, r_ref[...], lw_ref[...], cl_ref[...], k_ref[...], v_ref[...], a_ref[...],
                           b_ref[...], p)
        y_ref[...] = y
        state[...] = s1

    hm = pl.BlockSpec((H, WKV_CHUNK, N), lambda c: (0, c, 0))
    per_chunk = lambda m: pl.BlockSpec((1, H, m, m), lambda c: (c, 0, 0, 0))
    return pl.pallas_call(
        body, name="wkv_fwd", grid=(nc,), in_specs=[hm] * 7,
        out_specs=[hm, per_chunk(N), per_chunk(WKV_CHUNK)],
        out_shape=[jax.ShapeDtypeStruct((H, s, N), F32), jax.ShapeDtypeStruct((nc, H, N, N), F32),
                   jax.ShapeDtypeStruct((nc, H, WKV_CHUNK, WKV_CHUNK), F32)],
        scratch_shapes=[pltpu.VMEM((H, N, N), F32)], compiler_params=_params("arbitrary"))(*seq)


def _wkv_bwd(seq, ckpt, pinv, dy, slabs, owners):
    s = seq[0].shape[1]
    nc = s // WKV_CHUNK
    nx = len(slabs)

    def body(r_ref, lw_ref, cl_ref, k_ref, v_ref, a_ref, b_ref, ck_ref, p_ref, dy_ref, *refs):
        src_refs, d_refs, dst_refs = refs[:nx], refs[nx:nx + 7], refs[nx + 7:2 * nx + 7]
        dstate = refs[2 * nx + 7]
        start, wait = _exchange_ops(src_refs, dst_refs, owners, refs[2 * nx + 8:])

        @pl.when(pl.program_id(0) == 0)
        def _():
            dstate[...] = jnp.zeros_like(dstate)
            start()

        p = p_ref[0]
        lw, cl, a, b = lw_ref[...], cl_ref[...], a_ref[...], b_ref[...]
        _, vjp = jax.vjp(functools.partial(_wkv_apply, False), ck_ref[0], r_ref[...], lw, cl, k_ref[...], v_ref[...],
                         a, b, p)
        ds0, dr, dlw, dcl, dk, dv, da, db, dp = vjp((dy_ref[...], dstate[...]))
        dstate[...] = ds0
        _, vjp_x = jax.vjp(functools.partial(_wkv_aab, False), lw, cl, a, b)
        dlw2, dcl2, da2, db2 = vjp_x(_dot1(_dot1(p, dp, "tn"), p, "nt"))
        for d_ref, val in zip(d_refs, (dr, dlw + dlw2, dcl + dcl2, dk, dv, da + da2, db + db2)):
            d_ref[...] = val

        @pl.when(pl.program_id(0) == nc - 1)
        def _():
            wait()

    hm = pl.BlockSpec((H, WKV_CHUNK, N), lambda c: (0, nc - 1 - c, 0))
    per_chunk = lambda m: pl.BlockSpec((1, H, m, m), lambda c: (nc - 1 - c, 0, 0, 0))
    out = pl.pallas_call(
        body, name="wkv_bwd", grid=(nc,),
        in_specs=[hm] * 7 + [per_chunk(N), per_chunk(WKV_CHUNK), hm] + _hbm_specs(nx),
        out_specs=[hm] * 7 + _hbm_specs(nx),
        out_shape=[jax.ShapeDtypeStruct((H, s, N), F32)] * 7 + _received_shapes(slabs, owners),
        scratch_shapes=[pltpu.VMEM((H, N, N), F32)] + _exchange_scratch(nx),
        compiler_params=_params("arbitrary"))(*seq, ckpt, pinv, dy, *slabs)
    return out[:7], out[7:]


def _rwkv_post_fwd(y, r, k2, v, g, post_params):
    s = y.shape[1]
    tile = HEAD_TILE

    def body(y_ref, r_ref, k_ref, v_ref, g_ref, w_ref, b_ref, rk_ref, o_ref):
        o_ref[...] = _rwkv_post(*[_from_heads(ref) for ref in (y_ref, r_ref, k_ref, v_ref, g_ref)], w_ref[...],
                                b_ref[...], rk_ref[...])

    hm = pl.BlockSpec((H, tile, N), lambda i: (0, i, 0))
    par = pl.BlockSpec((1, DA), lambda i: (0, 0))
    return pl.pallas_call(
        body, name="rwkv_post_fwd", grid=(s // tile,), in_specs=[hm] * 5 + [par] * 3,
        out_specs=pl.BlockSpec((tile, DA), lambda i: (i, 0)), out_shape=jax.ShapeDtypeStruct((s, DA), F32),
        compiler_params=_params("arbitrary"))(y, r, k2, v, g, *post_params)


def _rwkv_post_bwd(y, r, k2, v, g, post_params, dya, slabs, lo):
    s = y.shape[1]
    tile = HEAD_TILE

    def body(y_ref, r_ref, k_ref, v_ref, g_ref, w_ref, b_ref, rk_ref, dya_ref, s_ref, *refs):
        d_refs, p_ref = refs[:8], refs[8]
        start, wait = _pair_swap_ops(s_ref, p_ref, lo, refs[9:])

        @pl.when(pl.program_id(0) == 0)
        def _():
            for ref in d_refs[5:]:
                ref[...] = jnp.zeros_like(ref)
            start()

        _, vjp = jax.vjp(_rwkv_post, *[_from_heads(ref) for ref in (y_ref, r_ref, k_ref, v_ref, g_ref)], w_ref[...],
                         b_ref[...], rk_ref[...])
        grads = vjp(dya_ref[...])
        for ref, val in zip(d_refs[:5], grads[:5]):
            for h, col in enumerate(_to_heads(val)):
                ref[h] = col
        for ref, val in zip(d_refs[5:], grads[5:]):
            ref[...] += val

        @pl.when(pl.program_id(0) == s // tile - 1)
        def _():
            wait()

    hm = pl.BlockSpec((H, tile, N), lambda i: (0, i, 0))
    par = pl.BlockSpec((1, DA), lambda i: (0, 0))
    return pl.pallas_call(
        body, name="rwkv_post_bwd", grid=(s // tile,),
        in_specs=[hm] * 5 + [par] * 3 + [pl.BlockSpec((tile, DA), lambda i: (i, 0))] + _hbm_specs(1),
        out_specs=[hm] * 5 + [par] * 3 + _hbm_specs(1),
        out_shape=[jax.ShapeDtypeStruct((H, s, N), F32)] * 5 + [jax.ShapeDtypeStruct((1, DA), F32)] * 3
        + [jax.ShapeDtypeStruct(slabs.shape, slabs.dtype)],
        scratch_shapes=_pair_swap_scratch(slabs.shape[0]),
        compiler_params=_params("arbitrary"))(y, r, k2, v, g, *post_params, dya, slabs)


def _tri(t):
    return (lax.broadcasted_iota(jnp.int32, (t, t), 0) >= lax.broadcasted_iota(jnp.int32, (t, t), 1)).astype(F32)


def _fox_pre_fwd(ub, uf, q_g, k_g, f_b):
    s = ub.shape[0]
    tile = HEAD_TILE

    def body(ub_ref, uf_ref, qg_ref, kg_ref, fb_ref, q_ref, k_ref, v_ref, cum_ref, carry):
        @pl.when(pl.program_id(0) == 0)
        def _():
            carry[...] = jnp.zeros_like(carry)

        qn, kn, logf = _fox_pre(ub_ref[:, :DA], ub_ref[:, DA:2 * DA], uf_ref[...], qg_ref[...], kg_ref[...],
                                fb_ref[...])
        for h, (q_col, k_col) in enumerate(zip(_to_heads(qn), _to_heads(kn))):
            q_ref[h] = q_col
            k_ref[h] = k_col
        v_ref[...] = _heads(ub_ref, 2 * DA)
        cum = jnp.dot(_tri(tile), logf, precision=HI, preferred_element_type=F32) + carry[...]
        cum_ref[...] = cum
        carry[...] = cum[tile - 1:tile, :]

    hm = pl.BlockSpec((H, tile, N), lambda i: (0, i, 0))
    fixed = lambda shape: pl.BlockSpec(shape, lambda i: (0,) * len(shape))
    return pl.pallas_call(
        body, name="fox_pre_fwd", grid=(s // tile,),
        in_specs=[pl.BlockSpec((tile, NB), lambda i: (i, 0)), pl.BlockSpec((tile, NF), lambda i: (i, 0)),
                  fixed((1, DA)), fixed((1, DA)), fixed((1, NF))],
        out_specs=[hm] * 3 + [pl.BlockSpec((tile, NF), lambda i: (i, 0))],
        out_shape=[jax.ShapeDtypeStruct((H, s, N), F32)] * 3 + [jax.ShapeDtypeStruct((s, NF), F32)],
        scratch_shapes=[pltpu.VMEM((1, NF), F32)], compiler_params=_params("arbitrary"))(ub, uf, q_g, k_g, f_b)


def _fox_pre_bwd(ub, uf, q_g, k_g, f_b, dqn, dkn, dvf, dgate, dcum_q, dcum_k):
    s = ub.shape[0]
    tile = HEAD_TILE
    nt = s // tile

    def body(ub_ref, uf_ref, qg_ref, kg_ref, fb_ref, dq_ref, dk_ref, dv_ref, dgate_ref, dcq_ref, dck_ref,
             dub_ref, duf_ref, dqg_ref, dkg_ref, dfb_ref, carry):
        @pl.when(pl.program_id(0) == 0)
        def _():
            carry[...] = jnp.zeros_like(carry)
            for ref in (dqg_ref, dkg_ref, dfb_ref):
                ref[...] = jnp.zeros_like(ref)

        dcum = dcq_ref[...] + dck_ref[...]
        dlogf = lax.dot_general(_tri(tile), dcum, (((0,), (0,)), ((), ())), precision=HI,
                                preferred_element_type=F32) + carry[...]
        carry[...] = dlogf[0:1, :]
        _, vjp = jax.vjp(_fox_pre, ub_ref[:, :DA], ub_ref[:, DA:2 * DA], uf_ref[...], qg_ref[...], kg_ref[...],
                         fb_ref[...])
        d_q, d_k, d_f, d_qg, d_kg, d_fb = vjp((_from_heads(dq_ref), _from_heads(dk_ref), dlogf))
        dub_ref[:, :DA] = d_q
        dub_ref[:, DA:2 * DA] = d_k
        _store_heads(dub_ref, 2 * DA, dv_ref[...])
        dub_ref[:, 3 * DA:] = dgate_ref[...]
        duf_ref[...] = d_f
        dqg_ref[...] += functools.reduce(jnp.add, _to_heads(d_qg))
        dkg_ref[...] += functools.reduce(jnp.add, _to_heads(d_kg))
        dfb_ref[...] += d_fb

    rev = lambda i: nt - 1 - i
    hm = pl.BlockSpec((H, tile, N), lambda i: (0, rev(i), 0))
    tok = lambda n: pl.BlockSpec((tile, n), lambda i: (rev(i), 0))
    fixed = lambda shape: pl.BlockSpec(shape, lambda i: (0,) * len(shape))
    return pl.pallas_call(
        body, name="fox_pre_bwd", grid=(nt,),
        in_specs=[tok(NB), tok(NF), fixed((1, DA)), fixed((1, DA)), fixed((1, NF)), hm, hm, hm, tok(DA), tok(NF),
                  tok(NF)],
        out_specs=[tok(NB), tok(NF), fixed((1, N)), fixed((1, N)), fixed((1, NF))],
        out_shape=[jax.ShapeDtypeStruct((s, NB), F32), jax.ShapeDtypeStruct((s, NF), F32),
                   jax.ShapeDtypeStruct((1, N), F32), jax.ShapeDtypeStruct((1, N), F32),
                   jax.ShapeDtypeStruct((1, NF), F32)],
        scratch_shapes=[pltpu.VMEM((1, NF), F32)],
        compiler_params=_params("arbitrary"))(ub, uf, q_g, k_g, f_b, dqn, dkn, dvf, dgate, dcum_q, dcum_k)


def _att_groups(s):
    blocks = s // ATT_TILE
    per = max(1, blocks // ATT_GROUPS)
    return per, blocks // per


def _att_parts(n, width):
    return ([(0, n - width, False)] if n > width else []) + [(n - width, n, True)]


def _att_scores(q_bf, k_ref, ck_ref, lo, hi, masked, row_offset):
    scores = _bdot_nt(q_bf, k_ref[0, lo:hi, :]) - ck_ref[0, :, lo:hi]
    if masked:
        rows = row_offset + lax.broadcasted_iota(jnp.int32, scores.shape, 0)
        scores = jnp.where(rows >= lax.broadcasted_iota(jnp.int32, scores.shape, 1), scores, -1e30)
    return scores


def _fox_attn_fwd(q, k, v, cum_q, cum_k):
    s = q.shape[1]
    t = ATT_TILE
    per, groups = _att_groups(s)

    def body(q_ref, k_ref, v_ref, cq_ref, ck_ref, o_ref, lse_ref):
        qi = pl.program_id(1)
        for g in range(groups):
            @pl.when(qi // per == g)
            def _(g=g):
                q_bf = (q_ref[0] * ATT_SCALE).astype(BF16)
                parts = _att_parts((g + 1) * per * t, per * t)
                scores = [_att_scores(q_bf, k_ref, ck_ref, lo, hi, masked, (qi - g * per) * t)
                          for lo, hi, masked in parts]
                m = functools.reduce(jnp.maximum, [jnp.max(sc, axis=-1, keepdims=True) for sc in scores])
                l, acc = 0.0, 0.0
                for sc, (lo, hi, _) in zip(scores, parts):
                    p = jnp.exp(sc - m)
                    l += jnp.sum(p, axis=-1, keepdims=True)
                    acc += _bdot(p, v_ref[0, lo:hi, :])
                o_ref[0] = acc / l
                lse_ref[0] = m + jnp.log(l) + cq_ref[0]

    qb = pl.BlockSpec((1, t, N), lambda h, i: (h, i, 0))
    kb = pl.BlockSpec((1, s, N), lambda h, i: (h, 0, 0))
    return pl.pallas_call(
        body, name="fox_attn_fwd", grid=(H, s // t),
        in_specs=[qb, kb, kb, pl.BlockSpec((1, t, 1), lambda h, i: (h, i, 0)),
                  pl.BlockSpec((1, 1, s), lambda h, i: (h, 0, 0))],
        out_specs=[qb, pl.BlockSpec((1, t, 1), lambda h, i: (h, i, 0))],
        out_shape=[jax.ShapeDtypeStruct((H, s, N), F32), jax.ShapeDtypeStruct((H, s, 1), F32)],
        compiler_params=_params("arbitrary", "arbitrary"))(q, k, v, cum_q, cum_k)


def _fox_attn_bwd(q, k, v, cum_q, cum_k, o, lse, do, slabs, owners):
    s = q.shape[1]
    t = ATT_TILE
    per, groups = _att_groups(s)
    nx = len(slabs)

    def body(q_ref, k_ref, v_ref, cq_ref, ck_ref, o_ref, lse_ref, do_ref, *refs):
        src_refs, (dq_ref, dk_ref, dv_ref, dcq_ref, dck_ref) = refs[:nx], refs[nx:nx + 5]
        start, wait = _exchange_ops(src_refs, refs[nx + 5:2 * nx + 5], owners, refs[2 * nx + 5:])
        qi = pl.program_id(1)

        @pl.when((pl.program_id(0) == 0) & (qi == 0))
        def _():
            start()

        @pl.when(qi == 0)
        def _():
            for ref in (dk_ref, dv_ref, dck_ref):
                ref[...] = jnp.zeros_like(ref)

        for g in range(groups):
            @pl.when(qi // per == g)
            def _(g=g):
                q_bf, do_bf = (q_ref[0] * ATT_SCALE).astype(BF16), do_ref[0].astype(BF16)
                row_term = cq_ref[0] - lse_ref[0]
                delta = jnp.sum(do_ref[0] * o_ref[0], axis=-1, keepdims=True)
                dq, dcq = 0.0, 0.0
                for lo, hi, masked in _att_parts((g + 1) * per * t, per * t):
                    p = jnp.exp(_att_scores(q_bf, k_ref, ck_ref, lo, hi, masked, (qi - g * per) * t) + row_term)
                    ds = p * (_bdot_nt(do_bf, v_ref[0, lo:hi, :]) - delta)
                    dq += _bdot(ds, k_ref[0, lo:hi, :])
                    dcq += jnp.sum(ds, axis=-1, keepdims=True)
                    dk_ref[0, lo:hi, :] += _bdot_tn(ds, q_bf)
                    dv_ref[0, lo:hi, :] += _bdot_tn(p, do_bf)
                    dck_ref[0, :, lo:hi] -= jnp.sum(ds, axis=0, keepdims=True)
                dq_ref[0] = dq * ATT_SCALE
                dcq_ref[0] = dcq

        @pl.when((pl.program_id(0) == H - 1) & (qi == s // t - 1))
        def _():
            wait()

    qb = pl.BlockSpec((1, t, N), lambda h, i: (h, i, 0))
    kb = pl.BlockSpec((1, s, N), lambda h, i: (h, 0, 0))
    cqb = pl.BlockSpec((1, t, 1), lambda h, i: (h, i, 0))
    ckb = pl.BlockSpec((1, 1, s), lambda h, i: (h, 0, 0))
    f32 = lambda *shape: jax.ShapeDtypeStruct(shape, F32)
    out = pl.pallas_call(
        body, name="fox_attn_bwd", grid=(H, s // t),
        in_specs=[qb, kb, kb, cqb, ckb, qb, cqb, qb] + _hbm_specs(nx), out_specs=[qb, kb, kb, cqb, ckb] + _hbm_specs(nx),
        out_shape=[f32(H, s, N), f32(H, s, N), f32(H, s, N), f32(H, s, 1), f32(H, 1, s)]
        + _received_shapes(slabs, owners),
        scratch_shapes=_exchange_scratch(nx),
        compiler_params=_params("arbitrary", "arbitrary"))(q, k, v, cum_q, cum_k, o, lse, do, *slabs)
    return out[:5], out[5:]


def _local_step(x, target, w, p):
    mu = p["shift_mu"]
    lora_matrix = lambda a: jnp.moveaxis(a, 0, 1).reshape(RANK, DA).astype(F32)
    pre_params = (mu[:, 0:DA], mu[:, DA:2 * DA], mu[:, 2 * DA:3 * DA], mu[:, 3 * DA + 2 * RANK:],
                  mu[:, 3 * DA:3 * DA + RANK], mu[:, 3 * DA + RANK:3 * DA + 2 * RANK],
                  lora_matrix(w["w_lora_up"]), p["w0"], lora_matrix(w["a_lora_up"]), p["a0"], p["k_k"], p["k_a"])
    post_params = (p["lnx_w"], p["lnx_b"], p["r_k"])
    q_g, k_g = jnp.tile(p["q_norm_g"], (1, H)), jnp.tile(p["k_norm_g"], (1, H))
    f_b = jnp.pad(p["f_bias"], ((0, 0), (0, NF - H)))
    fg = p["final_norm_g"].reshape(1, D)

    h = _rms_fwd(x, p["norm_g"])
    ua = _proj(h, w["in_a"], "proj_a")
    ub = _proj(h, w["in_b"], "proj_b")
    ug = _proj(h, w["in_g"], "proj_g")
    uf = _proj(h, w["in_f"], "proj_f")
    r, lw, cl, k2, v, av, bv, gg = _rwkv_pre_fwd(ua, pre_params)
    y, ckpt, pinv = _wkv_fwd((r, lw, cl, k2, v, av, bv))
    ya = _rwkv_post_fwd(y, r, k2, v, gg, post_params)
    qn, kn, vf, cum = _fox_pre_fwd(ub, uf, q_g, k_g, f_b)
    cum_t = cum[:, :H].T
    cum_q, cum_k = cum_t[:, :, None], cum_t[:, None, :]
    o, lse = _fox_attn_fwd(qn, kn, vf, cum_q, cum_k)

    (loss, dfg, dwo, dwoa, dwob, dx2, dya, do, dgate_b, dug) = _tail(
        x, target, ya, o, ub, ug, w["w_out_a"], w["w_out_b"], w["w_out"], fg)
    everyone = (0, N_DEV)
    (dqn, dkn, dvf, dcq, dck), (recv_woa, recv_wob, recv_wo) = _fox_attn_bwd(
        qn, kn, vf, cum_q, cum_k, o, lse, do,
        (_col_slabs(dwoa), _col_slabs(dwob), dwo.astype(BF16).reshape(N_DEV, D // N_DEV, D)), (everyone,) * 3)
    pad_f = lambda a: jnp.pad(a.T, ((0, 0), (0, NF - H)))
    dub, duf, dqg, dkg, dfb = _fox_pre_bwd(ub, uf, q_g, k_g, f_b, dqn, dkn, dvf, dgate_b,
                                           pad_f(dcq[:, :, 0]), pad_f(dck.reshape(H, -1)))
    dwt_b, dwt_g, dwt_f = (_proj_wgrad(h, du, name) for du, name in ((dub, "wgrad_b"), (dug, "wgrad_g"), (duf, "wgrad_f")))
    early = _slab_wt_grad((dwt_b, dwt_g, dwt_f), (_WT_SEGMENTS[1], _WT_SEGMENTS[2], _WT_SEGMENTS[3]), EARLY_FROM, N_DEV,
                          "slab_wt_early")
    dy, dr_p, dk_p, dv_p, dgg, dlnw, dlnb, drk, handed = _rwkv_post_bwd(y, r, k2, v, gg, post_params, dya, early,
                                                                          EARLY_FROM)
    early = _chip_sums(early, handed, EARLY_FROM, "chip_sums_early")
    (dr_s, dlw, dcl, dk_s, dv_s, dav, dbv), (recv_early,) = _wkv_bwd(
        (r, lw, cl, k2, v, av, bv), ckpt, pinv, dy, (early,), ((EARLY_FROM, N_DEV, "chips"),))
    pre_out = _rwkv_pre_bwd(ua, pre_params, (dr_s, dr_p, dlw, dcl, dk_s, dk_p, dv_s, dv_p, dav, dbv, dgg))
    dua, dpre = pre_out[0], pre_out[1:]
    dwt_a = _proj_wgrad(h, dua, "wgrad_a")

    flat = lambda a: a.reshape(1, -1)
    small = {
        "final_norm_g": dfg, "w0": dpre[7], "a0": dpre[9], "k_k": dpre[10], "k_a": dpre[11], "r_k": drk, "lnx_w": dlnw,
        "lnx_b": dlnb, "q_norm_g": dqg, "k_norm_g": dkg, "f_bias": dfb[:, :H],
        "shift_mu": jnp.concatenate([flat(dpre[0]), flat(dpre[1]), flat(dpre[2]), dpre[4], dpre[5], flat(dpre[3])], axis=1),
    }
    late = _slab_wt_grad((dwt_a, dwt_b), (_WT_SEGMENTS[0], _WT_SEGMENTS[1]), 0, EARLY_FROM, "slab_wt_late")
    late = _chip_sums(late, _pair_swap(late, 0, "pair_swap_late"), 0, "chip_sums_late")
    by_head = lambda a: jnp.moveaxis(a.reshape(RANK, H, N), 1, 0)
    loras = jnp.stack([by_head(dpre[6]), by_head(dpre[8])], axis=1).astype(BF16)
    dx, dng, (recv_late, recv_lora, recv_small) = _proj_xgrad(
        x, p["norm_g"], dx2, (dua, dub, dug, duf), (w["in_a"], w["in_b"], w["in_g"], w["in_f"]),
        (late, loras, _pack_small(small, loss)), ((0, EARLY_FROM, "chips"), everyone, everyone))
    return dx, dng, (recv_early, recv_late), (recv_woa, recv_wob, recv_wo, recv_lora), recv_small


def _position():
    return lax.axis_index("x"), lax.axis_index("y"), lax.axis_index("c")


def _hbm_specs(n):
    return [pl.BlockSpec(memory_space=pl.ANY)] * n


BIG_GATHER_COPIES = 13


def _all_gather(big, blocks, name):
    n = len(blocks)

    def body(*refs):
        big_ref, x_refs = refs[0], refs[1:1 + n]
        big_out, out_refs = refs[1 + n], refs[2 + n:2 + 2 * n]
        send_sems, recv_sems, local_sems = refs[2 + 2 * n:]
        x, y, c = _position()
        me, sibling = (x, y, c), (x, y, 1 - c)
        chips = [(1 - x, y), (x, 1 - y), (1 - x, 1 - y)]
        x_nbr, y_nbr, diag = chips
        cols = big_ref.shape[1] // 2

        def part(ref, h):
            return ref if h is None else ref.at[:, pl.ds(h * cols, cols)]

        def landed(chip, core, h):
            return part(big_out.at[4 * chip[0] + 2 * chip[1] + core], h)

        def big_copy(k, src, dst, to):
            return pltpu.make_async_remote_copy(src_ref=src, dst_ref=dst, send_sem=send_sems.at[7 * n + k],
                                                recv_sem=recv_sems.at[7 * n + k], device_id=to, device_id_type=MESH)

        def arrival(k, chip, core, h):
            dst = landed(chip, core, h)
            return big_copy(k, dst, dst, me)

        def pass_on(k, chip, h, to):
            src = landed(chip, c, h)
            return big_copy(k, src, src, to)

        big_mine = pltpu.make_async_copy(big_ref, landed((x, y), c, None), local_sems.at[n])
        big_mine.start()
        here = (x, y)
        big_sent = [big_copy(0, big_ref, landed(here, c, None), sibling),
                    big_copy(1, part(big_ref, 0), landed(here, c, 0), (*x_nbr, c)),
                    big_copy(2, part(big_ref, 1), landed(here, c, 1), (*y_nbr, c)),
                    big_copy(3, part(big_ref, 1), landed(here, c, 1), (*x_nbr, c)),
                    big_copy(4, part(big_ref, 0), landed(here, c, 0), (*y_nbr, c))]
        for cp in big_sent:
            cp.start()

        def copy(a, k, blk, to, own=False):
            dst = out_refs[a].at[4 * blk[0] + 2 * blk[1] + blk[2]]
            return pltpu.make_async_remote_copy(
                src_ref=x_refs[a] if own else dst, dst_ref=dst, send_sem=send_sems.at[7 * a + k],
                recv_sem=recv_sems.at[7 * a + k], device_id=to, device_id_type=MESH)

        mine = [pltpu.make_async_copy(x_refs[a], out_refs[a].at[4 * x + 2 * y + c], local_sems.at[a]) for a in range(n)]
        for cp in mine:
            cp.start()
        first = []
        for a in range(n):
            first.append(copy(a, 0, me, sibling, own=True))
            first += [copy(a, 1 + j, me, (*chip, c), own=True) for j, chip in enumerate(chips)]
        for cp in first:
            cp.start()

        big_steps = [(1, x_nbr, 0, (*y_nbr, c), 5, 7), (2, y_nbr, 1, (*x_nbr, c), 6, 8), (3, x_nbr, 1, None, None, 9),
                     (4, y_nbr, 0, None, None, 10), (5, diag, 0, None, None, 11), (6, diag, 1, None, None, 12)]
        for k, chip, h, onward, k_onward, k_sibling in big_steps:
            arrival(k, chip, c, h).wait_recv()
            if onward is not None:
                big_sent.append(pass_on(k_onward, chip, h, onward))
                big_sent[-1].start()
            big_sent.append(pass_on(k_sibling, chip, h, sibling))
            big_sent[-1].start()

        passed = []
        for j, chip in enumerate(chips):
            for a in range(n):
                copy(a, 1 + j, (*chip, c), me).wait_recv()
                passed.append(copy(a, 4 + j, (*chip, c), sibling))
                passed[-1].start()
        for a in range(n):
            copy(a, 0, sibling, me).wait_recv()
        for j, chip in enumerate(chips):
            for a in range(n):
                copy(a, 4 + j, (*chip, 1 - c), me).wait_recv()
        arrival(0, here, 1 - c, None).wait_recv()
        for k, chip, h, _, _, k_sibling in big_steps:
            arrival(k_sibling, chip, 1 - c, h).wait_recv()
        for cp in first + passed + big_sent:
            cp.wait_send()
        for cp in mine + [big_mine]:
            cp.wait()

    everything = [big] + list(blocks)
    return pl.pallas_call(
        body, name=name, out_shape=[jax.ShapeDtypeStruct((N_DEV,) + b.shape, b.dtype) for b in everything],
        in_specs=_hbm_specs(n + 1), out_specs=_hbm_specs(n + 1),
        scratch_shapes=[pltpu.SemaphoreType.DMA((7 * n + BIG_GATHER_COPIES,)),
                        pltpu.SemaphoreType.DMA((7 * n + BIG_GATHER_COPIES,)), pltpu.SemaphoreType.DMA((n + 1,))],
    )(*everything)


def _received_shapes(slabs, owners):
    return [jax.ShapeDtypeStruct((N_DEV // 2 if len(o) == 3 else N_DEV,) + s.shape[1:], s.dtype)
            for s, o in zip(slabs, owners)]


def _pair_swap_scratch(n):
    return [pltpu.SemaphoreType.DMA((n,)), pltpu.SemaphoreType.DMA((n,))]


def _pair_swap_ops(s_ref, p_ref, lo, sems):
    send_sems, recv_sems = sems
    n = s_ref.shape[0]

    def run(sending):
        x, y, c = _position()
        for side in (0, 1):
            mine = [pltpu.make_async_remote_copy(src_ref=s_ref.at[i], dst_ref=p_ref.at[i], send_sem=send_sems.at[i],
                                                 recv_sem=recv_sems.at[i], device_id=(x, y, 1 - c), device_id_type=MESH)
                    for i in range(n) if (lo + i) % 2 == side]

            @pl.when(c != side)
            def _():
                for cp in mine:
                    cp.start() if sending else cp.wait_send()

            if not sending:
                @pl.when(c == side)
                def _():
                    for cp in mine:
                        cp.wait_recv()

    return functools.partial(run, True), functools.partial(run, False)


def _pair_swap(slabs, lo, name):
    n = slabs.shape[0]

    def body(s_ref, p_ref, *sems):
        start, wait = _pair_swap_ops(s_ref, p_ref, lo, sems)
        start()
        wait()

    return pl.pallas_call(
        body, name=name, out_shape=jax.ShapeDtypeStruct(slabs.shape, slabs.dtype),
        in_specs=_hbm_specs(1), out_specs=_hbm_specs(1)[0], scratch_shapes=_pair_swap_scratch(n))(slabs)


def _chip_sums(slabs, swapped, lo, name):
    n, rows, cols = slabs.shape
    tile = W_IN_COL_TILE

    def body(s_ref, p_ref, o_ref):
        c = lax.axis_index("c")
        for i in range(n):
            @pl.when(c == (lo + i) % 2)
            def _(i=i):
                o_ref[i] = (s_ref[i].astype(F32) + p_ref[i].astype(F32)).astype(BF16)

    blk = pl.BlockSpec((n, rows, tile), lambda j: (0, 0, j))
    return pl.pallas_call(
        body, name=name, grid=(cols // tile,), in_specs=[blk, blk], out_specs=blk,
        out_shape=jax.ShapeDtypeStruct(slabs.shape, BF16), compiler_params=_params("arbitrary"))(slabs, swapped)


def _exchange_scratch(n):
    return [pltpu.SemaphoreType.DMA((7 * n,)), pltpu.SemaphoreType.DMA((7 * n,)), pltpu.SemaphoreType.DMA((n,))]


def _exchange_ops(src_refs, dst_refs, owners, sems):
    send_sems, recv_sems, local_sems = sems
    n = len(src_refs)

    def guarded(a, dev, fn):
        lo, hi = owners[a][:2]
        if (lo, hi) == (0, N_DEV):
            fn()
        else:
            pl.when((dev >= lo) & (dev < hi))(fn)

    def src(a, dev):
        ref = src_refs[a]
        return ref.at[0] if ref.shape[0] == 1 else ref.at[dev - owners[a][0]]

    def run(sending, waiting):
        x, y, c = _position()
        me = 4 * x + 2 * y + c
        for a in range(n):
            by_chip = len(owners[a]) == 3
            slot = (lambda qx, qy, qc: 2 * qx + qy) if by_chip else (lambda qx, qy, qc: 4 * qx + 2 * qy + qc)
            mine = slot(x, y, c)
            local = lambda a=a, mine=mine: pltpu.make_async_copy(src(a, me), dst_refs[a].at[mine], local_sems.at[a])
            if sending:
                guarded(a, me, lambda local=local: local().start())
            for m in range(2, N_DEV, 2) if by_chip else range(1, N_DEV):
                px, py, pc = x ^ (m >> 2), y ^ ((m >> 1) & 1), c ^ (m & 1)
                peer = 4 * px + 2 * py + pc
                theirs = slot(px, py, pc)
                sem = dict(send_sem=send_sems.at[7 * a + m - 1], recv_sem=recv_sems.at[7 * a + m - 1],
                           device_id=(px, py, pc), device_id_type=MESH)
                send = lambda a=a, peer=peer, sem=sem, mine=mine: pltpu.make_async_remote_copy(
                    src_ref=src(a, peer), dst_ref=dst_refs[a].at[mine], **sem)
                recv = lambda a=a, sem=sem, theirs=theirs: pltpu.make_async_remote_copy(
                    src_ref=src(a, me), dst_ref=dst_refs[a].at[theirs], **sem)
                if sending:
                    guarded(a, peer, lambda send=send: send().start())
                if waiting:
                    guarded(a, me, lambda recv=recv: recv().wait_recv())
                    guarded(a, peer, lambda send=send: send().wait_send())
            if waiting:
                guarded(a, me, lambda local=local: local().wait())

    return functools.partial(run, True, False), functools.partial(run, False, True)


def _sum_slabs(r_ref):
    g = r_ref[0].astype(F32)
    for k in range(1, r_ref.shape[0]):
        g = g + r_ref[k].astype(F32)
    return g


def _adamw(g, w, m, v):
    m_new = ADAM_B1 * m + (1.0 - ADAM_B1) * g
    v_new = ADAM_B2 * v + (1.0 - ADAM_B2) * (g * g)
    m_hat = m_new / (1.0 - ADAM_B1 ** ADAM_STEP)
    v_hat = v_new / (1.0 - ADAM_B2 ** ADAM_STEP)
    return g, -ADAM_LR * (m_hat / (jnp.sqrt(v_hat) + ADAM_EPS) + ADAM_WD * w), m_new, v_new


def _adamw_w_in(recv_early, recv_late, w, m, v, slabs, owners):
    rows, cols = w.shape
    tile = W_IN_COL_TILE
    nx = len(slabs)

    def body(early_ref, late_ref, w_ref, m_ref, v_ref, *refs):
        src_refs, o_refs, dst_refs = refs[:nx], refs[nx:nx + 4], refs[nx + 4:2 * nx + 4]
        start, wait = _exchange_ops(src_refs, dst_refs, owners, refs[2 * nx + 4:])
        x, y, c = _position()
        early_owner = 4 * x + 2 * y + c >= EARLY_FROM

        @pl.when(pl.program_id(0) == 0)
        def _():
            start()

        def update(g):
            for o_ref, val in zip(o_refs, _adamw(g, w_ref[...], m_ref[...], v_ref[...])):
                o_ref[...] = val

        pl.when(early_owner)(lambda: update(_sum_slabs(early_ref)))
        pl.when(jnp.logical_not(early_owner))(lambda: update(_sum_slabs(late_ref)))

        @pl.when(pl.program_id(0) == cols // tile - 1)
        def _():
            wait()

    blk = pl.BlockSpec((rows, tile), lambda i: (0, i))
    slots = lambda r: pl.BlockSpec((r.shape[0], rows, tile), lambda i: (0, 0, i))
    out = pl.pallas_call(
        body, name="adamw_w_in", grid=(cols // tile,),
        in_specs=[slots(recv_early), slots(recv_late), blk, blk, blk] + _hbm_specs(nx),
        out_specs=[blk] * 4 + _hbm_specs(nx),
        out_shape=[jax.ShapeDtypeStruct((rows, cols), F32)] * 4 + _received_shapes(slabs, owners),
        scratch_shapes=_exchange_scratch(nx),
        compiler_params=_params("arbitrary"))(recv_early, recv_late, w, m, v, *slabs)
    return out[:4], out[4:]


def _adamw_misc(recvs, recv_small, recv_norm, params):
    names = list(params)
    flat = [a for n in names for a in params[n]]

    def body(woa_ref, wob_ref, wo_ref, lora_ref, small_ref, norm_ref, *refs):
        p_refs, o_refs = refs[:len(flat)], refs[len(flat):]
        g_small = _sum_slabs(small_ref)
        g_lora = _sum_slabs(lora_ref)
        grads = {"w_out_a": _sum_slabs(woa_ref), "w_out_b": _sum_slabs(wob_ref), "w_out": _sum_slabs(wo_ref),
                 "w_lora_up": g_lora[0], "a_lora_up": g_lora[1], "norm_g": _sum_slabs(norm_ref)}
        for n, (off, size) in SMALL_SLOTS.items():
            grads[n] = g_small[:, off:off + size]
        for i, n in enumerate(names):
            w_ref, m_ref, v_ref = p_refs[3 * i:3 * i + 3]
            for o_ref, val in zip(o_refs[4 * i:4 * i + 4], _adamw(grads[n], w_ref[...], m_ref[...], v_ref[...])):
                o_ref[...] = val
        o_refs[-1][...] = g_small[:, LOSS_SLOT:LOSS_SLOT + 1]

    out = pl.pallas_call(
        body, name="adamw_misc",
        out_shape=[jax.ShapeDtypeStruct(params[n][0].shape, F32) for n in names for _ in range(4)]
        + [jax.ShapeDtypeStruct((1, 1), F32)],
        compiler_params=_params())(*recvs, recv_small, recv_norm, *flat)
    return {n: out[4 * i:4 * i + 4] for i, n in enumerate(names)}, out[-1]


_WT_SEGMENTS = ((0, NA), (NA, NB), (NA + NB + H, NG), (NA + NB, H))


def _split_wt(gathered):
    tile = W_IN_COL_TILE

    def body(g_ref, *o_refs):
        full = jnp.concatenate([g_ref[j] for j in range(N_DEV)], axis=0)
        for o_ref, (row, n) in zip(o_refs, _WT_SEGMENTS):
            seg = full[row:row + n]
            if n < o_ref.shape[0]:
                seg = jnp.concatenate([seg, jnp.zeros((o_ref.shape[0] - n, tile), BF16)], axis=0)
            o_ref[...] = seg

    sizes = (NA, NB, NG, NF)
    return pl.pallas_call(
        body, name="split_wt", grid=(D // tile,),
        in_specs=[pl.BlockSpec((N_DEV, COLS_PER_DEV, tile), lambda i: (0, 0, i))],
        out_specs=[pl.BlockSpec((n, tile), lambda i: (0, i)) for n in sizes],
        out_shape=[jax.ShapeDtypeStruct((n, D), BF16) for n in sizes],
        compiler_params=_params("arbitrary"))(gathered)


def _slab_wt_grad(segments, seg_rows, dev_lo, dev_hi, name):
    tile = W_IN_COL_TILE
    k = len(segments)

    def body(*refs):
        seg_refs, o_ref = refs[:k], refs[k]
        for j in range(dev_lo, dev_hi):
            lo, hi = COLS_PER_DEV * j, COLS_PER_DEV * (j + 1)
            parts = []
            for ref, (row, n) in sorted(zip(seg_refs, seg_rows), key=lambda t: t[1][0]):
                first, last = max(lo, row), min(hi, row + n)
                if first < last:
                    parts.append(ref[first - row:last - row, :])
            o_ref[j - dev_lo] = (parts[0] if len(parts) == 1 else jnp.concatenate(parts, axis=0)).astype(BF16)

    return pl.pallas_call(
        body, name=name, grid=(D // tile,),
        in_specs=[pl.BlockSpec((s.shape[0], tile), lambda i: (0, i)) for s in segments],
        out_specs=pl.BlockSpec((dev_hi - dev_lo, COLS_PER_DEV, tile), lambda i: (0, 0, i)),
        out_shape=jax.ShapeDtypeStruct((dev_hi - dev_lo, COLS_PER_DEV, D), BF16),
        compiler_params=_params("arbitrary"))(*segments)


def _by_cols(a):
    return jnp.moveaxis(a, 0, 1).reshape(a.shape[1], -1)


def _col_slabs(a):
    return jnp.moveaxis(a.reshape(a.shape[0], N_DEV, -1), 1, 0).astype(BF16)


def _pack_small(grads, loss):
    pieces, at = [], 0
    for n, (off, size) in list(SMALL_SLOTS.items()) + [("loss", (LOSS_SLOT, 1))]:
        pieces += [jnp.zeros((off - at,), F32), (loss if n == "loss" else grads[n]).reshape(-1)]
        at = off + size
    return jnp.concatenate(pieces + [jnp.zeros((SMALL_LEN - at,), F32)]).reshape(1, 1, SMALL_LEN)


def _gather_weights(t):
    cast = lambda a: a.astype(BF16)
    loras = jnp.stack([t["w_lora_up"][0], t["a_lora_up"][0]])
    wt, woa, wob, wo, lora = _all_gather(
        cast(t["w_in"][0].T), [cast(t["w_out_a"][0]), cast(t["w_out_b"][0]), cast(t["w_out"][0]), cast(loras)],
        "weight_gather")
    in_a, in_b, in_g, in_f = _split_wt(wt)
    return {"in_a": in_a, "in_b": in_b, "in_g": in_g, "in_f": in_f, "w_out_a": _by_cols(woa), "w_out_b": _by_cols(wob),
            "w_out": wo.reshape(D, D), "w_lora_up": lora[:, 0], "a_lora_up": lora[:, 1]}


def kernel(x, norm_g, w_in, shift_mu, w_lora_up, w0, a_lora_up, a0, k_k, k_a, r_k, lnx_w, lnx_b, f_bias, q_norm_g, k_norm_g, w_out_a, w_out_b, w_out, final_norm_g, loss_target, m_norm_g, m_w_in, m_shift_mu, m_w_lora_up, m_w0, m_a_lora_up, m_a0, m_k_k, m_k_a, m_r_k, m_lnx_w, m_lnx_b, m_f_bias, m_q_norm_g, m_k_norm_g, m_w_out_a, m_w_out_b, m_w_out, m_final_norm_g, v_norm_g, v_w_in, v_shift_mu, v_w_lora_up, v_w0, v_a_lora_up, v_a0, v_k_k, v_k_a, v_r_k, v_lnx_w, v_lnx_b, v_f_bias, v_q_norm_g, v_k_norm_g, v_w_out_a, v_w_out_b, v_w_out, v_final_norm_g):
    names = ("norm_g", "w_in", "shift_mu", "w_lora_up", "w0", "a_lora_up", "a0", "k_k", "k_a", "r_k", "lnx_w", "lnx_b",
             "f_bias", "q_norm_g", "k_norm_g", "w_out_a", "w_out_b", "w_out", "final_norm_g")
    weights = dict(zip(names, (norm_g, w_in, shift_mu, w_lora_up, w0, a_lora_up, a0, k_k, k_a, r_k, lnx_w, lnx_b,
                               f_bias, q_norm_g, k_norm_g, w_out_a, w_out_b, w_out, final_norm_g)))
    m_in = dict(zip(names, (m_norm_g, m_w_in, m_shift_mu, m_w_lora_up, m_w0, m_a_lora_up, m_a0, m_k_k, m_k_a, m_r_k,
                            m_lnx_w, m_lnx_b, m_f_bias, m_q_norm_g, m_k_norm_g, m_w_out_a, m_w_out_b, m_w_out,
                            m_final_norm_g)))
    v_in = dict(zip(names, (v_norm_g, v_w_in, v_shift_mu, v_w_lora_up, v_w0, v_a_lora_up, v_a0, v_k_k, v_k_a, v_r_k,
                            v_lnx_w, v_lnx_b, v_f_bias, v_q_norm_g, v_k_norm_g, v_w_out_a, v_w_out_b, v_w_out,
                            v_final_norm_g)))

    matrices = ("w_out_a", "w_out_b", "w_out", "w_lora_up", "a_lora_up")
    as_2d = lambda n, a: a[0] if n in matrices else a.reshape(1, -1)

    full = _gather_weights(weights)
    dx, dng, recv_wt, recvs, recv_small = _local_step(
        x[0], loss_target[0], full, {n: as_2d(n, weights[n]) for n in ("norm_g",) + tuple(SMALL_SLOTS)})

    res, (recv_norm,) = _adamw_w_in(*recv_wt, w_in[0].T, m_w_in[0].T, v_w_in[0].T, (dng[None],), ((0, N_DEV),))
    outs = {"w_in": [r.T[None] for r in res]}
    misc = [n for n in names if n != "w_in"]
    res, loss_sum = _adamw_misc(recvs, recv_small, recv_norm,
                                {n: tuple(as_2d(n, t[n]) for t in (weights, m_in, v_in)) for n in misc})
    for n in misc:
        outs[n] = [r.reshape(weights[n].shape) for r in res[n]]
    return (loss_sum.reshape(()), dx[None], *[outs[n][i] for i in range(4) for n in names])
```

```python
import functools
import math

import jax
import jax.numpy as jnp
from jax import lax
from jax.experimental import pallas as pl
from jax.experimental.pallas import tpu as pltpu

F32 = jnp.float32
BF16 = jnp.bfloat16
HI = lax.Precision.HIGHEST
MESH = pl.DeviceIdType.MESH

N_DEV = 8
D = 1024
H = 8
N = 64
DA = H * N
RANK = 64
NA = 4 * DA + 2 * RANK
NB = 4 * DA
NG = 2 * D
NF = 128
IN_COLS = NA + NB + H + NG
COLS_PER_DEV = IN_COLS // N_DEV
RMS_EPS = 1e-6
LNX_EPS = 64e-5
ATT_SCALE = N ** -0.5

ADAM_LR = 0.001
ADAM_B1 = 0.9
ADAM_B2 = 0.999
ADAM_EPS = 1e-08
ADAM_WD = 0.01
ADAM_STEP = 10

LANES = 128
WKV_CHUNK = 64
TOK_TILE = 256
HEAD_TILE = 128
ATT_TILE = 256
ATT_GROUPS = 8
VMEM_LIMIT = 56 * 1024 * 1024

SMALL_SLOTS = {"final_norm_g": (0, D), "shift_mu": (D, NA), "w0": (3200, DA), "a0": (3712, DA), "k_k": (4224, DA),
               "k_a": (4736, DA), "r_k": (5248, DA), "lnx_w": (5760, DA), "lnx_b": (6272, DA), "q_norm_g": (6784, N),
               "k_norm_g": (6912, N), "f_bias": (7040, H)}
LOSS_SLOT = 7168
SMALL_LEN = 7296
W_IN_COL_TILE = 256
EARLY_FROM = -(-NA // COLS_PER_DEV)


def _params(*sem):
    return pltpu.CompilerParams(dimension_semantics=sem or None, vmem_limit_bytes=VMEM_LIMIT)


def _bdot(a, b):
    return jnp.dot(a.astype(BF16), b.astype(BF16), preferred_element_type=F32)


def _bdot_nt(a, b):
    return lax.dot_general(a.astype(BF16), b.astype(BF16), (((1,), (1,)), ((), ())), preferred_element_type=F32)


def _bdot_tn(a, b):
    return lax.dot_general(a.astype(BF16), b.astype(BF16), (((0,), (0,)), ((), ())), preferred_element_type=F32)


def _sigmoid(x):
    return 1.0 / (1.0 + jnp.exp(-x))


def _softplus(x):
    return jnp.maximum(x, 0.0) + jnp.log(1.0 + jnp.exp(-jnp.abs(x)))


def _heads(ref, col0):
    return jnp.stack([ref[:, col0 + N * h:col0 + N * (h + 1)] for h in range(H)])


def _store_heads(ref, col0, val):
    for h in range(H):
        ref[:, col0 + N * h:col0 + N * (h + 1)] = val[h]


def _lerp(c, s, mu):
    return c + (s - c) * mu


def _head_sums(x):
    low = lax.broadcasted_iota(jnp.int32, (x.shape[0], LANES), 1) < N
    out = []
    for p in range(x.shape[1] // LANES):
        pair = x[:, LANES * p:LANES * (p + 1)]
        first = jnp.sum(jnp.where(low, pair, 0.0), axis=-1, keepdims=True)
        second = jnp.sum(jnp.where(low, 0.0, pair), axis=-1, keepdims=True)
        out.append(jnp.where(low, first, second))
    return jnp.concatenate(out, axis=-1)


def _to_heads(x):
    return [x[:, N * h:N * (h + 1)] for h in range(H)]


def _from_heads(ref):
    return jnp.concatenate([ref[h] for h in range(H)], axis=-1)


def _rwkv_pre(rc, rs, kc, ks, vc, vs, gc, gs, wdc, wds, adc, ads,
              mu_r, mu_k, mu_v, mu_g, mu_wd, mu_ad, w_up, w0, a_up, a0, k_k, k_a):
    r = _lerp(rc, rs, mu_r)
    k = _lerp(kc, ks, mu_k)
    v = _lerp(vc, vs, mu_v)
    g = _lerp(gc, gs, mu_g)
    wd = _lerp(wdc, wds, mu_wd)
    ad = _lerp(adc, ads, mu_ad)
    t = wd.shape[0]
    w_raw = -_softplus(-(w0 + _bdot(jnp.tanh(wd), w_up))) - 0.5
    lw = -jnp.exp(w_raw)
    row = lax.broadcasted_iota(jnp.int32, (t, t), 0)
    col = lax.broadcasted_iota(jnp.int32, (t, t), 1)
    same_chunk = ((row >= col) & (row // WKV_CHUNK == col // WKV_CHUNK)).astype(F32)
    cl = jnp.dot(same_chunk, lw, precision=HI, preferred_element_type=F32)
    alr = _sigmoid(a0 + _bdot(ad, a_up))
    kk = k * k_k
    kk = kk / jnp.maximum(jnp.sqrt(_head_sums(kk * kk)), 1e-12)
    k2 = k * (1.0 + (alr - 1.0) * k_a)
    return r, lw, cl, k2, v, -kk, kk * alr, g


_MM_DIMS = {"nn": (((2,), (1,)), ((0,), (0,))), "nt": (((2,), (2,)), ((0,), (0,))), "tn": (((1,), (1,)), ((0,), (0,)))}


def _split(x):
    hi = x.astype(BF16)
    return hi, (x - hi.astype(F32)).astype(BF16)


def _dot3(a, b, kind):
    ah, al = _split(a)
    bh, bl = _split(b)
    dot = functools.partial(lax.dot_general, dimension_numbers=_MM_DIMS[kind], preferred_element_type=F32)
    return dot(ah, bh) + (dot(ah, bl) + dot(al, bh))


def _dot1(a, b, kind):
    return lax.dot_general(a.astype(BF16), b.astype(BF16), dimension_numbers=_MM_DIMS[kind], preferred_element_type=F32)


@functools.partial(jax.custom_vjp, nondiff_argnums=(2, 3))
def _mm(a, b, kind, fine=True):
    return _dot3(a, b, kind) if fine else _dot1(a, b, kind)


def _mm_fwd(a, b, kind, fine):
    return _mm(a, b, kind, fine), (a, b)


def _mm_bwd(kind, fine, res, ct):
    a, b = res
    if kind == "nn":
        return _dot1(ct, b, "nt"), _dot1(a, ct, "tn")
    if kind == "nt":
        return _dot1(ct, b, "nn"), _dot1(ct, a, "tn")
    return _dot1(b, ct, "nt"), _dot1(a, ct, "nn")


_mm.defvjp(_mm_fwd, _mm_bwd)


def _chunk_masks(c):
    row = lax.broadcasted_iota(jnp.int32, (c, c), 0)
    col = lax.broadcasted_iota(jnp.int32, (c, c), 1)
    return (row >= col)[None], (row > col)[None], (row == col).astype(F32)[None]


def _wkv_aab(fine, lw, cl, a, b):
    _, strict, _ = _chunk_masks(a.shape[1])
    return jnp.where(strict, _mm(a * jnp.exp(cl - lw), b * jnp.exp(-cl), "nt", fine), 0.0)


def _tri_inverse(x):
    c = x.shape[1]
    p = _chunk_masks(c)[2] + x
    for _ in range(int(math.log2(c)) - 1):
        x = _dot1(x, x, "nn")
        p = p + _dot1(p, x, "nn")
    return p


def _wkv_apply(fine, s0, r, lw, cl, k, v, a, b, p):
    c = r.shape[1]
    incl, strict, _ = _chunk_masks(c)
    mm = functools.partial(_mm, fine=fine)
    gi = jnp.exp(-cl)
    left = jnp.concatenate([a * jnp.exp(cl - lw), r * jnp.exp(cl)], axis=1)
    right = jnp.concatenate([b * gi, k * gi], axis=1)
    m = mm(left, right, "nt")
    z0 = mm(left, s0, "nt")
    a_ak = jnp.where(strict, m[:, :c, c:], 0.0)
    row = lax.broadcasted_iota(jnp.int32, (c, 2 * c), 0)
    col = lax.broadcasted_iota(jnp.int32, (c, 2 * c), 1)
    a_r = jnp.where((row >= col % c)[None], m[:, c:, :], 0.0)
    sa = mm(p, z0[:, :c] + mm(a_ak, v, "nn"), "nn")
    sa_v = jnp.concatenate([sa, v], axis=1)
    y = z0[:, c:] + mm(a_r, sa_v, "nn")
    s1 = (s0 + mm(sa_v, right, "tn")) * jnp.exp(cl[:, c - 1:c, :])
    return y, s1


def _rwkv_post(y, r, k2, v, g, lnx_w, lnx_b, r_k):
    yc = y - _head_sums(y) * (1.0 / N)
    var = _head_sums(yc * yc) * (1.0 / N)
    yn = yc * lax.rsqrt(var + LNX_EPS) * lnx_w + lnx_b
    bonus = _head_sums(r * k2 * r_k) * v
    return (yn + bonus) * (g * _sigmoid(g))


def _fox_pre(q, k, f, q_g, k_g, f_b):
    qn = q * lax.rsqrt(_head_sums(q * q) * (1.0 / N) + RMS_EPS) * q_g
    kn = k * lax.rsqrt(_head_sums(k * k) * (1.0 / N) + RMS_EPS) * k_g
    x = f + f_b
    return qn, kn, jnp.minimum(x, 0.0) - jnp.log(1.0 + jnp.exp(-jnp.abs(x)))


def _rms_fwd(x, g):
    s = x.shape[0]

    def body(x_ref, g_ref, h_ref):
        xv = x_ref[...]
        h_ref[...] = (xv * lax.rsqrt(jnp.mean(xv * xv, axis=-1, keepdims=True) + RMS_EPS) * g_ref[...]).astype(BF16)

    return pl.pallas_call(
        body, name="rms_fwd", grid=(s // TOK_TILE,),
        in_specs=[pl.BlockSpec((TOK_TILE, D), lambda i: (i, 0)), pl.BlockSpec((1, D), lambda i: (0, 0))],
        out_specs=pl.BlockSpec((TOK_TILE, D), lambda i: (i, 0)),
        out_shape=jax.ShapeDtypeStruct((s, D), BF16), compiler_params=_params("arbitrary"))(x, g)


def _proj(h, wt, name):
    s, n = h.shape[0], wt.shape[0]

    def body(h_ref, w_ref, o_ref):
        o_ref[...] = _bdot_nt(h_ref[...], w_ref[...])

    return pl.pallas_call(
        body, name=name, grid=(s // TOK_TILE,),
        in_specs=[pl.BlockSpec((TOK_TILE, D), lambda i: (i, 0)), pl.BlockSpec((n, D), lambda i: (0, 0))],
        out_specs=pl.BlockSpec((TOK_TILE, n), lambda i: (i, 0)),
        out_shape=jax.ShapeDtypeStruct((s, n), F32), compiler_params=_params("arbitrary"))(h, wt)


def _proj_wgrad(h, du, name):
    s, n = du.shape

    def body(h_ref, du_ref, o_ref):
        @pl.when(pl.program_id(0) == 0)
        def _():
            o_ref[...] = jnp.zeros_like(o_ref)

        o_ref[...] += _bdot_tn(du_ref[...], h_ref[...])

    return pl.pallas_call(
        body, name=name, grid=(s // TOK_TILE,),
        in_specs=[pl.BlockSpec((TOK_TILE, D), lambda i: (i, 0)), pl.BlockSpec((TOK_TILE, n), lambda i: (i, 0))],
        out_specs=pl.BlockSpec((n, D), lambda i: (0, 0)),
        out_shape=jax.ShapeDtypeStruct((n, D), F32), compiler_params=_params("arbitrary"))(h, du)


def _proj_xgrad(x, g, dx2, dus, ws, slabs, owners):
    s = x.shape[0]
    tile = HEAD_TILE
    k = len(dus)
    nx = len(slabs)
    n_in = 3 + 2 * k + nx

    def body(*refs):
        x_ref, g_ref, dx2_ref = refs[:3]
        du_refs, w_refs = refs[3:3 + k], refs[3 + k:3 + 2 * k]
        src_refs = refs[3 + 2 * k:3 + 2 * k + nx]
        dx_ref, dg_ref = refs[n_in:n_in + 2]
        dst_refs = refs[n_in + 2:n_in + 2 + nx]
        start, wait = _exchange_ops(src_refs, dst_refs, owners, refs[n_in + 2 + nx:])

        @pl.when(pl.program_id(0) == 0)
        def _():
            dg_ref[...] = jnp.zeros_like(dg_ref)
            start()

        dh = _bdot(du_refs[0][...], w_refs[0][...])
        for du_ref, w_ref in zip(du_refs[1:], w_refs[1:]):
            dh += _bdot(du_ref[...], w_ref[...])
        xv = x_ref[...]
        rs = lax.rsqrt(jnp.mean(xv * xv, axis=-1, keepdims=True) + RMS_EPS)
        xn = xv * rs
        dg_ref[...] += jnp.sum(dh * xn, axis=0, keepdims=True)
        dxn = dh * g_ref[...]
        dx_ref[...] = rs * (dxn - xn * jnp.mean(dxn * xn, axis=-1, keepdims=True)) + dx2_ref[...]

        @pl.when(pl.program_id(0) == s // tile - 1)
        def _():
            wait()

    tok = lambda n: pl.BlockSpec((tile, n), lambda i: (i, 0))
    fixed = lambda a: pl.BlockSpec(a.shape, lambda i: (0,) * a.ndim)
    out = pl.pallas_call(
        body, name="proj_xgrad", grid=(s // tile,),
        in_specs=([tok(D), fixed(g), tok(D)] + [tok(du.shape[1]) for du in dus] + [fixed(w) for w in ws]
                  + _hbm_specs(nx)),
        out_specs=[tok(D), pl.BlockSpec((1, D), lambda i: (0, 0))] + _hbm_specs(nx),
        out_shape=[jax.ShapeDtypeStruct((s, D), F32), jax.ShapeDtypeStruct((1, D), F32)] + _received_shapes(slabs, owners),
        scratch_shapes=_exchange_scratch(nx),
        compiler_params=_params("arbitrary"))(x, g, dx2, *dus, *ws, *slabs)
    return out[0], out[1], out[2:]


def _tail(x, target, ya, o, ub, ug, w_oa, w_ob, w_o, fg):
    s = x.shape[0]
    tile = TOK_TILE

    def body(x_ref, t_ref, ya_ref, o_ref, gb_ref, ug_ref, woa_ref, wob_ref, wo_ref, fg_ref,
             loss_ref, dfg_ref, dwo_ref, dwoa_ref, dwob_ref, dx2_ref, dya_ref, do_ref, dgb_ref, dug_ref):
        @pl.when(pl.program_id(0) == 0)
        def _():
            for r in (loss_ref, dfg_ref, dwo_ref, dwoa_ref, dwob_ref):
                r[...] = jnp.zeros_like(r)

        ya_v = ya_ref[...]
        gate_b = gb_ref[...]
        sg_b = _sigmoid(gate_b)
        silu_b = gate_b * sg_b
        o_v = jnp.concatenate([o_ref[h] for h in range(H)], axis=-1)
        yb_v = o_v * silu_b
        big_a = _bdot(ya_v, woa_ref[...])
        big_b = _bdot(yb_v, wob_ref[...])
        sa = _sigmoid(ug_ref[:, :D])
        sb = _sigmoid(ug_ref[:, D:])
        merged = sa * big_a + sb * big_b
        x2 = x_ref[...] + _bdot(merged, wo_ref[...])
        rs = lax.rsqrt(jnp.mean(x2 * x2, axis=-1, keepdims=True) + RMS_EPS)
        xn = x2 * rs
        err = xn * fg_ref[...] - t_ref[...]
        loss_ref[...] += (0.5 / D) * jnp.sum(err * err)
        dout = err * (1.0 / D)
        dfg_ref[...] += jnp.sum(dout * xn, axis=0, keepdims=True)
        dxn = dout * fg_ref[...]
        dx2 = rs * (dxn - xn * jnp.mean(dxn * xn, axis=-1, keepdims=True))
        dx2_ref[...] = dx2
        dwo_ref[...] += _bdot_tn(merged, dx2)
        dmerged = _bdot_nt(dx2, wo_ref[...])
        dbig_a = dmerged * sa
        dbig_b = dmerged * sb
        dug_ref[:, :D] = dmerged * big_a * sa * (1.0 - sa)
        dug_ref[:, D:] = dmerged * big_b * sb * (1.0 - sb)
        dwoa_ref[...] += _bdot_tn(ya_v, dbig_a)
        dwob_ref[...] += _bdot_tn(yb_v, dbig_b)
        dya_ref[...] = _bdot_nt(dbig_a, woa_ref[...])
        dyb = _bdot_nt(dbig_b, wob_ref[...])
        dgb_ref[...] = dyb * o_v * (sg_b * (1.0 + gate_b * (1.0 - sg_b)))
        _dov = dyb * silu_b
        for h in range(H):
            do_ref[h] = _dov[:, N * h:N * (h + 1)]

    tok = lambda n: pl.BlockSpec((tile, n), lambda i: (i, 0))
    hm = pl.BlockSpec((H, tile, N), lambda i: (0, i, 0))
    fixed = lambda shape: pl.BlockSpec(shape, lambda i: (0,) * len(shape))
    f32 = lambda *shape: jax.ShapeDtypeStruct(shape, F32)
    return pl.pallas_call(
        body, name="tail", grid=(s // tile,),
        in_specs=[tok(D), tok(D), tok(DA), hm, pl.BlockSpec((tile, DA), lambda i: (i, 3)), tok(NG),
                  fixed((DA, D)), fixed((DA, D)), fixed((D, D)), fixed((1, D))],
        out_specs=[fixed((1, 1)), fixed((1, D)), fixed((D, D)), fixed((DA, D)), fixed((DA, D)),
                   tok(D), tok(DA), hm, tok(DA), tok(NG)],
        out_shape=[f32(1, 1), f32(1, D), f32(D, D), f32(DA, D), f32(DA, D),
                   f32(s, D), f32(s, DA), f32(H, s, N), f32(s, DA), f32(s, NG)],
        compiler_params=_params("arbitrary"))(x, target, ya, o, ub, ug, w_oa, w_ob, w_o, fg)


def _pre_operands(ua_ref, prev_ref, first):
    cur = ua_ref[...]
    t = cur.shape[0]
    prev_row = jnp.where(first, 0.0, prev_ref[7:8, :])
    rows = lax.broadcasted_iota(jnp.int32, cur.shape, 0)
    sh = jnp.where(rows == 0, prev_row, pltpu.roll(cur, 1, axis=0))
    ops = []
    for c0, n in ((0, DA), (DA, DA), (2 * DA, DA), (3 * DA + 2 * RANK, DA), (3 * DA, RANK), (3 * DA + RANK, RANK)):
        ops += [cur[:, c0:c0 + n], sh[:, c0:c0 + n]]
    del t
    return ops


def _ua_specs(tile, order):
    blocks = tile // 8
    return [pl.BlockSpec((tile, NA), lambda i: (order(i), 0)),
            pl.BlockSpec((8, NA), lambda i: (jnp.maximum(order(i) * blocks - 1, 0), 0))]


def _rwkv_pre_fwd(ua, pre_params):
    s = ua.shape[0]
    tile = HEAD_TILE

    def body(ua_ref, prev_ref, *refs):
        p_refs, o_refs = refs[:len(pre_params)], refs[len(pre_params):]
        ops = _pre_operands(ua_ref, prev_ref, pl.program_id(0) == 0)
        outs = _rwkv_pre(*ops, *[p[...] for p in p_refs])
        for o_ref, val in zip(o_refs, outs):
            o_ref[...] = val

    tm = pl.BlockSpec((tile, DA), lambda i: (i, 0))
    return pl.pallas_call(
        body, name="rwkv_pre_fwd", grid=(s // tile,),
        in_specs=_ua_specs(tile, lambda i: i) + [pl.BlockSpec(p.shape, lambda i, nd=p.ndim: (0,) * nd) for p in pre_params],
        out_specs=[tm] * 8, out_shape=[jax.ShapeDtypeStruct((s, DA), F32)] * 8,
        compiler_params=_params("arbitrary"))(ua, ua, *pre_params)


def _rwkv_pre_bwd(ua, pre_params, cots):
    s = ua.shape[0]
    tile = HEAD_TILE
    nt = s // tile
    n_p = len(pre_params)

    def body(ua_ref, prev_ref, *refs):
        p_refs, c_refs = refs[:n_p], refs[n_p:n_p + 11]
        dua_ref = refs[n_p + 11]
        dp_refs = refs[n_p + 12:n_p + 12 + n_p]
        carry_ref = refs[-1]
        i = pl.program_id(0)

        @pl.when(i == 0)
        def _():
            carry_ref[...] = jnp.zeros_like(carry_ref)
            for r in dp_refs:
                r[...] = jnp.zeros_like(r)

        ops = _pre_operands(ua_ref, prev_ref, i == nt - 1)
        _, vjp = jax.vjp(_rwkv_pre, *ops, *[p[...] for p in p_refs])
        c = [r[...] for r in c_refs]
        grads = vjp((c[0] + c[1], c[2], c[3], c[4] + c[5], c[6] + c[7], c[8], c[9], c[10]))
        d_ops, d_par = grads[:12], grads[12:]
        for r, val in zip(dp_refs, d_par):
            r[...] += val
        d_cur = jnp.concatenate([d_ops[0], d_ops[2], d_ops[4], d_ops[8], d_ops[10], d_ops[6]], axis=-1)
        d_sh = jnp.concatenate([d_ops[1], d_ops[3], d_ops[5], d_ops[9], d_ops[11], d_ops[7]], axis=-1)
        rows = lax.broadcasted_iota(jnp.int32, d_sh.shape, 0)
        dua_ref[...] = d_cur + jnp.where(rows == tile - 1, carry_ref[...], pltpu.roll(d_sh, tile - 1, axis=0))
        carry_ref[...] = d_sh[0:1, :]

    rev = lambda i: nt - 1 - i
    tm = pl.BlockSpec((tile, DA), lambda i: (rev(i), 0))
    fixed = [pl.BlockSpec(p.shape, lambda i, nd=p.ndim: (0,) * nd) for p in pre_params]
    return pl.pallas_call(
        body, name="rwkv_pre_bwd", grid=(nt,),
        in_specs=_ua_specs(tile, rev) + fixed + [tm] * 11,
        out_specs=[pl.BlockSpec((tile, NA), lambda i: (rev(i), 0))] + fixed,
        out_shape=[jax.ShapeDtypeStruct((s, NA), F32)] + [jax.ShapeDtypeStruct(p.shape, F32) for p in pre_params],
        scratch_shapes=[pltpu.VMEM((1, NA), F32)],
        compiler_params=_params("arbitrary"))(ua, ua, *pre_params, *cots)


def _wkv_fwd(seq):
    s = seq[0].shape[0]
    nc = s // WKV_CHUNK

    def body(r_ref, lw_ref, cl_ref, k_ref, v_ref, a_ref, b_ref, y_ref, ck_ref, p_ref, state):
        @pl.when(pl.program_id(0) == 0)
        def _():
            state[...] = jnp.zeros_like(state)

        r, lw, cl, k, v, a, b = (jnp.stack(_to_heads(ref[...])) for ref in (r_ref, lw_ref, cl_ref, k_ref, v_ref, a_ref,
                                                                             b_ref))
        s0 = state[...]
        ck_ref[0] = s0
        p = _tri_inverse(_wkv_aab(True, lw, cl, a, b))
        p_ref[0] = p
        y, s1 = _wkv_apply(True, s0, r, lw, cl, k, v, a, b, p)
        y_ref[...] = jnp.concatenate([y[h] for h in range(H)], axis=-1)
        state[...] = s1

    tm = pl.BlockSpec((WKV_CHUNK, DA), lambda c: (c, 0))
    per_chunk = lambda m: pl.BlockSpec((1, H, m, m), lambda c: (c, 0, 0, 0))
    return pl.pallas_call(
        body, name="wkv_fwd", grid=(nc,), in_specs=[tm] * 7,
        out_specs=[tm, per_chunk(N), per_chunk(WKV_CHUNK)],
        out_shape=[jax.ShapeDtypeStruct((s, DA), F32), jax.ShapeDtypeStruct((nc, H, N, N), F32),
                   jax.ShapeDtypeStruct((nc, H, WKV_CHUNK, WKV_CHUNK), F32)],
        scratch_shapes=[pltpu.VMEM((H, N, N), F32)], compiler_params=_params("arbitrary"))(*seq)


def _wkv_bwd(seq, ckpt, pinv, dy, slabs, owners):
    s = seq[0].shape[0]
    nc = s // WKV_CHUNK
    nx = len(slabs)

    def body(r_ref, lw_ref, cl_ref, k_ref, v_ref, a_ref, b_ref, ck_ref, p_ref, dy_ref, *refs):
        src_refs, d_refs, dst_refs = refs[:nx], refs[nx:nx + 7], refs[nx + 7:2 * nx + 7]
        dstate = refs[2 * nx + 7]
        start, wait = _exchange_ops(src_refs, dst_refs, owners, refs[2 * nx + 8:])

        @pl.when(pl.program_id(0) == 0)
        def _():
            dstate[...] = jnp.zeros_like(dstate)
            start()

        p = p_ref[0]
        r, lw, cl, k, v, a, b, dy = (jnp.stack(_to_heads(ref[...])) for ref in (r_ref, lw_ref, cl_ref, k_ref, v_ref,
                                                                                 a_ref, b_ref, dy_ref))
        _, vjp = jax.vjp(functools.partial(_wkv_apply, False), ck_ref[0], r, lw, cl, k, v, a, b, p)
        ds0, dr, dlw, dcl, dk, dv, da, db, dp = vjp((dy, dstate[...]))
        dstate[...] = ds0
        _, vjp_x = jax.vjp(functools.partial(_wkv_aab, False), lw, cl, a, b)
        dlw2, dcl2, da2, db2 = vjp_x(_dot1(_dot1(p, dp, "tn"), p, "nt"))
        for d_ref, val in zip(d_refs, (dr, dlw + dlw2, dcl + dcl2, dk, dv, da + da2, db + db2)):
            d_ref[...] = jnp.concatenate([val[h] for h in range(H)], axis=-1)

        @pl.when(pl.program_id(0) == nc - 1)
        def _():
            wait()

    tm = pl.BlockSpec((WKV_CHUNK, DA), lambda c: (nc - 1 - c, 0))
    per_chunk = lambda m: pl.BlockSpec((1, H, m, m), lambda c: (nc - 1 - c, 0, 0, 0))
    out = pl.pallas_call(
        body, name="wkv_bwd", grid=(nc,),
        in_specs=[tm] * 7 + [per_chunk(N), per_chunk(WKV_CHUNK), tm] + _hbm_specs(nx),
        out_specs=[tm] * 7 + _hbm_specs(nx),
        out_shape=[jax.ShapeDtypeStruct((s, DA), F32)] * 7 + _received_shapes(slabs, owners),
        scratch_shapes=[pltpu.VMEM((H, N, N), F32)] + _exchange_scratch(nx),
        compiler_params=_params("arbitrary"))(*seq, ckpt, pinv, dy, *slabs)
    return out[:7], out[7:]


def _rwkv_post_fwd(y, r, k2, v, g, post_params):
    s = y.shape[0]
    tile = TOK_TILE

    def body(*refs):
        refs[-1][...] = _rwkv_post(*[ref[...] for ref in refs[:-1]])

    tm = pl.BlockSpec((tile, DA), lambda i: (i, 0))
    par = pl.BlockSpec((1, DA), lambda i: (0, 0))
    return pl.pallas_call(
        body, name="rwkv_post_fwd", grid=(s // tile,), in_specs=[tm] * 5 + [par] * 3,
        out_specs=tm, out_shape=jax.ShapeDtypeStruct((s, DA), F32),
        compiler_params=_params("arbitrary"))(y, r, k2, v, g, *post_params)


def _rwkv_post_bwd(y, r, k2, v, g, post_params, dya, slabs, lo):
    s = y.shape[0]
    tile = HEAD_TILE

    def body(y_ref, r_ref, k_ref, v_ref, g_ref, w_ref, b_ref, rk_ref, dya_ref, s_ref, *refs):
        d_refs, p_ref = refs[:8], refs[8]
        start, wait = _pair_swap_ops(s_ref, p_ref, lo, refs[9:])

        @pl.when(pl.program_id(0) == 0)
        def _():
            for ref in d_refs[5:]:
                ref[...] = jnp.zeros_like(ref)
            start()

        _, vjp = jax.vjp(_rwkv_post, *[ref[...] for ref in (y_ref, r_ref, k_ref, v_ref, g_ref, w_ref, b_ref, rk_ref)])
        grads = vjp(dya_ref[...])
        for ref, val in zip(d_refs[:5], grads[:5]):
            ref[...] = val
        for ref, val in zip(d_refs[5:], grads[5:]):
            ref[...] += val

        @pl.when(pl.program_id(0) == s // tile - 1)
        def _():
            wait()

    tm = pl.BlockSpec((tile, DA), lambda i: (i, 0))
    par = pl.BlockSpec((1, DA), lambda i: (0, 0))
    return pl.pallas_call(
        body, name="rwkv_post_bwd", grid=(s // tile,),
        in_specs=[tm] * 5 + [par] * 3 + [tm] + _hbm_specs(1),
        out_specs=[tm] * 5 + [par] * 3 + _hbm_specs(1),
        out_shape=[jax.ShapeDtypeStruct((s, DA), F32)] * 5 + [jax.ShapeDtypeStruct((1, DA), F32)] * 3
        + [jax.ShapeDtypeStruct(slabs.shape, slabs.dtype)],
        scratch_shapes=_pair_swap_scratch(slabs.shape[0]),
        compiler_params=_params("arbitrary"))(y, r, k2, v, g, *post_params, dya, slabs)


def _tri(t):
    return (lax.broadcasted_iota(jnp.int32, (t, t), 0) >= lax.broadcasted_iota(jnp.int32, (t, t), 1)).astype(F32)


def _fox_pre_fwd(ub, uf, q_g, k_g, f_b):
    s = ub.shape[0]
    tile = HEAD_TILE

    def body(ub_ref, uf_ref, qg_ref, kg_ref, fb_ref, q_ref, k_ref, v_ref, cum_ref, carry):
        @pl.when(pl.program_id(0) == 0)
        def _():
            carry[...] = jnp.zeros_like(carry)

        qn, kn, logf = _fox_pre(ub_ref[:, :DA], ub_ref[:, DA:2 * DA], uf_ref[...], qg_ref[...], kg_ref[...],
                                fb_ref[...])
        for h, (q_col, k_col) in enumerate(zip(_to_heads(qn), _to_heads(kn))):
            q_ref[h] = q_col
            k_ref[h] = k_col
        v_ref[...] = _heads(ub_ref, 2 * DA)
        cum = jnp.dot(_tri(tile), logf, precision=HI, preferred_element_type=F32) + carry[...]
        cum_ref[...] = cum
        carry[...] = cum[tile - 1:tile, :]

    hm = pl.BlockSpec((H, tile, N), lambda i: (0, i, 0))
    fixed = lambda shape: pl.BlockSpec(shape, lambda i: (0,) * len(shape))
    return pl.pallas_call(
        body, name="fox_pre_fwd", grid=(s // tile,),
        in_specs=[pl.BlockSpec((tile, NB), lambda i: (i, 0)), pl.BlockSpec((tile, NF), lambda i: (i, 0)),
                  fixed((1, DA)), fixed((1, DA)), fixed((1, NF))],
        out_specs=[hm] * 3 + [pl.BlockSpec((tile, NF), lambda i: (i, 0))],
        out_shape=[jax.ShapeDtypeStruct((H, s, N), F32)] * 3 + [jax.ShapeDtypeStruct((s, NF), F32)],
        scratch_shapes=[pltpu.VMEM((1, NF), F32)], compiler_params=_params("arbitrary"))(ub, uf, q_g, k_g, f_b)


def _fox_pre_bwd(ub, uf, q_g, k_g, f_b, dqn, dkn, dvf, dgate, dcum_q, dcum_k):
    s = ub.shape[0]
    tile = HEAD_TILE
    nt = s // tile

    def body(ub_ref, uf_ref, qg_ref, kg_ref, fb_ref, dq_ref, dk_ref, dv_ref, dgate_ref, dcq_ref, dck_ref,
             dub_ref, duf_ref, dqg_ref, dkg_ref, dfb_ref, carry):
        @pl.when(pl.program_id(0) == 0)
        def _():
            carry[...] = jnp.zeros_like(carry)
            for ref in (dqg_ref, dkg_ref, dfb_ref):
                ref[...] = jnp.zeros_like(ref)

        dcum = dcq_ref[...] + dck_ref[...]
        dlogf = lax.dot_general(_tri(tile), dcum, (((0,), (0,)), ((), ())), precision=HI,
                                preferred_element_type=F32) + carry[...]
        carry[...] = dlogf[0:1, :]
        _, vjp = jax.vjp(_fox_pre, ub_ref[:, :DA], ub_ref[:, DA:2 * DA], uf_ref[...], qg_ref[...], kg_ref[...],
                         fb_ref[...])
        d_q, d_k, d_f, d_qg, d_kg, d_fb = vjp((_from_heads(dq_ref), _from_heads(dk_ref), dlogf))
        dub_ref[:, :DA] = d_q
        dub_ref[:, DA:2 * DA] = d_k
        _store_heads(dub_ref, 2 * DA, dv_ref[...])
        dub_ref[:, 3 * DA:] = dgate_ref[...]
        duf_ref[...] = d_f
        dqg_ref[...] += functools.reduce(jnp.add, _to_heads(d_qg))
        dkg_ref[...] += functools.reduce(jnp.add, _to_heads(d_kg))
        dfb_ref[...] += d_fb

    rev = lambda i: nt - 1 - i
    hm = pl.BlockSpec((H, tile, N), lambda i: (0, rev(i), 0))
    tok = lambda n: pl.BlockSpec((tile, n), lambda i: (rev(i), 0))
    fixed = lambda shape: pl.BlockSpec(shape, lambda i: (0,) * len(shape))
    return pl.pallas_call(
        body, name="fox_pre_bwd", grid=(nt,),
        in_specs=[tok(NB), tok(NF), fixed((1, DA)), fixed((1, DA)), fixed((1, NF)), hm, hm, hm, tok(DA), tok(NF),
                  tok(NF)],
        out_specs=[tok(NB), tok(NF), fixed((1, N)), fixed((1, N)), fixed((1, NF))],
        out_shape=[jax.ShapeDtypeStruct((s, NB), F32), jax.ShapeDtypeStruct((s, NF), F32),
                   jax.ShapeDtypeStruct((1, N), F32), jax.ShapeDtypeStruct((1, N), F32),
                   jax.ShapeDtypeStruct((1, NF), F32)],
        scratch_shapes=[pltpu.VMEM((1, NF), F32)],
        compiler_params=_params("arbitrary"))(ub, uf, q_g, k_g, f_b, dqn, dkn, dvf, dgate, dcum_q, dcum_k)


def _att_groups(s):
    blocks = s // ATT_TILE
    per = max(1, blocks // ATT_GROUPS)
    return per, blocks // per


def _att_parts(n, width):
    return ([(0, n - width, False)] if n > width else []) + [(n - width, n, True)]


def _att_scores(q_bf, k_ref, ck_ref, lo, hi, masked, row_offset):
    scores = _bdot_nt(q_bf, k_ref[0, lo:hi, :]) - ck_ref[0, :, lo:hi]
    if masked:
        rows = row_offset + lax.broadcasted_iota(jnp.int32, scores.shape, 0)
        scores = jnp.where(rows >= lax.broadcasted_iota(jnp.int32, scores.shape, 1), scores, -1e30)
    return scores


def _fox_attn_fwd(q, k, v, cum_q, cum_k):
    s = q.shape[1]
    t = ATT_TILE
    per, groups = _att_groups(s)

    def body(q_ref, k_ref, v_ref, cq_ref, ck_ref, o_ref, lse_ref):
        qi = pl.program_id(1)
        for g in range(groups):
            @pl.when(qi // per == g)
            def _(g=g):
                q_bf = (q_ref[0] * ATT_SCALE).astype(BF16)
                parts = _att_parts((g + 1) * per * t, per * t)
                scores = [_att_scores(q_bf, k_ref, ck_ref, lo, hi, masked, (qi - g * per) * t)
                          for lo, hi, masked in parts]
                m = functools.reduce(jnp.maximum, [jnp.max(sc, axis=-1, keepdims=True) for sc in scores])
                l, acc = 0.0, 0.0
                for sc, (lo, hi, _) in zip(scores, parts):
                    p = jnp.exp(sc - m)
                    l += jnp.sum(p, axis=-1, keepdims=True)
                    acc += _bdot(p, v_ref[0, lo:hi, :])
                o_ref[0] = acc / l
                lse_ref[0] = m + jnp.log(l) + cq_ref[0]

    qb = pl.BlockSpec((1, t, N), lambda h, i: (h, i, 0))
    kb = pl.BlockSpec((1, s, N), lambda h, i: (h, 0, 0))
    return pl.pallas_call(
        body, name="fox_attn_fwd", grid=(H, s // t),
        in_specs=[qb, kb, kb, pl.BlockSpec((1, t, 1), lambda h, i: (h, i, 0)),
                  pl.BlockSpec((1, 1, s), lambda h, i: (h, 0, 0))],
        out_specs=[qb, pl.BlockSpec((1, t, 1), lambda h, i: (h, i, 0))],
        out_shape=[jax.ShapeDtypeStruct((H, s, N), F32), jax.ShapeDtypeStruct((H, s, 1), F32)],
        compiler_params=_params("arbitrary", "arbitrary"))(q, k, v, cum_q, cum_k)


def _fox_attn_bwd(q, k, v, cum_q, cum_k, o, lse, do, slabs, owners):
    s = q.shape[1]
    t = ATT_TILE
    per, groups = _att_groups(s)
    nx = len(slabs)

    def body(q_ref, k_ref, v_ref, cq_ref, ck_ref, o_ref, lse_ref, do_ref, *refs):
        src_refs, (dq_ref, dk_ref, dv_ref, dcq_ref, dck_ref) = refs[:nx], refs[nx:nx + 5]
        start, wait = _exchange_ops(src_refs, refs[nx + 5:2 * nx + 5], owners, refs[2 * nx + 5:])
        qi = pl.program_id(1)

        @pl.when((pl.program_id(0) == 0) & (qi == 0))
        def _():
            start()

        @pl.when(qi == 0)
        def _():
            for ref in (dk_ref, dv_ref, dck_ref):
                ref[...] = jnp.zeros_like(ref)

        for g in range(groups):
            @pl.when(qi // per == g)
            def _(g=g):
                q_bf, do_bf = (q_ref[0] * ATT_SCALE).astype(BF16), do_ref[0].astype(BF16)
                row_term = cq_ref[0] - lse_ref[0]
                delta = jnp.sum(do_ref[0] * o_ref[0], axis=-1, keepdims=True)
                dq, dcq = 0.0, 0.0
                for lo, hi, masked in _att_parts((g + 1) * per * t, per * t):
                    p = jnp.exp(_att_scores(q_bf, k_ref, ck_ref, lo, hi, masked, (qi - g * per) * t) + row_term)
                    ds = p * (_bdot_nt(do_bf, v_ref[0, lo:hi, :]) - delta)
                    dq += _bdot(ds, k_ref[0, lo:hi, :])
                    dcq += jnp.sum(ds, axis=-1, keepdims=True)
                    dk_ref[0, lo:hi, :] += _bdot_tn(ds, q_bf)
                    dv_ref[0, lo:hi, :] += _bdot_tn(p, do_bf)
                    dck_ref[0, :, lo:hi] -= jnp.sum(ds, axis=0, keepdims=True)
                dq_ref[0] = dq * ATT_SCALE
                dcq_ref[0] = dcq

        @pl.when((pl.program_id(0) == H - 1) & (qi == s // t - 1))
        def _():
            wait()

    qb = pl.BlockSpec((1, t, N), lambda h, i: (h, i, 0))
    kb = pl.BlockSpec((1, s, N), lambda h, i: (h, 0, 0))
    cqb = pl.BlockSpec((1, t, 1), lambda h, i: (h, i, 0))
    ckb = pl.BlockSpec((1, 1, s), lambda h, i: (h, 0, 0))
    f32 = lambda *shape: jax.ShapeDtypeStruct(shape, F32)
    out = pl.pallas_call(
        body, name="fox_attn_bwd", grid=(H, s // t),
        in_specs=[qb, kb, kb, cqb, ckb, qb, cqb, qb] + _hbm_specs(nx), out_specs=[qb, kb, kb, cqb, ckb] + _hbm_specs(nx),
        out_shape=[f32(H, s, N), f32(H, s, N), f32(H, s, N), f32(H, s, 1), f32(H, 1, s)]
        + _received_shapes(slabs, owners),
        scratch_shapes=_exchange_scratch(nx),
        compiler_params=_params("arbitrary", "arbitrary"))(q, k, v, cum_q, cum_k, o, lse, do, *slabs)
    return out[:5], out[5:]


def _local_step(x, target, w, p):
    mu = p["shift_mu"]
    lora_matrix = lambda a: jnp.moveaxis(a, 0, 1).reshape(RANK, DA).astype(F32)
    pre_params = (mu[:, 0:DA], mu[:, DA:2 * DA], mu[:, 2 * DA:3 * DA], mu[:, 3 * DA + 2 * RANK:],
                  mu[:, 3 * DA:3 * DA + RANK], mu[:, 3 * DA + RANK:3 * DA + 2 * RANK],
                  lora_matrix(w["w_lora_up"]), p["w0"], lora_matrix(w["a_lora_up"]), p["a0"], p["k_k"], p["k_a"])
    post_params = (p["lnx_w"], p["lnx_b"], p["r_k"])
    q_g, k_g = jnp.tile(p["q_norm_g"], (1, H)), jnp.tile(p["k_norm_g"], (1, H))
    f_b = jnp.pad(p["f_bias"], ((0, 0), (0, NF - H)))
    fg = p["final_norm_g"].reshape(1, D)

    h = _rms_fwd(x, p["norm_g"])
    ua = _proj(h, w["in_a"], "proj_a")
    ub = _proj(h, w["in_b"], "proj_b")
    ug = _proj(h, w["in_g"], "proj_g")
    uf = _proj(h, w["in_f"], "proj_f")
    r, lw, cl, k2, v, av, bv, gg = _rwkv_pre_fwd(ua, pre_params)
    y, ckpt, pinv = _wkv_fwd((r, lw, cl, k2, v, av, bv))
    ya = _rwkv_post_fwd(y, r, k2, v, gg, post_params)
    qn, kn, vf, cum = _fox_pre_fwd(ub, uf, q_g, k_g, f_b)
    cum_t = cum[:, :H].T
    cum_q, cum_k = cum_t[:, :, None], cum_t[:, None, :]
    o, lse = _fox_attn_fwd(qn, kn, vf, cum_q, cum_k)

    (loss, dfg, dwo, dwoa, dwob, dx2, dya, do, dgate_b, dug) = _tail(
        x, target, ya, o, ub, ug, w["w_out_a"], w["w_out_b"], w["w_out"], fg)
    everyone = (0, N_DEV)
    (dqn, dkn, dvf, dcq, dck), (recv_woa, recv_wob, recv_wo) = _fox_attn_bwd(
        qn, kn, vf, cum_q, cum_k, o, lse, do,
        (_col_slabs(dwoa), _col_slabs(dwob), dwo.astype(BF16).reshape(N_DEV, D // N_DEV, D)), (everyone,) * 3)
    pad_f = lambda a: jnp.pad(a.T, ((0, 0), (0, NF - H)))
    dub, duf, dqg, dkg, dfb = _fox_pre_bwd(ub, uf, q_g, k_g, f_b, dqn, dkn, dvf, dgate_b,
                                           pad_f(dcq[:, :, 0]), pad_f(dck.reshape(H, -1)))
    dwt_b, dwt_g, dwt_f = (_proj_wgrad(h, du, name) for du, name in ((dub, "wgrad_b"), (dug, "wgrad_g"), (duf, "wgrad_f")))
    early = _slab_wt_grad((dwt_b, dwt_g, dwt_f), (_WT_SEGMENTS[1], _WT_SEGMENTS[2], _WT_SEGMENTS[3]), EARLY_FROM, N_DEV,
                          "slab_wt_early")
    dy, dr_p, dk_p, dv_p, dgg, dlnw, dlnb, drk, handed = _rwkv_post_bwd(y, r, k2, v, gg, post_params, dya, early,
                                                                          EARLY_FROM)
    early = _chip_sums(early, handed, EARLY_FROM, "chip_sums_early")
    (dr_s, dlw, dcl, dk_s, dv_s, dav, dbv), (recv_early,) = _wkv_bwd(
        (r, lw, cl, k2, v, av, bv), ckpt, pinv, dy, (early,), ((EARLY_FROM, N_DEV, "chips"),))
    pre_out = _rwkv_pre_bwd(ua, pre_params, (dr_s, dr_p, dlw, dcl, dk_s, dk_p, dv_s, dv_p, dav, dbv, dgg))
    dua, dpre = pre_out[0], pre_out[1:]
    dwt_a = _proj_wgrad(h, dua, "wgrad_a")

    flat = lambda a: a.reshape(1, -1)
    small = {
        "final_norm_g": dfg, "w0": dpre[7], "a0": dpre[9], "k_k": dpre[10], "k_a": dpre[11], "r_k": drk, "lnx_w": dlnw,
        "lnx_b": dlnb, "q_norm_g": dqg, "k_norm_g": dkg, "f_bias": dfb[:, :H],
        "shift_mu": jnp.concatenate([flat(dpre[0]), flat(dpre[1]), flat(dpre[2]), dpre[4], dpre[5], flat(dpre[3])], axis=1),
    }
    late = _slab_wt_grad((dwt_a, dwt_b), (_WT_SEGMENTS[0], _WT_SEGMENTS[1]), 0, EARLY_FROM, "slab_wt_late")
    late = _chip_sums(late, _pair_swap(late, 0, "pair_swap_late"), 0, "chip_sums_late")
    by_head = lambda a: jnp.moveaxis(a.reshape(RANK, H, N), 1, 0)
    loras = jnp.stack([by_head(dpre[6]), by_head(dpre[8])], axis=1).astype(BF16)
    dx, dng, (recv_late, recv_lora, recv_small) = _proj_xgrad(
        x, p["norm_g"], dx2, (dua, dub, dug, duf), (w["in_a"], w["in_b"], w["in_g"], w["in_f"]),
        (late, loras, _pack_small(small, loss)), ((0, EARLY_FROM, "chips"), everyone, everyone))
    return dx, dng, (recv_early, recv_late), (recv_woa, recv_wob, recv_wo, recv_lora), recv_small


def _position():
    return lax.axis_index("x"), lax.axis_index("y"), lax.axis_index("c")


def _hbm_specs(n):
    return [pl.BlockSpec(memory_space=pl.ANY)] * n


BIG_GATHER_COPIES = 13


def _all_gather(big, blocks, name):
    n = len(blocks)

    def body(*refs):
        big_ref, x_refs = refs[0], refs[1:1 + n]
        big_out, out_refs = refs[1 + n], refs[2 + n:2 + 2 * n]
        send_sems, recv_sems, local_sems = refs[2 + 2 * n:]
        x, y, c = _position()
        me, sibling = (x, y, c), (x, y, 1 - c)
        chips = [(1 - x, y), (x, 1 - y), (1 - x, 1 - y)]
        x_nbr, y_nbr, diag = chips
        cols = big_ref.shape[1] // 2

        def part(ref, h):
            return ref if h is None else ref.at[:, pl.ds(h * cols, cols)]

        def landed(chip, core, h):
            return part(big_out.at[4 * chip[0] + 2 * chip[1] + core], h)

        def big_copy(k, src, dst, to):
            return pltpu.make_async_remote_copy(src_ref=src, dst_ref=dst, send_sem=send_sems.at[7 * n + k],
                                                recv_sem=recv_sems.at[7 * n + k], device_id=to, device_id_type=MESH)

        def arrival(k, chip, core, h):
            dst = landed(chip, core, h)
            return big_copy(k, dst, dst, me)

        def pass_on(k, chip, h, to):
            src = landed(chip, c, h)
            return big_copy(k, src, src, to)

        big_mine = pltpu.make_async_copy(big_ref, landed((x, y), c, None), local_sems.at[n])
        big_mine.start()
        here = (x, y)
        big_sent = [big_copy(0, big_ref, landed(here, c, None), sibling),
                    big_copy(1, part(big_ref, 0), landed(here, c, 0), (*x_nbr, c)),
                    big_copy(2, part(big_ref, 1), landed(here, c, 1), (*y_nbr, c)),
                    big_copy(3, part(big_ref, 1), landed(here, c, 1), (*x_nbr, c)),
                    big_copy(4, part(big_ref, 0), landed(here, c, 0), (*y_nbr, c))]
        for cp in big_sent:
            cp.start()

        def copy(a, k, blk, to, own=False):
            dst = out_refs[a].at[4 * blk[0] + 2 * blk[1] + blk[2]]
            return pltpu.make_async_remote_copy(
                src_ref=x_refs[a] if own else dst, dst_ref=dst, send_sem=send_sems.at[7 * a + k],
                recv_sem=recv_sems.at[7 * a + k], device_id=to, device_id_type=MESH)

        mine = [pltpu.make_async_copy(x_refs[a], out_refs[a].at[4 * x + 2 * y + c], local_sems.at[a]) for a in range(n)]
        for cp in mine:
            cp.start()
        first = []
        for a in range(n):
            first.append(copy(a, 0, me, sibling, own=True))
            first += [copy(a, 1 + j, me, (*chip, c), own=True) for j, chip in enumerate(chips)]
        for cp in first:
            cp.start()

        big_steps = [(1, x_nbr, 0, (*y_nbr, c), 5, 7), (2, y_nbr, 1, (*x_nbr, c), 6, 8), (3, x_nbr, 1, None, None, 9),
                     (4, y_nbr, 0, None, None, 10), (5, diag, 0, None, None, 11), (6, diag, 1, None, None, 12)]
        for k, chip, h, onward, k_onward, k_sibling in big_steps:
            arrival(k, chip, c, h).wait_recv()
            if onward is not None:
                big_sent.append(pass_on(k_onward, chip, h, onward))
                big_sent[-1].start()
            big_sent.append(pass_on(k_sibling, chip, h, sibling))
            big_sent[-1].start()

        passed = []
        for j, chip in enumerate(chips):
            for a in range(n):
                copy(a, 1 + j, (*chip, c), me).wait_recv()
                passed.append(copy(a, 4 + j, (*chip, c), sibling))
                passed[-1].start()
        for a in range(n):
            copy(a, 0, sibling, me).wait_recv()
        for j, chip in enumerate(chips):
            for a in range(n):
                copy(a, 4 + j, (*chip, 1 - c), me).wait_recv()
        arrival(0, here, 1 - c, None).wait_recv()
        for k, chip, h, _, _, k_sibling in big_steps:
            arrival(k_sibling, chip, 1 - c, h).wait_recv()
        for cp in first + passed + big_sent:
            cp.wait_send()
        for cp in mine + [big_mine]:
            cp.wait()

    everything = [big] + list(blocks)
    return pl.pallas_call(
        body, name=name, out_shape=[jax.ShapeDtypeStruct((N_DEV,) + b.shape, b.dtype) for b in everything],
        in_specs=_hbm_specs(n + 1), out_specs=_hbm_specs(n + 1),
        scratch_shapes=[pltpu.SemaphoreType.DMA((7 * n + BIG_GATHER_COPIES,)),
                        pltpu.SemaphoreType.DMA((7 * n + BIG_GATHER_COPIES,)), pltpu.SemaphoreType.DMA((n + 1,))],
    )(*everything)


def _received_shapes(slabs, owners):
    return [jax.ShapeDtypeStruct((N_DEV // 2 if len(o) == 3 else N_DEV,) + s.shape[1:], s.dtype)
            for s, o in zip(slabs, owners)]


def _pair_swap_scratch(n):
    return [pltpu.SemaphoreType.DMA((n,)), pltpu.SemaphoreType.DMA((n,))]


def _pair_swap_ops(s_ref, p_ref, lo, sems):
    send_sems, recv_sems = sems
    n = s_ref.shape[0]

    def run(sending):
        x, y, c = _position()
        for side in (0, 1):
            mine = [pltpu.make_async_remote_copy(src_ref=s_ref.at[i], dst_ref=p_ref.at[i], send_sem=send_sems.at[i],
                                                 recv_sem=recv_sems.at[i], device_id=(x, y, 1 - c), device_id_type=MESH)
                    for i in range(n) if (lo + i) % 2 == side]

            @pl.when(c != side)
            def _():
                for cp in mine:
                    cp.start() if sending else cp.wait_send()

            if not sending:
                @pl.when(c == side)
                def _():
                    for cp in mine:
                        cp.wait_recv()

    return functools.partial(run, True), functools.partial(run, False)


def _pair_swap(slabs, lo, name):
    n = slabs.shape[0]

    def body(s_ref, p_ref, *sems):
        start, wait = _pair_swap_ops(s_ref, p_ref, lo, sems)
        start()
        wait()

    return pl.pallas_call(
        body, name=name, out_shape=jax.ShapeDtypeStruct(slabs.shape, slabs.dtype),
        in_specs=_hbm_specs(1), out_specs=_hbm_specs(1)[0], scratch_shapes=_pair_swap_scratch(n))(slabs)


def _chip_sums(slabs, swapped, lo, name):
    n, rows, cols = slabs.shape
    tile = W_IN_COL_TILE

    def body(s_ref, p_ref, o_ref):
        c = lax.axis_index("c")
        for i in range(n):
            @pl.when(c == (lo + i) % 2)
            def _(i=i):
                o_ref[i] = (s_ref[i].astype(F32) + p_ref[i].astype(F32)).astype(BF16)

    blk = pl.BlockSpec((n, rows, tile), lambda j: (0, 0, j))
    return pl.pallas_call(
        body, name=name, grid=(cols // tile,), in_specs=[blk, blk], out_specs=blk,
        out_shape=jax.ShapeDtypeStruct(slabs.shape, BF16), compiler_params=_params("arbitrary"))(slabs, swapped)


def _exchange_scratch(n):
    return [pltpu.SemaphoreType.DMA((7 * n,)), pltpu.SemaphoreType.DMA((7 * n,)), pltpu.SemaphoreType.DMA((n,))]


def _exchange_ops(src_refs, dst_refs, owners, sems):
    send_sems, recv_sems, local_sems = sems
    n = len(src_refs)

    def guarded(a, dev, fn):
        lo, hi = owners[a][:2]
        if (lo, hi) == (0, N_DEV):
            fn()
        else:
            pl.when((dev >= lo) & (dev < hi))(fn)

    def src(a, dev):
        ref = src_refs[a]
        return ref.at[0] if ref.shape[0] == 1 else ref.at[dev - owners[a][0]]

    def run(sending, waiting):
        x, y, c = _position()
        me = 4 * x + 2 * y + c
        for a in range(n):
            by_chip = len(owners[a]) == 3
            slot = (lambda qx, qy, qc: 2 * qx + qy) if by_chip else (lambda qx, qy, qc: 4 * qx + 2 * qy + qc)
            mine = slot(x, y, c)
            local = lambda a=a, mine=mine: pltpu.make_async_copy(src(a, me), dst_refs[a].at[mine], local_sems.at[a])
            if sending:
                guarded(a, me, lambda local=local: local().start())
            for m in range(2, N_DEV, 2) if by_chip else range(1, N_DEV):
                px, py, pc = x ^ (m >> 2), y ^ ((m >> 1) & 1), c ^ (m & 1)
                peer = 4 * px + 2 * py + pc
                theirs = slot(px, py, pc)
                sem = dict(send_sem=send_sems.at[7 * a + m - 1], recv_sem=recv_sems.at[7 * a + m - 1],
                           device_id=(px, py, pc), device_id_type=MESH)
                send = lambda a=a, peer=peer, sem=sem, mine=mine: pltpu.make_async_remote_copy(
                    src_ref=src(a, peer), dst_ref=dst_refs[a].at[mine], **sem)
                recv = lambda a=a, sem=sem, theirs=theirs: pltpu.make_async_remote_copy(
                    src_ref=src(a, me), dst_ref=dst_refs[a].at[theirs], **sem)
                if sending:
                    guarded(a, peer, lambda send=send: send().start())
                if waiting:
                    guarded(a, me, lambda recv=recv: recv().wait_recv())
                    guarded(a, peer, lambda send=send: send().wait_send())
            if waiting:
                guarded(a, me, lambda local=local: local().wait())

    return functools.partial(run, True, False), functools.partial(run, False, True)


def _sum_slabs(r_ref):
    g = r_ref[0].astype(F32)
    for k in range(1, r_ref.shape[0]):
        g = g + r_ref[k].astype(F32)
    return g


def _adamw(g, w, m, v):
    m_new = ADAM_B1 * m + (1.0 - ADAM_B1) * g
    v_new = ADAM_B2 * v + (1.0 - ADAM_B2) * (g * g)
    m_hat = m_new / (1.0 - ADAM_B1 ** ADAM_STEP)
    v_hat = v_new / (1.0 - ADAM_B2 ** ADAM_STEP)
    return g, -ADAM_LR * (m_hat / (jnp.sqrt(v_hat) + ADAM_EPS) + ADAM_WD * w), m_new, v_new


def _adamw_w_in(recv_early, recv_late, w, m, v, slabs, owners):
    rows, cols = w.shape
    tile = W_IN_COL_TILE
    nx = len(slabs)

    def body(early_ref, late_ref, w_ref, m_ref, v_ref, *refs):
        src_refs, o_refs, dst_refs = refs[:nx], refs[nx:nx + 4], refs[nx + 4:2 * nx + 4]
        start, wait = _exchange_ops(src_refs, dst_refs, owners, refs[2 * nx + 4:])
        x, y, c = _position()
        early_owner = 4 * x + 2 * y + c >= EARLY_FROM

        @pl.when(pl.program_id(0) == 0)
        def _():
            start()

        def update(g):
            for o_ref, val in zip(o_refs, _adamw(g, w_ref[...], m_ref[...], v_ref[...])):
                o_ref[...] = val

        pl.when(early_owner)(lambda: update(_sum_slabs(early_ref)))
        pl.when(jnp.logical_not(early_owner))(lambda: update(_sum_slabs(late_ref)))

        @pl.when(pl.program_id(0) == cols // tile - 1)
        def _():
            wait()

    blk = pl.BlockSpec((rows, tile), lambda i: (0, i))
    slots = lambda r: pl.BlockSpec((r.shape[0], rows, tile), lambda i: (0, 0, i))
    out = pl.pallas_call(
        body, name="adamw_w_in", grid=(cols // tile,),
        in_specs=[slots(recv_early), slots(recv_late), blk, blk, blk] + _hbm_specs(nx),
        out_specs=[blk] * 4 + _hbm_specs(nx),
        out_shape=[jax.ShapeDtypeStruct((rows, cols), F32)] * 4 + _received_shapes(slabs, owners),
        scratch_shapes=_exchange_scratch(nx),
        compiler_params=_params("arbitrary"))(recv_early, recv_late, w, m, v, *slabs)
    return out[:4], out[4:]


def _adamw_misc(recvs, recv_small, recv_norm, params):
    names = list(params)
    flat = [a for n in names for a in params[n]]

    def body(woa_ref, wob_ref, wo_ref, lora_ref, small_ref, norm_ref, *refs):
        p_refs, o_refs = refs[:len(flat)], refs[len(flat):]
        g_small = _sum_slabs(small_ref)
        g_lora = _sum_slabs(lora_ref)
        grads = {"w_out_a": _sum_slabs(woa_ref), "w_out_b": _sum_slabs(wob_ref), "w_out": _sum_slabs(wo_ref),
                 "w_lora_up": g_lora[0], "a_lora_up": g_lora[1], "norm_g": _sum_slabs(norm_ref)}
        for n, (off, size) in SMALL_SLOTS.items():
            grads[n] = g_small[:, off:off + size]
        for i, n in enumerate(names):
            w_ref, m_ref, v_ref = p_refs[3 * i:3 * i + 3]
            for o_ref, val in zip(o_refs[4 * i:4 * i + 4], _adamw(grads[n], w_ref[...], m_ref[...], v_ref[...])):
                o_ref[...] = val
        o_refs[-1][...] = g_small[:, LOSS_SLOT:LOSS_SLOT + 1]

    out = pl.pallas_call(
        body, name="adamw_misc",
        out_shape=[jax.ShapeDtypeStruct(params[n][0].shape, F32) for n in names for _ in range(4)]
        + [jax.ShapeDtypeStruct((1, 1), F32)],
        compiler_params=_params())(*recvs, recv_small, recv_norm, *flat)
    return {n: out[4 * i:4 * i + 4] for i, n in enumerate(names)}, out[-1]


_WT_SEGMENTS = ((0, NA), (NA, NB), (NA + NB + H, NG), (NA + NB, H))


def _split_wt(gathered):
    tile = W_IN_COL_TILE

    def body(g_ref, *o_refs):
        full = jnp.concatenate([g_ref[j] for j in range(N_DEV)], axis=0)
        for o_ref, (row, n) in zip(o_refs, _WT_SEGMENTS):
            seg = full[row:row + n]
            if n < o_ref.shape[0]:
                seg = jnp.concatenate([seg, jnp.zeros((o_ref.shape[0] - n, tile), BF16)], axis=0)
            o_ref[...] = seg

    sizes = (NA, NB, NG, NF)
    return pl.pallas_call(
        body, name="split_wt", grid=(D // tile,),
        in_specs=[pl.BlockSpec((N_DEV, COLS_PER_DEV, tile), lambda i: (0, 0, i))],
        out_specs=[pl.BlockSpec((n, tile), lambda i: (0, i)) for n in sizes],
        out_shape=[jax.ShapeDtypeStruct((n, D), BF16) for n in sizes],
        compiler_params=_params("arbitrary"))(gathered)


def _slab_wt_grad(segments, seg_rows, dev_lo, dev_hi, name):
    tile = W_IN_COL_TILE
    k = len(segments)

    def body(*refs):
        seg_refs, o_ref = refs[:k], refs[k]
        for j in range(dev_lo, dev_hi):
            lo, hi = COLS_PER_DEV * j, COLS_PER_DEV * (j + 1)
            parts = []
            for ref, (row, n) in sorted(zip(seg_refs, seg_rows), key=lambda t: t[1][0]):
                first, last = max(lo, row), min(hi, row + n)
                if first < last:
                    parts.append(ref[first - row:last - row, :])
            o_ref[j - dev_lo] = (parts[0] if len(parts) == 1 else jnp.concatenate(parts, axis=0)).astype(BF16)

    return pl.pallas_call(
        body, name=name, grid=(D // tile,),
        in_specs=[pl.BlockSpec((s.shape[0], tile), lambda i: (0, i)) for s in segments],
        out_specs=pl.BlockSpec((dev_hi - dev_lo, COLS_PER_DEV, tile), lambda i: (0, 0, i)),
        out_shape=jax.ShapeDtypeStruct((dev_hi - dev_lo, COLS_PER_DEV, D), BF16),
        compiler_params=_params("arbitrary"))(*segments)


def _by_cols(a):
    return jnp.moveaxis(a, 0, 1).reshape(a.shape[1], -1)


def _col_slabs(a):
    return jnp.moveaxis(a.reshape(a.shape[0], N_DEV, -1), 1, 0).astype(BF16)


def _pack_small(grads, loss):
    pieces, at = [], 0
    for n, (off, size) in list(SMALL_SLOTS.items()) + [("loss", (LOSS_SLOT, 1))]:
        pieces += [jnp.zeros((off - at,), F32), (loss if n == "loss" else grads[n]).reshape(-1)]
        at = off + size
    return jnp.concatenate(pieces + [jnp.zeros((SMALL_LEN - at,), F32)]).reshape(1, 1, SMALL_LEN)


def _gather_weights(t):
    cast = lambda a: a.astype(BF16)
    loras = jnp.stack([t["w_lora_up"][0], t["a_lora_up"][0]])
    wt, woa, wob, wo, lora = _all_gather(
        cast(t["w_in"][0].T), [cast(t["w_out_a"][0]), cast(t["w_out_b"][0]), cast(t["w_out"][0]), cast(loras)],
        "weight_gather")
    in_a, in_b, in_g, in_f = _split_wt(wt)
    return {"in_a": in_a, "in_b": in_b, "in_g": in_g, "in_f": in_f, "w_out_a": _by_cols(woa), "w_out_b": _by_cols(wob),
            "w_out": wo.reshape(D, D), "w_lora_up": lora[:, 0], "a_lora_up": lora[:, 1]}


def kernel(x, norm_g, w_in, shift_mu, w_lora_up, w0, a_lora_up, a0, k_k, k_a, r_k, lnx_w, lnx_b, f_bias, q_norm_g, k_norm_g, w_out_a, w_out_b, w_out, final_norm_g, loss_target, m_norm_g, m_w_in, m_shift_mu, m_w_lora_up, m_w0, m_a_lora_up, m_a0, m_k_k, m_k_a, m_r_k, m_lnx_w, m_lnx_b, m_f_bias, m_q_norm_g, m_k_norm_g, m_w_out_a, m_w_out_b, m_w_out, m_final_norm_g, v_norm_g, v_w_in, v_shift_mu, v_w_lora_up, v_w0, v_a_lora_up, v_a0, v_k_k, v_k_a, v_r_k, v_lnx_w, v_lnx_b, v_f_bias, v_q_norm_g, v_k_norm_g, v_w_out_a, v_w_out_b, v_w_out, v_final_norm_g):
    names = ("norm_g", "w_in", "shift_mu", "w_lora_up", "w0", "a_lora_up", "a0", "k_k", "k_a", "r_k", "lnx_w", "lnx_b",
             "f_bias", "q_norm_g", "k_norm_g", "w_out_a", "w_out_b", "w_out", "final_norm_g")
    weights = dict(zip(names, (norm_g, w_in, shift_mu, w_lora_up, w0, a_lora_up, a0, k_k, k_a, r_k, lnx_w, lnx_b,
                               f_bias, q_norm_g, k_norm_g, w_out_a, w_out_b, w_out, final_norm_g)))
    m_in = dict(zip(names, (m_norm_g, m_w_in, m_shift_mu, m_w_lora_up, m_w0, m_a_lora_up, m_a0, m_k_k, m_k_a, m_r_k,
                            m_lnx_w, m_lnx_b, m_f_bias, m_q_norm_g, m_k_norm_g, m_w_out_a, m_w_out_b, m_w_out,
                            m_final_norm_g)))
    v_in = dict(zip(names, (v_norm_g, v_w_in, v_shift_mu, v_w_lora_up, v_w0, v_a_lora_up, v_a0, v_k_k, v_k_a, v_r_k,
                            v_lnx_w, v_lnx_b, v_f_bias, v_q_norm_g, v_k_norm_g, v_w_out_a, v_w_out_b, v_w_out,
                            v_final_norm_g)))

    matrices = ("w_out_a", "w_out_b", "w_out", "w_lora_up", "a_lora_up")
    as_2d = lambda n, a: a[0] if n in matrices else a.reshape(1, -1)

    full = _gather_weights(weights)
    dx, dng, recv_wt, recvs, recv_small = _local_step(
        x[0], loss_target[0], full, {n: as_2d(n, weights[n]) for n in ("norm_g",) + tuple(SMALL_SLOTS)})

    res, (recv_norm,) = _adamw_w_in(*recv_wt, w_in[0].T, m_w_in[0].T, v_w_in[0].T, (dng[None],), ((0, N_DEV),))
    outs = {"w_in": [r.T[None] for r in res]}
    misc = [n for n in names if n != "w_in"]
    res, loss_sum = _adamw_misc(recvs, recv_small, recv_norm,
                                {n: tuple(as_2d(n, t[n]) for t in (weights, m_in, v_in)) for n in misc})
    for n in misc:
        outs[n] = [r.reshape(weights[n].shape) for r in res[n]]
    return (loss_sum.reshape(()), dx[None], *[outs[n][i] for i in range(4) for n in names])
```

```python
import functools
import math

import jax
import jax.numpy as jnp
from jax import lax
from jax.experimental import pallas as pl
from jax.experimental.pallas import tpu as pltpu

F32 = jnp.float32
BF16 = jnp.bfloat16
HI = lax.Precision.HIGHEST
MESH = pl.DeviceIdType.MESH

N_DEV = 8
D = 1024
H = 8
N = 64
DA = H * N
RANK = 64
NA = 4 * DA + 2 * RANK
NB = 4 * DA
NG = 2 * D
NF = 128
IN_COLS = NA + NB + H + NG
COLS_PER_DEV = IN_COLS // N_DEV
RMS_EPS = 1e-6
LNX_EPS = 64e-5
ATT_SCALE = N ** -0.5

ADAM_LR = 0.001
ADAM_B1 = 0.9
ADAM_B2 = 0.999
ADAM_EPS = 1e-08
ADAM_WD = 0.01
ADAM_STEP = 10

LANES = 128
WKV_CHUNK = 64
TOK_TILE = 256
HEAD_TILE = 128
ATT_TILE = 256
ATT_GROUPS = 8
VMEM_LIMIT = 56 * 1024 * 1024

SMALL_SLOTS = {"final_norm_g": (0, D), "shift_mu": (D, NA), "w0": (3200, DA), "a0": (3712, DA), "k_k": (4224, DA),
               "k_a": (4736, DA), "r_k": (5248, DA), "lnx_w": (5760, DA), "lnx_b": (6272, DA), "q_norm_g": (6784, N),
               "k_norm_g": (6912, N), "f_bias": (7040, H)}
LOSS_SLOT = 7168
SMALL_LEN = 7296
W_IN_COL_TILE = 256
EARLY_FROM = -(-NA // COLS_PER_DEV)


def _params(*sem):
    return pltpu.CompilerParams(dimension_semantics=sem or None, vmem_limit_bytes=VMEM_LIMIT)


def _bdot(a, b):
    return jnp.dot(a.astype(BF16), b.astype(BF16), preferred_element_type=F32)


def _bdot_nt(a, b):
    return lax.dot_general(a.astype(BF16), b.astype(BF16), (((1,), (1,)), ((), ())), preferred_element_type=F32)


def _bdot_tn(a, b):
    return lax.dot_general(a.astype(BF16), b.astype(BF16), (((0,), (0,)), ((), ())), preferred_element_type=F32)


def _sigmoid(x):
    return 1.0 / (1.0 + jnp.exp(-x))


def _softplus(x):
    return jnp.maximum(x, 0.0) + jnp.log(1.0 + jnp.exp(-jnp.abs(x)))


def _heads(ref, col0):
    return jnp.stack([ref[:, col0 + N * h:col0 + N * (h + 1)] for h in range(H)])


def _store_heads(ref, col0, val):
    for h in range(H):
        ref[:, col0 + N * h:col0 + N * (h + 1)] = val[h]


def _lerp(c, s, mu):
    return c + (s - c) * mu


def _head_sums(x):
    low = lax.broadcasted_iota(jnp.int32, (x.shape[0], LANES), 1) < N
    out = []
    for p in range(x.shape[1] // LANES):
        pair = x[:, LANES * p:LANES * (p + 1)]
        first = jnp.sum(jnp.where(low, pair, 0.0), axis=-1, keepdims=True)
        second = jnp.sum(jnp.where(low, 0.0, pair), axis=-1, keepdims=True)
        out.append(jnp.where(low, first, second))
    return jnp.concatenate(out, axis=-1)


def _to_heads(x):
    return [x[:, N * h:N * (h + 1)] for h in range(H)]


def _from_heads(ref):
    return jnp.concatenate([ref[h] for h in range(H)], axis=-1)


def _rwkv_pre(rc, rs, kc, ks, vc, vs, gc, gs, wdc, wds, adc, ads,
              mu_r, mu_k, mu_v, mu_g, mu_wd, mu_ad, w_up, w0, a_up, a0, k_k, k_a):
    r = _lerp(rc, rs, mu_r)
    k = _lerp(kc, ks, mu_k)
    v = _lerp(vc, vs, mu_v)
    g = _lerp(gc, gs, mu_g)
    wd = _lerp(wdc, wds, mu_wd)
    ad = _lerp(adc, ads, mu_ad)
    t = wd.shape[0]
    w_raw = -_softplus(-(w0 + _bdot(jnp.tanh(wd), w_up))) - 0.5
    lw = -jnp.exp(w_raw)
    row = lax.broadcasted_iota(jnp.int32, (t, t), 0)
    col = lax.broadcasted_iota(jnp.int32, (t, t), 1)
    same_chunk = ((row >= col) & (row // WKV_CHUNK == col // WKV_CHUNK)).astype(F32)
    cl = jnp.dot(same_chunk, lw, precision=HI, preferred_element_type=F32)
    alr = _sigmoid(a0 + _bdot(ad, a_up))
    kk = k * k_k
    kk = kk / jnp.maximum(jnp.sqrt(_head_sums(kk * kk)), 1e-12)
    k2 = k * (1.0 + (alr - 1.0) * k_a)
    return r, lw, cl, k2, v, -kk, kk * alr, g


_MM_DIMS = {"nn": (((2,), (1,)), ((0,), (0,))), "nt": (((2,), (2,)), ((0,), (0,))), "tn": (((1,), (1,)), ((0,), (0,)))}


def _split(x):
    hi = x.astype(BF16)
    return hi, (x - hi.astype(F32)).astype(BF16)


def _dot3(a, b, kind):
    ah, al = _split(a)
    bh, bl = _split(b)
    dot = functools.partial(lax.dot_general, dimension_numbers=_MM_DIMS[kind], preferred_element_type=F32)
    return dot(ah, bh) + (dot(ah, bl) + dot(al, bh))


def _dot1(a, b, kind):
    return lax.dot_general(a.astype(BF16), b.astype(BF16), dimension_numbers=_MM_DIMS[kind], preferred_element_type=F32)


@functools.partial(jax.custom_vjp, nondiff_argnums=(2, 3))
def _mm(a, b, kind, fine=True):
    return _dot3(a, b, kind) if fine else _dot1(a, b, kind)


def _mm_fwd(a, b, kind, fine):
    return _mm(a, b, kind, fine), (a, b)


def _mm_bwd(kind, fine, res, ct):
    a, b = res
    if kind == "nn":
        return _dot1(ct, b, "nt"), _dot1(a, ct, "tn")
    if kind == "nt":
        return _dot1(ct, b, "nn"), _dot1(ct, a, "tn")
    return _dot1(b, ct, "nt"), _dot1(a, ct, "nn")


_mm.defvjp(_mm_fwd, _mm_bwd)


def _chunk_masks(c):
    row = lax.broadcasted_iota(jnp.int32, (c, c), 0)
    col = lax.broadcasted_iota(jnp.int32, (c, c), 1)
    return (row >= col)[None], (row > col)[None], (row == col).astype(F32)[None]


def _wkv_aab(fine, lw, cl, a, b):
    _, strict, _ = _chunk_masks(a.shape[1])
    return jnp.where(strict, _mm(a * jnp.exp(cl - lw), b * jnp.exp(-cl), "nt", fine), 0.0)


def _tri_inverse(x):
    c = x.shape[1]
    p = _chunk_masks(c)[2] + x
    for _ in range(int(math.log2(c)) - 1):
        x = _dot1(x, x, "nn")
        p = p + _dot1(p, x, "nn")
    return p


def _wkv_apply(fine, s0, r, lw, cl, k, v, a, b, p):
    c = r.shape[1]
    incl, strict, _ = _chunk_masks(c)
    mm = functools.partial(_mm, fine=fine)
    gi = jnp.exp(-cl)
    left = jnp.concatenate([a * jnp.exp(cl - lw), r * jnp.exp(cl)], axis=1)
    right = jnp.concatenate([b * gi, k * gi], axis=1)
    m = mm(left, right, "nt")
    z0 = mm(left, s0, "nt")
    a_ak = jnp.where(strict, m[:, :c, c:], 0.0)
    row = lax.broadcasted_iota(jnp.int32, (c, 2 * c), 0)
    col = lax.broadcasted_iota(jnp.int32, (c, 2 * c), 1)
    a_r = jnp.where((row >= col % c)[None], m[:, c:, :], 0.0)
    sa = mm(p, z0[:, :c] + mm(a_ak, v, "nn"), "nn")
    sa_v = jnp.concatenate([sa, v], axis=1)
    y = z0[:, c:] + mm(a_r, sa_v, "nn")
    s1 = (s0 + mm(sa_v, right, "tn")) * jnp.exp(cl[:, c - 1:c, :])
    return y, s1


def _rwkv_post(y, r, k2, v, g, lnx_w, lnx_b, r_k):
    yc = y - _head_sums(y) * (1.0 / N)
    var = _head_sums(yc * yc) * (1.0 / N)
    yn = yc * lax.rsqrt(var + LNX_EPS) * lnx_w + lnx_b
    bonus = _head_sums(r * k2 * r_k) * v
    return (yn + bonus) * (g * _sigmoid(g))


def _fox_pre(q, k, f, q_g, k_g, f_b):
    qn = q * lax.rsqrt(_head_sums(q * q) * (1.0 / N) + RMS_EPS) * q_g
    kn = k * lax.rsqrt(_head_sums(k * k) * (1.0 / N) + RMS_EPS) * k_g
    x = f + f_b
    return qn, kn, jnp.minimum(x, 0.0) - jnp.log(1.0 + jnp.exp(-jnp.abs(x)))


def _rms_fwd(x, g):
    s = x.shape[0]

    def body(x_ref, g_ref, h_ref):
        xv = x_ref[...]
        h_ref[...] = (xv * lax.rsqrt(jnp.mean(xv * xv, axis=-1, keepdims=True) + RMS_EPS) * g_ref[...]).astype(BF16)

    return pl.pallas_call(
        body, name="rms_fwd", grid=(s // TOK_TILE,),
        in_specs=[pl.BlockSpec((TOK_TILE, D), lambda i: (i, 0)), pl.BlockSpec((1, D), lambda i: (0, 0))],
        out_specs=pl.BlockSpec((TOK_TILE, D), lambda i: (i, 0)),
        out_shape=jax.ShapeDtypeStruct((s, D), BF16), compiler_params=_params("arbitrary"))(x, g)


def _proj(h, wt, name):
    s, n = h.shape[0], wt.shape[0]

    def body(h_ref, w_ref, o_ref):
        o_ref[...] = _bdot_nt(h_ref[...], w_ref[...])

    return pl.pallas_call(
        body, name=name, grid=(s // TOK_TILE,),
        in_specs=[pl.BlockSpec((TOK_TILE, D), lambda i: (i, 0)), pl.BlockSpec((n, D), lambda i: (0, 0))],
        out_specs=pl.BlockSpec((TOK_TILE, n), lambda i: (i, 0)),
        out_shape=jax.ShapeDtypeStruct((s, n), F32), compiler_params=_params("arbitrary"))(h, wt)


def _proj_wgrad(h, du, name):
    s, n = du.shape

    def body(h_ref, du_ref, o_ref):
        @pl.when(pl.program_id(0) == 0)
        def _():
            o_ref[...] = jnp.zeros_like(o_ref)

        o_ref[...] += _bdot_tn(du_ref[...], h_ref[...])

    return pl.pallas_call(
        body, name=name, grid=(s // TOK_TILE,),
        in_specs=[pl.BlockSpec((TOK_TILE, D), lambda i: (i, 0)), pl.BlockSpec((TOK_TILE, n), lambda i: (i, 0))],
        out_specs=pl.BlockSpec((n, D), lambda i: (0, 0)),
        out_shape=jax.ShapeDtypeStruct((n, D), F32), compiler_params=_params("arbitrary"))(h, du)


def _proj_xgrad(x, g, dx2, dus, ws, slabs, owners):
    s = x.shape[0]
    tile = HEAD_TILE
    k = len(dus)
    nx = len(slabs)
    n_in = 3 + 2 * k + nx

    def body(*refs):
        x_ref, g_ref, dx2_ref = refs[:3]
        du_refs, w_refs = refs[3:3 + k], refs[3 + k:3 + 2 * k]
        src_refs = refs[3 + 2 * k:3 + 2 * k + nx]
        dx_ref, dg_ref = refs[n_in:n_in + 2]
        dst_refs = refs[n_in + 2:n_in + 2 + nx]
        start, wait = _exchange_ops(src_refs, dst_refs, owners, refs[n_in + 2 + nx:])

        @pl.when(pl.program_id(0) == 0)
        def _():
            dg_ref[...] = jnp.zeros_like(dg_ref)
            start()

        dh = _bdot(du_refs[0][...], w_refs[0][...])
        for du_ref, w_ref in zip(du_refs[1:], w_refs[1:]):
            dh += _bdot(du_ref[...], w_ref[...])
        xv = x_ref[...]
        rs = lax.rsqrt(jnp.mean(xv * xv, axis=-1, keepdims=True) + RMS_EPS)
        xn = xv * rs
        dg_ref[...] += jnp.sum(dh * xn, axis=0, keepdims=True)
        dxn = dh * g_ref[...]
        dx_ref[...] = rs * (dxn - xn * jnp.mean(dxn * xn, axis=-1, keepdims=True)) + dx2_ref[...]

        @pl.when(pl.program_id(0) == s // tile - 1)
        def _():
            wait()

    tok = lambda n: pl.BlockSpec((tile, n), lambda i: (i, 0))
    fixed = lambda a: pl.BlockSpec(a.shape, lambda i: (0,) * a.ndim)
    out = pl.pallas_call(
        body, name="proj_xgrad", grid=(s // tile,),
        in_specs=([tok(D), fixed(g), tok(D)] + [tok(du.shape[1]) for du in dus] + [fixed(w) for w in ws]
                  + _hbm_specs(nx)),
        out_specs=[tok(D), pl.BlockSpec((1, D), lambda i: (0, 0))] + _hbm_specs(nx),
        out_shape=[jax.ShapeDtypeStruct((s, D), F32), jax.ShapeDtypeStruct((1, D), F32)] + _received_shapes(slabs, owners),
        scratch_shapes=_exchange_scratch(nx),
        compiler_params=_params("arbitrary"))(x, g, dx2, *dus, *ws, *slabs)
    return out[0], out[1], out[2:]


def _tail(x, target, ya, o, ub, ug, w_oa, w_ob, w_o, fg):
    s = x.shape[0]
    tile = TOK_TILE

    def body(x_ref, t_ref, ya_ref, o_ref, gb_ref, ug_ref, woa_ref, wob_ref, wo_ref, fg_ref,
             loss_ref, dfg_ref, dwo_ref, dwoa_ref, dwob_ref, dx2_ref, dya_ref, do_ref, dgb_ref, dug_ref):
        @pl.when(pl.program_id(0) == 0)
        def _():
            for r in (loss_ref, dfg_ref, dwo_ref, dwoa_ref, dwob_ref):
                r[...] = jnp.zeros_like(r)

        ya_v = ya_ref[...]
        gate_b = gb_ref[...]
        sg_b = _sigmoid(gate_b)
        silu_b = gate_b * sg_b
        o_v = jnp.concatenate([o_ref[h] for h in range(H)], axis=-1)
        yb_v = o_v * silu_b
        big_a = _bdot(ya_v, woa_ref[...])
        big_b = _bdot(yb_v, wob_ref[...])
        sa = _sigmoid(ug_ref[:, :D])
        sb = _sigmoid(ug_ref[:, D:])
        merged = sa * big_a + sb * big_b
        x2 = x_ref[...] + _bdot(merged, wo_ref[...])
        rs = lax.rsqrt(jnp.mean(x2 * x2, axis=-1, keepdims=True) + RMS_EPS)
        xn = x2 * rs
        err = xn * fg_ref[...] - t_ref[...]
        loss_ref[...] += (0.5 / D) * jnp.sum(err * err)
        dout = err * (1.0 / D)
        dfg_ref[...] += jnp.sum(dout * xn, axis=0, keepdims=True)
        dxn = dout * fg_ref[...]
        dx2 = rs * (dxn - xn * jnp.mean(dxn * xn, axis=-1, keepdims=True))
        dx2_ref[...] = dx2
        dwo_ref[...] += _bdot_tn(merged, dx2)
        dmerged = _bdot_nt(dx2, wo_ref[...])
        dbig_a = dmerged * sa
        dbig_b = dmerged * sb
        dug_ref[:, :D] = dmerged * big_a * sa * (1.0 - sa)
        dug_ref[:, D:] = dmerged * big_b * sb * (1.0 - sb)
        dwoa_ref[...] += _bdot_tn(ya_v, dbig_a)
        dwob_ref[...] += _bdot_tn(yb_v, dbig_b)
        dya_ref[...] = _bdot_nt(dbig_a, woa_ref[...])
        dyb = _bdot_nt(dbig_b, wob_ref[...])
        dgb_ref[...] = dyb * o_v * (sg_b * (1.0 + gate_b * (1.0 - sg_b)))
        _dov = dyb * silu_b
        for h in range(H):
            do_ref[h] = _dov[:, N * h:N * (h + 1)]

    tok = lambda n: pl.BlockSpec((tile, n), lambda i: (i, 0))
    hm = pl.BlockSpec((H, tile, N), lambda i: (0, i, 0))
    fixed = lambda shape: pl.BlockSpec(shape, lambda i: (0,) * len(shape))
    f32 = lambda *shape: jax.ShapeDtypeStruct(shape, F32)
    return pl.pallas_call(
        body, name="tail", grid=(s // tile,),
        in_specs=[tok(D), tok(D), tok(DA), hm, pl.BlockSpec((tile, DA), lambda i: (i, 3)), tok(NG),
                  fixed((DA, D)), fixed((DA, D)), fixed((D, D)), fixed((1, D))],
        out_specs=[fixed((1, 1)), fixed((1, D)), fixed((D, D)), fixed((DA, D)), fixed((DA, D)),
                   tok(D), tok(DA), hm, tok(DA), tok(NG)],
        out_shape=[f32(1, 1), f32(1, D), f32(D, D), f32(DA, D), f32(DA, D),
                   f32(s, D), f32(s, DA), f32(H, s, N), f32(s, DA), f32(s, NG)],
        compiler_params=_params("arbitrary"))(x, target, ya, o, ub, ug, w_oa, w_ob, w_o, fg)


def _pre_operands(ua_ref, prev_ref, first):
    cur = ua_ref[...]
    t = cur.shape[0]
    prev_row = jnp.where(first, 0.0, prev_ref[7:8, :])
    rows = lax.broadcasted_iota(jnp.int32, cur.shape, 0)
    sh = jnp.where(rows == 0, prev_row, pltpu.roll(cur, 1, axis=0))
    ops = []
    for c0, n in ((0, DA), (DA, DA), (2 * DA, DA), (3 * DA + 2 * RANK, DA), (3 * DA, RANK), (3 * DA + RANK, RANK)):
        ops += [cur[:, c0:c0 + n], sh[:, c0:c0 + n]]
    del t
    return ops


def _ua_specs(tile, order):
    blocks = tile // 8
    return [pl.BlockSpec((tile, NA), lambda i: (order(i), 0)),
            pl.BlockSpec((8, NA), lambda i: (jnp.maximum(order(i) * blocks - 1, 0), 0))]


def _rwkv_pre_fwd(ua, pre_params):
    s = ua.shape[0]
    tile = HEAD_TILE

    def body(ua_ref, prev_ref, *refs):
        p_refs, o_refs = refs[:len(pre_params)], refs[len(pre_params):]
        ops = _pre_operands(ua_ref, prev_ref, pl.program_id(0) == 0)
        outs = _rwkv_pre(*ops, *[p[...] for p in p_refs])
        for o_ref, val in zip(o_refs, outs):
            o_ref[...] = val

    tm = pl.BlockSpec((tile, DA), lambda i: (i, 0))
    return pl.pallas_call(
        body, name="rwkv_pre_fwd", grid=(s // tile,),
        in_specs=_ua_specs(tile, lambda i: i) + [pl.BlockSpec(p.shape, lambda i, nd=p.ndim: (0,) * nd) for p in pre_params],
        out_specs=[tm] * 8, out_shape=[jax.ShapeDtypeStruct((s, DA), F32)] * 8,
        compiler_params=_params("arbitrary"))(ua, ua, *pre_params)


def _rwkv_pre_bwd(ua, pre_params, cots):
    s = ua.shape[0]
    tile = HEAD_TILE
    nt = s // tile
    n_p = len(pre_params)

    def body(ua_ref, prev_ref, *refs):
        p_refs, c_refs = refs[:n_p], refs[n_p:n_p + 11]
        dua_ref = refs[n_p + 11]
        dp_refs = refs[n_p + 12:n_p + 12 + n_p]
        carry_ref = refs[-1]
        i = pl.program_id(0)

        @pl.when(i == 0)
        def _():
            carry_ref[...] = jnp.zeros_like(carry_ref)
            for r in dp_refs:
                r[...] = jnp.zeros_like(r)

        ops = _pre_operands(ua_ref, prev_ref, i == nt - 1)
        _, vjp = jax.vjp(_rwkv_pre, *ops, *[p[...] for p in p_refs])
        c = [r[...] for r in c_refs]
        grads = vjp((c[0] + c[1], c[2], c[3], c[4] + c[5], c[6] + c[7], c[8], c[9], c[10]))
        d_ops, d_par = grads[:12], grads[12:]
        for r, val in zip(dp_refs, d_par):
            r[...] += val
        d_cur = jnp.concatenate([d_ops[0], d_ops[2], d_ops[4], d_ops[8], d_ops[10], d_ops[6]], axis=-1)
        d_sh = jnp.concatenate([d_ops[1], d_ops[3], d_ops[5], d_ops[9], d_ops[11], d_ops[7]], axis=-1)
        rows = lax.broadcasted_iota(jnp.int32, d_sh.shape, 0)
        dua_ref[...] = d_cur + jnp.where(rows == tile - 1, carry_ref[...], pltpu.roll(d_sh, tile - 1, axis=0))
        carry_ref[...] = d_sh[0:1, :]

    rev = lambda i: nt - 1 - i
    tm = pl.BlockSpec((tile, DA), lambda i: (rev(i), 0))
    fixed = [pl.BlockSpec(p.shape, lambda i, nd=p.ndim: (0,) * nd) for p in pre_params]
    return pl.pallas_call(
        body, name="rwkv_pre_bwd", grid=(nt,),
        in_specs=_ua_specs(tile, rev) + fixed + [tm] * 11,
        out_specs=[pl.BlockSpec((tile, NA), lambda i: (rev(i), 0))] + fixed,
        out_shape=[jax.ShapeDtypeStruct((s, NA), F32)] + [jax.ShapeDtypeStruct(p.shape, F32) for p in pre_params],
        scratch_shapes=[pltpu.VMEM((1, NA), F32)],
        compiler_params=_params("arbitrary"))(ua, ua, *pre_params, *cots)


def _wkv_fwd(seq):
    s = seq[0].shape[0]
    nc = s // WKV_CHUNK

    def body(r_ref, lw_ref, cl_ref, k_ref, v_ref, a_ref, b_ref, y_ref, ck_ref, p_ref, state):
        @pl.when(pl.program_id(0) == 0)
        def _():
            state[...] = jnp.zeros_like(state)

        r, lw, cl, k, v, a, b = (jnp.stack(_to_heads(ref[...])) for ref in (r_ref, lw_ref, cl_ref, k_ref, v_ref, a_ref,
                                                                             b_ref))
        s0 = state[...]
        ck_ref[0] = s0
        p = _tri_inverse(_wkv_aab(True, lw, cl, a, b))
        p_ref[0] = p
        y, s1 = _wkv_apply(True, s0, r, lw, cl, k, v, a, b, p)
        y_ref[...] = jnp.concatenate([y[h] for h in range(H)], axis=-1)
        state[...] = s1

    tm = pl.BlockSpec((WKV_CHUNK, DA), lambda c: (c, 0))
    per_chunk = lambda m: pl.BlockSpec((1, H, m, m), lambda c: (c, 0, 0, 0))
    return pl.pallas_call(
        body, name="wkv_fwd", grid=(nc,), in_specs=[tm] * 7,
        out_specs=[tm, per_chunk(N), per_chunk(WKV_CHUNK)],
        out_shape=[jax.ShapeDtypeStruct((s, DA), F32), jax.ShapeDtypeStruct((nc, H, N, N), F32),
                   jax.ShapeDtypeStruct((nc, H, WKV_CHUNK, WKV_CHUNK), F32)],
        scratch_shapes=[pltpu.VMEM((H, N, N), F32)], compiler_params=_params("arbitrary"))(*seq)


def _wkv_bwd(seq, ckpt, pinv, dy, slabs, owners):
    s = seq[0].shape[0]
    nc = s // WKV_CHUNK
    nx = len(slabs)

    def body(r_ref, lw_ref, cl_ref, k_ref, v_ref, a_ref, b_ref, ck_ref, p_ref, dy_ref, *refs):
        src_refs, d_refs, dst_refs = refs[:nx], refs[nx:nx + 7], refs[nx + 7:2 * nx + 7]
        dstate = refs[2 * nx + 7]
        start, wait = _exchange_ops(src_refs, dst_refs, owners, refs[2 * nx + 8:])

        @pl.when(pl.program_id(0) == 0)
        def _():
            dstate[...] = jnp.zeros_like(dstate)
            start()

        p = p_ref[0]
        r, lw, cl, k, v, a, b, dy = (jnp.stack(_to_heads(ref[...])) for ref in (r_ref, lw_ref, cl_ref, k_ref, v_ref,
                                                                                 a_ref, b_ref, dy_ref))
        _, vjp = jax.vjp(functools.partial(_wkv_apply, False), ck_ref[0], r, lw, cl, k, v, a, b, p)
        ds0, dr, dlw, dcl, dk, dv, da, db, dp = vjp((dy, dstate[...]))
        dstate[...] = ds0
        _, vjp_x = jax.vjp(functools.partial(_wkv_aab, False), lw, cl, a, b)
        dlw2, dcl2, da2, db2 = vjp_x(_dot1(_dot1(p, dp, "tn"), p, "nt"))
        for d_ref, val in zip(d_refs, (dr, dlw + dlw2, dcl + dcl2, dk, dv, da + da2, db + db2)):
            d_ref[...] = jnp.concatenate([val[h] for h in range(H)], axis=-1)

        @pl.when(pl.program_id(0) == nc - 1)
        def _():
            wait()

    tm = pl.BlockSpec((WKV_CHUNK, DA), lambda c: (nc - 1 - c, 0))
    per_chunk = lambda m: pl.BlockSpec((1, H, m, m), lambda c: (nc - 1 - c, 0, 0, 0))
    out = pl.pallas_call(
        body, name="wkv_bwd", grid=(nc,),
        in_specs=[tm] * 7 + [per_chunk(N), per_chunk(WKV_CHUNK), tm] + _hbm_specs(nx),
        out_specs=[tm] * 7 + _hbm_specs(nx),
        out_shape=[jax.ShapeDtypeStruct((s, DA), F32)] * 7 + _received_shapes(slabs, owners),
        scratch_shapes=[pltpu.VMEM((H, N, N), F32)] + _exchange_scratch(nx),
        compiler_params=_params("arbitrary"))(*seq, ckpt, pinv, dy, *slabs)
    return out[:7], out[7:]


def _rwkv_post_fwd(y, r, k2, v, g, post_params):
    s = y.shape[0]
    tile = TOK_TILE

    def body(*refs):
        refs[-1][...] = _rwkv_post(*[ref[...] for ref in refs[:-1]])

    tm = pl.BlockSpec((tile, DA), lambda i: (i, 0))
    par = pl.BlockSpec((1, DA), lambda i: (0, 0))
    return pl.pallas_call(
        body, name="rwkv_post_fwd", grid=(s // tile,), in_specs=[tm] * 5 + [par] * 3,
        out_specs=tm, out_shape=jax.ShapeDtypeStruct((s, DA), F32),
        compiler_params=_params("arbitrary"))(y, r, k2, v, g, *post_params)


def _rwkv_post_bwd(y, r, k2, v, g, post_params, dya, slabs, lo):
    s = y.shape[0]
    tile = HEAD_TILE

    def body(y_ref, r_ref, k_ref, v_ref, g_ref, w_ref, b_ref, rk_ref, dya_ref, s_ref, *refs):
        d_refs, p_ref = refs[:8], refs[8]
        start, wait = _pair_swap_ops(s_ref, p_ref, lo, refs[9:])

        @pl.when(pl.program_id(0) == 0)
        def _():
            for ref in d_refs[5:]:
                ref[...] = jnp.zeros_like(ref)
            start()

        _, vjp = jax.vjp(_rwkv_post, *[ref[...] for ref in (y_ref, r_ref, k_ref, v_ref, g_ref, w_ref, b_ref, rk_ref)])
        grads = vjp(dya_ref[...])
        for ref, val in zip(d_refs[:5], grads[:5]):
            ref[...] = val
        for ref, val in zip(d_refs[5:], grads[5:]):
            ref[...] += val

        @pl.when(pl.program_id(0) == s // tile - 1)
        def _():
            wait()

    tm = pl.BlockSpec((tile, DA), lambda i: (i, 0))
    par = pl.BlockSpec((1, DA), lambda i: (0, 0))
    return pl.pallas_call(
        body, name="rwkv_post_bwd", grid=(s // tile,),
        in_specs=[tm] * 5 + [par] * 3 + [tm] + _hbm_specs(1),
        out_specs=[tm] * 5 + [par] * 3 + _hbm_specs(1),
        out_shape=[jax.ShapeDtypeStruct((s, DA), F32)] * 5 + [jax.ShapeDtypeStruct((1, DA), F32)] * 3
        + [jax.ShapeDtypeStruct(slabs.shape, slabs.dtype)],
        scratch_shapes=_pair_swap_scratch(slabs.shape[0]),
        compiler_params=_params("arbitrary"))(y, r, k2, v, g, *post_params, dya, slabs)


def _tri(t):
    return (lax.broadcasted_iota(jnp.int32, (t, t), 0) >= lax.broadcasted_iota(jnp.int32, (t, t), 1)).astype(F32)


def _fox_pre_fwd(ub, uf, q_g, k_g, f_b):
    s = ub.shape[0]
    tile = HEAD_TILE

    def body(ub_ref, uf_ref, qg_ref, kg_ref, fb_ref, q_ref, k_ref, v_ref, cum_ref, carry):
        @pl.when(pl.program_id(0) == 0)
        def _():
            carry[...] = jnp.zeros_like(carry)

        qn, kn, logf = _fox_pre(ub_ref[:, :DA], ub_ref[:, DA:2 * DA], uf_ref[...], qg_ref[...], kg_ref[...],
                                fb_ref[...])
        for h, (q_col, k_col) in enumerate(zip(_to_heads(qn), _to_heads(kn))):
            q_ref[h] = q_col
            k_ref[h] = k_col
        v_ref[...] = _heads(ub_ref, 2 * DA)
        cum = jnp.dot(_tri(tile), logf, precision=HI, preferred_element_type=F32) + carry[...]
        cum_ref[...] = cum
        carry[...] = cum[tile - 1:tile, :]

    hm = pl.BlockSpec((H, tile, N), lambda i: (0, i, 0))
    fixed = lambda shape: pl.BlockSpec(shape, lambda i: (0,) * len(shape))
    return pl.pallas_call(
        body, name="fox_pre_fwd", grid=(s // tile,),
        in_specs=[pl.BlockSpec((tile, NB), lambda i: (i, 0)), pl.BlockSpec((tile, NF), lambda i: (i, 0)),
                  fixed((1, DA)), fixed((1, DA)), fixed((1, NF))],
        out_specs=[hm] * 3 + [pl.BlockSpec((tile, NF), lambda i: (i, 0))],
        out_shape=[jax.ShapeDtypeStruct((H, s, N), F32)] * 3 + [jax.ShapeDtypeStruct((s, NF), F32)],
        scratch_shapes=[pltpu.VMEM((1, NF), F32)], compiler_params=_params("arbitrary"))(ub, uf, q_g, k_g, f_b)


def _fox_pre_bwd(ub, uf, q_g, k_g, f_b, dqn, dkn, dvf, dgate, dcum_q, dcum_k):
    s = ub.shape[0]
    tile = HEAD_TILE
    nt = s // tile

    def body(ub_ref, uf_ref, qg_ref, kg_ref, fb_ref, dq_ref, dk_ref, dv_ref, dgate_ref, dcq_ref, dck_ref,
             dub_ref, duf_ref, dqg_ref, dkg_ref, dfb_ref, carry):
        @pl.when(pl.program_id(0) == 0)
        def _():
            carry[...] = jnp.zeros_like(carry)
            for ref in (dqg_ref, dkg_ref, dfb_ref):
                ref[...] = jnp.zeros_like(ref)

        dcum = dcq_ref[...] + dck_ref[...]
        dlogf = lax.dot_general(_tri(tile), dcum, (((0,), (0,)), ((), ())), precision=HI,
                                preferred_element_type=F32) + carry[...]
        carry[...] = dlogf[0:1, :]
        _, vjp = jax.vjp(_fox_pre, ub_ref[:, :DA], ub_ref[:, DA:2 * DA], uf_ref[...], qg_ref[...], kg_ref[...],
                         fb_ref[...])
        d_q, d_k, d_f, d_qg, d_kg, d_fb = vjp((_from_heads(dq_ref), _from_heads(dk_ref), dlogf))
        dub_ref[:, :DA] = d_q
        dub_ref[:, DA:2 * DA] = d_k
        _store_heads(dub_ref, 2 * DA, dv_ref[...])
        dub_ref[:, 3 * DA:] = dgate_ref[...]
        duf_ref[...] = d_f
        dqg_ref[...] += functools.reduce(jnp.add, _to_heads(d_qg))
        dkg_ref[...] += functools.reduce(jnp.add, _to_heads(d_kg))
        dfb_ref[...] += d_fb

    rev = lambda i: nt - 1 - i
    hm = pl.BlockSpec((H, tile, N), lambda i: (0, rev(i), 0))
    tok = lambda n: pl.BlockSpec((tile, n), lambda i: (rev(i), 0))
    fixed = lambda shape: pl.BlockSpec(shape, lambda i: (0,) * len(shape))
    return pl.pallas_call(
        body, name="fox_pre_bwd", grid=(nt,),
        in_specs=[tok(NB), tok(NF), fixed((1, DA)), fixed((1, DA)), fixed((1, NF)), hm, hm, hm, tok(DA), tok(NF),
                  tok(NF)],
        out_specs=[tok(NB), tok(NF), fixed((1, N)), fixed((1, N)), fixed((1, NF))],
        out_shape=[jax.ShapeDtypeStruct((s, NB), F32), jax.ShapeDtypeStruct((s, NF), F32),
                   jax.ShapeDtypeStruct((1, N), F32), jax.ShapeDtypeStruct((1, N), F32),
                   jax.ShapeDtypeStruct((1, NF), F32)],
        scratch_shapes=[pltpu.VMEM((1, NF), F32)],
        compiler_params=_params("arbitrary"))(ub, uf, q_g, k_g, f_b, dqn, dkn, dvf, dgate, dcum_q, dcum_k)


def _att_groups(s):
    blocks = s // ATT_TILE
    per = max(1, blocks // ATT_GROUPS)
    return per, blocks // per


def _att_parts(n, width):
    return ([(0, n - width, False)] if n > width else []) + [(n - width, n, True)]


def _att_scores(q_bf, k_ref, ck_ref, lo, hi, masked, row_offset):
    scores = _bdot_nt(q_bf, k_ref[0, lo:hi, :]) - ck_ref[0, :, lo:hi]
    if masked:
        rows = row_offset + lax.broadcasted_iota(jnp.int32, scores.shape, 0)
        scores = jnp.where(rows >= lax.broadcasted_iota(jnp.int32, scores.shape, 1), scores, -1e30)
    return scores


def _fox_attn_fwd(q, k, v, cum_q, cum_k):
    s = q.shape[1]
    t = ATT_TILE
    per, groups = _att_groups(s)

    def body(q_ref, k_ref, v_ref, cq_ref, ck_ref, o_ref, lse_ref):
        qi = pl.program_id(1)
        for g in range(groups):
            @pl.when(qi // per == g)
            def _(g=g):
                q_bf = (q_ref[0] * ATT_SCALE).astype(BF16)
                parts = _att_parts((g + 1) * per * t, per * t)
                scores = [_att_scores(q_bf, k_ref, ck_ref, lo, hi, masked, (qi - g * per) * t)
                          for lo, hi, masked in parts]
                m = functools.reduce(jnp.maximum, [jnp.max(sc, axis=-1, keepdims=True) for sc in scores])
                l, acc = 0.0, 0.0
                for sc, (lo, hi, _) in zip(scores, parts):
                    p = jnp.exp(sc - m)
                    l += jnp.sum(p, axis=-1, keepdims=True)
                    acc += _bdot(p, v_ref[0, lo:hi, :])
                o_ref[0] = acc / l
                lse_ref[0] = m + jnp.log(l) + cq_ref[0]

    qb = pl.BlockSpec((1, t, N), lambda h, i: (h, i, 0))
    kb = pl.BlockSpec((1, s, N), lambda h, i: (h, 0, 0))
    return pl.pallas_call(
        body, name="fox_attn_fwd", grid=(H, s // t),
        in_specs=[qb, kb, kb, pl.BlockSpec((1, t, 1), lambda h, i: (h, i, 0)),
                  pl.BlockSpec((1, 1, s), lambda h, i: (h, 0, 0))],
        out_specs=[qb, pl.BlockSpec((1, t, 1), lambda h, i: (h, i, 0))],
        out_shape=[jax.ShapeDtypeStruct((H, s, N), F32), jax.ShapeDtypeStruct((H, s, 1), F32)],
        compiler_params=_params("arbitrary", "arbitrary"))(q, k, v, cum_q, cum_k)


def _fox_attn_bwd(q, k, v, cum_q, cum_k, o, lse, do, slabs, owners):
    s = q.shape[1]
    t = ATT_TILE
    per, groups = _att_groups(s)
    nx = len(slabs)

    def body(q_ref, k_ref, v_ref, cq_ref, ck_ref, o_ref, lse_ref, do_ref, *refs):
        src_refs, (dq_ref, dk_ref, dv_ref, dcq_ref, dck_ref) = refs[:nx], refs[nx:nx + 5]
        start, wait = _exchange_ops(src_refs, refs[nx + 5:2 * nx + 5], owners, refs[2 * nx + 5:])
        qi = pl.program_id(1)

        @pl.when((pl.program_id(0) == 0) & (qi == 0))
        def _():
            start()

        @pl.when(qi == 0)
        def _():
            for ref in (dk_ref, dv_ref, dck_ref):
                ref[...] = jnp.zeros_like(ref)

        for g in range(groups):
            @pl.when(qi // per == g)
            def _(g=g):
                q_bf, do_bf = (q_ref[0] * ATT_SCALE).astype(BF16), do_ref[0].astype(BF16)
                row_term = cq_ref[0] - lse_ref[0]
                delta = jnp.sum(do_ref[0] * o_ref[0], axis=-1, keepdims=True)
                dq, dcq = 0.0, 0.0
                for lo, hi, masked in _att_parts((g + 1) * per * t, per * t):
                    p = jnp.exp(_att_scores(q_bf, k_ref, ck_ref, lo, hi, masked, (qi - g * per) * t) + row_term)
                    ds = p * (_bdot_nt(do_bf, v_ref[0, lo:hi, :]) - delta)
                    dq += _bdot(ds, k_ref[0, lo:hi, :])
                    dcq += jnp.sum(ds, axis=-1, keepdims=True)
                    dk_ref[0, lo:hi, :] += _bdot_tn(ds, q_bf)
                    dv_ref[0, lo:hi, :] += _bdot_tn(p, do_bf)
                    dck_ref[0, :, lo:hi] -= jnp.sum(ds, axis=0, keepdims=True)
                dq_ref[0] = dq * ATT_SCALE
                dcq_ref[0] = dcq

        @pl.when((pl.program_id(0) == H - 1) & (qi == s // t - 1))
        def _():
            wait()

    qb = pl.BlockSpec((1, t, N), lambda h, i: (h, i, 0))
    kb = pl.BlockSpec((1, s, N), lambda h, i: (h, 0, 0))
    cqb = pl.BlockSpec((1, t, 1), lambda h, i: (h, i, 0))
    ckb = pl.BlockSpec((1, 1, s), lambda h, i: (h, 0, 0))
    f32 = lambda *shape: jax.ShapeDtypeStruct(shape, F32)
    out = pl.pallas_call(
        body, name="fox_attn_bwd", grid=(H, s // t),
        in_specs=[qb, kb, kb, cqb, ckb, qb, cqb, qb] + _hbm_specs(nx), out_specs=[qb, kb, kb, cqb, ckb] + _hbm_specs(nx),
        out_shape=[f32(H, s, N), f32(H, s, N), f32(H, s, N), f32(H, s, 1), f32(H, 1, s)]
        + _received_shapes(slabs, owners),
        scratch_shapes=_exchange_scratch(nx),
        compiler_params=_params("arbitrary", "arbitrary"))(q, k, v, cum_q, cum_k, o, lse, do, *slabs)
    return out[:5], out[5:]


def _local_step(x, target, w, p):
    mu = p["shift_mu"]
    lora_matrix = lambda a: jnp.moveaxis(a, 0, 1).reshape(RANK, DA).astype(F32)
    pre_params = (mu[:, 0:DA], mu[:, DA:2 * DA], mu[:, 2 * DA:3 * DA], mu[:, 3 * DA + 2 * RANK:],
                  mu[:, 3 * DA:3 * DA + RANK], mu[:, 3 * DA + RANK:3 * DA + 2 * RANK],
                  lora_matrix(w["w_lora_up"]), p["w0"], lora_matrix(w["a_lora_up"]), p["a0"], p["k_k"], p["k_a"])
    post_params = (p["lnx_w"], p["lnx_b"], p["r_k"])
    q_g, k_g = jnp.tile(p["q_norm_g"], (1, H)), jnp.tile(p["k_norm_g"], (1, H))
    f_b = jnp.pad(p["f_bias"], ((0, 0), (0, NF - H)))
    fg = p["final_norm_g"].reshape(1, D)

    h = _rms_fwd(x, p["norm_g"])
    ua = _proj(h, w["in_a"], "proj_a")
    ub = _proj(h, w["in_b"], "proj_b")
    ug = _proj(h, w["in_g"], "proj_g")
    uf = _proj(h, w["in_f"], "proj_f")
    r, lw, cl, k2, v, av, bv, gg = _rwkv_pre_fwd(ua, pre_params)
    y, ckpt, pinv = _wkv_fwd((r, lw, cl, k2, v, av, bv))
    ya = _rwkv_post_fwd(y, r, k2, v, gg, post_params)
    qn, kn, vf, cum = _fox_pre_fwd(ub, uf, q_g, k_g, f_b)
    cum_t = cum[:, :H].T
    cum_q, cum_k = cum_t[:, :, None], cum_t[:, None, :]
    o, lse = _fox_attn_fwd(qn, kn, vf, cum_q, cum_k)

    (loss, dfg, dwo, dwoa, dwob, dx2, dya, do, dgate_b, dug) = _tail(
        x, target, ya, o, ub, ug, w["w_out_a"], w["w_out_b"], w["w_out"], fg)
    everyone = (0, N_DEV)
    (dqn, dkn, dvf, dcq, dck), (recv_woa, recv_wob, recv_wo) = _fox_attn_bwd(
        qn, kn, vf, cum_q, cum_k, o, lse, do,
        (_col_slabs(dwoa), _col_slabs(dwob), dwo.astype(BF16).reshape(N_DEV, D // N_DEV, D)), (everyone,) * 3)
    pad_f = lambda a: jnp.pad(a.T, ((0, 0), (0, NF - H)))
    dub, duf, dqg, dkg, dfb = _fox_pre_bwd(ub, uf, q_g, k_g, f_b, dqn, dkn, dvf, dgate_b,
                                           pad_f(dcq[:, :, 0]), pad_f(dck.reshape(H, -1)))
    dwt_b, dwt_g, dwt_f = (_proj_wgrad(h, du, name) for du, name in ((dub, "wgrad_b"), (dug, "wgrad_g"), (duf, "wgrad_f")))
    early = _slab_wt_grad((dwt_b, dwt_g, dwt_f), (_WT_SEGMENTS[1], _WT_SEGMENTS[2], _WT_SEGMENTS[3]), EARLY_FROM, N_DEV,
                          "slab_wt_early")
    dy, dr_p, dk_p, dv_p, dgg, dlnw, dlnb, drk, handed = _rwkv_post_bwd(y, r, k2, v, gg, post_params, dya, early,
                                                                          EARLY_FROM)
    early = _chip_sums(early, handed, EARLY_FROM, "chip_sums_early")
    (dr_s, dlw, dcl, dk_s, dv_s, dav, dbv), (recv_early,) = _wkv_bwd(
        (r, lw, cl, k2, v, av, bv), ckpt, pinv, dy, (early,), ((EARLY_FROM, N_DEV, "chips"),))
    pre_out = _rwkv_pre_bwd(ua, pre_params, (dr_s, dr_p, dlw, dcl, dk_s, dk_p, dv_s, dv_p, dav, dbv, dgg))
    dua, dpre = pre_out[0], pre_out[1:]
    dwt_a = _proj_wgrad(h, dua, "wgrad_a")

    flat = lambda a: a.reshape(1, -1)
    small = {
        "final_norm_g": dfg, "w0": dpre[7], "a0": dpre[9], "k_k": dpre[10], "k_a": dpre[11], "r_k": drk, "lnx_w": dlnw,
        "lnx_b": dlnb, "q_norm_g": dqg, "k_norm_g": dkg, "f_bias": dfb[:, :H],
        "shift_mu": jnp.concatenate([flat(dpre[0]), flat(dpre[1]), flat(dpre[2]), dpre[4], dpre[5], flat(dpre[3])], axis=1),
    }
    late = _slab_wt_grad((dwt_a, dwt_b), (_WT_SEGMENTS[0], _WT_SEGMENTS[1]), 0, EARLY_FROM, "slab_wt_late")
    late = _chip_sums(late, _pair_swap(late, 0, "pair_swap_late"), 0, "chip_sums_late")
    by_head = lambda a: jnp.moveaxis(a.reshape(RANK, H, N), 1, 0)
    loras = jnp.stack([by_head(dpre[6]), by_head(dpre[8])], axis=1).astype(BF16)
    dx, dng, (recv_late, recv_lora, recv_small) = _proj_xgrad(
        x, p["norm_g"], dx2, (dua, dub, dug, duf), (w["in_a"], w["in_b"], w["in_g"], w["in_f"]),
        (late, loras, _pack_small(small, loss)), ((0, EARLY_FROM, "chips"), everyone, everyone))
    return dx, dng, (recv_early, recv_late), (recv_woa, recv_wob, recv_wo, recv_lora), recv_small


def _position():
    return lax.axis_index("x"), lax.axis_index("y"), lax.axis_index("c")


def _hbm_specs(n):
    return [pl.BlockSpec(memory_space=pl.ANY)] * n


BIG_GATHER_COPIES = 13
GATHER_ROW_CUT = 400


def _all_gather(big, blocks, name):
    n = len(blocks)

    def body(*refs):
        big_ref, x_refs = refs[0], refs[1:1 + n]
        big_out, out_refs = refs[1 + n], refs[2 + n:2 + 2 * n]
        send_sems, recv_sems, local_sems = refs[2 + 2 * n:]
        x, y, c = _position()
        me, sibling = (x, y, c), (x, y, 1 - c)
        chips = [(1 - x, y), (x, 1 - y), (1 - x, 1 - y)]
        x_nbr, y_nbr, diag = chips
        rows = big_ref.shape[0]
        cut = GATHER_ROW_CUT

        def part(ref, h):
            return ref if h is None else ref.at[pl.ds(0, cut)] if h == 0 else ref.at[pl.ds(cut, rows - cut)]

        def landed(chip, core, h):
            return part(big_out.at[4 * chip[0] + 2 * chip[1] + core], h)

        def big_copy(k, src, dst, to):
            return pltpu.make_async_remote_copy(src_ref=src, dst_ref=dst, send_sem=send_sems.at[7 * n + k],
                                                recv_sem=recv_sems.at[7 * n + k], device_id=to, device_id_type=MESH)

        def arrival(k, chip, core, h):
            dst = landed(chip, core, h)
            return big_copy(k, dst, dst, me)

        def pass_on(k, chip, h, to):
            src = landed(chip, c, h)
            return big_copy(k, src, src, to)

        big_mine = pltpu.make_async_copy(big_ref, landed((x, y), c, None), local_sems.at[n])
        big_mine.start()
        here = (x, y)
        big_sent = [big_copy(0, big_ref, landed(here, c, None), sibling),
                    big_copy(1, part(big_ref, 0), landed(here, c, 0), (*x_nbr, c)),
                    big_copy(2, part(big_ref, 1), landed(here, c, 1), (*y_nbr, c)),
                    big_copy(3, part(big_ref, 1), landed(here, c, 1), (*x_nbr, c)),
                    big_copy(4, part(big_ref, 0), landed(here, c, 0), (*y_nbr, c))]
        for cp in big_sent:
            cp.start()

        def copy(a, k, blk, to, own=False):
            dst = out_refs[a].at[4 * blk[0] + 2 * blk[1] + blk[2]]
            return pltpu.make_async_remote_copy(
                src_ref=x_refs[a] if own else dst, dst_ref=dst, send_sem=send_sems.at[7 * a + k],
                recv_sem=recv_sems.at[7 * a + k], device_id=to, device_id_type=MESH)

        mine = [pltpu.make_async_copy(x_refs[a], out_refs[a].at[4 * x + 2 * y + c], local_sems.at[a]) for a in range(n)]
        for cp in mine:
            cp.start()
        first = []
        for a in range(n):
            first.append(copy(a, 0, me, sibling, own=True))
            first += [copy(a, 1 + j, me, (*chip, c), own=True) for j, chip in enumerate(chips)]
        for cp in first:
            cp.start()

        big_steps = [(1, x_nbr, 0, (*y_nbr, c), 5, 7), (2, y_nbr, 1, (*x_nbr, c), 6, 8), (3, x_nbr, 1, None, None, 9),
                     (4, y_nbr, 0, None, None, 10), (5, diag, 0, None, None, 11), (6, diag, 1, None, None, 12)]
        for k, chip, h, onward, k_onward, k_sibling in big_steps:
            arrival(k, chip, c, h).wait_recv()
            if onward is not None:
                big_sent.append(pass_on(k_onward, chip, h, onward))
                big_sent[-1].start()
            big_sent.append(pass_on(k_sibling, chip, h, sibling))
            big_sent[-1].start()

        passed = []
        for j, chip in enumerate(chips):
            for a in range(n):
                copy(a, 1 + j, (*chip, c), me).wait_recv()
                passed.append(copy(a, 4 + j, (*chip, c), sibling))
                passed[-1].start()
        for a in range(n):
            copy(a, 0, sibling, me).wait_recv()
        for j, chip in enumerate(chips):
            for a in range(n):
                copy(a, 4 + j, (*chip, 1 - c), me).wait_recv()
        arrival(0, here, 1 - c, None).wait_recv()
        for k, chip, h, _, _, k_sibling in big_steps:
            arrival(k_sibling, chip, 1 - c, h).wait_recv()
        for cp in first + passed + big_sent:
            cp.wait_send()
        for cp in mine + [big_mine]:
            cp.wait()

    everything = [big] + list(blocks)
    return pl.pallas_call(
        body, name=name, out_shape=[jax.ShapeDtypeStruct((N_DEV,) + b.shape, b.dtype) for b in everything],
        in_specs=_hbm_specs(n + 1), out_specs=_hbm_specs(n + 1),
        scratch_shapes=[pltpu.SemaphoreType.DMA((7 * n + BIG_GATHER_COPIES,)),
                        pltpu.SemaphoreType.DMA((7 * n + BIG_GATHER_COPIES,)), pltpu.SemaphoreType.DMA((n + 1,))],
    )(*everything)


def _received_shapes(slabs, owners):
    return [jax.ShapeDtypeStruct((N_DEV // 2 if len(o) == 3 else N_DEV,) + s.shape[1:], s.dtype)
            for s, o in zip(slabs, owners)]


def _pair_swap_scratch(n):
    return [pltpu.SemaphoreType.DMA((n,)), pltpu.SemaphoreType.DMA((n,))]


def _pair_swap_ops(s_ref, p_ref, lo, sems):
    send_sems, recv_sems = sems
    n = s_ref.shape[0]

    def run(sending):
        x, y, c = _position()
        for side in (0, 1):
            mine = [pltpu.make_async_remote_copy(src_ref=s_ref.at[i], dst_ref=p_ref.at[i], send_sem=send_sems.at[i],
                                                 recv_sem=recv_sems.at[i], device_id=(x, y, 1 - c), device_id_type=MESH)
                    for i in range(n) if (lo + i) % 2 == side]

            @pl.when(c != side)
            def _():
                for cp in mine:
                    cp.start() if sending else cp.wait_send()

            if not sending:
                @pl.when(c == side)
                def _():
                    for cp in mine:
                        cp.wait_recv()

    return functools.partial(run, True), functools.partial(run, False)


def _pair_swap(slabs, lo, name):
    n = slabs.shape[0]

    def body(s_ref, p_ref, *sems):
        start, wait = _pair_swap_ops(s_ref, p_ref, lo, sems)
        start()
        wait()

    return pl.pallas_call(
        body, name=name, out_shape=jax.ShapeDtypeStruct(slabs.shape, slabs.dtype),
        in_specs=_hbm_specs(1), out_specs=_hbm_specs(1)[0], scratch_shapes=_pair_swap_scratch(n))(slabs)


def _chip_sums(slabs, swapped, lo, name):
    n, rows, cols = slabs.shape
    tile = W_IN_COL_TILE

    def body(s_ref, p_ref, o_ref):
        c = lax.axis_index("c")
        for i in range(n):
            @pl.when(c == (lo + i) % 2)
            def _(i=i):
                o_ref[i] = (s_ref[i].astype(F32) + p_ref[i].astype(F32)).astype(BF16)

    blk = pl.BlockSpec((n, rows, tile), lambda j: (0, 0, j))
    return pl.pallas_call(
        body, name=name, grid=(cols // tile,), in_specs=[blk, blk], out_specs=blk,
        out_shape=jax.ShapeDtypeStruct(slabs.shape, BF16), compiler_params=_params("arbitrary"))(slabs, swapped)


def _exchange_scratch(n):
    return [pltpu.SemaphoreType.DMA((7 * n,)), pltpu.SemaphoreType.DMA((7 * n,)), pltpu.SemaphoreType.DMA((n,))]


def _exchange_ops(src_refs, dst_refs, owners, sems):
    send_sems, recv_sems, local_sems = sems
    n = len(src_refs)

    def guarded(a, dev, fn):
        lo, hi = owners[a][:2]
        if (lo, hi) == (0, N_DEV):
            fn()
        else:
            pl.when((dev >= lo) & (dev < hi))(fn)

    def src(a, dev):
        ref = src_refs[a]
        return ref.at[0] if ref.shape[0] == 1 else ref.at[dev - owners[a][0]]

    def run(sending, waiting):
        x, y, c = _position()
        me = 4 * x + 2 * y + c
        for a in range(n):
            by_chip = len(owners[a]) == 3
            slot = (lambda qx, qy, qc: 2 * qx + qy) if by_chip else (lambda qx, qy, qc: 4 * qx + 2 * qy + qc)
            mine = slot(x, y, c)
            local = lambda a=a, mine=mine: pltpu.make_async_copy(src(a, me), dst_refs[a].at[mine], local_sems.at[a])
            if sending:
                guarded(a, me, lambda local=local: local().start())
            for m in range(2, N_DEV, 2) if by_chip else range(1, N_DEV):
                px, py, pc = x ^ (m >> 2), y ^ ((m >> 1) & 1), c ^ (m & 1)
                peer = 4 * px + 2 * py + pc
                theirs = slot(px, py, pc)
                sem = dict(send_sem=send_sems.at[7 * a + m - 1], recv_sem=recv_sems.at[7 * a + m - 1],
                           device_id=(px, py, pc), device_id_type=MESH)
                send = lambda a=a, peer=peer, sem=sem, mine=mine: pltpu.make_async_remote_copy(
                    src_ref=src(a, peer), dst_ref=dst_refs[a].at[mine], **sem)
                recv = lambda a=a, sem=sem, theirs=theirs: pltpu.make_async_remote_copy(
                    src_ref=src(a, me), dst_ref=dst_refs[a].at[theirs], **sem)
                if sending:
                    guarded(a, peer, lambda send=send: send().start())
                if waiting:
                    guarded(a, me, lambda recv=recv: recv().wait_recv())
                    guarded(a, peer, lambda send=send: send().wait_send())
            if waiting:
                guarded(a, me, lambda local=local: local().wait())

    return functools.partial(run, True, False), functools.partial(run, False, True)


def _sum_slabs(r_ref):
    g = r_ref[0].astype(F32)
    for k in range(1, r_ref.shape[0]):
        g = g + r_ref[k].astype(F32)
    return g


def _adamw(g, w, m, v):
    m_new = ADAM_B1 * m + (1.0 - ADAM_B1) * g
    v_new = ADAM_B2 * v + (1.0 - ADAM_B2) * (g * g)
    m_hat = m_new / (1.0 - ADAM_B1 ** ADAM_STEP)
    v_hat = v_new / (1.0 - ADAM_B2 ** ADAM_STEP)
    return g, -ADAM_LR * (m_hat / (jnp.sqrt(v_hat) + ADAM_EPS) + ADAM_WD * w), m_new, v_new


def _adamw_w_in(recv_early, recv_late, w, m, v, slabs, owners):
    rows, cols = w.shape
    tile = W_IN_COL_TILE
    nx = len(slabs)

    def body(early_ref, late_ref, w_ref, m_ref, v_ref, *refs):
        src_refs, o_refs, dst_refs = refs[:nx], refs[nx:nx + 4], refs[nx + 4:2 * nx + 4]
        start, wait = _exchange_ops(src_refs, dst_refs, owners, refs[2 * nx + 4:])
        x, y, c = _position()
        early_owner = 4 * x + 2 * y + c >= EARLY_FROM

        @pl.when(pl.program_id(0) == 0)
        def _():
            start()

        def update(g):
            for o_ref, val in zip(o_refs, _adamw(g, w_ref[...], m_ref[...], v_ref[...])):
                o_ref[...] = val

        pl.when(early_owner)(lambda: update(_sum_slabs(early_ref)))
        pl.when(jnp.logical_not(early_owner))(lambda: update(_sum_slabs(late_ref)))

        @pl.when(pl.program_id(0) == cols // tile - 1)
        def _():
            wait()

    blk = pl.BlockSpec((rows, tile), lambda i: (0, i))
    slots = lambda r: pl.BlockSpec((r.shape[0], rows, tile), lambda i: (0, 0, i))
    out = pl.pallas_call(
        body, name="adamw_w_in", grid=(cols // tile,),
        in_specs=[slots(recv_early), slots(recv_late), blk, blk, blk] + _hbm_specs(nx),
        out_specs=[blk] * 4 + _hbm_specs(nx),
        out_shape=[jax.ShapeDtypeStruct((rows, cols), F32)] * 4 + _received_shapes(slabs, owners),
        scratch_shapes=_exchange_scratch(nx),
        compiler_params=_params("arbitrary"))(recv_early, recv_late, w, m, v, *slabs)
    return out[:4], out[4:]


def _adamw_misc(recvs, recv_small, recv_norm, params):
    names = list(params)
    flat = [a for n in names for a in params[n]]

    def body(woa_ref, wob_ref, wo_ref, lora_ref, small_ref, norm_ref, *refs):
        p_refs, o_refs = refs[:len(flat)], refs[len(flat):]
        g_small = _sum_slabs(small_ref)
        g_lora = _sum_slabs(lora_ref)
        grads = {"w_out_a": _sum_slabs(woa_ref), "w_out_b": _sum_slabs(wob_ref), "w_out": _sum_slabs(wo_ref),
                 "w_lora_up": g_lora[0], "a_lora_up": g_lora[1], "norm_g": _sum_slabs(norm_ref)}
        for n, (off, size) in SMALL_SLOTS.items():
            grads[n] = g_small[:, off:off + size]
        for i, n in enumerate(names):
            w_ref, m_ref, v_ref = p_refs[3 * i:3 * i + 3]
            for o_ref, val in zip(o_refs[4 * i:4 * i + 4], _adamw(grads[n], w_ref[...], m_ref[...], v_ref[...])):
                o_ref[...] = val
        o_refs[-1][...] = g_small[:, LOSS_SLOT:LOSS_SLOT + 1]

    out = pl.pallas_call(
        body, name="adamw_misc",
        out_shape=[jax.ShapeDtypeStruct(params[n][0].shape, F32) for n in names for _ in range(4)]
        + [jax.ShapeDtypeStruct((1, 1), F32)],
        compiler_params=_params())(*recvs, recv_small, recv_norm, *flat)
    return {n: out[4 * i:4 * i + 4] for i, n in enumerate(names)}, out[-1]


_WT_SEGMENTS = ((0, NA), (NA, NB), (NA + NB + H, NG), (NA + NB, H))


def _split_wt(gathered):
    tile = W_IN_COL_TILE

    def body(g_ref, *o_refs):
        full = jnp.concatenate([g_ref[j] for j in range(N_DEV)], axis=0)
        for o_ref, (row, n) in zip(o_refs, _WT_SEGMENTS):
            seg = full[row:row + n]
            if n < o_ref.shape[0]:
                seg = jnp.concatenate([seg, jnp.zeros((o_ref.shape[0] - n, tile), BF16)], axis=0)
            o_ref[...] = seg

    sizes = (NA, NB, NG, NF)
    return pl.pallas_call(
        body, name="split_wt", grid=(D // tile,),
        in_specs=[pl.BlockSpec((N_DEV, COLS_PER_DEV, tile), lambda i: (0, 0, i))],
        out_specs=[pl.BlockSpec((n, tile), lambda i: (0, i)) for n in sizes],
        out_shape=[jax.ShapeDtypeStruct((n, D), BF16) for n in sizes],
        compiler_params=_params("arbitrary"))(gathered)


def _slab_wt_grad(segments, seg_rows, dev_lo, dev_hi, name):
    tile = W_IN_COL_TILE
    k = len(segments)

    def body(*refs):
        seg_refs, o_ref = refs[:k], refs[k]
        for j in range(dev_lo, dev_hi):
            lo, hi = COLS_PER_DEV * j, COLS_PER_DEV * (j + 1)
            parts = []
            for ref, (row, n) in sorted(zip(seg_refs, seg_rows), key=lambda t: t[1][0]):
                first, last = max(lo, row), min(hi, row + n)
                if first < last:
                    parts.append(ref[first - row:last - row, :])
            o_ref[j - dev_lo] = (parts[0] if len(parts) == 1 else jnp.concatenate(parts, axis=0)).astype(BF16)

    return pl.pallas_call(
        body, name=name, grid=(D // tile,),
        in_specs=[pl.BlockSpec((s.shape[0], tile), lambda i: (0, i)) for s in segments],
        out_specs=pl.BlockSpec((dev_hi - dev_lo, COLS_PER_DEV, tile), lambda i: (0, 0, i)),
        out_shape=jax.ShapeDtypeStruct((dev_hi - dev_lo, COLS_PER_DEV, D), BF16),
        compiler_params=_params("arbitrary"))(*segments)


def _by_cols(a):
    return jnp.moveaxis(a, 0, 1).reshape(a.shape[1], -1)


def _col_slabs(a):
    return jnp.moveaxis(a.reshape(a.shape[0], N_DEV, -1), 1, 0).astype(BF16)


def _pack_small(grads, loss):
    pieces, at = [], 0
    for n, (off, size) in list(SMALL_SLOTS.items()) + [("loss", (LOSS_SLOT, 1))]:
        pieces += [jnp.zeros((off - at,), F32), (loss if n == "loss" else grads[n]).reshape(-1)]
        at = off + size
    return jnp.concatenate(pieces + [jnp.zeros((SMALL_LEN - at,), F32)]).reshape(1, 1, SMALL_LEN)


def _gather_weights(t):
    cast = lambda a: a.astype(BF16)
    loras = jnp.stack([t["w_lora_up"][0], t["a_lora_up"][0]])
    wt, woa, wob, wo, lora = _all_gather(
        cast(t["w_in"][0].T), [cast(t["w_out_a"][0]), cast(t["w_out_b"][0]), cast(t["w_out"][0]), cast(loras)],
        "weight_gather")
    in_a, in_b, in_g, in_f = _split_wt(wt)
    return {"in_a": in_a, "in_b": in_b, "in_g": in_g, "in_f": in_f, "w_out_a": _by_cols(woa), "w_out_b": _by_cols(wob),
            "w_out": wo.reshape(D, D), "w_lora_up": lora[:, 0], "a_lora_up": lora[:, 1]}


def kernel(x, norm_g, w_in, shift_mu, w_lora_up, w0, a_lora_up, a0, k_k, k_a, r_k, lnx_w, lnx_b, f_bias, q_norm_g, k_norm_g, w_out_a, w_out_b, w_out, final_norm_g, loss_target, m_norm_g, m_w_in, m_shift_mu, m_w_lora_up, m_w0, m_a_lora_up, m_a0, m_k_k, m_k_a, m_r_k, m_lnx_w, m_lnx_b, m_f_bias, m_q_norm_g, m_k_norm_g, m_w_out_a, m_w_out_b, m_w_out, m_final_norm_g, v_norm_g, v_w_in, v_shift_mu, v_w_lora_up, v_w0, v_a_lora_up, v_a0, v_k_k, v_k_a, v_r_k, v_lnx_w, v_lnx_b, v_f_bias, v_q_norm_g, v_k_norm_g, v_w_out_a, v_w_out_b, v_w_out, v_final_norm_g):
    names = ("norm_g", "w_in", "shift_mu", "w_lora_up", "w0", "a_lora_up", "a0", "k_k", "k_a", "r_k", "lnx_w", "lnx_b",
             "f_bias", "q_norm_g", "k_norm_g", "w_out_a", "w_out_b", "w_out", "final_norm_g")
    weights = dict(zip(names, (norm_g, w_in, shift_mu, w_lora_up, w0, a_lora_up, a0, k_k, k_a, r_k, lnx_w, lnx_b,
                               f_bias, q_norm_g, k_norm_g, w_out_a, w_out_b, w_out, final_norm_g)))
    m_in = dict(zip(names, (m_norm_g, m_w_in, m_shift_mu, m_w_lora_up, m_w0, m_a_lora_up, m_a0, m_k_k, m_k_a, m_r_k,
                            m_lnx_w, m_lnx_b, m_f_bias, m_q_norm_g, m_k_norm_g, m_w_out_a, m_w_out_b, m_w_out,
                            m_final_norm_g)))
    v_in = dict(zip(names, (v_norm_g, v_w_in, v_shift_mu, v_w_lora_up, v_w0, v_a_lora_up, v_a0, v_k_k, v_k_a, v_r_k,
                            v_lnx_w, v_lnx_b, v_f_bias, v_q_norm_g, v_k_norm_g, v_w_out_a, v_w_out_b, v_w_out,
                            v_final_norm_g)))

    matrices = ("w_out_a", "w_out_b", "w_out", "w_lora_up", "a_lora_up")
    as_2d = lambda n, a: a[0] if n in matrices else a.reshape(1, -1)

    full = _gather_weights(weights)
    dx, dng, recv_wt, recvs, recv_small = _local_step(
        x[0], loss_target[0], full, {n: as_2d(n, weights[n]) for n in ("norm_g",) + tuple(SMALL_SLOTS)})

    res, (recv_norm,) = _adamw_w_in(*recv_wt, w_in[0].T, m_w_in[0].T, v_w_in[0].T, (dng[None],), ((0, N_DEV),))
    outs = {"w_in": [r.T[None] for r in res]}
    misc = [n for n in names if n != "w_in"]
    res, loss_sum = _adamw_misc(recvs, recv_small, recv_norm,
                                {n: tuple(as_2d(n, t[n]) for t in (weights, m_in, v_in)) for n in misc})
    for n in misc:
        outs[n] = [r.reshape(weights[n].shape) for r in res[n]]
    return (loss_sum.reshape(()), dx[None], *[outs[n][i] for i in range(4) for n in names])
```

```python
import functools
import math

import jax
import jax.numpy as jnp
from jax import lax
from jax.experimental import pallas as pl
from jax.experimental.pallas import tpu as pltpu

F32 = jnp.float32
BF16 = jnp.bfloat16
HI = lax.Precision.HIGHEST
MESH = pl.DeviceIdType.MESH

N_DEV = 8
D = 1024
H = 8
N = 64
DA = H * N
RANK = 64
NA = 4 * DA + 2 * RANK
NB = 4 * DA
NG = 2 * D
NF = 128
IN_COLS = NA + NB + H + NG
COLS_PER_DEV = IN_COLS // N_DEV
RMS_EPS = 1e-6
LNX_EPS = 64e-5
ATT_SCALE = N ** -0.5

ADAM_LR = 0.001
ADAM_B1 = 0.9
ADAM_B2 = 0.999
ADAM_EPS = 1e-08
ADAM_WD = 0.01
ADAM_STEP = 10

LANES = 128
WKV_CHUNK = 64
TOK_TILE = 256
HEAD_TILE = 128
ATT_TILE = 256
ATT_GROUPS = 8
VMEM_LIMIT = 56 * 1024 * 1024

SMALL_SLOTS = {"final_norm_g": (0, D), "shift_mu": (D, NA), "w0": (3200, DA), "a0": (3712, DA), "k_k": (4224, DA),
               "k_a": (4736, DA), "r_k": (5248, DA), "lnx_w": (5760, DA), "lnx_b": (6272, DA), "q_norm_g": (6784, N),
               "k_norm_g": (6912, N), "f_bias": (7040, H)}
LOSS_SLOT = 7168
SMALL_LEN = 7296
W_IN_COL_TILE = 256
EARLY_FROM = -(-NA // COLS_PER_DEV)


def _params(*sem):
    return pltpu.CompilerParams(dimension_semantics=sem or None, vmem_limit_bytes=VMEM_LIMIT)


def _bdot(a, b):
    return jnp.dot(a.astype(BF16), b.astype(BF16), preferred_element_type=F32)


def _bdot_nt(a, b):
    return lax.dot_general(a.astype(BF16), b.astype(BF16), (((1,), (1,)), ((), ())), preferred_element_type=F32)


def _bdot_tn(a, b):
    return lax.dot_general(a.astype(BF16), b.astype(BF16), (((0,), (0,)), ((), ())), preferred_element_type=F32)


def _sigmoid(x):
    return 1.0 / (1.0 + jnp.exp(-x))


def _softplus(x):
    return jnp.maximum(x, 0.0) + jnp.log(1.0 + jnp.exp(-jnp.abs(x)))


def _heads(ref, col0):
    return jnp.stack([ref[:, col0 + N * h:col0 + N * (h + 1)] for h in range(H)])


def _store_heads(ref, col0, val):
    for h in range(H):
        ref[:, col0 + N * h:col0 + N * (h + 1)] = val[h]


def _lerp(c, s, mu):
    return c + (s - c) * mu


def _head_sums(x):
    low = lax.broadcasted_iota(jnp.int32, (x.shape[0], LANES), 1) < N
    out = []
    for p in range(x.shape[1] // LANES):
        pair = x[:, LANES * p:LANES * (p + 1)]
        first = jnp.sum(jnp.where(low, pair, 0.0), axis=-1, keepdims=True)
        second = jnp.sum(jnp.where(low, 0.0, pair), axis=-1, keepdims=True)
        out.append(jnp.where(low, first, second))
    return jnp.concatenate(out, axis=-1)


def _to_heads(x):
    return [x[:, N * h:N * (h + 1)] for h in range(H)]


def _from_heads(ref):
    return jnp.concatenate([ref[h] for h in range(H)], axis=-1)


def _rwkv_pre(rc, rs, kc, ks, vc, vs, gc, gs, wdc, wds, adc, ads,
              mu_r, mu_k, mu_v, mu_g, mu_wd, mu_ad, w_up, w0, a_up, a0, k_k, k_a):
    r = _lerp(rc, rs, mu_r)
    k = _lerp(kc, ks, mu_k)
    v = _lerp(vc, vs, mu_v)
    g = _lerp(gc, gs, mu_g)
    wd = _lerp(wdc, wds, mu_wd)
    ad = _lerp(adc, ads, mu_ad)
    t = wd.shape[0]
    w_raw = -_softplus(-(w0 + _bdot(jnp.tanh(wd), w_up))) - 0.5
    lw = -jnp.exp(w_raw)
    row = lax.broadcasted_iota(jnp.int32, (t, t), 0)
    col = lax.broadcasted_iota(jnp.int32, (t, t), 1)
    same_chunk = ((row >= col) & (row // WKV_CHUNK == col // WKV_CHUNK)).astype(F32)
    cl = jnp.dot(same_chunk, lw, precision=HI, preferred_element_type=F32)
    alr = _sigmoid(a0 + _bdot(ad, a_up))
    kk = k * k_k
    kk = kk / jnp.maximum(jnp.sqrt(_head_sums(kk * kk)), 1e-12)
    k2 = k * (1.0 + (alr - 1.0) * k_a)
    return r, lw, cl, k2, v, -kk, kk * alr, g


_MM_DIMS = {"nn": (((2,), (1,)), ((0,), (0,))), "nt": (((2,), (2,)), ((0,), (0,))), "tn": (((1,), (1,)), ((0,), (0,)))}


def _split(x):
    hi = x.astype(BF16)
    return hi, (x - hi.astype(F32)).astype(BF16)


def _dot3(a, b, kind):
    ah, al = _split(a)
    bh, bl = _split(b)
    dot = functools.partial(lax.dot_general, dimension_numbers=_MM_DIMS[kind], preferred_element_type=F32)
    return dot(ah, bh) + (dot(ah, bl) + dot(al, bh))


def _dot1(a, b, kind):
    return lax.dot_general(a.astype(BF16), b.astype(BF16), dimension_numbers=_MM_DIMS[kind], preferred_element_type=F32)


@functools.partial(jax.custom_vjp, nondiff_argnums=(2, 3))
def _mm(a, b, kind, fine=True):
    return _dot3(a, b, kind) if fine else _dot1(a, b, kind)


def _mm_fwd(a, b, kind, fine):
    return _mm(a, b, kind, fine), (a, b)


def _mm_bwd(kind, fine, res, ct):
    a, b = res
    if kind == "nn":
        return _dot1(ct, b, "nt"), _dot1(a, ct, "tn")
    if kind == "nt":
        return _dot1(ct, b, "nn"), _dot1(ct, a, "tn")
    return _dot1(b, ct, "nt"), _dot1(a, ct, "nn")


_mm.defvjp(_mm_fwd, _mm_bwd)


def _chunk_masks(c):
    row = lax.broadcasted_iota(jnp.int32, (c, c), 0)
    col = lax.broadcasted_iota(jnp.int32, (c, c), 1)
    return (row >= col)[None], (row > col)[None], (row == col).astype(F32)[None]


def _wkv_aab(fine, lw, cl, a, b):
    _, strict, _ = _chunk_masks(a.shape[1])
    return jnp.where(strict, _mm(a * jnp.exp(cl - lw), b * jnp.exp(-cl), "nt", fine), 0.0)


def _tri_inverse(x):
    c = x.shape[1]
    p = _chunk_masks(c)[2] + x
    for _ in range(int(math.log2(c)) - 1):
        x = _dot1(x, x, "nn")
        p = p + _dot1(p, x, "nn")
    return p


def _wkv_apply(fine, s0, r, lw, cl, k, v, a, b, p):
    c = r.shape[1]
    incl, strict, _ = _chunk_masks(c)
    mm = functools.partial(_mm, fine=fine)
    gi = jnp.exp(-cl)
    left = jnp.concatenate([a * jnp.exp(cl - lw), r * jnp.exp(cl)], axis=1)
    right = jnp.concatenate([b * gi, k * gi], axis=1)
    m = mm(left, right, "nt")
    z0 = mm(left, s0, "nt")
    a_ak = jnp.where(strict, m[:, :c, c:], 0.0)
    row = lax.broadcasted_iota(jnp.int32, (c, 2 * c), 0)
    col = lax.broadcasted_iota(jnp.int32, (c, 2 * c), 1)
    a_r = jnp.where((row >= col % c)[None], m[:, c:, :], 0.0)
    sa = mm(p, z0[:, :c] + mm(a_ak, v, "nn"), "nn")
    sa_v = jnp.concatenate([sa, v], axis=1)
    y = z0[:, c:] + mm(a_r, sa_v, "nn")
    s1 = (s0 + mm(sa_v, right, "tn")) * jnp.exp(cl[:, c - 1:c, :])
    return y, s1


def _rwkv_post(y, r, k2, v, g, lnx_w, lnx_b, r_k):
    yc = y - _head_sums(y) * (1.0 / N)
    var = _head_sums(yc * yc) * (1.0 / N)
    yn = yc * lax.rsqrt(var + LNX_EPS) * lnx_w + lnx_b
    bonus = _head_sums(r * k2 * r_k) * v
    return (yn + bonus) * (g * _sigmoid(g))


def _fox_pre(q, k, f, q_g, k_g, f_b):
    qn = q * lax.rsqrt(_head_sums(q * q) * (1.0 / N) + RMS_EPS) * q_g
    kn = k * lax.rsqrt(_head_sums(k * k) * (1.0 / N) + RMS_EPS) * k_g
    x = f + f_b
    return qn, kn, jnp.minimum(x, 0.0) - jnp.log(1.0 + jnp.exp(-jnp.abs(x)))


def _norm_proj(x, g, wts):
    s = x.shape[0]
    k = len(wts)

    def body(x_ref, g_ref, *refs):
        w_refs, h_ref, o_refs = refs[:k], refs[k], refs[k + 1:]
        xv = x_ref[...]
        h = (xv * lax.rsqrt(jnp.mean(xv * xv, axis=-1, keepdims=True) + RMS_EPS) * g_ref[...]).astype(BF16)
        h_ref[...] = h
        for w_ref, o_ref in zip(w_refs, o_refs):
            o_ref[...] = _bdot_nt(h, w_ref[...])

    tok = lambda n: pl.BlockSpec((TOK_TILE, n), lambda i: (i, 0))
    out = pl.pallas_call(
        body, name="norm_proj", grid=(s // TOK_TILE,),
        in_specs=[tok(D), pl.BlockSpec((1, D), lambda i: (0, 0))] + [pl.BlockSpec(w.shape, lambda i: (0, 0)) for w in wts],
        out_specs=[tok(D)] + [tok(w.shape[0]) for w in wts],
        out_shape=[jax.ShapeDtypeStruct((s, D), BF16)] + [jax.ShapeDtypeStruct((s, w.shape[0]), F32) for w in wts],
        compiler_params=_params("arbitrary"))(x, g, *wts)
    return out[0], out[1:]


def _proj_wgrad(h, du, name):
    s, n = du.shape

    def body(h_ref, du_ref, o_ref):
        @pl.when(pl.program_id(0) == 0)
        def _():
            o_ref[...] = jnp.zeros_like(o_ref)

        o_ref[...] += _bdot_tn(du_ref[...], h_ref[...])

    return pl.pallas_call(
        body, name=name, grid=(s // TOK_TILE,),
        in_specs=[pl.BlockSpec((TOK_TILE, D), lambda i: (i, 0)), pl.BlockSpec((TOK_TILE, n), lambda i: (i, 0))],
        out_specs=pl.BlockSpec((n, D), lambda i: (0, 0)),
        out_shape=jax.ShapeDtypeStruct((n, D), F32), compiler_params=_params("arbitrary"))(h, du)


def _proj_wgrad_late(h, dua, dwt_b_head):
    s = dua.shape[0]
    steps = s // TOK_TILE

    def body(h_ref, du_ref, b_ref, o_ref, acc):
        @pl.when(pl.program_id(0) == 0)
        def _():
            acc[...] = jnp.zeros_like(acc)

        acc[...] += _bdot_tn(du_ref[...], h_ref[...])

        @pl.when(pl.program_id(0) == steps - 1)
        def _():
            for j in range(EARLY_FROM):
                lo, hi = COLS_PER_DEV * j, COLS_PER_DEV * (j + 1)
                parts = [acc[lo:min(hi, NA), :]] + ([b_ref[:hi - NA, :]] if hi > NA else [])
                o_ref[j] = (parts[0] if len(parts) == 1 else jnp.concatenate(parts, axis=0)).astype(BF16)

    return pl.pallas_call(
        body, name="wgrad_a", grid=(steps,),
        in_specs=[pl.BlockSpec((TOK_TILE, D), lambda i: (i, 0)), pl.BlockSpec((TOK_TILE, NA), lambda i: (i, 0)),
                  pl.BlockSpec(dwt_b_head.shape, lambda i: (0, 0))],
        out_specs=pl.BlockSpec((EARLY_FROM, COLS_PER_DEV, D), lambda i: (0, 0, 0)),
        out_shape=jax.ShapeDtypeStruct((EARLY_FROM, COLS_PER_DEV, D), BF16),
        scratch_shapes=[pltpu.VMEM((NA, D), F32)], compiler_params=_params("arbitrary"))(h, dua, dwt_b_head)


def _proj_xgrad(x, g, dx2, dus, ws, slabs, owners):
    s = x.shape[0]
    tile = HEAD_TILE
    k = len(dus)
    nx = len(slabs)
    n_in = 3 + 2 * k + nx

    def body(*refs):
        x_ref, g_ref, dx2_ref = refs[:3]
        du_refs, w_refs = refs[3:3 + k], refs[3 + k:3 + 2 * k]
        src_refs = refs[3 + 2 * k:3 + 2 * k + nx]
        dx_ref, dg_ref = refs[n_in:n_in + 2]
        dst_refs = refs[n_in + 2:n_in + 2 + nx]
        start, wait = _exchange_ops(src_refs, dst_refs, owners, refs[n_in + 2 + nx:])

        @pl.when(pl.program_id(0) == 0)
        def _():
            dg_ref[...] = jnp.zeros_like(dg_ref)
            start()

        dh = _bdot(du_refs[0][...], w_refs[0][...])
        for du_ref, w_ref in zip(du_refs[1:], w_refs[1:]):
            dh += _bdot(du_ref[...], w_ref[...])
        xv = x_ref[...]
        rs = lax.rsqrt(jnp.mean(xv * xv, axis=-1, keepdims=True) + RMS_EPS)
        xn = xv * rs
        dg_ref[...] += jnp.sum(dh * xn, axis=0, keepdims=True)
        dxn = dh * g_ref[...]
        dx_ref[...] = rs * (dxn - xn * jnp.mean(dxn * xn, axis=-1, keepdims=True)) + dx2_ref[...]

        @pl.when(pl.program_id(0) == s // tile - 1)
        def _():
            wait()

    tok = lambda n: pl.BlockSpec((tile, n), lambda i: (i, 0))
    fixed = lambda a: pl.BlockSpec(a.shape, lambda i: (0,) * a.ndim)
    out = pl.pallas_call(
        body, name="proj_xgrad", grid=(s // tile,),
        in_specs=([tok(D), fixed(g), tok(D)] + [tok(du.shape[1]) for du in dus] + [fixed(w) for w in ws]
                  + _hbm_specs(nx)),
        out_specs=[tok(D), pl.BlockSpec((1, D), lambda i: (0, 0))] + _hbm_specs(nx),
        out_shape=[jax.ShapeDtypeStruct((s, D), F32), jax.ShapeDtypeStruct((1, D), F32)] + _received_shapes(slabs, owners),
        scratch_shapes=_exchange_scratch(nx),
        compiler_params=_params("arbitrary"))(x, g, dx2, *dus, *ws, *slabs)
    return out[0], out[1], out[2:]


def _tail(x, target, ya, o, ub, ug, w_oa, w_ob, w_o, fg):
    s = x.shape[0]
    tile = TOK_TILE

    def body(x_ref, t_ref, ya_ref, o_ref, gb_ref, ug_ref, woa_ref, wob_ref, wo_ref, fg_ref,
             loss_ref, dfg_ref, dwo_ref, dwoa_ref, dwob_ref, dx2_ref, dya_ref, do_ref, dgb_ref, dug_ref):
        @pl.when(pl.program_id(0) == 0)
        def _():
            for r in (loss_ref, dfg_ref, dwo_ref, dwoa_ref, dwob_ref):
                r[...] = jnp.zeros_like(r)

        ya_v = ya_ref[...]
        gate_b = gb_ref[...]
        sg_b = _sigmoid(gate_b)
        silu_b = gate_b * sg_b
        o_v = jnp.concatenate([o_ref[h] for h in range(H)], axis=-1)
        yb_v = o_v * silu_b
        big_a = _bdot(ya_v, woa_ref[...])
        big_b = _bdot(yb_v, wob_ref[...])
        sa = _sigmoid(ug_ref[:, :D])
        sb = _sigmoid(ug_ref[:, D:])
        merged = sa * big_a + sb * big_b
        x2 = x_ref[...] + _bdot(merged, wo_ref[...])
        rs = lax.rsqrt(jnp.mean(x2 * x2, axis=-1, keepdims=True) + RMS_EPS)
        xn = x2 * rs
        err = xn * fg_ref[...] - t_ref[...]
        loss_ref[...] += (0.5 / D) * jnp.sum(err * err)
        dout = err * (1.0 / D)
        dfg_ref[...] += jnp.sum(dout * xn, axis=0, keepdims=True)
        dxn = dout * fg_ref[...]
        dx2 = rs * (dxn - xn * jnp.mean(dxn * xn, axis=-1, keepdims=True))
        dx2_ref[...] = dx2
        dwo_ref[...] += _bdot_tn(merged, dx2)
        dmerged = _bdot_nt(dx2, wo_ref[...])
        dbig_a = dmerged * sa
        dbig_b = dmerged * sb
        dug_ref[:, :D] = dmerged * big_a * sa * (1.0 - sa)
        dug_ref[:, D:] = dmerged * big_b * sb * (1.0 - sb)
        dwoa_ref[...] += _bdot_tn(ya_v, dbig_a)
        dwob_ref[...] += _bdot_tn(yb_v, dbig_b)
        dya_ref[...] = _bdot_nt(dbig_a, woa_ref[...])
        dyb = _bdot_nt(dbig_b, wob_ref[...])
        dgb_ref[...] = dyb * o_v * (sg_b * (1.0 + gate_b * (1.0 - sg_b)))
        _dov = dyb * silu_b
        for h in range(H):
            do_ref[h] = _dov[:, N * h:N * (h + 1)]

    tok = lambda n: pl.BlockSpec((tile, n), lambda i: (i, 0))
    hm = pl.BlockSpec((H, tile, N), lambda i: (0, i, 0))
    fixed = lambda shape: pl.BlockSpec(shape, lambda i: (0,) * len(shape))
    f32 = lambda *shape: jax.ShapeDtypeStruct(shape, F32)
    return pl.pallas_call(
        body, name="tail", grid=(s // tile,),
        in_specs=[tok(D), tok(D), tok(DA), hm, pl.BlockSpec((tile, DA), lambda i: (i, 3)), tok(NG),
                  fixed((DA, D)), fixed((DA, D)), fixed((D, D)), fixed((1, D))],
        out_specs=[fixed((1, 1)), fixed((1, D)), fixed((D, D)), fixed((DA, D)), fixed((DA, D)),
                   tok(D), tok(DA), hm, tok(DA), tok(NG)],
        out_shape=[f32(1, 1), f32(1, D), f32(D, D), f32(DA, D), f32(DA, D),
                   f32(s, D), f32(s, DA), f32(H, s, N), f32(s, DA), f32(s, NG)],
        compiler_params=_params("arbitrary"))(x, target, ya, o, ub, ug, w_oa, w_ob, w_o, fg)


def _pre_operands(ua_ref, prev_ref, first):
    cur = ua_ref[...]
    t = cur.shape[0]
    prev_row = jnp.where(first, 0.0, prev_ref[7:8, :])
    rows = lax.broadcasted_iota(jnp.int32, cur.shape, 0)
    sh = jnp.where(rows == 0, prev_row, pltpu.roll(cur, 1, axis=0))
    ops = []
    for c0, n in ((0, DA), (DA, DA), (2 * DA, DA), (3 * DA + 2 * RANK, DA), (3 * DA, RANK), (3 * DA + RANK, RANK)):
        ops += [cur[:, c0:c0 + n], sh[:, c0:c0 + n]]
    del t
    return ops


def _ua_specs(tile, order):
    blocks = tile // 8
    return [pl.BlockSpec((tile, NA), lambda i: (order(i), 0)),
            pl.BlockSpec((8, NA), lambda i: (jnp.maximum(order(i) * blocks - 1, 0), 0))]


def _rwkv_pre_fwd(ua, pre_params):
    s = ua.shape[0]
    tile = HEAD_TILE

    def body(ua_ref, prev_ref, *refs):
        p_refs, o_refs = refs[:len(pre_params)], refs[len(pre_params):]
        ops = _pre_operands(ua_ref, prev_ref, pl.program_id(0) == 0)
        outs = _rwkv_pre(*ops, *[p[...] for p in p_refs])
        for o_ref, val in zip(o_refs, outs):
            o_ref[...] = val

    tm = pl.BlockSpec((tile, DA), lambda i: (i, 0))
    return pl.pallas_call(
        body, name="rwkv_pre_fwd", grid=(s // tile,),
        in_specs=_ua_specs(tile, lambda i: i) + [pl.BlockSpec(p.shape, lambda i, nd=p.ndim: (0,) * nd) for p in pre_params],
        out_specs=[tm] * 8, out_shape=[jax.ShapeDtypeStruct((s, DA), F32)] * 8,
        compiler_params=_params("arbitrary"))(ua, ua, *pre_params)


def _rwkv_pre_bwd(ua, pre_params, cots):
    s = ua.shape[0]
    tile = HEAD_TILE
    nt = s // tile
    n_p = len(pre_params)

    def body(ua_ref, prev_ref, *refs):
        p_refs, c_refs = refs[:n_p], refs[n_p:n_p + 11]
        dua_ref = refs[n_p + 11]
        dp_refs = refs[n_p + 12:n_p + 12 + n_p]
        carry_ref = refs[-1]
        i = pl.program_id(0)

        @pl.when(i == 0)
        def _():
            carry_ref[...] = jnp.zeros_like(carry_ref)
            for r in dp_refs:
                r[...] = jnp.zeros_like(r)

        ops = _pre_operands(ua_ref, prev_ref, i == nt - 1)
        _, vjp = jax.vjp(_rwkv_pre, *ops, *[p[...] for p in p_refs])
        c = [r[...] for r in c_refs]
        grads = vjp((c[0] + c[1], c[2], c[3], c[4] + c[5], c[6] + c[7], c[8], c[9], c[10]))
        d_ops, d_par = grads[:12], grads[12:]
        for r, val in zip(dp_refs, d_par):
            r[...] += val
        d_cur = jnp.concatenate([d_ops[0], d_ops[2], d_ops[4], d_ops[8], d_ops[10], d_ops[6]], axis=-1)
        d_sh = jnp.concatenate([d_ops[1], d_ops[3], d_ops[5], d_ops[9], d_ops[11], d_ops[7]], axis=-1)
        rows = lax.broadcasted_iota(jnp.int32, d_sh.shape, 0)
        dua_ref[...] = d_cur + jnp.where(rows == tile - 1, carry_ref[...], pltpu.roll(d_sh, tile - 1, axis=0))
        carry_ref[...] = d_sh[0:1, :]

    rev = lambda i: nt - 1 - i
    tm = pl.BlockSpec((tile, DA), lambda i: (rev(i), 0))
    fixed = [pl.BlockSpec(p.shape, lambda i, nd=p.ndim: (0,) * nd) for p in pre_params]
    return pl.pallas_call(
        body, name="rwkv_pre_bwd", grid=(nt,),
        in_specs=_ua_specs(tile, rev) + fixed + [tm] * 11,
        out_specs=[pl.BlockSpec((tile, NA), lambda i: (rev(i), 0))] + fixed,
        out_shape=[jax.ShapeDtypeStruct((s, NA), F32)] + [jax.ShapeDtypeStruct(p.shape, F32) for p in pre_params],
        scratch_shapes=[pltpu.VMEM((1, NA), F32)],
        compiler_params=_params("arbitrary"))(ua, ua, *pre_params, *cots)


def _wkv_fwd(seq):
    s = seq[0].shape[0]
    nc = s // WKV_CHUNK

    def body(r_ref, lw_ref, cl_ref, k_ref, v_ref, a_ref, b_ref, y_ref, ck_ref, p_ref, state):
        @pl.when(pl.program_id(0) == 0)
        def _():
            state[...] = jnp.zeros_like(state)

        r, lw, cl, k, v, a, b = (jnp.stack(_to_heads(ref[...])) for ref in (r_ref, lw_ref, cl_ref, k_ref, v_ref, a_ref,
                                                                             b_ref))
        s0 = state[...]
        ck_ref[0] = s0
        p = _tri_inverse(_wkv_aab(True, lw, cl, a, b))
        p_ref[0] = p
        y, s1 = _wkv_apply(True, s0, r, lw, cl, k, v, a, b, p)
        y_ref[...] = jnp.concatenate([y[h] for h in range(H)], axis=-1)
        state[...] = s1

    tm = pl.BlockSpec((WKV_CHUNK, DA), lambda c: (c, 0))
    per_chunk = lambda m: pl.BlockSpec((1, H, m, m), lambda c: (c, 0, 0, 0))
    return pl.pallas_call(
        body, name="wkv_fwd", grid=(nc,), in_specs=[tm] * 7,
        out_specs=[tm, per_chunk(N), per_chunk(WKV_CHUNK)],
        out_shape=[jax.ShapeDtypeStruct((s, DA), F32), jax.ShapeDtypeStruct((nc, H, N, N), F32),
                   jax.ShapeDtypeStruct((nc, H, WKV_CHUNK, WKV_CHUNK), F32)],
        scratch_shapes=[pltpu.VMEM((H, N, N), F32)], compiler_params=_params("arbitrary"))(*seq)


def _wkv_bwd(seq, ckpt, pinv, dy, slabs, owners):
    s = seq[0].shape[0]
    nc = s // WKV_CHUNK
    nx = len(slabs)

    def body(r_ref, lw_ref, cl_ref, k_ref, v_ref, a_ref, b_ref, ck_ref, p_ref, dy_ref, *refs):
        src_refs, d_refs, dst_refs = refs[:nx], refs[nx:nx + 7], refs[nx + 7:2 * nx + 7]
        dstate = refs[2 * nx + 7]
        start, wait = _exchange_ops(src_refs, dst_refs, owners, refs[2 * nx + 8:])

        @pl.when(pl.program_id(0) == 0)
        def _():
            dstate[...] = jnp.zeros_like(dstate)
            start()

        p = p_ref[0]
        r, lw, cl, k, v, a, b, dy = (jnp.stack(_to_heads(ref[...])) for ref in (r_ref, lw_ref, cl_ref, k_ref, v_ref,
                                                                                 a_ref, b_ref, dy_ref))
        _, vjp = jax.vjp(functools.partial(_wkv_apply, False), ck_ref[0], r, lw, cl, k, v, a, b, p)
        ds0, dr, dlw, dcl, dk, dv, da, db, dp = vjp((dy, dstate[...]))
        dstate[...] = ds0
        _, vjp_x = jax.vjp(functools.partial(_wkv_aab, False), lw, cl, a, b)
        dlw2, dcl2, da2, db2 = vjp_x(_dot1(_dot1(p, dp, "tn"), p, "nt"))
        for d_ref, val in zip(d_refs, (dr, dlw + dlw2, dcl + dcl2, dk, dv, da + da2, db + db2)):
            d_ref[...] = jnp.concatenate([val[h] for h in range(H)], axis=-1)

        @pl.when(pl.program_id(0) == nc - 1)
        def _():
            wait()

    tm = pl.BlockSpec((WKV_CHUNK, DA), lambda c: (nc - 1 - c, 0))
    per_chunk = lambda m: pl.BlockSpec((1, H, m, m), lambda c: (nc - 1 - c, 0, 0, 0))
    out = pl.pallas_call(
        body, name="wkv_bwd", grid=(nc,),
        in_specs=[tm] * 7 + [per_chunk(N), per_chunk(WKV_CHUNK), tm] + _hbm_specs(nx),
        out_specs=[tm] * 7 + _hbm_specs(nx),
        out_shape=[jax.ShapeDtypeStruct((s, DA), F32)] * 7 + _received_shapes(slabs, owners),
        scratch_shapes=[pltpu.VMEM((H, N, N), F32)] + _exchange_scratch(nx),
        compiler_params=_params("arbitrary"))(*seq, ckpt, pinv, dy, *slabs)
    return out[:7], out[7:]


def _rwkv_post_fwd(y, r, k2, v, g, post_params):
    s = y.shape[0]
    tile = TOK_TILE

    def body(*refs):
        refs[-1][...] = _rwkv_post(*[ref[...] for ref in refs[:-1]])

    tm = pl.BlockSpec((tile, DA), lambda i: (i, 0))
    par = pl.BlockSpec((1, DA), lambda i: (0, 0))
    return pl.pallas_call(
        body, name="rwkv_post_fwd", grid=(s // tile,), in_specs=[tm] * 5 + [par] * 3,
        out_specs=tm, out_shape=jax.ShapeDtypeStruct((s, DA), F32),
        compiler_params=_params("arbitrary"))(y, r, k2, v, g, *post_params)


def _rwkv_post_bwd(y, r, k2, v, g, post_params, dya, slabs, lo):
    s = y.shape[0]
    tile = HEAD_TILE

    def body(y_ref, r_ref, k_ref, v_ref, g_ref, w_ref, b_ref, rk_ref, dya_ref, s_ref, *refs):
        d_refs, p_ref = refs[:8], refs[8]
        start, wait = _pair_swap_ops(s_ref, p_ref, lo, refs[9:])

        @pl.when(pl.program_id(0) == 0)
        def _():
            for ref in d_refs[5:]:
                ref[...] = jnp.zeros_like(ref)
            start()

        _, vjp = jax.vjp(_rwkv_post, *[ref[...] for ref in (y_ref, r_ref, k_ref, v_ref, g_ref, w_ref, b_ref, rk_ref)])
        grads = vjp(dya_ref[...])
        for ref, val in zip(d_refs[:5], grads[:5]):
            ref[...] = val
        for ref, val in zip(d_refs[5:], grads[5:]):
            ref[...] += val

        @pl.when(pl.program_id(0) == s // tile - 1)
        def _():
            wait()

    tm = pl.BlockSpec((tile, DA), lambda i: (i, 0))
    par = pl.BlockSpec((1, DA), lambda i: (0, 0))
    return pl.pallas_call(
        body, name="rwkv_post_bwd", grid=(s // tile,),
        in_specs=[tm] * 5 + [par] * 3 + [tm] + _hbm_specs(1),
        out_specs=[tm] * 5 + [par] * 3 + _hbm_specs(1),
        out_shape=[jax.ShapeDtypeStruct((s, DA), F32)] * 5 + [jax.ShapeDtypeStruct((1, DA), F32)] * 3
        + [jax.ShapeDtypeStruct(slabs.shape, slabs.dtype)],
        scratch_shapes=_pair_swap_scratch(slabs.shape[0]),
        compiler_params=_params("arbitrary"))(y, r, k2, v, g, *post_params, dya, slabs)


def _tri(t):
    return (lax.broadcasted_iota(jnp.int32, (t, t), 0) >= lax.broadcasted_iota(jnp.int32, (t, t), 1)).astype(F32)


def _fox_pre_fwd(ub, uf, q_g, k_g, f_b):
    s = ub.shape[0]
    tile = HEAD_TILE

    def body(ub_ref, uf_ref, qg_ref, kg_ref, fb_ref, q_ref, k_ref, v_ref, cum_ref, carry):
        @pl.when(pl.program_id(0) == 0)
        def _():
            carry[...] = jnp.zeros_like(carry)

        qn, kn, logf = _fox_pre(ub_ref[:, :DA], ub_ref[:, DA:2 * DA], uf_ref[...], qg_ref[...], kg_ref[...],
                                fb_ref[...])
        for h, (q_col, k_col) in enumerate(zip(_to_heads(qn), _to_heads(kn))):
            q_ref[h] = q_col
            k_ref[h] = k_col
        v_ref[...] = _heads(ub_ref, 2 * DA)
        cum = jnp.dot(_tri(tile), logf, precision=HI, preferred_element_type=F32) + carry[...]
        cum_ref[...] = cum
        carry[...] = cum[tile - 1:tile, :]

    hm = pl.BlockSpec((H, tile, N), lambda i: (0, i, 0))
    fixed = lambda shape: pl.BlockSpec(shape, lambda i: (0,) * len(shape))
    return pl.pallas_call(
        body, name="fox_pre_fwd", grid=(s // tile,),
        in_specs=[pl.BlockSpec((tile, NB), lambda i: (i, 0)), pl.BlockSpec((tile, NF), lambda i: (i, 0)),
                  fixed((1, DA)), fixed((1, DA)), fixed((1, NF))],
        out_specs=[hm] * 3 + [pl.BlockSpec((tile, NF), lambda i: (i, 0))],
        out_shape=[jax.ShapeDtypeStruct((H, s, N), F32)] * 3 + [jax.ShapeDtypeStruct((s, NF), F32)],
        scratch_shapes=[pltpu.VMEM((1, NF), F32)], compiler_params=_params("arbitrary"))(ub, uf, q_g, k_g, f_b)


def _fox_pre_bwd(ub, uf, q_g, k_g, f_b, dqn, dkn, dvf, dgate, dcum_q, dcum_k):
    s = ub.shape[0]
    tile = HEAD_TILE
    nt = s // tile

    def body(ub_ref, uf_ref, qg_ref, kg_ref, fb_ref, dq_ref, dk_ref, dv_ref, dgate_ref, dcq_ref, dck_ref,
             dub_ref, duf_ref, dqg_ref, dkg_ref, dfb_ref, carry):
        @pl.when(pl.program_id(0) == 0)
        def _():
            carry[...] = jnp.zeros_like(carry)
            for ref in (dqg_ref, dkg_ref, dfb_ref):
                ref[...] = jnp.zeros_like(ref)

        dcum = dcq_ref[...] + dck_ref[...]
        dlogf = lax.dot_general(_tri(tile), dcum, (((0,), (0,)), ((), ())), precision=HI,
                                preferred_element_type=F32) + carry[...]
        carry[...] = dlogf[0:1, :]
        _, vjp = jax.vjp(_fox_pre, ub_ref[:, :DA], ub_ref[:, DA:2 * DA], uf_ref[...], qg_ref[...], kg_ref[...],
                         fb_ref[...])
        d_q, d_k, d_f, d_qg, d_kg, d_fb = vjp((_from_heads(dq_ref), _from_heads(dk_ref), dlogf))
        dub_ref[:, :DA] = d_q
        dub_ref[:, DA:2 * DA] = d_k
        _store_heads(dub_ref, 2 * DA, dv_ref[...])
        dub_ref[:, 3 * DA:] = dgate_ref[...]
        duf_ref[...] = d_f
        dqg_ref[...] += functools.reduce(jnp.add, _to_heads(d_qg))
        dkg_ref[...] += functools.reduce(jnp.add, _to_heads(d_kg))
        dfb_ref[...] += d_fb

    rev = lambda i: nt - 1 - i
    hm = pl.BlockSpec((H, tile, N), lambda i: (0, rev(i), 0))
    tok = lambda n: pl.BlockSpec((tile, n), lambda i: (rev(i), 0))
    fixed = lambda shape: pl.BlockSpec(shape, lambda i: (0,) * len(shape))
    return pl.pallas_call(
        body, name="fox_pre_bwd", grid=(nt,),
        in_specs=[tok(NB), tok(NF), fixed((1, DA)), fixed((1, DA)), fixed((1, NF)), hm, hm, hm, tok(DA), tok(NF),
                  tok(NF)],
        out_specs=[tok(NB), tok(NF), fixed((1, N)), fixed((1, N)), fixed((1, NF))],
        out_shape=[jax.ShapeDtypeStruct((s, NB), F32), jax.ShapeDtypeStruct((s, NF), F32),
                   jax.ShapeDtypeStruct((1, N), F32), jax.ShapeDtypeStruct((1, N), F32),
                   jax.ShapeDtypeStruct((1, NF), F32)],
        scratch_shapes=[pltpu.VMEM((1, NF), F32)],
        compiler_params=_params("arbitrary"))(ub, uf, q_g, k_g, f_b, dqn, dkn, dvf, dgate, dcum_q, dcum_k)


def _att_groups(s):
    blocks = s // ATT_TILE
    per = max(1, blocks // ATT_GROUPS)
    return per, blocks // per


def _att_parts(n, width):
    return ([(0, n - width, False)] if n > width else []) + [(n - width, n, True)]


def _att_scores(q_bf, k_ref, ck_ref, lo, hi, masked, row_offset):
    scores = _bdot_nt(q_bf, k_ref[0, lo:hi, :]) - ck_ref[0, :, lo:hi]
    if masked:
        rows = row_offset + lax.broadcasted_iota(jnp.int32, scores.shape, 0)
        scores = jnp.where(rows >= lax.broadcasted_iota(jnp.int32, scores.shape, 1), scores, -1e30)
    return scores


def _fox_attn_fwd(q, k, v, cum_q, cum_k):
    s = q.shape[1]
    t = ATT_TILE
    per, groups = _att_groups(s)

    def body(q_ref, k_ref, v_ref, cq_ref, ck_ref, o_ref, lse_ref):
        qi = pl.program_id(1)
        for g in range(groups):
            @pl.when(qi // per == g)
            def _(g=g):
                q_bf = (q_ref[0] * ATT_SCALE).astype(BF16)
                parts = _att_parts((g + 1) * per * t, per * t)
                scores = [_att_scores(q_bf, k_ref, ck_ref, lo, hi, masked, (qi - g * per) * t)
                          for lo, hi, masked in parts]
                m = functools.reduce(jnp.maximum, [jnp.max(sc, axis=-1, keepdims=True) for sc in scores])
                l, acc = 0.0, 0.0
                for sc, (lo, hi, _) in zip(scores, parts):
                    p = jnp.exp(sc - m)
                    l += jnp.sum(p, axis=-1, keepdims=True)
                    acc += _bdot(p, v_ref[0, lo:hi, :])
                o_ref[0] = acc / l
                lse_ref[0] = m + jnp.log(l) + cq_ref[0]

    qb = pl.BlockSpec((1, t, N), lambda h, i: (h, i, 0))
    kb = pl.BlockSpec((1, s, N), lambda h, i: (h, 0, 0))
    return pl.pallas_call(
        body, name="fox_attn_fwd", grid=(H, s // t),
        in_specs=[qb, kb, kb, pl.BlockSpec((1, t, 1), lambda h, i: (h, i, 0)),
                  pl.BlockSpec((1, 1, s), lambda h, i: (h, 0, 0))],
        out_specs=[qb, pl.BlockSpec((1, t, 1), lambda h, i: (h, i, 0))],
        out_shape=[jax.ShapeDtypeStruct((H, s, N), F32), jax.ShapeDtypeStruct((H, s, 1), F32)],
        compiler_params=_params("arbitrary", "arbitrary"))(q, k, v, cum_q, cum_k)


def _fox_attn_bwd(q, k, v, cum_q, cum_k, o, lse, do, slabs, owners):
    s = q.shape[1]
    t = ATT_TILE
    per, groups = _att_groups(s)
    nx = len(slabs)

    def body(q_ref, k_ref, v_ref, cq_ref, ck_ref, o_ref, lse_ref, do_ref, *refs):
        src_refs, (dq_ref, dk_ref, dv_ref, dcq_ref, dck_ref) = refs[:nx], refs[nx:nx + 5]
        start, wait = _exchange_ops(src_refs, refs[nx + 5:2 * nx + 5], owners, refs[2 * nx + 5:])
        qi = pl.program_id(1)

        @pl.when((pl.program_id(0) == 0) & (qi == 0))
        def _():
            start()

        @pl.when(qi == 0)
        def _():
            for ref in (dk_ref, dv_ref, dck_ref):
                ref[...] = jnp.zeros_like(ref)

        for g in range(groups):
            @pl.when(qi // per == g)
            def _(g=g):
                q_bf, do_bf = (q_ref[0] * ATT_SCALE).astype(BF16), do_ref[0].astype(BF16)
                row_term = cq_ref[0] - lse_ref[0]
                delta = jnp.sum(do_ref[0] * o_ref[0], axis=-1, keepdims=True)
                dq, dcq = 0.0, 0.0
                for lo, hi, masked in _att_parts((g + 1) * per * t, per * t):
                    p = jnp.exp(_att_scores(q_bf, k_ref, ck_ref, lo, hi, masked, (qi - g * per) * t) + row_term)
                    ds = p * (_bdot_nt(do_bf, v_ref[0, lo:hi, :]) - delta)
                    dq += _bdot(ds, k_ref[0, lo:hi, :])
                    dcq += jnp.sum(ds, axis=-1, keepdims=True)
                    dk_ref[0, lo:hi, :] += _bdot_tn(ds, q_bf)
                    dv_ref[0, lo:hi, :] += _bdot_tn(p, do_bf)
                    dck_ref[0, :, lo:hi] -= jnp.sum(ds, axis=0, keepdims=True)
                dq_ref[0] = dq * ATT_SCALE
                dcq_ref[0] = dcq

        @pl.when((pl.program_id(0) == H - 1) & (qi == s // t - 1))
        def _():
            wait()

    qb = pl.BlockSpec((1, t, N), lambda h, i: (h, i, 0))
    kb = pl.BlockSpec((1, s, N), lambda h, i: (h, 0, 0))
    cqb = pl.BlockSpec((1, t, 1), lambda h, i: (h, i, 0))
    ckb = pl.BlockSpec((1, 1, s), lambda h, i: (h, 0, 0))
    f32 = lambda *shape: jax.ShapeDtypeStruct(shape, F32)
    out = pl.pallas_call(
        body, name="fox_attn_bwd", grid=(H, s // t),
        in_specs=[qb, kb, kb, cqb, ckb, qb, cqb, qb] + _hbm_specs(nx), out_specs=[qb, kb, kb, cqb, ckb] + _hbm_specs(nx),
        out_shape=[f32(H, s, N), f32(H, s, N), f32(H, s, N), f32(H, s, 1), f32(H, 1, s)]
        + _received_shapes(slabs, owners),
        scratch_shapes=_exchange_scratch(nx),
        compiler_params=_params("arbitrary", "arbitrary"))(q, k, v, cum_q, cum_k, o, lse, do, *slabs)
    return out[:5], out[5:]


def _local_step(x, target, w, p):
    mu = p["shift_mu"]
    lora_matrix = lambda a: jnp.moveaxis(a, 0, 1).reshape(RANK, DA).astype(F32)
    pre_params = (mu[:, 0:DA], mu[:, DA:2 * DA], mu[:, 2 * DA:3 * DA], mu[:, 3 * DA + 2 * RANK:],
                  mu[:, 3 * DA:3 * DA + RANK], mu[:, 3 * DA + RANK:3 * DA + 2 * RANK],
                  lora_matrix(w["w_lora_up"]), p["w0"], lora_matrix(w["a_lora_up"]), p["a0"], p["k_k"], p["k_a"])
    post_params = (p["lnx_w"], p["lnx_b"], p["r_k"])
    q_g, k_g = jnp.tile(p["q_norm_g"], (1, H)), jnp.tile(p["k_norm_g"], (1, H))
    f_b = jnp.pad(p["f_bias"], ((0, 0), (0, NF - H)))
    fg = p["final_norm_g"].reshape(1, D)

    h, (ua, ub, ug, uf) = _norm_proj(x, p["norm_g"], (w["in_a"], w["in_b"], w["in_g"], w["in_f"]))
    r, lw, cl, k2, v, av, bv, gg = _rwkv_pre_fwd(ua, pre_params)
    y, ckpt, pinv = _wkv_fwd((r, lw, cl, k2, v, av, bv))
    ya = _rwkv_post_fwd(y, r, k2, v, gg, post_params)
    qn, kn, vf, cum = _fox_pre_fwd(ub, uf, q_g, k_g, f_b)
    cum_t = cum[:, :H].T
    cum_q, cum_k = cum_t[:, :, None], cum_t[:, None, :]
    o, lse = _fox_attn_fwd(qn, kn, vf, cum_q, cum_k)

    (loss, dfg, dwo, dwoa, dwob, dx2, dya, do, dgate_b, dug) = _tail(
        x, target, ya, o, ub, ug, w["w_out_a"], w["w_out_b"], w["w_out"], fg)
    everyone = (0, N_DEV)
    (dqn, dkn, dvf, dcq, dck), (recv_woa, recv_wob, recv_wo) = _fox_attn_bwd(
        qn, kn, vf, cum_q, cum_k, o, lse, do,
        (_col_slabs(dwoa), _col_slabs(dwob), dwo.astype(BF16).reshape(N_DEV, D // N_DEV, D)), (everyone,) * 3)
    pad_f = lambda a: jnp.pad(a.T, ((0, 0), (0, NF - H)))
    dub, duf, dqg, dkg, dfb = _fox_pre_bwd(ub, uf, q_g, k_g, f_b, dqn, dkn, dvf, dgate_b,
                                           pad_f(dcq[:, :, 0]), pad_f(dck.reshape(H, -1)))
    dwt_b, dwt_g, dwt_f = (_proj_wgrad(h, du, name) for du, name in ((dub, "wgrad_b"), (dug, "wgrad_g"), (duf, "wgrad_f")))
    early = _slab_wt_grad((dwt_b, dwt_g, dwt_f), (_WT_SEGMENTS[1], _WT_SEGMENTS[2], _WT_SEGMENTS[3]), EARLY_FROM, N_DEV,
                          "slab_wt_early")
    dy, dr_p, dk_p, dv_p, dgg, dlnw, dlnb, drk, handed = _rwkv_post_bwd(y, r, k2, v, gg, post_params, dya, early,
                                                                          EARLY_FROM)
    early = _chip_sums(early, handed, EARLY_FROM, "chip_sums_early")
    (dr_s, dlw, dcl, dk_s, dv_s, dav, dbv), (recv_early,) = _wkv_bwd(
        (r, lw, cl, k2, v, av, bv), ckpt, pinv, dy, (early,), ((EARLY_FROM, N_DEV, "chips"),))
    pre_out = _rwkv_pre_bwd(ua, pre_params, (dr_s, dr_p, dlw, dcl, dk_s, dk_p, dv_s, dv_p, dav, dbv, dgg))
    dua, dpre = pre_out[0], pre_out[1:]
    spill = EARLY_FROM * COLS_PER_DEV - NA
    late = _proj_wgrad_late(h, dua, dwt_b[:-(-spill // 8) * 8])

    flat = lambda a: a.reshape(1, -1)
    small = {
        "final_norm_g": dfg, "w0": dpre[7], "a0": dpre[9], "k_k": dpre[10], "k_a": dpre[11], "r_k": drk, "lnx_w": dlnw,
        "lnx_b": dlnb, "q_norm_g": dqg, "k_norm_g": dkg, "f_bias": dfb[:, :H],
        "shift_mu": jnp.concatenate([flat(dpre[0]), flat(dpre[1]), flat(dpre[2]), dpre[4], dpre[5], flat(dpre[3])], axis=1),
    }
    late = _chip_sums(late, _pair_swap(late, 0, "pair_swap_late"), 0, "chip_sums_late")
    by_head = lambda a: jnp.moveaxis(a.reshape(RANK, H, N), 1, 0)
    loras = jnp.stack([by_head(dpre[6]), by_head(dpre[8])], axis=1).astype(BF16)
    dx, dng, (recv_late, recv_lora, recv_small) = _proj_xgrad(
        x, p["norm_g"], dx2, (dua, dub, dug, duf), (w["in_a"], w["in_b"], w["in_g"], w["in_f"]),
        (late, loras, _pack_small(small, loss)), ((0, EARLY_FROM, "chips"), everyone, everyone))
    return dx, dng, (recv_early, recv_late), (recv_woa, recv_wob, recv_wo, recv_lora), recv_small


def _position():
    return lax.axis_index("x"), lax.axis_index("y"), lax.axis_index("c")


def _hbm_specs(n):
    return [pl.BlockSpec(memory_space=pl.ANY)] * n


BIG_GATHER_COPIES = 13
GATHER_ROW_CUT = 400


def _all_gather(big, blocks, name):
    n = len(blocks)

    def body(*refs):
        big_ref, x_refs = refs[0], refs[1:1 + n]
        big_out, out_refs = refs[1 + n], refs[2 + n:2 + 2 * n]
        send_sems, recv_sems, local_sems = refs[2 + 2 * n:]
        x, y, c = _position()
        me, sibling = (x, y, c), (x, y, 1 - c)
        chips = [(1 - x, y), (x, 1 - y), (1 - x, 1 - y)]
        x_nbr, y_nbr, diag = chips
        rows = big_ref.shape[0]
        cut = GATHER_ROW_CUT

        def part(ref, h):
            return ref if h is None else ref.at[pl.ds(0, cut)] if h == 0 else ref.at[pl.ds(cut, rows - cut)]

        def landed(chip, core, h):
            return part(big_out.at[4 * chip[0] + 2 * chip[1] + core], h)

        def big_copy(k, src, dst, to):
            return pltpu.make_async_remote_copy(src_ref=src, dst_ref=dst, send_sem=send_sems.at[7 * n + k],
                                                recv_sem=recv_sems.at[7 * n + k], device_id=to, device_id_type=MESH)

        def arrival(k, chip, core, h):
            dst = landed(chip, core, h)
            return big_copy(k, dst, dst, me)

        def pass_on(k, chip, h, to):
            src = landed(chip, c, h)
            return big_copy(k, src, src, to)

        big_mine = pltpu.make_async_copy(big_ref, landed((x, y), c, None), local_sems.at[n])
        big_mine.start()
        here = (x, y)
        big_sent = [big_copy(0, big_ref, landed(here, c, None), sibling),
                    big_copy(1, part(big_ref, 0), landed(here, c, 0), (*x_nbr, c)),
                    big_copy(2, part(big_ref, 1), landed(here, c, 1), (*y_nbr, c)),
                    big_copy(3, part(big_ref, 1), landed(here, c, 1), (*x_nbr, c)),
                    big_copy(4, part(big_ref, 0), landed(here, c, 0), (*y_nbr, c))]
        for cp in big_sent:
            cp.start()

        def copy(a, k, blk, to, own=False):
            dst = out_refs[a].at[4 * blk[0] + 2 * blk[1] + blk[2]]
            return pltpu.make_async_remote_copy(
                src_ref=x_refs[a] if own else dst, dst_ref=dst, send_sem=send_sems.at[7 * a + k],
                recv_sem=recv_sems.at[7 * a + k], device_id=to, device_id_type=MESH)

        mine = [pltpu.make_async_copy(x_refs[a], out_refs[a].at[4 * x + 2 * y + c], local_sems.at[a]) for a in range(n)]
        for cp in mine:
            cp.start()
        first = []
        for a in range(n):
            first.append(copy(a, 0, me, sibling, own=True))
            first += [copy(a, 1 + j, me, (*chip, c), own=True) for j, chip in enumerate(chips)]
        for cp in first:
            cp.start()

        big_steps = [(1, x_nbr, 0, (*y_nbr, c), 5, 7), (2, y_nbr, 1, (*x_nbr, c), 6, 8), (3, x_nbr, 1, None, None, 9),
                     (4, y_nbr, 0, None, None, 10), (5, diag, 0, None, None, 11), (6, diag, 1, None, None, 12)]
        for k, chip, h, onward, k_onward, k_sibling in big_steps:
            arrival(k, chip, c, h).wait_recv()
            if onward is not None:
                big_sent.append(pass_on(k_onward, chip, h, onward))
                big_sent[-1].start()
            big_sent.append(pass_on(k_sibling, chip, h, sibling))
            big_sent[-1].start()

        passed = []
        for j, chip in enumerate(chips):
            for a in range(n):
                copy(a, 1 + j, (*chip, c), me).wait_recv()
                passed.append(copy(a, 4 + j, (*chip, c), sibling))
                passed[-1].start()
        for a in range(n):
            copy(a, 0, sibling, me).wait_recv()
        for j, chip in enumerate(chips):
            for a in range(n):
                copy(a, 4 + j, (*chip, 1 - c), me).wait_recv()
        arrival(0, here, 1 - c, None).wait_recv()
        for k, chip, h, _, _, k_sibling in big_steps:
            arrival(k_sibling, chip, 1 - c, h).wait_recv()
        for cp in first + passed + big_sent:
            cp.wait_send()
        for cp in mine + [big_mine]:
            cp.wait()

    everything = [big] + list(blocks)
    return pl.pallas_call(
        body, name=name, out_shape=[jax.ShapeDtypeStruct((N_DEV,) + b.shape, b.dtype) for b in everything],
        in_specs=_hbm_specs(n + 1), out_specs=_hbm_specs(n + 1),
        scratch_shapes=[pltpu.SemaphoreType.DMA((7 * n + BIG_GATHER_COPIES,)),
                        pltpu.SemaphoreType.DMA((7 * n + BIG_GATHER_COPIES,)), pltpu.SemaphoreType.DMA((n + 1,))],
    )(*everything)


def _received_shapes(slabs, owners):
    return [jax.ShapeDtypeStruct((N_DEV // 2 if len(o) == 3 else N_DEV,) + s.shape[1:], s.dtype)
            for s, o in zip(slabs, owners)]


def _pair_swap_scratch(n):
    return [pltpu.SemaphoreType.DMA((n,)), pltpu.SemaphoreType.DMA((n,))]


def _pair_swap_ops(s_ref, p_ref, lo, sems):
    send_sems, recv_sems = sems
    n = s_ref.shape[0]

    def run(sending):
        x, y, c = _position()
        for side in (0, 1):
            mine = [pltpu.make_async_remote_copy(src_ref=s_ref.at[i], dst_ref=p_ref.at[i], send_sem=send_sems.at[i],
                                                 recv_sem=recv_sems.at[i], device_id=(x, y, 1 - c), device_id_type=MESH)
                    for i in range(n) if (lo + i) % 2 == side]

            @pl.when(c != side)
            def _():
                for cp in mine:
                    cp.start() if sending else cp.wait_send()

            if not sending:
                @pl.when(c == side)
                def _():
                    for cp in mine:
                        cp.wait_recv()

    return functools.partial(run, True), functools.partial(run, False)


def _pair_swap(slabs, lo, name):
    n = slabs.shape[0]

    def body(s_ref, p_ref, *sems):
        start, wait = _pair_swap_ops(s_ref, p_ref, lo, sems)
        start()
        wait()

    return pl.pallas_call(
        body, name=name, out_shape=jax.ShapeDtypeStruct(slabs.shape, slabs.dtype),
        in_specs=_hbm_specs(1), out_specs=_hbm_specs(1)[0], scratch_shapes=_pair_swap_scratch(n))(slabs)


def _chip_sums(slabs, swapped, lo, name):
    n, rows, cols = slabs.shape
    tile = W_IN_COL_TILE

    def body(s_ref, p_ref, o_ref):
        c = lax.axis_index("c")
        for i in range(n):
            @pl.when(c == (lo + i) % 2)
            def _(i=i):
                o_ref[i] = (s_ref[i].astype(F32) + p_ref[i].astype(F32)).astype(BF16)

    blk = pl.BlockSpec((n, rows, tile), lambda j: (0, 0, j))
    return pl.pallas_call(
        body, name=name, grid=(cols // tile,), in_specs=[blk, blk], out_specs=blk,
        out_shape=jax.ShapeDtypeStruct(slabs.shape, BF16), compiler_params=_params("arbitrary"))(slabs, swapped)


def _exchange_scratch(n):
    return [pltpu.SemaphoreType.DMA((7 * n,)), pltpu.SemaphoreType.DMA((7 * n,)), pltpu.SemaphoreType.DMA((n,))]


def _exchange_ops(src_refs, dst_refs, owners, sems):
    send_sems, recv_sems, local_sems = sems
    n = len(src_refs)

    def guarded(a, dev, fn):
        lo, hi = owners[a][:2]
        if (lo, hi) == (0, N_DEV):
            fn()
        else:
            pl.when((dev >= lo) & (dev < hi))(fn)

    def src(a, dev):
        ref = src_refs[a]
        return ref.at[0] if ref.shape[0] == 1 else ref.at[dev - owners[a][0]]

    def run(sending, waiting):
        x, y, c = _position()
        me = 4 * x + 2 * y + c
        for a in range(n):
            by_chip = len(owners[a]) == 3
            slot = (lambda qx, qy, qc: 2 * qx + qy) if by_chip else (lambda qx, qy, qc: 4 * qx + 2 * qy + qc)
            mine = slot(x, y, c)
            local = lambda a=a, mine=mine: pltpu.make_async_copy(src(a, me), dst_refs[a].at[mine], local_sems.at[a])
            if sending:
                guarded(a, me, lambda local=local: local().start())
            for m in range(2, N_DEV, 2) if by_chip else range(1, N_DEV):
                px, py, pc = x ^ (m >> 2), y ^ ((m >> 1) & 1), c ^ (m & 1)
                peer = 4 * px + 2 * py + pc
                theirs = slot(px, py, pc)
                sem = dict(send_sem=send_sems.at[7 * a + m - 1], recv_sem=recv_sems.at[7 * a + m - 1],
                           device_id=(px, py, pc), device_id_type=MESH)
                send = lambda a=a, peer=peer, sem=sem, mine=mine: pltpu.make_async_remote_copy(
                    src_ref=src(a, peer), dst_ref=dst_refs[a].at[mine], **sem)
                recv = lambda a=a, sem=sem, theirs=theirs: pltpu.make_async_remote_copy(
                    src_ref=src(a, me), dst_ref=dst_refs[a].at[theirs], **sem)
                if sending:
                    guarded(a, peer, lambda send=send: send().start())
                if waiting:
                    guarded(a, me, lambda recv=recv: recv().wait_recv())
                    guarded(a, peer, lambda send=send: send().wait_send())
            if waiting:
                guarded(a, me, lambda local=local: local().wait())

    return functools.partial(run, True, False), functools.partial(run, False, True)


def _sum_slabs(r_ref):
    g = r_ref[0].astype(F32)
    for k in range(1, r_ref.shape[0]):
        g = g + r_ref[k].astype(F32)
    return g


def _adamw(g, w, m, v):
    m_new = ADAM_B1 * m + (1.0 - ADAM_B1) * g
    v_new = ADAM_B2 * v + (1.0 - ADAM_B2) * (g * g)
    m_hat = m_new / (1.0 - ADAM_B1 ** ADAM_STEP)
    v_hat = v_new / (1.0 - ADAM_B2 ** ADAM_STEP)
    return g, -ADAM_LR * (m_hat / (jnp.sqrt(v_hat) + ADAM_EPS) + ADAM_WD * w), m_new, v_new


def _adamw_w_in(recv_early, recv_late, w, m, v, slabs, owners):
    rows, cols = w.shape
    tile = W_IN_COL_TILE
    nx = len(slabs)

    def body(early_ref, late_ref, w_ref, m_ref, v_ref, *refs):
        src_refs, o_refs, dst_refs = refs[:nx], refs[nx:nx + 4], refs[nx + 4:2 * nx + 4]
        start, wait = _exchange_ops(src_refs, dst_refs, owners, refs[2 * nx + 4:])
        x, y, c = _position()
        early_owner = 4 * x + 2 * y + c >= EARLY_FROM

        @pl.when(pl.program_id(0) == 0)
        def _():
            start()

        def update(g):
            for o_ref, val in zip(o_refs, _adamw(g, w_ref[...], m_ref[...], v_ref[...])):
                o_ref[...] = val

        pl.when(early_owner)(lambda: update(_sum_slabs(early_ref)))
        pl.when(jnp.logical_not(early_owner))(lambda: update(_sum_slabs(late_ref)))

        @pl.when(pl.program_id(0) == cols // tile - 1)
        def _():
            wait()

    blk = pl.BlockSpec((rows, tile), lambda i: (0, i))
    slots = lambda r: pl.BlockSpec((r.shape[0], rows, tile), lambda i: (0, 0, i))
    out = pl.pallas_call(
        body, name="adamw_w_in", grid=(cols // tile,),
        in_specs=[slots(recv_early), slots(recv_late), blk, blk, blk] + _hbm_specs(nx),
        out_specs=[blk] * 4 + _hbm_specs(nx),
        out_shape=[jax.ShapeDtypeStruct((rows, cols), F32)] * 4 + _received_shapes(slabs, owners),
        scratch_shapes=_exchange_scratch(nx),
        compiler_params=_params("arbitrary"))(recv_early, recv_late, w, m, v, *slabs)
    return out[:4], out[4:]


def _adamw_misc(recvs, recv_small, recv_norm, params):
    names = list(params)
    flat = [a for n in names for a in params[n]]

    def body(woa_ref, wob_ref, wo_ref, lora_ref, small_ref, norm_ref, *refs):
        p_refs, o_refs = refs[:len(flat)], refs[len(flat):]
        g_small = _sum_slabs(small_ref)
        g_lora = _sum_slabs(lora_ref)
        grads = {"w_out_a": _sum_slabs(woa_ref), "w_out_b": _sum_slabs(wob_ref), "w_out": _sum_slabs(wo_ref),
                 "w_lora_up": g_lora[0], "a_lora_up": g_lora[1], "norm_g": _sum_slabs(norm_ref)}
        for n, (off, size) in SMALL_SLOTS.items():
            grads[n] = g_small[:, off:off + size]
        for i, n in enumerate(names):
            w_ref, m_ref, v_ref = p_refs[3 * i:3 * i + 3]
            for o_ref, val in zip(o_refs[4 * i:4 * i + 4], _adamw(grads[n], w_ref[...], m_ref[...], v_ref[...])):
                o_ref[...] = val
        o_refs[-1][...] = g_small[:, LOSS_SLOT:LOSS_SLOT + 1]

    out = pl.pallas_call(
        body, name="adamw_misc",
        out_shape=[jax.ShapeDtypeStruct(params[n][0].shape, F32) for n in names for _ in range(4)]
        + [jax.ShapeDtypeStruct((1, 1), F32)],
        compiler_params=_params())(*recvs, recv_small, recv_norm, *flat)
    return {n: out[4 * i:4 * i + 4] for i, n in enumerate(names)}, out[-1]


_WT_SEGMENTS = ((0, NA), (NA, NB), (NA + NB + H, NG), (NA + NB, H))


def _split_wt(gathered):
    tile = W_IN_COL_TILE

    def body(g_ref, *o_refs):
        full = jnp.concatenate([g_ref[j] for j in range(N_DEV)], axis=0)
        for o_ref, (row, n) in zip(o_refs, _WT_SEGMENTS):
            seg = full[row:row + n]
            if n < o_ref.shape[0]:
                seg = jnp.concatenate([seg, jnp.zeros((o_ref.shape[0] - n, tile), BF16)], axis=0)
            o_ref[...] = seg

    sizes = (NA, NB, NG, NF)
    return pl.pallas_call(
        body, name="split_wt", grid=(D // tile,),
        in_specs=[pl.BlockSpec((N_DEV, COLS_PER_DEV, tile), lambda i: (0, 0, i))],
        out_specs=[pl.BlockSpec((n, tile), lambda i: (0, i)) for n in sizes],
        out_shape=[jax.ShapeDtypeStruct((n, D), BF16) for n in sizes],
        compiler_params=_params("arbitrary"))(gathered)


def _slab_wt_grad(segments, seg_rows, dev_lo, dev_hi, name):
    tile = W_IN_COL_TILE
    k = len(segments)

    def body(*refs):
        seg_refs, o_ref = refs[:k], refs[k]
        for j in range(dev_lo, dev_hi):
            lo, hi = COLS_PER_DEV * j, COLS_PER_DEV * (j + 1)
            parts = []
            for ref, (row, n) in sorted(zip(seg_refs, seg_rows), key=lambda t: t[1][0]):
                first, last = max(lo, row), min(hi, row + n)
                if first < last:
                    parts.append(ref[first - row:last - row, :])
            o_ref[j - dev_lo] = (parts[0] if len(parts) == 1 else jnp.concatenate(parts, axis=0)).astype(BF16)

    return pl.pallas_call(
        body, name=name, grid=(D // tile,),
        in_specs=[pl.BlockSpec((s.shape[0], tile), lambda i: (0, i)) for s in segments],
        out_specs=pl.BlockSpec((dev_hi - dev_lo, COLS_PER_DEV, tile), lambda i: (0, 0, i)),
        out_shape=jax.ShapeDtypeStruct((dev_hi - dev_lo, COLS_PER_DEV, D), BF16),
        compiler_params=_params("arbitrary"))(*segments)


def _by_cols(a):
    return jnp.moveaxis(a, 0, 1).reshape(a.shape[1], -1)


def _col_slabs(a):
    return jnp.moveaxis(a.reshape(a.shape[0], N_DEV, -1), 1, 0).astype(BF16)


def _pack_small(grads, loss):
    pieces, at = [], 0
    for n, (off, size) in list(SMALL_SLOTS.items()) + [("loss", (LOSS_SLOT, 1))]:
        pieces += [jnp.zeros((off - at,), F32), (loss if n == "loss" else grads[n]).reshape(-1)]
        at = off + size
    return jnp.concatenate(pieces + [jnp.zeros((SMALL_LEN - at,), F32)]).reshape(1, 1, SMALL_LEN)


def _gather_weights(t):
    cast = lambda a: a.astype(BF16)
    loras = jnp.stack([t["w_lora_up"][0], t["a_lora_up"][0]])
    wt, woa, wob, wo, lora = _all_gather(
        cast(t["w_in"][0].T), [cast(t["w_out_a"][0]), cast(t["w_out_b"][0]), cast(t["w_out"][0]), cast(loras)],
        "weight_gather")
    in_a, in_b, in_g, in_f = _split_wt(wt)
    return {"in_a": in_a, "in_b": in_b, "in_g": in_g, "in_f": in_f, "w_out_a": _by_cols(woa), "w_out_b": _by_cols(wob),
            "w_out": wo.reshape(D, D), "w_lora_up": lora[:, 0], "a_lora_up": lora[:, 1]}


def kernel(x, norm_g, w_in, shift_mu, w_lora_up, w0, a_lora_up, a0, k_k, k_a, r_k, lnx_w, lnx_b, f_bias, q_norm_g, k_norm_g, w_out_a, w_out_b, w_out, final_norm_g, loss_target, m_norm_g, m_w_in, m_shift_mu, m_w_lora_up, m_w0, m_a_lora_up, m_a0, m_k_k, m_k_a, m_r_k, m_lnx_w, m_lnx_b, m_f_bias, m_q_norm_g, m_k_norm_g, m_w_out_a, m_w_out_b, m_w_out, m_final_norm_g, v_norm_g, v_w_in, v_shift_mu, v_w_lora_up, v_w0, v_a_lora_up, v_a0, v_k_k, v_k_a, v_r_k, v_lnx_w, v_lnx_b, v_f_bias, v_q_norm_g, v_k_norm_g, v_w_out_a, v_w_out_b, v_w_out, v_final_norm_g):
    names = ("norm_g", "w_in", "shift_mu", "w_lora_up", "w0", "a_lora_up", "a0", "k_k", "k_a", "r_k", "lnx_w", "lnx_b",
             "f_bias", "q_norm_g", "k_norm_g", "w_out_a", "w_out_b", "w_out", "final_norm_g")
    weights = dict(zip(names, (norm_g, w_in, shift_mu, w_lora_up, w0, a_lora_up, a0, k_k, k_a, r_k, lnx_w, lnx_b,
                               f_bias, q_norm_g, k_norm_g, w_out_a, w_out_b, w_out, final_norm_g)))
    m_in = dict(zip(names, (m_norm_g, m_w_in, m_shift_mu, m_w_lora_up, m_w0, m_a_lora_up, m_a0, m_k_k, m_k_a, m_r_k,
                            m_lnx_w, m_lnx_b, m_f_bias, m_q_norm_g, m_k_norm_g, m_w_out_a, m_w_out_b, m_w_out,
                            m_final_norm_g)))
    v_in = dict(zip(names, (v_norm_g, v_w_in, v_shift_mu, v_w_lora_up, v_w0, v_a_lora_up, v_a0, v_k_k, v_k_a, v_r_k,
                            v_lnx_w, v_lnx_b, v_f_bias, v_q_norm_g, v_k_norm_g, v_w_out_a, v_w_out_b, v_w_out,
                            v_final_norm_g)))

    matrices = ("w_out_a", "w_out_b", "w_out", "w_lora_up", "a_lora_up")
    as_2d = lambda n, a: a[0] if n in matrices else a.reshape(1, -1)

    full = _gather_weights(weights)
    dx, dng, recv_wt, recvs, recv_small = _local_step(
        x[0], loss_target[0], full, {n: as_2d(n, weights[n]) for n in ("norm_g",) + tuple(SMALL_SLOTS)})

    res, (recv_norm,) = _adamw_w_in(*recv_wt, w_in[0].T, m_w_in[0].T, v_w_in[0].T, (dng[None],), ((0, N_DEV),))
    outs = {"w_in": [r.T[None] for r in res]}
    misc = [n for n in names if n != "w_in"]
    res, loss_sum = _adamw_misc(recvs, recv_small, recv_norm,
                                {n: tuple(as_2d(n, t[n]) for t in (weights, m_in, v_in)) for n in misc})
    for n in misc:
        outs[n] = [r.reshape(weights[n].shape) for r in res[n]]
    return (loss_sum.reshape(()), dx[None], *[outs[n][i] for i in range(4) for n in names])
```

```python
import functools
import math

import jax
import jax.numpy as jnp
from jax import lax
from jax.experimental import pallas as pl
from jax.experimental.pallas import tpu as pltpu

F32 = jnp.float32
BF16 = jnp.bfloat16
HI = lax.Precision.HIGHEST
MESH = pl.DeviceIdType.MESH

N_DEV = 8
D = 1024
H = 8
N = 64
DA = H * N
RANK = 64
NA = 4 * DA + 2 * RANK
NB = 4 * DA
NG = 2 * D
NF = 128
IN_COLS = NA + NB + H + NG
COLS_PER_DEV = IN_COLS // N_DEV
RMS_EPS = 1e-6
LNX_EPS = 64e-5
ATT_SCALE = N ** -0.5

ADAM_LR = 0.001
ADAM_B1 = 0.9
ADAM_B2 = 0.999
ADAM_EPS = 1e-08
ADAM_WD = 0.01
ADAM_STEP = 10

LANES = 128
WKV_CHUNK = 64
TOK_TILE = 256
HEAD_TILE = 128
ATT_TILE = 256
ATT_GROUPS = 8
VMEM_LIMIT = 56 * 1024 * 1024

SMALL_SLOTS = {"final_norm_g": (0, D), "shift_mu": (D, NA), "w0": (3200, DA), "a0": (3712, DA), "k_k": (4224, DA),
               "k_a": (4736, DA), "r_k": (5248, DA), "lnx_w": (5760, DA), "lnx_b": (6272, DA), "q_norm_g": (6784, N),
               "k_norm_g": (6912, N), "f_bias": (7040, H)}
LOSS_SLOT = 7168
SMALL_LEN = 7296
W_IN_COL_TILE = 256
EARLY_FROM = -(-NA // COLS_PER_DEV)


def _params(*sem):
    return pltpu.CompilerParams(dimension_semantics=sem or None, vmem_limit_bytes=VMEM_LIMIT)


def _bdot(a, b):
    return jnp.dot(a.astype(BF16), b.astype(BF16), preferred_element_type=F32)


def _bdot_nt(a, b):
    return lax.dot_general(a.astype(BF16), b.astype(BF16), (((1,), (1,)), ((), ())), preferred_element_type=F32)


def _bdot_tn(a, b):
    return lax.dot_general(a.astype(BF16), b.astype(BF16), (((0,), (0,)), ((), ())), preferred_element_type=F32)


def _sigmoid(x):
    return 1.0 / (1.0 + jnp.exp(-x))


def _softplus(x):
    return jnp.maximum(x, 0.0) + jnp.log(1.0 + jnp.exp(-jnp.abs(x)))


def _heads(ref, col0):
    return jnp.stack([ref[:, col0 + N * h:col0 + N * (h + 1)] for h in range(H)])


def _store_heads(ref, col0, val):
    for h in range(H):
        ref[:, col0 + N * h:col0 + N * (h + 1)] = val[h]


def _lerp(c, s, mu):
    return c + (s - c) * mu


def _head_sums(x):
    low = lax.broadcasted_iota(jnp.int32, (x.shape[0], LANES), 1) < N
    out = []
    for p in range(x.shape[1] // LANES):
        pair = x[:, LANES * p:LANES * (p + 1)]
        first = jnp.sum(jnp.where(low, pair, 0.0), axis=-1, keepdims=True)
        second = jnp.sum(jnp.where(low, 0.0, pair), axis=-1, keepdims=True)
        out.append(jnp.where(low, first, second))
    return jnp.concatenate(out, axis=-1)


def _to_heads(x):
    return [x[:, N * h:N * (h + 1)] for h in range(H)]


def _from_heads(ref):
    return jnp.concatenate([ref[h] for h in range(H)], axis=-1)


def _rwkv_pre(rc, rs, kc, ks, vc, vs, gc, gs, wdc, wds, adc, ads,
              mu_r, mu_k, mu_v, mu_g, mu_wd, mu_ad, w_up, w0, a_up, a0, k_k, k_a):
    r = _lerp(rc, rs, mu_r)
    k = _lerp(kc, ks, mu_k)
    v = _lerp(vc, vs, mu_v)
    g = _lerp(gc, gs, mu_g)
    wd = _lerp(wdc, wds, mu_wd)
    ad = _lerp(adc, ads, mu_ad)
    t = wd.shape[0]
    w_raw = -_softplus(-(w0 + _bdot(jnp.tanh(wd), w_up))) - 0.5
    lw = -jnp.exp(w_raw)
    row = lax.broadcasted_iota(jnp.int32, (t, t), 0)
    col = lax.broadcasted_iota(jnp.int32, (t, t), 1)
    same_chunk = ((row >= col) & (row // WKV_CHUNK == col // WKV_CHUNK)).astype(F32)
    cl = jnp.dot(same_chunk, lw, precision=HI, preferred_element_type=F32)
    alr = _sigmoid(a0 + _bdot(ad, a_up))
    kk = k * k_k
    kk = kk / jnp.maximum(jnp.sqrt(_head_sums(kk * kk)), 1e-12)
    k2 = k * (1.0 + (alr - 1.0) * k_a)
    return r, lw, cl, k2, v, -kk, kk * alr, g


_MM_DIMS = {"nn": (((2,), (1,)), ((0,), (0,))), "nt": (((2,), (2,)), ((0,), (0,))), "tn": (((1,), (1,)), ((0,), (0,)))}


def _split(x):
    hi = x.astype(BF16)
    return hi, (x - hi.astype(F32)).astype(BF16)


def _dot3(a, b, kind):
    ah, al = _split(a)
    bh, bl = _split(b)
    dot = functools.partial(lax.dot_general, dimension_numbers=_MM_DIMS[kind], preferred_element_type=F32)
    return dot(ah, bh) + (dot(ah, bl) + dot(al, bh))


def _dot1(a, b, kind):
    return lax.dot_general(a.astype(BF16), b.astype(BF16), dimension_numbers=_MM_DIMS[kind], preferred_element_type=F32)


@functools.partial(jax.custom_vjp, nondiff_argnums=(2, 3))
def _mm(a, b, kind, fine=True):
    return _dot3(a, b, kind) if fine else _dot1(a, b, kind)


def _mm_fwd(a, b, kind, fine):
    return _mm(a, b, kind, fine), (a, b)


def _mm_bwd(kind, fine, res, ct):
    a, b = res
    if kind == "nn":
        return _dot1(ct, b, "nt"), _dot1(a, ct, "tn")
    if kind == "nt":
        return _dot1(ct, b, "nn"), _dot1(ct, a, "tn")
    return _dot1(b, ct, "nt"), _dot1(a, ct, "nn")


_mm.defvjp(_mm_fwd, _mm_bwd)


def _chunk_masks(c):
    row = lax.broadcasted_iota(jnp.int32, (c, c), 0)
    col = lax.broadcasted_iota(jnp.int32, (c, c), 1)
    return (row >= col)[None], (row > col)[None], (row == col).astype(F32)[None]


def _wkv_aab(fine, lw, cl, a, b):
    _, strict, _ = _chunk_masks(a.shape[1])
    return jnp.where(strict, _mm(a * jnp.exp(cl - lw), b * jnp.exp(-cl), "nt", fine), 0.0)


def _tri_inverse(x):
    c = x.shape[1]
    p = _chunk_masks(c)[2] + x
    for _ in range(int(math.log2(c)) - 1):
        x = _dot1(x, x, "nn")
        p = p + _dot1(p, x, "nn")
    return p


def _wkv_apply(fine, s0, r, lw, cl, k, v, a, b, p):
    c = r.shape[1]
    incl, strict, _ = _chunk_masks(c)
    mm = functools.partial(_mm, fine=fine)
    gi = jnp.exp(-cl)
    left = jnp.concatenate([a * jnp.exp(cl - lw), r * jnp.exp(cl)], axis=1)
    right = jnp.concatenate([b * gi, k * gi], axis=1)
    m = mm(left, right, "nt")
    z0 = mm(left, s0, "nt")
    a_ak = jnp.where(strict, m[:, :c, c:], 0.0)
    row = lax.broadcasted_iota(jnp.int32, (c, 2 * c), 0)
    col = lax.broadcasted_iota(jnp.int32, (c, 2 * c), 1)
    a_r = jnp.where((row >= col % c)[None], m[:, c:, :], 0.0)
    sa = mm(p, z0[:, :c] + mm(a_ak, v, "nn"), "nn")
    sa_v = jnp.concatenate([sa, v], axis=1)
    y = z0[:, c:] + mm(a_r, sa_v, "nn")
    s1 = (s0 + mm(sa_v, right, "tn")) * jnp.exp(cl[:, c - 1:c, :])
    return y, s1


def _rwkv_post(y, r, k2, v, g, lnx_w, lnx_b, r_k):
    yc = y - _head_sums(y) * (1.0 / N)
    var = _head_sums(yc * yc) * (1.0 / N)
    yn = yc * lax.rsqrt(var + LNX_EPS) * lnx_w + lnx_b
    bonus = _head_sums(r * k2 * r_k) * v
    return (yn + bonus) * (g * _sigmoid(g))


def _fox_pre(q, k, f, q_g, k_g, f_b):
    qn = q * lax.rsqrt(_head_sums(q * q) * (1.0 / N) + RMS_EPS) * q_g
    kn = k * lax.rsqrt(_head_sums(k * k) * (1.0 / N) + RMS_EPS) * k_g
    x = f + f_b
    return qn, kn, jnp.minimum(x, 0.0) - jnp.log(1.0 + jnp.exp(-jnp.abs(x)))


def _norm_proj(x, g, wts):
    s = x.shape[0]
    k = len(wts)

    def body(x_ref, g_ref, *refs):
        w_refs, h_ref, o_refs = refs[:k], refs[k], refs[k + 1:]
        xv = x_ref[...]
        h = (xv * lax.rsqrt(jnp.mean(xv * xv, axis=-1, keepdims=True) + RMS_EPS) * g_ref[...]).astype(BF16)
        h_ref[...] = h
        for w_ref, o_ref in zip(w_refs, o_refs):
            o_ref[...] = _bdot_nt(h, w_ref[...])

    tok = lambda n: pl.BlockSpec((TOK_TILE, n), lambda i: (i, 0))
    out = pl.pallas_call(
        body, name="norm_proj", grid=(s // TOK_TILE,),
        in_specs=[tok(D), pl.BlockSpec((1, D), lambda i: (0, 0))] + [pl.BlockSpec(w.shape, lambda i: (0, 0)) for w in wts],
        out_specs=[tok(D)] + [tok(w.shape[0]) for w in wts],
        out_shape=[jax.ShapeDtypeStruct((s, D), BF16)] + [jax.ShapeDtypeStruct((s, w.shape[0]), F32) for w in wts],
        compiler_params=_params("arbitrary"))(x, g, *wts)
    return out[0], out[1:]


def _proj_wgrad_early(h, dub, dug, duf, head_rows):
    s = dub.shape[0]
    steps = s // TOK_TILE
    seg_rows = (_WT_SEGMENTS[1], _WT_SEGMENTS[2], _WT_SEGMENTS[3])

    def body(h_ref, b_ref, g_ref, f_ref, o_ref, head_ref, *accs):
        @pl.when(pl.program_id(0) == 0)
        def _():
            for acc in accs:
                acc[...] = jnp.zeros_like(acc)

        h = h_ref[...]
        for acc, du_ref in zip(accs, (b_ref, g_ref, f_ref)):
            acc[...] += _bdot_tn(du_ref[...], h)

        @pl.when(pl.program_id(0) == steps - 1)
        def _():
            head_ref[...] = accs[0][:head_rows, :]
            for j in range(EARLY_FROM, N_DEV):
                lo, hi = COLS_PER_DEV * j, COLS_PER_DEV * (j + 1)
                parts = []
                for acc, (row, n) in sorted(zip(accs, seg_rows), key=lambda t: t[1][0]):
                    first, last = max(lo, row), min(hi, row + n)
                    if first < last:
                        parts.append(acc[first - row:last - row, :])
                o_ref[j - EARLY_FROM] = (parts[0] if len(parts) == 1 else jnp.concatenate(parts, axis=0)).astype(BF16)

    tok = lambda n: pl.BlockSpec((TOK_TILE, n), lambda i: (i, 0))
    n_early = N_DEV - EARLY_FROM
    return pl.pallas_call(
        body, name="wgrad_bgf", grid=(steps,), in_specs=[tok(D), tok(NB), tok(NG), tok(NF)],
        out_specs=[pl.BlockSpec((n_early, COLS_PER_DEV, D), lambda i: (0, 0, 0)),
                   pl.BlockSpec((head_rows, D), lambda i: (0, 0))],
        out_shape=[jax.ShapeDtypeStruct((n_early, COLS_PER_DEV, D), BF16), jax.ShapeDtypeStruct((head_rows, D), F32)],
        scratch_shapes=[pltpu.VMEM((n, D), F32) for n in (NB, NG, NF)],
        compiler_params=_params("arbitrary"))(h, dub, dug, duf)


def _proj_wgrad_late(h, dua, dwt_b_head):
    s = dua.shape[0]
    steps = s // TOK_TILE

    def body(h_ref, du_ref, b_ref, o_ref, acc):
        @pl.when(pl.program_id(0) == 0)
        def _():
            acc[...] = jnp.zeros_like(acc)

        acc[...] += _bdot_tn(du_ref[...], h_ref[...])

        @pl.when(pl.program_id(0) == steps - 1)
        def _():
            for j in range(EARLY_FROM):
                lo, hi = COLS_PER_DEV * j, COLS_PER_DEV * (j + 1)
                parts = [acc[lo:min(hi, NA), :]] + ([b_ref[:hi - NA, :]] if hi > NA else [])
                o_ref[j] = (parts[0] if len(parts) == 1 else jnp.concatenate(parts, axis=0)).astype(BF16)

    return pl.pallas_call(
        body, name="wgrad_a", grid=(steps,),
        in_specs=[pl.BlockSpec((TOK_TILE, D), lambda i: (i, 0)), pl.BlockSpec((TOK_TILE, NA), lambda i: (i, 0)),
                  pl.BlockSpec(dwt_b_head.shape, lambda i: (0, 0))],
        out_specs=pl.BlockSpec((EARLY_FROM, COLS_PER_DEV, D), lambda i: (0, 0, 0)),
        out_shape=jax.ShapeDtypeStruct((EARLY_FROM, COLS_PER_DEV, D), BF16),
        scratch_shapes=[pltpu.VMEM((NA, D), F32)], compiler_params=_params("arbitrary"))(h, dua, dwt_b_head)


def _proj_xgrad(x, g, dx2, dus, ws, slabs, owners):
    s = x.shape[0]
    tile = HEAD_TILE
    k = len(dus)
    nx = len(slabs)
    n_in = 3 + 2 * k + nx

    def body(*refs):
        x_ref, g_ref, dx2_ref = refs[:3]
        du_refs, w_refs = refs[3:3 + k], refs[3 + k:3 + 2 * k]
        src_refs = refs[3 + 2 * k:3 + 2 * k + nx]
        dx_ref, dg_ref = refs[n_in:n_in + 2]
        dst_refs = refs[n_in + 2:n_in + 2 + nx]
        start, wait = _exchange_ops(src_refs, dst_refs, owners, refs[n_in + 2 + nx:])

        @pl.when(pl.program_id(0) == 0)
        def _():
            dg_ref[...] = jnp.zeros_like(dg_ref)
            start()

        dh = _bdot(du_refs[0][...], w_refs[0][...])
        for du_ref, w_ref in zip(du_refs[1:], w_refs[1:]):
            dh += _bdot(du_ref[...], w_ref[...])
        xv = x_ref[...]
        rs = lax.rsqrt(jnp.mean(xv * xv, axis=-1, keepdims=True) + RMS_EPS)
        xn = xv * rs
        dg_ref[...] += jnp.sum(dh * xn, axis=0, keepdims=True)
        dxn = dh * g_ref[...]
        dx_ref[...] = rs * (dxn - xn * jnp.mean(dxn * xn, axis=-1, keepdims=True)) + dx2_ref[...]

        @pl.when(pl.program_id(0) == s // tile - 1)
        def _():
            wait()

    tok = lambda n: pl.BlockSpec((tile, n), lambda i: (i, 0))
    fixed = lambda a: pl.BlockSpec(a.shape, lambda i: (0,) * a.ndim)
    out = pl.pallas_call(
        body, name="proj_xgrad", grid=(s // tile,),
        in_specs=([tok(D), fixed(g), tok(D)] + [tok(du.shape[1]) for du in dus] + [fixed(w) for w in ws]
                  + _hbm_specs(nx)),
        out_specs=[tok(D), pl.BlockSpec((1, D), lambda i: (0, 0))] + _hbm_specs(nx),
        out_shape=[jax.ShapeDtypeStruct((s, D), F32), jax.ShapeDtypeStruct((1, D), F32)] + _received_shapes(slabs, owners),
        scratch_shapes=_exchange_scratch(nx),
        compiler_params=_params("arbitrary"))(x, g, dx2, *dus, *ws, *slabs)
    return out[0], out[1], out[2:]


def _tail(x, target, ya, o, ub, ug, w_oa, w_ob, w_o, fg):
    s = x.shape[0]
    tile = TOK_TILE

    def body(x_ref, t_ref, ya_ref, o_ref, gb_ref, ug_ref, woa_ref, wob_ref, wo_ref, fg_ref,
             loss_ref, dfg_ref, dwo_ref, dwoa_ref, dwob_ref, dx2_ref, dya_ref, do_ref, dgb_ref, dug_ref):
        @pl.when(pl.program_id(0) == 0)
        def _():
            for r in (loss_ref, dfg_ref, dwo_ref, dwoa_ref, dwob_ref):
                r[...] = jnp.zeros_like(r)

        ya_v = ya_ref[...]
        gate_b = gb_ref[...]
        sg_b = _sigmoid(gate_b)
        silu_b = gate_b * sg_b
        o_v = jnp.concatenate([o_ref[h] for h in range(H)], axis=-1)
        yb_v = o_v * silu_b
        big_a = _bdot(ya_v, woa_ref[...])
        big_b = _bdot(yb_v, wob_ref[...])
        sa = _sigmoid(ug_ref[:, :D])
        sb = _sigmoid(ug_ref[:, D:])
        merged = sa * big_a + sb * big_b
        x2 = x_ref[...] + _bdot(merged, wo_ref[...])
        rs = lax.rsqrt(jnp.mean(x2 * x2, axis=-1, keepdims=True) + RMS_EPS)
        xn = x2 * rs
        err = xn * fg_ref[...] - t_ref[...]
        loss_ref[...] += (0.5 / D) * jnp.sum(err * err)
        dout = err * (1.0 / D)
        dfg_ref[...] += jnp.sum(dout * xn, axis=0, keepdims=True)
        dxn = dout * fg_ref[...]
        dx2 = rs * (dxn - xn * jnp.mean(dxn * xn, axis=-1, keepdims=True))
        dx2_ref[...] = dx2
        dwo_ref[...] += _bdot_tn(merged, dx2)
        dmerged = _bdot_nt(dx2, wo_ref[...])
        dbig_a = dmerged * sa
        dbig_b = dmerged * sb
        dug_ref[:, :D] = dmerged * big_a * sa * (1.0 - sa)
        dug_ref[:, D:] = dmerged * big_b * sb * (1.0 - sb)
        dwoa_ref[...] += _bdot_tn(ya_v, dbig_a)
        dwob_ref[...] += _bdot_tn(yb_v, dbig_b)
        dya_ref[...] = _bdot_nt(dbig_a, woa_ref[...])
        dyb = _bdot_nt(dbig_b, wob_ref[...])
        dgb_ref[...] = dyb * o_v * (sg_b * (1.0 + gate_b * (1.0 - sg_b)))
        _dov = dyb * silu_b
        for h in range(H):
            do_ref[h] = _dov[:, N * h:N * (h + 1)]

    tok = lambda n: pl.BlockSpec((tile, n), lambda i: (i, 0))
    hm = pl.BlockSpec((H, tile, N), lambda i: (0, i, 0))
    fixed = lambda shape: pl.BlockSpec(shape, lambda i: (0,) * len(shape))
    f32 = lambda *shape: jax.ShapeDtypeStruct(shape, F32)
    return pl.pallas_call(
        body, name="tail", grid=(s // tile,),
        in_specs=[tok(D), tok(D), tok(DA), hm, pl.BlockSpec((tile, DA), lambda i: (i, 3)), tok(NG),
                  fixed((DA, D)), fixed((DA, D)), fixed((D, D)), fixed((1, D))],
        out_specs=[fixed((1, 1)), fixed((1, D)), fixed((D, D)), fixed((DA, D)), fixed((DA, D)),
                   tok(D), tok(DA), hm, tok(DA), tok(NG)],
        out_shape=[f32(1, 1), f32(1, D), f32(D, D), f32(DA, D), f32(DA, D),
                   f32(s, D), f32(s, DA), f32(H, s, N), f32(s, DA), f32(s, NG)],
        compiler_params=_params("arbitrary"))(x, target, ya, o, ub, ug, w_oa, w_ob, w_o, fg)


def _pre_operands(ua_ref, prev_ref, first):
    cur = ua_ref[...]
    t = cur.shape[0]
    prev_row = jnp.where(first, 0.0, prev_ref[7:8, :])
    rows = lax.broadcasted_iota(jnp.int32, cur.shape, 0)
    sh = jnp.where(rows == 0, prev_row, pltpu.roll(cur, 1, axis=0))
    ops = []
    for c0, n in ((0, DA), (DA, DA), (2 * DA, DA), (3 * DA + 2 * RANK, DA), (3 * DA, RANK), (3 * DA + RANK, RANK)):
        ops += [cur[:, c0:c0 + n], sh[:, c0:c0 + n]]
    del t
    return ops


def _ua_specs(tile, order):
    blocks = tile // 8
    return [pl.BlockSpec((tile, NA), lambda i: (order(i), 0)),
            pl.BlockSpec((8, NA), lambda i: (jnp.maximum(order(i) * blocks - 1, 0), 0))]


def _rwkv_pre_fwd(ua, pre_params):
    s = ua.shape[0]
    tile = HEAD_TILE

    def body(ua_ref, prev_ref, *refs):
        p_refs, o_refs = refs[:len(pre_params)], refs[len(pre_params):]
        ops = _pre_operands(ua_ref, prev_ref, pl.program_id(0) == 0)
        outs = _rwkv_pre(*ops, *[p[...] for p in p_refs])
        for o_ref, val in zip(o_refs, outs):
            o_ref[...] = val

    tm = pl.BlockSpec((tile, DA), lambda i: (i, 0))
    return pl.pallas_call(
        body, name="rwkv_pre_fwd", grid=(s // tile,),
        in_specs=_ua_specs(tile, lambda i: i) + [pl.BlockSpec(p.shape, lambda i, nd=p.ndim: (0,) * nd) for p in pre_params],
        out_specs=[tm] * 8, out_shape=[jax.ShapeDtypeStruct((s, DA), F32)] * 8,
        compiler_params=_params("arbitrary"))(ua, ua, *pre_params)


def _rwkv_pre_bwd(ua, pre_params, cots):
    s = ua.shape[0]
    tile = HEAD_TILE
    nt = s // tile
    n_p = len(pre_params)

    def body(ua_ref, prev_ref, *refs):
        p_refs, c_refs = refs[:n_p], refs[n_p:n_p + 11]
        dua_ref = refs[n_p + 11]
        dp_refs = refs[n_p + 12:n_p + 12 + n_p]
        carry_ref = refs[-1]
        i = pl.program_id(0)

        @pl.when(i == 0)
        def _():
            carry_ref[...] = jnp.zeros_like(carry_ref)
            for r in dp_refs:
                r[...] = jnp.zeros_like(r)

        ops = _pre_operands(ua_ref, prev_ref, i == nt - 1)
        _, vjp = jax.vjp(_rwkv_pre, *ops, *[p[...] for p in p_refs])
        c = [r[...] for r in c_refs]
        grads = vjp((c[0] + c[1], c[2], c[3], c[4] + c[5], c[6] + c[7], c[8], c[9], c[10]))
        d_ops, d_par = grads[:12], grads[12:]
        for r, val in zip(dp_refs, d_par):
            r[...] += val
        d_cur = jnp.concatenate([d_ops[0], d_ops[2], d_ops[4], d_ops[8], d_ops[10], d_ops[6]], axis=-1)
        d_sh = jnp.concatenate([d_ops[1], d_ops[3], d_ops[5], d_ops[9], d_ops[11], d_ops[7]], axis=-1)
        rows = lax.broadcasted_iota(jnp.int32, d_sh.shape, 0)
        dua_ref[...] = d_cur + jnp.where(rows == tile - 1, carry_ref[...], pltpu.roll(d_sh, tile - 1, axis=0))
        carry_ref[...] = d_sh[0:1, :]

    rev = lambda i: nt - 1 - i
    tm = pl.BlockSpec((tile, DA), lambda i: (rev(i), 0))
    fixed = [pl.BlockSpec(p.shape, lambda i, nd=p.ndim: (0,) * nd) for p in pre_params]
    return pl.pallas_call(
        body, name="rwkv_pre_bwd", grid=(nt,),
        in_specs=_ua_specs(tile, rev) + fixed + [tm] * 11,
        out_specs=[pl.BlockSpec((tile, NA), lambda i: (rev(i), 0))] + fixed,
        out_shape=[jax.ShapeDtypeStruct((s, NA), F32)] + [jax.ShapeDtypeStruct(p.shape, F32) for p in pre_params],
        scratch_shapes=[pltpu.VMEM((1, NA), F32)],
        compiler_params=_params("arbitrary"))(ua, ua, *pre_params, *cots)


def _wkv_fwd(seq):
    s = seq[0].shape[0]
    nc = s // WKV_CHUNK

    def body(r_ref, lw_ref, cl_ref, k_ref, v_ref, a_ref, b_ref, y_ref, ck_ref, p_ref, state):
        @pl.when(pl.program_id(0) == 0)
        def _():
            state[...] = jnp.zeros_like(state)

        r, lw, cl, k, v, a, b = (jnp.stack(_to_heads(ref[...])) for ref in (r_ref, lw_ref, cl_ref, k_ref, v_ref, a_ref,
                                                                             b_ref))
        s0 = state[...]
        ck_ref[0] = s0
        p = _tri_inverse(_wkv_aab(True, lw, cl, a, b))
        p_ref[0] = p
        y, s1 = _wkv_apply(True, s0, r, lw, cl, k, v, a, b, p)
        y_ref[...] = jnp.concatenate([y[h] for h in range(H)], axis=-1)
        state[...] = s1

    tm = pl.BlockSpec((WKV_CHUNK, DA), lambda c: (c, 0))
    per_chunk = lambda m: pl.BlockSpec((1, H, m, m), lambda c: (c, 0, 0, 0))
    return pl.pallas_call(
        body, name="wkv_fwd", grid=(nc,), in_specs=[tm] * 7,
        out_specs=[tm, per_chunk(N), per_chunk(WKV_CHUNK)],
        out_shape=[jax.ShapeDtypeStruct((s, DA), F32), jax.ShapeDtypeStruct((nc, H, N, N), F32),
                   jax.ShapeDtypeStruct((nc, H, WKV_CHUNK, WKV_CHUNK), F32)],
        scratch_shapes=[pltpu.VMEM((H, N, N), F32)], compiler_params=_params("arbitrary"))(*seq)


def _wkv_bwd(seq, ckpt, pinv, dy, slabs, owners):
    s = seq[0].shape[0]
    nc = s // WKV_CHUNK
    nx = len(slabs)

    def body(r_ref, lw_ref, cl_ref, k_ref, v_ref, a_ref, b_ref, ck_ref, p_ref, dy_ref, *refs):
        src_refs, d_refs, dst_refs = refs[:nx], refs[nx:nx + 7], refs[nx + 7:2 * nx + 7]
        dstate = refs[2 * nx + 7]
        start, wait = _exchange_ops(src_refs, dst_refs, owners, refs[2 * nx + 8:])

        @pl.when(pl.program_id(0) == 0)
        def _():
            dstate[...] = jnp.zeros_like(dstate)
            start()

        p = p_ref[0]
        r, lw, cl, k, v, a, b, dy = (jnp.stack(_to_heads(ref[...])) for ref in (r_ref, lw_ref, cl_ref, k_ref, v_ref,
                                                                                 a_ref, b_ref, dy_ref))
        _, vjp = jax.vjp(functools.partial(_wkv_apply, False), ck_ref[0], r, lw, cl, k, v, a, b, p)
        ds0, dr, dlw, dcl, dk, dv, da, db, dp = vjp((dy, dstate[...]))
        dstate[...] = ds0
        _, vjp_x = jax.vjp(functools.partial(_wkv_aab, False), lw, cl, a, b)
        dlw2, dcl2, da2, db2 = vjp_x(_dot1(_dot1(p, dp, "tn"), p, "nt"))
        for d_ref, val in zip(d_refs, (dr, dlw + dlw2, dcl + dcl2, dk, dv, da + da2, db + db2)):
            d_ref[...] = jnp.concatenate([val[h] for h in range(H)], axis=-1)

        @pl.when(pl.program_id(0) == nc - 1)
        def _():
            wait()

    tm = pl.BlockSpec((WKV_CHUNK, DA), lambda c: (nc - 1 - c, 0))
    per_chunk = lambda m: pl.BlockSpec((1, H, m, m), lambda c: (nc - 1 - c, 0, 0, 0))
    out = pl.pallas_call(
        body, name="wkv_bwd", grid=(nc,),
        in_specs=[tm] * 7 + [per_chunk(N), per_chunk(WKV_CHUNK), tm] + _hbm_specs(nx),
        out_specs=[tm] * 7 + _hbm_specs(nx),
        out_shape=[jax.ShapeDtypeStruct((s, DA), F32)] * 7 + _received_shapes(slabs, owners),
        scratch_shapes=[pltpu.VMEM((H, N, N), F32)] + _exchange_scratch(nx),
        compiler_params=_params("arbitrary"))(*seq, ckpt, pinv, dy, *slabs)
    return out[:7], out[7:]


def _rwkv_post_fwd(y, r, k2, v, g, post_params):
    s = y.shape[0]
    tile = TOK_TILE

    def body(*refs):
        refs[-1][...] = _rwkv_post(*[ref[...] for ref in refs[:-1]])

    tm = pl.BlockSpec((tile, DA), lambda i: (i, 0))
    par = pl.BlockSpec((1, DA), lambda i: (0, 0))
    return pl.pallas_call(
        body, name="rwkv_post_fwd", grid=(s // tile,), in_specs=[tm] * 5 + [par] * 3,
        out_specs=tm, out_shape=jax.ShapeDtypeStruct((s, DA), F32),
        compiler_params=_params("arbitrary"))(y, r, k2, v, g, *post_params)


def _rwkv_post_bwd(y, r, k2, v, g, post_params, dya, slabs, lo):
    s = y.shape[0]
    tile = HEAD_TILE

    def body(y_ref, r_ref, k_ref, v_ref, g_ref, w_ref, b_ref, rk_ref, dya_ref, s_ref, *refs):
        d_refs, p_ref = refs[:8], refs[8]
        start, wait = _pair_swap_ops(s_ref, p_ref, lo, refs[9:])

        @pl.when(pl.program_id(0) == 0)
        def _():
            for ref in d_refs[5:]:
                ref[...] = jnp.zeros_like(ref)
            start()

        _, vjp = jax.vjp(_rwkv_post, *[ref[...] for ref in (y_ref, r_ref, k_ref, v_ref, g_ref, w_ref, b_ref, rk_ref)])
        grads = vjp(dya_ref[...])
        for ref, val in zip(d_refs[:5], grads[:5]):
            ref[...] = val
        for ref, val in zip(d_refs[5:], grads[5:]):
            ref[...] += val

        @pl.when(pl.program_id(0) == s // tile - 1)
        def _():
            wait()

    tm = pl.BlockSpec((tile, DA), lambda i: (i, 0))
    par = pl.BlockSpec((1, DA), lambda i: (0, 0))
    return pl.pallas_call(
        body, name="rwkv_post_bwd", grid=(s // tile,),
        in_specs=[tm] * 5 + [par] * 3 + [tm] + _hbm_specs(1),
        out_specs=[tm] * 5 + [par] * 3 + _hbm_specs(1),
        out_shape=[jax.ShapeDtypeStruct((s, DA), F32)] * 5 + [jax.ShapeDtypeStruct((1, DA), F32)] * 3
        + [jax.ShapeDtypeStruct(slabs.shape, slabs.dtype)],
        scratch_shapes=_pair_swap_scratch(slabs.shape[0]),
        compiler_params=_params("arbitrary"))(y, r, k2, v, g, *post_params, dya, slabs)


def _tri(t):
    return (lax.broadcasted_iota(jnp.int32, (t, t), 0) >= lax.broadcasted_iota(jnp.int32, (t, t), 1)).astype(F32)


def _fox_pre_fwd(ub, uf, q_g, k_g, f_b):
    s = ub.shape[0]
    tile = HEAD_TILE

    def body(ub_ref, uf_ref, qg_ref, kg_ref, fb_ref, q_ref, k_ref, v_ref, cum_ref, carry):
        @pl.when(pl.program_id(0) == 0)
        def _():
            carry[...] = jnp.zeros_like(carry)

        qn, kn, logf = _fox_pre(ub_ref[:, :DA], ub_ref[:, DA:2 * DA], uf_ref[...], qg_ref[...], kg_ref[...],
                                fb_ref[...])
        for h, (q_col, k_col) in enumerate(zip(_to_heads(qn), _to_heads(kn))):
            q_ref[h] = q_col
            k_ref[h] = k_col
        v_ref[...] = _heads(ub_ref, 2 * DA)
        cum = jnp.dot(_tri(tile), logf, precision=HI, preferred_element_type=F32) + carry[...]
        cum_ref[...] = cum
        carry[...] = cum[tile - 1:tile, :]

    hm = pl.BlockSpec((H, tile, N), lambda i: (0, i, 0))
    fixed = lambda shape: pl.BlockSpec(shape, lambda i: (0,) * len(shape))
    return pl.pallas_call(
        body, name="fox_pre_fwd", grid=(s // tile,),
        in_specs=[pl.BlockSpec((tile, NB), lambda i: (i, 0)), pl.BlockSpec((tile, NF), lambda i: (i, 0)),
                  fixed((1, DA)), fixed((1, DA)), fixed((1, NF))],
        out_specs=[hm] * 3 + [pl.BlockSpec((tile, NF), lambda i: (i, 0))],
        out_shape=[jax.ShapeDtypeStruct((H, s, N), F32)] * 3 + [jax.ShapeDtypeStruct((s, NF), F32)],
        scratch_shapes=[pltpu.VMEM((1, NF), F32)], compiler_params=_params("arbitrary"))(ub, uf, q_g, k_g, f_b)


def _fox_pre_bwd(ub, uf, q_g, k_g, f_b, dqn, dkn, dvf, dgate, dcum_q, dcum_k):
    s = ub.shape[0]
    tile = HEAD_TILE
    nt = s // tile

    def body(ub_ref, uf_ref, qg_ref, kg_ref, fb_ref, dq_ref, dk_ref, dv_ref, dgate_ref, dcq_ref, dck_ref,
             dub_ref, duf_ref, dqg_ref, dkg_ref, dfb_ref, carry):
        @pl.when(pl.program_id(0) == 0)
        def _():
            carry[...] = jnp.zeros_like(carry)
            for ref in (dqg_ref, dkg_ref, dfb_ref):
                ref[...] = jnp.zeros_like(ref)

        dcum = dcq_ref[...] + dck_ref[...]
        dlogf = lax.dot_general(_tri(tile), dcum, (((0,), (0,)), ((), ())), precision=HI,
                                preferred_element_type=F32) + carry[...]
        carry[...] = dlogf[0:1, :]
        _, vjp = jax.vjp(_fox_pre, ub_ref[:, :DA], ub_ref[:, DA:2 * DA], uf_ref[...], qg_ref[...], kg_ref[...],
                         fb_ref[...])
        d_q, d_k, d_f, d_qg, d_kg, d_fb = vjp((_from_heads(dq_ref), _from_heads(dk_ref), dlogf))
        dub_ref[:, :DA] = d_q
        dub_ref[:, DA:2 * DA] = d_k
        _store_heads(dub_ref, 2 * DA, dv_ref[...])
        dub_ref[:, 3 * DA:] = dgate_ref[...]
        duf_ref[...] = d_f
        dqg_ref[...] += functools.reduce(jnp.add, _to_heads(d_qg))
        dkg_ref[...] += functools.reduce(jnp.add, _to_heads(d_kg))
        dfb_ref[...] += d_fb

    rev = lambda i: nt - 1 - i
    hm = pl.BlockSpec((H, tile, N), lambda i: (0, rev(i), 0))
    tok = lambda n: pl.BlockSpec((tile, n), lambda i: (rev(i), 0))
    fixed = lambda shape: pl.BlockSpec(shape, lambda i: (0,) * len(shape))
    return pl.pallas_call(
        body, name="fox_pre_bwd", grid=(nt,),
        in_specs=[tok(NB), tok(NF), fixed((1, DA)), fixed((1, DA)), fixed((1, NF)), hm, hm, hm, tok(DA), tok(NF),
                  tok(NF)],
        out_specs=[tok(NB), tok(NF), fixed((1, N)), fixed((1, N)), fixed((1, NF))],
        out_shape=[jax.ShapeDtypeStruct((s, NB), F32), jax.ShapeDtypeStruct((s, NF), F32),
                   jax.ShapeDtypeStruct((1, N), F32), jax.ShapeDtypeStruct((1, N), F32),
                   jax.ShapeDtypeStruct((1, NF), F32)],
        scratch_shapes=[pltpu.VMEM((1, NF), F32)],
        compiler_params=_params("arbitrary"))(ub, uf, q_g, k_g, f_b, dqn, dkn, dvf, dgate, dcum_q, dcum_k)


def _att_groups(s):
    blocks = s // ATT_TILE
    per = max(1, blocks // ATT_GROUPS)
    return per, blocks // per


def _att_parts(n, width):
    return ([(0, n - width, False)] if n > width else []) + [(n - width, n, True)]


def _att_scores(q_bf, k_ref, ck_ref, lo, hi, masked, row_offset):
    scores = _bdot_nt(q_bf, k_ref[0, lo:hi, :]) - ck_ref[0, :, lo:hi]
    if masked:
        rows = row_offset + lax.broadcasted_iota(jnp.int32, scores.shape, 0)
        scores = jnp.where(rows >= lax.broadcasted_iota(jnp.int32, scores.shape, 1), scores, -1e30)
    return scores


def _fox_attn_fwd(q, k, v, cum_q, cum_k):
    s = q.shape[1]
    t = ATT_TILE
    per, groups = _att_groups(s)

    def body(q_ref, k_ref, v_ref, cq_ref, ck_ref, o_ref, lse_ref):
        qi = pl.program_id(1)
        for g in range(groups):
            @pl.when(qi // per == g)
            def _(g=g):
                q_bf = (q_ref[0] * ATT_SCALE).astype(BF16)
                parts = _att_parts((g + 1) * per * t, per * t)
                scores = [_att_scores(q_bf, k_ref, ck_ref, lo, hi, masked, (qi - g * per) * t)
                          for lo, hi, masked in parts]
                m = functools.reduce(jnp.maximum, [jnp.max(sc, axis=-1, keepdims=True) for sc in scores])
                l, acc = 0.0, 0.0
                for sc, (lo, hi, _) in zip(scores, parts):
                    p = jnp.exp(sc - m)
                    l += jnp.sum(p, axis=-1, keepdims=True)
                    acc += _bdot(p, v_ref[0, lo:hi, :])
                o_ref[0] = acc / l
                lse_ref[0] = m + jnp.log(l) + cq_ref[0]

    qb = pl.BlockSpec((1, t, N), lambda h, i: (h, i, 0))
    kb = pl.BlockSpec((1, s, N), lambda h, i: (h, 0, 0))
    return pl.pallas_call(
        body, name="fox_attn_fwd", grid=(H, s // t),
        in_specs=[qb, kb, kb, pl.BlockSpec((1, t, 1), lambda h, i: (h, i, 0)),
                  pl.BlockSpec((1, 1, s), lambda h, i: (h, 0, 0))],
        out_specs=[qb, pl.BlockSpec((1, t, 1), lambda h, i: (h, i, 0))],
        out_shape=[jax.ShapeDtypeStruct((H, s, N), F32), jax.ShapeDtypeStruct((H, s, 1), F32)],
        compiler_params=_params("arbitrary", "arbitrary"))(q, k, v, cum_q, cum_k)


def _fox_attn_bwd(q, k, v, cum_q, cum_k, o, lse, do, slabs, owners):
    s = q.shape[1]
    t = ATT_TILE
    per, groups = _att_groups(s)
    nx = len(slabs)

    def body(q_ref, k_ref, v_ref, cq_ref, ck_ref, o_ref, lse_ref, do_ref, *refs):
        src_refs, (dq_ref, dk_ref, dv_ref, dcq_ref, dck_ref) = refs[:nx], refs[nx:nx + 5]
        start, wait = _exchange_ops(src_refs, refs[nx + 5:2 * nx + 5], owners, refs[2 * nx + 5:])
        qi = pl.program_id(1)

        @pl.when((pl.program_id(0) == 0) & (qi == 0))
        def _():
            start()

        @pl.when(qi == 0)
        def _():
            for ref in (dk_ref, dv_ref, dck_ref):
                ref[...] = jnp.zeros_like(ref)

        for g in range(groups):
            @pl.when(qi // per == g)
            def _(g=g):
                q_bf, do_bf = (q_ref[0] * ATT_SCALE).astype(BF16), do_ref[0].astype(BF16)
                row_term = cq_ref[0] - lse_ref[0]
                delta = jnp.sum(do_ref[0] * o_ref[0], axis=-1, keepdims=True)
                dq, dcq = 0.0, 0.0
                for lo, hi, masked in _att_parts((g + 1) * per * t, per * t):
                    p = jnp.exp(_att_scores(q_bf, k_ref, ck_ref, lo, hi, masked, (qi - g * per) * t) + row_term)
                    ds = p * (_bdot_nt(do_bf, v_ref[0, lo:hi, :]) - delta)
                    dq += _bdot(ds, k_ref[0, lo:hi, :])
                    dcq += jnp.sum(ds, axis=-1, keepdims=True)
                    dk_ref[0, lo:hi, :] += _bdot_tn(ds, q_bf)
                    dv_ref[0, lo:hi, :] += _bdot_tn(p, do_bf)
                    dck_ref[0, :, lo:hi] -= jnp.sum(ds, axis=0, keepdims=True)
                dq_ref[0] = dq * ATT_SCALE
                dcq_ref[0] = dcq

        @pl.when((pl.program_id(0) == H - 1) & (qi == s // t - 1))
        def _():
            wait()

    qb = pl.BlockSpec((1, t, N), lambda h, i: (h, i, 0))
    kb = pl.BlockSpec((1, s, N), lambda h, i: (h, 0, 0))
    cqb = pl.BlockSpec((1, t, 1), lambda h, i: (h, i, 0))
    ckb = pl.BlockSpec((1, 1, s), lambda h, i: (h, 0, 0))
    f32 = lambda *shape: jax.ShapeDtypeStruct(shape, F32)
    out = pl.pallas_call(
        body, name="fox_attn_bwd", grid=(H, s // t),
        in_specs=[qb, kb, kb, cqb, ckb, qb, cqb, qb] + _hbm_specs(nx), out_specs=[qb, kb, kb, cqb, ckb] + _hbm_specs(nx),
        out_shape=[f32(H, s, N), f32(H, s, N), f32(H, s, N), f32(H, s, 1), f32(H, 1, s)]
        + _received_shapes(slabs, owners),
        scratch_shapes=_exchange_scratch(nx),
        compiler_params=_params("arbitrary", "arbitrary"))(q, k, v, cum_q, cum_k, o, lse, do, *slabs)
    return out[:5], out[5:]


def _local_step(x, target, w, p):
    mu = p["shift_mu"]
    lora_matrix = lambda a: jnp.moveaxis(a, 0, 1).reshape(RANK, DA).astype(F32)
    pre_params = (mu[:, 0:DA], mu[:, DA:2 * DA], mu[:, 2 * DA:3 * DA], mu[:, 3 * DA + 2 * RANK:],
                  mu[:, 3 * DA:3 * DA + RANK], mu[:, 3 * DA + RANK:3 * DA + 2 * RANK],
                  lora_matrix(w["w_lora_up"]), p["w0"], lora_matrix(w["a_lora_up"]), p["a0"], p["k_k"], p["k_a"])
    post_params = (p["lnx_w"], p["lnx_b"], p["r_k"])
    q_g, k_g = jnp.tile(p["q_norm_g"], (1, H)), jnp.tile(p["k_norm_g"], (1, H))
    f_b = jnp.pad(p["f_bias"], ((0, 0), (0, NF - H)))
    fg = p["final_norm_g"].reshape(1, D)

    h, (ua, ub, ug, uf) = _norm_proj(x, p["norm_g"], (w["in_a"], w["in_b"], w["in_g"], w["in_f"]))
    r, lw, cl, k2, v, av, bv, gg = _rwkv_pre_fwd(ua, pre_params)
    y, ckpt, pinv = _wkv_fwd((r, lw, cl, k2, v, av, bv))
    ya = _rwkv_post_fwd(y, r, k2, v, gg, post_params)
    qn, kn, vf, cum = _fox_pre_fwd(ub, uf, q_g, k_g, f_b)
    cum_t = cum[:, :H].T
    cum_q, cum_k = cum_t[:, :, None], cum_t[:, None, :]
    o, lse = _fox_attn_fwd(qn, kn, vf, cum_q, cum_k)

    (loss, dfg, dwo, dwoa, dwob, dx2, dya, do, dgate_b, dug) = _tail(
        x, target, ya, o, ub, ug, w["w_out_a"], w["w_out_b"], w["w_out"], fg)
    everyone = (0, N_DEV)
    (dqn, dkn, dvf, dcq, dck), (recv_woa, recv_wob, recv_wo) = _fox_attn_bwd(
        qn, kn, vf, cum_q, cum_k, o, lse, do,
        (_col_slabs(dwoa), _col_slabs(dwob), dwo.astype(BF16).reshape(N_DEV, D // N_DEV, D)), (everyone,) * 3)
    pad_f = lambda a: jnp.pad(a.T, ((0, 0), (0, NF - H)))
    dub, duf, dqg, dkg, dfb = _fox_pre_bwd(ub, uf, q_g, k_g, f_b, dqn, dkn, dvf, dgate_b,
                                           pad_f(dcq[:, :, 0]), pad_f(dck.reshape(H, -1)))
    spill = EARLY_FROM * COLS_PER_DEV - NA
    early, dwt_b_head = _proj_wgrad_early(h, dub, dug, duf, -(-spill // 8) * 8)
    dy, dr_p, dk_p, dv_p, dgg, dlnw, dlnb, drk, handed = _rwkv_post_bwd(y, r, k2, v, gg, post_params, dya, early,
                                                                          EARLY_FROM)
    early = _chip_sums(early, handed, EARLY_FROM, "chip_sums_early")
    (dr_s, dlw, dcl, dk_s, dv_s, dav, dbv), (recv_early,) = _wkv_bwd(
        (r, lw, cl, k2, v, av, bv), ckpt, pinv, dy, (early,), ((EARLY_FROM, N_DEV, "chips"),))
    pre_out = _rwkv_pre_bwd(ua, pre_params, (dr_s, dr_p, dlw, dcl, dk_s, dk_p, dv_s, dv_p, dav, dbv, dgg))
    dua, dpre = pre_out[0], pre_out[1:]
    late = _proj_wgrad_late(h, dua, dwt_b_head)

    flat = lambda a: a.reshape(1, -1)
    small = {
        "final_norm_g": dfg, "w0": dpre[7], "a0": dpre[9], "k_k": dpre[10], "k_a": dpre[11], "r_k": drk, "lnx_w": dlnw,
        "lnx_b": dlnb, "q_norm_g": dqg, "k_norm_g": dkg, "f_bias": dfb[:, :H],
        "shift_mu": jnp.concatenate([flat(dpre[0]), flat(dpre[1]), flat(dpre[2]), dpre[4], dpre[5], flat(dpre[3])], axis=1),
    }
    late = _chip_sums(late, _pair_swap(late, 0, "pair_swap_late"), 0, "chip_sums_late")
    by_head = lambda a: jnp.moveaxis(a.reshape(RANK, H, N), 1, 0)
    loras = jnp.stack([by_head(dpre[6]), by_head(dpre[8])], axis=1).astype(BF16)
    dx, dng, (recv_late, recv_lora, recv_small) = _proj_xgrad(
        x, p["norm_g"], dx2, (dua, dub, dug, duf), (w["in_a"], w["in_b"], w["in_g"], w["in_f"]),
        (late, loras, _pack_small(small, loss)), ((0, EARLY_FROM, "chips"), everyone, everyone))
    return dx, dng, (recv_early, recv_late), (recv_woa, recv_wob, recv_wo, recv_lora), recv_small


def _position():
    return lax.axis_index("x"), lax.axis_index("y"), lax.axis_index("c")


def _hbm_specs(n):
    return [pl.BlockSpec(memory_space=pl.ANY)] * n


BIG_GATHER_COPIES = 13
GATHER_ROW_CUT = 400


def _all_gather(big, blocks, name):
    n = len(blocks)

    def body(*refs):
        big_ref, x_refs = refs[0], refs[1:1 + n]
        big_out, out_refs = refs[1 + n], refs[2 + n:2 + 2 * n]
        send_sems, recv_sems, local_sems = refs[2 + 2 * n:]
        x, y, c = _position()
        me, sibling = (x, y, c), (x, y, 1 - c)
        chips = [(1 - x, y), (x, 1 - y), (1 - x, 1 - y)]
        x_nbr, y_nbr, diag = chips
        rows = big_ref.shape[0]
        cut = GATHER_ROW_CUT

        def part(ref, h):
            return ref if h is None else ref.at[pl.ds(0, cut)] if h == 0 else ref.at[pl.ds(cut, rows - cut)]

        def landed(chip, core, h):
            return part(big_out.at[4 * chip[0] + 2 * chip[1] + core], h)

        def big_copy(k, src, dst, to):
            return pltpu.make_async_remote_copy(src_ref=src, dst_ref=dst, send_sem=send_sems.at[7 * n + k],
                                                recv_sem=recv_sems.at[7 * n + k], device_id=to, device_id_type=MESH)

        def arrival(k, chip, core, h):
            dst = landed(chip, core, h)
            return big_copy(k, dst, dst, me)

        def pass_on(k, chip, h, to):
            src = landed(chip, c, h)
            return big_copy(k, src, src, to)

        big_mine = pltpu.make_async_copy(big_ref, landed((x, y), c, None), local_sems.at[n])
        big_mine.start()
        here = (x, y)
        big_sent = [big_copy(0, big_ref, landed(here, c, None), sibling),
                    big_copy(1, part(big_ref, 0), landed(here, c, 0), (*x_nbr, c)),
                    big_copy(2, part(big_ref, 1), landed(here, c, 1), (*y_nbr, c)),
                    big_copy(3, part(big_ref, 1), landed(here, c, 1), (*x_nbr, c)),
                    big_copy(4, part(big_ref, 0), landed(here, c, 0), (*y_nbr, c))]
        for cp in big_sent:
            cp.start()

        def copy(a, k, blk, to, own=False):
            dst = out_refs[a].at[4 * blk[0] + 2 * blk[1] + blk[2]]
            return pltpu.make_async_remote_copy(
                src_ref=x_refs[a] if own else dst, dst_ref=dst, send_sem=send_sems.at[7 * a + k],
                recv_sem=recv_sems.at[7 * a + k], device_id=to, device_id_type=MESH)

        mine = [pltpu.make_async_copy(x_refs[a], out_refs[a].at[4 * x + 2 * y + c], local_sems.at[a]) for a in range(n)]
        for cp in mine:
            cp.start()
        first = []
        for a in range(n):
            first.append(copy(a, 0, me, sibling, own=True))
            first += [copy(a, 1 + j, me, (*chip, c), own=True) for j, chip in enumerate(chips)]
        for cp in first:
            cp.start()

        big_steps = [(1, x_nbr, 0, (*y_nbr, c), 5, 7), (2, y_nbr, 1, (*x_nbr, c), 6, 8), (3, x_nbr, 1, None, None, 9),
                     (4, y_nbr, 0, None, None, 10), (5, diag, 0, None, None, 11), (6, diag, 1, None, None, 12)]
        for k, chip, h, onward, k_onward, k_sibling in big_steps:
            arrival(k, chip, c, h).wait_recv()
            if onward is not None:
                big_sent.append(pass_on(k_onward, chip, h, onward))
                big_sent[-1].start()
            big_sent.append(pass_on(k_sibling, chip, h, sibling))
            big_sent[-1].start()

        passed = []
        for j, chip in enumerate(chips):
            for a in range(n):
                copy(a, 1 + j, (*chip, c), me).wait_recv()
                passed.append(copy(a, 4 + j, (*chip, c), sibling))
                passed[-1].start()
        for a in range(n):
            copy(a, 0, sibling, me).wait_recv()
        for j, chip in enumerate(chips):
            for a in range(n):
                copy(a, 4 + j, (*chip, 1 - c), me).wait_recv()
        arrival(0, here, 1 - c, None).wait_recv()
        for k, chip, h, _, _, k_sibling in big_steps:
            arrival(k_sibling, chip, 1 - c, h).wait_recv()
        for cp in first + passed + big_sent:
            cp.wait_send()
        for cp in mine + [big_mine]:
            cp.wait()

    everything = [big] + list(blocks)
    return pl.pallas_call(
        body, name=name, out_shape=[jax.ShapeDtypeStruct((N_DEV,) + b.shape, b.dtype) for b in everything],
        in_specs=_hbm_specs(n + 1), out_specs=_hbm_specs(n + 1),
        scratch_shapes=[pltpu.SemaphoreType.DMA((7 * n + BIG_GATHER_COPIES,)),
                        pltpu.SemaphoreType.DMA((7 * n + BIG_GATHER_COPIES,)), pltpu.SemaphoreType.DMA((n + 1,))],
    )(*everything)


def _received_shapes(slabs, owners):
    return [jax.ShapeDtypeStruct((N_DEV // 2 if len(o) == 3 else N_DEV,) + s.shape[1:], s.dtype)
            for s, o in zip(slabs, owners)]


def _pair_swap_scratch(n):
    return [pltpu.SemaphoreType.DMA((n,)), pltpu.SemaphoreType.DMA((n,))]


def _pair_swap_ops(s_ref, p_ref, lo, sems):
    send_sems, recv_sems = sems
    n = s_ref.shape[0]

    def run(sending):
        x, y, c = _position()
        for side in (0, 1):
            mine = [pltpu.make_async_remote_copy(src_ref=s_ref.at[i], dst_ref=p_ref.at[i], send_sem=send_sems.at[i],
                                                 recv_sem=recv_sems.at[i], device_id=(x, y, 1 - c), device_id_type=MESH)
                    for i in range(n) if (lo + i) % 2 == side]

            @pl.when(c != side)
            def _():
                for cp in mine:
                    cp.start() if sending else cp.wait_send()

            if not sending:
                @pl.when(c == side)
                def _():
                    for cp in mine:
                        cp.wait_recv()

    return functools.partial(run, True), functools.partial(run, False)


def _pair_swap(slabs, lo, name):
    n = slabs.shape[0]

    def body(s_ref, p_ref, *sems):
        start, wait = _pair_swap_ops(s_ref, p_ref, lo, sems)
        start()
        wait()

    return pl.pallas_call(
        body, name=name, out_shape=jax.ShapeDtypeStruct(slabs.shape, slabs.dtype),
        in_specs=_hbm_specs(1), out_specs=_hbm_specs(1)[0], scratch_shapes=_pair_swap_scratch(n))(slabs)


def _chip_sums(slabs, swapped, lo, name):
    n, rows, cols = slabs.shape
    tile = W_IN_COL_TILE

    def body(s_ref, p_ref, o_ref):
        c = lax.axis_index("c")
        for i in range(n):
            @pl.when(c == (lo + i) % 2)
            def _(i=i):
                o_ref[i] = (s_ref[i].astype(F32) + p_ref[i].astype(F32)).astype(BF16)

    blk = pl.BlockSpec((n, rows, tile), lambda j: (0, 0, j))
    return pl.pallas_call(
        body, name=name, grid=(cols // tile,), in_specs=[blk, blk], out_specs=blk,
        out_shape=jax.ShapeDtypeStruct(slabs.shape, BF16), compiler_params=_params("arbitrary"))(slabs, swapped)


def _exchange_scratch(n):
    return [pltpu.SemaphoreType.DMA((7 * n,)), pltpu.SemaphoreType.DMA((7 * n,)), pltpu.SemaphoreType.DMA((n,))]


def _exchange_ops(src_refs, dst_refs, owners, sems):
    send_sems, recv_sems, local_sems = sems
    n = len(src_refs)

    def guarded(a, dev, fn):
        lo, hi = owners[a][:2]
        if (lo, hi) == (0, N_DEV):
            fn()
        else:
            pl.when((dev >= lo) & (dev < hi))(fn)

    def src(a, dev):
        ref = src_refs[a]
        return ref.at[0] if ref.shape[0] == 1 else ref.at[dev - owners[a][0]]

    def run(sending, waiting):
        x, y, c = _position()
        me = 4 * x + 2 * y + c
        for a in range(n):
            by_chip = len(owners[a]) == 3
            slot = (lambda qx, qy, qc: 2 * qx + qy) if by_chip else (lambda qx, qy, qc: 4 * qx + 2 * qy + qc)
            mine = slot(x, y, c)
            local = lambda a=a, mine=mine: pltpu.make_async_copy(src(a, me), dst_refs[a].at[mine], local_sems.at[a])
            if sending:
                guarded(a, me, lambda local=local: local().start())
            for m in range(2, N_DEV, 2) if by_chip else range(1, N_DEV):
                px, py, pc = x ^ (m >> 2), y ^ ((m >> 1) & 1), c ^ (m & 1)
                peer = 4 * px + 2 * py + pc
                theirs = slot(px, py, pc)
                sem = dict(send_sem=send_sems.at[7 * a + m - 1], recv_sem=recv_sems.at[7 * a + m - 1],
                           device_id=(px, py, pc), device_id_type=MESH)
                send = lambda a=a, peer=peer, sem=sem, mine=mine: pltpu.make_async_remote_copy(
                    src_ref=src(a, peer), dst_ref=dst_refs[a].at[mine], **sem)
                recv = lambda a=a, sem=sem, theirs=theirs: pltpu.make_async_remote_copy(
                    src_ref=src(a, me), dst_ref=dst_refs[a].at[theirs], **sem)
                if sending:
                    guarded(a, peer, lambda send=send: send().start())
                if waiting:
                    guarded(a, me, lambda recv=recv: recv().wait_recv())
                    guarded(a, peer, lambda send=send: send().wait_send())
            if waiting:
                guarded(a, me, lambda local=local: local().wait())

    return functools.partial(run, True, False), functools.partial(run, False, True)


def _sum_slabs(r_ref):
    g = r_ref[0].astype(F32)
    for k in range(1, r_ref.shape[0]):
        g = g + r_ref[k].astype(F32)
    return g


def _adamw(g, w, m, v):
    m_new = ADAM_B1 * m + (1.0 - ADAM_B1) * g
    v_new = ADAM_B2 * v + (1.0 - ADAM_B2) * (g * g)
    m_hat = m_new / (1.0 - ADAM_B1 ** ADAM_STEP)
    v_hat = v_new / (1.0 - ADAM_B2 ** ADAM_STEP)
    return g, -ADAM_LR * (m_hat / (jnp.sqrt(v_hat) + ADAM_EPS) + ADAM_WD * w), m_new, v_new


def _adamw_w_in(recv_early, recv_late, w, m, v, slabs, owners):
    rows, cols = w.shape
    tile = W_IN_COL_TILE
    nx = len(slabs)

    def body(early_ref, late_ref, w_ref, m_ref, v_ref, *refs):
        src_refs, o_refs, dst_refs = refs[:nx], refs[nx:nx + 4], refs[nx + 4:2 * nx + 4]
        start, wait = _exchange_ops(src_refs, dst_refs, owners, refs[2 * nx + 4:])
        x, y, c = _position()
        early_owner = 4 * x + 2 * y + c >= EARLY_FROM

        @pl.when(pl.program_id(0) == 0)
        def _():
            start()

        def update(g):
            for o_ref, val in zip(o_refs, _adamw(g, w_ref[...], m_ref[...], v_ref[...])):
                o_ref[...] = val

        pl.when(early_owner)(lambda: update(_sum_slabs(early_ref)))
        pl.when(jnp.logical_not(early_owner))(lambda: update(_sum_slabs(late_ref)))

        @pl.when(pl.program_id(0) == cols // tile - 1)
        def _():
            wait()

    blk = pl.BlockSpec((rows, tile), lambda i: (0, i))
    slots = lambda r: pl.BlockSpec((r.shape[0], rows, tile), lambda i: (0, 0, i))
    out = pl.pallas_call(
        body, name="adamw_w_in", grid=(cols // tile,),
        in_specs=[slots(recv_early), slots(recv_late), blk, blk, blk] + _hbm_specs(nx),
        out_specs=[blk] * 4 + _hbm_specs(nx),
        out_shape=[jax.ShapeDtypeStruct((rows, cols), F32)] * 4 + _received_shapes(slabs, owners),
        scratch_shapes=_exchange_scratch(nx),
        compiler_params=_params("arbitrary"))(recv_early, recv_late, w, m, v, *slabs)
    return out[:4], out[4:]


def _adamw_misc(recvs, recv_small, recv_norm, params):
    names = list(params)
    flat = [a for n in names for a in params[n]]

    def body(woa_ref, wob_ref, wo_ref, lora_ref, small_ref, norm_ref, *refs):
        p_refs, o_refs = refs[:len(flat)], refs[len(flat):]
        g_small = _sum_slabs(small_ref)
        g_lora = _sum_slabs(lora_ref)
        grads = {"w_out_a": _sum_slabs(woa_ref), "w_out_b": _sum_slabs(wob_ref), "w_out": _sum_slabs(wo_ref),
                 "w_lora_up": g_lora[0], "a_lora_up": g_lora[1], "norm_g": _sum_slabs(norm_ref)}
        for n, (off, size) in SMALL_SLOTS.items():
            grads[n] = g_small[:, off:off + size]
        for i, n in enumerate(names):
            w_ref, m_ref, v_ref = p_refs[3 * i:3 * i + 3]
            for o_ref, val in zip(o_refs[4 * i:4 * i + 4], _adamw(grads[n], w_ref[...], m_ref[...], v_ref[...])):
                o_ref[...] = val
        o_refs[-1][...] = g_small[:, LOSS_SLOT:LOSS_SLOT + 1]

    out = pl.pallas_call(
        body, name="adamw_misc",
        out_shape=[jax.ShapeDtypeStruct(params[n][0].shape, F32) for n in names for _ in range(4)]
        + [jax.ShapeDtypeStruct((1, 1), F32)],
        compiler_params=_params())(*recvs, recv_small, recv_norm, *flat)
    return {n: out[4 * i:4 * i + 4] for i, n in enumerate(names)}, out[-1]


_WT_SEGMENTS = ((0, NA), (NA, NB), (NA + NB + H, NG), (NA + NB, H))


def _split_wt(gathered):
    tile = W_IN_COL_TILE

    def body(g_ref, *o_refs):
        full = jnp.concatenate([g_ref[j] for j in range(N_DEV)], axis=0)
        for o_ref, (row, n) in zip(o_refs, _WT_SEGMENTS):
            seg = full[row:row + n]
            if n < o_ref.shape[0]:
                seg = jnp.concatenate([seg, jnp.zeros((o_ref.shape[0] - n, tile), BF16)], axis=0)
            o_ref[...] = seg

    sizes = (NA, NB, NG, NF)
    return pl.pallas_call(
        body, name="split_wt", grid=(D // tile,),
        in_specs=[pl.BlockSpec((N_DEV, COLS_PER_DEV, tile), lambda i: (0, 0, i))],
        out_specs=[pl.BlockSpec((n, tile), lambda i: (0, i)) for n in sizes],
        out_shape=[jax.ShapeDtypeStruct((n, D), BF16) for n in sizes],
        compiler_params=_params("arbitrary"))(gathered)


def _by_cols(a):
    return jnp.moveaxis(a, 0, 1).reshape(a.shape[1], -1)


def _col_slabs(a):
    return jnp.moveaxis(a.reshape(a.shape[0], N_DEV, -1), 1, 0).astype(BF16)


def _pack_small(grads, loss):
    pieces, at = [], 0
    for n, (off, size) in list(SMALL_SLOTS.items()) + [("loss", (LOSS_SLOT, 1))]:
        pieces += [jnp.zeros((off - at,), F32), (loss if n == "loss" else grads[n]).reshape(-1)]
        at = off + size
    return jnp.concatenate(pieces + [jnp.zeros((SMALL_LEN - at,), F32)]).reshape(1, 1, SMALL_LEN)


def _gather_weights(t):
    cast = lambda a: a.astype(BF16)
    loras = jnp.stack([t["w_lora_up"][0], t["a_lora_up"][0]])
    wt, woa, wob, wo, lora = _all_gather(
        cast(t["w_in"][0].T), [cast(t["w_out_a"][0]), cast(t["w_out_b"][0]), cast(t["w_out"][0]), cast(loras)],
        "weight_gather")
    in_a, in_b, in_g, in_f = _split_wt(wt)
    return {"in_a": in_a, "in_b": in_b, "in_g": in_g, "in_f": in_f, "w_out_a": _by_cols(woa), "w_out_b": _by_cols(wob),
            "w_out": wo.reshape(D, D), "w_lora_up": lora[:, 0], "a_lora_up": lora[:, 1]}


def kernel(x, norm_g, w_in, shift_mu, w_lora_up, w0, a_lora_up, a0, k_k, k_a, r_k, lnx_w, lnx_b, f_bias, q_norm_g, k_norm_g, w_out_a, w_out_b, w_out, final_norm_g, loss_target, m_norm_g, m_w_in, m_shift_mu, m_w_lora_up, m_w0, m_a_lora_up, m_a0, m_k_k, m_k_a, m_r_k, m_lnx_w, m_lnx_b, m_f_bias, m_q_norm_g, m_k_norm_g, m_w_out_a, m_w_out_b, m_w_out, m_final_norm_g, v_norm_g, v_w_in, v_shift_mu, v_w_lora_up, v_w0, v_a_lora_up, v_a0, v_k_k, v_k_a, v_r_k, v_lnx_w, v_lnx_b, v_f_bias, v_q_norm_g, v_k_norm_g, v_w_out_a, v_w_out_b, v_w_out, v_final_norm_g):
    names = ("norm_g", "w_in", "shift_mu", "w_lora_up", "w0", "a_lora_up", "a0", "k_k", "k_a", "r_k", "lnx_w", "lnx_b",
             "f_bias", "q_norm_g", "k_norm_g", "w_out_a", "w_out_b", "w_out", "final_norm_g")
    weights = dict(zip(names, (norm_g, w_in, shift_mu, w_lora_up, w0, a_lora_up, a0, k_k, k_a, r_k, lnx_w, lnx_b,
                               f_bias, q_norm_g, k_norm_g, w_out_a, w_out_b, w_out, final_norm_g)))
    m_in = dict(zip(names, (m_norm_g, m_w_in, m_shift_mu, m_w_lora_up, m_w0, m_a_lora_up, m_a0, m_k_k, m_k_a, m_r_k,
                            m_lnx_w, m_lnx_b, m_f_bias, m_q_norm_g, m_k_norm_g, m_w_out_a, m_w_out_b, m_w_out,
                            m_final_norm_g)))
    v_in = dict(zip(names, (v_norm_g, v_w_in, v_shift_mu, v_w_lora_up, v_w0, v_a_lora_up, v_a0, v_k_k, v_k_a, v_r_k,
                            v_lnx_w, v_lnx_b, v_f_bias, v_q_norm_g, v_k_norm_g, v_w_out_a, v_w_out_b, v_w_out,
                            v_final_norm_g)))

    matrices = ("w_out_a", "w_out_b", "w_out", "w_lora_up", "a_lora_up")
    as_2d = lambda n, a: a[0] if n in matrices else a.reshape(1, -1)

    full = _gather_weights(weights)
    dx, dng, recv_wt, recvs, recv_small = _local_step(
        x[0], loss_target[0], full, {n: as_2d(n, weights[n]) for n in ("norm_g",) + tuple(SMALL_SLOTS)})

    res, (recv_norm,) = _adamw_w_in(*recv_wt, w_in[0].T, m_w_in[0].T, v_w_in[0].T, (dng[None],), ((0, N_DEV),))
    outs = {"w_in": [r.T[None] for r in res]}
    misc = [n for n in names if n != "w_in"]
    res, loss_sum = _adamw_misc(recvs, recv_small, recv_norm,
                                {n: tuple(as_2d(n, t[n]) for t in (weights, m_in, v_in)) for n in misc})
    for n in misc:
        outs[n] = [r.reshape(weights[n].shape) for r in res[n]]
    return (loss_sum.reshape(()), dx[None], *[outs[n][i] for i in range(4) for n in names])
```

```python
import functools
import math

import jax
import jax.numpy as jnp
from jax import lax
from jax.experimental import pallas as pl
from jax.experimental.pallas import tpu as pltpu

F32 = jnp.float32
BF16 = jnp.bfloat16
HI = lax.Precision.HIGHEST
MESH = pl.DeviceIdType.MESH

N_DEV = 8
D = 1024
H = 8
N = 64
DA = H * N
RANK = 64
NA = 4 * DA + 2 * RANK
NB = 4 * DA
NG = 2 * D
NF = 128
IN_COLS = NA + NB + H + NG
COLS_PER_DEV = IN_COLS // N_DEV
RMS_EPS = 1e-6
LNX_EPS = 64e-5
ATT_SCALE = N ** -0.5

ADAM_LR = 0.001
ADAM_B1 = 0.9
ADAM_B2 = 0.999
ADAM_EPS = 1e-08
ADAM_WD = 0.01
ADAM_STEP = 10

LANES = 128
WKV_CHUNK = 64
TOK_TILE = 256
HEAD_TILE = 256
XGRAD_TILE = 128
ATT_TILE = 256
ATT_GROUPS = 8
VMEM_LIMIT = 56 * 1024 * 1024


def _lane_tile_slots(sizes):
    slots, at = {}, 0
    for name, size in sizes:
        slots[name] = (at, size)
        at += -(-size // LANES) * LANES
    return slots, at


SMALL_SLOTS, LOSS_SLOT = _lane_tile_slots((
    ("final_norm_g", D), ("shift_mu", NA), ("w0", DA), ("a0", DA), ("k_k", DA), ("k_a", DA), ("r_k", DA), ("lnx_w", DA),
    ("lnx_b", DA), ("q_norm_g", N), ("k_norm_g", N), ("f_bias", H)))
SMALL_LEN = LOSS_SLOT + LANES
W_IN_COL_TILE = 256
EARLY_FROM = -(-NA // COLS_PER_DEV)


def _params(*sem):
    return pltpu.CompilerParams(dimension_semantics=sem or None, vmem_limit_bytes=VMEM_LIMIT)


def _bdot(a, b):
    return jnp.dot(a.astype(BF16), b.astype(BF16), preferred_element_type=F32)


def _bdot_nt(a, b):
    return lax.dot_general(a.astype(BF16), b.astype(BF16), (((1,), (1,)), ((), ())), preferred_element_type=F32)


def _bdot_tn(a, b):
    return lax.dot_general(a.astype(BF16), b.astype(BF16), (((0,), (0,)), ((), ())), preferred_element_type=F32)


def _sigmoid(x):
    return 1.0 / (1.0 + jnp.exp(-x))


def _softplus(x):
    return jnp.maximum(x, 0.0) + jnp.log(1.0 + jnp.exp(-jnp.abs(x)))


def _heads(ref, col0):
    return jnp.stack([ref[:, col0 + N * h:col0 + N * (h + 1)] for h in range(H)])


def _store_heads(ref, col0, val):
    for h in range(H):
        ref[:, col0 + N * h:col0 + N * (h + 1)] = val[h]


def _lerp(c, s, mu):
    return c + (s - c) * mu


def _head_sums(x):
    low = lax.broadcasted_iota(jnp.int32, (x.shape[0], LANES), 1) < N
    out = []
    for p in range(x.shape[1] // LANES):
        pair = x[:, LANES * p:LANES * (p + 1)]
        first = jnp.sum(jnp.where(low, pair, 0.0), axis=-1, keepdims=True)
        second = jnp.sum(jnp.where(low, 0.0, pair), axis=-1, keepdims=True)
        out.append(jnp.where(low, first, second))
    return jnp.concatenate(out, axis=-1)


def _to_heads(x):
    return [x[:, N * h:N * (h + 1)] for h in range(H)]


def _from_heads(ref):
    return jnp.concatenate([ref[h] for h in range(H)], axis=-1)


def _rwkv_pre(rc, rs, kc, ks, vc, vs, gc, gs, wdc, wds, adc, ads,
              mu_r, mu_k, mu_v, mu_g, mu_wd, mu_ad, w_up, w0, a_up, a0, k_k, k_a):
    r = _lerp(rc, rs, mu_r)
    k = _lerp(kc, ks, mu_k)
    v = _lerp(vc, vs, mu_v)
    g = _lerp(gc, gs, mu_g)
    wd = _lerp(wdc, wds, mu_wd)
    ad = _lerp(adc, ads, mu_ad)
    t = wd.shape[0]
    w_raw = -_softplus(-(w0 + _bdot(jnp.tanh(wd), w_up))) - 0.5
    lw = -jnp.exp(w_raw)
    row = lax.broadcasted_iota(jnp.int32, (t, t), 0)
    col = lax.broadcasted_iota(jnp.int32, (t, t), 1)
    same_chunk = ((row >= col) & (row // WKV_CHUNK == col // WKV_CHUNK)).astype(F32)
    cl = jnp.dot(same_chunk, lw, precision=HI, preferred_element_type=F32)
    alr = _sigmoid(a0 + _bdot(ad, a_up))
    kk = k * k_k
    kk = kk / jnp.maximum(jnp.sqrt(_head_sums(kk * kk)), 1e-12)
    k2 = k * (1.0 + (alr - 1.0) * k_a)
    return r, lw, cl, k2, v, -kk, kk * alr, g


_MM_DIMS = {"nn": (((2,), (1,)), ((0,), (0,))), "nt": (((2,), (2,)), ((0,), (0,))), "tn": (((1,), (1,)), ((0,), (0,)))}


def _split(x):
    hi = x.astype(BF16)
    return hi, (x - hi.astype(F32)).astype(BF16)


def _dot3(a, b, kind):
    ah, al = _split(a)
    bh, bl = _split(b)
    dot = functools.partial(lax.dot_general, dimension_numbers=_MM_DIMS[kind], preferred_element_type=F32)
    return dot(ah, bh) + (dot(ah, bl) + dot(al, bh))


def _dot1(a, b, kind):
    return lax.dot_general(a.astype(BF16), b.astype(BF16), dimension_numbers=_MM_DIMS[kind], preferred_element_type=F32)


@functools.partial(jax.custom_vjp, nondiff_argnums=(2, 3))
def _mm(a, b, kind, fine=True):
    return _dot3(a, b, kind) if fine else _dot1(a, b, kind)


def _mm_fwd(a, b, kind, fine):
    return _mm(a, b, kind, fine), (a, b)


def _mm_bwd(kind, fine, res, ct):
    a, b = res
    if kind == "nn":
        return _dot1(ct, b, "nt"), _dot1(a, ct, "tn")
    if kind == "nt":
        return _dot1(ct, b, "nn"), _dot1(ct, a, "tn")
    return _dot1(b, ct, "nt"), _dot1(a, ct, "nn")


_mm.defvjp(_mm_fwd, _mm_bwd)


def _chunk_masks(c):
    row = lax.broadcasted_iota(jnp.int32, (c, c), 0)
    col = lax.broadcasted_iota(jnp.int32, (c, c), 1)
    return (row >= col)[None], (row > col)[None], (row == col).astype(F32)[None]


def _wkv_aab(fine, lw, cl, a, b):
    _, strict, _ = _chunk_masks(a.shape[1])
    return jnp.where(strict, _mm(a * jnp.exp(cl - lw), b * jnp.exp(-cl), "nt", fine), 0.0)


def _tri_inverse(x):
    c = x.shape[1]
    p = _chunk_masks(c)[2] + x
    for _ in range(int(math.log2(c)) - 1):
        x = _dot1(x, x, "nn")
        p = p + _dot1(p, x, "nn")
    return p


def _wkv_apply(fine, s0, r, lw, cl, k, v, a, b, p):
    c = r.shape[1]
    incl, strict, _ = _chunk_masks(c)
    mm = functools.partial(_mm, fine=fine)
    gi = jnp.exp(-cl)
    left = jnp.concatenate([a * jnp.exp(cl - lw), r * jnp.exp(cl)], axis=1)
    right = jnp.concatenate([b * gi, k * gi], axis=1)
    m = mm(left, right, "nt")
    z0 = mm(left, s0, "nt")
    a_ak = jnp.where(strict, m[:, :c, c:], 0.0)
    row = lax.broadcasted_iota(jnp.int32, (c, 2 * c), 0)
    col = lax.broadcasted_iota(jnp.int32, (c, 2 * c), 1)
    a_r = jnp.where((row >= col % c)[None], m[:, c:, :], 0.0)
    sa = mm(p, z0[:, :c] + mm(a_ak, v, "nn"), "nn")
    sa_v = jnp.concatenate([sa, v], axis=1)
    y = z0[:, c:] + mm(a_r, sa_v, "nn")
    s1 = (s0 + mm(sa_v, right, "tn")) * jnp.exp(cl[:, c - 1:c, :])
    return y, s1


def _rwkv_post(y, r, k2, v, g, lnx_w, lnx_b, r_k):
    yc = y - _head_sums(y) * (1.0 / N)
    var = _head_sums(yc * yc) * (1.0 / N)
    yn = yc * lax.rsqrt(var + LNX_EPS) * lnx_w + lnx_b
    bonus = _head_sums(r * k2 * r_k) * v
    return (yn + bonus) * (g * _sigmoid(g))


def _fox_pre(q, k, f, q_g, k_g, f_b):
    qn = q * lax.rsqrt(_head_sums(q * q) * (1.0 / N) + RMS_EPS) * q_g
    kn = k * lax.rsqrt(_head_sums(k * k) * (1.0 / N) + RMS_EPS) * k_g
    x = f + f_b
    return qn, kn, jnp.minimum(x, 0.0) - jnp.log(1.0 + jnp.exp(-jnp.abs(x)))


def _norm_proj(x, g, wts):
    s = x.shape[0]
    k = len(wts)

    def body(x_ref, g_ref, *refs):
        w_refs, h_ref, o_refs = refs[:k], refs[k], refs[k + 1:]
        xv = x_ref[...]
        h = (xv * lax.rsqrt(jnp.mean(xv * xv, axis=-1, keepdims=True) + RMS_EPS) * g_ref[...]).astype(BF16)
        h_ref[...] = h
        for w_ref, o_ref in zip(w_refs, o_refs):
            o_ref[...] = _bdot_nt(h, w_ref[...])

    tok = lambda n: pl.BlockSpec((TOK_TILE, n), lambda i: (i, 0))
    out = pl.pallas_call(
        body, name="norm_proj", grid=(s // TOK_TILE,),
        in_specs=[tok(D), pl.BlockSpec((1, D), lambda i: (0, 0))] + [pl.BlockSpec(w.shape, lambda i: (0, 0)) for w in wts],
        out_specs=[tok(D)] + [tok(w.shape[0]) for w in wts],
        out_shape=[jax.ShapeDtypeStruct((s, D), BF16)] + [jax.ShapeDtypeStruct((s, w.shape[0]), F32) for w in wts],
        compiler_params=_params("arbitrary"))(x, g, *wts)
    return out[0], out[1:]


def _proj_wgrad_early(h, dub, dug, duf, head_rows):
    s = dub.shape[0]
    steps = s // TOK_TILE
    seg_rows = (_WT_SEGMENTS[1], _WT_SEGMENTS[2], _WT_SEGMENTS[3])

    def body(h_ref, b_ref, g_ref, f_ref, o_ref, head_ref, *accs):
        @pl.when(pl.program_id(0) == 0)
        def _():
            for acc in accs:
                acc[...] = jnp.zeros_like(acc)

        h = h_ref[...]
        for acc, du_ref in zip(accs, (b_ref, g_ref, f_ref)):
            acc[...] += _bdot_tn(du_ref[...], h)

        @pl.when(pl.program_id(0) == steps - 1)
        def _():
            head_ref[...] = accs[0][:head_rows, :]
            for j in range(EARLY_FROM, N_DEV):
                lo, hi = COLS_PER_DEV * j, COLS_PER_DEV * (j + 1)
                parts = []
                for acc, (row, n) in sorted(zip(accs, seg_rows), key=lambda t: t[1][0]):
                    first, last = max(lo, row), min(hi, row + n)
                    if first < last:
                        parts.append(acc[first - row:last - row, :])
                o_ref[j - EARLY_FROM] = (parts[0] if len(parts) == 1 else jnp.concatenate(parts, axis=0)).astype(BF16)

    tok = lambda n: pl.BlockSpec((TOK_TILE, n), lambda i: (i, 0))
    n_early = N_DEV - EARLY_FROM
    return pl.pallas_call(
        body, name="wgrad_bgf", grid=(steps,), in_specs=[tok(D), tok(NB), tok(NG), tok(NF)],
        out_specs=[pl.BlockSpec((n_early, COLS_PER_DEV, D), lambda i: (0, 0, 0)),
                   pl.BlockSpec((head_rows, D), lambda i: (0, 0))],
        out_shape=[jax.ShapeDtypeStruct((n_early, COLS_PER_DEV, D), BF16), jax.ShapeDtypeStruct((head_rows, D), F32)],
        scratch_shapes=[pltpu.VMEM((n, D), F32) for n in (NB, NG, NF)],
        compiler_params=_params("arbitrary"))(h, dub, dug, duf)


def _proj_wgrad_late(h, dua, dwt_b_head):
    s = dua.shape[0]
    steps = s // TOK_TILE

    def body(h_ref, du_ref, b_ref, o_ref, acc):
        @pl.when(pl.program_id(0) == 0)
        def _():
            acc[...] = jnp.zeros_like(acc)

        acc[...] += _bdot_tn(du_ref[...], h_ref[...])

        @pl.when(pl.program_id(0) == steps - 1)
        def _():
            for j in range(EARLY_FROM):
                lo, hi = COLS_PER_DEV * j, COLS_PER_DEV * (j + 1)
                parts = [acc[lo:min(hi, NA), :]] + ([b_ref[:hi - NA, :]] if hi > NA else [])
                o_ref[j] = (parts[0] if len(parts) == 1 else jnp.concatenate(parts, axis=0)).astype(BF16)

    return pl.pallas_call(
        body, name="wgrad_a", grid=(steps,),
        in_specs=[pl.BlockSpec((TOK_TILE, D), lambda i: (i, 0)), pl.BlockSpec((TOK_TILE, NA), lambda i: (i, 0)),
                  pl.BlockSpec(dwt_b_head.shape, lambda i: (0, 0))],
        out_specs=pl.BlockSpec((EARLY_FROM, COLS_PER_DEV, D), lambda i: (0, 0, 0)),
        out_shape=jax.ShapeDtypeStruct((EARLY_FROM, COLS_PER_DEV, D), BF16),
        scratch_shapes=[pltpu.VMEM((NA, D), F32)], compiler_params=_params("arbitrary"))(h, dua, dwt_b_head)


def _proj_xgrad(x, g, dx2, dus, ws, slabs, owners):
    s = x.shape[0]
    tile = XGRAD_TILE
    k = len(dus)
    nx = len(slabs)
    n_in = 3 + 2 * k + nx

    def body(*refs):
        x_ref, g_ref, dx2_ref = refs[:3]
        du_refs, w_refs = refs[3:3 + k], refs[3 + k:3 + 2 * k]
        src_refs = refs[3 + 2 * k:3 + 2 * k + nx]
        dx_ref, dg_ref = refs[n_in:n_in + 2]
        dst_refs = refs[n_in + 2:n_in + 2 + nx]
        start, wait = _exchange_ops(src_refs, dst_refs, owners, refs[n_in + 2 + nx:])

        @pl.when(pl.program_id(0) == 0)
        def _():
            dg_ref[...] = jnp.zeros_like(dg_ref)
            start()

        dh = _bdot(du_refs[0][...], w_refs[0][...])
        for du_ref, w_ref in zip(du_refs[1:], w_refs[1:]):
            dh += _bdot(du_ref[...], w_ref[...])
        xv = x_ref[...]
        rs = lax.rsqrt(jnp.mean(xv * xv, axis=-1, keepdims=True) + RMS_EPS)
        xn = xv * rs
        dg_ref[...] += jnp.sum(dh * xn, axis=0, keepdims=True)
        dxn = dh * g_ref[...]
        dx_ref[...] = rs * (dxn - xn * jnp.mean(dxn * xn, axis=-1, keepdims=True)) + dx2_ref[...]

        @pl.when(pl.program_id(0) == s // tile - 1)
        def _():
            wait()

    tok = lambda n: pl.BlockSpec((tile, n), lambda i: (i, 0))
    fixed = lambda a: pl.BlockSpec(a.shape, lambda i: (0,) * a.ndim)
    out = pl.pallas_call(
        body, name="proj_xgrad", grid=(s // tile,),
        in_specs=([tok(D), fixed(g), tok(D)] + [tok(du.shape[1]) for du in dus] + [fixed(w) for w in ws]
                  + _hbm_specs(nx)),
        out_specs=[tok(D), pl.BlockSpec((1, D), lambda i: (0, 0))] + _hbm_specs(nx),
        out_shape=[jax.ShapeDtypeStruct((s, D), F32), jax.ShapeDtypeStruct((1, D), F32)] + _received_shapes(slabs, owners),
        scratch_shapes=_exchange_scratch(nx),
        compiler_params=_params("arbitrary"))(x, g, dx2, *dus, *ws, *slabs)
    return out[0], out[1], out[2:]


def _tail(x, target, ya, o, ub, ug, w_oa, w_ob, w_o, fg):
    s = x.shape[0]
    tile = TOK_TILE

    def body(x_ref, t_ref, ya_ref, o_ref, gb_ref, ug_ref, woa_ref, wob_ref, wo_ref, fg_ref,
             loss_ref, dfg_ref, dwo_ref, dwoa_ref, dwob_ref, dx2_ref, dya_ref, do_ref, dgb_ref, dug_ref):
        @pl.when(pl.program_id(0) == 0)
        def _():
            for r in (loss_ref, dfg_ref, dwo_ref, dwoa_ref, dwob_ref):
                r[...] = jnp.zeros_like(r)

        ya_v = ya_ref[...]
        gate_b = gb_ref[...]
        sg_b = _sigmoid(gate_b)
        silu_b = gate_b * sg_b
        o_v = jnp.concatenate([o_ref[h] for h in range(H)], axis=-1)
        yb_v = o_v * silu_b
        big_a = _bdot(ya_v, woa_ref[...])
        big_b = _bdot(yb_v, wob_ref[...])
        sa = _sigmoid(ug_ref[:, :D])
        sb = _sigmoid(ug_ref[:, D:])
        merged = sa * big_a + sb * big_b
        x2 = x_ref[...] + _bdot(merged, wo_ref[...])
        rs = lax.rsqrt(jnp.mean(x2 * x2, axis=-1, keepdims=True) + RMS_EPS)
        xn = x2 * rs
        err = xn * fg_ref[...] - t_ref[...]
        loss_ref[...] += (0.5 / D) * jnp.sum(err * err)
        dout = err * (1.0 / D)
        dfg_ref[...] += jnp.sum(dout * xn, axis=0, keepdims=True)
        dxn = dout * fg_ref[...]
        dx2 = rs * (dxn - xn * jnp.mean(dxn * xn, axis=-1, keepdims=True))
        dx2_ref[...] = dx2
        dwo_ref[...] += _bdot_tn(merged, dx2)
        dmerged = _bdot_nt(dx2, wo_ref[...])
        dbig_a = dmerged * sa
        dbig_b = dmerged * sb
        dug_ref[:, :D] = dmerged * big_a * sa * (1.0 - sa)
        dug_ref[:, D:] = dmerged * big_b * sb * (1.0 - sb)
        dwoa_ref[...] += _bdot_tn(ya_v, dbig_a)
        dwob_ref[...] += _bdot_tn(yb_v, dbig_b)
        dya_ref[...] = _bdot_nt(dbig_a, woa_ref[...])
        dyb = _bdot_nt(dbig_b, wob_ref[...])
        dgb_ref[...] = dyb * o_v * (sg_b * (1.0 + gate_b * (1.0 - sg_b)))
        _dov = dyb * silu_b
        for h in range(H):
            do_ref[h] = _dov[:, N * h:N * (h + 1)]

    tok = lambda n: pl.BlockSpec((tile, n), lambda i: (i, 0))
    hm = pl.BlockSpec((H, tile, N), lambda i: (0, i, 0))
    fixed = lambda shape: pl.BlockSpec(shape, lambda i: (0,) * len(shape))
    f32 = lambda *shape: jax.ShapeDtypeStruct(shape, F32)
    return pl.pallas_call(
        body, name="tail", grid=(s // tile,),
        in_specs=[tok(D), tok(D), tok(DA), hm, pl.BlockSpec((tile, DA), lambda i: (i, 3)), tok(NG),
                  fixed((DA, D)), fixed((DA, D)), fixed((D, D)), fixed((1, D))],
        out_specs=[fixed((1, 1)), fixed((1, D)), fixed((D, D)), fixed((DA, D)), fixed((DA, D)),
                   tok(D), tok(DA), hm, tok(DA), tok(NG)],
        out_shape=[f32(1, 1), f32(1, D), f32(D, D), f32(DA, D), f32(DA, D),
                   f32(s, D), f32(s, DA), f32(H, s, N), f32(s, DA), f32(s, NG)],
        compiler_params=_params("arbitrary"))(x, target, ya, o, ub, ug, w_oa, w_ob, w_o, fg)


def _pre_operands(ua_ref, prev_ref, first):
    cur = ua_ref[...]
    t = cur.shape[0]
    prev_row = jnp.where(first, 0.0, prev_ref[7:8, :])
    rows = lax.broadcasted_iota(jnp.int32, cur.shape, 0)
    sh = jnp.where(rows == 0, prev_row, pltpu.roll(cur, 1, axis=0))
    ops = []
    for c0, n in ((0, DA), (DA, DA), (2 * DA, DA), (3 * DA + 2 * RANK, DA), (3 * DA, RANK), (3 * DA + RANK, RANK)):
        ops += [cur[:, c0:c0 + n], sh[:, c0:c0 + n]]
    del t
    return ops


def _ua_specs(tile, order):
    blocks = tile // 8
    return [pl.BlockSpec((tile, NA), lambda i: (order(i), 0)),
            pl.BlockSpec((8, NA), lambda i: (jnp.maximum(order(i) * blocks - 1, 0), 0))]


def _rwkv_pre_fwd(ua, pre_params):
    s = ua.shape[0]
    tile = HEAD_TILE

    def body(ua_ref, prev_ref, *refs):
        p_refs, o_refs = refs[:len(pre_params)], refs[len(pre_params):]
        ops = _pre_operands(ua_ref, prev_ref, pl.program_id(0) == 0)
        outs = _rwkv_pre(*ops, *[p[...] for p in p_refs])
        for o_ref, val in zip(o_refs, outs):
            o_ref[...] = val

    tm = pl.BlockSpec((tile, DA), lambda i: (i, 0))
    return pl.pallas_call(
        body, name="rwkv_pre_fwd", grid=(s // tile,),
        in_specs=_ua_specs(tile, lambda i: i) + [pl.BlockSpec(p.shape, lambda i, nd=p.ndim: (0,) * nd) for p in pre_params],
        out_specs=[tm] * 8, out_shape=[jax.ShapeDtypeStruct((s, DA), F32)] * 8,
        compiler_params=_params("arbitrary"))(ua, ua, *pre_params)


def _rwkv_pre_bwd(ua, pre_params, cots):
    s = ua.shape[0]
    tile = HEAD_TILE
    nt = s // tile
    n_p = len(pre_params)

    def body(ua_ref, prev_ref, *refs):
        p_refs, c_refs = refs[:n_p], refs[n_p:n_p + 11]
        dua_ref = refs[n_p + 11]
        dp_refs = refs[n_p + 12:n_p + 12 + n_p]
        carry_ref = refs[-1]
        i = pl.program_id(0)

        @pl.when(i == 0)
        def _():
            carry_ref[...] = jnp.zeros_like(carry_ref)
            for r in dp_refs:
                r[...] = jnp.zeros_like(r)

        ops = _pre_operands(ua_ref, prev_ref, i == nt - 1)
        _, vjp = jax.vjp(_rwkv_pre, *ops, *[p[...] for p in p_refs])
        c = [r[...] for r in c_refs]
        grads = vjp((c[0] + c[1], c[2], c[3], c[4] + c[5], c[6] + c[7], c[8], c[9], c[10]))
        d_ops, d_par = grads[:12], grads[12:]
        for r, val in zip(dp_refs, d_par):
            r[...] += val
        d_cur = jnp.concatenate([d_ops[0], d_ops[2], d_ops[4], d_ops[8], d_ops[10], d_ops[6]], axis=-1)
        d_sh = jnp.concatenate([d_ops[1], d_ops[3], d_ops[5], d_ops[9], d_ops[11], d_ops[7]], axis=-1)
        rows = lax.broadcasted_iota(jnp.int32, d_sh.shape, 0)
        dua_ref[...] = d_cur + jnp.where(rows == tile - 1, carry_ref[...], pltpu.roll(d_sh, tile - 1, axis=0))
        carry_ref[...] = d_sh[0:1, :]

    rev = lambda i: nt - 1 - i
    tm = pl.BlockSpec((tile, DA), lambda i: (rev(i), 0))
    fixed = [pl.BlockSpec(p.shape, lambda i, nd=p.ndim: (0,) * nd) for p in pre_params]
    return pl.pallas_call(
        body, name="rwkv_pre_bwd", grid=(nt,),
        in_specs=_ua_specs(tile, rev) + fixed + [tm] * 11,
        out_specs=[pl.BlockSpec((tile, NA), lambda i: (rev(i), 0))] + fixed,
        out_shape=[jax.ShapeDtypeStruct((s, NA), F32)] + [jax.ShapeDtypeStruct(p.shape, F32) for p in pre_params],
        scratch_shapes=[pltpu.VMEM((1, NA), F32)],
        compiler_params=_params("arbitrary"))(ua, ua, *pre_params, *cots)


def _wkv_fwd(seq):
    s = seq[0].shape[0]
    nc = s // WKV_CHUNK

    def body(r_ref, lw_ref, cl_ref, k_ref, v_ref, a_ref, b_ref, y_ref, ck_ref, p_ref, state):
        @pl.when(pl.program_id(0) == 0)
        def _():
            state[...] = jnp.zeros_like(state)

        r, lw, cl, k, v, a, b = (jnp.stack(_to_heads(ref[...])) for ref in (r_ref, lw_ref, cl_ref, k_ref, v_ref, a_ref,
                                                                             b_ref))
        s0 = state[...]
        ck_ref[0] = s0
        p = _tri_inverse(_wkv_aab(True, lw, cl, a, b))
        p_ref[0] = p
        y, s1 = _wkv_apply(True, s0, r, lw, cl, k, v, a, b, p)
        y_ref[...] = jnp.concatenate([y[h] for h in range(H)], axis=-1)
        state[...] = s1

    tm = pl.BlockSpec((WKV_CHUNK, DA), lambda c: (c, 0))
    per_chunk = lambda m: pl.BlockSpec((1, H, m, m), lambda c: (c, 0, 0, 0))
    return pl.pallas_call(
        body, name="wkv_fwd", grid=(nc,), in_specs=[tm] * 7,
        out_specs=[tm, per_chunk(N), per_chunk(WKV_CHUNK)],
        out_shape=[jax.ShapeDtypeStruct((s, DA), F32), jax.ShapeDtypeStruct((nc, H, N, N), F32),
                   jax.ShapeDtypeStruct((nc, H, WKV_CHUNK, WKV_CHUNK), F32)],
        scratch_shapes=[pltpu.VMEM((H, N, N), F32)], compiler_params=_params("arbitrary"))(*seq)


def _wkv_bwd(seq, ckpt, pinv, dy, slabs, owners):
    s = seq[0].shape[0]
    nc = s // WKV_CHUNK
    nx = len(slabs)

    def body(r_ref, lw_ref, cl_ref, k_ref, v_ref, a_ref, b_ref, ck_ref, p_ref, dy_ref, *refs):
        src_refs, d_refs, dst_refs = refs[:nx], refs[nx:nx + 7], refs[nx + 7:2 * nx + 7]
        dstate = refs[2 * nx + 7]
        start, wait = _exchange_ops(src_refs, dst_refs, owners, refs[2 * nx + 8:])

        @pl.when(pl.program_id(0) == 0)
        def _():
            dstate[...] = jnp.zeros_like(dstate)
            start()

        p = p_ref[0]
        r, lw, cl, k, v, a, b, dy = (jnp.stack(_to_heads(ref[...])) for ref in (r_ref, lw_ref, cl_ref, k_ref, v_ref,
                                                                                 a_ref, b_ref, dy_ref))
        _, vjp = jax.vjp(functools.partial(_wkv_apply, False), ck_ref[0], r, lw, cl, k, v, a, b, p)
        ds0, dr, dlw, dcl, dk, dv, da, db, dp = vjp((dy, dstate[...]))
        dstate[...] = ds0
        _, vjp_x = jax.vjp(functools.partial(_wkv_aab, False), lw, cl, a, b)
        dlw2, dcl2, da2, db2 = vjp_x(_dot1(_dot1(p, dp, "tn"), p, "nt"))
        for d_ref, val in zip(d_refs, (dr, dlw + dlw2, dcl + dcl2, dk, dv, da + da2, db + db2)):
            d_ref[...] = jnp.concatenate([val[h] for h in range(H)], axis=-1)

        @pl.when(pl.program_id(0) == nc - 1)
        def _():
            wait()

    tm = pl.BlockSpec((WKV_CHUNK, DA), lambda c: (nc - 1 - c, 0))
    per_chunk = lambda m: pl.BlockSpec((1, H, m, m), lambda c: (nc - 1 - c, 0, 0, 0))
    out = pl.pallas_call(
        body, name="wkv_bwd", grid=(nc,),
        in_specs=[tm] * 7 + [per_chunk(N), per_chunk(WKV_CHUNK), tm] + _hbm_specs(nx),
        out_specs=[tm] * 7 + _hbm_specs(nx),
        out_shape=[jax.ShapeDtypeStruct((s, DA), F32)] * 7 + _received_shapes(slabs, owners),
        scratch_shapes=[pltpu.VMEM((H, N, N), F32)] + _exchange_scratch(nx),
        compiler_params=_params("arbitrary"))(*seq, ckpt, pinv, dy, *slabs)
    return out[:7], out[7:]


def _rwkv_post_fwd(y, r, k2, v, g, post_params):
    s = y.shape[0]
    tile = TOK_TILE

    def body(*refs):
        refs[-1][...] = _rwkv_post(*[ref[...] for ref in refs[:-1]])

    tm = pl.BlockSpec((tile, DA), lambda i: (i, 0))
    par = pl.BlockSpec((1, DA), lambda i: (0, 0))
    return pl.pallas_call(
        body, name="rwkv_post_fwd", grid=(s // tile,), in_specs=[tm] * 5 + [par] * 3,
        out_specs=tm, out_shape=jax.ShapeDtypeStruct((s, DA), F32),
        compiler_params=_params("arbitrary"))(y, r, k2, v, g, *post_params)


def _rwkv_post_bwd(y, r, k2, v, g, post_params, dya, slabs, lo):
    s = y.shape[0]
    tile = HEAD_TILE

    def body(y_ref, r_ref, k_ref, v_ref, g_ref, w_ref, b_ref, rk_ref, dya_ref, s_ref, *refs):
        d_refs, p_ref = refs[:8], refs[8]
        start, wait = _pair_swap_ops(s_ref, p_ref, lo, refs[9:])

        @pl.when(pl.program_id(0) == 0)
        def _():
            for ref in d_refs[5:]:
                ref[...] = jnp.zeros_like(ref)
            start()

        _, vjp = jax.vjp(_rwkv_post, *[ref[...] for ref in (y_ref, r_ref, k_ref, v_ref, g_ref, w_ref, b_ref, rk_ref)])
        grads = vjp(dya_ref[...])
        for ref, val in zip(d_refs[:5], grads[:5]):
            ref[...] = val
        for ref, val in zip(d_refs[5:], grads[5:]):
            ref[...] += val

        @pl.when(pl.program_id(0) == s // tile - 1)
        def _():
            wait()

    tm = pl.BlockSpec((tile, DA), lambda i: (i, 0))
    par = pl.BlockSpec((1, DA), lambda i: (0, 0))
    return pl.pallas_call(
        body, name="rwkv_post_bwd", grid=(s // tile,),
        in_specs=[tm] * 5 + [par] * 3 + [tm] + _hbm_specs(1),
        out_specs=[tm] * 5 + [par] * 3 + _hbm_specs(1),
        out_shape=[jax.ShapeDtypeStruct((s, DA), F32)] * 5 + [jax.ShapeDtypeStruct((1, DA), F32)] * 3
        + [jax.ShapeDtypeStruct(slabs.shape, slabs.dtype)],
        scratch_shapes=_pair_swap_scratch(slabs.shape[0]),
        compiler_params=_params("arbitrary"))(y, r, k2, v, g, *post_params, dya, slabs)


def _tri(t):
    return (lax.broadcasted_iota(jnp.int32, (t, t), 0) >= lax.broadcasted_iota(jnp.int32, (t, t), 1)).astype(F32)


def _fox_pre_fwd(ub, uf, q_g, k_g, f_b):
    s = ub.shape[0]
    tile = HEAD_TILE

    def body(ub_ref, uf_ref, qg_ref, kg_ref, fb_ref, q_ref, k_ref, v_ref, cum_ref, carry):
        @pl.when(pl.program_id(0) == 0)
        def _():
            carry[...] = jnp.zeros_like(carry)

        qn, kn, logf = _fox_pre(ub_ref[:, :DA], ub_ref[:, DA:2 * DA], uf_ref[...], qg_ref[...], kg_ref[...],
                                fb_ref[...])
        for h, (q_col, k_col) in enumerate(zip(_to_heads(qn), _to_heads(kn))):
            q_ref[h] = q_col
            k_ref[h] = k_col
        v_ref[...] = _heads(ub_ref, 2 * DA)
        cum = jnp.dot(_tri(tile), logf, precision=HI, preferred_element_type=F32) + carry[...]
        cum_ref[...] = cum
        carry[...] = cum[tile - 1:tile, :]

    hm = pl.BlockSpec((H, tile, N), lambda i: (0, i, 0))
    fixed = lambda shape: pl.BlockSpec(shape, lambda i: (0,) * len(shape))
    return pl.pallas_call(
        body, name="fox_pre_fwd", grid=(s // tile,),
        in_specs=[pl.BlockSpec((tile, NB), lambda i: (i, 0)), pl.BlockSpec((tile, NF), lambda i: (i, 0)),
                  fixed((1, DA)), fixed((1, DA)), fixed((1, NF))],
        out_specs=[hm] * 3 + [pl.BlockSpec((tile, NF), lambda i: (i, 0))],
        out_shape=[jax.ShapeDtypeStruct((H, s, N), F32)] * 3 + [jax.ShapeDtypeStruct((s, NF), F32)],
        scratch_shapes=[pltpu.VMEM((1, NF), F32)], compiler_params=_params("arbitrary"))(ub, uf, q_g, k_g, f_b)


def _fox_pre_bwd(ub, uf, q_g, k_g, f_b, dqn, dkn, dvf, dgate, dcum_q, dcum_k):
    s = ub.shape[0]
    tile = HEAD_TILE
    nt = s // tile

    def body(ub_ref, uf_ref, qg_ref, kg_ref, fb_ref, dq_ref, dk_ref, dv_ref, dgate_ref, dcq_ref, dck_ref,
             dub_ref, duf_ref, dqg_ref, dkg_ref, dfb_ref, carry):
        @pl.when(pl.program_id(0) == 0)
        def _():
            carry[...] = jnp.zeros_like(carry)
            for ref in (dqg_ref, dkg_ref, dfb_ref):
                ref[...] = jnp.zeros_like(ref)

        dcum = dcq_ref[...] + dck_ref[...]
        dlogf = lax.dot_general(_tri(tile), dcum, (((0,), (0,)), ((), ())), precision=HI,
                                preferred_element_type=F32) + carry[...]
        carry[...] = dlogf[0:1, :]
        _, vjp = jax.vjp(_fox_pre, ub_ref[:, :DA], ub_ref[:, DA:2 * DA], uf_ref[...], qg_ref[...], kg_ref[...],
                         fb_ref[...])
        d_q, d_k, d_f, d_qg, d_kg, d_fb = vjp((_from_heads(dq_ref), _from_heads(dk_ref), dlogf))
        dub_ref[:, :DA] = d_q
        dub_ref[:, DA:2 * DA] = d_k
        _store_heads(dub_ref, 2 * DA, dv_ref[...])
        dub_ref[:, 3 * DA:] = dgate_ref[...]
        duf_ref[...] = d_f
        dqg_ref[...] += functools.reduce(jnp.add, _to_heads(d_qg))
        dkg_ref[...] += functools.reduce(jnp.add, _to_heads(d_kg))
        dfb_ref[...] += d_fb

    rev = lambda i: nt - 1 - i
    hm = pl.BlockSpec((H, tile, N), lambda i: (0, rev(i), 0))
    tok = lambda n: pl.BlockSpec((tile, n), lambda i: (rev(i), 0))
    fixed = lambda shape: pl.BlockSpec(shape, lambda i: (0,) * len(shape))
    return pl.pallas_call(
        body, name="fox_pre_bwd", grid=(nt,),
        in_specs=[tok(NB), tok(NF), fixed((1, DA)), fixed((1, DA)), fixed((1, NF)), hm, hm, hm, tok(DA), tok(NF),
                  tok(NF)],
        out_specs=[tok(NB), tok(NF), fixed((1, N)), fixed((1, N)), fixed((1, NF))],
        out_shape=[jax.ShapeDtypeStruct((s, NB), F32), jax.ShapeDtypeStruct((s, NF), F32),
                   jax.ShapeDtypeStruct((1, N), F32), jax.ShapeDtypeStruct((1, N), F32),
                   jax.ShapeDtypeStruct((1, NF), F32)],
        scratch_shapes=[pltpu.VMEM((1, NF), F32)],
        compiler_params=_params("arbitrary"))(ub, uf, q_g, k_g, f_b, dqn, dkn, dvf, dgate, dcum_q, dcum_k)


def _att_groups(s):
    blocks = s // ATT_TILE
    per = max(1, blocks // ATT_GROUPS)
    return per, blocks // per


def _att_parts(n, width):
    return ([(0, n - width, False)] if n > width else []) + [(n - width, n, True)]


def _att_scores(q_bf, k_ref, ck_ref, lo, hi, masked, row_offset):
    scores = _bdot_nt(q_bf, k_ref[0, lo:hi, :]) - ck_ref[0, :, lo:hi]
    if masked:
        rows = row_offset + lax.broadcasted_iota(jnp.int32, scores.shape, 0)
        scores = jnp.where(rows >= lax.broadcasted_iota(jnp.int32, scores.shape, 1), scores, -1e30)
    return scores


def _fox_attn_fwd(q, k, v, cum_q, cum_k):
    s = q.shape[1]
    t = ATT_TILE
    per, groups = _att_groups(s)

    def body(q_ref, k_ref, v_ref, cq_ref, ck_ref, o_ref, lse_ref):
        qi = pl.program_id(1)
        for g in range(groups):
            @pl.when(qi // per == g)
            def _(g=g):
                q_bf = (q_ref[0] * ATT_SCALE).astype(BF16)
                parts = _att_parts((g + 1) * per * t, per * t)
                scores = [_att_scores(q_bf, k_ref, ck_ref, lo, hi, masked, (qi - g * per) * t)
                          for lo, hi, masked in parts]
                m = functools.reduce(jnp.maximum, [jnp.max(sc, axis=-1, keepdims=True) for sc in scores])
                l, acc = 0.0, 0.0
                for sc, (lo, hi, _) in zip(scores, parts):
                    p = jnp.exp(sc - m)
                    l += jnp.sum(p, axis=-1, keepdims=True)
                    acc += _bdot(p, v_ref[0, lo:hi, :])
                o_ref[0] = acc / l
                lse_ref[0] = m + jnp.log(l) + cq_ref[0]

    qb = pl.BlockSpec((1, t, N), lambda h, i: (h, i, 0))
    kb = pl.BlockSpec((1, s, N), lambda h, i: (h, 0, 0))
    return pl.pallas_call(
        body, name="fox_attn_fwd", grid=(H, s // t),
        in_specs=[qb, kb, kb, pl.BlockSpec((1, t, 1), lambda h, i: (h, i, 0)),
                  pl.BlockSpec((1, 1, s), lambda h, i: (h, 0, 0))],
        out_specs=[qb, pl.BlockSpec((1, t, 1), lambda h, i: (h, i, 0))],
        out_shape=[jax.ShapeDtypeStruct((H, s, N), F32), jax.ShapeDtypeStruct((H, s, 1), F32)],
        compiler_params=_params("arbitrary", "arbitrary"))(q, k, v, cum_q, cum_k)


def _fox_attn_bwd(q, k, v, cum_q, cum_k, o, lse, do, slabs, owners):
    s = q.shape[1]
    t = ATT_TILE
    per, groups = _att_groups(s)
    nx = len(slabs)

    def body(q_ref, k_ref, v_ref, cq_ref, ck_ref, o_ref, lse_ref, do_ref, *refs):
        src_refs, (dq_ref, dk_ref, dv_ref, dcq_ref, dck_ref) = refs[:nx], refs[nx:nx + 5]
        start, wait = _exchange_ops(src_refs, refs[nx + 5:2 * nx + 5], owners, refs[2 * nx + 5:])
        qi = pl.program_id(1)

        @pl.when((pl.program_id(0) == 0) & (qi == 0))
        def _():
            start()

        @pl.when(qi == 0)
        def _():
            for ref in (dk_ref, dv_ref, dck_ref):
                ref[...] = jnp.zeros_like(ref)

        for g in range(groups):
            @pl.when(qi // per == g)
            def _(g=g):
                q_bf, do_bf = (q_ref[0] * ATT_SCALE).astype(BF16), do_ref[0].astype(BF16)
                row_term = cq_ref[0] - lse_ref[0]
                delta = jnp.sum(do_ref[0] * o_ref[0], axis=-1, keepdims=True)
                dq, dcq = 0.0, 0.0
                for lo, hi, masked in _att_parts((g + 1) * per * t, per * t):
                    p = jnp.exp(_att_scores(q_bf, k_ref, ck_ref, lo, hi, masked, (qi - g * per) * t) + row_term)
                    ds = p * (_bdot_nt(do_bf, v_ref[0, lo:hi, :]) - delta)
                    dq += _bdot(ds, k_ref[0, lo:hi, :])
                    dcq += jnp.sum(ds, axis=-1, keepdims=True)
                    dk_ref[0, lo:hi, :] += _bdot_tn(ds, q_bf)
                    dv_ref[0, lo:hi, :] += _bdot_tn(p, do_bf)
                    dck_ref[0, :, lo:hi] -= jnp.sum(ds, axis=0, keepdims=True)
                dq_ref[0] = dq * ATT_SCALE
                dcq_ref[0] = dcq

        @pl.when((pl.program_id(0) == H - 1) & (qi == s // t - 1))
        def _():
            wait()

    qb = pl.BlockSpec((1, t, N), lambda h, i: (h, i, 0))
    kb = pl.BlockSpec((1, s, N), lambda h, i: (h, 0, 0))
    cqb = pl.BlockSpec((1, t, 1), lambda h, i: (h, i, 0))
    ckb = pl.BlockSpec((1, 1, s), lambda h, i: (h, 0, 0))
    f32 = lambda *shape: jax.ShapeDtypeStruct(shape, F32)
    out = pl.pallas_call(
        body, name="fox_attn_bwd", grid=(H, s // t),
        in_specs=[qb, kb, kb, cqb, ckb, qb, cqb, qb] + _hbm_specs(nx), out_specs=[qb, kb, kb, cqb, ckb] + _hbm_specs(nx),
        out_shape=[f32(H, s, N), f32(H, s, N), f32(H, s, N), f32(H, s, 1), f32(H, 1, s)]
        + _received_shapes(slabs, owners),
        scratch_shapes=_exchange_scratch(nx),
        compiler_params=_params("arbitrary", "arbitrary"))(q, k, v, cum_q, cum_k, o, lse, do, *slabs)
    return out[:5], out[5:]


def _local_step(x, target, w, p):
    mu = p["shift_mu"]
    lora_matrix = lambda a: jnp.moveaxis(a, 0, 1).reshape(RANK, DA).astype(F32)
    pre_params = (mu[:, 0:DA], mu[:, DA:2 * DA], mu[:, 2 * DA:3 * DA], mu[:, 3 * DA + 2 * RANK:],
                  mu[:, 3 * DA:3 * DA + RANK], mu[:, 3 * DA + RANK:3 * DA + 2 * RANK],
                  lora_matrix(w["w_lora_up"]), p["w0"], lora_matrix(w["a_lora_up"]), p["a0"], p["k_k"], p["k_a"])
    post_params = (p["lnx_w"], p["lnx_b"], p["r_k"])
    q_g, k_g = jnp.tile(p["q_norm_g"], (1, H)), jnp.tile(p["k_norm_g"], (1, H))
    f_b = jnp.pad(p["f_bias"], ((0, 0), (0, NF - H)))
    fg = p["final_norm_g"].reshape(1, D)

    h, (ua, ub, ug, uf) = _norm_proj(x, p["norm_g"], (w["in_a"], w["in_b"], w["in_g"], w["in_f"]))
    r, lw, cl, k2, v, av, bv, gg = _rwkv_pre_fwd(ua, pre_params)
    y, ckpt, pinv = _wkv_fwd((r, lw, cl, k2, v, av, bv))
    ya = _rwkv_post_fwd(y, r, k2, v, gg, post_params)
    qn, kn, vf, cum = _fox_pre_fwd(ub, uf, q_g, k_g, f_b)
    cum_t = cum[:, :H].T
    cum_q, cum_k = cum_t[:, :, None], cum_t[:, None, :]
    o, lse = _fox_attn_fwd(qn, kn, vf, cum_q, cum_k)

    (loss, dfg, dwo, dwoa, dwob, dx2, dya, do, dgate_b, dug) = _tail(
        x, target, ya, o, ub, ug, w["w_out_a"], w["w_out_b"], w["w_out"], fg)
    everyone = (0, N_DEV)
    (dqn, dkn, dvf, dcq, dck), (recv_woa, recv_wob, recv_wo) = _fox_attn_bwd(
        qn, kn, vf, cum_q, cum_k, o, lse, do,
        (_col_slabs(dwoa), _col_slabs(dwob), dwo.astype(BF16).reshape(N_DEV, D // N_DEV, D)), (everyone,) * 3)
    pad_f = lambda a: jnp.pad(a.T, ((0, 0), (0, NF - H)))
    dub, duf, dqg, dkg, dfb = _fox_pre_bwd(ub, uf, q_g, k_g, f_b, dqn, dkn, dvf, dgate_b,
                                           pad_f(dcq[:, :, 0]), pad_f(dck.reshape(H, -1)))
    spill = EARLY_FROM * COLS_PER_DEV - NA
    early, dwt_b_head = _proj_wgrad_early(h, dub, dug, duf, -(-spill // 8) * 8)
    dy, dr_p, dk_p, dv_p, dgg, dlnw, dlnb, drk, handed = _rwkv_post_bwd(y, r, k2, v, gg, post_params, dya, early,
                                                                          EARLY_FROM)
    early = _chip_sums(early, handed, EARLY_FROM, "chip_sums_early")
    (dr_s, dlw, dcl, dk_s, dv_s, dav, dbv), (recv_early,) = _wkv_bwd(
        (r, lw, cl, k2, v, av, bv), ckpt, pinv, dy, (early,), ((EARLY_FROM, N_DEV, "chips"),))
    pre_out = _rwkv_pre_bwd(ua, pre_params, (dr_s, dr_p, dlw, dcl, dk_s, dk_p, dv_s, dv_p, dav, dbv, dgg))
    dua, dpre = pre_out[0], pre_out[1:]
    late = _proj_wgrad_late(h, dua, dwt_b_head)

    flat = lambda a: a.reshape(1, -1)
    small = {
        "final_norm_g": dfg, "w0": dpre[7], "a0": dpre[9], "k_k": dpre[10], "k_a": dpre[11], "r_k": drk, "lnx_w": dlnw,
        "lnx_b": dlnb, "q_norm_g": dqg, "k_norm_g": dkg, "f_bias": dfb[:, :H],
        "shift_mu": jnp.concatenate([flat(dpre[0]), flat(dpre[1]), flat(dpre[2]), dpre[4], dpre[5], flat(dpre[3])], axis=1),
    }
    late = _chip_sums(late, _pair_swap(late, 0, "pair_swap_late"), 0, "chip_sums_late")
    by_head = lambda a: jnp.moveaxis(a.reshape(RANK, H, N), 1, 0)
    loras = jnp.stack([by_head(dpre[6]), by_head(dpre[8])], axis=1).astype(BF16)
    dx, dng, (recv_late, recv_lora, recv_small) = _proj_xgrad(
        x, p["norm_g"], dx2, (dua, dub, dug, duf), (w["in_a"], w["in_b"], w["in_g"], w["in_f"]),
        (late, loras, _pack_small(small, loss)), ((0, EARLY_FROM, "chips"), everyone, everyone))
    return dx, dng, (recv_early, recv_late), (recv_woa, recv_wob, recv_wo, recv_lora), recv_small


def _position():
    return lax.axis_index("x"), lax.axis_index("y"), lax.axis_index("c")


def _hbm_specs(n):
    return [pl.BlockSpec(memory_space=pl.ANY)] * n


BIG_GATHER_COPIES = 13
GATHER_ROW_CUT = 400


def _all_gather(big, blocks, name):
    n = len(blocks)

    def body(*refs):
        big_ref, x_refs = refs[0], refs[1:1 + n]
        big_out, out_refs = refs[1 + n], refs[2 + n:2 + 2 * n]
        send_sems, recv_sems, local_sems = refs[2 + 2 * n:]
        x, y, c = _position()
        me, sibling = (x, y, c), (x, y, 1 - c)
        chips = [(1 - x, y), (x, 1 - y), (1 - x, 1 - y)]
        x_nbr, y_nbr, diag = chips
        rows = big_ref.shape[0]
        cut = GATHER_ROW_CUT

        def part(ref, h):
            return ref if h is None else ref.at[pl.ds(0, cut)] if h == 0 else ref.at[pl.ds(cut, rows - cut)]

        def landed(chip, core, h):
            return part(big_out.at[4 * chip[0] + 2 * chip[1] + core], h)

        def big_copy(k, src, dst, to):
            return pltpu.make_async_remote_copy(src_ref=src, dst_ref=dst, send_sem=send_sems.at[7 * n + k],
                                                recv_sem=recv_sems.at[7 * n + k], device_id=to, device_id_type=MESH)

        def arrival(k, chip, core, h):
            dst = landed(chip, core, h)
            return big_copy(k, dst, dst, me)

        def pass_on(k, chip, h, to):
            src = landed(chip, c, h)
            return big_copy(k, src, src, to)

        big_mine = pltpu.make_async_copy(big_ref, landed((x, y), c, None), local_sems.at[n])
        big_mine.start()
        here = (x, y)
        big_sent = [big_copy(0, big_ref, landed(here, c, None), sibling),
                    big_copy(1, part(big_ref, 0), landed(here, c, 0), (*x_nbr, c)),
                    big_copy(2, part(big_ref, 1), landed(here, c, 1), (*y_nbr, c)),
                    big_copy(3, part(big_ref, 1), landed(here, c, 1), (*x_nbr, c)),
                    big_copy(4, part(big_ref, 0), landed(here, c, 0), (*y_nbr, c))]
        for cp in big_sent:
            cp.start()

        def copy(a, k, blk, to, own=False):
            dst = out_refs[a].at[4 * blk[0] + 2 * blk[1] + blk[2]]
            return pltpu.make_async_remote_copy(
                src_ref=x_refs[a] if own else dst, dst_ref=dst, send_sem=send_sems.at[7 * a + k],
                recv_sem=recv_sems.at[7 * a + k], device_id=to, device_id_type=MESH)

        mine = [pltpu.make_async_copy(x_refs[a], out_refs[a].at[4 * x + 2 * y + c], local_sems.at[a]) for a in range(n)]
        for cp in mine:
            cp.start()
        first = []
        for a in range(n):
            first.append(copy(a, 0, me, sibling, own=True))
            first += [copy(a, 1 + j, me, (*chip, c), own=True) for j, chip in enumerate(chips)]
        for cp in first:
            cp.start()

        big_steps = [(1, x_nbr, 0, (*y_nbr, c), 5, 7), (2, y_nbr, 1, (*x_nbr, c), 6, 8), (3, x_nbr, 1, None, None, 9),
                     (4, y_nbr, 0, None, None, 10), (5, diag, 0, None, None, 11), (6, diag, 1, None, None, 12)]
        for k, chip, h, onward, k_onward, k_sibling in big_steps:
            arrival(k, chip, c, h).wait_recv()
            if onward is not None:
                big_sent.append(pass_on(k_onward, chip, h, onward))
                big_sent[-1].start()
            big_sent.append(pass_on(k_sibling, chip, h, sibling))
            big_sent[-1].start()

        passed = []
        for j, chip in enumerate(chips):
            for a in range(n):
                copy(a, 1 + j, (*chip, c), me).wait_recv()
                passed.append(copy(a, 4 + j, (*chip, c), sibling))
                passed[-1].start()
        for a in range(n):
            copy(a, 0, sibling, me).wait_recv()
        for j, chip in enumerate(chips):
            for a in range(n):
                copy(a, 4 + j, (*chip, 1 - c), me).wait_recv()
        arrival(0, here, 1 - c, None).wait_recv()
        for k, chip, h, _, _, k_sibling in big_steps:
            arrival(k_sibling, chip, 1 - c, h).wait_recv()
        for cp in first + passed + big_sent:
            cp.wait_send()
        for cp in mine + [big_mine]:
            cp.wait()

    everything = [big] + list(blocks)
    return pl.pallas_call(
        body, name=name, out_shape=[jax.ShapeDtypeStruct((N_DEV,) + b.shape, b.dtype) for b in everything],
        in_specs=_hbm_specs(n + 1), out_specs=_hbm_specs(n + 1),
        scratch_shapes=[pltpu.SemaphoreType.DMA((7 * n + BIG_GATHER_COPIES,)),
                        pltpu.SemaphoreType.DMA((7 * n + BIG_GATHER_COPIES,)), pltpu.SemaphoreType.DMA((n + 1,))],
    )(*everything)


def _received_shapes(slabs, owners):
    return [jax.ShapeDtypeStruct((N_DEV // 2 if len(o) == 3 else N_DEV,) + s.shape[1:], s.dtype)
            for s, o in zip(slabs, owners)]


def _pair_swap_scratch(n):
    return [pltpu.SemaphoreType.DMA((n,)), pltpu.SemaphoreType.DMA((n,))]


def _pair_swap_ops(s_ref, p_ref, lo, sems):
    send_sems, recv_sems = sems
    n = s_ref.shape[0]

    def run(sending):
        x, y, c = _position()
        for side in (0, 1):
            mine = [pltpu.make_async_remote_copy(src_ref=s_ref.at[i], dst_ref=p_ref.at[i], send_sem=send_sems.at[i],
                                                 recv_sem=recv_sems.at[i], device_id=(x, y, 1 - c), device_id_type=MESH)
                    for i in range(n) if (lo + i) % 2 == side]

            @pl.when(c != side)
            def _():
                for cp in mine:
                    cp.start() if sending else cp.wait_send()

            if not sending:
                @pl.when(c == side)
                def _():
                    for cp in mine:
                        cp.wait_recv()

    return functools.partial(run, True), functools.partial(run, False)


def _pair_swap(slabs, lo, name):
    n = slabs.shape[0]

    def body(s_ref, p_ref, *sems):
        start, wait = _pair_swap_ops(s_ref, p_ref, lo, sems)
        start()
        wait()

    return pl.pallas_call(
        body, name=name, out_shape=jax.ShapeDtypeStruct(slabs.shape, slabs.dtype),
        in_specs=_hbm_specs(1), out_specs=_hbm_specs(1)[0], scratch_shapes=_pair_swap_scratch(n))(slabs)


def _chip_sums(slabs, swapped, lo, name):
    n, rows, cols = slabs.shape
    tile = W_IN_COL_TILE

    def body(s_ref, p_ref, o_ref):
        c = lax.axis_index("c")
        for i in range(n):
            @pl.when(c == (lo + i) % 2)
            def _(i=i):
                o_ref[i] = (s_ref[i].astype(F32) + p_ref[i].astype(F32)).astype(BF16)

    blk = pl.BlockSpec((n, rows, tile), lambda j: (0, 0, j))
    return pl.pallas_call(
        body, name=name, grid=(cols // tile,), in_specs=[blk, blk], out_specs=blk,
        out_shape=jax.ShapeDtypeStruct(slabs.shape, BF16), compiler_params=_params("arbitrary"))(slabs, swapped)


def _exchange_scratch(n):
    return [pltpu.SemaphoreType.DMA((7 * n,)), pltpu.SemaphoreType.DMA((7 * n,)), pltpu.SemaphoreType.DMA((n,))]


def _exchange_ops(src_refs, dst_refs, owners, sems):
    send_sems, recv_sems, local_sems = sems
    n = len(src_refs)

    def guarded(a, dev, fn):
        lo, hi = owners[a][:2]
        if (lo, hi) == (0, N_DEV):
            fn()
        else:
            pl.when((dev >= lo) & (dev < hi))(fn)

    def src(a, dev):
        ref = src_refs[a]
        return ref.at[0] if ref.shape[0] == 1 else ref.at[dev - owners[a][0]]

    def run(sending, waiting):
        x, y, c = _position()
        me = 4 * x + 2 * y + c
        for a in range(n):
            by_chip = len(owners[a]) == 3
            slot = (lambda qx, qy, qc: 2 * qx + qy) if by_chip else (lambda qx, qy, qc: 4 * qx + 2 * qy + qc)
            mine = slot(x, y, c)
            local = lambda a=a, mine=mine: pltpu.make_async_copy(src(a, me), dst_refs[a].at[mine], local_sems.at[a])
            if sending:
                guarded(a, me, lambda local=local: local().start())
            for m in range(2, N_DEV, 2) if by_chip else range(1, N_DEV):
                px, py, pc = x ^ (m >> 2), y ^ ((m >> 1) & 1), c ^ (m & 1)
                peer = 4 * px + 2 * py + pc
                theirs = slot(px, py, pc)
                sem = dict(send_sem=send_sems.at[7 * a + m - 1], recv_sem=recv_sems.at[7 * a + m - 1],
                           device_id=(px, py, pc), device_id_type=MESH)
                send = lambda a=a, peer=peer, sem=sem, mine=mine: pltpu.make_async_remote_copy(
                    src_ref=src(a, peer), dst_ref=dst_refs[a].at[mine], **sem)
                recv = lambda a=a, sem=sem, theirs=theirs: pltpu.make_async_remote_copy(
                    src_ref=src(a, me), dst_ref=dst_refs[a].at[theirs], **sem)
                if sending:
                    guarded(a, peer, lambda send=send: send().start())
                if waiting:
                    guarded(a, me, lambda recv=recv: recv().wait_recv())
                    guarded(a, peer, lambda send=send: send().wait_send())
            if waiting:
                guarded(a, me, lambda local=local: local().wait())

    return functools.partial(run, True, False), functools.partial(run, False, True)


def _sum_slabs(r_ref):
    g = r_ref[0].astype(F32)
    for k in range(1, r_ref.shape[0]):
        g = g + r_ref[k].astype(F32)
    return g


def _adamw(g, w, m, v):
    m_new = ADAM_B1 * m + (1.0 - ADAM_B1) * g
    v_new = ADAM_B2 * v + (1.0 - ADAM_B2) * (g * g)
    m_hat = m_new / (1.0 - ADAM_B1 ** ADAM_STEP)
    v_hat = v_new / (1.0 - ADAM_B2 ** ADAM_STEP)
    return g, -ADAM_LR * (m_hat / (jnp.sqrt(v_hat) + ADAM_EPS) + ADAM_WD * w), m_new, v_new


def _adamw_w_in(recv_early, recv_late, w, m, v, slabs, owners):
    rows, cols = w.shape
    tile = W_IN_COL_TILE
    nx = len(slabs)

    def body(early_ref, late_ref, w_ref, m_ref, v_ref, *refs):
        src_refs, o_refs, dst_refs = refs[:nx], refs[nx:nx + 4], refs[nx + 4:2 * nx + 4]
        start, wait = _exchange_ops(src_refs, dst_refs, owners, refs[2 * nx + 4:])
        x, y, c = _position()
        early_owner = 4 * x + 2 * y + c >= EARLY_FROM

        @pl.when(pl.program_id(0) == 0)
        def _():
            start()

        def update(g):
            for o_ref, val in zip(o_refs, _adamw(g, w_ref[...], m_ref[...], v_ref[...])):
                o_ref[...] = val

        pl.when(early_owner)(lambda: update(_sum_slabs(early_ref)))
        pl.when(jnp.logical_not(early_owner))(lambda: update(_sum_slabs(late_ref)))

        @pl.when(pl.program_id(0) == cols // tile - 1)
        def _():
            wait()

    blk = pl.BlockSpec((rows, tile), lambda i: (0, i))
    slots = lambda r: pl.BlockSpec((r.shape[0], rows, tile), lambda i: (0, 0, i))
    out = pl.pallas_call(
        body, name="adamw_w_in", grid=(cols // tile,),
        in_specs=[slots(recv_early), slots(recv_late), blk, blk, blk] + _hbm_specs(nx),
        out_specs=[blk] * 4 + _hbm_specs(nx),
        out_shape=[jax.ShapeDtypeStruct((rows, cols), F32)] * 4 + _received_shapes(slabs, owners),
        scratch_shapes=_exchange_scratch(nx),
        compiler_params=_params("arbitrary"))(recv_early, recv_late, w, m, v, *slabs)
    return out[:4], out[4:]


def _adamw_misc(recvs, recv_small, recv_norm, params):
    names = list(params)
    flat = [a for n in names for a in params[n]]

    def body(woa_ref, wob_ref, wo_ref, lora_ref, small_ref, norm_ref, *refs):
        p_refs, o_refs = refs[:len(flat)], refs[len(flat):]
        g_small = _sum_slabs(small_ref)
        g_lora = _sum_slabs(lora_ref)
        grads = {"w_out_a": _sum_slabs(woa_ref), "w_out_b": _sum_slabs(wob_ref), "w_out": _sum_slabs(wo_ref),
                 "w_lora_up": g_lora[0], "a_lora_up": g_lora[1], "norm_g": _sum_slabs(norm_ref)}
        for n, (off, size) in SMALL_SLOTS.items():
            grads[n] = g_small[:, off:off + size]
        for i, n in enumerate(names):
            w_ref, m_ref, v_ref = p_refs[3 * i:3 * i + 3]
            for o_ref, val in zip(o_refs[4 * i:4 * i + 4], _adamw(grads[n], w_ref[...], m_ref[...], v_ref[...])):
                o_ref[...] = val
        o_refs[-1][...] = g_small[:, LOSS_SLOT:LOSS_SLOT + 1]

    out = pl.pallas_call(
        body, name="adamw_misc",
        out_shape=[jax.ShapeDtypeStruct(params[n][0].shape, F32) for n in names for _ in range(4)]
        + [jax.ShapeDtypeStruct((1, 1), F32)],
        compiler_params=_params())(*recvs, recv_small, recv_norm, *flat)
    return {n: out[4 * i:4 * i + 4] for i, n in enumerate(names)}, out[-1]


_WT_SEGMENTS = ((0, NA), (NA, NB), (NA + NB + H, NG), (NA + NB, H))


def _split_wt(gathered):
    tile = W_IN_COL_TILE

    def body(g_ref, *o_refs):
        full = jnp.concatenate([g_ref[j] for j in range(N_DEV)], axis=0)
        for o_ref, (row, n) in zip(o_refs, _WT_SEGMENTS):
            seg = full[row:row + n]
            if n < o_ref.shape[0]:
                seg = jnp.concatenate([seg, jnp.zeros((o_ref.shape[0] - n, tile), BF16)], axis=0)
            o_ref[...] = seg

    sizes = (NA, NB, NG, NF)
    return pl.pallas_call(
        body, name="split_wt", grid=(D // tile,),
        in_specs=[pl.BlockSpec((N_DEV, COLS_PER_DEV, tile), lambda i: (0, 0, i))],
        out_specs=[pl.BlockSpec((n, tile), lambda i: (0, i)) for n in sizes],
        out_shape=[jax.ShapeDtypeStruct((n, D), BF16) for n in sizes],
        compiler_params=_params("arbitrary"))(gathered)


def _by_cols(a):
    return jnp.moveaxis(a, 0, 1).reshape(a.shape[1], -1)


def _col_slabs(a):
    return jnp.moveaxis(a.reshape(a.shape[0], N_DEV, -1), 1, 0).astype(BF16)


def _pack_small(grads, loss):
    pieces, at = [], 0
    for n, (off, size) in list(SMALL_SLOTS.items()) + [("loss", (LOSS_SLOT, 1))]:
        pieces += [jnp.zeros((off - at,), F32), (loss if n == "loss" else grads[n]).reshape(-1)]
        at = off + size
    return jnp.concatenate(pieces + [jnp.zeros((SMALL_LEN - at,), F32)]).reshape(1, 1, SMALL_LEN)


def _gather_weights(t):
    cast = lambda a: a.astype(BF16)
    loras = jnp.stack([t["w_lora_up"][0], t["a_lora_up"][0]])
    wt, woa, wob, wo, lora = _all_gather(
        cast(t["w_in"][0].T), [cast(t["w_out_a"][0]), cast(t["w_out_b"][0]), cast(t["w_out"][0]), cast(loras)],
        "weight_gather")
    in_a, in_b, in_g, in_f = _split_wt(wt)
    return {"in_a": in_a, "in_b": in_b, "in_g": in_g, "in_f": in_f, "w_out_a": _by_cols(woa), "w_out_b": _by_cols(wob),
            "w_out": wo.reshape(D, D), "w_lora_up": lora[:, 0], "a_lora_up": lora[:, 1]}


def kernel(x, norm_g, w_in, shift_mu, w_lora_up, w0, a_lora_up, a0, k_k, k_a, r_k, lnx_w, lnx_b, f_bias, q_norm_g, k_norm_g, w_out_a, w_out_b, w_out, final_norm_g, loss_target, m_norm_g, m_w_in, m_shift_mu, m_w_lora_up, m_w0, m_a_lora_up, m_a0, m_k_k, m_k_a, m_r_k, m_lnx_w, m_lnx_b, m_f_bias, m_q_norm_g, m_k_norm_g, m_w_out_a, m_w_out_b, m_w_out, m_final_norm_g, v_norm_g, v_w_in, v_shift_mu, v_w_lora_up, v_w0, v_a_lora_up, v_a0, v_k_k, v_k_a, v_r_k, v_lnx_w, v_lnx_b, v_f_bias, v_q_norm_g, v_k_norm_g, v_w_out_a, v_w_out_b, v_w_out, v_final_norm_g):
    names = ("norm_g", "w_in", "shift_mu", "w_lora_up", "w0", "a_lora_up", "a0", "k_k", "k_a", "r_k", "lnx_w", "lnx_b",
             "f_bias", "q_norm_g", "k_norm_g", "w_out_a", "w_out_b", "w_out", "final_norm_g")
    weights = dict(zip(names, (norm_g, w_in, shift_mu, w_lora_up, w0, a_lora_up, a0, k_k, k_a, r_k, lnx_w, lnx_b,
                               f_bias, q_norm_g, k_norm_g, w_out_a, w_out_b, w_out, final_norm_g)))
    m_in = dict(zip(names, (m_norm_g, m_w_in, m_shift_mu, m_w_lora_up, m_w0, m_a_lora_up, m_a0, m_k_k, m_k_a, m_r_k,
                            m_lnx_w, m_lnx_b, m_f_bias, m_q_norm_g, m_k_norm_g, m_w_out_a, m_w_out_b, m_w_out,
                            m_final_norm_g)))
    v_in = dict(zip(names, (v_norm_g, v_w_in, v_shift_mu, v_w_lora_up, v_w0, v_a_lora_up, v_a0, v_k_k, v_k_a, v_r_k,
                            v_lnx_w, v_lnx_b, v_f_bias, v_q_norm_g, v_k_norm_g, v_w_out_a, v_w_out_b, v_w_out,
                            v_final_norm_g)))

    matrices = ("w_out_a", "w_out_b", "w_out", "w_lora_up", "a_lora_up")
    as_2d = lambda n, a: a[0] if n in matrices else a.reshape(1, -1)

    full = _gather_weights(weights)
    dx, dng, recv_wt, recvs, recv_small = _local_step(
        x[0], loss_target[0], full, {n: as_2d(n, weights[n]) for n in ("norm_g",) + tuple(SMALL_SLOTS)})

    res, (recv_norm,) = _adamw_w_in(*recv_wt, w_in[0].T, m_w_in[0].T, v_w_in[0].T, (dng[None],), ((0, N_DEV),))
    outs = {"w_in": [r.T[None] for r in res]}
    misc = [n for n in names if n != "w_in"]
    res, loss_sum = _adamw_misc(recvs, recv_small, recv_norm,
                                {n: tuple(as_2d(n, t[n]) for t in (weights, m_in, v_in)) for n in misc})
    for n in misc:
        outs[n] = [r.reshape(weights[n].shape) for r in res[n]]
    return (loss_sum.reshape(()), dx[None], *[outs[n][i] for i in range(4) for n in names])
```

```python
import functools
import math

import jax
import jax.numpy as jnp
from jax import lax
from jax.experimental import pallas as pl
from jax.experimental.pallas import tpu as pltpu

F32 = jnp.float32
BF16 = jnp.bfloat16
HI = lax.Precision.HIGHEST
MESH = pl.DeviceIdType.MESH

N_DEV = 8
D = 1024
H = 8
N = 64
DA = H * N
RANK = 64
NA = 4 * DA + 2 * RANK
NB = 4 * DA
NG = 2 * D
NF = 128
IN_COLS = NA + NB + H + NG
COLS_PER_DEV = IN_COLS // N_DEV
RMS_EPS = 1e-6
LNX_EPS = 64e-5
ATT_SCALE = N ** -0.5

ADAM_LR = 0.001
ADAM_B1 = 0.9
ADAM_B2 = 0.999
ADAM_EPS = 1e-08
ADAM_WD = 0.01
ADAM_STEP = 10

LANES = 128
WKV_CHUNK = 64
TOK_TILE = 256
HEAD_TILE = 256
XGRAD_TILE = 128
ATT_TILE = 256
ATT_GROUPS = 8
VMEM_LIMIT = 56 * 1024 * 1024


def _lane_tile_slots(sizes):
    slots, at = {}, 0
    for name, size in sizes:
        slots[name] = (at, size)
        at += -(-size // LANES) * LANES
    return slots, at


SMALL_SLOTS, LOSS_SLOT = _lane_tile_slots((
    ("final_norm_g", D), ("shift_mu", NA), ("w0", DA), ("a0", DA), ("k_k", DA), ("k_a", DA), ("r_k", DA), ("lnx_w", DA),
    ("lnx_b", DA), ("q_norm_g", N), ("k_norm_g", N), ("f_bias", H)))
SMALL_LEN = LOSS_SLOT + LANES
W_IN_COL_TILE = 256
EARLY_FROM = -(-NA // COLS_PER_DEV)


def _params(*sem):
    return pltpu.CompilerParams(dimension_semantics=sem or None, vmem_limit_bytes=VMEM_LIMIT)


def _bdot(a, b):
    return jnp.dot(a.astype(BF16), b.astype(BF16), preferred_element_type=F32)


def _bdot_nt(a, b):
    return lax.dot_general(a.astype(BF16), b.astype(BF16), (((1,), (1,)), ((), ())), preferred_element_type=F32)


def _bdot_tn(a, b):
    return lax.dot_general(a.astype(BF16), b.astype(BF16), (((0,), (0,)), ((), ())), preferred_element_type=F32)


def _sigmoid(x):
    return 1.0 / (1.0 + jnp.exp(-x))


def _softplus(x):
    return jnp.maximum(x, 0.0) + jnp.log(1.0 + jnp.exp(-jnp.abs(x)))


def _heads(ref, col0):
    return jnp.stack([ref[:, col0 + N * h:col0 + N * (h + 1)] for h in range(H)])


def _store_heads(ref, col0, val):
    for h in range(H):
        ref[:, col0 + N * h:col0 + N * (h + 1)] = val[h]


def _lerp(c, s, mu):
    return c + (s - c) * mu


def _head_sums(x):
    low = lax.broadcasted_iota(jnp.int32, (x.shape[0], LANES), 1) < N
    out = []
    for p in range(x.shape[1] // LANES):
        pair = x[:, LANES * p:LANES * (p + 1)]
        first = jnp.sum(jnp.where(low, pair, 0.0), axis=-1, keepdims=True)
        second = jnp.sum(jnp.where(low, 0.0, pair), axis=-1, keepdims=True)
        out.append(jnp.where(low, first, second))
    return jnp.concatenate(out, axis=-1)


def _to_heads(x):
    return [x[:, N * h:N * (h + 1)] for h in range(H)]


def _from_heads(ref):
    return jnp.concatenate([ref[h] for h in range(H)], axis=-1)


def _rwkv_pre(rc, rs, kc, ks, vc, vs, gc, gs, wdc, wds, adc, ads,
              mu_r, mu_k, mu_v, mu_g, mu_wd, mu_ad, w_up, w0, a_up, a0, k_k, k_a):
    r = _lerp(rc, rs, mu_r)
    k = _lerp(kc, ks, mu_k)
    v = _lerp(vc, vs, mu_v)
    g = _lerp(gc, gs, mu_g)
    wd = _lerp(wdc, wds, mu_wd)
    ad = _lerp(adc, ads, mu_ad)
    t = wd.shape[0]
    w_raw = -_softplus(-(w0 + _bdot(jnp.tanh(wd), w_up))) - 0.5
    lw = -jnp.exp(w_raw)
    row = lax.broadcasted_iota(jnp.int32, (t, t), 0)
    col = lax.broadcasted_iota(jnp.int32, (t, t), 1)
    same_chunk = ((row >= col) & (row // WKV_CHUNK == col // WKV_CHUNK)).astype(F32)
    cl = jnp.dot(same_chunk, lw, precision=HI, preferred_element_type=F32)
    alr = _sigmoid(a0 + _bdot(ad, a_up))
    kk = k * k_k
    kk = kk / jnp.maximum(jnp.sqrt(_head_sums(kk * kk)), 1e-12)
    k2 = k * (1.0 + (alr - 1.0) * k_a)
    return r, lw, cl, k2, v, -kk, kk * alr, g


_MM_DIMS = {"nn": (((2,), (1,)), ((0,), (0,))), "nt": (((2,), (2,)), ((0,), (0,))), "tn": (((1,), (1,)), ((0,), (0,)))}


def _dot1(a, b, kind):
    return lax.dot_general(a.astype(BF16), b.astype(BF16), dimension_numbers=_MM_DIMS[kind], preferred_element_type=F32)


@functools.partial(jax.custom_vjp, nondiff_argnums=(2,))
def _mm(a, b, kind):
    return _dot1(a, b, kind)


def _mm_fwd(a, b, kind):
    return _dot1(a, b, kind), (a, b)


def _mm_bwd(kind, res, ct):
    a, b = res
    if kind == "nn":
        return _dot1(ct, b, "nt"), _dot1(a, ct, "tn")
    if kind == "nt":
        return _dot1(ct, b, "nn"), _dot1(ct, a, "tn")
    return _dot1(b, ct, "nt"), _dot1(a, ct, "nn")


_mm.defvjp(_mm_fwd, _mm_bwd)


def _chunk_masks(c):
    row = lax.broadcasted_iota(jnp.int32, (c, c), 0)
    col = lax.broadcasted_iota(jnp.int32, (c, c), 1)
    return (row >= col)[None], (row > col)[None], (row == col).astype(F32)[None]


def _wkv_aab(lw, cl, a, b):
    _, strict, _ = _chunk_masks(a.shape[1])
    return jnp.where(strict, _mm(a * jnp.exp(cl - lw), b * jnp.exp(-cl), "nt"), 0.0)


def _tri_inverse(x):
    c = x.shape[1]
    p = _chunk_masks(c)[2] + x
    for _ in range(int(math.log2(c)) - 1):
        x = _dot1(x, x, "nn")
        p = p + _dot1(p, x, "nn")
    return p


def _wkv_apply(s0, r, lw, cl, k, v, a, b, p):
    c = r.shape[1]
    incl, strict, _ = _chunk_masks(c)
    gi = jnp.exp(-cl)
    left = jnp.concatenate([a * jnp.exp(cl - lw), r * jnp.exp(cl)], axis=1)
    right = jnp.concatenate([b * gi, k * gi], axis=1)
    m = _mm(left, right, "nt")
    z0 = _mm(left, s0, "nt")
    a_ak = jnp.where(strict, m[:, :c, c:], 0.0)
    row = lax.broadcasted_iota(jnp.int32, (c, 2 * c), 0)
    col = lax.broadcasted_iota(jnp.int32, (c, 2 * c), 1)
    a_r = jnp.where((row >= col % c)[None], m[:, c:, :], 0.0)
    sa = _mm(p, z0[:, :c] + _mm(a_ak, v, "nn"), "nn")
    sa_v = jnp.concatenate([sa, v], axis=1)
    y = z0[:, c:] + _mm(a_r, sa_v, "nn")
    s1 = (s0 + _mm(sa_v, right, "tn")) * jnp.exp(cl[:, c - 1:c, :])
    return y, s1


def _rwkv_post(y, r, k2, v, g, lnx_w, lnx_b, r_k):
    yc = y - _head_sums(y) * (1.0 / N)
    var = _head_sums(yc * yc) * (1.0 / N)
    yn = yc * lax.rsqrt(var + LNX_EPS) * lnx_w + lnx_b
    bonus = _head_sums(r * k2 * r_k) * v
    return (yn + bonus) * (g * _sigmoid(g))


def _fox_pre(q, k, f, q_g, k_g, f_b):
    qn = q * lax.rsqrt(_head_sums(q * q) * (1.0 / N) + RMS_EPS) * q_g
    kn = k * lax.rsqrt(_head_sums(k * k) * (1.0 / N) + RMS_EPS) * k_g
    x = f + f_b
    return qn, kn, jnp.minimum(x, 0.0) - jnp.log(1.0 + jnp.exp(-jnp.abs(x)))


def _norm_proj(x, g, wts):
    s = x.shape[0]
    k = len(wts)

    def body(x_ref, g_ref, *refs):
        w_refs, h_ref, o_refs = refs[:k], refs[k], refs[k + 1:]
        xv = x_ref[...]
        h = (xv * lax.rsqrt(jnp.mean(xv * xv, axis=-1, keepdims=True) + RMS_EPS) * g_ref[...]).astype(BF16)
        h_ref[...] = h
        for w_ref, o_ref in zip(w_refs, o_refs):
            o_ref[...] = _bdot_nt(h, w_ref[...])

    tok = lambda n: pl.BlockSpec((TOK_TILE, n), lambda i: (i, 0))
    out = pl.pallas_call(
        body, name="norm_proj", grid=(s // TOK_TILE,),
        in_specs=[tok(D), pl.BlockSpec((1, D), lambda i: (0, 0))] + [pl.BlockSpec(w.shape, lambda i: (0, 0)) for w in wts],
        out_specs=[tok(D)] + [tok(w.shape[0]) for w in wts],
        out_shape=[jax.ShapeDtypeStruct((s, D), BF16)] + [jax.ShapeDtypeStruct((s, w.shape[0]), F32) for w in wts],
        compiler_params=_params("arbitrary"))(x, g, *wts)
    return out[0], out[1:]


def _proj_wgrad_early(h, dub, dug, duf, head_rows):
    s = dub.shape[0]
    steps = s // TOK_TILE
    seg_rows = (_WT_SEGMENTS[1], _WT_SEGMENTS[2], _WT_SEGMENTS[3])

    def body(h_ref, b_ref, g_ref, f_ref, o_ref, head_ref, *accs):
        @pl.when(pl.program_id(0) == 0)
        def _():
            for acc in accs:
                acc[...] = jnp.zeros_like(acc)

        h = h_ref[...]
        for acc, du_ref in zip(accs, (b_ref, g_ref, f_ref)):
            acc[...] += _bdot_tn(du_ref[...], h)

        @pl.when(pl.program_id(0) == steps - 1)
        def _():
            head_ref[...] = accs[0][:head_rows, :]
            for j in range(EARLY_FROM, N_DEV):
                lo, hi = COLS_PER_DEV * j, COLS_PER_DEV * (j + 1)
                parts = []
                for acc, (row, n) in sorted(zip(accs, seg_rows), key=lambda t: t[1][0]):
                    first, last = max(lo, row), min(hi, row + n)
                    if first < last:
                        parts.append(acc[first - row:last - row, :])
                o_ref[j - EARLY_FROM] = (parts[0] if len(parts) == 1 else jnp.concatenate(parts, axis=0)).astype(BF16)

    tok = lambda n: pl.BlockSpec((TOK_TILE, n), lambda i: (i, 0))
    n_early = N_DEV - EARLY_FROM
    return pl.pallas_call(
        body, name="wgrad_bgf", grid=(steps,), in_specs=[tok(D), tok(NB), tok(NG), tok(NF)],
        out_specs=[pl.BlockSpec((n_early, COLS_PER_DEV, D), lambda i: (0, 0, 0)),
                   pl.BlockSpec((head_rows, D), lambda i: (0, 0))],
        out_shape=[jax.ShapeDtypeStruct((n_early, COLS_PER_DEV, D), BF16), jax.ShapeDtypeStruct((head_rows, D), F32)],
        scratch_shapes=[pltpu.VMEM((n, D), F32) for n in (NB, NG, NF)],
        compiler_params=_params("arbitrary"))(h, dub, dug, duf)


def _proj_wgrad_late(h, dua, dwt_b_head):
    s = dua.shape[0]
    steps = s // TOK_TILE

    def body(h_ref, du_ref, b_ref, o_ref, acc):
        @pl.when(pl.program_id(0) == 0)
        def _():
            acc[...] = jnp.zeros_like(acc)

        acc[...] += _bdot_tn(du_ref[...], h_ref[...])

        @pl.when(pl.program_id(0) == steps - 1)
        def _():
            for j in range(EARLY_FROM):
                lo, hi = COLS_PER_DEV * j, COLS_PER_DEV * (j + 1)
                parts = [acc[lo:min(hi, NA), :]] + ([b_ref[:hi - NA, :]] if hi > NA else [])
                o_ref[j] = (parts[0] if len(parts) == 1 else jnp.concatenate(parts, axis=0)).astype(BF16)

    return pl.pallas_call(
        body, name="wgrad_a", grid=(steps,),
        in_specs=[pl.BlockSpec((TOK_TILE, D), lambda i: (i, 0)), pl.BlockSpec((TOK_TILE, NA), lambda i: (i, 0)),
                  pl.BlockSpec(dwt_b_head.shape, lambda i: (0, 0))],
        out_specs=pl.BlockSpec((EARLY_FROM, COLS_PER_DEV, D), lambda i: (0, 0, 0)),
        out_shape=jax.ShapeDtypeStruct((EARLY_FROM, COLS_PER_DEV, D), BF16),
        scratch_shapes=[pltpu.VMEM((NA, D), F32)], compiler_params=_params("arbitrary"))(h, dua, dwt_b_head)


def _proj_xgrad(x, g, dx2, dus, ws, slabs, owners):
    s = x.shape[0]
    tile = XGRAD_TILE
    k = len(dus)
    nx = len(slabs)
    n_in = 3 + 2 * k + nx

    def body(*refs):
        x_ref, g_ref, dx2_ref = refs[:3]
        du_refs, w_refs = refs[3:3 + k], refs[3 + k:3 + 2 * k]
        src_refs = refs[3 + 2 * k:3 + 2 * k + nx]
        dx_ref, dg_ref = refs[n_in:n_in + 2]
        dst_refs = refs[n_in + 2:n_in + 2 + nx]
        start, wait = _exchange_ops(src_refs, dst_refs, owners, refs[n_in + 2 + nx:])

        @pl.when(pl.program_id(0) == 0)
        def _():
            dg_ref[...] = jnp.zeros_like(dg_ref)
            start()

        dh = _bdot(du_refs[0][...], w_refs[0][...])
        for du_ref, w_ref in zip(du_refs[1:], w_refs[1:]):
            dh += _bdot(du_ref[...], w_ref[...])
        xv = x_ref[...]
        rs = lax.rsqrt(jnp.mean(xv * xv, axis=-1, keepdims=True) + RMS_EPS)
        xn = xv * rs
        dg_ref[...] += jnp.sum(dh * xn, axis=0, keepdims=True)
        dxn = dh * g_ref[...]
        dx_ref[...] = rs * (dxn - xn * jnp.mean(dxn * xn, axis=-1, keepdims=True)) + dx2_ref[...]

        @pl.when(pl.program_id(0) == s // tile - 1)
        def _():
            wait()

    tok = lambda n: pl.BlockSpec((tile, n), lambda i: (i, 0))
    fixed = lambda a: pl.BlockSpec(a.shape, lambda i: (0,) * a.ndim)
    out = pl.pallas_call(
        body, name="proj_xgrad", grid=(s // tile,),
        in_specs=([tok(D), fixed(g), tok(D)] + [tok(du.shape[1]) for du in dus] + [fixed(w) for w in ws]
                  + _hbm_specs(nx)),
        out_specs=[tok(D), pl.BlockSpec((1, D), lambda i: (0, 0))] + _hbm_specs(nx),
        out_shape=[jax.ShapeDtypeStruct((s, D), F32), jax.ShapeDtypeStruct((1, D), F32)] + _received_shapes(slabs, owners),
        scratch_shapes=_exchange_scratch(nx),
        compiler_params=_params("arbitrary"))(x, g, dx2, *dus, *ws, *slabs)
    return out[0], out[1], out[2:]


def _tail(x, target, ya, o, ub, ug, w_oa, w_ob, w_o, fg):
    s = x.shape[0]
    tile = TOK_TILE

    def body(x_ref, t_ref, ya_ref, o_ref, gb_ref, ug_ref, woa_ref, wob_ref, wo_ref, fg_ref,
             loss_ref, dfg_ref, dwo_ref, dwoa_ref, dwob_ref, dx2_ref, dya_ref, do_ref, dgb_ref, dug_ref):
        @pl.when(pl.program_id(0) == 0)
        def _():
            for r in (loss_ref, dfg_ref, dwo_ref, dwoa_ref, dwob_ref):
                r[...] = jnp.zeros_like(r)

        ya_v = ya_ref[...]
        gate_b = gb_ref[...]
        sg_b = _sigmoid(gate_b)
        silu_b = gate_b * sg_b
        o_v = jnp.concatenate([o_ref[h] for h in range(H)], axis=-1)
        yb_v = o_v * silu_b
        big_a = _bdot(ya_v, woa_ref[...])
        big_b = _bdot(yb_v, wob_ref[...])
        sa = _sigmoid(ug_ref[:, :D])
        sb = _sigmoid(ug_ref[:, D:])
        merged = sa * big_a + sb * big_b
        x2 = x_ref[...] + _bdot(merged, wo_ref[...])
        rs = lax.rsqrt(jnp.mean(x2 * x2, axis=-1, keepdims=True) + RMS_EPS)
        xn = x2 * rs
        err = xn * fg_ref[...] - t_ref[...]
        loss_ref[...] += (0.5 / D) * jnp.sum(err * err)
        dout = err * (1.0 / D)
        dfg_ref[...] += jnp.sum(dout * xn, axis=0, keepdims=True)
        dxn = dout * fg_ref[...]
        dx2 = rs * (dxn - xn * jnp.mean(dxn * xn, axis=-1, keepdims=True))
        dx2_ref[...] = dx2
        dwo_ref[...] += _bdot_tn(merged, dx2)
        dmerged = _bdot_nt(dx2, wo_ref[...])
        dbig_a = dmerged * sa
        dbig_b = dmerged * sb
        dug_ref[:, :D] = dmerged * big_a * sa * (1.0 - sa)
        dug_ref[:, D:] = dmerged * big_b * sb * (1.0 - sb)
        dwoa_ref[...] += _bdot_tn(ya_v, dbig_a)
        dwob_ref[...] += _bdot_tn(yb_v, dbig_b)
        dya_ref[...] = _bdot_nt(dbig_a, woa_ref[...])
        dyb = _bdot_nt(dbig_b, wob_ref[...])
        dgb_ref[...] = dyb * o_v * (sg_b * (1.0 + gate_b * (1.0 - sg_b)))
        _dov = dyb * silu_b
        for h in range(H):
            do_ref[h] = _dov[:, N * h:N * (h + 1)]

    tok = lambda n: pl.BlockSpec((tile, n), lambda i: (i, 0))
    hm = pl.BlockSpec((H, tile, N), lambda i: (0, i, 0))
    fixed = lambda shape: pl.BlockSpec(shape, lambda i: (0,) * len(shape))
    f32 = lambda *shape: jax.ShapeDtypeStruct(shape, F32)
    return pl.pallas_call(
        body, name="tail", grid=(s // tile,),
        in_specs=[tok(D), tok(D), tok(DA), hm, pl.BlockSpec((tile, DA), lambda i: (i, 3)), tok(NG),
                  fixed((DA, D)), fixed((DA, D)), fixed((D, D)), fixed((1, D))],
        out_specs=[fixed((1, 1)), fixed((1, D)), fixed((D, D)), fixed((DA, D)), fixed((DA, D)),
                   tok(D), tok(DA), hm, tok(DA), tok(NG)],
        out_shape=[f32(1, 1), f32(1, D), f32(D, D), f32(DA, D), f32(DA, D),
                   f32(s, D), f32(s, DA), f32(H, s, N), f32(s, DA), f32(s, NG)],
        compiler_params=_params("arbitrary"))(x, target, ya, o, ub, ug, w_oa, w_ob, w_o, fg)


def _pre_operands(ua_ref, prev_ref, first):
    cur = ua_ref[...]
    t = cur.shape[0]
    prev_row = jnp.where(first, 0.0, prev_ref[7:8, :])
    rows = lax.broadcasted_iota(jnp.int32, cur.shape, 0)
    sh = jnp.where(rows == 0, prev_row, pltpu.roll(cur, 1, axis=0))
    ops = []
    for c0, n in ((0, DA), (DA, DA), (2 * DA, DA), (3 * DA + 2 * RANK, DA), (3 * DA, RANK), (3 * DA + RANK, RANK)):
        ops += [cur[:, c0:c0 + n], sh[:, c0:c0 + n]]
    del t
    return ops


def _ua_specs(tile, order):
    blocks = tile // 8
    return [pl.BlockSpec((tile, NA), lambda i: (order(i), 0)),
            pl.BlockSpec((8, NA), lambda i: (jnp.maximum(order(i) * blocks - 1, 0), 0))]


def _rwkv_pre_fwd(ua, pre_params):
    s = ua.shape[0]
    tile = HEAD_TILE

    def body(ua_ref, prev_ref, *refs):
        p_refs, o_refs = refs[:len(pre_params)], refs[len(pre_params):]
        ops = _pre_operands(ua_ref, prev_ref, pl.program_id(0) == 0)
        outs = _rwkv_pre(*ops, *[p[...] for p in p_refs])
        for o_ref, val in zip(o_refs, outs):
            o_ref[...] = val

    tm = pl.BlockSpec((tile, DA), lambda i: (i, 0))
    return pl.pallas_call(
        body, name="rwkv_pre_fwd", grid=(s // tile,),
        in_specs=_ua_specs(tile, lambda i: i) + [pl.BlockSpec(p.shape, lambda i, nd=p.ndim: (0,) * nd) for p in pre_params],
        out_specs=[tm] * 8, out_shape=[jax.ShapeDtypeStruct((s, DA), F32)] * 8,
        compiler_params=_params("arbitrary"))(ua, ua, *pre_params)


def _rwkv_pre_bwd(ua, pre_params, cots):
    s = ua.shape[0]
    tile = HEAD_TILE
    nt = s // tile
    n_p = len(pre_params)

    def body(ua_ref, prev_ref, *refs):
        p_refs, c_refs = refs[:n_p], refs[n_p:n_p + 11]
        dua_ref = refs[n_p + 11]
        dp_refs = refs[n_p + 12:n_p + 12 + n_p]
        carry_ref = refs[-1]
        i = pl.program_id(0)

        @pl.when(i == 0)
        def _():
            carry_ref[...] = jnp.zeros_like(carry_ref)
            for r in dp_refs:
                r[...] = jnp.zeros_like(r)

        ops = _pre_operands(ua_ref, prev_ref, i == nt - 1)
        _, vjp = jax.vjp(_rwkv_pre, *ops, *[p[...] for p in p_refs])
        c = [r[...] for r in c_refs]
        grads = vjp((c[0] + c[1], c[2], c[3], c[4] + c[5], c[6] + c[7], c[8], c[9], c[10]))
        d_ops, d_par = grads[:12], grads[12:]
        for r, val in zip(dp_refs, d_par):
            r[...] += val
        d_cur = jnp.concatenate([d_ops[0], d_ops[2], d_ops[4], d_ops[8], d_ops[10], d_ops[6]], axis=-1)
        d_sh = jnp.concatenate([d_ops[1], d_ops[3], d_ops[5], d_ops[9], d_ops[11], d_ops[7]], axis=-1)
        rows = lax.broadcasted_iota(jnp.int32, d_sh.shape, 0)
        dua_ref[...] = d_cur + jnp.where(rows == tile - 1, carry_ref[...], pltpu.roll(d_sh, tile - 1, axis=0))
        carry_ref[...] = d_sh[0:1, :]

    rev = lambda i: nt - 1 - i
    tm = pl.BlockSpec((tile, DA), lambda i: (rev(i), 0))
    fixed = [pl.BlockSpec(p.shape, lambda i, nd=p.ndim: (0,) * nd) for p in pre_params]
    return pl.pallas_call(
        body, name="rwkv_pre_bwd", grid=(nt,),
        in_specs=_ua_specs(tile, rev) + fixed + [tm] * 11,
        out_specs=[pl.BlockSpec((tile, NA), lambda i: (rev(i), 0))] + fixed,
        out_shape=[jax.ShapeDtypeStruct((s, NA), F32)] + [jax.ShapeDtypeStruct(p.shape, F32) for p in pre_params],
        scratch_shapes=[pltpu.VMEM((1, NA), F32)],
        compiler_params=_params("arbitrary"))(ua, ua, *pre_params, *cots)


def _wkv_fwd(seq):
    s = seq[0].shape[0]
    nc = s // WKV_CHUNK

    def body(r_ref, lw_ref, cl_ref, k_ref, v_ref, a_ref, b_ref, y_ref, ck_ref, p_ref, state):
        @pl.when(pl.program_id(0) == 0)
        def _():
            state[...] = jnp.zeros_like(state)

        r, lw, cl, k, v, a, b = (jnp.stack(_to_heads(ref[...])) for ref in (r_ref, lw_ref, cl_ref, k_ref, v_ref, a_ref,
                                                                             b_ref))
        s0 = state[...]
        ck_ref[0] = s0
        p = _tri_inverse(_wkv_aab(lw, cl, a, b))
        p_ref[0] = p
        y, s1 = _wkv_apply(s0, r, lw, cl, k, v, a, b, p)
        y_ref[...] = jnp.concatenate([y[h] for h in range(H)], axis=-1)
        state[...] = s1

    tm = pl.BlockSpec((WKV_CHUNK, DA), lambda c: (c, 0))
    per_chunk = lambda m: pl.BlockSpec((1, H, m, m), lambda c: (c, 0, 0, 0))
    return pl.pallas_call(
        body, name="wkv_fwd", grid=(nc,), in_specs=[tm] * 7,
        out_specs=[tm, per_chunk(N), per_chunk(WKV_CHUNK)],
        out_shape=[jax.ShapeDtypeStruct((s, DA), F32), jax.ShapeDtypeStruct((nc, H, N, N), F32),
                   jax.ShapeDtypeStruct((nc, H, WKV_CHUNK, WKV_CHUNK), F32)],
        scratch_shapes=[pltpu.VMEM((H, N, N), F32)], compiler_params=_params("arbitrary"))(*seq)


def _wkv_bwd(seq, ckpt, pinv, dy, slabs, owners):
    s = seq[0].shape[0]
    nc = s // WKV_CHUNK
    nx = len(slabs)

    def body(r_ref, lw_ref, cl_ref, k_ref, v_ref, a_ref, b_ref, ck_ref, p_ref, dy_ref, *refs):
        src_refs, d_refs, dst_refs = refs[:nx], refs[nx:nx + 7], refs[nx + 7:2 * nx + 7]
        dstate = refs[2 * nx + 7]
        start, wait = _exchange_ops(src_refs, dst_refs, owners, refs[2 * nx + 8:])

        @pl.when(pl.program_id(0) == 0)
        def _():
            dstate[...] = jnp.zeros_like(dstate)
            start()

        p = p_ref[0]
        r, lw, cl, k, v, a, b, dy = (jnp.stack(_to_heads(ref[...])) for ref in (r_ref, lw_ref, cl_ref, k_ref, v_ref,
                                                                                 a_ref, b_ref, dy_ref))
        _, vjp = jax.vjp(_wkv_apply, ck_ref[0], r, lw, cl, k, v, a, b, p)
        ds0, dr, dlw, dcl, dk, dv, da, db, dp = vjp((dy, dstate[...]))
        dstate[...] = ds0
        _, vjp_x = jax.vjp(_wkv_aab, lw, cl, a, b)
        dlw2, dcl2, da2, db2 = vjp_x(_dot1(_dot1(p, dp, "tn"), p, "nt"))
        for d_ref, val in zip(d_refs, (dr, dlw + dlw2, dcl + dcl2, dk, dv, da + da2, db + db2)):
            d_ref[...] = jnp.concatenate([val[h] for h in range(H)], axis=-1)

        @pl.when(pl.program_id(0) == nc - 1)
        def _():
            wait()

    tm = pl.BlockSpec((WKV_CHUNK, DA), lambda c: (nc - 1 - c, 0))
    per_chunk = lambda m: pl.BlockSpec((1, H, m, m), lambda c: (nc - 1 - c, 0, 0, 0))
    out = pl.pallas_call(
        body, name="wkv_bwd", grid=(nc,),
        in_specs=[tm] * 7 + [per_chunk(N), per_chunk(WKV_CHUNK), tm] + _hbm_specs(nx),
        out_specs=[tm] * 7 + _hbm_specs(nx),
        out_shape=[jax.ShapeDtypeStruct((s, DA), F32)] * 7 + _received_shapes(slabs, owners),
        scratch_shapes=[pltpu.VMEM((H, N, N), F32)] + _exchange_scratch(nx),
        compiler_params=_params("arbitrary"))(*seq, ckpt, pinv, dy, *slabs)
    return out[:7], out[7:]


def _rwkv_post_fwd(y, r, k2, v, g, post_params):
    s = y.shape[0]
    tile = TOK_TILE

    def body(*refs):
        refs[-1][...] = _rwkv_post(*[ref[...] for ref in refs[:-1]])

    tm = pl.BlockSpec((tile, DA), lambda i: (i, 0))
    par = pl.BlockSpec((1, DA), lambda i: (0, 0))
    return pl.pallas_call(
        body, name="rwkv_post_fwd", grid=(s // tile,), in_specs=[tm] * 5 + [par] * 3,
        out_specs=tm, out_shape=jax.ShapeDtypeStruct((s, DA), F32),
        compiler_params=_params("arbitrary"))(y, r, k2, v, g, *post_params)


def _rwkv_post_bwd(y, r, k2, v, g, post_params, dya, slabs, lo):
    s = y.shape[0]
    tile = HEAD_TILE

    def body(y_ref, r_ref, k_ref, v_ref, g_ref, w_ref, b_ref, rk_ref, dya_ref, s_ref, *refs):
        d_refs, p_ref = refs[:8], refs[8]
        start, wait = _pair_swap_ops(s_ref, p_ref, lo, refs[9:])

        @pl.when(pl.program_id(0) == 0)
        def _():
            for ref in d_refs[5:]:
                ref[...] = jnp.zeros_like(ref)
            start()

        _, vjp = jax.vjp(_rwkv_post, *[ref[...] for ref in (y_ref, r_ref, k_ref, v_ref, g_ref, w_ref, b_ref, rk_ref)])
        grads = vjp(dya_ref[...])
        for ref, val in zip(d_refs[:5], grads[:5]):
            ref[...] = val
        for ref, val in zip(d_refs[5:], grads[5:]):
            ref[...] += val

        @pl.when(pl.program_id(0) == s // tile - 1)
        def _():
            wait()

    tm = pl.BlockSpec((tile, DA), lambda i: (i, 0))
    par = pl.BlockSpec((1, DA), lambda i: (0, 0))
    return pl.pallas_call(
        body, name="rwkv_post_bwd", grid=(s // tile,),
        in_specs=[tm] * 5 + [par] * 3 + [tm] + _hbm_specs(1),
        out_specs=[tm] * 5 + [par] * 3 + _hbm_specs(1),
        out_shape=[jax.ShapeDtypeStruct((s, DA), F32)] * 5 + [jax.ShapeDtypeStruct((1, DA), F32)] * 3
        + [jax.ShapeDtypeStruct(slabs.shape, slabs.dtype)],
        scratch_shapes=_pair_swap_scratch(slabs.shape[0]),
        compiler_params=_params("arbitrary"))(y, r, k2, v, g, *post_params, dya, slabs)


def _tri(t):
    return (lax.broadcasted_iota(jnp.int32, (t, t), 0) >= lax.broadcasted_iota(jnp.int32, (t, t), 1)).astype(F32)


def _fox_pre_fwd(ub, uf, q_g, k_g, f_b):
    s = ub.shape[0]
    tile = HEAD_TILE

    def body(ub_ref, uf_ref, qg_ref, kg_ref, fb_ref, q_ref, k_ref, v_ref, cum_ref, carry):
        @pl.when(pl.program_id(0) == 0)
        def _():
            carry[...] = jnp.zeros_like(carry)

        qn, kn, logf = _fox_pre(ub_ref[:, :DA], ub_ref[:, DA:2 * DA], uf_ref[...], qg_ref[...], kg_ref[...],
                                fb_ref[...])
        for h, (q_col, k_col) in enumerate(zip(_to_heads(qn), _to_heads(kn))):
            q_ref[h] = q_col
            k_ref[h] = k_col
        v_ref[...] = _heads(ub_ref, 2 * DA)
        cum = jnp.dot(_tri(tile), logf, precision=HI, preferred_element_type=F32) + carry[...]
        cum_ref[...] = cum
        carry[...] = cum[tile - 1:tile, :]

    hm = pl.BlockSpec((H, tile, N), lambda i: (0, i, 0))
    fixed = lambda shape: pl.BlockSpec(shape, lambda i: (0,) * len(shape))
    return pl.pallas_call(
        body, name="fox_pre_fwd", grid=(s // tile,),
        in_specs=[pl.BlockSpec((tile, NB), lambda i: (i, 0)), pl.BlockSpec((tile, NF), lambda i: (i, 0)),
                  fixed((1, DA)), fixed((1, DA)), fixed((1, NF))],
        out_specs=[hm] * 3 + [pl.BlockSpec((tile, NF), lambda i: (i, 0))],
        out_shape=[jax.ShapeDtypeStruct((H, s, N), F32)] * 3 + [jax.ShapeDtypeStruct((s, NF), F32)],
        scratch_shapes=[pltpu.VMEM((1, NF), F32)], compiler_params=_params("arbitrary"))(ub, uf, q_g, k_g, f_b)


def _fox_pre_bwd(ub, uf, q_g, k_g, f_b, dqn, dkn, dvf, dgate, dcum_q, dcum_k):
    s = ub.shape[0]
    tile = HEAD_TILE
    nt = s // tile

    def body(ub_ref, uf_ref, qg_ref, kg_ref, fb_ref, dq_ref, dk_ref, dv_ref, dgate_ref, dcq_ref, dck_ref,
             dub_ref, duf_ref, dqg_ref, dkg_ref, dfb_ref, carry):
        @pl.when(pl.program_id(0) == 0)
        def _():
            carry[...] = jnp.zeros_like(carry)
            for ref in (dqg_ref, dkg_ref, dfb_ref):
                ref[...] = jnp.zeros_like(ref)

        dcum = dcq_ref[...] + dck_ref[...]
        dlogf = lax.dot_general(_tri(tile), dcum, (((0,), (0,)), ((), ())), precision=HI,
                                preferred_element_type=F32) + carry[...]
        carry[...] = dlogf[0:1, :]
        _, vjp = jax.vjp(_fox_pre, ub_ref[:, :DA], ub_ref[:, DA:2 * DA], uf_ref[...], qg_ref[...], kg_ref[...],
                         fb_ref[...])
        d_q, d_k, d_f, d_qg, d_kg, d_fb = vjp((_from_heads(dq_ref), _from_heads(dk_ref), dlogf))
        dub_ref[:, :DA] = d_q
        dub_ref[:, DA:2 * DA] = d_k
        _store_heads(dub_ref, 2 * DA, dv_ref[...])
        dub_ref[:, 3 * DA:] = dgate_ref[...]
        duf_ref[...] = d_f
        dqg_ref[...] += functools.reduce(jnp.add, _to_heads(d_qg))
        dkg_ref[...] += functools.reduce(jnp.add, _to_heads(d_kg))
        dfb_ref[...] += d_fb

    rev = lambda i: nt - 1 - i
    hm = pl.BlockSpec((H, tile, N), lambda i: (0, rev(i), 0))
    tok = lambda n: pl.BlockSpec((tile, n), lambda i: (rev(i), 0))
    fixed = lambda shape: pl.BlockSpec(shape, lambda i: (0,) * len(shape))
    return pl.pallas_call(
        body, name="fox_pre_bwd", grid=(nt,),
        in_specs=[tok(NB), tok(NF), fixed((1, DA)), fixed((1, DA)), fixed((1, NF)), hm, hm, hm, tok(DA), tok(NF),
                  tok(NF)],
        out_specs=[tok(NB), tok(NF), fixed((1, N)), fixed((1, N)), fixed((1, NF))],
        out_shape=[jax.ShapeDtypeStruct((s, NB), F32), jax.ShapeDtypeStruct((s, NF), F32),
                   jax.ShapeDtypeStruct((1, N), F32), jax.ShapeDtypeStruct((1, N), F32),
                   jax.ShapeDtypeStruct((1, NF), F32)],
        scratch_shapes=[pltpu.VMEM((1, NF), F32)],
        compiler_params=_params("arbitrary"))(ub, uf, q_g, k_g, f_b, dqn, dkn, dvf, dgate, dcum_q, dcum_k)


def _att_groups(s):
    blocks = s // ATT_TILE
    per = max(1, blocks // ATT_GROUPS)
    return per, blocks // per


def _att_parts(n, width):
    return ([(0, n - width, False)] if n > width else []) + [(n - width, n, True)]


def _att_scores(q_bf, k_ref, ck_ref, lo, hi, masked, row_offset):
    scores = _bdot_nt(q_bf, k_ref[0, lo:hi, :]) - ck_ref[0, :, lo:hi]
    if masked:
        rows = row_offset + lax.broadcasted_iota(jnp.int32, scores.shape, 0)
        scores = jnp.where(rows >= lax.broadcasted_iota(jnp.int32, scores.shape, 1), scores, -1e30)
    return scores


def _fox_attn_fwd(q, k, v, cum_q, cum_k):
    s = q.shape[1]
    t = ATT_TILE
    per, groups = _att_groups(s)

    def body(q_ref, k_ref, v_ref, cq_ref, ck_ref, o_ref, lse_ref):
        qi = pl.program_id(1)
        for g in range(groups):
            @pl.when(qi // per == g)
            def _(g=g):
                q_bf = (q_ref[0] * ATT_SCALE).astype(BF16)
                parts = _att_parts((g + 1) * per * t, per * t)
                scores = [_att_scores(q_bf, k_ref, ck_ref, lo, hi, masked, (qi - g * per) * t)
                          for lo, hi, masked in parts]
                m = functools.reduce(jnp.maximum, [jnp.max(sc, axis=-1, keepdims=True) for sc in scores])
                l, acc = 0.0, 0.0
                for sc, (lo, hi, _) in zip(scores, parts):
                    p = jnp.exp(sc - m)
                    l += jnp.sum(p, axis=-1, keepdims=True)
                    acc += _bdot(p, v_ref[0, lo:hi, :])
                o_ref[0] = acc / l
                lse_ref[0] = m + jnp.log(l) + cq_ref[0]

    qb = pl.BlockSpec((1, t, N), lambda h, i: (h, i, 0))
    kb = pl.BlockSpec((1, s, N), lambda h, i: (h, 0, 0))
    return pl.pallas_call(
        body, name="fox_attn_fwd", grid=(H, s // t),
        in_specs=[qb, kb, kb, pl.BlockSpec((1, t, 1), lambda h, i: (h, i, 0)),
                  pl.BlockSpec((1, 1, s), lambda h, i: (h, 0, 0))],
        out_specs=[qb, pl.BlockSpec((1, t, 1), lambda h, i: (h, i, 0))],
        out_shape=[jax.ShapeDtypeStruct((H, s, N), F32), jax.ShapeDtypeStruct((H, s, 1), F32)],
        compiler_params=_params("arbitrary", "arbitrary"))(q, k, v, cum_q, cum_k)


def _fox_attn_bwd(q, k, v, cum_q, cum_k, o, lse, do, slabs, owners):
    s = q.shape[1]
    t = ATT_TILE
    per, groups = _att_groups(s)
    nx = len(slabs)

    def body(q_ref, k_ref, v_ref, cq_ref, ck_ref, o_ref, lse_ref, do_ref, *refs):
        src_refs, (dq_ref, dk_ref, dv_ref, dcq_ref, dck_ref) = refs[:nx], refs[nx:nx + 5]
        start, wait = _exchange_ops(src_refs, refs[nx + 5:2 * nx + 5], owners, refs[2 * nx + 5:])
        qi = pl.program_id(1)

        @pl.when((pl.program_id(0) == 0) & (qi == 0))
        def _():
            start()

        @pl.when(qi == 0)
        def _():
            for ref in (dk_ref, dv_ref, dck_ref):
                ref[...] = jnp.zeros_like(ref)

        for g in range(groups):
            @pl.when(qi // per == g)
            def _(g=g):
                q_bf, do_bf = (q_ref[0] * ATT_SCALE).astype(BF16), do_ref[0].astype(BF16)
                row_term = cq_ref[0] - lse_ref[0]
                delta = jnp.sum(do_ref[0] * o_ref[0], axis=-1, keepdims=True)
                dq, dcq = 0.0, 0.0
                for lo, hi, masked in _att_parts((g + 1) * per * t, per * t):
                    p = jnp.exp(_att_scores(q_bf, k_ref, ck_ref, lo, hi, masked, (qi - g * per) * t) + row_term)
                    ds = p * (_bdot_nt(do_bf, v_ref[0, lo:hi, :]) - delta)
                    dq += _bdot(ds, k_ref[0, lo:hi, :])
                    dcq += jnp.sum(ds, axis=-1, keepdims=True)
                    dk_ref[0, lo:hi, :] += _bdot_tn(ds, q_bf)
                    dv_ref[0, lo:hi, :] += _bdot_tn(p, do_bf)
                    dck_ref[0, :, lo:hi] -= jnp.sum(ds, axis=0, keepdims=True)
                dq_ref[0] = dq * ATT_SCALE
                dcq_ref[0] = dcq

        @pl.when((pl.program_id(0) == H - 1) & (qi == s // t - 1))
        def _():
            wait()

    qb = pl.BlockSpec((1, t, N), lambda h, i: (h, i, 0))
    kb = pl.BlockSpec((1, s, N), lambda h, i: (h, 0, 0))
    cqb = pl.BlockSpec((1, t, 1), lambda h, i: (h, i, 0))
    ckb = pl.BlockSpec((1, 1, s), lambda h, i: (h, 0, 0))
    f32 = lambda *shape: jax.ShapeDtypeStruct(shape, F32)
    out = pl.pallas_call(
        body, name="fox_attn_bwd", grid=(H, s // t),
        in_specs=[qb, kb, kb, cqb, ckb, qb, cqb, qb] + _hbm_specs(nx), out_specs=[qb, kb, kb, cqb, ckb] + _hbm_specs(nx),
        out_shape=[f32(H, s, N), f32(H, s, N), f32(H, s, N), f32(H, s, 1), f32(H, 1, s)]
        + _received_shapes(slabs, owners),
        scratch_shapes=_exchange_scratch(nx),
        compiler_params=_params("arbitrary", "arbitrary"))(q, k, v, cum_q, cum_k, o, lse, do, *slabs)
    return out[:5], out[5:]


def _local_step(x, target, w, p):
    mu = p["shift_mu"]
    lora_matrix = lambda a: jnp.moveaxis(a, 0, 1).reshape(RANK, DA).astype(F32)
    pre_params = (mu[:, 0:DA], mu[:, DA:2 * DA], mu[:, 2 * DA:3 * DA], mu[:, 3 * DA + 2 * RANK:],
                  mu[:, 3 * DA:3 * DA + RANK], mu[:, 3 * DA + RANK:3 * DA + 2 * RANK],
                  lora_matrix(w["w_lora_up"]), p["w0"], lora_matrix(w["a_lora_up"]), p["a0"], p["k_k"], p["k_a"])
    post_params = (p["lnx_w"], p["lnx_b"], p["r_k"])
    q_g, k_g = jnp.tile(p["q_norm_g"], (1, H)), jnp.tile(p["k_norm_g"], (1, H))
    f_b = jnp.pad(p["f_bias"], ((0, 0), (0, NF - H)))
    fg = p["final_norm_g"].reshape(1, D)

    h, (ua, ub, ug, uf) = _norm_proj(x, p["norm_g"], (w["in_a"], w["in_b"], w["in_g"], w["in_f"]))
    r, lw, cl, k2, v, av, bv, gg = _rwkv_pre_fwd(ua, pre_params)
    y, ckpt, pinv = _wkv_fwd((r, lw, cl, k2, v, av, bv))
    ya = _rwkv_post_fwd(y, r, k2, v, gg, post_params)
    qn, kn, vf, cum = _fox_pre_fwd(ub, uf, q_g, k_g, f_b)
    cum_t = cum[:, :H].T
    cum_q, cum_k = cum_t[:, :, None], cum_t[:, None, :]
    o, lse = _fox_attn_fwd(qn, kn, vf, cum_q, cum_k)

    (loss, dfg, dwo, dwoa, dwob, dx2, dya, do, dgate_b, dug) = _tail(
        x, target, ya, o, ub, ug, w["w_out_a"], w["w_out_b"], w["w_out"], fg)
    everyone = (0, N_DEV)
    (dqn, dkn, dvf, dcq, dck), (recv_woa, recv_wob, recv_wo) = _fox_attn_bwd(
        qn, kn, vf, cum_q, cum_k, o, lse, do,
        (_col_slabs(dwoa), _col_slabs(dwob), dwo.astype(BF16).reshape(N_DEV, D // N_DEV, D)), (everyone,) * 3)
    pad_f = lambda a: jnp.pad(a.T, ((0, 0), (0, NF - H)))
    dub, duf, dqg, dkg, dfb = _fox_pre_bwd(ub, uf, q_g, k_g, f_b, dqn, dkn, dvf, dgate_b,
                                           pad_f(dcq[:, :, 0]), pad_f(dck.reshape(H, -1)))
    spill = EARLY_FROM * COLS_PER_DEV - NA
    early, dwt_b_head = _proj_wgrad_early(h, dub, dug, duf, -(-spill // 8) * 8)
    dy, dr_p, dk_p, dv_p, dgg, dlnw, dlnb, drk, handed = _rwkv_post_bwd(y, r, k2, v, gg, post_params, dya, early,
                                                                          EARLY_FROM)
    early = _chip_sums(early, handed, EARLY_FROM, "chip_sums_early")
    (dr_s, dlw, dcl, dk_s, dv_s, dav, dbv), (recv_early,) = _wkv_bwd(
        (r, lw, cl, k2, v, av, bv), ckpt, pinv, dy, (early,), ((EARLY_FROM, N_DEV, "chips"),))
    pre_out = _rwkv_pre_bwd(ua, pre_params, (dr_s, dr_p, dlw, dcl, dk_s, dk_p, dv_s, dv_p, dav, dbv, dgg))
    dua, dpre = pre_out[0], pre_out[1:]
    late = _proj_wgrad_late(h, dua, dwt_b_head)

    flat = lambda a: a.reshape(1, -1)
    small = {
        "final_norm_g": dfg, "w0": dpre[7], "a0": dpre[9], "k_k": dpre[10], "k_a": dpre[11], "r_k": drk, "lnx_w": dlnw,
        "lnx_b": dlnb, "q_norm_g": dqg, "k_norm_g": dkg, "f_bias": dfb[:, :H],
        "shift_mu": jnp.concatenate([flat(dpre[0]), flat(dpre[1]), flat(dpre[2]), dpre[4], dpre[5], flat(dpre[3])], axis=1),
    }
    late = _chip_sums(late, _pair_swap(late, 0, "pair_swap_late"), 0, "chip_sums_late")
    by_head = lambda a: jnp.moveaxis(a.reshape(RANK, H, N), 1, 0)
    loras = jnp.stack([by_head(dpre[6]), by_head(dpre[8])], axis=1).astype(BF16)
    dx, dng, (recv_late, recv_lora, recv_small) = _proj_xgrad(
        x, p["norm_g"], dx2, (dua, dub, dug, duf), (w["in_a"], w["in_b"], w["in_g"], w["in_f"]),
        (late, loras, _pack_small(small, loss)), ((0, EARLY_FROM, "chips"), everyone, everyone))
    return dx, dng, (recv_early, recv_late), (recv_woa, recv_wob, recv_wo, recv_lora), recv_small


def _position():
    return lax.axis_index("x"), lax.axis_index("y"), lax.axis_index("c")


def _hbm_specs(n):
    return [pl.BlockSpec(memory_space=pl.ANY)] * n


BIG_GATHER_COPIES = 13
GATHER_ROW_CUT = 400


def _all_gather(big, blocks, name):
    n = len(blocks)

    def body(*refs):
        big_ref, x_refs = refs[0], refs[1:1 + n]
        big_out, out_refs = refs[1 + n], refs[2 + n:2 + 2 * n]
        send_sems, recv_sems, local_sems = refs[2 + 2 * n:]
        x, y, c = _position()
        me, sibling = (x, y, c), (x, y, 1 - c)
        chips = [(1 - x, y), (x, 1 - y), (1 - x, 1 - y)]
        x_nbr, y_nbr, diag = chips
        rows = big_ref.shape[0]
        cut = GATHER_ROW_CUT

        def part(ref, h):
            return ref if h is None else ref.at[pl.ds(0, cut)] if h == 0 else ref.at[pl.ds(cut, rows - cut)]

        def landed(chip, core, h):
            return part(big_out.at[4 * chip[0] + 2 * chip[1] + core], h)

        def big_copy(k, src, dst, to):
            return pltpu.make_async_remote_copy(src_ref=src, dst_ref=dst, send_sem=send_sems.at[7 * n + k],
                                                recv_sem=recv_sems.at[7 * n + k], device_id=to, device_id_type=MESH)

        def arrival(k, chip, core, h):
            dst = landed(chip, core, h)
            return big_copy(k, dst, dst, me)

        def pass_on(k, chip, h, to):
            src = landed(chip, c, h)
            return big_copy(k, src, src, to)

        big_mine = pltpu.make_async_copy(big_ref, landed((x, y), c, None), local_sems.at[n])
        big_mine.start()
        here = (x, y)
        big_sent = [big_copy(0, big_ref, landed(here, c, None), sibling),
                    big_copy(1, part(big_ref, 0), landed(here, c, 0), (*x_nbr, c)),
                    big_copy(2, part(big_ref, 1), landed(here, c, 1), (*y_nbr, c)),
                    big_copy(3, part(big_ref, 1), landed(here, c, 1), (*x_nbr, c)),
                    big_copy(4, part(big_ref, 0), landed(here, c, 0), (*y_nbr, c))]
        for cp in big_sent:
            cp.start()

        def copy(a, k, blk, to, own=False):
            dst = out_refs[a].at[4 * blk[0] + 2 * blk[1] + blk[2]]
            return pltpu.make_async_remote_copy(
                src_ref=x_refs[a] if own else dst, dst_ref=dst, send_sem=send_sems.at[7 * a + k],
                recv_sem=recv_sems.at[7 * a + k], device_id=to, device_id_type=MESH)

        mine = [pltpu.make_async_copy(x_refs[a], out_refs[a].at[4 * x + 2 * y + c], local_sems.at[a]) for a in range(n)]
        for cp in mine:
            cp.start()
        first = []
        for a in range(n):
            first.append(copy(a, 0, me, sibling, own=True))
            first += [copy(a, 1 + j, me, (*chip, c), own=True) for j, chip in enumerate(chips)]
        for cp in first:
            cp.start()

        big_steps = [(1, x_nbr, 0, (*y_nbr, c), 5, 7), (2, y_nbr, 1, (*x_nbr, c), 6, 8), (3, x_nbr, 1, None, None, 9),
                     (4, y_nbr, 0, None, None, 10), (5, diag, 0, None, None, 11), (6, diag, 1, None, None, 12)]
        for k, chip, h, onward, k_onward, k_sibling in big_steps:
            arrival(k, chip, c, h).wait_recv()
            if onward is not None:
                big_sent.append(pass_on(k_onward, chip, h, onward))
                big_sent[-1].start()
            big_sent.append(pass_on(k_sibling, chip, h, sibling))
            big_sent[-1].start()

        passed = []
        for j, chip in enumerate(chips):
            for a in range(n):
                copy(a, 1 + j, (*chip, c), me).wait_recv()
                passed.append(copy(a, 4 + j, (*chip, c), sibling))
                passed[-1].start()
        for a in range(n):
            copy(a, 0, sibling, me).wait_recv()
        for j, chip in enumerate(chips):
            for a in range(n):
                copy(a, 4 + j, (*chip, 1 - c), me).wait_recv()
        arrival(0, here, 1 - c, None).wait_recv()
        for k, chip, h, _, _, k_sibling in big_steps:
            arrival(k_sibling, chip, 1 - c, h).wait_recv()
        for cp in first + passed + big_sent:
            cp.wait_send()
        for cp in mine + [big_mine]:
            cp.wait()

    everything = [big] + list(blocks)
    return pl.pallas_call(
        body, name=name, out_shape=[jax.ShapeDtypeStruct((N_DEV,) + b.shape, b.dtype) for b in everything],
        in_specs=_hbm_specs(n + 1), out_specs=_hbm_specs(n + 1),
        scratch_shapes=[pltpu.SemaphoreType.DMA((7 * n + BIG_GATHER_COPIES,)),
                        pltpu.SemaphoreType.DMA((7 * n + BIG_GATHER_COPIES,)), pltpu.SemaphoreType.DMA((n + 1,))],
    )(*everything)


def _received_shapes(slabs, owners):
    return [jax.ShapeDtypeStruct((N_DEV // 2 if len(o) == 3 else N_DEV,) + s.shape[1:], s.dtype)
            for s, o in zip(slabs, owners)]


def _pair_swap_scratch(n):
    return [pltpu.SemaphoreType.DMA((n,)), pltpu.SemaphoreType.DMA((n,))]


def _pair_swap_ops(s_ref, p_ref, lo, sems):
    send_sems, recv_sems = sems
    n = s_ref.shape[0]

    def run(sending):
        x, y, c = _position()
        for side in (0, 1):
            mine = [pltpu.make_async_remote_copy(src_ref=s_ref.at[i], dst_ref=p_ref.at[i], send_sem=send_sems.at[i],
                                                 recv_sem=recv_sems.at[i], device_id=(x, y, 1 - c), device_id_type=MESH)
                    for i in range(n) if (lo + i) % 2 == side]

            @pl.when(c != side)
            def _():
                for cp in mine:
                    cp.start() if sending else cp.wait_send()

            if not sending:
                @pl.when(c == side)
                def _():
                    for cp in mine:
                        cp.wait_recv()

    return functools.partial(run, True), functools.partial(run, False)


def _pair_swap(slabs, lo, name):
    n = slabs.shape[0]

    def body(s_ref, p_ref, *sems):
        start, wait = _pair_swap_ops(s_ref, p_ref, lo, sems)
        start()
        wait()

    return pl.pallas_call(
        body, name=name, out_shape=jax.ShapeDtypeStruct(slabs.shape, slabs.dtype),
        in_specs=_hbm_specs(1), out_specs=_hbm_specs(1)[0], scratch_shapes=_pair_swap_scratch(n))(slabs)


def _chip_sums(slabs, swapped, lo, name):
    n, rows, cols = slabs.shape
    tile = W_IN_COL_TILE

    def body(s_ref, p_ref, o_ref):
        c = lax.axis_index("c")
        for i in range(n):
            @pl.when(c == (lo + i) % 2)
            def _(i=i):
                o_ref[i] = (s_ref[i].astype(F32) + p_ref[i].astype(F32)).astype(BF16)

    blk = pl.BlockSpec((n, rows, tile), lambda j: (0, 0, j))
    return pl.pallas_call(
        body, name=name, grid=(cols // tile,), in_specs=[blk, blk], out_specs=blk,
        out_shape=jax.ShapeDtypeStruct(slabs.shape, BF16), compiler_params=_params("arbitrary"))(slabs, swapped)


def _exchange_scratch(n):
    return [pltpu.SemaphoreType.DMA((7 * n,)), pltpu.SemaphoreType.DMA((7 * n,)), pltpu.SemaphoreType.DMA((n,))]


def _exchange_ops(src_refs, dst_refs, owners, sems):
    send_sems, recv_sems, local_sems = sems
    n = len(src_refs)

    def guarded(a, dev, fn):
        lo, hi = owners[a][:2]
        if (lo, hi) == (0, N_DEV):
            fn()
        else:
            pl.when((dev >= lo) & (dev < hi))(fn)

    def src(a, dev):
        ref = src_refs[a]
        return ref.at[0] if ref.shape[0] == 1 else ref.at[dev - owners[a][0]]

    def run(sending, waiting):
        x, y, c = _position()
        me = 4 * x + 2 * y + c
        for a in range(n):
            by_chip = len(owners[a]) == 3
            slot = (lambda qx, qy, qc: 2 * qx + qy) if by_chip else (lambda qx, qy, qc: 4 * qx + 2 * qy + qc)
            mine = slot(x, y, c)
            local = lambda a=a, mine=mine: pltpu.make_async_copy(src(a, me), dst_refs[a].at[mine], local_sems.at[a])
            if sending:
                guarded(a, me, lambda local=local: local().start())
            for m in range(2, N_DEV, 2) if by_chip else range(1, N_DEV):
                px, py, pc = x ^ (m >> 2), y ^ ((m >> 1) & 1), c ^ (m & 1)
                peer = 4 * px + 2 * py + pc
                theirs = slot(px, py, pc)
                sem = dict(send_sem=send_sems.at[7 * a + m - 1], recv_sem=recv_sems.at[7 * a + m - 1],
                           device_id=(px, py, pc), device_id_type=MESH)
                send = lambda a=a, peer=peer, sem=sem, mine=mine: pltpu.make_async_remote_copy(
                    src_ref=src(a, peer), dst_ref=dst_refs[a].at[mine], **sem)
                recv = lambda a=a, sem=sem, theirs=theirs: pltpu.make_async_remote_copy(
                    src_ref=src(a, me), dst_ref=dst_refs[a].at[theirs], **sem)
                if sending:
                    guarded(a, peer, lambda send=send: send().start())
                if waiting:
                    guarded(a, me, lambda recv=recv: recv().wait_recv())
                    guarded(a, peer, lambda send=send: send().wait_send())
            if waiting:
                guarded(a, me, lambda local=local: local().wait())

    return functools.partial(run, True, False), functools.partial(run, False, True)


def _sum_slabs(r_ref):
    g = r_ref[0].astype(F32)
    for k in range(1, r_ref.shape[0]):
        g = g + r_ref[k].astype(F32)
    return g


def _adamw(g, w, m, v):
    m_new = ADAM_B1 * m + (1.0 - ADAM_B1) * g
    v_new = ADAM_B2 * v + (1.0 - ADAM_B2) * (g * g)
    m_hat = m_new / (1.0 - ADAM_B1 ** ADAM_STEP)
    v_hat = v_new / (1.0 - ADAM_B2 ** ADAM_STEP)
    return g, -ADAM_LR * (m_hat / (jnp.sqrt(v_hat) + ADAM_EPS) + ADAM_WD * w), m_new, v_new


def _adamw_w_in(recv_early, recv_late, w, m, v, slabs, owners):
    rows, cols = w.shape
    tile = W_IN_COL_TILE
    nx = len(slabs)

    def body(early_ref, late_ref, w_ref, m_ref, v_ref, *refs):
        src_refs, o_refs, dst_refs = refs[:nx], refs[nx:nx + 4], refs[nx + 4:2 * nx + 4]
        start, wait = _exchange_ops(src_refs, dst_refs, owners, refs[2 * nx + 4:])
        x, y, c = _position()
        early_owner = 4 * x + 2 * y + c >= EARLY_FROM

        @pl.when(pl.program_id(0) == 0)
        def _():
            start()

        def update(g):
            for o_ref, val in zip(o_refs, _adamw(g, w_ref[...], m_ref[...], v_ref[...])):
                o_ref[...] = val

        pl.when(early_owner)(lambda: update(_sum_slabs(early_ref)))
        pl.when(jnp.logical_not(early_owner))(lambda: update(_sum_slabs(late_ref)))

        @pl.when(pl.program_id(0) == cols // tile - 1)
        def _():
            wait()

    blk = pl.BlockSpec((rows, tile), lambda i: (0, i))
    slots = lambda r: pl.BlockSpec((r.shape[0], rows, tile), lambda i: (0, 0, i))
    out = pl.pallas_call(
        body, name="adamw_w_in", grid=(cols // tile,),
        in_specs=[slots(recv_early), slots(recv_late), blk, blk, blk] + _hbm_specs(nx),
        out_specs=[blk] * 4 + _hbm_specs(nx),
        out_shape=[jax.ShapeDtypeStruct((rows, cols), F32)] * 4 + _received_shapes(slabs, owners),
        scratch_shapes=_exchange_scratch(nx),
        compiler_params=_params("arbitrary"))(recv_early, recv_late, w, m, v, *slabs)
    return out[:4], out[4:]


def _adamw_misc(recvs, recv_small, recv_norm, params):
    names = list(params)
    flat = [a for n in names for a in params[n]]

    def body(woa_ref, wob_ref, wo_ref, lora_ref, small_ref, norm_ref, *refs):
        p_refs, o_refs = refs[:len(flat)], refs[len(flat):]
        g_small = _sum_slabs(small_ref)
        g_lora = _sum_slabs(lora_ref)
        grads = {"w_out_a": _sum_slabs(woa_ref), "w_out_b": _sum_slabs(wob_ref), "w_out": _sum_slabs(wo_ref),
                 "w_lora_up": g_lora[0], "a_lora_up": g_lora[1], "norm_g": _sum_slabs(norm_ref)}
        for n, (off, size) in SMALL_SLOTS.items():
            grads[n] = g_small[:, off:off + size]
        for i, n in enumerate(names):
            w_ref, m_ref, v_ref = p_refs[3 * i:3 * i + 3]
            for o_ref, val in zip(o_refs[4 * i:4 * i + 4], _adamw(grads[n], w_ref[...], m_ref[...], v_ref[...])):
                o_ref[...] = val
        o_refs[-1][...] = g_small[:, LOSS_SLOT:LOSS_SLOT + 1]

    out = pl.pallas_call(
        body, name="adamw_misc",
        out_shape=[jax.ShapeDtypeStruct(params[n][0].shape, F32) for n in names for _ in range(4)]
        + [jax.ShapeDtypeStruct((1, 1), F32)],
        compiler_params=_params())(*recvs, recv_small, recv_norm, *flat)
    return {n: out[4 * i:4 * i + 4] for i, n in enumerate(names)}, out[-1]


_WT_SEGMENTS = ((0, NA), (NA, NB), (NA + NB + H, NG), (NA + NB, H))


def _split_wt(gathered):
    tile = W_IN_COL_TILE

    def body(g_ref, *o_refs):
        full = jnp.concatenate([g_ref[j] for j in range(N_DEV)], axis=0)
        for o_ref, (row, n) in zip(o_refs, _WT_SEGMENTS):
            seg = full[row:row + n]
            if n < o_ref.shape[0]:
                seg = jnp.concatenate([seg, jnp.zeros((o_ref.shape[0] - n, tile), BF16)], axis=0)
            o_ref[...] = seg

    sizes = (NA, NB, NG, NF)
    return pl.pallas_call(
        body, name="split_wt", grid=(D // tile,),
        in_specs=[pl.BlockSpec((N_DEV, COLS_PER_DEV, tile), lambda i: (0, 0, i))],
        out_specs=[pl.BlockSpec((n, tile), lambda i: (0, i)) for n in sizes],
        out_shape=[jax.ShapeDtypeStruct((n, D), BF16) for n in sizes],
        compiler_params=_params("arbitrary"))(gathered)


def _by_cols(a):
    return jnp.moveaxis(a, 0, 1).reshape(a.shape[1], -1)


def _col_slabs(a):
    return jnp.moveaxis(a.reshape(a.shape[0], N_DEV, -1), 1, 0).astype(BF16)


def _pack_small(grads, loss):
    pieces, at = [], 0
    for n, (off, size) in list(SMALL_SLOTS.items()) + [("loss", (LOSS_SLOT, 1))]:
        pieces += [jnp.zeros((off - at,), F32), (loss if n == "loss" else grads[n]).reshape(-1)]
        at = off + size
    return jnp.concatenate(pieces + [jnp.zeros((SMALL_LEN - at,), F32)]).reshape(1, 1, SMALL_LEN)


def _gather_weights(t):
    cast = lambda a: a.astype(BF16)
    loras = jnp.stack([t["w_lora_up"][0], t["a_lora_up"][0]])
    wt, woa, wob, wo, lora = _all_gather(
        cast(t["w_in"][0].T), [cast(t["w_out_a"][0]), cast(t["w_out_b"][0]), cast(t["w_out"][0]), cast(loras)],
        "weight_gather")
    in_a, in_b, in_g, in_f = _split_wt(wt)
    return {"in_a": in_a, "in_b": in_b, "in_g": in_g, "in_f": in_f, "w_out_a": _by_cols(woa), "w_out_b": _by_cols(wob),
            "w_out": wo.reshape(D, D), "w_lora_up": lora[:, 0], "a_lora_up": lora[:, 1]}


def kernel(x, norm_g, w_in, shift_mu, w_lora_up, w0, a_lora_up, a0, k_k, k_a, r_k, lnx_w, lnx_b, f_bias, q_norm_g, k_norm_g, w_out_a, w_out_b, w_out, final_norm_g, loss_target, m_norm_g, m_w_in, m_shift_mu, m_w_lora_up, m_w0, m_a_lora_up, m_a0, m_k_k, m_k_a, m_r_k, m_lnx_w, m_lnx_b, m_f_bias, m_q_norm_g, m_k_norm_g, m_w_out_a, m_w_out_b, m_w_out, m_final_norm_g, v_norm_g, v_w_in, v_shift_mu, v_w_lora_up, v_w0, v_a_lora_up, v_a0, v_k_k, v_k_a, v_r_k, v_lnx_w, v_lnx_b, v_f_bias, v_q_norm_g, v_k_norm_g, v_w_out_a, v_w_out_b, v_w_out, v_final_norm_g):
    names = ("norm_g", "w_in", "shift_mu", "w_lora_up", "w0", "a_lora_up", "a0", "k_k", "k_a", "r_k", "lnx_w", "lnx_b",
             "f_bias", "q_norm_g", "k_norm_g", "w_out_a", "w_out_b", "w_out", "final_norm_g")
    weights = dict(zip(names, (norm_g, w_in, shift_mu, w_lora_up, w0, a_lora_up, a0, k_k, k_a, r_k, lnx_w, lnx_b,
                               f_bias, q_norm_g, k_norm_g, w_out_a, w_out_b, w_out, final_norm_g)))
    m_in = dict(zip(names, (m_norm_g, m_w_in, m_shift_mu, m_w_lora_up, m_w0, m_a_lora_up, m_a0, m_k_k, m_k_a, m_r_k,
                            m_lnx_w, m_lnx_b, m_f_bias, m_q_norm_g, m_k_norm_g, m_w_out_a, m_w_out_b, m_w_out,
                            m_final_norm_g)))
    v_in = dict(zip(names, (v_norm_g, v_w_in, v_shift_mu, v_w_lora_up, v_w0, v_a_lora_up, v_a0, v_k_k, v_k_a, v_r_k,
                            v_lnx_w, v_lnx_b, v_f_bias, v_q_norm_g, v_k_norm_g, v_w_out_a, v_w_out_b, v_w_out,
                            v_final_norm_g)))

    matrices = ("w_out_a", "w_out_b", "w_out", "w_lora_up", "a_lora_up")
    as_2d = lambda n, a: a[0] if n in matrices else a.reshape(1, -1)

    full = _gather_weights(weights)
    dx, dng, recv_wt, recvs, recv_small = _local_step(
        x[0], loss_target[0], full, {n: as_2d(n, weights[n]) for n in ("norm_g",) + tuple(SMALL_SLOTS)})

    res, (recv_norm,) = _adamw_w_in(*recv_wt, w_in[0].T, m_w_in[0].T, v_w_in[0].T, (dng[None],), ((0, N_DEV),))
    outs = {"w_in": [r.T[None] for r in res]}
    misc = [n for n in names if n != "w_in"]
    res, loss_sum = _adamw_misc(recvs, recv_small, recv_norm,
                                {n: tuple(as_2d(n, t[n]) for t in (weights, m_in, v_in)) for n in misc})
    for n in misc:
        outs[n] = [r.reshape(weights[n].shape) for r in res[n]]
    return (loss_sum.reshape(()), dx[None], *[outs[n][i] for i in range(4) for n in names])
```

```python
import functools
import math

import jax
import jax.numpy as jnp
from jax import lax
from jax.experimental import pallas as pl
from jax.experimental.pallas import tpu as pltpu

F32 = jnp.float32
BF16 = jnp.bfloat16
HI = lax.Precision.HIGHEST
MESH = pl.DeviceIdType.MESH

N_DEV = 8
D = 1024
H = 8
N = 64
DA = H * N
RANK = 64
NA = 4 * DA + 2 * RANK
NB = 4 * DA
NG = 2 * D
NF = 128
IN_COLS = NA + NB + H + NG
COLS_PER_DEV = IN_COLS // N_DEV
RMS_EPS = 1e-6
LNX_EPS = 64e-5
ATT_SCALE = N ** -0.5

ADAM_LR = 0.001
ADAM_B1 = 0.9
ADAM_B2 = 0.999
ADAM_EPS = 1e-08
ADAM_WD = 0.01
ADAM_STEP = 10

LANES = 128
WKV_CHUNK = 64
TOK_TILE = 256
HEAD_TILE = 256
XGRAD_TILE = 128
ATT_TILE = 256
ATT_GROUPS = 8
VMEM_LIMIT = 56 * 1024 * 1024


def _lane_tile_slots(sizes):
    slots, at = {}, 0
    for name, size in sizes:
        slots[name] = (at, size)
        at += -(-size // LANES) * LANES
    return slots, at


SMALL_SLOTS, LOSS_SLOT = _lane_tile_slots((
    ("final_norm_g", D), ("shift_mu", NA), ("w0", DA), ("a0", DA), ("k_k", DA), ("k_a", DA), ("r_k", DA), ("lnx_w", DA),
    ("lnx_b", DA), ("q_norm_g", N), ("k_norm_g", N), ("f_bias", H)))
SMALL_LEN = LOSS_SLOT + LANES
W_IN_COL_TILE = 512
EARLY_FROM = -(-NA // COLS_PER_DEV)


def _params(*sem):
    return pltpu.CompilerParams(dimension_semantics=sem or None, vmem_limit_bytes=VMEM_LIMIT)


def _bdot(a, b):
    return jnp.dot(a.astype(BF16), b.astype(BF16), preferred_element_type=F32)


def _bdot_nt(a, b):
    return lax.dot_general(a.astype(BF16), b.astype(BF16), (((1,), (1,)), ((), ())), preferred_element_type=F32)


def _bdot_tn(a, b):
    return lax.dot_general(a.astype(BF16), b.astype(BF16), (((0,), (0,)), ((), ())), preferred_element_type=F32)


def _sigmoid(x):
    return 1.0 / (1.0 + jnp.exp(-x))


def _softplus(x):
    return jnp.maximum(x, 0.0) + jnp.log(1.0 + jnp.exp(-jnp.abs(x)))


def _heads(ref, col0):
    return jnp.stack([ref[:, col0 + N * h:col0 + N * (h + 1)] for h in range(H)])


def _store_heads(ref, col0, val):
    for h in range(H):
        ref[:, col0 + N * h:col0 + N * (h + 1)] = val[h]


def _lerp(c, s, mu):
    return c + (s - c) * mu


def _head_sums(x):
    low = lax.broadcasted_iota(jnp.int32, (x.shape[0], LANES), 1) < N
    out = []
    for p in range(x.shape[1] // LANES):
        pair = x[:, LANES * p:LANES * (p + 1)]
        first = jnp.sum(jnp.where(low, pair, 0.0), axis=-1, keepdims=True)
        second = jnp.sum(jnp.where(low, 0.0, pair), axis=-1, keepdims=True)
        out.append(jnp.where(low, first, second))
    return jnp.concatenate(out, axis=-1)


def _to_heads(x):
    return [x[:, N * h:N * (h + 1)] for h in range(H)]


def _from_heads(ref):
    return jnp.concatenate([ref[h] for h in range(H)], axis=-1)


def _rwkv_pre(rc, rs, kc, ks, vc, vs, gc, gs, wdc, wds, adc, ads,
              mu_r, mu_k, mu_v, mu_g, mu_wd, mu_ad, w_up, w0, a_up, a0, k_k, k_a):
    r = _lerp(rc, rs, mu_r)
    k = _lerp(kc, ks, mu_k)
    v = _lerp(vc, vs, mu_v)
    g = _lerp(gc, gs, mu_g)
    wd = _lerp(wdc, wds, mu_wd)
    ad = _lerp(adc, ads, mu_ad)
    t = wd.shape[0]
    w_raw = -_softplus(-(w0 + _bdot(jnp.tanh(wd), w_up))) - 0.5
    lw = -jnp.exp(w_raw)
    row = lax.broadcasted_iota(jnp.int32, (t, t), 0)
    col = lax.broadcasted_iota(jnp.int32, (t, t), 1)
    same_chunk = ((row >= col) & (row // WKV_CHUNK == col // WKV_CHUNK)).astype(F32)
    cl = jnp.dot(same_chunk, lw, precision=HI, preferred_element_type=F32)
    alr = _sigmoid(a0 + _bdot(ad, a_up))
    kk = k * k_k
    kk = kk / jnp.maximum(jnp.sqrt(_head_sums(kk * kk)), 1e-12)
    k2 = k * (1.0 + (alr - 1.0) * k_a)
    return r, lw, cl, k2, v, -kk, kk * alr, g


_MM_DIMS = {"nn": (((2,), (1,)), ((0,), (0,))), "nt": (((2,), (2,)), ((0,), (0,))), "tn": (((1,), (1,)), ((0,), (0,)))}


def _dot1(a, b, kind):
    return lax.dot_general(a.astype(BF16), b.astype(BF16), dimension_numbers=_MM_DIMS[kind], preferred_element_type=F32)


@functools.partial(jax.custom_vjp, nondiff_argnums=(2,))
def _mm(a, b, kind):
    return _dot1(a, b, kind)


def _mm_fwd(a, b, kind):
    return _dot1(a, b, kind), (a, b)


def _mm_bwd(kind, res, ct):
    a, b = res
    if kind == "nn":
        return _dot1(ct, b, "nt"), _dot1(a, ct, "tn")
    if kind == "nt":
        return _dot1(ct, b, "nn"), _dot1(ct, a, "tn")
    return _dot1(b, ct, "nt"), _dot1(a, ct, "nn")


_mm.defvjp(_mm_fwd, _mm_bwd)


def _chunk_masks(c):
    row = lax.broadcasted_iota(jnp.int32, (c, c), 0)
    col = lax.broadcasted_iota(jnp.int32, (c, c), 1)
    return (row >= col)[None], (row > col)[None], (row == col).astype(F32)[None]


def _wkv_aab(lw, cl, a, b):
    _, strict, _ = _chunk_masks(a.shape[1])
    return jnp.where(strict, _mm(a * jnp.exp(cl - lw), b * jnp.exp(-cl), "nt"), 0.0)


def _tri_inverse(x):
    c = x.shape[1]
    p = _chunk_masks(c)[2] + x
    for _ in range(int(math.log2(c)) - 1):
        x = _dot1(x, x, "nn")
        p = p + _dot1(p, x, "nn")
    return p


def _wkv_apply(s0, r, lw, cl, k, v, a, b, p):
    c = r.shape[1]
    incl, strict, _ = _chunk_masks(c)
    gi = jnp.exp(-cl)
    left = jnp.concatenate([a * jnp.exp(cl - lw), r * jnp.exp(cl)], axis=1)
    right = jnp.concatenate([b * gi, k * gi], axis=1)
    m = _mm(left, right, "nt")
    z0 = _mm(left, s0, "nt")
    a_ak = jnp.where(strict, m[:, :c, c:], 0.0)
    row = lax.broadcasted_iota(jnp.int32, (c, 2 * c), 0)
    col = lax.broadcasted_iota(jnp.int32, (c, 2 * c), 1)
    a_r = jnp.where((row >= col % c)[None], m[:, c:, :], 0.0)
    sa = _mm(p, z0[:, :c] + _mm(a_ak, v, "nn"), "nn")
    sa_v = jnp.concatenate([sa, v], axis=1)
    y = z0[:, c:] + _mm(a_r, sa_v, "nn")
    s1 = (s0 + _mm(sa_v, right, "tn")) * jnp.exp(cl[:, c - 1:c, :])
    return y, s1


def _rwkv_post(y, r, k2, v, g, lnx_w, lnx_b, r_k):
    yc = y - _head_sums(y) * (1.0 / N)
    var = _head_sums(yc * yc) * (1.0 / N)
    yn = yc * lax.rsqrt(var + LNX_EPS) * lnx_w + lnx_b
    bonus = _head_sums(r * k2 * r_k) * v
    return (yn + bonus) * (g * _sigmoid(g))


def _fox_pre(q, k, f, q_g, k_g, f_b):
    qn = q * lax.rsqrt(_head_sums(q * q) * (1.0 / N) + RMS_EPS) * q_g
    kn = k * lax.rsqrt(_head_sums(k * k) * (1.0 / N) + RMS_EPS) * k_g
    x = f + f_b
    return qn, kn, jnp.minimum(x, 0.0) - jnp.log(1.0 + jnp.exp(-jnp.abs(x)))


def _norm_proj(x, g, wts):
    s = x.shape[0]
    k = len(wts)

    def body(x_ref, g_ref, *refs):
        w_refs, h_ref, o_refs = refs[:k], refs[k], refs[k + 1:]
        xv = x_ref[...]
        h = (xv * lax.rsqrt(jnp.mean(xv * xv, axis=-1, keepdims=True) + RMS_EPS) * g_ref[...]).astype(BF16)
        h_ref[...] = h
        for w_ref, o_ref in zip(w_refs, o_refs):
            o_ref[...] = _bdot_nt(h, w_ref[...])

    tok = lambda n: pl.BlockSpec((TOK_TILE, n), lambda i: (i, 0))
    out = pl.pallas_call(
        body, name="norm_proj", grid=(s // TOK_TILE,),
        in_specs=[tok(D), pl.BlockSpec((1, D), lambda i: (0, 0))] + [pl.BlockSpec(w.shape, lambda i: (0, 0)) for w in wts],
        out_specs=[tok(D)] + [tok(w.shape[0]) for w in wts],
        out_shape=[jax.ShapeDtypeStruct((s, D), BF16)] + [jax.ShapeDtypeStruct((s, w.shape[0]), F32) for w in wts],
        compiler_params=_params("arbitrary"))(x, g, *wts)
    return out[0], out[1:]


def _proj_wgrad_early(h, dub, dug, duf, head_rows):
    s = dub.shape[0]
    steps = s // TOK_TILE
    seg_rows = (_WT_SEGMENTS[1], _WT_SEGMENTS[2], _WT_SEGMENTS[3])

    def body(h_ref, b_ref, g_ref, f_ref, o_ref, head_ref, *accs):
        @pl.when(pl.program_id(0) == 0)
        def _():
            for acc in accs:
                acc[...] = jnp.zeros_like(acc)

        h = h_ref[...]
        for acc, du_ref in zip(accs, (b_ref, g_ref, f_ref)):
            acc[...] += _bdot_tn(du_ref[...], h)

        @pl.when(pl.program_id(0) == steps - 1)
        def _():
            head_ref[...] = accs[0][:head_rows, :]
            for j in range(EARLY_FROM, N_DEV):
                lo, hi = COLS_PER_DEV * j, COLS_PER_DEV * (j + 1)
                parts = []
                for acc, (row, n) in sorted(zip(accs, seg_rows), key=lambda t: t[1][0]):
                    first, last = max(lo, row), min(hi, row + n)
                    if first < last:
                        parts.append(acc[first - row:last - row, :])
                o_ref[j - EARLY_FROM] = (parts[0] if len(parts) == 1 else jnp.concatenate(parts, axis=0)).astype(BF16)

    tok = lambda n: pl.BlockSpec((TOK_TILE, n), lambda i: (i, 0))
    n_early = N_DEV - EARLY_FROM
    return pl.pallas_call(
        body, name="wgrad_bgf", grid=(steps,), in_specs=[tok(D), tok(NB), tok(NG), tok(NF)],
        out_specs=[pl.BlockSpec((n_early, COLS_PER_DEV, D), lambda i: (0, 0, 0)),
                   pl.BlockSpec((head_rows, D), lambda i: (0, 0))],
        out_shape=[jax.ShapeDtypeStruct((n_early, COLS_PER_DEV, D), BF16), jax.ShapeDtypeStruct((head_rows, D), F32)],
        scratch_shapes=[pltpu.VMEM((n, D), F32) for n in (NB, NG, NF)],
        compiler_params=_params("arbitrary"))(h, dub, dug, duf)


def _proj_wgrad_late(h, dua, dwt_b_head):
    s = dua.shape[0]
    steps = s // TOK_TILE

    def body(h_ref, du_ref, b_ref, o_ref, acc):
        @pl.when(pl.program_id(0) == 0)
        def _():
            acc[...] = jnp.zeros_like(acc)

        acc[...] += _bdot_tn(du_ref[...], h_ref[...])

        @pl.when(pl.program_id(0) == steps - 1)
        def _():
            for j in range(EARLY_FROM):
                lo, hi = COLS_PER_DEV * j, COLS_PER_DEV * (j + 1)
                parts = [acc[lo:min(hi, NA), :]] + ([b_ref[:hi - NA, :]] if hi > NA else [])
                o_ref[j] = (parts[0] if len(parts) == 1 else jnp.concatenate(parts, axis=0)).astype(BF16)

    return pl.pallas_call(
        body, name="wgrad_a", grid=(steps,),
        in_specs=[pl.BlockSpec((TOK_TILE, D), lambda i: (i, 0)), pl.BlockSpec((TOK_TILE, NA), lambda i: (i, 0)),
                  pl.BlockSpec(dwt_b_head.shape, lambda i: (0, 0))],
        out_specs=pl.BlockSpec((EARLY_FROM, COLS_PER_DEV, D), lambda i: (0, 0, 0)),
        out_shape=jax.ShapeDtypeStruct((EARLY_FROM, COLS_PER_DEV, D), BF16),
        scratch_shapes=[pltpu.VMEM((NA, D), F32)], compiler_params=_params("arbitrary"))(h, dua, dwt_b_head)


def _proj_xgrad(x, g, dx2, dus, ws, slabs, owners):
    s = x.shape[0]
    tile = XGRAD_TILE
    k = len(dus)
    nx = len(slabs)
    n_in = 3 + 2 * k + nx

    def body(*refs):
        x_ref, g_ref, dx2_ref = refs[:3]
        du_refs, w_refs = refs[3:3 + k], refs[3 + k:3 + 2 * k]
        src_refs = refs[3 + 2 * k:3 + 2 * k + nx]
        dx_ref, dg_ref = refs[n_in:n_in + 2]
        dst_refs = refs[n_in + 2:n_in + 2 + nx]
        start, wait = _exchange_ops(src_refs, dst_refs, owners, refs[n_in + 2 + nx:])

        @pl.when(pl.program_id(0) == 0)
        def _():
            dg_ref[...] = jnp.zeros_like(dg_ref)
            start()

        dh = _bdot(du_refs[0][...], w_refs[0][...])
        for du_ref, w_ref in zip(du_refs[1:], w_refs[1:]):
            dh += _bdot(du_ref[...], w_ref[...])
        xv = x_ref[...]
        rs = lax.rsqrt(jnp.mean(xv * xv, axis=-1, keepdims=True) + RMS_EPS)
        xn = xv * rs
        dg_ref[...] += jnp.sum(dh * xn, axis=0, keepdims=True)
        dxn = dh * g_ref[...]
        dx_ref[...] = rs * (dxn - xn * jnp.mean(dxn * xn, axis=-1, keepdims=True)) + dx2_ref[...]

        @pl.when(pl.program_id(0) == s // tile - 1)
        def _():
            wait()

    tok = lambda n: pl.BlockSpec((tile, n), lambda i: (i, 0))
    fixed = lambda a: pl.BlockSpec(a.shape, lambda i: (0,) * a.ndim)
    out = pl.pallas_call(
        body, name="proj_xgrad", grid=(s // tile,),
        in_specs=([tok(D), fixed(g), tok(D)] + [tok(du.shape[1]) for du in dus] + [fixed(w) for w in ws]
                  + _hbm_specs(nx)),
        out_specs=[tok(D), pl.BlockSpec((1, D), lambda i: (0, 0))] + _hbm_specs(nx),
        out_shape=[jax.ShapeDtypeStruct((s, D), F32), jax.ShapeDtypeStruct((1, D), F32)] + _received_shapes(slabs, owners),
        scratch_shapes=_exchange_scratch(nx),
        compiler_params=_params("arbitrary"))(x, g, dx2, *dus, *ws, *slabs)
    return out[0], out[1], out[2:]


def _tail(x, target, ya, o, ub, ug, w_oa, w_ob, w_o, fg):
    s = x.shape[0]
    tile = TOK_TILE

    def body(x_ref, t_ref, ya_ref, o_ref, gb_ref, ug_ref, woa_ref, wob_ref, wo_ref, fg_ref,
             loss_ref, dfg_ref, dwo_ref, dwoa_ref, dwob_ref, dx2_ref, dya_ref, do_ref, dgb_ref, dug_ref):
        @pl.when(pl.program_id(0) == 0)
        def _():
            for r in (loss_ref, dfg_ref, dwo_ref, dwoa_ref, dwob_ref):
                r[...] = jnp.zeros_like(r)

        ya_v = ya_ref[...]
        gate_b = gb_ref[...]
        sg_b = _sigmoid(gate_b)
        silu_b = gate_b * sg_b
        o_v = jnp.concatenate([o_ref[h] for h in range(H)], axis=-1)
        yb_v = o_v * silu_b
        big_a = _bdot(ya_v, woa_ref[...])
        big_b = _bdot(yb_v, wob_ref[...])
        sa = _sigmoid(ug_ref[:, :D])
        sb = _sigmoid(ug_ref[:, D:])
        merged = sa * big_a + sb * big_b
        x2 = x_ref[...] + _bdot(merged, wo_ref[...])
        rs = lax.rsqrt(jnp.mean(x2 * x2, axis=-1, keepdims=True) + RMS_EPS)
        xn = x2 * rs
        err = xn * fg_ref[...] - t_ref[...]
        loss_ref[...] += (0.5 / D) * jnp.sum(err * err)
        dout = err * (1.0 / D)
        dfg_ref[...] += jnp.sum(dout * xn, axis=0, keepdims=True)
        dxn = dout * fg_ref[...]
        dx2 = rs * (dxn - xn * jnp.mean(dxn * xn, axis=-1, keepdims=True))
        dx2_ref[...] = dx2
        dwo_ref[...] += _bdot_tn(merged, dx2)
        dmerged = _bdot_nt(dx2, wo_ref[...])
        dbig_a = dmerged * sa
        dbig_b = dmerged * sb
        dug_ref[:, :D] = dmerged * big_a * sa * (1.0 - sa)
        dug_ref[:, D:] = dmerged * big_b * sb * (1.0 - sb)
        dwoa_ref[...] += _bdot_tn(ya_v, dbig_a)
        dwob_ref[...] += _bdot_tn(yb_v, dbig_b)
        dya_ref[...] = _bdot_nt(dbig_a, woa_ref[...])
        dyb = _bdot_nt(dbig_b, wob_ref[...])
        dgb_ref[...] = dyb * o_v * (sg_b * (1.0 + gate_b * (1.0 - sg_b)))
        _dov = dyb * silu_b
        for h in range(H):
            do_ref[h] = _dov[:, N * h:N * (h + 1)]

    tok = lambda n: pl.BlockSpec((tile, n), lambda i: (i, 0))
    hm = pl.BlockSpec((H, tile, N), lambda i: (0, i, 0))
    fixed = lambda shape: pl.BlockSpec(shape, lambda i: (0,) * len(shape))
    f32 = lambda *shape: jax.ShapeDtypeStruct(shape, F32)
    return pl.pallas_call(
        body, name="tail", grid=(s // tile,),
        in_specs=[tok(D), tok(D), tok(DA), hm, pl.BlockSpec((tile, DA), lambda i: (i, 3)), tok(NG),
                  fixed((DA, D)), fixed((DA, D)), fixed((D, D)), fixed((1, D))],
        out_specs=[fixed((1, 1)), fixed((1, D)), fixed((D, D)), fixed((DA, D)), fixed((DA, D)),
                   tok(D), tok(DA), hm, tok(DA), tok(NG)],
        out_shape=[f32(1, 1), f32(1, D), f32(D, D), f32(DA, D), f32(DA, D),
                   f32(s, D), f32(s, DA), f32(H, s, N), f32(s, DA), f32(s, NG)],
        compiler_params=_params("arbitrary"))(x, target, ya, o, ub, ug, w_oa, w_ob, w_o, fg)


def _pre_operands(ua_ref, prev_ref, first):
    cur = ua_ref[...]
    t = cur.shape[0]
    prev_row = jnp.where(first, 0.0, prev_ref[7:8, :])
    rows = lax.broadcasted_iota(jnp.int32, cur.shape, 0)
    sh = jnp.where(rows == 0, prev_row, pltpu.roll(cur, 1, axis=0))
    ops = []
    for c0, n in ((0, DA), (DA, DA), (2 * DA, DA), (3 * DA + 2 * RANK, DA), (3 * DA, RANK), (3 * DA + RANK, RANK)):
        ops += [cur[:, c0:c0 + n], sh[:, c0:c0 + n]]
    del t
    return ops


def _ua_specs(tile, order):
    blocks = tile // 8
    return [pl.BlockSpec((tile, NA), lambda i: (order(i), 0)),
            pl.BlockSpec((8, NA), lambda i: (jnp.maximum(order(i) * blocks - 1, 0), 0))]


def _rwkv_pre_fwd(ua, pre_params):
    s = ua.shape[0]
    tile = HEAD_TILE

    def body(ua_ref, prev_ref, *refs):
        p_refs, o_refs = refs[:len(pre_params)], refs[len(pre_params):]
        ops = _pre_operands(ua_ref, prev_ref, pl.program_id(0) == 0)
        outs = _rwkv_pre(*ops, *[p[...] for p in p_refs])
        for o_ref, val in zip(o_refs, outs):
            o_ref[...] = val

    tm = pl.BlockSpec((tile, DA), lambda i: (i, 0))
    return pl.pallas_call(
        body, name="rwkv_pre_fwd", grid=(s // tile,),
        in_specs=_ua_specs(tile, lambda i: i) + [pl.BlockSpec(p.shape, lambda i, nd=p.ndim: (0,) * nd) for p in pre_params],
        out_specs=[tm] * 8, out_shape=[jax.ShapeDtypeStruct((s, DA), F32)] * 8,
        compiler_params=_params("arbitrary"))(ua, ua, *pre_params)


def _rwkv_pre_bwd(ua, pre_params, cots):
    s = ua.shape[0]
    tile = HEAD_TILE
    nt = s // tile
    n_p = len(pre_params)

    def body(ua_ref, prev_ref, *refs):
        p_refs, c_refs = refs[:n_p], refs[n_p:n_p + 11]
        dua_ref = refs[n_p + 11]
        dp_refs = refs[n_p + 12:n_p + 12 + n_p]
        carry_ref = refs[-1]
        i = pl.program_id(0)

        @pl.when(i == 0)
        def _():
            carry_ref[...] = jnp.zeros_like(carry_ref)
            for r in dp_refs:
                r[...] = jnp.zeros_like(r)

        ops = _pre_operands(ua_ref, prev_ref, i == nt - 1)
        _, vjp = jax.vjp(_rwkv_pre, *ops, *[p[...] for p in p_refs])
        c = [r[...] for r in c_refs]
        grads = vjp((c[0] + c[1], c[2], c[3], c[4] + c[5], c[6] + c[7], c[8], c[9], c[10]))
        d_ops, d_par = grads[:12], grads[12:]
        for r, val in zip(dp_refs, d_par):
            r[...] += val
        d_cur = jnp.concatenate([d_ops[0], d_ops[2], d_ops[4], d_ops[8], d_ops[10], d_ops[6]], axis=-1)
        d_sh = jnp.concatenate([d_ops[1], d_ops[3], d_ops[5], d_ops[9], d_ops[11], d_ops[7]], axis=-1)
        rows = lax.broadcasted_iota(jnp.int32, d_sh.shape, 0)
        dua_ref[...] = d_cur + jnp.where(rows == tile - 1, carry_ref[...], pltpu.roll(d_sh, tile - 1, axis=0))
        carry_ref[...] = d_sh[0:1, :]

    rev = lambda i: nt - 1 - i
    tm = pl.BlockSpec((tile, DA), lambda i: (rev(i), 0))
    fixed = [pl.BlockSpec(p.shape, lambda i, nd=p.ndim: (0,) * nd) for p in pre_params]
    return pl.pallas_call(
        body, name="rwkv_pre_bwd", grid=(nt,),
        in_specs=_ua_specs(tile, rev) + fixed + [tm] * 11,
        out_specs=[pl.BlockSpec((tile, NA), lambda i: (rev(i), 0))] + fixed,
        out_shape=[jax.ShapeDtypeStruct((s, NA), F32)] + [jax.ShapeDtypeStruct(p.shape, F32) for p in pre_params],
        scratch_shapes=[pltpu.VMEM((1, NA), F32)],
        compiler_params=_params("arbitrary"))(ua, ua, *pre_params, *cots)


def _wkv_fwd(seq):
    s = seq[0].shape[0]
    nc = s // WKV_CHUNK

    def body(r_ref, lw_ref, cl_ref, k_ref, v_ref, a_ref, b_ref, y_ref, ck_ref, p_ref, state):
        @pl.when(pl.program_id(0) == 0)
        def _():
            state[...] = jnp.zeros_like(state)

        r, lw, cl, k, v, a, b = (jnp.stack(_to_heads(ref[...])) for ref in (r_ref, lw_ref, cl_ref, k_ref, v_ref, a_ref,
                                                                             b_ref))
        s0 = state[...]
        ck_ref[0] = s0
        p = _tri_inverse(_wkv_aab(lw, cl, a, b))
        p_ref[0] = p
        y, s1 = _wkv_apply(s0, r, lw, cl, k, v, a, b, p)
        y_ref[...] = jnp.concatenate([y[h] for h in range(H)], axis=-1)
        state[...] = s1

    tm = pl.BlockSpec((WKV_CHUNK, DA), lambda c: (c, 0))
    per_chunk = lambda m: pl.BlockSpec((1, H, m, m), lambda c: (c, 0, 0, 0))
    return pl.pallas_call(
        body, name="wkv_fwd", grid=(nc,), in_specs=[tm] * 7,
        out_specs=[tm, per_chunk(N), per_chunk(WKV_CHUNK)],
        out_shape=[jax.ShapeDtypeStruct((s, DA), F32), jax.ShapeDtypeStruct((nc, H, N, N), F32),
                   jax.ShapeDtypeStruct((nc, H, WKV_CHUNK, WKV_CHUNK), F32)],
        scratch_shapes=[pltpu.VMEM((H, N, N), F32)], compiler_params=_params("arbitrary"))(*seq)


def _wkv_bwd(seq, ckpt, pinv, dy, slabs, owners):
    s = seq[0].shape[0]
    nc = s // WKV_CHUNK
    nx = len(slabs)

    def body(r_ref, lw_ref, cl_ref, k_ref, v_ref, a_ref, b_ref, ck_ref, p_ref, dy_ref, *refs):
        src_refs, d_refs, dst_refs = refs[:nx], refs[nx:nx + 7], refs[nx + 7:2 * nx + 7]
        dstate = refs[2 * nx + 7]
        start, wait = _exchange_ops(src_refs, dst_refs, owners, refs[2 * nx + 8:])

        @pl.when(pl.program_id(0) == 0)
        def _():
            dstate[...] = jnp.zeros_like(dstate)
            start()

        p = p_ref[0]
        r, lw, cl, k, v, a, b, dy = (jnp.stack(_to_heads(ref[...])) for ref in (r_ref, lw_ref, cl_ref, k_ref, v_ref,
                                                                                 a_ref, b_ref, dy_ref))
        _, vjp = jax.vjp(_wkv_apply, ck_ref[0], r, lw, cl, k, v, a, b, p)
        ds0, dr, dlw, dcl, dk, dv, da, db, dp = vjp((dy, dstate[...]))
        dstate[...] = ds0
        _, vjp_x = jax.vjp(_wkv_aab, lw, cl, a, b)
        dlw2, dcl2, da2, db2 = vjp_x(_dot1(_dot1(p, dp, "tn"), p, "nt"))
        for d_ref, val in zip(d_refs, (dr, dlw + dlw2, dcl + dcl2, dk, dv, da + da2, db + db2)):
            d_ref[...] = jnp.concatenate([val[h] for h in range(H)], axis=-1)

        @pl.when(pl.program_id(0) == nc - 1)
        def _():
            wait()

    tm = pl.BlockSpec((WKV_CHUNK, DA), lambda c: (nc - 1 - c, 0))
    per_chunk = lambda m: pl.BlockSpec((1, H, m, m), lambda c: (nc - 1 - c, 0, 0, 0))
    out = pl.pallas_call(
        body, name="wkv_bwd", grid=(nc,),
        in_specs=[tm] * 7 + [per_chunk(N), per_chunk(WKV_CHUNK), tm] + _hbm_specs(nx),
        out_specs=[tm] * 7 + _hbm_specs(nx),
        out_shape=[jax.ShapeDtypeStruct((s, DA), F32)] * 7 + _received_shapes(slabs, owners),
        scratch_shapes=[pltpu.VMEM((H, N, N), F32)] + _exchange_scratch(nx),
        compiler_params=_params("arbitrary"))(*seq, ckpt, pinv, dy, *slabs)
    return out[:7], out[7:]


def _rwkv_post_fwd(y, r, k2, v, g, post_params):
    s = y.shape[0]
    tile = TOK_TILE

    def body(*refs):
        refs[-1][...] = _rwkv_post(*[ref[...] for ref in refs[:-1]])

    tm = pl.BlockSpec((tile, DA), lambda i: (i, 0))
    par = pl.BlockSpec((1, DA), lambda i: (0, 0))
    return pl.pallas_call(
        body, name="rwkv_post_fwd", grid=(s // tile,), in_specs=[tm] * 5 + [par] * 3,
        out_specs=tm, out_shape=jax.ShapeDtypeStruct((s, DA), F32),
        compiler_params=_params("arbitrary"))(y, r, k2, v, g, *post_params)


def _rwkv_post_bwd(y, r, k2, v, g, post_params, dya, slabs, lo):
    s = y.shape[0]
    tile = HEAD_TILE

    def body(y_ref, r_ref, k_ref, v_ref, g_ref, w_ref, b_ref, rk_ref, dya_ref, s_ref, *refs):
        d_refs, p_ref = refs[:8], refs[8]
        start, wait = _pair_swap_ops(s_ref, p_ref, lo, refs[9:])

        @pl.when(pl.program_id(0) == 0)
        def _():
            for ref in d_refs[5:]:
                ref[...] = jnp.zeros_like(ref)
            start()

        _, vjp = jax.vjp(_rwkv_post, *[ref[...] for ref in (y_ref, r_ref, k_ref, v_ref, g_ref, w_ref, b_ref, rk_ref)])
        grads = vjp(dya_ref[...])
        for ref, val in zip(d_refs[:5], grads[:5]):
            ref[...] = val
        for ref, val in zip(d_refs[5:], grads[5:]):
            ref[...] += val

        @pl.when(pl.program_id(0) == s // tile - 1)
        def _():
            wait()

    tm = pl.BlockSpec((tile, DA), lambda i: (i, 0))
    par = pl.BlockSpec((1, DA), lambda i: (0, 0))
    return pl.pallas_call(
        body, name="rwkv_post_bwd", grid=(s // tile,),
        in_specs=[tm] * 5 + [par] * 3 + [tm] + _hbm_specs(1),
        out_specs=[tm] * 5 + [par] * 3 + _hbm_specs(1),
        out_shape=[jax.ShapeDtypeStruct((s, DA), F32)] * 5 + [jax.ShapeDtypeStruct((1, DA), F32)] * 3
        + [jax.ShapeDtypeStruct(slabs.shape, slabs.dtype)],
        scratch_shapes=_pair_swap_scratch(slabs.shape[0]),
        compiler_params=_params("arbitrary"))(y, r, k2, v, g, *post_params, dya, slabs)


def _tri(t):
    return (lax.broadcasted_iota(jnp.int32, (t, t), 0) >= lax.broadcasted_iota(jnp.int32, (t, t), 1)).astype(F32)


def _fox_pre_fwd(ub, uf, q_g, k_g, f_b):
    s = ub.shape[0]
    tile = HEAD_TILE

    def body(ub_ref, uf_ref, qg_ref, kg_ref, fb_ref, q_ref, k_ref, v_ref, cum_ref, carry):
        @pl.when(pl.program_id(0) == 0)
        def _():
            carry[...] = jnp.zeros_like(carry)

        qn, kn, logf = _fox_pre(ub_ref[:, :DA], ub_ref[:, DA:2 * DA], uf_ref[...], qg_ref[...], kg_ref[...],
                                fb_ref[...])
        for h, (q_col, k_col) in enumerate(zip(_to_heads(qn), _to_heads(kn))):
            q_ref[h] = q_col
            k_ref[h] = k_col
        v_ref[...] = _heads(ub_ref, 2 * DA)
        cum = jnp.dot(_tri(tile), logf, precision=HI, preferred_element_type=F32) + carry[...]
        cum_ref[...] = cum
        carry[...] = cum[tile - 1:tile, :]

    hm = pl.BlockSpec((H, tile, N), lambda i: (0, i, 0))
    fixed = lambda shape: pl.BlockSpec(shape, lambda i: (0,) * len(shape))
    return pl.pallas_call(
        body, name="fox_pre_fwd", grid=(s // tile,),
        in_specs=[pl.BlockSpec((tile, NB), lambda i: (i, 0)), pl.BlockSpec((tile, NF), lambda i: (i, 0)),
                  fixed((1, DA)), fixed((1, DA)), fixed((1, NF))],
        out_specs=[hm] * 3 + [pl.BlockSpec((tile, NF), lambda i: (i, 0))],
        out_shape=[jax.ShapeDtypeStruct((H, s, N), F32)] * 3 + [jax.ShapeDtypeStruct((s, NF), F32)],
        scratch_shapes=[pltpu.VMEM((1, NF), F32)], compiler_params=_params("arbitrary"))(ub, uf, q_g, k_g, f_b)


def _fox_pre_bwd(ub, uf, q_g, k_g, f_b, dqn, dkn, dvf, dgate, dcum_q, dcum_k):
    s = ub.shape[0]
    tile = HEAD_TILE
    nt = s // tile

    def body(ub_ref, uf_ref, qg_ref, kg_ref, fb_ref, dq_ref, dk_ref, dv_ref, dgate_ref, dcq_ref, dck_ref,
             dub_ref, duf_ref, dqg_ref, dkg_ref, dfb_ref, carry):
        @pl.when(pl.program_id(0) == 0)
        def _():
            carry[...] = jnp.zeros_like(carry)
            for ref in (dqg_ref, dkg_ref, dfb_ref):
                ref[...] = jnp.zeros_like(ref)

        dcum = dcq_ref[...] + dck_ref[...]
        dlogf = lax.dot_general(_tri(tile), dcum, (((0,), (0,)), ((), ())), precision=HI,
                                preferred_element_type=F32) + carry[...]
        carry[...] = dlogf[0:1, :]
        _, vjp = jax.vjp(_fox_pre, ub_ref[:, :DA], ub_ref[:, DA:2 * DA], uf_ref[...], qg_ref[...], kg_ref[...],
                         fb_ref[...])
        d_q, d_k, d_f, d_qg, d_kg, d_fb = vjp((_from_heads(dq_ref), _from_heads(dk_ref), dlogf))
        dub_ref[:, :DA] = d_q
        dub_ref[:, DA:2 * DA] = d_k
        _store_heads(dub_ref, 2 * DA, dv_ref[...])
        dub_ref[:, 3 * DA:] = dgate_ref[...]
        duf_ref[...] = d_f
        dqg_ref[...] += functools.reduce(jnp.add, _to_heads(d_qg))
        dkg_ref[...] += functools.reduce(jnp.add, _to_heads(d_kg))
        dfb_ref[...] += d_fb

    rev = lambda i: nt - 1 - i
    hm = pl.BlockSpec((H, tile, N), lambda i: (0, rev(i), 0))
    tok = lambda n: pl.BlockSpec((tile, n), lambda i: (rev(i), 0))
    fixed = lambda shape: pl.BlockSpec(shape, lambda i: (0,) * len(shape))
    return pl.pallas_call(
        body, name="fox_pre_bwd", grid=(nt,),
        in_specs=[tok(NB), tok(NF), fixed((1, DA)), fixed((1, DA)), fixed((1, NF)), hm, hm, hm, tok(DA), tok(NF),
                  tok(NF)],
        out_specs=[tok(NB), tok(NF), fixed((1, N)), fixed((1, N)), fixed((1, NF))],
        out_shape=[jax.ShapeDtypeStruct((s, NB), F32), jax.ShapeDtypeStruct((s, NF), F32),
                   jax.ShapeDtypeStruct((1, N), F32), jax.ShapeDtypeStruct((1, N), F32),
                   jax.ShapeDtypeStruct((1, NF), F32)],
        scratch_shapes=[pltpu.VMEM((1, NF), F32)],
        compiler_params=_params("arbitrary"))(ub, uf, q_g, k_g, f_b, dqn, dkn, dvf, dgate, dcum_q, dcum_k)


def _att_groups(s):
    blocks = s // ATT_TILE
    per = max(1, blocks // ATT_GROUPS)
    return per, blocks // per


def _att_parts(n, width):
    return ([(0, n - width, False)] if n > width else []) + [(n - width, n, True)]


def _att_scores(q_bf, k_ref, ck_ref, lo, hi, masked, row_offset):
    scores = _bdot_nt(q_bf, k_ref[0, lo:hi, :]) - ck_ref[0, :, lo:hi]
    if masked:
        rows = row_offset + lax.broadcasted_iota(jnp.int32, scores.shape, 0)
        scores = jnp.where(rows >= lax.broadcasted_iota(jnp.int32, scores.shape, 1), scores, -1e30)
    return scores


def _fox_attn_fwd(q, k, v, cum_q, cum_k):
    s = q.shape[1]
    t = ATT_TILE
    per, groups = _att_groups(s)

    def body(q_ref, k_ref, v_ref, cq_ref, ck_ref, o_ref, lse_ref):
        qi = pl.program_id(1)
        for g in range(groups):
            @pl.when(qi // per == g)
            def _(g=g):
                q_bf = (q_ref[0] * ATT_SCALE).astype(BF16)
                parts = _att_parts((g + 1) * per * t, per * t)
                scores = [_att_scores(q_bf, k_ref, ck_ref, lo, hi, masked, (qi - g * per) * t)
                          for lo, hi, masked in parts]
                m = functools.reduce(jnp.maximum, [jnp.max(sc, axis=-1, keepdims=True) for sc in scores])
                l, acc = 0.0, 0.0
                for sc, (lo, hi, _) in zip(scores, parts):
                    p = jnp.exp(sc - m)
                    l += jnp.sum(p, axis=-1, keepdims=True)
                    acc += _bdot(p, v_ref[0, lo:hi, :])
                o_ref[0] = acc / l
                lse_ref[0] = m + jnp.log(l) + cq_ref[0]

    qb = pl.BlockSpec((1, t, N), lambda h, i: (h, i, 0))
    kb = pl.BlockSpec((1, s, N), lambda h, i: (h, 0, 0))
    return pl.pallas_call(
        body, name="fox_attn_fwd", grid=(H, s // t),
        in_specs=[qb, kb, kb, pl.BlockSpec((1, t, 1), lambda h, i: (h, i, 0)),
                  pl.BlockSpec((1, 1, s), lambda h, i: (h, 0, 0))],
        out_specs=[qb, pl.BlockSpec((1, t, 1), lambda h, i: (h, i, 0))],
        out_shape=[jax.ShapeDtypeStruct((H, s, N), F32), jax.ShapeDtypeStruct((H, s, 1), F32)],
        compiler_params=_params("arbitrary", "arbitrary"))(q, k, v, cum_q, cum_k)


def _fox_attn_bwd(q, k, v, cum_q, cum_k, o, lse, do, slabs, owners):
    s = q.shape[1]
    t = ATT_TILE
    per, groups = _att_groups(s)
    nx = len(slabs)

    def body(q_ref, k_ref, v_ref, cq_ref, ck_ref, o_ref, lse_ref, do_ref, *refs):
        src_refs, (dq_ref, dk_ref, dv_ref, dcq_ref, dck_ref) = refs[:nx], refs[nx:nx + 5]
        start, wait = _exchange_ops(src_refs, refs[nx + 5:2 * nx + 5], owners, refs[2 * nx + 5:])
        qi = pl.program_id(1)

        @pl.when((pl.program_id(0) == 0) & (qi == 0))
        def _():
            start()

        @pl.when(qi == 0)
        def _():
            for ref in (dk_ref, dv_ref, dck_ref):
                ref[...] = jnp.zeros_like(ref)

        for g in range(groups):
            @pl.when(qi // per == g)
            def _(g=g):
                q_bf, do_bf = (q_ref[0] * ATT_SCALE).astype(BF16), do_ref[0].astype(BF16)
                row_term = cq_ref[0] - lse_ref[0]
                delta = jnp.sum(do_ref[0] * o_ref[0], axis=-1, keepdims=True)
                dq, dcq = 0.0, 0.0
                for lo, hi, masked in _att_parts((g + 1) * per * t, per * t):
                    p = jnp.exp(_att_scores(q_bf, k_ref, ck_ref, lo, hi, masked, (qi - g * per) * t) + row_term)
                    ds = p * (_bdot_nt(do_bf, v_ref[0, lo:hi, :]) - delta)
                    dq += _bdot(ds, k_ref[0, lo:hi, :])
                    dcq += jnp.sum(ds, axis=-1, keepdims=True)
                    dk_ref[0, lo:hi, :] += _bdot_tn(ds, q_bf)
                    dv_ref[0, lo:hi, :] += _bdot_tn(p, do_bf)
                    dck_ref[0, :, lo:hi] -= jnp.sum(ds, axis=0, keepdims=True)
                dq_ref[0] = dq * ATT_SCALE
                dcq_ref[0] = dcq

        @pl.when((pl.program_id(0) == H - 1) & (qi == s // t - 1))
        def _():
            wait()

    qb = pl.BlockSpec((1, t, N), lambda h, i: (h, i, 0))
    kb = pl.BlockSpec((1, s, N), lambda h, i: (h, 0, 0))
    cqb = pl.BlockSpec((1, t, 1), lambda h, i: (h, i, 0))
    ckb = pl.BlockSpec((1, 1, s), lambda h, i: (h, 0, 0))
    f32 = lambda *shape: jax.ShapeDtypeStruct(shape, F32)
    out = pl.pallas_call(
        body, name="fox_attn_bwd", grid=(H, s // t),
        in_specs=[qb, kb, kb, cqb, ckb, qb, cqb, qb] + _hbm_specs(nx), out_specs=[qb, kb, kb, cqb, ckb] + _hbm_specs(nx),
        out_shape=[f32(H, s, N), f32(H, s, N), f32(H, s, N), f32(H, s, 1), f32(H, 1, s)]
        + _received_shapes(slabs, owners),
        scratch_shapes=_exchange_scratch(nx),
        compiler_params=_params("arbitrary", "arbitrary"))(q, k, v, cum_q, cum_k, o, lse, do, *slabs)
    return out[:5], out[5:]


def _local_step(x, target, w, p):
    mu = p["shift_mu"]
    lora_matrix = lambda a: jnp.moveaxis(a, 0, 1).reshape(RANK, DA).astype(F32)
    pre_params = (mu[:, 0:DA], mu[:, DA:2 * DA], mu[:, 2 * DA:3 * DA], mu[:, 3 * DA + 2 * RANK:],
                  mu[:, 3 * DA:3 * DA + RANK], mu[:, 3 * DA + RANK:3 * DA + 2 * RANK],
                  lora_matrix(w["w_lora_up"]), p["w0"], lora_matrix(w["a_lora_up"]), p["a0"], p["k_k"], p["k_a"])
    post_params = (p["lnx_w"], p["lnx_b"], p["r_k"])
    q_g, k_g = jnp.tile(p["q_norm_g"], (1, H)), jnp.tile(p["k_norm_g"], (1, H))
    f_b = jnp.pad(p["f_bias"], ((0, 0), (0, NF - H)))
    fg = p["final_norm_g"].reshape(1, D)

    h, (ua, ub, ug, uf) = _norm_proj(x, p["norm_g"], (w["in_a"], w["in_b"], w["in_g"], w["in_f"]))
    r, lw, cl, k2, v, av, bv, gg = _rwkv_pre_fwd(ua, pre_params)
    y, ckpt, pinv = _wkv_fwd((r, lw, cl, k2, v, av, bv))
    ya = _rwkv_post_fwd(y, r, k2, v, gg, post_params)
    qn, kn, vf, cum = _fox_pre_fwd(ub, uf, q_g, k_g, f_b)
    cum_t = cum[:, :H].T
    cum_q, cum_k = cum_t[:, :, None], cum_t[:, None, :]
    o, lse = _fox_attn_fwd(qn, kn, vf, cum_q, cum_k)

    (loss, dfg, dwo, dwoa, dwob, dx2, dya, do, dgate_b, dug) = _tail(
        x, target, ya, o, ub, ug, w["w_out_a"], w["w_out_b"], w["w_out"], fg)
    everyone = (0, N_DEV)
    (dqn, dkn, dvf, dcq, dck), (recv_woa, recv_wob, recv_wo) = _fox_attn_bwd(
        qn, kn, vf, cum_q, cum_k, o, lse, do,
        (_col_slabs(dwoa), _col_slabs(dwob), dwo.astype(BF16).reshape(N_DEV, D // N_DEV, D)), (everyone,) * 3)
    pad_f = lambda a: jnp.pad(a.T, ((0, 0), (0, NF - H)))
    dub, duf, dqg, dkg, dfb = _fox_pre_bwd(ub, uf, q_g, k_g, f_b, dqn, dkn, dvf, dgate_b,
                                           pad_f(dcq[:, :, 0]), pad_f(dck.reshape(H, -1)))
    spill = EARLY_FROM * COLS_PER_DEV - NA
    early, dwt_b_head = _proj_wgrad_early(h, dub, dug, duf, -(-spill // 8) * 8)
    dy, dr_p, dk_p, dv_p, dgg, dlnw, dlnb, drk, handed = _rwkv_post_bwd(y, r, k2, v, gg, post_params, dya, early,
                                                                          EARLY_FROM)
    early = _chip_sums(early, handed, EARLY_FROM, "chip_sums_early")
    (dr_s, dlw, dcl, dk_s, dv_s, dav, dbv), (recv_early,) = _wkv_bwd(
        (r, lw, cl, k2, v, av, bv), ckpt, pinv, dy, (early,), ((EARLY_FROM, N_DEV, "chips"),))
    pre_out = _rwkv_pre_bwd(ua, pre_params, (dr_s, dr_p, dlw, dcl, dk_s, dk_p, dv_s, dv_p, dav, dbv, dgg))
    dua, dpre = pre_out[0], pre_out[1:]
    late = _proj_wgrad_late(h, dua, dwt_b_head)

    flat = lambda a: a.reshape(1, -1)
    small = {
        "final_norm_g": dfg, "w0": dpre[7], "a0": dpre[9], "k_k": dpre[10], "k_a": dpre[11], "r_k": drk, "lnx_w": dlnw,
        "lnx_b": dlnb, "q_norm_g": dqg, "k_norm_g": dkg, "f_bias": dfb[:, :H],
        "shift_mu": jnp.concatenate([flat(dpre[0]), flat(dpre[1]), flat(dpre[2]), dpre[4], dpre[5], flat(dpre[3])], axis=1),
    }
    late = _chip_sums(late, _pair_swap(late, 0, "pair_swap_late"), 0, "chip_sums_late")
    by_head = lambda a: jnp.moveaxis(a.reshape(RANK, H, N), 1, 0)
    loras = jnp.stack([by_head(dpre[6]), by_head(dpre[8])], axis=1).astype(BF16)
    dx, dng, (recv_late, recv_lora, recv_small) = _proj_xgrad(
        x, p["norm_g"], dx2, (dua, dub, dug, duf), (w["in_a"], w["in_b"], w["in_g"], w["in_f"]),
        (late, loras, _pack_small(small, loss)), ((0, EARLY_FROM, "chips"), everyone, everyone))
    return dx, dng, (recv_early, recv_late), (recv_woa, recv_wob, recv_wo, recv_lora), recv_small


def _position():
    return lax.axis_index("x"), lax.axis_index("y"), lax.axis_index("c")


def _hbm_specs(n):
    return [pl.BlockSpec(memory_space=pl.ANY)] * n


BIG_GATHER_COPIES = 13
GATHER_ROW_CUT = 400


def _all_gather(big, blocks, name):
    n = len(blocks)

    def body(*refs):
        big_ref, x_refs = refs[0], refs[1:1 + n]
        big_out, out_refs = refs[1 + n], refs[2 + n:2 + 2 * n]
        send_sems, recv_sems, local_sems = refs[2 + 2 * n:]
        x, y, c = _position()
        me, sibling = (x, y, c), (x, y, 1 - c)
        chips = [(1 - x, y), (x, 1 - y), (1 - x, 1 - y)]
        x_nbr, y_nbr, diag = chips
        rows = big_ref.shape[0]
        cut = GATHER_ROW_CUT

        def part(ref, h):
            return ref if h is None else ref.at[pl.ds(0, cut)] if h == 0 else ref.at[pl.ds(cut, rows - cut)]

        def landed(chip, core, h):
            return part(big_out.at[4 * chip[0] + 2 * chip[1] + core], h)

        def big_copy(k, src, dst, to):
            return pltpu.make_async_remote_copy(src_ref=src, dst_ref=dst, send_sem=send_sems.at[7 * n + k],
                                                recv_sem=recv_sems.at[7 * n + k], device_id=to, device_id_type=MESH)

        def arrival(k, chip, core, h):
            dst = landed(chip, core, h)
            return big_copy(k, dst, dst, me)

        def pass_on(k, chip, h, to):
            src = landed(chip, c, h)
            return big_copy(k, src, src, to)

        big_mine = pltpu.make_async_copy(big_ref, landed((x, y), c, None), local_sems.at[n])
        big_mine.start()
        here = (x, y)
        big_sent = [big_copy(0, big_ref, landed(here, c, None), sibling),
                    big_copy(1, part(big_ref, 0), landed(here, c, 0), (*x_nbr, c)),
                    big_copy(2, part(big_ref, 1), landed(here, c, 1), (*y_nbr, c)),
                    big_copy(3, part(big_ref, 1), landed(here, c, 1), (*x_nbr, c)),
                    big_copy(4, part(big_ref, 0), landed(here, c, 0), (*y_nbr, c))]
        for cp in big_sent:
            cp.start()

        def copy(a, k, blk, to, own=False):
            dst = out_refs[a].at[4 * blk[0] + 2 * blk[1] + blk[2]]
            return pltpu.make_async_remote_copy(
                src_ref=x_refs[a] if own else dst, dst_ref=dst, send_sem=send_sems.at[7 * a + k],
                recv_sem=recv_sems.at[7 * a + k], device_id=to, device_id_type=MESH)

        mine = [pltpu.make_async_copy(x_refs[a], out_refs[a].at[4 * x + 2 * y + c], local_sems.at[a]) for a in range(n)]
        for cp in mine:
            cp.start()
        first = []
        for a in range(n):
            first.append(copy(a, 0, me, sibling, own=True))
            first += [copy(a, 1 + j, me, (*chip, c), own=True) for j, chip in enumerate(chips)]
        for cp in first:
            cp.start()

        big_steps = [(1, x_nbr, 0, (*y_nbr, c), 5, 7), (2, y_nbr, 1, (*x_nbr, c), 6, 8), (3, x_nbr, 1, None, None, 9),
                     (4, y_nbr, 0, None, None, 10), (5, diag, 0, None, None, 11), (6, diag, 1, None, None, 12)]
        for k, chip, h, onward, k_onward, k_sibling in big_steps:
            arrival(k, chip, c, h).wait_recv()
            if onward is not None:
                big_sent.append(pass_on(k_onward, chip, h, onward))
                big_sent[-1].start()
            big_sent.append(pass_on(k_sibling, chip, h, sibling))
            big_sent[-1].start()

        passed = []
        for j, chip in enumerate(chips):
            for a in range(n):
                copy(a, 1 + j, (*chip, c), me).wait_recv()
                passed.append(copy(a, 4 + j, (*chip, c), sibling))
                passed[-1].start()
        for a in range(n):
            copy(a, 0, sibling, me).wait_recv()
        for j, chip in enumerate(chips):
            for a in range(n):
                copy(a, 4 + j, (*chip, 1 - c), me).wait_recv()
        arrival(0, here, 1 - c, None).wait_recv()
        for k, chip, h, _, _, k_sibling in big_steps:
            arrival(k_sibling, chip, 1 - c, h).wait_recv()
        for cp in first + passed + big_sent:
            cp.wait_send()
        for cp in mine + [big_mine]:
            cp.wait()

    everything = [big] + list(blocks)
    return pl.pallas_call(
        body, name=name, out_shape=[jax.ShapeDtypeStruct((N_DEV,) + b.shape, b.dtype) for b in everything],
        in_specs=_hbm_specs(n + 1), out_specs=_hbm_specs(n + 1),
        scratch_shapes=[pltpu.SemaphoreType.DMA((7 * n + BIG_GATHER_COPIES,)),
                        pltpu.SemaphoreType.DMA((7 * n + BIG_GATHER_COPIES,)), pltpu.SemaphoreType.DMA((n + 1,))],
    )(*everything)


def _received_shapes(slabs, owners):
    return [jax.ShapeDtypeStruct((N_DEV // 2 if len(o) == 3 else N_DEV,) + s.shape[1:], s.dtype)
            for s, o in zip(slabs, owners)]


def _pair_swap_scratch(n):
    return [pltpu.SemaphoreType.DMA((n,)), pltpu.SemaphoreType.DMA((n,))]


def _pair_swap_ops(s_ref, p_ref, lo, sems):
    send_sems, recv_sems = sems
    n = s_ref.shape[0]

    def run(sending):
        x, y, c = _position()
        for side in (0, 1):
            mine = [pltpu.make_async_remote_copy(src_ref=s_ref.at[i], dst_ref=p_ref.at[i], send_sem=send_sems.at[i],
                                                 recv_sem=recv_sems.at[i], device_id=(x, y, 1 - c), device_id_type=MESH)
                    for i in range(n) if (lo + i) % 2 == side]

            @pl.when(c != side)
            def _():
                for cp in mine:
                    cp.start() if sending else cp.wait_send()

            if not sending:
                @pl.when(c == side)
                def _():
                    for cp in mine:
                        cp.wait_recv()

    return functools.partial(run, True), functools.partial(run, False)


def _pair_swap(slabs, lo, name):
    n = slabs.shape[0]

    def body(s_ref, p_ref, *sems):
        start, wait = _pair_swap_ops(s_ref, p_ref, lo, sems)
        start()
        wait()

    return pl.pallas_call(
        body, name=name, out_shape=jax.ShapeDtypeStruct(slabs.shape, slabs.dtype),
        in_specs=_hbm_specs(1), out_specs=_hbm_specs(1)[0], scratch_shapes=_pair_swap_scratch(n))(slabs)


def _chip_sums(slabs, swapped, lo, name):
    n, rows, cols = slabs.shape
    tile = W_IN_COL_TILE

    def body(s_ref, p_ref, o_ref):
        c = lax.axis_index("c")
        for i in range(n):
            @pl.when(c == (lo + i) % 2)
            def _(i=i):
                o_ref[i] = (s_ref[i].astype(F32) + p_ref[i].astype(F32)).astype(BF16)

    blk = pl.BlockSpec((n, rows, tile), lambda j: (0, 0, j))
    return pl.pallas_call(
        body, name=name, grid=(cols // tile,), in_specs=[blk, blk], out_specs=blk,
        out_shape=jax.ShapeDtypeStruct(slabs.shape, BF16), compiler_params=_params("arbitrary"))(slabs, swapped)


def _exchange_scratch(n):
    return [pltpu.SemaphoreType.DMA((7 * n,)), pltpu.SemaphoreType.DMA((7 * n,)), pltpu.SemaphoreType.DMA((n,))]


def _exchange_ops(src_refs, dst_refs, owners, sems):
    send_sems, recv_sems, local_sems = sems
    n = len(src_refs)

    def guarded(a, dev, fn):
        lo, hi = owners[a][:2]
        if (lo, hi) == (0, N_DEV):
            fn()
        else:
            pl.when((dev >= lo) & (dev < hi))(fn)

    def src(a, dev):
        ref = src_refs[a]
        return ref.at[0] if ref.shape[0] == 1 else ref.at[dev - owners[a][0]]

    def run(sending, waiting):
        x, y, c = _position()
        me = 4 * x + 2 * y + c
        for a in range(n):
            by_chip = len(owners[a]) == 3
            slot = (lambda qx, qy, qc: 2 * qx + qy) if by_chip else (lambda qx, qy, qc: 4 * qx + 2 * qy + qc)
            mine = slot(x, y, c)
            local = lambda a=a, mine=mine: pltpu.make_async_copy(src(a, me), dst_refs[a].at[mine], local_sems.at[a])
            if sending:
                guarded(a, me, lambda local=local: local().start())
            for m in range(2, N_DEV, 2) if by_chip else range(1, N_DEV):
                px, py, pc = x ^ (m >> 2), y ^ ((m >> 1) & 1), c ^ (m & 1)
                peer = 4 * px + 2 * py + pc
                theirs = slot(px, py, pc)
                sem = dict(send_sem=send_sems.at[7 * a + m - 1], recv_sem=recv_sems.at[7 * a + m - 1],
                           device_id=(px, py, pc), device_id_type=MESH)
                send = lambda a=a, peer=peer, sem=sem, mine=mine: pltpu.make_async_remote_copy(
                    src_ref=src(a, peer), dst_ref=dst_refs[a].at[mine], **sem)
                recv = lambda a=a, sem=sem, theirs=theirs: pltpu.make_async_remote_copy(
                    src_ref=src(a, me), dst_ref=dst_refs[a].at[theirs], **sem)
                if sending:
                    guarded(a, peer, lambda send=send: send().start())
                if waiting:
                    guarded(a, me, lambda recv=recv: recv().wait_recv())
                    guarded(a, peer, lambda send=send: send().wait_send())
            if waiting:
                guarded(a, me, lambda local=local: local().wait())

    return functools.partial(run, True, False), functools.partial(run, False, True)


def _sum_slabs(r_ref):
    g = r_ref[0].astype(F32)
    for k in range(1, r_ref.shape[0]):
        g = g + r_ref[k].astype(F32)
    return g


def _adamw(g, w, m, v):
    m_new = ADAM_B1 * m + (1.0 - ADAM_B1) * g
    v_new = ADAM_B2 * v + (1.0 - ADAM_B2) * (g * g)
    m_hat = m_new / (1.0 - ADAM_B1 ** ADAM_STEP)
    v_hat = v_new / (1.0 - ADAM_B2 ** ADAM_STEP)
    return g, -ADAM_LR * (m_hat / (jnp.sqrt(v_hat) + ADAM_EPS) + ADAM_WD * w), m_new, v_new


def _adamw_w_in(recv_early, recv_late, w, m, v, slabs, owners):
    rows, cols = w.shape
    tile = W_IN_COL_TILE
    nx = len(slabs)

    def body(early_ref, late_ref, w_ref, m_ref, v_ref, *refs):
        src_refs, o_refs, dst_refs = refs[:nx], refs[nx:nx + 4], refs[nx + 4:2 * nx + 4]
        start, wait = _exchange_ops(src_refs, dst_refs, owners, refs[2 * nx + 4:])
        x, y, c = _position()
        early_owner = 4 * x + 2 * y + c >= EARLY_FROM

        @pl.when(pl.program_id(0) == 0)
        def _():
            start()

        def update(g):
            for o_ref, val in zip(o_refs, _adamw(g, w_ref[...], m_ref[...], v_ref[...])):
                o_ref[...] = val

        pl.when(early_owner)(lambda: update(_sum_slabs(early_ref)))
        pl.when(jnp.logical_not(early_owner))(lambda: update(_sum_slabs(late_ref)))

        @pl.when(pl.program_id(0) == cols // tile - 1)
        def _():
            wait()

    blk = pl.BlockSpec((rows, tile), lambda i: (0, i))
    slots = lambda r: pl.BlockSpec((r.shape[0], rows, tile), lambda i: (0, 0, i))
    out = pl.pallas_call(
        body, name="adamw_w_in", grid=(cols // tile,),
        in_specs=[slots(recv_early), slots(recv_late), blk, blk, blk] + _hbm_specs(nx),
        out_specs=[blk] * 4 + _hbm_specs(nx),
        out_shape=[jax.ShapeDtypeStruct((rows, cols), F32)] * 4 + _received_shapes(slabs, owners),
        scratch_shapes=_exchange_scratch(nx),
        compiler_params=_params("arbitrary"))(recv_early, recv_late, w, m, v, *slabs)
    return out[:4], out[4:]


def _adamw_misc(recvs, recv_small, recv_norm, params):
    names = list(params)
    flat = [a for n in names for a in params[n]]

    def body(woa_ref, wob_ref, wo_ref, lora_ref, small_ref, norm_ref, *refs):
        p_refs, o_refs = refs[:len(flat)], refs[len(flat):]
        g_small = _sum_slabs(small_ref)
        g_lora = _sum_slabs(lora_ref)
        grads = {"w_out_a": _sum_slabs(woa_ref), "w_out_b": _sum_slabs(wob_ref), "w_out": _sum_slabs(wo_ref),
                 "w_lora_up": g_lora[0], "a_lora_up": g_lora[1], "norm_g": _sum_slabs(norm_ref)}
        for n, (off, size) in SMALL_SLOTS.items():
            grads[n] = g_small[:, off:off + size]
        for i, n in enumerate(names):
            w_ref, m_ref, v_ref = p_refs[3 * i:3 * i + 3]
            for o_ref, val in zip(o_refs[4 * i:4 * i + 4], _adamw(grads[n], w_ref[...], m_ref[...], v_ref[...])):
                o_ref[...] = val
        o_refs[-1][...] = g_small[:, LOSS_SLOT:LOSS_SLOT + 1]

    out = pl.pallas_call(
        body, name="adamw_misc",
        out_shape=[jax.ShapeDtypeStruct(params[n][0].shape, F32) for n in names for _ in range(4)]
        + [jax.ShapeDtypeStruct((1, 1), F32)],
        compiler_params=_params())(*recvs, recv_small, recv_norm, *flat)
    return {n: out[4 * i:4 * i + 4] for i, n in enumerate(names)}, out[-1]


_WT_SEGMENTS = ((0, NA), (NA, NB), (NA + NB + H, NG), (NA + NB, H))


def _split_wt(gathered):
    tile = W_IN_COL_TILE

    def body(g_ref, *o_refs):
        full = jnp.concatenate([g_ref[j] for j in range(N_DEV)], axis=0)
        for o_ref, (row, n) in zip(o_refs, _WT_SEGMENTS):
            seg = full[row:row + n]
            if n < o_ref.shape[0]:
                seg = jnp.concatenate([seg, jnp.zeros((o_ref.shape[0] - n, tile), BF16)], axis=0)
            o_ref[...] = seg

    sizes = (NA, NB, NG, NF)
    return pl.pallas_call(
        body, name="split_wt", grid=(D // tile,),
        in_specs=[pl.BlockSpec((N_DEV, COLS_PER_DEV, tile), lambda i: (0, 0, i))],
        out_specs=[pl.BlockSpec((n, tile), lambda i: (0, i)) for n in sizes],
        out_shape=[jax.ShapeDtypeStruct((n, D), BF16) for n in sizes],
        compiler_params=_params("arbitrary"))(gathered)


def _by_cols(a):
    return jnp.moveaxis(a, 0, 1).reshape(a.shape[1], -1)


def _col_slabs(a):
    return jnp.moveaxis(a.reshape(a.shape[0], N_DEV, -1), 1, 0).astype(BF16)


def _pack_small(grads, loss):
    pieces, at = [], 0
    for n, (off, size) in list(SMALL_SLOTS.items()) + [("loss", (LOSS_SLOT, 1))]:
        pieces += [jnp.zeros((off - at,), F32), (loss if n == "loss" else grads[n]).reshape(-1)]
        at = off + size
    return jnp.concatenate(pieces + [jnp.zeros((SMALL_LEN - at,), F32)]).reshape(1, 1, SMALL_LEN)


def _gather_weights(t):
    cast = lambda a: a.astype(BF16)
    loras = jnp.stack([t["w_lora_up"][0], t["a_lora_up"][0]])
    wt, woa, wob, wo, lora = _all_gather(
        cast(t["w_in"][0].T), [cast(t["w_out_a"][0]), cast(t["w_out_b"][0]), cast(t["w_out"][0]), cast(loras)],
        "weight_gather")
    in_a, in_b, in_g, in_f = _split_wt(wt)
    return {"in_a": in_a, "in_b": in_b, "in_g": in_g, "in_f": in_f, "w_out_a": _by_cols(woa), "w_out_b": _by_cols(wob),
            "w_out": wo.reshape(D, D), "w_lora_up": lora[:, 0], "a_lora_up": lora[:, 1]}


def kernel(x, norm_g, w_in, shift_mu, w_lora_up, w0, a_lora_up, a0, k_k, k_a, r_k, lnx_w, lnx_b, f_bias, q_norm_g, k_norm_g, w_out_a, w_out_b, w_out, final_norm_g, loss_target, m_norm_g, m_w_in, m_shift_mu, m_w_lora_up, m_w0, m_a_lora_up, m_a0, m_k_k, m_k_a, m_r_k, m_lnx_w, m_lnx_b, m_f_bias, m_q_norm_g, m_k_norm_g, m_w_out_a, m_w_out_b, m_w_out, m_final_norm_g, v_norm_g, v_w_in, v_shift_mu, v_w_lora_up, v_w0, v_a_lora_up, v_a0, v_k_k, v_k_a, v_r_k, v_lnx_w, v_lnx_b, v_f_bias, v_q_norm_g, v_k_norm_g, v_w_out_a, v_w_out_b, v_w_out, v_final_norm_g):
    names = ("norm_g", "w_in", "shift_mu", "w_lora_up", "w0", "a_lora_up", "a0", "k_k", "k_a", "r_k", "lnx_w", "lnx_b",
             "f_bias", "q_norm_g", "k_norm_g", "w_out_a", "w_out_b", "w_out", "final_norm_g")
    weights = dict(zip(names, (norm_g, w_in, shift_mu, w_lora_up, w0, a_lora_up, a0, k_k, k_a, r_k, lnx_w, lnx_b,
                               f_bias, q_norm_g, k_norm_g, w_out_a, w_out_b, w_out, final_norm_g)))
    m_in = dict(zip(names, (m_norm_g, m_w_in, m_shift_mu, m_w_lora_up, m_w0, m_a_lora_up, m_a0, m_k_k, m_k_a, m_r_k,
                            m_lnx_w, m_lnx_b, m_f_bias, m_q_norm_g, m_k_norm_g, m_w_out_a, m_w_out_b, m_w_out,
                            m_final_norm_g)))
    v_in = dict(zip(names, (v_norm_g, v_w_in, v_shift_mu, v_w_lora_up, v_w0, v_a_lora_up, v_a0, v_k_k, v_k_a, v_r_k,
                            v_lnx_w, v_lnx_b, v_f_bias, v_q_norm_g, v_k_norm_g, v_w_out_a, v_w_out_b, v_w_out,
                            v_final_norm_g)))

    matrices = ("w_out_a", "w_out_b", "w_out", "w_lora_up", "a_lora_up")
    as_2d = lambda n, a: a[0] if n in matrices else a.reshape(1, -1)

    full = _gather_weights(weights)
    dx, dng, recv_wt, recvs, recv_small = _local_step(
        x[0], loss_target[0], full, {n: as_2d(n, weights[n]) for n in ("norm_g",) + tuple(SMALL_SLOTS)})

    res, (recv_norm,) = _adamw_w_in(*recv_wt, w_in[0].T, m_w_in[0].T, v_w_in[0].T, (dng[None],), ((0, N_DEV),))
    outs = {"w_in": [r.T[None] for r in res]}
    misc = [n for n in names if n != "w_in"]
    res, loss_sum = _adamw_misc(recvs, recv_small, recv_norm,
                                {n: tuple(as_2d(n, t[n]) for t in (weights, m_in, v_in)) for n in misc})
    for n in misc:
        outs[n] = [r.reshape(weights[n].shape) for r in res[n]]
    return (loss_sum.reshape(()), dx[None], *[outs[n][i] for i in range(4) for n in names])
```

```python
import functools
import math

import jax
import jax.numpy as jnp
from jax import lax
from jax.experimental import pallas as pl
from jax.experimental.pallas import tpu as pltpu

F32 = jnp.float32
BF16 = jnp.bfloat16
HI = lax.Precision.HIGHEST
MESH = pl.DeviceIdType.MESH

N_DEV = 8
D = 1024
H = 8
N = 64
DA = H * N
RANK = 64
NA = 4 * DA + 2 * RANK
NB = 4 * DA
NG = 2 * D
NF = 128
IN_COLS = NA + NB + H + NG
COLS_PER_DEV = IN_COLS // N_DEV
RMS_EPS = 1e-6
LNX_EPS = 64e-5
ATT_SCALE = N ** -0.5

ADAM_LR = 0.001
ADAM_B1 = 0.9
ADAM_B2 = 0.999
ADAM_EPS = 1e-08
ADAM_WD = 0.01
ADAM_STEP = 10

LANES = 128
WKV_CHUNK = 64
TOK_TILE = 256
HEAD_TILE = 256
XGRAD_TILE = 128
ATT_TILE = 256
ATT_GROUPS = 8
VMEM_LIMIT = 56 * 1024 * 1024


def _lane_tile_slots(sizes):
    slots, at = {}, 0
    for name, size in sizes:
        slots[name] = (at, size)
        at += -(-size // LANES) * LANES
    return slots, at


SMALL_SLOTS, LOSS_SLOT = _lane_tile_slots((
    ("final_norm_g", D), ("shift_mu", NA), ("w0", DA), ("a0", DA), ("k_k", DA), ("k_a", DA), ("r_k", DA), ("lnx_w", DA),
    ("lnx_b", DA), ("q_norm_g", N), ("k_norm_g", N), ("f_bias", H)))
SMALL_LEN = LOSS_SLOT + LANES
W_IN_COL_TILE = 512
EARLY_FROM = -(-NA // COLS_PER_DEV)


def _params(*sem):
    return pltpu.CompilerParams(dimension_semantics=sem or None, vmem_limit_bytes=VMEM_LIMIT)


def _bdot(a, b):
    return jnp.dot(a.astype(BF16), b.astype(BF16), preferred_element_type=F32)


def _bdot_nt(a, b):
    return lax.dot_general(a.astype(BF16), b.astype(BF16), (((1,), (1,)), ((), ())), preferred_element_type=F32)


def _bdot_tn(a, b):
    return lax.dot_general(a.astype(BF16), b.astype(BF16), (((0,), (0,)), ((), ())), preferred_element_type=F32)


def _sigmoid(x):
    return 1.0 / (1.0 + jnp.exp(-x))


def _softplus(x):
    return jnp.maximum(x, 0.0) + jnp.log(1.0 + jnp.exp(-jnp.abs(x)))


def _heads(ref, col0):
    return jnp.stack([ref[:, col0 + N * h:col0 + N * (h + 1)] for h in range(H)])


def _lerp(c, s, mu):
    return c + (s - c) * mu


def _head_sums(x):
    low = lax.broadcasted_iota(jnp.int32, (x.shape[0], LANES), 1) < N
    out = []
    for p in range(x.shape[1] // LANES):
        pair = x[:, LANES * p:LANES * (p + 1)]
        first = jnp.sum(jnp.where(low, pair, 0.0), axis=-1, keepdims=True)
        second = jnp.sum(jnp.where(low, 0.0, pair), axis=-1, keepdims=True)
        out.append(jnp.where(low, first, second))
    return jnp.concatenate(out, axis=-1)


def _to_heads(x):
    return [x[:, N * h:N * (h + 1)] for h in range(H)]


def _from_heads(ref):
    return jnp.concatenate([ref[h] for h in range(H)], axis=-1)


def _rwkv_pre(rc, rs, kc, ks, vc, vs, gc, gs, wdc, wds, adc, ads,
              mu_r, mu_k, mu_v, mu_g, mu_wd, mu_ad, w_up, w0, a_up, a0, k_k, k_a):
    r = _lerp(rc, rs, mu_r)
    k = _lerp(kc, ks, mu_k)
    v = _lerp(vc, vs, mu_v)
    g = _lerp(gc, gs, mu_g)
    wd = _lerp(wdc, wds, mu_wd)
    ad = _lerp(adc, ads, mu_ad)
    t = wd.shape[0]
    w_raw = -_softplus(-(w0 + _bdot(jnp.tanh(wd), w_up))) - 0.5
    lw = -jnp.exp(w_raw)
    row = lax.broadcasted_iota(jnp.int32, (t, t), 0)
    col = lax.broadcasted_iota(jnp.int32, (t, t), 1)
    same_chunk = ((row >= col) & (row // WKV_CHUNK == col // WKV_CHUNK)).astype(F32)
    cl = jnp.dot(same_chunk, lw, precision=HI, preferred_element_type=F32)
    alr = _sigmoid(a0 + _bdot(ad, a_up))
    kk = k * k_k
    kk = kk / jnp.maximum(jnp.sqrt(_head_sums(kk * kk)), 1e-12)
    k2 = k * (1.0 + (alr - 1.0) * k_a)
    return r, lw, cl, k2, v, -kk, kk * alr, g


_MM_DIMS = {"nn": (((2,), (1,)), ((0,), (0,))), "nt": (((2,), (2,)), ((0,), (0,))), "tn": (((1,), (1,)), ((0,), (0,)))}


def _dot1(a, b, kind):
    return lax.dot_general(a.astype(BF16), b.astype(BF16), dimension_numbers=_MM_DIMS[kind], preferred_element_type=F32)


@functools.partial(jax.custom_vjp, nondiff_argnums=(2,))
def _mm(a, b, kind):
    return _dot1(a, b, kind)


def _mm_fwd(a, b, kind):
    return _dot1(a, b, kind), (a, b)


def _mm_bwd(kind, res, ct):
    a, b = res
    if kind == "nn":
        return _dot1(ct, b, "nt"), _dot1(a, ct, "tn")
    if kind == "nt":
        return _dot1(ct, b, "nn"), _dot1(ct, a, "tn")
    return _dot1(b, ct, "nt"), _dot1(a, ct, "nn")


_mm.defvjp(_mm_fwd, _mm_bwd)


def _chunk_masks(c):
    row = lax.broadcasted_iota(jnp.int32, (c, c), 0)
    col = lax.broadcasted_iota(jnp.int32, (c, c), 1)
    return (row >= col)[None], (row > col)[None], (row == col).astype(F32)[None]


def _wkv_aab(lw, cl, a, b):
    _, strict, _ = _chunk_masks(a.shape[1])
    return jnp.where(strict, _mm(a * jnp.exp(cl - lw), b * jnp.exp(-cl), "nt"), 0.0)


def _tri_inverse(x):
    c = x.shape[1]
    p = _chunk_masks(c)[2] + x
    for _ in range(int(math.log2(c)) - 1):
        x = _dot1(x, x, "nn")
        p = p + _dot1(p, x, "nn")
    return p


def _wkv_apply(s0, r, lw, cl, k, v, a, b, p):
    c = r.shape[1]
    incl, strict, _ = _chunk_masks(c)
    gi = jnp.exp(-cl)
    left = jnp.concatenate([a * jnp.exp(cl - lw), r * jnp.exp(cl)], axis=1)
    right = jnp.concatenate([b * gi, k * gi], axis=1)
    m = _mm(left, right, "nt")
    z0 = _mm(left, s0, "nt")
    a_ak = jnp.where(strict, m[:, :c, c:], 0.0)
    row = lax.broadcasted_iota(jnp.int32, (c, 2 * c), 0)
    col = lax.broadcasted_iota(jnp.int32, (c, 2 * c), 1)
    a_r = jnp.where((row >= col % c)[None], m[:, c:, :], 0.0)
    sa = _mm(p, z0[:, :c] + _mm(a_ak, v, "nn"), "nn")
    sa_v = jnp.concatenate([sa, v], axis=1)
    y = z0[:, c:] + _mm(a_r, sa_v, "nn")
    s1 = (s0 + _mm(sa_v, right, "tn")) * jnp.exp(cl[:, c - 1:c, :])
    return y, s1


def _rwkv_post(y, r, k2, v, g, lnx_w, lnx_b, r_k):
    yc = y - _head_sums(y) * (1.0 / N)
    var = _head_sums(yc * yc) * (1.0 / N)
    yn = yc * lax.rsqrt(var + LNX_EPS) * lnx_w + lnx_b
    bonus = _head_sums(r * k2 * r_k) * v
    return (yn + bonus) * (g * _sigmoid(g))


def _fox_pre(q, k, f, q_g, k_g, f_b):
    qn = q * lax.rsqrt(_head_sums(q * q) * (1.0 / N) + RMS_EPS) * q_g
    kn = k * lax.rsqrt(_head_sums(k * k) * (1.0 / N) + RMS_EPS) * k_g
    x = f + f_b
    return qn, kn, jnp.minimum(x, 0.0) - jnp.log(1.0 + jnp.exp(-jnp.abs(x)))


def _norm_proj(x, g, wts):
    s = x.shape[0]
    k = len(wts)

    def body(x_ref, g_ref, *refs):
        w_refs, h_ref, o_refs = refs[:k], refs[k], refs[k + 1:]
        xv = x_ref[...]
        h = (xv * lax.rsqrt(jnp.mean(xv * xv, axis=-1, keepdims=True) + RMS_EPS) * g_ref[...]).astype(BF16)
        h_ref[...] = h
        for w_ref, o_ref in zip(w_refs, o_refs):
            o_ref[...] = _bdot_nt(h, w_ref[...])

    tok = lambda n: pl.BlockSpec((TOK_TILE, n), lambda i: (i, 0))
    out = pl.pallas_call(
        body, name="norm_proj", grid=(s // TOK_TILE,),
        in_specs=[tok(D), pl.BlockSpec((1, D), lambda i: (0, 0))] + [pl.BlockSpec(w.shape, lambda i: (0, 0)) for w in wts],
        out_specs=[tok(D)] + [tok(w.shape[0]) for w in wts],
        out_shape=[jax.ShapeDtypeStruct((s, D), BF16)] + [jax.ShapeDtypeStruct((s, w.shape[0]), F32) for w in wts],
        compiler_params=_params("arbitrary"))(x, g, *wts)
    return out[0], out[1:]


def _proj_wgrad_early(h, dub, dug, duf, head_rows):
    s = dub.shape[0]
    steps = s // TOK_TILE
    seg_rows = (_WT_SEGMENTS[1], _WT_SEGMENTS[2], _WT_SEGMENTS[3])

    def body(h_ref, b_ref, g_ref, f_ref, o_ref, head_ref, *accs):
        @pl.when(pl.program_id(0) == 0)
        def _():
            for acc in accs:
                acc[...] = jnp.zeros_like(acc)

        h = h_ref[...]
        for acc, du_ref in zip(accs, (b_ref, g_ref, f_ref)):
            acc[...] += _bdot_tn(du_ref[...], h)

        @pl.when(pl.program_id(0) == steps - 1)
        def _():
            head_ref[...] = accs[0][:head_rows, :]
            for j in range(EARLY_FROM, N_DEV):
                lo, hi = COLS_PER_DEV * j, COLS_PER_DEV * (j + 1)
                parts = []
                for acc, (row, n) in sorted(zip(accs, seg_rows), key=lambda t: t[1][0]):
                    first, last = max(lo, row), min(hi, row + n)
                    if first < last:
                        parts.append(acc[first - row:last - row, :])
                o_ref[j - EARLY_FROM] = (parts[0] if len(parts) == 1 else jnp.concatenate(parts, axis=0)).astype(BF16)

    tok = lambda n: pl.BlockSpec((TOK_TILE, n), lambda i: (i, 0))
    n_early = N_DEV - EARLY_FROM
    return pl.pallas_call(
        body, name="wgrad_bgf", grid=(steps,), in_specs=[tok(D), tok(NB), tok(NG), tok(NF)],
        out_specs=[pl.BlockSpec((n_early, COLS_PER_DEV, D), lambda i: (0, 0, 0)),
                   pl.BlockSpec((head_rows, D), lambda i: (0, 0))],
        out_shape=[jax.ShapeDtypeStruct((n_early, COLS_PER_DEV, D), BF16), jax.ShapeDtypeStruct((head_rows, D), F32)],
        scratch_shapes=[pltpu.VMEM((n, D), F32) for n in (NB, NG, NF)],
        compiler_params=_params("arbitrary"))(h, dub, dug, duf)


def _proj_wgrad_late(h, dua, dwt_b_head):
    s = dua.shape[0]
    steps = s // TOK_TILE

    def body(h_ref, du_ref, b_ref, o_ref, acc):
        @pl.when(pl.program_id(0) == 0)
        def _():
            acc[...] = jnp.zeros_like(acc)

        acc[...] += _bdot_tn(du_ref[...], h_ref[...])

        @pl.when(pl.program_id(0) == steps - 1)
        def _():
            for j in range(EARLY_FROM):
                lo, hi = COLS_PER_DEV * j, COLS_PER_DEV * (j + 1)
                parts = [acc[lo:min(hi, NA), :]] + ([b_ref[:hi - NA, :]] if hi > NA else [])
                o_ref[j] = (parts[0] if len(parts) == 1 else jnp.concatenate(parts, axis=0)).astype(BF16)

    return pl.pallas_call(
        body, name="wgrad_a", grid=(steps,),
        in_specs=[pl.BlockSpec((TOK_TILE, D), lambda i: (i, 0)), pl.BlockSpec((TOK_TILE, NA), lambda i: (i, 0)),
                  pl.BlockSpec(dwt_b_head.shape, lambda i: (0, 0))],
        out_specs=pl.BlockSpec((EARLY_FROM, COLS_PER_DEV, D), lambda i: (0, 0, 0)),
        out_shape=jax.ShapeDtypeStruct((EARLY_FROM, COLS_PER_DEV, D), BF16),
        scratch_shapes=[pltpu.VMEM((NA, D), F32)], compiler_params=_params("arbitrary"))(h, dua, dwt_b_head)


def _proj_xgrad(x, g, dx2, dus, ws, slabs, owners):
    s = x.shape[0]
    tile = XGRAD_TILE
    k = len(dus)
    nx = len(slabs)
    n_in = 3 + 2 * k + nx

    def body(*refs):
        x_ref, g_ref, dx2_ref = refs[:3]
        du_refs, w_refs = refs[3:3 + k], refs[3 + k:3 + 2 * k]
        src_refs = refs[3 + 2 * k:3 + 2 * k + nx]
        dx_ref, dg_ref = refs[n_in:n_in + 2]
        dst_refs = refs[n_in + 2:n_in + 2 + nx]
        start, wait = _exchange_ops(src_refs, dst_refs, owners, refs[n_in + 2 + nx:])

        @pl.when(pl.program_id(0) == 0)
        def _():
            dg_ref[...] = jnp.zeros_like(dg_ref)
            start()

        dh = _bdot(du_refs[0][...], w_refs[0][...])
        for du_ref, w_ref in zip(du_refs[1:], w_refs[1:]):
            dh += _bdot(du_ref[...], w_ref[...])
        xv = x_ref[...]
        rs = lax.rsqrt(jnp.mean(xv * xv, axis=-1, keepdims=True) + RMS_EPS)
        xn = xv * rs
        dg_ref[...] += jnp.sum(dh * xn, axis=0, keepdims=True)
        dxn = dh * g_ref[...]
        dx_ref[...] = rs * (dxn - xn * jnp.mean(dxn * xn, axis=-1, keepdims=True)) + dx2_ref[...]

        @pl.when(pl.program_id(0) == s // tile - 1)
        def _():
            wait()

    tok = lambda n: pl.BlockSpec((tile, n), lambda i: (i, 0))
    fixed = lambda a: pl.BlockSpec(a.shape, lambda i: (0,) * a.ndim)
    out = pl.pallas_call(
        body, name="proj_xgrad", grid=(s // tile,),
        in_specs=([tok(D), fixed(g), tok(D)] + [tok(du.shape[1]) for du in dus] + [fixed(w) for w in ws]
                  + _hbm_specs(nx)),
        out_specs=[tok(D), pl.BlockSpec((1, D), lambda i: (0, 0))] + _hbm_specs(nx),
        out_shape=[jax.ShapeDtypeStruct((s, D), F32), jax.ShapeDtypeStruct((1, D), F32)] + _received_shapes(slabs, owners),
        scratch_shapes=_exchange_scratch(nx),
        compiler_params=_params("arbitrary"))(x, g, dx2, *dus, *ws, *slabs)
    return out[0], out[1], out[2:]


def _tail(x, target, ya, o, ub, ug, w_oa, w_ob, w_o, fg):
    s = x.shape[0]
    tile = TOK_TILE

    def body(x_ref, t_ref, ya_ref, o_ref, gb_ref, ug_ref, woa_ref, wob_ref, wo_ref, fg_ref,
             loss_ref, dfg_ref, dwo_ref, dwoa_ref, dwob_ref, dx2_ref, dya_ref, do_ref, dgb_ref, dug_ref):
        @pl.when(pl.program_id(0) == 0)
        def _():
            for r in (loss_ref, dfg_ref, dwo_ref, dwoa_ref, dwob_ref):
                r[...] = jnp.zeros_like(r)

        ya_v = ya_ref[...]
        gate_b = gb_ref[...]
        sg_b = _sigmoid(gate_b)
        silu_b = gate_b * sg_b
        o_v = jnp.concatenate([o_ref[h] for h in range(H)], axis=-1)
        yb_v = o_v * silu_b
        big_a = _bdot(ya_v, woa_ref[...])
        big_b = _bdot(yb_v, wob_ref[...])
        sa = _sigmoid(ug_ref[:, :D])
        sb = _sigmoid(ug_ref[:, D:])
        merged = sa * big_a + sb * big_b
        x2 = x_ref[...] + _bdot(merged, wo_ref[...])
        rs = lax.rsqrt(jnp.mean(x2 * x2, axis=-1, keepdims=True) + RMS_EPS)
        xn = x2 * rs
        err = xn * fg_ref[...] - t_ref[...]
        loss_ref[...] += (0.5 / D) * jnp.sum(err * err)
        dout = err * (1.0 / D)
        dfg_ref[...] += jnp.sum(dout * xn, axis=0, keepdims=True)
        dxn = dout * fg_ref[...]
        dx2 = rs * (dxn - xn * jnp.mean(dxn * xn, axis=-1, keepdims=True))
        dx2_ref[...] = dx2
        dwo_ref[...] += _bdot_tn(merged, dx2)
        dmerged = _bdot_nt(dx2, wo_ref[...])
        dbig_a = dmerged * sa
        dbig_b = dmerged * sb
        dug_ref[:, :D] = (dmerged * big_a * sa * (1.0 - sa)).astype(BF16)
        dug_ref[:, D:] = (dmerged * big_b * sb * (1.0 - sb)).astype(BF16)
        dwoa_ref[...] += _bdot_tn(ya_v, dbig_a)
        dwob_ref[...] += _bdot_tn(yb_v, dbig_b)
        dya_ref[...] = _bdot_nt(dbig_a, woa_ref[...])
        dyb = _bdot_nt(dbig_b, wob_ref[...])
        dgb_ref[...] = dyb * o_v * (sg_b * (1.0 + gate_b * (1.0 - sg_b)))
        _dov = dyb * silu_b
        for h in range(H):
            do_ref[h] = _dov[:, N * h:N * (h + 1)]

    tok = lambda n: pl.BlockSpec((tile, n), lambda i: (i, 0))
    hm = pl.BlockSpec((H, tile, N), lambda i: (0, i, 0))
    fixed = lambda shape: pl.BlockSpec(shape, lambda i: (0,) * len(shape))
    f32 = lambda *shape: jax.ShapeDtypeStruct(shape, F32)
    return pl.pallas_call(
        body, name="tail", grid=(s // tile,),
        in_specs=[tok(D), tok(D), tok(DA), hm, pl.BlockSpec((tile, DA), lambda i: (i, 3)), tok(NG),
                  fixed((DA, D)), fixed((DA, D)), fixed((D, D)), fixed((1, D))],
        out_specs=[fixed((1, 1)), fixed((1, D)), fixed((D, D)), fixed((DA, D)), fixed((DA, D)),
                   tok(D), tok(DA), hm, tok(DA), tok(NG)],
        out_shape=[f32(1, 1), f32(1, D), f32(D, D), f32(DA, D), f32(DA, D),
                   f32(s, D), f32(s, DA), f32(H, s, N), f32(s, DA), jax.ShapeDtypeStruct((s, NG), BF16)],
        compiler_params=_params("arbitrary"))(x, target, ya, o, ub, ug, w_oa, w_ob, w_o, fg)


def _pre_operands(ua_ref, prev_ref, first):
    cur = ua_ref[...]
    t = cur.shape[0]
    prev_row = jnp.where(first, 0.0, prev_ref[7:8, :])
    rows = lax.broadcasted_iota(jnp.int32, cur.shape, 0)
    sh = jnp.where(rows == 0, prev_row, pltpu.roll(cur, 1, axis=0))
    ops = []
    for c0, n in ((0, DA), (DA, DA), (2 * DA, DA), (3 * DA + 2 * RANK, DA), (3 * DA, RANK), (3 * DA + RANK, RANK)):
        ops += [cur[:, c0:c0 + n], sh[:, c0:c0 + n]]
    del t
    return ops


def _ua_specs(tile, order):
    blocks = tile // 8
    return [pl.BlockSpec((tile, NA), lambda i: (order(i), 0)),
            pl.BlockSpec((8, NA), lambda i: (jnp.maximum(order(i) * blocks - 1, 0), 0))]


def _rwkv_pre_fwd(ua, pre_params):
    s = ua.shape[0]
    tile = HEAD_TILE

    def body(ua_ref, prev_ref, *refs):
        p_refs, o_refs = refs[:len(pre_params)], refs[len(pre_params):]
        ops = _pre_operands(ua_ref, prev_ref, pl.program_id(0) == 0)
        outs = _rwkv_pre(*ops, *[p[...] for p in p_refs])
        for o_ref, val in zip(o_refs, outs):
            o_ref[...] = val

    tm = pl.BlockSpec((tile, DA), lambda i: (i, 0))
    return pl.pallas_call(
        body, name="rwkv_pre_fwd", grid=(s // tile,),
        in_specs=_ua_specs(tile, lambda i: i) + [pl.BlockSpec(p.shape, lambda i, nd=p.ndim: (0,) * nd) for p in pre_params],
        out_specs=[tm] * 8, out_shape=[jax.ShapeDtypeStruct((s, DA), F32)] * 8,
        compiler_params=_params("arbitrary"))(ua, ua, *pre_params)


def _rwkv_pre_bwd(ua, pre_params, cots):
    s = ua.shape[0]
    tile = HEAD_TILE
    nt = s // tile
    n_p = len(pre_params)

    def body(ua_ref, prev_ref, *refs):
        p_refs, c_refs = refs[:n_p], refs[n_p:n_p + 11]
        dua_ref = refs[n_p + 11]
        dp_refs = refs[n_p + 12:n_p + 12 + n_p]
        carry_ref = refs[-1]
        i = pl.program_id(0)

        @pl.when(i == 0)
        def _():
            carry_ref[...] = jnp.zeros_like(carry_ref)
            for r in dp_refs:
                r[...] = jnp.zeros_like(r)

        ops = _pre_operands(ua_ref, prev_ref, i == nt - 1)
        _, vjp = jax.vjp(_rwkv_pre, *ops, *[p[...] for p in p_refs])
        c = [r[...] for r in c_refs]
        grads = vjp((c[0] + c[1], c[2], c[3], c[4] + c[5], c[6] + c[7], c[8], c[9], c[10]))
        d_ops, d_par = grads[:12], grads[12:]
        for r, val in zip(dp_refs, d_par):
            r[...] += val
        d_cur = jnp.concatenate([d_ops[0], d_ops[2], d_ops[4], d_ops[8], d_ops[10], d_ops[6]], axis=-1)
        d_sh = jnp.concatenate([d_ops[1], d_ops[3], d_ops[5], d_ops[9], d_ops[11], d_ops[7]], axis=-1)
        rows = lax.broadcasted_iota(jnp.int32, d_sh.shape, 0)
        dua = d_cur + jnp.where(rows == tile - 1, carry_ref[...], pltpu.roll(d_sh, tile - 1, axis=0))
        dua_ref[...] = dua.astype(BF16)
        carry_ref[...] = d_sh[0:1, :]

    rev = lambda i: nt - 1 - i
    tm = pl.BlockSpec((tile, DA), lambda i: (rev(i), 0))
    fixed = [pl.BlockSpec(p.shape, lambda i, nd=p.ndim: (0,) * nd) for p in pre_params]
    return pl.pallas_call(
        body, name="rwkv_pre_bwd", grid=(nt,),
        in_specs=_ua_specs(tile, rev) + fixed + [tm] * 11,
        out_specs=[pl.BlockSpec((tile, NA), lambda i: (rev(i), 0))] + fixed,
        out_shape=[jax.ShapeDtypeStruct((s, NA), BF16)] + [jax.ShapeDtypeStruct(p.shape, F32) for p in pre_params],
        scratch_shapes=[pltpu.VMEM((1, NA), F32)],
        compiler_params=_params("arbitrary"))(ua, ua, *pre_params, *cots)


def _wkv_fwd(seq):
    s = seq[0].shape[0]
    nc = s // WKV_CHUNK

    def body(r_ref, lw_ref, cl_ref, k_ref, v_ref, a_ref, b_ref, y_ref, ck_ref, p_ref, state):
        @pl.when(pl.program_id(0) == 0)
        def _():
            state[...] = jnp.zeros_like(state)

        r, lw, cl, k, v, a, b = (jnp.stack(_to_heads(ref[...])) for ref in (r_ref, lw_ref, cl_ref, k_ref, v_ref, a_ref,
                                                                             b_ref))
        s0 = state[...]
        ck_ref[0] = s0
        p = _tri_inverse(_wkv_aab(lw, cl, a, b))
        p_ref[0] = p
        y, s1 = _wkv_apply(s0, r, lw, cl, k, v, a, b, p)
        y_ref[...] = jnp.concatenate([y[h] for h in range(H)], axis=-1)
        state[...] = s1

    tm = pl.BlockSpec((WKV_CHUNK, DA), lambda c: (c, 0))
    per_chunk = lambda m: pl.BlockSpec((1, H, m, m), lambda c: (c, 0, 0, 0))
    return pl.pallas_call(
        body, name="wkv_fwd", grid=(nc,), in_specs=[tm] * 7,
        out_specs=[tm, per_chunk(N), per_chunk(WKV_CHUNK)],
        out_shape=[jax.ShapeDtypeStruct((s, DA), F32), jax.ShapeDtypeStruct((nc, H, N, N), F32),
                   jax.ShapeDtypeStruct((nc, H, WKV_CHUNK, WKV_CHUNK), F32)],
        scratch_shapes=[pltpu.VMEM((H, N, N), F32)], compiler_params=_params("arbitrary"))(*seq)


def _wkv_bwd(seq, ckpt, pinv, dy, slabs, owners):
    s = seq[0].shape[0]
    nc = s // WKV_CHUNK
    nx = len(slabs)

    def body(r_ref, lw_ref, cl_ref, k_ref, v_ref, a_ref, b_ref, ck_ref, p_ref, dy_ref, *refs):
        src_refs, d_refs, dst_refs = refs[:nx], refs[nx:nx + 7], refs[nx + 7:2 * nx + 7]
        dstate = refs[2 * nx + 7]
        start, wait = _exchange_ops(src_refs, dst_refs, owners, refs[2 * nx + 8:])

        @pl.when(pl.program_id(0) == 0)
        def _():
            dstate[...] = jnp.zeros_like(dstate)
            start()

        p = p_ref[0]
        r, lw, cl, k, v, a, b, dy = (jnp.stack(_to_heads(ref[...])) for ref in (r_ref, lw_ref, cl_ref, k_ref, v_ref,
                                                                                 a_ref, b_ref, dy_ref))
        _, vjp = jax.vjp(_wkv_apply, ck_ref[0], r, lw, cl, k, v, a, b, p)
        ds0, dr, dlw, dcl, dk, dv, da, db, dp = vjp((dy, dstate[...]))
        dstate[...] = ds0
        _, vjp_x = jax.vjp(_wkv_aab, lw, cl, a, b)
        dlw2, dcl2, da2, db2 = vjp_x(_dot1(_dot1(p, dp, "tn"), p, "nt"))
        for d_ref, val in zip(d_refs, (dr, dlw + dlw2, dcl + dcl2, dk, dv, da + da2, db + db2)):
            d_ref[...] = jnp.concatenate([val[h] for h in range(H)], axis=-1)

        @pl.when(pl.program_id(0) == nc - 1)
        def _():
            wait()

    tm = pl.BlockSpec((WKV_CHUNK, DA), lambda c: (nc - 1 - c, 0))
    per_chunk = lambda m: pl.BlockSpec((1, H, m, m), lambda c: (nc - 1 - c, 0, 0, 0))
    out = pl.pallas_call(
        body, name="wkv_bwd", grid=(nc,),
        in_specs=[tm] * 7 + [per_chunk(N), per_chunk(WKV_CHUNK), tm] + _hbm_specs(nx),
        out_specs=[tm] * 7 + _hbm_specs(nx),
        out_shape=[jax.ShapeDtypeStruct((s, DA), F32)] * 7 + _received_shapes(slabs, owners),
        scratch_shapes=[pltpu.VMEM((H, N, N), F32)] + _exchange_scratch(nx),
        compiler_params=_params("arbitrary"))(*seq, ckpt, pinv, dy, *slabs)
    return out[:7], out[7:]


def _rwkv_post_fwd(y, r, k2, v, g, post_params):
    s = y.shape[0]
    tile = TOK_TILE

    def body(*refs):
        refs[-1][...] = _rwkv_post(*[ref[...] for ref in refs[:-1]])

    tm = pl.BlockSpec((tile, DA), lambda i: (i, 0))
    par = pl.BlockSpec((1, DA), lambda i: (0, 0))
    return pl.pallas_call(
        body, name="rwkv_post_fwd", grid=(s // tile,), in_specs=[tm] * 5 + [par] * 3,
        out_specs=tm, out_shape=jax.ShapeDtypeStruct((s, DA), F32),
        compiler_params=_params("arbitrary"))(y, r, k2, v, g, *post_params)


def _rwkv_post_bwd(y, r, k2, v, g, post_params, dya, slabs, lo):
    s = y.shape[0]
    tile = HEAD_TILE

    def body(y_ref, r_ref, k_ref, v_ref, g_ref, w_ref, b_ref, rk_ref, dya_ref, s_ref, *refs):
        d_refs, p_ref = refs[:8], refs[8]
        start, wait = _pair_swap_ops(s_ref, p_ref, lo, refs[9:])

        @pl.when(pl.program_id(0) == 0)
        def _():
            for ref in d_refs[5:]:
                ref[...] = jnp.zeros_like(ref)
            start()

        _, vjp = jax.vjp(_rwkv_post, *[ref[...] for ref in (y_ref, r_ref, k_ref, v_ref, g_ref, w_ref, b_ref, rk_ref)])
        grads = vjp(dya_ref[...])
        for ref, val in zip(d_refs[:5], grads[:5]):
            ref[...] = val
        for ref, val in zip(d_refs[5:], grads[5:]):
            ref[...] += val

        @pl.when(pl.program_id(0) == s // tile - 1)
        def _():
            wait()

    tm = pl.BlockSpec((tile, DA), lambda i: (i, 0))
    par = pl.BlockSpec((1, DA), lambda i: (0, 0))
    return pl.pallas_call(
        body, name="rwkv_post_bwd", grid=(s // tile,),
        in_specs=[tm] * 5 + [par] * 3 + [tm] + _hbm_specs(1),
        out_specs=[tm] * 5 + [par] * 3 + _hbm_specs(1),
        out_shape=[jax.ShapeDtypeStruct((s, DA), F32)] * 5 + [jax.ShapeDtypeStruct((1, DA), F32)] * 3
        + [jax.ShapeDtypeStruct(slabs.shape, slabs.dtype)],
        scratch_shapes=_pair_swap_scratch(slabs.shape[0]),
        compiler_params=_params("arbitrary"))(y, r, k2, v, g, *post_params, dya, slabs)


def _tri(t):
    return (lax.broadcasted_iota(jnp.int32, (t, t), 0) >= lax.broadcasted_iota(jnp.int32, (t, t), 1)).astype(F32)


def _fox_pre_fwd(ub, uf, q_g, k_g, f_b):
    s = ub.shape[0]
    tile = HEAD_TILE

    def body(ub_ref, uf_ref, qg_ref, kg_ref, fb_ref, q_ref, k_ref, v_ref, cum_ref, carry):
        @pl.when(pl.program_id(0) == 0)
        def _():
            carry[...] = jnp.zeros_like(carry)

        qn, kn, logf = _fox_pre(ub_ref[:, :DA], ub_ref[:, DA:2 * DA], uf_ref[...], qg_ref[...], kg_ref[...],
                                fb_ref[...])
        for h, (q_col, k_col) in enumerate(zip(_to_heads(qn), _to_heads(kn))):
            q_ref[h] = q_col
            k_ref[h] = k_col
        v_ref[...] = _heads(ub_ref, 2 * DA)
        cum = jnp.dot(_tri(tile), logf, precision=HI, preferred_element_type=F32) + carry[...]
        cum_ref[...] = cum
        carry[...] = cum[tile - 1:tile, :]

    hm = pl.BlockSpec((H, tile, N), lambda i: (0, i, 0))
    fixed = lambda shape: pl.BlockSpec(shape, lambda i: (0,) * len(shape))
    return pl.pallas_call(
        body, name="fox_pre_fwd", grid=(s // tile,),
        in_specs=[pl.BlockSpec((tile, NB), lambda i: (i, 0)), pl.BlockSpec((tile, NF), lambda i: (i, 0)),
                  fixed((1, DA)), fixed((1, DA)), fixed((1, NF))],
        out_specs=[hm] * 3 + [pl.BlockSpec((tile, NF), lambda i: (i, 0))],
        out_shape=[jax.ShapeDtypeStruct((H, s, N), F32)] * 3 + [jax.ShapeDtypeStruct((s, NF), F32)],
        scratch_shapes=[pltpu.VMEM((1, NF), F32)], compiler_params=_params("arbitrary"))(ub, uf, q_g, k_g, f_b)


def _fox_pre_bwd(ub, uf, q_g, k_g, f_b, dqn, dkn, dvf, dgate, dcum_q, dcum_k):
    s = ub.shape[0]
    tile = HEAD_TILE
    nt = s // tile

    def body(ub_ref, uf_ref, qg_ref, kg_ref, fb_ref, dq_ref, dk_ref, dv_ref, dgate_ref, dcq_ref, dck_ref,
             dub_ref, duf_ref, dqg_ref, dkg_ref, dfb_ref, carry):
        @pl.when(pl.program_id(0) == 0)
        def _():
            carry[...] = jnp.zeros_like(carry)
            for ref in (dqg_ref, dkg_ref, dfb_ref):
                ref[...] = jnp.zeros_like(ref)

        dcum = dcq_ref[...] + dck_ref[...]
        dlogf = lax.dot_general(_tri(tile), dcum, (((0,), (0,)), ((), ())), precision=HI,
                                preferred_element_type=F32) + carry[...]
        carry[...] = dlogf[0:1, :]
        _, vjp = jax.vjp(_fox_pre, ub_ref[:, :DA], ub_ref[:, DA:2 * DA], uf_ref[...], qg_ref[...], kg_ref[...],
                         fb_ref[...])
        d_q, d_k, d_f, d_qg, d_kg, d_fb = vjp((_from_heads(dq_ref), _from_heads(dk_ref), dlogf))
        dub_ref[...] = jnp.concatenate([d_q, d_k, _from_heads(dv_ref), dgate_ref[...]], axis=-1).astype(BF16)
        duf_ref[...] = d_f.astype(BF16)
        dqg_ref[...] += functools.reduce(jnp.add, _to_heads(d_qg))
        dkg_ref[...] += functools.reduce(jnp.add, _to_heads(d_kg))
        dfb_ref[...] += d_fb

    rev = lambda i: nt - 1 - i
    hm = pl.BlockSpec((H, tile, N), lambda i: (0, rev(i), 0))
    tok = lambda n: pl.BlockSpec((tile, n), lambda i: (rev(i), 0))
    fixed = lambda shape: pl.BlockSpec(shape, lambda i: (0,) * len(shape))
    return pl.pallas_call(
        body, name="fox_pre_bwd", grid=(nt,),
        in_specs=[tok(NB), tok(NF), fixed((1, DA)), fixed((1, DA)), fixed((1, NF)), hm, hm, hm, tok(DA), tok(NF),
                  tok(NF)],
        out_specs=[tok(NB), tok(NF), fixed((1, N)), fixed((1, N)), fixed((1, NF))],
        out_shape=[jax.ShapeDtypeStruct((s, NB), BF16), jax.ShapeDtypeStruct((s, NF), BF16),
                   jax.ShapeDtypeStruct((1, N), F32), jax.ShapeDtypeStruct((1, N), F32),
                   jax.ShapeDtypeStruct((1, NF), F32)],
        scratch_shapes=[pltpu.VMEM((1, NF), F32)],
        compiler_params=_params("arbitrary"))(ub, uf, q_g, k_g, f_b, dqn, dkn, dvf, dgate, dcum_q, dcum_k)


def _att_groups(s):
    blocks = s // ATT_TILE
    per = max(1, blocks // ATT_GROUPS)
    return per, blocks // per


def _att_parts(n, width):
    return ([(0, n - width, False)] if n > width else []) + [(n - width, n, True)]


def _att_scores(q_bf, k_ref, ck_ref, lo, hi, masked, row_offset):
    scores = _bdot_nt(q_bf, k_ref[0, lo:hi, :]) - ck_ref[0, :, lo:hi]
    if masked:
        rows = row_offset + lax.broadcasted_iota(jnp.int32, scores.shape, 0)
        scores = jnp.where(rows >= lax.broadcasted_iota(jnp.int32, scores.shape, 1), scores, -1e30)
    return scores


def _fox_attn_fwd(q, k, v, cum_q, cum_k):
    s = q.shape[1]
    t = ATT_TILE
    per, groups = _att_groups(s)

    def body(q_ref, k_ref, v_ref, cq_ref, ck_ref, o_ref, lse_ref):
        qi = pl.program_id(1)
        for g in range(groups):
            @pl.when(qi // per == g)
            def _(g=g):
                q_bf = (q_ref[0] * ATT_SCALE).astype(BF16)
                parts = _att_parts((g + 1) * per * t, per * t)
                scores = [_att_scores(q_bf, k_ref, ck_ref, lo, hi, masked, (qi - g * per) * t)
                          for lo, hi, masked in parts]
                m = functools.reduce(jnp.maximum, [jnp.max(sc, axis=-1, keepdims=True) for sc in scores])
                l, acc = 0.0, 0.0
                for sc, (lo, hi, _) in zip(scores, parts):
                    p = jnp.exp(sc - m)
                    l += jnp.sum(p, axis=-1, keepdims=True)
                    acc += _bdot(p, v_ref[0, lo:hi, :])
                o_ref[0] = acc / l
                lse_ref[0] = m + jnp.log(l) + cq_ref[0]

    qb = pl.BlockSpec((1, t, N), lambda h, i: (h, i, 0))
    kb = pl.BlockSpec((1, s, N), lambda h, i: (h, 0, 0))
    return pl.pallas_call(
        body, name="fox_attn_fwd", grid=(H, s // t),
        in_specs=[qb, kb, kb, pl.BlockSpec((1, t, 1), lambda h, i: (h, i, 0)),
                  pl.BlockSpec((1, 1, s), lambda h, i: (h, 0, 0))],
        out_specs=[qb, pl.BlockSpec((1, t, 1), lambda h, i: (h, i, 0))],
        out_shape=[jax.ShapeDtypeStruct((H, s, N), F32), jax.ShapeDtypeStruct((H, s, 1), F32)],
        compiler_params=_params("arbitrary", "arbitrary"))(q, k, v, cum_q, cum_k)


def _fox_attn_bwd(q, k, v, cum_q, cum_k, o, lse, do, slabs, owners):
    s = q.shape[1]
    t = ATT_TILE
    per, groups = _att_groups(s)
    nx = len(slabs)

    def body(q_ref, k_ref, v_ref, cq_ref, ck_ref, o_ref, lse_ref, do_ref, *refs):
        src_refs, (dq_ref, dk_ref, dv_ref, dcq_ref, dck_ref) = refs[:nx], refs[nx:nx + 5]
        start, wait = _exchange_ops(src_refs, refs[nx + 5:2 * nx + 5], owners, refs[2 * nx + 5:])
        qi = pl.program_id(1)

        @pl.when((pl.program_id(0) == 0) & (qi == 0))
        def _():
            start()

        @pl.when(qi == 0)
        def _():
            for ref in (dk_ref, dv_ref, dck_ref):
                ref[...] = jnp.zeros_like(ref)

        for g in range(groups):
            @pl.when(qi // per == g)
            def _(g=g):
                q_bf, do_bf = (q_ref[0] * ATT_SCALE).astype(BF16), do_ref[0].astype(BF16)
                row_term = cq_ref[0] - lse_ref[0]
                delta = jnp.sum(do_ref[0] * o_ref[0], axis=-1, keepdims=True)
                dq, dcq = 0.0, 0.0
                for lo, hi, masked in _att_parts((g + 1) * per * t, per * t):
                    p = jnp.exp(_att_scores(q_bf, k_ref, ck_ref, lo, hi, masked, (qi - g * per) * t) + row_term)
                    ds = p * (_bdot_nt(do_bf, v_ref[0, lo:hi, :]) - delta)
                    dq += _bdot(ds, k_ref[0, lo:hi, :])
                    dcq += jnp.sum(ds, axis=-1, keepdims=True)
                    dk_ref[0, lo:hi, :] += _bdot_tn(ds, q_bf)
                    dv_ref[0, lo:hi, :] += _bdot_tn(p, do_bf)
                    dck_ref[0, :, lo:hi] -= jnp.sum(ds, axis=0, keepdims=True)
                dq_ref[0] = dq * ATT_SCALE
                dcq_ref[0] = dcq

        @pl.when((pl.program_id(0) == H - 1) & (qi == s // t - 1))
        def _():
            wait()

    qb = pl.BlockSpec((1, t, N), lambda h, i: (h, i, 0))
    kb = pl.BlockSpec((1, s, N), lambda h, i: (h, 0, 0))
    cqb = pl.BlockSpec((1, t, 1), lambda h, i: (h, i, 0))
    ckb = pl.BlockSpec((1, 1, s), lambda h, i: (h, 0, 0))
    f32 = lambda *shape: jax.ShapeDtypeStruct(shape, F32)
    out = pl.pallas_call(
        body, name="fox_attn_bwd", grid=(H, s // t),
        in_specs=[qb, kb, kb, cqb, ckb, qb, cqb, qb] + _hbm_specs(nx), out_specs=[qb, kb, kb, cqb, ckb] + _hbm_specs(nx),
        out_shape=[f32(H, s, N), f32(H, s, N), f32(H, s, N), f32(H, s, 1), f32(H, 1, s)]
        + _received_shapes(slabs, owners),
        scratch_shapes=_exchange_scratch(nx),
        compiler_params=_params("arbitrary", "arbitrary"))(q, k, v, cum_q, cum_k, o, lse, do, *slabs)
    return out[:5], out[5:]


def _local_step(x, target, w, p):
    mu = p["shift_mu"]
    lora_matrix = lambda a: jnp.moveaxis(a, 0, 1).reshape(RANK, DA).astype(F32)
    pre_params = (mu[:, 0:DA], mu[:, DA:2 * DA], mu[:, 2 * DA:3 * DA], mu[:, 3 * DA + 2 * RANK:],
                  mu[:, 3 * DA:3 * DA + RANK], mu[:, 3 * DA + RANK:3 * DA + 2 * RANK],
                  lora_matrix(w["w_lora_up"]), p["w0"], lora_matrix(w["a_lora_up"]), p["a0"], p["k_k"], p["k_a"])
    post_params = (p["lnx_w"], p["lnx_b"], p["r_k"])
    q_g, k_g = jnp.tile(p["q_norm_g"], (1, H)), jnp.tile(p["k_norm_g"], (1, H))
    f_b = jnp.pad(p["f_bias"], ((0, 0), (0, NF - H)))
    fg = p["final_norm_g"].reshape(1, D)

    h, (ua, ub, ug, uf) = _norm_proj(x, p["norm_g"], (w["in_a"], w["in_b"], w["in_g"], w["in_f"]))
    r, lw, cl, k2, v, av, bv, gg = _rwkv_pre_fwd(ua, pre_params)
    y, ckpt, pinv = _wkv_fwd((r, lw, cl, k2, v, av, bv))
    ya = _rwkv_post_fwd(y, r, k2, v, gg, post_params)
    qn, kn, vf, cum = _fox_pre_fwd(ub, uf, q_g, k_g, f_b)
    cum_t = cum[:, :H].T
    cum_q, cum_k = cum_t[:, :, None], cum_t[:, None, :]
    o, lse = _fox_attn_fwd(qn, kn, vf, cum_q, cum_k)

    (loss, dfg, dwo, dwoa, dwob, dx2, dya, do, dgate_b, dug) = _tail(
        x, target, ya, o, ub, ug, w["w_out_a"], w["w_out_b"], w["w_out"], fg)
    everyone = (0, N_DEV)
    (dqn, dkn, dvf, dcq, dck), (recv_woa, recv_wob, recv_wo) = _fox_attn_bwd(
        qn, kn, vf, cum_q, cum_k, o, lse, do,
        (_col_slabs(dwoa), _col_slabs(dwob), dwo.astype(BF16).reshape(N_DEV, D // N_DEV, D)), (everyone,) * 3)
    pad_f = lambda a: jnp.pad(a.T, ((0, 0), (0, NF - H)))
    dub, duf, dqg, dkg, dfb = _fox_pre_bwd(ub, uf, q_g, k_g, f_b, dqn, dkn, dvf, dgate_b,
                                           pad_f(dcq[:, :, 0]), pad_f(dck.reshape(H, -1)))
    spill = EARLY_FROM * COLS_PER_DEV - NA
    early, dwt_b_head = _proj_wgrad_early(h, dub, dug, duf, -(-spill // 8) * 8)
    dy, dr_p, dk_p, dv_p, dgg, dlnw, dlnb, drk, handed = _rwkv_post_bwd(y, r, k2, v, gg, post_params, dya, early,
                                                                          EARLY_FROM)
    early = _chip_sums(early, handed, EARLY_FROM, "chip_sums_early")
    (dr_s, dlw, dcl, dk_s, dv_s, dav, dbv), (recv_early,) = _wkv_bwd(
        (r, lw, cl, k2, v, av, bv), ckpt, pinv, dy, (early,), ((EARLY_FROM, N_DEV, "chips"),))
    pre_out = _rwkv_pre_bwd(ua, pre_params, (dr_s, dr_p, dlw, dcl, dk_s, dk_p, dv_s, dv_p, dav, dbv, dgg))
    dua, dpre = pre_out[0], pre_out[1:]
    late = _proj_wgrad_late(h, dua, dwt_b_head)

    flat = lambda a: a.reshape(1, -1)
    small = {
        "final_norm_g": dfg, "w0": dpre[7], "a0": dpre[9], "k_k": dpre[10], "k_a": dpre[11], "r_k": drk, "lnx_w": dlnw,
        "lnx_b": dlnb, "q_norm_g": dqg, "k_norm_g": dkg, "f_bias": dfb[:, :H],
        "shift_mu": jnp.concatenate([flat(dpre[0]), flat(dpre[1]), flat(dpre[2]), dpre[4], dpre[5], flat(dpre[3])], axis=1),
    }
    late = _chip_sums(late, _pair_swap(late, 0, "pair_swap_late"), 0, "chip_sums_late")
    by_head = lambda a: jnp.moveaxis(a.reshape(RANK, H, N), 1, 0)
    loras = jnp.stack([by_head(dpre[6]), by_head(dpre[8])], axis=1).astype(BF16)
    dx, dng, (recv_late, recv_lora, recv_small) = _proj_xgrad(
        x, p["norm_g"], dx2, (dua, dub, dug, duf), (w["in_a"], w["in_b"], w["in_g"], w["in_f"]),
        (late, loras, _pack_small(small, loss)), ((0, EARLY_FROM, "chips"), everyone, everyone))
    return dx, dng, (recv_early, recv_late), (recv_woa, recv_wob, recv_wo, recv_lora), recv_small


def _position():
    return lax.axis_index("x"), lax.axis_index("y"), lax.axis_index("c")


def _hbm_specs(n):
    return [pl.BlockSpec(memory_space=pl.ANY)] * n


BIG_GATHER_COPIES = 13
GATHER_ROW_CUT = 400


def _all_gather(big, blocks, name):
    n = len(blocks)

    def body(*refs):
        big_ref, x_refs = refs[0], refs[1:1 + n]
        big_out, out_refs = refs[1 + n], refs[2 + n:2 + 2 * n]
        send_sems, recv_sems, local_sems = refs[2 + 2 * n:]
        x, y, c = _position()
        me, sibling = (x, y, c), (x, y, 1 - c)
        chips = [(1 - x, y), (x, 1 - y), (1 - x, 1 - y)]
        x_nbr, y_nbr, diag = chips
        rows = big_ref.shape[0]
        cut = GATHER_ROW_CUT

        def part(ref, h):
            return ref if h is None else ref.at[pl.ds(0, cut)] if h == 0 else ref.at[pl.ds(cut, rows - cut)]

        def landed(chip, core, h):
            return part(big_out.at[4 * chip[0] + 2 * chip[1] + core], h)

        def big_copy(k, src, dst, to):
            return pltpu.make_async_remote_copy(src_ref=src, dst_ref=dst, send_sem=send_sems.at[7 * n + k],
                                                recv_sem=recv_sems.at[7 * n + k], device_id=to, device_id_type=MESH)

        def arrival(k, chip, core, h):
            dst = landed(chip, core, h)
            return big_copy(k, dst, dst, me)

        def pass_on(k, chip, h, to):
            src = landed(chip, c, h)
            return big_copy(k, src, src, to)

        big_mine = pltpu.make_async_copy(big_ref, landed((x, y), c, None), local_sems.at[n])
        big_mine.start()
        here = (x, y)
        big_sent = [big_copy(0, big_ref, landed(here, c, None), sibling),
                    big_copy(1, part(big_ref, 0), landed(here, c, 0), (*x_nbr, c)),
                    big_copy(2, part(big_ref, 1), landed(here, c, 1), (*y_nbr, c)),
                    big_copy(3, part(big_ref, 1), landed(here, c, 1), (*x_nbr, c)),
                    big_copy(4, part(big_ref, 0), landed(here, c, 0), (*y_nbr, c))]
        for cp in big_sent:
            cp.start()

        def copy(a, k, blk, to, own=False):
            dst = out_refs[a].at[4 * blk[0] + 2 * blk[1] + blk[2]]
            return pltpu.make_async_remote_copy(
                src_ref=x_refs[a] if own else dst, dst_ref=dst, send_sem=send_sems.at[7 * a + k],
                recv_sem=recv_sems.at[7 * a + k], device_id=to, device_id_type=MESH)

        mine = [pltpu.make_async_copy(x_refs[a], out_refs[a].at[4 * x + 2 * y + c], local_sems.at[a]) for a in range(n)]
        for cp in mine:
            cp.start()
        first = []
        for a in range(n):
            first.append(copy(a, 0, me, sibling, own=True))
            first += [copy(a, 1 + j, me, (*chip, c), own=True) for j, chip in enumerate(chips)]
        for cp in first:
            cp.start()

        big_steps = [(1, x_nbr, 0, (*y_nbr, c), 5, 7), (2, y_nbr, 1, (*x_nbr, c), 6, 8), (3, x_nbr, 1, None, None, 9),
                     (4, y_nbr, 0, None, None, 10), (5, diag, 0, None, None, 11), (6, diag, 1, None, None, 12)]
        for k, chip, h, onward, k_onward, k_sibling in big_steps:
            arrival(k, chip, c, h).wait_recv()
            if onward is not None:
                big_sent.append(pass_on(k_onward, chip, h, onward))
                big_sent[-1].start()
            big_sent.append(pass_on(k_sibling, chip, h, sibling))
            big_sent[-1].start()

        passed = []
        for j, chip in enumerate(chips):
            for a in range(n):
                copy(a, 1 + j, (*chip, c), me).wait_recv()
                passed.append(copy(a, 4 + j, (*chip, c), sibling))
                passed[-1].start()
        for a in range(n):
            copy(a, 0, sibling, me).wait_recv()
        for j, chip in enumerate(chips):
            for a in range(n):
                copy(a, 4 + j, (*chip, 1 - c), me).wait_recv()
        arrival(0, here, 1 - c, None).wait_recv()
        for k, chip, h, _, _, k_sibling in big_steps:
            arrival(k_sibling, chip, 1 - c, h).wait_recv()
        for cp in first + passed + big_sent:
            cp.wait_send()
        for cp in mine + [big_mine]:
            cp.wait()

    everything = [big] + list(blocks)
    return pl.pallas_call(
        body, name=name, out_shape=[jax.ShapeDtypeStruct((N_DEV,) + b.shape, b.dtype) for b in everything],
        in_specs=_hbm_specs(n + 1), out_specs=_hbm_specs(n + 1),
        scratch_shapes=[pltpu.SemaphoreType.DMA((7 * n + BIG_GATHER_COPIES,)),
                        pltpu.SemaphoreType.DMA((7 * n + BIG_GATHER_COPIES,)), pltpu.SemaphoreType.DMA((n + 1,))],
    )(*everything)


def _received_shapes(slabs, owners):
    return [jax.ShapeDtypeStruct((N_DEV // 2 if len(o) == 3 else N_DEV,) + s.shape[1:], s.dtype)
            for s, o in zip(slabs, owners)]


def _pair_swap_scratch(n):
    return [pltpu.SemaphoreType.DMA((n,)), pltpu.SemaphoreType.DMA((n,))]


def _pair_swap_ops(s_ref, p_ref, lo, sems):
    send_sems, recv_sems = sems
    n = s_ref.shape[0]

    def run(sending):
        x, y, c = _position()
        for side in (0, 1):
            mine = [pltpu.make_async_remote_copy(src_ref=s_ref.at[i], dst_ref=p_ref.at[i], send_sem=send_sems.at[i],
                                                 recv_sem=recv_sems.at[i], device_id=(x, y, 1 - c), device_id_type=MESH)
                    for i in range(n) if (lo + i) % 2 == side]

            @pl.when(c != side)
            def _():
                for cp in mine:
                    cp.start() if sending else cp.wait_send()

            if not sending:
                @pl.when(c == side)
                def _():
                    for cp in mine:
                        cp.wait_recv()

    return functools.partial(run, True), functools.partial(run, False)


def _pair_swap(slabs, lo, name):
    n = slabs.shape[0]

    def body(s_ref, p_ref, *sems):
        start, wait = _pair_swap_ops(s_ref, p_ref, lo, sems)
        start()
        wait()

    return pl.pallas_call(
        body, name=name, out_shape=jax.ShapeDtypeStruct(slabs.shape, slabs.dtype),
        in_specs=_hbm_specs(1), out_specs=_hbm_specs(1)[0], scratch_shapes=_pair_swap_scratch(n))(slabs)


def _chip_sums(slabs, swapped, lo, name):
    n, rows, cols = slabs.shape
    tile = W_IN_COL_TILE

    def body(s_ref, p_ref, o_ref):
        c = lax.axis_index("c")
        for i in range(n):
            @pl.when(c == (lo + i) % 2)
            def _(i=i):
                o_ref[i] = (s_ref[i].astype(F32) + p_ref[i].astype(F32)).astype(BF16)

    blk = pl.BlockSpec((n, rows, tile), lambda j: (0, 0, j))
    return pl.pallas_call(
        body, name=name, grid=(cols // tile,), in_specs=[blk, blk], out_specs=blk,
        out_shape=jax.ShapeDtypeStruct(slabs.shape, BF16), compiler_params=_params("arbitrary"))(slabs, swapped)


def _exchange_scratch(n):
    return [pltpu.SemaphoreType.DMA((7 * n,)), pltpu.SemaphoreType.DMA((7 * n,)), pltpu.SemaphoreType.DMA((n,))]


def _exchange_ops(src_refs, dst_refs, owners, sems):
    send_sems, recv_sems, local_sems = sems
    n = len(src_refs)

    def guarded(a, dev, fn):
        lo, hi = owners[a][:2]
        if (lo, hi) == (0, N_DEV):
            fn()
        else:
            pl.when((dev >= lo) & (dev < hi))(fn)

    def src(a, dev):
        ref = src_refs[a]
        return ref.at[0] if ref.shape[0] == 1 else ref.at[dev - owners[a][0]]

    def run(sending, waiting):
        x, y, c = _position()
        me = 4 * x + 2 * y + c
        for a in range(n):
            by_chip = len(owners[a]) == 3
            slot = (lambda qx, qy, qc: 2 * qx + qy) if by_chip else (lambda qx, qy, qc: 4 * qx + 2 * qy + qc)
            mine = slot(x, y, c)
            local = lambda a=a, mine=mine: pltpu.make_async_copy(src(a, me), dst_refs[a].at[mine], local_sems.at[a])
            if sending:
                guarded(a, me, lambda local=local: local().start())
            for m in range(2, N_DEV, 2) if by_chip else range(1, N_DEV):
                px, py, pc = x ^ (m >> 2), y ^ ((m >> 1) & 1), c ^ (m & 1)
                peer = 4 * px + 2 * py + pc
                theirs = slot(px, py, pc)
                sem = dict(send_sem=send_sems.at[7 * a + m - 1], recv_sem=recv_sems.at[7 * a + m - 1],
                           device_id=(px, py, pc), device_id_type=MESH)
                send = lambda a=a, peer=peer, sem=sem, mine=mine: pltpu.make_async_remote_copy(
                    src_ref=src(a, peer), dst_ref=dst_refs[a].at[mine], **sem)
                recv = lambda a=a, sem=sem, theirs=theirs: pltpu.make_async_remote_copy(
                    src_ref=src(a, me), dst_ref=dst_refs[a].at[theirs], **sem)
                if sending:
                    guarded(a, peer, lambda send=send: send().start())
                if waiting:
                    guarded(a, me, lambda recv=recv: recv().wait_recv())
                    guarded(a, peer, lambda send=send: send().wait_send())
            if waiting:
                guarded(a, me, lambda local=local: local().wait())

    return functools.partial(run, True, False), functools.partial(run, False, True)


def _sum_slabs(r_ref):
    g = r_ref[0].astype(F32)
    for k in range(1, r_ref.shape[0]):
        g = g + r_ref[k].astype(F32)
    return g


def _adamw(g, w, m, v):
    m_new = ADAM_B1 * m + (1.0 - ADAM_B1) * g
    v_new = ADAM_B2 * v + (1.0 - ADAM_B2) * (g * g)
    m_hat = m_new / (1.0 - ADAM_B1 ** ADAM_STEP)
    v_hat = v_new / (1.0 - ADAM_B2 ** ADAM_STEP)
    return g, -ADAM_LR * (m_hat / (jnp.sqrt(v_hat) + ADAM_EPS) + ADAM_WD * w), m_new, v_new


def _adamw_w_in(recv_early, recv_late, w, m, v, slabs, owners):
    rows, cols = w.shape
    tile = W_IN_COL_TILE
    nx = len(slabs)

    def body(early_ref, late_ref, w_ref, m_ref, v_ref, *refs):
        src_refs, o_refs, dst_refs = refs[:nx], refs[nx:nx + 4], refs[nx + 4:2 * nx + 4]
        start, wait = _exchange_ops(src_refs, dst_refs, owners, refs[2 * nx + 4:])
        x, y, c = _position()
        early_owner = 4 * x + 2 * y + c >= EARLY_FROM

        @pl.when(pl.program_id(0) == 0)
        def _():
            start()

        def update(g):
            for o_ref, val in zip(o_refs, _adamw(g, w_ref[...], m_ref[...], v_ref[...])):
                o_ref[...] = val

        pl.when(early_owner)(lambda: update(_sum_slabs(early_ref)))
        pl.when(jnp.logical_not(early_owner))(lambda: update(_sum_slabs(late_ref)))

        @pl.when(pl.program_id(0) == cols // tile - 1)
        def _():
            wait()

    blk = pl.BlockSpec((rows, tile), lambda i: (0, i))
    slots = lambda r: pl.BlockSpec((r.shape[0], rows, tile), lambda i: (0, 0, i))
    out = pl.pallas_call(
        body, name="adamw_w_in", grid=(cols // tile,),
        in_specs=[slots(recv_early), slots(recv_late), blk, blk, blk] + _hbm_specs(nx),
        out_specs=[blk] * 4 + _hbm_specs(nx),
        out_shape=[jax.ShapeDtypeStruct((rows, cols), F32)] * 4 + _received_shapes(slabs, owners),
        scratch_shapes=_exchange_scratch(nx),
        compiler_params=_params("arbitrary"))(recv_early, recv_late, w, m, v, *slabs)
    return out[:4], out[4:]


def _adamw_misc(recvs, recv_small, recv_norm, params):
    names = list(params)
    flat = [a for n in names for a in params[n]]

    def body(woa_ref, wob_ref, wo_ref, lora_ref, small_ref, norm_ref, *refs):
        p_refs, o_refs = refs[:len(flat)], refs[len(flat):]
        g_small = _sum_slabs(small_ref)
        g_lora = _sum_slabs(lora_ref)
        grads = {"w_out_a": _sum_slabs(woa_ref), "w_out_b": _sum_slabs(wob_ref), "w_out": _sum_slabs(wo_ref),
                 "w_lora_up": g_lora[0], "a_lora_up": g_lora[1], "norm_g": _sum_slabs(norm_ref)}
        for n, (off, size) in SMALL_SLOTS.items():
            grads[n] = g_small[:, off:off + size]
        for i, n in enumerate(names):
            w_ref, m_ref, v_ref = p_refs[3 * i:3 * i + 3]
            for o_ref, val in zip(o_refs[4 * i:4 * i + 4], _adamw(grads[n], w_ref[...], m_ref[...], v_ref[...])):
                o_ref[...] = val
        o_refs[-1][...] = g_small[:, LOSS_SLOT:LOSS_SLOT + 1]

    out = pl.pallas_call(
        body, name="adamw_misc",
        out_shape=[jax.ShapeDtypeStruct(params[n][0].shape, F32) for n in names for _ in range(4)]
        + [jax.ShapeDtypeStruct((1, 1), F32)],
        compiler_params=_params())(*recvs, recv_small, recv_norm, *flat)
    return {n: out[4 * i:4 * i + 4] for i, n in enumerate(names)}, out[-1]


_WT_SEGMENTS = ((0, NA), (NA, NB), (NA + NB + H, NG), (NA + NB, H))


def _split_wt(gathered):
    tile = W_IN_COL_TILE

    def body(g_ref, *o_refs):
        full = jnp.concatenate([g_ref[j] for j in range(N_DEV)], axis=0)
        for o_ref, (row, n) in zip(o_refs, _WT_SEGMENTS):
            seg = full[row:row + n]
            if n < o_ref.shape[0]:
                seg = jnp.concatenate([seg, jnp.zeros((o_ref.shape[0] - n, tile), BF16)], axis=0)
            o_ref[...] = seg

    sizes = (NA, NB, NG, NF)
    return pl.pallas_call(
        body, name="split_wt", grid=(D // tile,),
        in_specs=[pl.BlockSpec((N_DEV, COLS_PER_DEV, tile), lambda i: (0, 0, i))],
        out_specs=[pl.BlockSpec((n, tile), lambda i: (0, i)) for n in sizes],
        out_shape=[jax.ShapeDtypeStruct((n, D), BF16) for n in sizes],
        compiler_params=_params("arbitrary"))(gathered)


def _by_cols(a):
    return jnp.moveaxis(a, 0, 1).reshape(a.shape[1], -1)


def _col_slabs(a):
    return jnp.moveaxis(a.reshape(a.shape[0], N_DEV, -1), 1, 0).astype(BF16)


def _pack_small(grads, loss):
    pieces, at = [], 0
    for n, (off, size) in list(SMALL_SLOTS.items()) + [("loss", (LOSS_SLOT, 1))]:
        pieces += [jnp.zeros((off - at,), F32), (loss if n == "loss" else grads[n]).reshape(-1)]
        at = off + size
    return jnp.concatenate(pieces + [jnp.zeros((SMALL_LEN - at,), F32)]).reshape(1, 1, SMALL_LEN)


def _gather_weights(t):
    cast = lambda a: a.astype(BF16)
    loras = jnp.stack([t["w_lora_up"][0], t["a_lora_up"][0]])
    wt, woa, wob, wo, lora = _all_gather(
        cast(t["w_in"][0].T), [cast(t["w_out_a"][0]), cast(t["w_out_b"][0]), cast(t["w_out"][0]), cast(loras)],
        "weight_gather")
    in_a, in_b, in_g, in_f = _split_wt(wt)
    return {"in_a": in_a, "in_b": in_b, "in_g": in_g, "in_f": in_f, "w_out_a": _by_cols(woa), "w_out_b": _by_cols(wob),
            "w_out": wo.reshape(D, D), "w_lora_up": lora[:, 0], "a_lora_up": lora[:, 1]}


def kernel(x, norm_g, w_in, shift_mu, w_lora_up, w0, a_lora_up, a0, k_k, k_a, r_k, lnx_w, lnx_b, f_bias, q_norm_g, k_norm_g, w_out_a, w_out_b, w_out, final_norm_g, loss_target, m_norm_g, m_w_in, m_shift_mu, m_w_lora_up, m_w0, m_a_lora_up, m_a0, m_k_k, m_k_a, m_r_k, m_lnx_w, m_lnx_b, m_f_bias, m_q_norm_g, m_k_norm_g, m_w_out_a, m_w_out_b, m_w_out, m_final_norm_g, v_norm_g, v_w_in, v_shift_mu, v_w_lora_up, v_w0, v_a_lora_up, v_a0, v_k_k, v_k_a, v_r_k, v_lnx_w, v_lnx_b, v_f_bias, v_q_norm_g, v_k_norm_g, v_w_out_a, v_w_out_b, v_w_out, v_final_norm_g):
    names = ("norm_g", "w_in", "shift_mu", "w_lora_up", "w0", "a_lora_up", "a0", "k_k", "k_a", "r_k", "lnx_w", "lnx_b",
             "f_bias", "q_norm_g", "k_norm_g", "w_out_a", "w_out_b", "w_out", "final_norm_g")
    weights = dict(zip(names, (norm_g, w_in, shift_mu, w_lora_up, w0, a_lora_up, a0, k_k, k_a, r_k, lnx_w, lnx_b,
                               f_bias, q_norm_g, k_norm_g, w_out_a, w_out_b, w_out, final_norm_g)))
    m_in = dict(zip(names, (m_norm_g, m_w_in, m_shift_mu, m_w_lora_up, m_w0, m_a_lora_up, m_a0, m_k_k, m_k_a, m_r_k,
                            m_lnx_w, m_lnx_b, m_f_bias, m_q_norm_g, m_k_norm_g, m_w_out_a, m_w_out_b, m_w_out,
                            m_final_norm_g)))
    v_in = dict(zip(names, (v_norm_g, v_w_in, v_shift_mu, v_w_lora_up, v_w0, v_a_lora_up, v_a0, v_k_k, v_k_a, v_r_k,
                            v_lnx_w, v_lnx_b, v_f_bias, v_q_norm_g, v_k_norm_g, v_w_out_a, v_w_out_b, v_w_out,
                            v_final_norm_g)))

    matrices = ("w_out_a", "w_out_b", "w_out", "w_lora_up", "a_lora_up")
    as_2d = lambda n, a: a[0] if n in matrices else a.reshape(1, -1)

    full = _gather_weights(weights)
    dx, dng, recv_wt, recvs, recv_small = _local_step(
        x[0], loss_target[0], full, {n: as_2d(n, weights[n]) for n in ("norm_g",) + tuple(SMALL_SLOTS)})

    res, (recv_norm,) = _adamw_w_in(*recv_wt, w_in[0].T, m_w_in[0].T, v_w_in[0].T, (dng[None],), ((0, N_DEV),))
    outs = {"w_in": [r.T[None] for r in res]}
    misc = [n for n in names if n != "w_in"]
    res, loss_sum = _adamw_misc(recvs, recv_small, recv_norm,
                                {n: tuple(as_2d(n, t[n]) for t in (weights, m_in, v_in)) for n in misc})
    for n in misc:
        outs[n] = [r.reshape(weights[n].shape) for r in res[n]]
    return (loss_sum.reshape(()), dx[None], *[outs[n][i] for i in range(4) for n in names])
```

```python
import functools
import math

import jax
import jax.numpy as jnp
from jax import lax
from jax.experimental import pallas as pl
from jax.experimental.pallas import tpu as pltpu

F32 = jnp.float32
BF16 = jnp.bfloat16
HI = lax.Precision.HIGHEST
MESH = pl.DeviceIdType.MESH

N_DEV = 8
D = 1024
H = 8
N = 64
DA = H * N
RANK = 64
NA = 4 * DA + 2 * RANK
NB = 4 * DA
NG = 2 * D
NF = 128
IN_COLS = NA + NB + H + NG
COLS_PER_DEV = IN_COLS // N_DEV
RMS_EPS = 1e-6
LNX_EPS = 64e-5
ATT_SCALE = N ** -0.5

ADAM_LR = 0.001
ADAM_B1 = 0.9
ADAM_B2 = 0.999
ADAM_EPS = 1e-08
ADAM_WD = 0.01
ADAM_STEP = 10

LANES = 128
WKV_CHUNK = 64
TOK_TILE = 256
HEAD_TILE = 256
XGRAD_TILE = 128
WGRAD_TILE = 512
ATT_TILE = 256
ATT_GROUPS = 8
VMEM_LIMIT = 56 * 1024 * 1024


def _lane_tile_slots(sizes):
    slots, at = {}, 0
    for name, size in sizes:
        slots[name] = (at, size)
        at += -(-size // LANES) * LANES
    return slots, at


SMALL_SLOTS, LOSS_SLOT = _lane_tile_slots((
    ("final_norm_g", D), ("shift_mu", NA), ("w0", DA), ("a0", DA), ("k_k", DA), ("k_a", DA), ("r_k", DA), ("lnx_w", DA),
    ("lnx_b", DA), ("q_norm_g", N), ("k_norm_g", N), ("f_bias", H)))
SMALL_LEN = LOSS_SLOT + LANES
W_IN_COL_TILE = 512
EARLY_FROM = -(-NA // COLS_PER_DEV)


def _params(*sem):
    return pltpu.CompilerParams(dimension_semantics=sem or None, vmem_limit_bytes=VMEM_LIMIT)


def _bdot(a, b):
    return jnp.dot(a.astype(BF16), b.astype(BF16), preferred_element_type=F32)


def _bdot_nt(a, b):
    return lax.dot_general(a.astype(BF16), b.astype(BF16), (((1,), (1,)), ((), ())), preferred_element_type=F32)


def _bdot_tn(a, b):
    return lax.dot_general(a.astype(BF16), b.astype(BF16), (((0,), (0,)), ((), ())), preferred_element_type=F32)


def _sigmoid(x):
    return 1.0 / (1.0 + jnp.exp(-x))


def _softplus(x):
    return jnp.maximum(x, 0.0) + jnp.log(1.0 + jnp.exp(-jnp.abs(x)))


def _heads(ref, col0):
    return jnp.stack([ref[:, col0 + N * h:col0 + N * (h + 1)] for h in range(H)])


def _lerp(c, s, mu):
    return c + (s - c) * mu


def _head_sums(x):
    low = lax.broadcasted_iota(jnp.int32, (x.shape[0], LANES), 1) < N
    out = []
    for p in range(x.shape[1] // LANES):
        pair = x[:, LANES * p:LANES * (p + 1)]
        first = jnp.sum(jnp.where(low, pair, 0.0), axis=-1, keepdims=True)
        second = jnp.sum(jnp.where(low, 0.0, pair), axis=-1, keepdims=True)
        out.append(jnp.where(low, first, second))
    return jnp.concatenate(out, axis=-1)


def _to_heads(x):
    return [x[:, N * h:N * (h + 1)] for h in range(H)]


def _from_heads(ref):
    return jnp.concatenate([ref[h] for h in range(H)], axis=-1)


def _rwkv_pre(rc, rs, kc, ks, vc, vs, gc, gs, wdc, wds, adc, ads,
              mu_r, mu_k, mu_v, mu_g, mu_wd, mu_ad, w_up, w0, a_up, a0, k_k, k_a):
    r = _lerp(rc, rs, mu_r)
    k = _lerp(kc, ks, mu_k)
    v = _lerp(vc, vs, mu_v)
    g = _lerp(gc, gs, mu_g)
    wd = _lerp(wdc, wds, mu_wd)
    ad = _lerp(adc, ads, mu_ad)
    t = wd.shape[0]
    w_raw = -_softplus(-(w0 + _bdot(jnp.tanh(wd), w_up))) - 0.5
    lw = -jnp.exp(w_raw)
    row = lax.broadcasted_iota(jnp.int32, (t, t), 0)
    col = lax.broadcasted_iota(jnp.int32, (t, t), 1)
    same_chunk = ((row >= col) & (row // WKV_CHUNK == col // WKV_CHUNK)).astype(F32)
    cl = jnp.dot(same_chunk, lw, precision=HI, preferred_element_type=F32)
    alr = _sigmoid(a0 + _bdot(ad, a_up))
    kk = k * k_k
    kk = kk / jnp.maximum(jnp.sqrt(_head_sums(kk * kk)), 1e-12)
    k2 = k * (1.0 + (alr - 1.0) * k_a)
    return r, lw, cl, k2, v, -kk, kk * alr, g


_MM_DIMS = {"nn": (((2,), (1,)), ((0,), (0,))), "nt": (((2,), (2,)), ((0,), (0,))), "tn": (((1,), (1,)), ((0,), (0,)))}


def _dot1(a, b, kind):
    return lax.dot_general(a.astype(BF16), b.astype(BF16), dimension_numbers=_MM_DIMS[kind], preferred_element_type=F32)


@functools.partial(jax.custom_vjp, nondiff_argnums=(2,))
def _mm(a, b, kind):
    return _dot1(a, b, kind)


def _mm_fwd(a, b, kind):
    return _dot1(a, b, kind), (a, b)


def _mm_bwd(kind, res, ct):
    a, b = res
    if kind == "nn":
        return _dot1(ct, b, "nt"), _dot1(a, ct, "tn")
    if kind == "nt":
        return _dot1(ct, b, "nn"), _dot1(ct, a, "tn")
    return _dot1(b, ct, "nt"), _dot1(a, ct, "nn")


_mm.defvjp(_mm_fwd, _mm_bwd)


def _chunk_masks(c):
    row = lax.broadcasted_iota(jnp.int32, (c, c), 0)
    col = lax.broadcasted_iota(jnp.int32, (c, c), 1)
    return (row >= col)[None], (row > col)[None], (row == col).astype(F32)[None]


def _wkv_aab(lw, cl, a, b):
    _, strict, _ = _chunk_masks(a.shape[1])
    return jnp.where(strict, _mm(a * jnp.exp(cl - lw), b * jnp.exp(-cl), "nt"), 0.0)


def _tri_inverse(x):
    c = x.shape[1]
    p = _chunk_masks(c)[2] + x
    for _ in range(int(math.log2(c)) - 1):
        x = _dot1(x, x, "nn")
        p = p + _dot1(p, x, "nn")
    return p


def _wkv_apply(s0, r, lw, cl, k, v, a, b, p):
    c = r.shape[1]
    incl, strict, _ = _chunk_masks(c)
    gi = jnp.exp(-cl)
    left = jnp.concatenate([a * jnp.exp(cl - lw), r * jnp.exp(cl)], axis=1)
    right = jnp.concatenate([b * gi, k * gi], axis=1)
    m = _mm(left, right, "nt")
    z0 = _mm(left, s0, "nt")
    a_ak = jnp.where(strict, m[:, :c, c:], 0.0)
    row = lax.broadcasted_iota(jnp.int32, (c, 2 * c), 0)
    col = lax.broadcasted_iota(jnp.int32, (c, 2 * c), 1)
    a_r = jnp.where((row >= col % c)[None], m[:, c:, :], 0.0)
    sa = _mm(p, z0[:, :c] + _mm(a_ak, v, "nn"), "nn")
    sa_v = jnp.concatenate([sa, v], axis=1)
    y = z0[:, c:] + _mm(a_r, sa_v, "nn")
    s1 = (s0 + _mm(sa_v, right, "tn")) * jnp.exp(cl[:, c - 1:c, :])
    return y, s1


def _rwkv_post(y, r, k2, v, g, lnx_w, lnx_b, r_k):
    yc = y - _head_sums(y) * (1.0 / N)
    var = _head_sums(yc * yc) * (1.0 / N)
    yn = yc * lax.rsqrt(var + LNX_EPS) * lnx_w + lnx_b
    bonus = _head_sums(r * k2 * r_k) * v
    return (yn + bonus) * (g * _sigmoid(g))


def _fox_pre(q, k, f, q_g, k_g, f_b):
    qn = q * lax.rsqrt(_head_sums(q * q) * (1.0 / N) + RMS_EPS) * q_g
    kn = k * lax.rsqrt(_head_sums(k * k) * (1.0 / N) + RMS_EPS) * k_g
    x = f + f_b
    return qn, kn, jnp.minimum(x, 0.0) - jnp.log(1.0 + jnp.exp(-jnp.abs(x)))


def _norm_proj(x, g, wts):
    s = x.shape[0]
    k = len(wts)

    def body(x_ref, g_ref, *refs):
        w_refs, h_ref, o_refs = refs[:k], refs[k], refs[k + 1:]
        xv = x_ref[...]
        h = (xv * lax.rsqrt(jnp.mean(xv * xv, axis=-1, keepdims=True) + RMS_EPS) * g_ref[...]).astype(BF16)
        h_ref[...] = h
        for w_ref, o_ref in zip(w_refs, o_refs):
            o_ref[...] = _bdot_nt(h, w_ref[...])

    tok = lambda n: pl.BlockSpec((TOK_TILE, n), lambda i: (i, 0))
    out = pl.pallas_call(
        body, name="norm_proj", grid=(s // TOK_TILE,),
        in_specs=[tok(D), pl.BlockSpec((1, D), lambda i: (0, 0))] + [pl.BlockSpec(w.shape, lambda i: (0, 0)) for w in wts],
        out_specs=[tok(D)] + [tok(w.shape[0]) for w in wts],
        out_shape=[jax.ShapeDtypeStruct((s, D), BF16)] + [jax.ShapeDtypeStruct((s, w.shape[0]), F32) for w in wts],
        compiler_params=_params("arbitrary"))(x, g, *wts)
    return out[0], out[1:]


def _proj_wgrad_early(h, dub, dug, duf, head_rows):
    s = dub.shape[0]
    steps = s // WGRAD_TILE
    seg_rows = (_WT_SEGMENTS[1], _WT_SEGMENTS[2], _WT_SEGMENTS[3])

    def body(h_ref, b_ref, g_ref, f_ref, o_ref, head_ref, *accs):
        @pl.when(pl.program_id(0) == 0)
        def _():
            for acc in accs:
                acc[...] = jnp.zeros_like(acc)

        h = h_ref[...]
        for acc, du_ref in zip(accs, (b_ref, g_ref, f_ref)):
            acc[...] += _bdot_tn(du_ref[...], h)

        @pl.when(pl.program_id(0) == steps - 1)
        def _():
            head_ref[...] = accs[0][:head_rows, :]
            for j in range(EARLY_FROM, N_DEV):
                lo, hi = COLS_PER_DEV * j, COLS_PER_DEV * (j + 1)
                parts = []
                for acc, (row, n) in sorted(zip(accs, seg_rows), key=lambda t: t[1][0]):
                    first, last = max(lo, row), min(hi, row + n)
                    if first < last:
                        parts.append(acc[first - row:last - row, :])
                o_ref[j - EARLY_FROM] = (parts[0] if len(parts) == 1 else jnp.concatenate(parts, axis=0)).astype(BF16)

    tok = lambda n: pl.BlockSpec((WGRAD_TILE, n), lambda i: (i, 0))
    n_early = N_DEV - EARLY_FROM
    return pl.pallas_call(
        body, name="wgrad_bgf", grid=(steps,), in_specs=[tok(D), tok(NB), tok(NG), tok(NF)],
        out_specs=[pl.BlockSpec((n_early, COLS_PER_DEV, D), lambda i: (0, 0, 0)),
                   pl.BlockSpec((head_rows, D), lambda i: (0, 0))],
        out_shape=[jax.ShapeDtypeStruct((n_early, COLS_PER_DEV, D), BF16), jax.ShapeDtypeStruct((head_rows, D), F32)],
        scratch_shapes=[pltpu.VMEM((n, D), F32) for n in (NB, NG, NF)],
        compiler_params=_params("arbitrary"))(h, dub, dug, duf)


def _proj_wgrad_late(h, dua, dwt_b_head):
    s = dua.shape[0]
    steps = s // WGRAD_TILE

    def body(h_ref, du_ref, b_ref, o_ref, acc):
        @pl.when(pl.program_id(0) == 0)
        def _():
            acc[...] = jnp.zeros_like(acc)

        acc[...] += _bdot_tn(du_ref[...], h_ref[...])

        @pl.when(pl.program_id(0) == steps - 1)
        def _():
            for j in range(EARLY_FROM):
                lo, hi = COLS_PER_DEV * j, COLS_PER_DEV * (j + 1)
                parts = [acc[lo:min(hi, NA), :]] + ([b_ref[:hi - NA, :]] if hi > NA else [])
                o_ref[j] = (parts[0] if len(parts) == 1 else jnp.concatenate(parts, axis=0)).astype(BF16)

    return pl.pallas_call(
        body, name="wgrad_a", grid=(steps,),
        in_specs=[pl.BlockSpec((WGRAD_TILE, D), lambda i: (i, 0)), pl.BlockSpec((WGRAD_TILE, NA), lambda i: (i, 0)),
                  pl.BlockSpec(dwt_b_head.shape, lambda i: (0, 0))],
        out_specs=pl.BlockSpec((EARLY_FROM, COLS_PER_DEV, D), lambda i: (0, 0, 0)),
        out_shape=jax.ShapeDtypeStruct((EARLY_FROM, COLS_PER_DEV, D), BF16),
        scratch_shapes=[pltpu.VMEM((NA, D), F32)], compiler_params=_params("arbitrary"))(h, dua, dwt_b_head)


def _proj_xgrad(x, g, dx2, dus, ws, slabs, owners):
    s = x.shape[0]
    tile = XGRAD_TILE
    k = len(dus)
    nx = len(slabs)
    n_in = 3 + 2 * k + nx

    def body(*refs):
        x_ref, g_ref, dx2_ref = refs[:3]
        du_refs, w_refs = refs[3:3 + k], refs[3 + k:3 + 2 * k]
        src_refs = refs[3 + 2 * k:3 + 2 * k + nx]
        dx_ref, dg_ref = refs[n_in:n_in + 2]
        dst_refs = refs[n_in + 2:n_in + 2 + nx]
        start, wait = _exchange_ops(src_refs, dst_refs, owners, refs[n_in + 2 + nx:])

        @pl.when(pl.program_id(0) == 0)
        def _():
            dg_ref[...] = jnp.zeros_like(dg_ref)
            start()

        dh = _bdot(du_refs[0][...], w_refs[0][...])
        for du_ref, w_ref in zip(du_refs[1:], w_refs[1:]):
            dh += _bdot(du_ref[...], w_ref[...])
        xv = x_ref[...]
        rs = lax.rsqrt(jnp.mean(xv * xv, axis=-1, keepdims=True) + RMS_EPS)
        xn = xv * rs
        dg_ref[...] += jnp.sum(dh * xn, axis=0, keepdims=True)
        dxn = dh * g_ref[...]
        dx_ref[...] = rs * (dxn - xn * jnp.mean(dxn * xn, axis=-1, keepdims=True)) + dx2_ref[...]

        @pl.when(pl.program_id(0) == s // tile - 1)
        def _():
            wait()

    tok = lambda n: pl.BlockSpec((tile, n), lambda i: (i, 0))
    fixed = lambda a: pl.BlockSpec(a.shape, lambda i: (0,) * a.ndim)
    out = pl.pallas_call(
        body, name="proj_xgrad", grid=(s // tile,),
        in_specs=([tok(D), fixed(g), tok(D)] + [tok(du.shape[1]) for du in dus] + [fixed(w) for w in ws]
                  + _hbm_specs(nx)),
        out_specs=[tok(D), pl.BlockSpec((1, D), lambda i: (0, 0))] + _hbm_specs(nx),
        out_shape=[jax.ShapeDtypeStruct((s, D), F32), jax.ShapeDtypeStruct((1, D), F32)] + _received_shapes(slabs, owners),
        scratch_shapes=_exchange_scratch(nx),
        compiler_params=_params("arbitrary"))(x, g, dx2, *dus, *ws, *slabs)
    return out[0], out[1], out[2:]


def _tail(x, target, ya, o, ub, ug, w_oa, w_ob, w_o, fg):
    s = x.shape[0]
    tile = TOK_TILE

    def body(x_ref, t_ref, ya_ref, o_ref, gb_ref, ug_ref, woa_ref, wob_ref, wo_ref, fg_ref,
             loss_ref, dfg_ref, dwo_ref, dwoa_ref, dwob_ref, dx2_ref, dya_ref, do_ref, dgb_ref, dug_ref):
        @pl.when(pl.program_id(0) == 0)
        def _():
            for r in (loss_ref, dfg_ref, dwo_ref, dwoa_ref, dwob_ref):
                r[...] = jnp.zeros_like(r)

        ya_v = ya_ref[...]
        gate_b = gb_ref[...]
        sg_b = _sigmoid(gate_b)
        silu_b = gate_b * sg_b
        o_v = jnp.concatenate([o_ref[h] for h in range(H)], axis=-1)
        yb_v = o_v * silu_b
        big_a = _bdot(ya_v, woa_ref[...])
        big_b = _bdot(yb_v, wob_ref[...])
        sa = _sigmoid(ug_ref[:, :D])
        sb = _sigmoid(ug_ref[:, D:])
        merged = sa * big_a + sb * big_b
        x2 = x_ref[...] + _bdot(merged, wo_ref[...])
        rs = lax.rsqrt(jnp.mean(x2 * x2, axis=-1, keepdims=True) + RMS_EPS)
        xn = x2 * rs
        err = xn * fg_ref[...] - t_ref[...]
        loss_ref[...] += (0.5 / D) * jnp.sum(err * err)
        dout = err * (1.0 / D)
        dfg_ref[...] += jnp.sum(dout * xn, axis=0, keepdims=True)
        dxn = dout * fg_ref[...]
        dx2 = rs * (dxn - xn * jnp.mean(dxn * xn, axis=-1, keepdims=True))
        dx2_ref[...] = dx2
        dwo_ref[...] += _bdot_tn(merged, dx2)
        dmerged = _bdot_nt(dx2, wo_ref[...])
        dbig_a = dmerged * sa
        dbig_b = dmerged * sb
        dug_ref[:, :D] = (dmerged * big_a * sa * (1.0 - sa)).astype(BF16)
        dug_ref[:, D:] = (dmerged * big_b * sb * (1.0 - sb)).astype(BF16)
        dwoa_ref[...] += _bdot_tn(ya_v, dbig_a)
        dwob_ref[...] += _bdot_tn(yb_v, dbig_b)
        dya_ref[...] = _bdot_nt(dbig_a, woa_ref[...])
        dyb = _bdot_nt(dbig_b, wob_ref[...])
        dgb_ref[...] = dyb * o_v * (sg_b * (1.0 + gate_b * (1.0 - sg_b)))
        _dov = dyb * silu_b
        for h in range(H):
            do_ref[h] = _dov[:, N * h:N * (h + 1)]

    tok = lambda n: pl.BlockSpec((tile, n), lambda i: (i, 0))
    hm = pl.BlockSpec((H, tile, N), lambda i: (0, i, 0))
    fixed = lambda shape: pl.BlockSpec(shape, lambda i: (0,) * len(shape))
    f32 = lambda *shape: jax.ShapeDtypeStruct(shape, F32)
    return pl.pallas_call(
        body, name="tail", grid=(s // tile,),
        in_specs=[tok(D), tok(D), tok(DA), hm, pl.BlockSpec((tile, DA), lambda i: (i, 3)), tok(NG),
                  fixed((DA, D)), fixed((DA, D)), fixed((D, D)), fixed((1, D))],
        out_specs=[fixed((1, 1)), fixed((1, D)), fixed((D, D)), fixed((DA, D)), fixed((DA, D)),
                   tok(D), tok(DA), hm, tok(DA), tok(NG)],
        out_shape=[f32(1, 1), f32(1, D), f32(D, D), f32(DA, D), f32(DA, D),
                   f32(s, D), f32(s, DA), f32(H, s, N), f32(s, DA), jax.ShapeDtypeStruct((s, NG), BF16)],
        compiler_params=_params("arbitrary"))(x, target, ya, o, ub, ug, w_oa, w_ob, w_o, fg)


def _pre_operands(ua_ref, prev_ref, first):
    cur = ua_ref[...]
    t = cur.shape[0]
    prev_row = jnp.where(first, 0.0, prev_ref[7:8, :])
    rows = lax.broadcasted_iota(jnp.int32, cur.shape, 0)
    sh = jnp.where(rows == 0, prev_row, pltpu.roll(cur, 1, axis=0))
    ops = []
    for c0, n in ((0, DA), (DA, DA), (2 * DA, DA), (3 * DA + 2 * RANK, DA), (3 * DA, RANK), (3 * DA + RANK, RANK)):
        ops += [cur[:, c0:c0 + n], sh[:, c0:c0 + n]]
    del t
    return ops


def _ua_specs(tile, order):
    blocks = tile // 8
    return [pl.BlockSpec((tile, NA), lambda i: (order(i), 0)),
            pl.BlockSpec((8, NA), lambda i: (jnp.maximum(order(i) * blocks - 1, 0), 0))]


def _rwkv_pre_fwd(ua, pre_params):
    s = ua.shape[0]
    tile = HEAD_TILE

    def body(ua_ref, prev_ref, *refs):
        p_refs, o_refs = refs[:len(pre_params)], refs[len(pre_params):]
        ops = _pre_operands(ua_ref, prev_ref, pl.program_id(0) == 0)
        outs = _rwkv_pre(*ops, *[p[...] for p in p_refs])
        for o_ref, val in zip(o_refs, outs):
            o_ref[...] = val

    tm = pl.BlockSpec((tile, DA), lambda i: (i, 0))
    return pl.pallas_call(
        body, name="rwkv_pre_fwd", grid=(s // tile,),
        in_specs=_ua_specs(tile, lambda i: i) + [pl.BlockSpec(p.shape, lambda i, nd=p.ndim: (0,) * nd) for p in pre_params],
        out_specs=[tm] * 8, out_shape=[jax.ShapeDtypeStruct((s, DA), F32)] * 8,
        compiler_params=_params("arbitrary"))(ua, ua, *pre_params)


def _rwkv_pre_bwd(ua, pre_params, cots):
    s = ua.shape[0]
    tile = HEAD_TILE
    nt = s // tile
    n_p = len(pre_params)

    def body(ua_ref, prev_ref, *refs):
        p_refs, c_refs = refs[:n_p], refs[n_p:n_p + 11]
        dua_ref = refs[n_p + 11]
        dp_refs = refs[n_p + 12:n_p + 12 + n_p]
        carry_ref = refs[-1]
        i = pl.program_id(0)

        @pl.when(i == 0)
        def _():
            carry_ref[...] = jnp.zeros_like(carry_ref)
            for r in dp_refs:
                r[...] = jnp.zeros_like(r)

        ops = _pre_operands(ua_ref, prev_ref, i == nt - 1)
        _, vjp = jax.vjp(_rwkv_pre, *ops, *[p[...] for p in p_refs])
        c = [r[...] for r in c_refs]
        grads = vjp((c[0] + c[1], c[2], c[3], c[4] + c[5], c[6] + c[7], c[8], c[9], c[10]))
        d_ops, d_par = grads[:12], grads[12:]
        for r, val in zip(dp_refs, d_par):
            r[...] += val
        d_cur = jnp.concatenate([d_ops[0], d_ops[2], d_ops[4], d_ops[8], d_ops[10], d_ops[6]], axis=-1)
        d_sh = jnp.concatenate([d_ops[1], d_ops[3], d_ops[5], d_ops[9], d_ops[11], d_ops[7]], axis=-1)
        rows = lax.broadcasted_iota(jnp.int32, d_sh.shape, 0)
        dua = d_cur + jnp.where(rows == tile - 1, carry_ref[...], pltpu.roll(d_sh, tile - 1, axis=0))
        dua_ref[...] = dua.astype(BF16)
        carry_ref[...] = d_sh[0:1, :]

    rev = lambda i: nt - 1 - i
    tm = pl.BlockSpec((tile, DA), lambda i: (rev(i), 0))
    fixed = [pl.BlockSpec(p.shape, lambda i, nd=p.ndim: (0,) * nd) for p in pre_params]
    return pl.pallas_call(
        body, name="rwkv_pre_bwd", grid=(nt,),
        in_specs=_ua_specs(tile, rev) + fixed + [tm] * 11,
        out_specs=[pl.BlockSpec((tile, NA), lambda i: (rev(i), 0))] + fixed,
        out_shape=[jax.ShapeDtypeStruct((s, NA), BF16)] + [jax.ShapeDtypeStruct(p.shape, F32) for p in pre_params],
        scratch_shapes=[pltpu.VMEM((1, NA), F32)],
        compiler_params=_params("arbitrary"))(ua, ua, *pre_params, *cots)


def _wkv_fwd(seq):
    s = seq[0].shape[0]
    nc = s // WKV_CHUNK

    def body(r_ref, lw_ref, cl_ref, k_ref, v_ref, a_ref, b_ref, y_ref, ck_ref, p_ref, state):
        @pl.when(pl.program_id(0) == 0)
        def _():
            state[...] = jnp.zeros_like(state)

        r, lw, cl, k, v, a, b = (jnp.stack(_to_heads(ref[...])) for ref in (r_ref, lw_ref, cl_ref, k_ref, v_ref, a_ref,
                                                                             b_ref))
        s0 = state[...]
        ck_ref[0] = s0
        p = _tri_inverse(_wkv_aab(lw, cl, a, b))
        p_ref[0] = p
        y, s1 = _wkv_apply(s0, r, lw, cl, k, v, a, b, p)
        y_ref[...] = jnp.concatenate([y[h] for h in range(H)], axis=-1)
        state[...] = s1

    tm = pl.BlockSpec((WKV_CHUNK, DA), lambda c: (c, 0))
    per_chunk = lambda m: pl.BlockSpec((1, H, m, m), lambda c: (c, 0, 0, 0))
    return pl.pallas_call(
        body, name="wkv_fwd", grid=(nc,), in_specs=[tm] * 7,
        out_specs=[tm, per_chunk(N), per_chunk(WKV_CHUNK)],
        out_shape=[jax.ShapeDtypeStruct((s, DA), F32), jax.ShapeDtypeStruct((nc, H, N, N), F32),
                   jax.ShapeDtypeStruct((nc, H, WKV_CHUNK, WKV_CHUNK), F32)],
        scratch_shapes=[pltpu.VMEM((H, N, N), F32)], compiler_params=_params("arbitrary"))(*seq)


def _wkv_bwd(seq, ckpt, pinv, dy, slabs, owners):
    s = seq[0].shape[0]
    nc = s // WKV_CHUNK
    nx = len(slabs)

    def body(r_ref, lw_ref, cl_ref, k_ref, v_ref, a_ref, b_ref, ck_ref, p_ref, dy_ref, *refs):
        src_refs, d_refs, dst_refs = refs[:nx], refs[nx:nx + 7], refs[nx + 7:2 * nx + 7]
        dstate = refs[2 * nx + 7]
        start, wait = _exchange_ops(src_refs, dst_refs, owners, refs[2 * nx + 8:])

        @pl.when(pl.program_id(0) == 0)
        def _():
            dstate[...] = jnp.zeros_like(dstate)
            start()

        p = p_ref[0]
        r, lw, cl, k, v, a, b, dy = (jnp.stack(_to_heads(ref[...])) for ref in (r_ref, lw_ref, cl_ref, k_ref, v_ref,
                                                                                 a_ref, b_ref, dy_ref))
        _, vjp = jax.vjp(_wkv_apply, ck_ref[0], r, lw, cl, k, v, a, b, p)
        ds0, dr, dlw, dcl, dk, dv, da, db, dp = vjp((dy, dstate[...]))
        dstate[...] = ds0
        _, vjp_x = jax.vjp(_wkv_aab, lw, cl, a, b)
        dlw2, dcl2, da2, db2 = vjp_x(_dot1(_dot1(p, dp, "tn"), p, "nt"))
        for d_ref, val in zip(d_refs, (dr, dlw + dlw2, dcl + dcl2, dk, dv, da + da2, db + db2)):
            d_ref[...] = jnp.concatenate([val[h] for h in range(H)], axis=-1)

        @pl.when(pl.program_id(0) == nc - 1)
        def _():
            wait()

    tm = pl.BlockSpec((WKV_CHUNK, DA), lambda c: (nc - 1 - c, 0))
    per_chunk = lambda m: pl.BlockSpec((1, H, m, m), lambda c: (nc - 1 - c, 0, 0, 0))
    out = pl.pallas_call(
        body, name="wkv_bwd", grid=(nc,),
        in_specs=[tm] * 7 + [per_chunk(N), per_chunk(WKV_CHUNK), tm] + _hbm_specs(nx),
        out_specs=[tm] * 7 + _hbm_specs(nx),
        out_shape=[jax.ShapeDtypeStruct((s, DA), F32)] * 7 + _received_shapes(slabs, owners),
        scratch_shapes=[pltpu.VMEM((H, N, N), F32)] + _exchange_scratch(nx),
        compiler_params=_params("arbitrary"))(*seq, ckpt, pinv, dy, *slabs)
    return out[:7], out[7:]


def _rwkv_post_fwd(y, r, k2, v, g, post_params):
    s = y.shape[0]
    tile = TOK_TILE

    def body(*refs):
        refs[-1][...] = _rwkv_post(*[ref[...] for ref in refs[:-1]])

    tm = pl.BlockSpec((tile, DA), lambda i: (i, 0))
    par = pl.BlockSpec((1, DA), lambda i: (0, 0))
    return pl.pallas_call(
        body, name="rwkv_post_fwd", grid=(s // tile,), in_specs=[tm] * 5 + [par] * 3,
        out_specs=tm, out_shape=jax.ShapeDtypeStruct((s, DA), F32),
        compiler_params=_params("arbitrary"))(y, r, k2, v, g, *post_params)


def _rwkv_post_bwd(y, r, k2, v, g, post_params, dya, slabs, lo):
    s = y.shape[0]
    tile = HEAD_TILE

    def body(y_ref, r_ref, k_ref, v_ref, g_ref, w_ref, b_ref, rk_ref, dya_ref, s_ref, *refs):
        d_refs, p_ref = refs[:8], refs[8]
        start, wait = _pair_swap_ops(s_ref, p_ref, lo, refs[9:])

        @pl.when(pl.program_id(0) == 0)
        def _():
            for ref in d_refs[5:]:
                ref[...] = jnp.zeros_like(ref)
            start()

        _, vjp = jax.vjp(_rwkv_post, *[ref[...] for ref in (y_ref, r_ref, k_ref, v_ref, g_ref, w_ref, b_ref, rk_ref)])
        grads = vjp(dya_ref[...])
        for ref, val in zip(d_refs[:5], grads[:5]):
            ref[...] = val
        for ref, val in zip(d_refs[5:], grads[5:]):
            ref[...] += val

        @pl.when(pl.program_id(0) == s // tile - 1)
        def _():
            wait()

    tm = pl.BlockSpec((tile, DA), lambda i: (i, 0))
    par = pl.BlockSpec((1, DA), lambda i: (0, 0))
    return pl.pallas_call(
        body, name="rwkv_post_bwd", grid=(s // tile,),
        in_specs=[tm] * 5 + [par] * 3 + [tm] + _hbm_specs(1),
        out_specs=[tm] * 5 + [par] * 3 + _hbm_specs(1),
        out_shape=[jax.ShapeDtypeStruct((s, DA), F32)] * 5 + [jax.ShapeDtypeStruct((1, DA), F32)] * 3
        + [jax.ShapeDtypeStruct(slabs.shape, slabs.dtype)],
        scratch_shapes=_pair_swap_scratch(slabs.shape[0]),
        compiler_params=_params("arbitrary"))(y, r, k2, v, g, *post_params, dya, slabs)


def _tri(t):
    return (lax.broadcasted_iota(jnp.int32, (t, t), 0) >= lax.broadcasted_iota(jnp.int32, (t, t), 1)).astype(F32)


def _fox_pre_fwd(ub, uf, q_g, k_g, f_b):
    s = ub.shape[0]
    tile = HEAD_TILE

    def body(ub_ref, uf_ref, qg_ref, kg_ref, fb_ref, q_ref, k_ref, v_ref, cum_ref, carry):
        @pl.when(pl.program_id(0) == 0)
        def _():
            carry[...] = jnp.zeros_like(carry)

        qn, kn, logf = _fox_pre(ub_ref[:, :DA], ub_ref[:, DA:2 * DA], uf_ref[...], qg_ref[...], kg_ref[...],
                                fb_ref[...])
        for h, (q_col, k_col) in enumerate(zip(_to_heads(qn), _to_heads(kn))):
            q_ref[h] = q_col
            k_ref[h] = k_col
        v_ref[...] = _heads(ub_ref, 2 * DA)
        cum = jnp.dot(_tri(tile), logf, precision=HI, preferred_element_type=F32) + carry[...]
        cum_ref[...] = cum
        carry[...] = cum[tile - 1:tile, :]

    hm = pl.BlockSpec((H, tile, N), lambda i: (0, i, 0))
    fixed = lambda shape: pl.BlockSpec(shape, lambda i: (0,) * len(shape))
    return pl.pallas_call(
        body, name="fox_pre_fwd", grid=(s // tile,),
        in_specs=[pl.BlockSpec((tile, NB), lambda i: (i, 0)), pl.BlockSpec((tile, NF), lambda i: (i, 0)),
                  fixed((1, DA)), fixed((1, DA)), fixed((1, NF))],
        out_specs=[hm] * 3 + [pl.BlockSpec((tile, NF), lambda i: (i, 0))],
        out_shape=[jax.ShapeDtypeStruct((H, s, N), F32)] * 3 + [jax.ShapeDtypeStruct((s, NF), F32)],
        scratch_shapes=[pltpu.VMEM((1, NF), F32)], compiler_params=_params("arbitrary"))(ub, uf, q_g, k_g, f_b)


def _fox_pre_bwd(ub, uf, q_g, k_g, f_b, dqn, dkn, dvf, dgate, dcum_q, dcum_k):
    s = ub.shape[0]
    tile = HEAD_TILE
    nt = s // tile

    def body(ub_ref, uf_ref, qg_ref, kg_ref, fb_ref, dq_ref, dk_ref, dv_ref, dgate_ref, dcq_ref, dck_ref,
             dub_ref, duf_ref, dqg_ref, dkg_ref, dfb_ref, carry):
        @pl.when(pl.program_id(0) == 0)
        def _():
            carry[...] = jnp.zeros_like(carry)
            for ref in (dqg_ref, dkg_ref, dfb_ref):
                ref[...] = jnp.zeros_like(ref)

        dcum = dcq_ref[...] + dck_ref[...]
        dlogf = lax.dot_general(_tri(tile), dcum, (((0,), (0,)), ((), ())), precision=HI,
                                preferred_element_type=F32) + carry[...]
        carry[...] = dlogf[0:1, :]
        _, vjp = jax.vjp(_fox_pre, ub_ref[:, :DA], ub_ref[:, DA:2 * DA], uf_ref[...], qg_ref[...], kg_ref[...],
                         fb_ref[...])
        d_q, d_k, d_f, d_qg, d_kg, d_fb = vjp((_from_heads(dq_ref), _from_heads(dk_ref), dlogf))
        dub_ref[...] = jnp.concatenate([d_q, d_k, _from_heads(dv_ref), dgate_ref[...]], axis=-1).astype(BF16)
        duf_ref[...] = d_f.astype(BF16)
        dqg_ref[...] += functools.reduce(jnp.add, _to_heads(d_qg))
        dkg_ref[...] += functools.reduce(jnp.add, _to_heads(d_kg))
        dfb_ref[...] += d_fb

    rev = lambda i: nt - 1 - i
    hm = pl.BlockSpec((H, tile, N), lambda i: (0, rev(i), 0))
    tok = lambda n: pl.BlockSpec((tile, n), lambda i: (rev(i), 0))
    fixed = lambda shape: pl.BlockSpec(shape, lambda i: (0,) * len(shape))
    return pl.pallas_call(
        body, name="fox_pre_bwd", grid=(nt,),
        in_specs=[tok(NB), tok(NF), fixed((1, DA)), fixed((1, DA)), fixed((1, NF)), hm, hm, hm, tok(DA), tok(NF),
                  tok(NF)],
        out_specs=[tok(NB), tok(NF), fixed((1, N)), fixed((1, N)), fixed((1, NF))],
        out_shape=[jax.ShapeDtypeStruct((s, NB), BF16), jax.ShapeDtypeStruct((s, NF), BF16),
                   jax.ShapeDtypeStruct((1, N), F32), jax.ShapeDtypeStruct((1, N), F32),
                   jax.ShapeDtypeStruct((1, NF), F32)],
        scratch_shapes=[pltpu.VMEM((1, NF), F32)],
        compiler_params=_params("arbitrary"))(ub, uf, q_g, k_g, f_b, dqn, dkn, dvf, dgate, dcum_q, dcum_k)


def _att_groups(s):
    blocks = s // ATT_TILE
    per = max(1, blocks // ATT_GROUPS)
    return per, blocks // per


def _att_parts(n, width):
    return ([(0, n - width, False)] if n > width else []) + [(n - width, n, True)]


def _att_scores(q_bf, k_ref, ck_ref, lo, hi, masked, row_offset):
    scores = _bdot_nt(q_bf, k_ref[0, lo:hi, :]) - ck_ref[0, :, lo:hi]
    if masked:
        rows = row_offset + lax.broadcasted_iota(jnp.int32, scores.shape, 0)
        scores = jnp.where(rows >= lax.broadcasted_iota(jnp.int32, scores.shape, 1), scores, -1e30)
    return scores


def _fox_attn_fwd(q, k, v, cum_q, cum_k):
    s = q.shape[1]
    t = ATT_TILE
    per, groups = _att_groups(s)

    def body(q_ref, k_ref, v_ref, cq_ref, ck_ref, o_ref, lse_ref):
        qi = pl.program_id(1)
        for g in range(groups):
            @pl.when(qi // per == g)
            def _(g=g):
                q_bf = (q_ref[0] * ATT_SCALE).astype(BF16)
                parts = _att_parts((g + 1) * per * t, per * t)
                scores = [_att_scores(q_bf, k_ref, ck_ref, lo, hi, masked, (qi - g * per) * t)
                          for lo, hi, masked in parts]
                m = functools.reduce(jnp.maximum, [jnp.max(sc, axis=-1, keepdims=True) for sc in scores])
                l, acc = 0.0, 0.0
                for sc, (lo, hi, _) in zip(scores, parts):
                    p = jnp.exp(sc - m)
                    l += jnp.sum(p, axis=-1, keepdims=True)
                    acc += _bdot(p, v_ref[0, lo:hi, :])
                o_ref[0] = acc / l
                lse_ref[0] = m + jnp.log(l) + cq_ref[0]

    qb = pl.BlockSpec((1, t, N), lambda h, i: (h, i, 0))
    kb = pl.BlockSpec((1, s, N), lambda h, i: (h, 0, 0))
    return pl.pallas_call(
        body, name="fox_attn_fwd", grid=(H, s // t),
        in_specs=[qb, kb, kb, pl.BlockSpec((1, t, 1), lambda h, i: (h, i, 0)),
                  pl.BlockSpec((1, 1, s), lambda h, i: (h, 0, 0))],
        out_specs=[qb, pl.BlockSpec((1, t, 1), lambda h, i: (h, i, 0))],
        out_shape=[jax.ShapeDtypeStruct((H, s, N), F32), jax.ShapeDtypeStruct((H, s, 1), F32)],
        compiler_params=_params("arbitrary", "arbitrary"))(q, k, v, cum_q, cum_k)


def _fox_attn_bwd(q, k, v, cum_q, cum_k, o, lse, do, slabs, owners):
    s = q.shape[1]
    t = ATT_TILE
    per, groups = _att_groups(s)
    nx = len(slabs)

    def body(q_ref, k_ref, v_ref, cq_ref, ck_ref, o_ref, lse_ref, do_ref, *refs):
        src_refs, (dq_ref, dk_ref, dv_ref, dcq_ref, dck_ref) = refs[:nx], refs[nx:nx + 5]
        start, wait = _exchange_ops(src_refs, refs[nx + 5:2 * nx + 5], owners, refs[2 * nx + 5:])
        qi = pl.program_id(1)

        @pl.when((pl.program_id(0) == 0) & (qi == 0))
        def _():
            start()

        @pl.when(qi == 0)
        def _():
            for ref in (dk_ref, dv_ref, dck_ref):
                ref[...] = jnp.zeros_like(ref)

        for g in range(groups):
            @pl.when(qi // per == g)
            def _(g=g):
                q_bf, do_bf = (q_ref[0] * ATT_SCALE).astype(BF16), do_ref[0].astype(BF16)
                row_term = cq_ref[0] - lse_ref[0]
                delta = jnp.sum(do_ref[0] * o_ref[0], axis=-1, keepdims=True)
                dq, dcq = 0.0, 0.0
                for lo, hi, masked in _att_parts((g + 1) * per * t, per * t):
                    p = jnp.exp(_att_scores(q_bf, k_ref, ck_ref, lo, hi, masked, (qi - g * per) * t) + row_term)
                    ds = p * (_bdot_nt(do_bf, v_ref[0, lo:hi, :]) - delta)
                    dq += _bdot(ds, k_ref[0, lo:hi, :])
                    dcq += jnp.sum(ds, axis=-1, keepdims=True)
                    dk_ref[0, lo:hi, :] += _bdot_tn(ds, q_bf)
                    dv_ref[0, lo:hi, :] += _bdot_tn(p, do_bf)
                    dck_ref[0, :, lo:hi] -= jnp.sum(ds, axis=0, keepdims=True)
                dq_ref[0] = dq * ATT_SCALE
                dcq_ref[0] = dcq

        @pl.when((pl.program_id(0) == H - 1) & (qi == s // t - 1))
        def _():
            wait()

    qb = pl.BlockSpec((1, t, N), lambda h, i: (h, i, 0))
    kb = pl.BlockSpec((1, s, N), lambda h, i: (h, 0, 0))
    cqb = pl.BlockSpec((1, t, 1), lambda h, i: (h, i, 0))
    ckb = pl.BlockSpec((1, 1, s), lambda h, i: (h, 0, 0))
    f32 = lambda *shape: jax.ShapeDtypeStruct(shape, F32)
    out = pl.pallas_call(
        body, name="fox_attn_bwd", grid=(H, s // t),
        in_specs=[qb, kb, kb, cqb, ckb, qb, cqb, qb] + _hbm_specs(nx), out_specs=[qb, kb, kb, cqb, ckb] + _hbm_specs(nx),
        out_shape=[f32(H, s, N), f32(H, s, N), f32(H, s, N), f32(H, s, 1), f32(H, 1, s)]
        + _received_shapes(slabs, owners),
        scratch_shapes=_exchange_scratch(nx),
        compiler_params=_params("arbitrary", "arbitrary"))(q, k, v, cum_q, cum_k, o, lse, do, *slabs)
    return out[:5], out[5:]


def _local_step(x, target, w, p):
    mu = p["shift_mu"]
    lora_matrix = lambda a: jnp.moveaxis(a, 0, 1).reshape(RANK, DA).astype(F32)
    pre_params = (mu[:, 0:DA], mu[:, DA:2 * DA], mu[:, 2 * DA:3 * DA], mu[:, 3 * DA + 2 * RANK:],
                  mu[:, 3 * DA:3 * DA + RANK], mu[:, 3 * DA + RANK:3 * DA + 2 * RANK],
                  lora_matrix(w["w_lora_up"]), p["w0"], lora_matrix(w["a_lora_up"]), p["a0"], p["k_k"], p["k_a"])
    post_params = (p["lnx_w"], p["lnx_b"], p["r_k"])
    q_g, k_g = jnp.tile(p["q_norm_g"], (1, H)), jnp.tile(p["k_norm_g"], (1, H))
    f_b = jnp.pad(p["f_bias"], ((0, 0), (0, NF - H)))
    fg = p["final_norm_g"].reshape(1, D)

    h, (ua, ub, ug, uf) = _norm_proj(x, p["norm_g"], (w["in_a"], w["in_b"], w["in_g"], w["in_f"]))
    r, lw, cl, k2, v, av, bv, gg = _rwkv_pre_fwd(ua, pre_params)
    y, ckpt, pinv = _wkv_fwd((r, lw, cl, k2, v, av, bv))
    ya = _rwkv_post_fwd(y, r, k2, v, gg, post_params)
    qn, kn, vf, cum = _fox_pre_fwd(ub, uf, q_g, k_g, f_b)
    cum_t = cum[:, :H].T
    cum_q, cum_k = cum_t[:, :, None], cum_t[:, None, :]
    o, lse = _fox_attn_fwd(qn, kn, vf, cum_q, cum_k)

    (loss, dfg, dwo, dwoa, dwob, dx2, dya, do, dgate_b, dug) = _tail(
        x, target, ya, o, ub, ug, w["w_out_a"], w["w_out_b"], w["w_out"], fg)
    everyone = (0, N_DEV)
    (dqn, dkn, dvf, dcq, dck), (recv_woa, recv_wob, recv_wo) = _fox_attn_bwd(
        qn, kn, vf, cum_q, cum_k, o, lse, do,
        (_col_slabs(dwoa), _col_slabs(dwob), dwo.astype(BF16).reshape(N_DEV, D // N_DEV, D)), (everyone,) * 3)
    pad_f = lambda a: jnp.pad(a.T, ((0, 0), (0, NF - H)))
    dub, duf, dqg, dkg, dfb = _fox_pre_bwd(ub, uf, q_g, k_g, f_b, dqn, dkn, dvf, dgate_b,
                                           pad_f(dcq[:, :, 0]), pad_f(dck.reshape(H, -1)))
    spill = EARLY_FROM * COLS_PER_DEV - NA
    early, dwt_b_head = _proj_wgrad_early(h, dub, dug, duf, -(-spill // 8) * 8)
    dy, dr_p, dk_p, dv_p, dgg, dlnw, dlnb, drk, handed = _rwkv_post_bwd(y, r, k2, v, gg, post_params, dya, early,
                                                                          EARLY_FROM)
    early = _chip_sums(early, handed, EARLY_FROM, "chip_sums_early")
    (dr_s, dlw, dcl, dk_s, dv_s, dav, dbv), (recv_early,) = _wkv_bwd(
        (r, lw, cl, k2, v, av, bv), ckpt, pinv, dy, (early,), ((EARLY_FROM, N_DEV, "chips"),))
    pre_out = _rwkv_pre_bwd(ua, pre_params, (dr_s, dr_p, dlw, dcl, dk_s, dk_p, dv_s, dv_p, dav, dbv, dgg))
    dua, dpre = pre_out[0], pre_out[1:]
    late = _proj_wgrad_late(h, dua, dwt_b_head)

    flat = lambda a: a.reshape(1, -1)
    small = {
        "final_norm_g": dfg, "w0": dpre[7], "a0": dpre[9], "k_k": dpre[10], "k_a": dpre[11], "r_k": drk, "lnx_w": dlnw,
        "lnx_b": dlnb, "q_norm_g": dqg, "k_norm_g": dkg, "f_bias": dfb[:, :H],
        "shift_mu": jnp.concatenate([flat(dpre[0]), flat(dpre[1]), flat(dpre[2]), dpre[4], dpre[5], flat(dpre[3])], axis=1),
    }
    late = _chip_sums(late, _pair_swap(late, 0, "pair_swap_late"), 0, "chip_sums_late")
    by_head = lambda a: jnp.moveaxis(a.reshape(RANK, H, N), 1, 0)
    loras = jnp.stack([by_head(dpre[6]), by_head(dpre[8])], axis=1).astype(BF16)
    dx, dng, (recv_late, recv_lora, recv_small) = _proj_xgrad(
        x, p["norm_g"], dx2, (dua, dub, dug, duf), (w["in_a"], w["in_b"], w["in_g"], w["in_f"]),
        (late, loras, _pack_small(small, loss)), ((0, EARLY_FROM, "chips"), everyone, everyone))
    return dx, dng, (recv_early, recv_late), (recv_woa, recv_wob, recv_wo, recv_lora), recv_small


def _position():
    return lax.axis_index("x"), lax.axis_index("y"), lax.axis_index("c")


def _hbm_specs(n):
    return [pl.BlockSpec(memory_space=pl.ANY)] * n


BIG_GATHER_COPIES = 13
GATHER_ROW_CUT = 400


def _all_gather(big, blocks, name):
    n = len(blocks)

    def body(*refs):
        big_ref, x_refs = refs[0], refs[1:1 + n]
        big_out, out_refs = refs[1 + n], refs[2 + n:2 + 2 * n]
        send_sems, recv_sems, local_sems = refs[2 + 2 * n:]
        x, y, c = _position()
        me, sibling = (x, y, c), (x, y, 1 - c)
        chips = [(1 - x, y), (x, 1 - y), (1 - x, 1 - y)]
        x_nbr, y_nbr, diag = chips
        rows = big_ref.shape[0]
        cut = GATHER_ROW_CUT

        def part(ref, h):
            return ref if h is None else ref.at[pl.ds(0, cut)] if h == 0 else ref.at[pl.ds(cut, rows - cut)]

        def landed(chip, core, h):
            return part(big_out.at[4 * chip[0] + 2 * chip[1] + core], h)

        def big_copy(k, src, dst, to):
            return pltpu.make_async_remote_copy(src_ref=src, dst_ref=dst, send_sem=send_sems.at[7 * n + k],
                                                recv_sem=recv_sems.at[7 * n + k], device_id=to, device_id_type=MESH)

        def arrival(k, chip, core, h):
            dst = landed(chip, core, h)
            return big_copy(k, dst, dst, me)

        def pass_on(k, chip, h, to):
            src = landed(chip, c, h)
            return big_copy(k, src, src, to)

        big_mine = pltpu.make_async_copy(big_ref, landed((x, y), c, None), local_sems.at[n])
        big_mine.start()
        here = (x, y)
        big_sent = [big_copy(0, big_ref, landed(here, c, None), sibling),
                    big_copy(1, part(big_ref, 0), landed(here, c, 0), (*x_nbr, c)),
                    big_copy(2, part(big_ref, 1), landed(here, c, 1), (*y_nbr, c)),
                    big_copy(3, part(big_ref, 1), landed(here, c, 1), (*x_nbr, c)),
                    big_copy(4, part(big_ref, 0), landed(here, c, 0), (*y_nbr, c))]
        for cp in big_sent:
            cp.start()

        def copy(a, k, blk, to, own=False):
            dst = out_refs[a].at[4 * blk[0] + 2 * blk[1] + blk[2]]
            return pltpu.make_async_remote_copy(
                src_ref=x_refs[a] if own else dst, dst_ref=dst, send_sem=send_sems.at[7 * a + k],
                recv_sem=recv_sems.at[7 * a + k], device_id=to, device_id_type=MESH)

        mine = [pltpu.make_async_copy(x_refs[a], out_refs[a].at[4 * x + 2 * y + c], local_sems.at[a]) for a in range(n)]
        for cp in mine:
            cp.start()
        first = []
        for a in range(n):
            first.append(copy(a, 0, me, sibling, own=True))
            first += [copy(a, 1 + j, me, (*chip, c), own=True) for j, chip in enumerate(chips)]
        for cp in first:
            cp.start()

        big_steps = [(1, x_nbr, 0, (*y_nbr, c), 5, 7), (2, y_nbr, 1, (*x_nbr, c), 6, 8), (3, x_nbr, 1, None, None, 9),
                     (4, y_nbr, 0, None, None, 10), (5, diag, 0, None, None, 11), (6, diag, 1, None, None, 12)]
        for k, chip, h, onward, k_onward, k_sibling in big_steps:
            arrival(k, chip, c, h).wait_recv()
            if onward is not None:
                big_sent.append(pass_on(k_onward, chip, h, onward))
                big_sent[-1].start()
            big_sent.append(pass_on(k_sibling, chip, h, sibling))
            big_sent[-1].start()

        passed = []
        for j, chip in enumerate(chips):
            for a in range(n):
                copy(a, 1 + j, (*chip, c), me).wait_recv()
                passed.append(copy(a, 4 + j, (*chip, c), sibling))
                passed[-1].start()
        for a in range(n):
            copy(a, 0, sibling, me).wait_recv()
        for j, chip in enumerate(chips):
            for a in range(n):
                copy(a, 4 + j, (*chip, 1 - c), me).wait_recv()
        arrival(0, here, 1 - c, None).wait_recv()
        for k, chip, h, _, _, k_sibling in big_steps:
            arrival(k_sibling, chip, 1 - c, h).wait_recv()
        for cp in first + passed + big_sent:
            cp.wait_send()
        for cp in mine + [big_mine]:
            cp.wait()

    everything = [big] + list(blocks)
    return pl.pallas_call(
        body, name=name, out_shape=[jax.ShapeDtypeStruct((N_DEV,) + b.shape, b.dtype) for b in everything],
        in_specs=_hbm_specs(n + 1), out_specs=_hbm_specs(n + 1),
        scratch_shapes=[pltpu.SemaphoreType.DMA((7 * n + BIG_GATHER_COPIES,)),
                        pltpu.SemaphoreType.DMA((7 * n + BIG_GATHER_COPIES,)), pltpu.SemaphoreType.DMA((n + 1,))],
    )(*everything)


def _received_shapes(slabs, owners):
    return [jax.ShapeDtypeStruct((N_DEV // 2 if len(o) == 3 else N_DEV,) + s.shape[1:], s.dtype)
            for s, o in zip(slabs, owners)]


def _pair_swap_scratch(n):
    return [pltpu.SemaphoreType.DMA((n,)), pltpu.SemaphoreType.DMA((n,))]


def _pair_swap_ops(s_ref, p_ref, lo, sems):
    send_sems, recv_sems = sems
    n = s_ref.shape[0]

    def run(sending):
        x, y, c = _position()
        for side in (0, 1):
            mine = [pltpu.make_async_remote_copy(src_ref=s_ref.at[i], dst_ref=p_ref.at[i], send_sem=send_sems.at[i],
                                                 recv_sem=recv_sems.at[i], device_id=(x, y, 1 - c), device_id_type=MESH)
                    for i in range(n) if (lo + i) % 2 == side]

            @pl.when(c != side)
            def _():
                for cp in mine:
                    cp.start() if sending else cp.wait_send()

            if not sending:
                @pl.when(c == side)
                def _():
                    for cp in mine:
                        cp.wait_recv()

    return functools.partial(run, True), functools.partial(run, False)


def _pair_swap(slabs, lo, name):
    n = slabs.shape[0]

    def body(s_ref, p_ref, *sems):
        start, wait = _pair_swap_ops(s_ref, p_ref, lo, sems)
        start()
        wait()

    return pl.pallas_call(
        body, name=name, out_shape=jax.ShapeDtypeStruct(slabs.shape, slabs.dtype),
        in_specs=_hbm_specs(1), out_specs=_hbm_specs(1)[0], scratch_shapes=_pair_swap_scratch(n))(slabs)


def _chip_sums(slabs, swapped, lo, name):
    n, rows, cols = slabs.shape
    tile = W_IN_COL_TILE

    def body(s_ref, p_ref, o_ref):
        c = lax.axis_index("c")
        for i in range(n):
            @pl.when(c == (lo + i) % 2)
            def _(i=i):
                o_ref[i] = (s_ref[i].astype(F32) + p_ref[i].astype(F32)).astype(BF16)

    blk = pl.BlockSpec((n, rows, tile), lambda j: (0, 0, j))
    return pl.pallas_call(
        body, name=name, grid=(cols // tile,), in_specs=[blk, blk], out_specs=blk,
        out_shape=jax.ShapeDtypeStruct(slabs.shape, BF16), compiler_params=_params("arbitrary"))(slabs, swapped)


def _exchange_scratch(n):
    return [pltpu.SemaphoreType.DMA((7 * n,)), pltpu.SemaphoreType.DMA((7 * n,)), pltpu.SemaphoreType.DMA((n,))]


def _exchange_ops(src_refs, dst_refs, owners, sems):
    send_sems, recv_sems, local_sems = sems
    n = len(src_refs)

    def guarded(a, dev, fn):
        lo, hi = owners[a][:2]
        if (lo, hi) == (0, N_DEV):
            fn()
        else:
            pl.when((dev >= lo) & (dev < hi))(fn)

    def src(a, dev):
        ref = src_refs[a]
        return ref.at[0] if ref.shape[0] == 1 else ref.at[dev - owners[a][0]]

    def run(sending, waiting):
        x, y, c = _position()
        me = 4 * x + 2 * y + c
        for a in range(n):
            by_chip = len(owners[a]) == 3
            slot = (lambda qx, qy, qc: 2 * qx + qy) if by_chip else (lambda qx, qy, qc: 4 * qx + 2 * qy + qc)
            mine = slot(x, y, c)
            local = lambda a=a, mine=mine: pltpu.make_async_copy(src(a, me), dst_refs[a].at[mine], local_sems.at[a])
            if sending:
                guarded(a, me, lambda local=local: local().start())
            for m in range(2, N_DEV, 2) if by_chip else range(1, N_DEV):
                px, py, pc = x ^ (m >> 2), y ^ ((m >> 1) & 1), c ^ (m & 1)
                peer = 4 * px + 2 * py + pc
                theirs = slot(px, py, pc)
                sem = dict(send_sem=send_sems.at[7 * a + m - 1], recv_sem=recv_sems.at[7 * a + m - 1],
                           device_id=(px, py, pc), device_id_type=MESH)
                send = lambda a=a, peer=peer, sem=sem, mine=mine: pltpu.make_async_remote_copy(
                    src_ref=src(a, peer), dst_ref=dst_refs[a].at[mine], **sem)
                recv = lambda a=a, sem=sem, theirs=theirs: pltpu.make_async_remote_copy(
                    src_ref=src(a, me), dst_ref=dst_refs[a].at[theirs], **sem)
                if sending:
                    guarded(a, peer, lambda send=send: send().start())
                if waiting:
                    guarded(a, me, lambda recv=recv: recv().wait_recv())
                    guarded(a, peer, lambda send=send: send().wait_send())
            if waiting:
                guarded(a, me, lambda local=local: local().wait())

    return functools.partial(run, True, False), functools.partial(run, False, True)


def _sum_slabs(r_ref):
    g = r_ref[0].astype(F32)
    for k in range(1, r_ref.shape[0]):
        g = g + r_ref[k].astype(F32)
    return g


def _adamw(g, w, m, v):
    m_new = ADAM_B1 * m + (1.0 - ADAM_B1) * g
    v_new = ADAM_B2 * v + (1.0 - ADAM_B2) * (g * g)
    m_hat = m_new / (1.0 - ADAM_B1 ** ADAM_STEP)
    v_hat = v_new / (1.0 - ADAM_B2 ** ADAM_STEP)
    return g, -ADAM_LR * (m_hat / (jnp.sqrt(v_hat) + ADAM_EPS) + ADAM_WD * w), m_new, v_new


def _adamw_w_in(recv_early, recv_late, w, m, v, slabs, owners):
    rows, cols = w.shape
    tile = W_IN_COL_TILE
    nx = len(slabs)

    def body(early_ref, late_ref, w_ref, m_ref, v_ref, *refs):
        src_refs, o_refs, dst_refs = refs[:nx], refs[nx:nx + 4], refs[nx + 4:2 * nx + 4]
        start, wait = _exchange_ops(src_refs, dst_refs, owners, refs[2 * nx + 4:])
        x, y, c = _position()
        early_owner = 4 * x + 2 * y + c >= EARLY_FROM

        @pl.when(pl.program_id(0) == 0)
        def _():
            start()

        def update(g):
            for o_ref, val in zip(o_refs, _adamw(g, w_ref[...], m_ref[...], v_ref[...])):
                o_ref[...] = val

        pl.when(early_owner)(lambda: update(_sum_slabs(early_ref)))
        pl.when(jnp.logical_not(early_owner))(lambda: update(_sum_slabs(late_ref)))

        @pl.when(pl.program_id(0) == cols // tile - 1)
        def _():
            wait()

    blk = pl.BlockSpec((rows, tile), lambda i: (0, i))
    slots = lambda r: pl.BlockSpec((r.shape[0], rows, tile), lambda i: (0, 0, i))
    out = pl.pallas_call(
        body, name="adamw_w_in", grid=(cols // tile,),
        in_specs=[slots(recv_early), slots(recv_late), blk, blk, blk] + _hbm_specs(nx),
        out_specs=[blk] * 4 + _hbm_specs(nx),
        out_shape=[jax.ShapeDtypeStruct((rows, cols), F32)] * 4 + _received_shapes(slabs, owners),
        scratch_shapes=_exchange_scratch(nx),
        compiler_params=_params("arbitrary"))(recv_early, recv_late, w, m, v, *slabs)
    return out[:4], out[4:]


def _adamw_misc(recvs, recv_small, recv_norm, params):
    names = list(params)
    flat = [a for n in names for a in params[n]]

    def body(woa_ref, wob_ref, wo_ref, lora_ref, small_ref, norm_ref, *refs):
        p_refs, o_refs = refs[:len(flat)], refs[len(flat):]
        g_small = _sum_slabs(small_ref)
        g_lora = _sum_slabs(lora_ref)
        grads = {"w_out_a": _sum_slabs(woa_ref), "w_out_b": _sum_slabs(wob_ref), "w_out": _sum_slabs(wo_ref),
                 "w_lora_up": g_lora[0], "a_lora_up": g_lora[1], "norm_g": _sum_slabs(norm_ref)}
        for n, (off, size) in SMALL_SLOTS.items():
            grads[n] = g_small[:, off:off + size]
        for i, n in enumerate(names):
            w_ref, m_ref, v_ref = p_refs[3 * i:3 * i + 3]
            for o_ref, val in zip(o_refs[4 * i:4 * i + 4], _adamw(grads[n], w_ref[...], m_ref[...], v_ref[...])):
                o_ref[...] = val
        o_refs[-1][...] = g_small[:, LOSS_SLOT:LOSS_SLOT + 1]

    out = pl.pallas_call(
        body, name="adamw_misc",
        out_shape=[jax.ShapeDtypeStruct(params[n][0].shape, F32) for n in names for _ in range(4)]
        + [jax.ShapeDtypeStruct((1, 1), F32)],
        compiler_params=_params())(*recvs, recv_small, recv_norm, *flat)
    return {n: out[4 * i:4 * i + 4] for i, n in enumerate(names)}, out[-1]


_WT_SEGMENTS = ((0, NA), (NA, NB), (NA + NB + H, NG), (NA + NB, H))


def _split_wt(gathered):
    tile = W_IN_COL_TILE

    def body(g_ref, *o_refs):
        full = jnp.concatenate([g_ref[j] for j in range(N_DEV)], axis=0)
        for o_ref, (row, n) in zip(o_refs, _WT_SEGMENTS):
            seg = full[row:row + n]
            if n < o_ref.shape[0]:
                seg = jnp.concatenate([seg, jnp.zeros((o_ref.shape[0] - n, tile), BF16)], axis=0)
            o_ref[...] = seg

    sizes = (NA, NB, NG, NF)
    return pl.pallas_call(
        body, name="split_wt", grid=(D // tile,),
        in_specs=[pl.BlockSpec((N_DEV, COLS_PER_DEV, tile), lambda i: (0, 0, i))],
        out_specs=[pl.BlockSpec((n, tile), lambda i: (0, i)) for n in sizes],
        out_shape=[jax.ShapeDtypeStruct((n, D), BF16) for n in sizes],
        compiler_params=_params("arbitrary"))(gathered)


def _by_cols(a):
    return jnp.moveaxis(a, 0, 1).reshape(a.shape[1], -1)


def _col_slabs(a):
    return jnp.moveaxis(a.reshape(a.shape[0], N_DEV, -1), 1, 0).astype(BF16)


def _pack_small(grads, loss):
    pieces, at = [], 0
    for n, (off, size) in list(SMALL_SLOTS.items()) + [("loss", (LOSS_SLOT, 1))]:
        pieces += [jnp.zeros((off - at,), F32), (loss if n == "loss" else grads[n]).reshape(-1)]
        at = off + size
    return jnp.concatenate(pieces + [jnp.zeros((SMALL_LEN - at,), F32)]).reshape(1, 1, SMALL_LEN)


def _gather_weights(t):
    cast = lambda a: a.astype(BF16)
    loras = jnp.stack([t["w_lora_up"][0], t["a_lora_up"][0]])
    wt, woa, wob, wo, lora = _all_gather(
        cast(t["w_in"][0].T), [cast(t["w_out_a"][0]), cast(t["w_out_b"][0]), cast(t["w_out"][0]), cast(loras)],
        "weight_gather")
    in_a, in_b, in_g, in_f = _split_wt(wt)
    return {"in_a": in_a, "in_b": in_b, "in_g": in_g, "in_f": in_f, "w_out_a": _by_cols(woa), "w_out_b": _by_cols(wob),
            "w_out": wo.reshape(D, D), "w_lora_up": lora[:, 0], "a_lora_up": lora[:, 1]}


def kernel(x, norm_g, w_in, shift_mu, w_lora_up, w0, a_lora_up, a0, k_k, k_a, r_k, lnx_w, lnx_b, f_bias, q_norm_g, k_norm_g, w_out_a, w_out_b, w_out, final_norm_g, loss_target, m_norm_g, m_w_in, m_shift_mu, m_w_lora_up, m_w0, m_a_lora_up, m_a0, m_k_k, m_k_a, m_r_k, m_lnx_w, m_lnx_b, m_f_bias, m_q_norm_g, m_k_norm_g, m_w_out_a, m_w_out_b, m_w_out, m_final_norm_g, v_norm_g, v_w_in, v_shift_mu, v_w_lora_up, v_w0, v_a_lora_up, v_a0, v_k_k, v_k_a, v_r_k, v_lnx_w, v_lnx_b, v_f_bias, v_q_norm_g, v_k_norm_g, v_w_out_a, v_w_out_b, v_w_out, v_final_norm_g):
    names = ("norm_g", "w_in", "shift_mu", "w_lora_up", "w0", "a_lora_up", "a0", "k_k", "k_a", "r_k", "lnx_w", "lnx_b",
             "f_bias", "q_norm_g", "k_norm_g", "w_out_a", "w_out_b", "w_out", "final_norm_g")
    weights = dict(zip(names, (norm_g, w_in, shift_mu, w_lora_up, w0, a_lora_up, a0, k_k, k_a, r_k, lnx_w, lnx_b,
                               f_bias, q_norm_g, k_norm_g, w_out_a, w_out_b, w_out, final_norm_g)))
    m_in = dict(zip(names, (m_norm_g, m_w_in, m_shift_mu, m_w_lora_up, m_w0, m_a_lora_up, m_a0, m_k_k, m_k_a, m_r_k,
                            m_lnx_w, m_lnx_b, m_f_bias, m_q_norm_g, m_k_norm_g, m_w_out_a, m_w_out_b, m_w_out,
                            m_final_norm_g)))
    v_in = dict(zip(names, (v_norm_g, v_w_in, v_shift_mu, v_w_lora_up, v_w0, v_a_lora_up, v_a0, v_k_k, v_k_a, v_r_k,
                            v_lnx_w, v_lnx_b, v_f_bias, v_q_norm_g, v_k_norm_g, v_w_out_a, v_w_out_b, v_w_out,
                            v_final_norm_g)))

    matrices = ("w_out_a", "w_out_b", "w_out", "w_lora_up", "a_lora_up")
    as_2d = lambda n, a: a[0] if n in matrices else a.reshape(1, -1)

    full = _gather_weights(weights)
    dx, dng, recv_wt, recvs, recv_small = _local_step(
        x[0], loss_target[0], full, {n: as_2d(n, weights[n]) for n in ("norm_g",) + tuple(SMALL_SLOTS)})

    res, (recv_norm,) = _adamw_w_in(*recv_wt, w_in[0].T, m_w_in[0].T, v_w_in[0].T, (dng[None],), ((0, N_DEV),))
    outs = {"w_in": [r.T[None] for r in res]}
    misc = [n for n in names if n != "w_in"]
    res, loss_sum = _adamw_misc(recvs, recv_small, recv_norm,
                                {n: tuple(as_2d(n, t[n]) for t in (weights, m_in, v_in)) for n in misc})
    for n in misc:
        outs[n] = [r.reshape(weights[n].shape) for r in res[n]]
    return (loss_sum.reshape(()), dx[None], *[outs[n][i] for i in range(4) for n in names])
```

```python
import functools
import math

import jax
import jax.numpy as jnp
from jax import lax
from jax.experimental import pallas as pl
from jax.experimental.pallas import tpu as pltpu

F32 = jnp.float32
BF16 = jnp.bfloat16
HI = lax.Precision.HIGHEST
MESH = pl.DeviceIdType.MESH

N_DEV = 8
D = 1024
H = 8
N = 64
DA = H * N
RANK = 64
NA = 4 * DA + 2 * RANK
NB = 4 * DA
NG = 2 * D
NF = 128
IN_COLS = NA + NB + H + NG
COLS_PER_DEV = IN_COLS // N_DEV
RMS_EPS = 1e-6
LNX_EPS = 64e-5
ATT_SCALE = N ** -0.5

ADAM_LR = 0.001
ADAM_B1 = 0.9
ADAM_B2 = 0.999
ADAM_EPS = 1e-08
ADAM_WD = 0.01
ADAM_STEP = 10

LANES = 128
WKV_CHUNK = 64
TOK_TILE = 256
HEAD_TILE = 256
XGRAD_TILE = 256
WGRAD_TILE = 512
ATT_TILE = 256
ATT_GROUPS = 8
VMEM_LIMIT = 56 * 1024 * 1024


def _lane_tile_slots(sizes):
    slots, at = {}, 0
    for name, size in sizes:
        slots[name] = (at, size)
        at += -(-size // LANES) * LANES
    return slots, at


SMALL_SLOTS, LOSS_SLOT = _lane_tile_slots((
    ("final_norm_g", D), ("shift_mu", NA), ("w0", DA), ("a0", DA), ("k_k", DA), ("k_a", DA), ("r_k", DA), ("lnx_w", DA),
    ("lnx_b", DA), ("q_norm_g", N), ("k_norm_g", N), ("f_bias", H)))
SMALL_LEN = LOSS_SLOT + LANES
W_IN_COL_TILE = 512
EARLY_FROM = -(-NA // COLS_PER_DEV)


def _params(*sem):
    return pltpu.CompilerParams(dimension_semantics=sem or None, vmem_limit_bytes=VMEM_LIMIT)


def _bdot(a, b):
    return jnp.dot(a.astype(BF16), b.astype(BF16), preferred_element_type=F32)


def _bdot_nt(a, b):
    return lax.dot_general(a.astype(BF16), b.astype(BF16), (((1,), (1,)), ((), ())), preferred_element_type=F32)


def _bdot_tn(a, b):
    return lax.dot_general(a.astype(BF16), b.astype(BF16), (((0,), (0,)), ((), ())), preferred_element_type=F32)


def _sigmoid(x):
    return 1.0 / (1.0 + jnp.exp(-x))


def _softplus(x):
    return jnp.maximum(x, 0.0) + jnp.log(1.0 + jnp.exp(-jnp.abs(x)))


def _heads(ref, col0):
    return jnp.stack([ref[:, col0 + N * h:col0 + N * (h + 1)] for h in range(H)])


def _lerp(c, s, mu):
    return c + (s - c) * mu


def _head_sums(x):
    low = lax.broadcasted_iota(jnp.int32, (x.shape[0], LANES), 1) < N
    out = []
    for p in range(x.shape[1] // LANES):
        pair = x[:, LANES * p:LANES * (p + 1)]
        first = jnp.sum(jnp.where(low, pair, 0.0), axis=-1, keepdims=True)
        second = jnp.sum(jnp.where(low, 0.0, pair), axis=-1, keepdims=True)
        out.append(jnp.where(low, first, second))
    return jnp.concatenate(out, axis=-1)


def _to_heads(x):
    return [x[:, N * h:N * (h + 1)] for h in range(H)]


def _from_heads(ref):
    return jnp.concatenate([ref[h] for h in range(H)], axis=-1)


def _rwkv_pre(rc, rs, kc, ks, vc, vs, gc, gs, wdc, wds, adc, ads,
              mu_r, mu_k, mu_v, mu_g, mu_wd, mu_ad, w_up, w0, a_up, a0, k_k, k_a):
    r = _lerp(rc, rs, mu_r)
    k = _lerp(kc, ks, mu_k)
    v = _lerp(vc, vs, mu_v)
    g = _lerp(gc, gs, mu_g)
    wd = _lerp(wdc, wds, mu_wd)
    ad = _lerp(adc, ads, mu_ad)
    t = wd.shape[0]
    w_raw = -_softplus(-(w0 + _bdot(jnp.tanh(wd), w_up))) - 0.5
    lw = -jnp.exp(w_raw)
    row = lax.broadcasted_iota(jnp.int32, (t, t), 0)
    col = lax.broadcasted_iota(jnp.int32, (t, t), 1)
    same_chunk = ((row >= col) & (row // WKV_CHUNK == col // WKV_CHUNK)).astype(F32)
    cl = jnp.dot(same_chunk, lw, precision=HI, preferred_element_type=F32)
    alr = _sigmoid(a0 + _bdot(ad, a_up))
    kk = k * k_k
    kk = kk / jnp.maximum(jnp.sqrt(_head_sums(kk * kk)), 1e-12)
    k2 = k * (1.0 + (alr - 1.0) * k_a)
    return r, lw, cl, k2, v, -kk, kk * alr, g


_MM_DIMS = {"nn": (((2,), (1,)), ((0,), (0,))), "nt": (((2,), (2,)), ((0,), (0,))), "tn": (((1,), (1,)), ((0,), (0,)))}


def _dot1(a, b, kind):
    return lax.dot_general(a.astype(BF16), b.astype(BF16), dimension_numbers=_MM_DIMS[kind], preferred_element_type=F32)


@functools.partial(jax.custom_vjp, nondiff_argnums=(2,))
def _mm(a, b, kind):
    return _dot1(a, b, kind)


def _mm_fwd(a, b, kind):
    return _dot1(a, b, kind), (a, b)


def _mm_bwd(kind, res, ct):
    a, b = res
    if kind == "nn":
        return _dot1(ct, b, "nt"), _dot1(a, ct, "tn")
    if kind == "nt":
        return _dot1(ct, b, "nn"), _dot1(ct, a, "tn")
    return _dot1(b, ct, "nt"), _dot1(a, ct, "nn")


_mm.defvjp(_mm_fwd, _mm_bwd)


def _chunk_masks(c):
    row = lax.broadcasted_iota(jnp.int32, (c, c), 0)
    col = lax.broadcasted_iota(jnp.int32, (c, c), 1)
    return (row >= col)[None], (row > col)[None], (row == col).astype(F32)[None]


def _wkv_aab(lw, cl, a, b):
    _, strict, _ = _chunk_masks(a.shape[1])
    return jnp.where(strict, _mm(a * jnp.exp(cl - lw), b * jnp.exp(-cl), "nt"), 0.0)


def _tri_inverse(x):
    c = x.shape[1]
    p = _chunk_masks(c)[2] + x
    for _ in range(int(math.log2(c)) - 1):
        x = _dot1(x, x, "nn")
        p = p + _dot1(p, x, "nn")
    return p


def _wkv_apply(s0, r, lw, cl, k, v, a, b, p):
    c = r.shape[1]
    incl, strict, _ = _chunk_masks(c)
    gi = jnp.exp(-cl)
    left = jnp.concatenate([a * jnp.exp(cl - lw), r * jnp.exp(cl)], axis=1)
    right = jnp.concatenate([b * gi, k * gi], axis=1)
    m = _mm(left, right, "nt")
    z0 = _mm(left, s0, "nt")
    a_ak = jnp.where(strict, m[:, :c, c:], 0.0)
    row = lax.broadcasted_iota(jnp.int32, (c, 2 * c), 0)
    col = lax.broadcasted_iota(jnp.int32, (c, 2 * c), 1)
    a_r = jnp.where((row >= col % c)[None], m[:, c:, :], 0.0)
    sa = _mm(p, z0[:, :c] + _mm(a_ak, v, "nn"), "nn")
    sa_v = jnp.concatenate([sa, v], axis=1)
    y = z0[:, c:] + _mm(a_r, sa_v, "nn")
    s1 = (s0 + _mm(sa_v, right, "tn")) * jnp.exp(cl[:, c - 1:c, :])
    return y, s1


def _rwkv_post(y, r, k2, v, g, lnx_w, lnx_b, r_k):
    yc = y - _head_sums(y) * (1.0 / N)
    var = _head_sums(yc * yc) * (1.0 / N)
    yn = yc * lax.rsqrt(var + LNX_EPS) * lnx_w + lnx_b
    bonus = _head_sums(r * k2 * r_k) * v
    return (yn + bonus) * (g * _sigmoid(g))


def _fox_pre(q, k, f, q_g, k_g, f_b):
    qn = q * lax.rsqrt(_head_sums(q * q) * (1.0 / N) + RMS_EPS) * q_g
    kn = k * lax.rsqrt(_head_sums(k * k) * (1.0 / N) + RMS_EPS) * k_g
    x = f + f_b
    return qn, kn, jnp.minimum(x, 0.0) - jnp.log(1.0 + jnp.exp(-jnp.abs(x)))


def _norm_proj(x, g, wts):
    s = x.shape[0]
    k = len(wts)

    def body(x_ref, g_ref, *refs):
        w_refs, h_ref, o_refs = refs[:k], refs[k], refs[k + 1:]
        xv = x_ref[...]
        h = (xv * lax.rsqrt(jnp.mean(xv * xv, axis=-1, keepdims=True) + RMS_EPS) * g_ref[...]).astype(BF16)
        h_ref[...] = h
        for w_ref, o_ref in zip(w_refs, o_refs):
            o_ref[...] = _bdot_nt(h, w_ref[...])

    tok = lambda n: pl.BlockSpec((TOK_TILE, n), lambda i: (i, 0))
    out = pl.pallas_call(
        body, name="norm_proj", grid=(s // TOK_TILE,),
        in_specs=[tok(D), pl.BlockSpec((1, D), lambda i: (0, 0))] + [pl.BlockSpec(w.shape, lambda i: (0, 0)) for w in wts],
        out_specs=[tok(D)] + [tok(w.shape[0]) for w in wts],
        out_shape=[jax.ShapeDtypeStruct((s, D), BF16)] + [jax.ShapeDtypeStruct((s, w.shape[0]), F32) for w in wts],
        compiler_params=_params("arbitrary"))(x, g, *wts)
    return out[0], out[1:]


def _proj_wgrad_early(h, dub, dug, duf, head_rows):
    s = dub.shape[0]
    steps = s // WGRAD_TILE
    seg_rows = (_WT_SEGMENTS[1], _WT_SEGMENTS[2], _WT_SEGMENTS[3])

    def body(h_ref, b_ref, g_ref, f_ref, o_ref, head_ref, *accs):
        @pl.when(pl.program_id(0) == 0)
        def _():
            for acc in accs:
                acc[...] = jnp.zeros_like(acc)

        h = h_ref[...]
        for acc, du_ref in zip(accs, (b_ref, g_ref, f_ref)):
            acc[...] += _bdot_tn(du_ref[...], h)

        @pl.when(pl.program_id(0) == steps - 1)
        def _():
            head_ref[...] = accs[0][:head_rows, :]
            for j in range(EARLY_FROM, N_DEV):
                lo, hi = COLS_PER_DEV * j, COLS_PER_DEV * (j + 1)
                parts = []
                for acc, (row, n) in sorted(zip(accs, seg_rows), key=lambda t: t[1][0]):
                    first, last = max(lo, row), min(hi, row + n)
                    if first < last:
                        parts.append(acc[first - row:last - row, :])
                o_ref[j - EARLY_FROM] = (parts[0] if len(parts) == 1 else jnp.concatenate(parts, axis=0)).astype(BF16)

    tok = lambda n: pl.BlockSpec((WGRAD_TILE, n), lambda i: (i, 0))
    n_early = N_DEV - EARLY_FROM
    return pl.pallas_call(
        body, name="wgrad_bgf", grid=(steps,), in_specs=[tok(D), tok(NB), tok(NG), tok(NF)],
        out_specs=[pl.BlockSpec((n_early, COLS_PER_DEV, D), lambda i: (0, 0, 0)),
                   pl.BlockSpec((head_rows, D), lambda i: (0, 0))],
        out_shape=[jax.ShapeDtypeStruct((n_early, COLS_PER_DEV, D), BF16), jax.ShapeDtypeStruct((head_rows, D), F32)],
        scratch_shapes=[pltpu.VMEM((n, D), F32) for n in (NB, NG, NF)],
        compiler_params=_params("arbitrary"))(h, dub, dug, duf)


def _proj_wgrad_late(h, dua, dwt_b_head):
    s = dua.shape[0]
    steps = s // WGRAD_TILE

    def body(h_ref, du_ref, b_ref, o_ref, acc):
        @pl.when(pl.program_id(0) == 0)
        def _():
            acc[...] = jnp.zeros_like(acc)

        acc[...] += _bdot_tn(du_ref[...], h_ref[...])

        @pl.when(pl.program_id(0) == steps - 1)
        def _():
            for j in range(EARLY_FROM):
                lo, hi = COLS_PER_DEV * j, COLS_PER_DEV * (j + 1)
                parts = [acc[lo:min(hi, NA), :]] + ([b_ref[:hi - NA, :]] if hi > NA else [])
                o_ref[j] = (parts[0] if len(parts) == 1 else jnp.concatenate(parts, axis=0)).astype(BF16)

    return pl.pallas_call(
        body, name="wgrad_a", grid=(steps,),
        in_specs=[pl.BlockSpec((WGRAD_TILE, D), lambda i: (i, 0)), pl.BlockSpec((WGRAD_TILE, NA), lambda i: (i, 0)),
                  pl.BlockSpec(dwt_b_head.shape, lambda i: (0, 0))],
        out_specs=pl.BlockSpec((EARLY_FROM, COLS_PER_DEV, D), lambda i: (0, 0, 0)),
        out_shape=jax.ShapeDtypeStruct((EARLY_FROM, COLS_PER_DEV, D), BF16),
        scratch_shapes=[pltpu.VMEM((NA, D), F32)], compiler_params=_params("arbitrary"))(h, dua, dwt_b_head)


def _proj_xgrad(x, g, dx2, dus, ws, slabs, owners):
    s = x.shape[0]
    tile = XGRAD_TILE
    k = len(dus)
    nx = len(slabs)
    n_in = 3 + 2 * k + nx

    def body(*refs):
        x_ref, g_ref, dx2_ref = refs[:3]
        du_refs, w_refs = refs[3:3 + k], refs[3 + k:3 + 2 * k]
        src_refs = refs[3 + 2 * k:3 + 2 * k + nx]
        dx_ref, dg_ref = refs[n_in:n_in + 2]
        dst_refs = refs[n_in + 2:n_in + 2 + nx]
        start, wait = _exchange_ops(src_refs, dst_refs, owners, refs[n_in + 2 + nx:])

        @pl.when(pl.program_id(0) == 0)
        def _():
            dg_ref[...] = jnp.zeros_like(dg_ref)
            start()

        dh = _bdot(du_refs[0][...], w_refs[0][...])
        for du_ref, w_ref in zip(du_refs[1:], w_refs[1:]):
            dh += _bdot(du_ref[...], w_ref[...])
        xv = x_ref[...]
        rs = lax.rsqrt(jnp.mean(xv * xv, axis=-1, keepdims=True) + RMS_EPS)
        xn = xv * rs
        dg_ref[...] += jnp.sum(dh * xn, axis=0, keepdims=True)
        dxn = dh * g_ref[...]
        dx_ref[...] = rs * (dxn - xn * jnp.mean(dxn * xn, axis=-1, keepdims=True)) + dx2_ref[...]

        @pl.when(pl.program_id(0) == s // tile - 1)
        def _():
            wait()

    tok = lambda n: pl.BlockSpec((tile, n), lambda i: (i, 0))
    fixed = lambda a: pl.BlockSpec(a.shape, lambda i: (0,) * a.ndim)
    out = pl.pallas_call(
        body, name="proj_xgrad", grid=(s // tile,),
        in_specs=([tok(D), fixed(g), tok(D)] + [tok(du.shape[1]) for du in dus] + [fixed(w) for w in ws]
                  + _hbm_specs(nx)),
        out_specs=[tok(D), pl.BlockSpec((1, D), lambda i: (0, 0))] + _hbm_specs(nx),
        out_shape=[jax.ShapeDtypeStruct((s, D), F32), jax.ShapeDtypeStruct((1, D), F32)] + _received_shapes(slabs, owners),
        scratch_shapes=_exchange_scratch(nx),
        compiler_params=_params("arbitrary"))(x, g, dx2, *dus, *ws, *slabs)
    return out[0], out[1], out[2:]


def _tail(x, target, ya, o, ub, ug, w_oa, w_ob, w_o, fg):
    s = x.shape[0]
    tile = TOK_TILE

    def body(x_ref, t_ref, ya_ref, o_ref, gb_ref, ug_ref, woa_ref, wob_ref, wo_ref, fg_ref,
             loss_ref, dfg_ref, dwo_ref, dwoa_ref, dwob_ref, dx2_ref, dya_ref, do_ref, dgb_ref, dug_ref):
        @pl.when(pl.program_id(0) == 0)
        def _():
            for r in (loss_ref, dfg_ref, dwo_ref, dwoa_ref, dwob_ref):
                r[...] = jnp.zeros_like(r)

        ya_v = ya_ref[...]
        gate_b = gb_ref[...]
        sg_b = _sigmoid(gate_b)
        silu_b = gate_b * sg_b
        o_v = jnp.concatenate([o_ref[h] for h in range(H)], axis=-1)
        yb_v = o_v * silu_b
        big_a = _bdot(ya_v, woa_ref[...])
        big_b = _bdot(yb_v, wob_ref[...])
        sa = _sigmoid(ug_ref[:, :D])
        sb = _sigmoid(ug_ref[:, D:])
        merged = sa * big_a + sb * big_b
        x2 = x_ref[...] + _bdot(merged, wo_ref[...])
        rs = lax.rsqrt(jnp.mean(x2 * x2, axis=-1, keepdims=True) + RMS_EPS)
        xn = x2 * rs
        err = xn * fg_ref[...] - t_ref[...]
        loss_ref[...] += (0.5 / D) * jnp.sum(err * err)
        dout = err * (1.0 / D)
        dfg_ref[...] += jnp.sum(dout * xn, axis=0, keepdims=True)
        dxn = dout * fg_ref[...]
        dx2 = rs * (dxn - xn * jnp.mean(dxn * xn, axis=-1, keepdims=True))
        dx2_ref[...] = dx2
        dwo_ref[...] += _bdot_tn(merged, dx2)
        dmerged = _bdot_nt(dx2, wo_ref[...])
        dbig_a = dmerged * sa
        dbig_b = dmerged * sb
        dug_ref[:, :D] = (dmerged * big_a * sa * (1.0 - sa)).astype(BF16)
        dug_ref[:, D:] = (dmerged * big_b * sb * (1.0 - sb)).astype(BF16)
        dwoa_ref[...] += _bdot_tn(ya_v, dbig_a)
        dwob_ref[...] += _bdot_tn(yb_v, dbig_b)
        dya_ref[...] = _bdot_nt(dbig_a, woa_ref[...])
        dyb = _bdot_nt(dbig_b, wob_ref[...])
        dgb_ref[...] = dyb * o_v * (sg_b * (1.0 + gate_b * (1.0 - sg_b)))
        _dov = dyb * silu_b
        for h in range(H):
            do_ref[h] = _dov[:, N * h:N * (h + 1)]

    tok = lambda n: pl.BlockSpec((tile, n), lambda i: (i, 0))
    hm = pl.BlockSpec((H, tile, N), lambda i: (0, i, 0))
    fixed = lambda shape: pl.BlockSpec(shape, lambda i: (0,) * len(shape))
    f32 = lambda *shape: jax.ShapeDtypeStruct(shape, F32)
    return pl.pallas_call(
        body, name="tail", grid=(s // tile,),
        in_specs=[tok(D), tok(D), tok(DA), hm, pl.BlockSpec((tile, DA), lambda i: (i, 3)), tok(NG),
                  fixed((DA, D)), fixed((DA, D)), fixed((D, D)), fixed((1, D))],
        out_specs=[fixed((1, 1)), fixed((1, D)), fixed((D, D)), fixed((DA, D)), fixed((DA, D)),
                   tok(D), tok(DA), hm, tok(DA), tok(NG)],
        out_shape=[f32(1, 1), f32(1, D), f32(D, D), f32(DA, D), f32(DA, D),
                   f32(s, D), f32(s, DA), f32(H, s, N), f32(s, DA), jax.ShapeDtypeStruct((s, NG), BF16)],
        compiler_params=_params("arbitrary"))(x, target, ya, o, ub, ug, w_oa, w_ob, w_o, fg)


def _pre_operands(ua_ref, prev_ref, first):
    cur = ua_ref[...]
    t = cur.shape[0]
    prev_row = jnp.where(first, 0.0, prev_ref[7:8, :])
    rows = lax.broadcasted_iota(jnp.int32, cur.shape, 0)
    sh = jnp.where(rows == 0, prev_row, pltpu.roll(cur, 1, axis=0))
    ops = []
    for c0, n in ((0, DA), (DA, DA), (2 * DA, DA), (3 * DA + 2 * RANK, DA), (3 * DA, RANK), (3 * DA + RANK, RANK)):
        ops += [cur[:, c0:c0 + n], sh[:, c0:c0 + n]]
    del t
    return ops


def _ua_specs(tile, order):
    blocks = tile // 8
    return [pl.BlockSpec((tile, NA), lambda i: (order(i), 0)),
            pl.BlockSpec((8, NA), lambda i: (jnp.maximum(order(i) * blocks - 1, 0), 0))]


def _rwkv_pre_fwd(ua, pre_params):
    s = ua.shape[0]
    tile = HEAD_TILE

    def body(ua_ref, prev_ref, *refs):
        p_refs, o_refs = refs[:len(pre_params)], refs[len(pre_params):]
        ops = _pre_operands(ua_ref, prev_ref, pl.program_id(0) == 0)
        outs = _rwkv_pre(*ops, *[p[...] for p in p_refs])
        for o_ref, val in zip(o_refs, outs):
            o_ref[...] = val

    tm = pl.BlockSpec((tile, DA), lambda i: (i, 0))
    return pl.pallas_call(
        body, name="rwkv_pre_fwd", grid=(s // tile,),
        in_specs=_ua_specs(tile, lambda i: i) + [pl.BlockSpec(p.shape, lambda i, nd=p.ndim: (0,) * nd) for p in pre_params],
        out_specs=[tm] * 8, out_shape=[jax.ShapeDtypeStruct((s, DA), F32)] * 8,
        compiler_params=_params("arbitrary"))(ua, ua, *pre_params)


def _rwkv_pre_bwd(ua, pre_params, cots):
    s = ua.shape[0]
    tile = HEAD_TILE
    nt = s // tile
    n_p = len(pre_params)

    def body(ua_ref, prev_ref, *refs):
        p_refs, c_refs = refs[:n_p], refs[n_p:n_p + 11]
        dua_ref = refs[n_p + 11]
        dp_refs = refs[n_p + 12:n_p + 12 + n_p]
        carry_ref = refs[-1]
        i = pl.program_id(0)

        @pl.when(i == 0)
        def _():
            carry_ref[...] = jnp.zeros_like(carry_ref)
            for r in dp_refs:
                r[...] = jnp.zeros_like(r)

        ops = _pre_operands(ua_ref, prev_ref, i == nt - 1)
        _, vjp = jax.vjp(_rwkv_pre, *ops, *[p[...] for p in p_refs])
        c = [r[...] for r in c_refs]
        grads = vjp((c[0] + c[1], c[2], c[3], c[4] + c[5], c[6] + c[7], c[8], c[9], c[10]))
        d_ops, d_par = grads[:12], grads[12:]
        for r, val in zip(dp_refs, d_par):
            r[...] += val
        d_cur = jnp.concatenate([d_ops[0], d_ops[2], d_ops[4], d_ops[8], d_ops[10], d_ops[6]], axis=-1)
        d_sh = jnp.concatenate([d_ops[1], d_ops[3], d_ops[5], d_ops[9], d_ops[11], d_ops[7]], axis=-1)
        rows = lax.broadcasted_iota(jnp.int32, d_sh.shape, 0)
        dua = d_cur + jnp.where(rows == tile - 1, carry_ref[...], pltpu.roll(d_sh, tile - 1, axis=0))
        dua_ref[...] = dua.astype(BF16)
        carry_ref[...] = d_sh[0:1, :]

    rev = lambda i: nt - 1 - i
    tm = pl.BlockSpec((tile, DA), lambda i: (rev(i), 0))
    fixed = [pl.BlockSpec(p.shape, lambda i, nd=p.ndim: (0,) * nd) for p in pre_params]
    return pl.pallas_call(
        body, name="rwkv_pre_bwd", grid=(nt,),
        in_specs=_ua_specs(tile, rev) + fixed + [tm] * 11,
        out_specs=[pl.BlockSpec((tile, NA), lambda i: (rev(i), 0))] + fixed,
        out_shape=[jax.ShapeDtypeStruct((s, NA), BF16)] + [jax.ShapeDtypeStruct(p.shape, F32) for p in pre_params],
        scratch_shapes=[pltpu.VMEM((1, NA), F32)],
        compiler_params=_params("arbitrary"))(ua, ua, *pre_params, *cots)


def _wkv_fwd(seq):
    s = seq[0].shape[0]
    nc = s // WKV_CHUNK

    def body(r_ref, lw_ref, cl_ref, k_ref, v_ref, a_ref, b_ref, y_ref, ck_ref, p_ref, state):
        @pl.when(pl.program_id(0) == 0)
        def _():
            state[...] = jnp.zeros_like(state)

        r, lw, cl, k, v, a, b = (jnp.stack(_to_heads(ref[...])) for ref in (r_ref, lw_ref, cl_ref, k_ref, v_ref, a_ref,
                                                                             b_ref))
        s0 = state[...]
        ck_ref[0] = s0
        p = _tri_inverse(_wkv_aab(lw, cl, a, b))
        p_ref[0] = p
        y, s1 = _wkv_apply(s0, r, lw, cl, k, v, a, b, p)
        y_ref[...] = jnp.concatenate([y[h] for h in range(H)], axis=-1)
        state[...] = s1

    tm = pl.BlockSpec((WKV_CHUNK, DA), lambda c: (c, 0))
    per_chunk = lambda m: pl.BlockSpec((1, H, m, m), lambda c: (c, 0, 0, 0))
    return pl.pallas_call(
        body, name="wkv_fwd", grid=(nc,), in_specs=[tm] * 7,
        out_specs=[tm, per_chunk(N), per_chunk(WKV_CHUNK)],
        out_shape=[jax.ShapeDtypeStruct((s, DA), F32), jax.ShapeDtypeStruct((nc, H, N, N), F32),
                   jax.ShapeDtypeStruct((nc, H, WKV_CHUNK, WKV_CHUNK), F32)],
        scratch_shapes=[pltpu.VMEM((H, N, N), F32)], compiler_params=_params("arbitrary"))(*seq)


def _wkv_bwd(seq, ckpt, pinv, dy, slabs, owners):
    s = seq[0].shape[0]
    nc = s // WKV_CHUNK
    nx = len(slabs)

    def body(r_ref, lw_ref, cl_ref, k_ref, v_ref, a_ref, b_ref, ck_ref, p_ref, dy_ref, *refs):
        src_refs, d_refs, dst_refs = refs[:nx], refs[nx:nx + 7], refs[nx + 7:2 * nx + 7]
        dstate = refs[2 * nx + 7]
        start, wait = _exchange_ops(src_refs, dst_refs, owners, refs[2 * nx + 8:])

        @pl.when(pl.program_id(0) == 0)
        def _():
            dstate[...] = jnp.zeros_like(dstate)
            start()

        p = p_ref[0]
        r, lw, cl, k, v, a, b, dy = (jnp.stack(_to_heads(ref[...])) for ref in (r_ref, lw_ref, cl_ref, k_ref, v_ref,
                                                                                 a_ref, b_ref, dy_ref))
        _, vjp = jax.vjp(_wkv_apply, ck_ref[0], r, lw, cl, k, v, a, b, p)
        ds0, dr, dlw, dcl, dk, dv, da, db, dp = vjp((dy, dstate[...]))
        dstate[...] = ds0
        _, vjp_x = jax.vjp(_wkv_aab, lw, cl, a, b)
        dlw2, dcl2, da2, db2 = vjp_x(_dot1(_dot1(p, dp, "tn"), p, "nt"))
        for d_ref, val in zip(d_refs, (dr, dlw + dlw2, dcl + dcl2, dk, dv, da + da2, db + db2)):
            d_ref[...] = jnp.concatenate([val[h] for h in range(H)], axis=-1)

        @pl.when(pl.program_id(0) == nc - 1)
        def _():
            wait()

    tm = pl.BlockSpec((WKV_CHUNK, DA), lambda c: (nc - 1 - c, 0))
    per_chunk = lambda m: pl.BlockSpec((1, H, m, m), lambda c: (nc - 1 - c, 0, 0, 0))
    out = pl.pallas_call(
        body, name="wkv_bwd", grid=(nc,),
        in_specs=[tm] * 7 + [per_chunk(N), per_chunk(WKV_CHUNK), tm] + _hbm_specs(nx),
        out_specs=[tm] * 7 + _hbm_specs(nx),
        out_shape=[jax.ShapeDtypeStruct((s, DA), F32)] * 7 + _received_shapes(slabs, owners),
        scratch_shapes=[pltpu.VMEM((H, N, N), F32)] + _exchange_scratch(nx),
        compiler_params=_params("arbitrary"))(*seq, ckpt, pinv, dy, *slabs)
    return out[:7], out[7:]


def _rwkv_post_fwd(y, r, k2, v, g, post_params):
    s = y.shape[0]
    tile = TOK_TILE

    def body(*refs):
        refs[-1][...] = _rwkv_post(*[ref[...] for ref in refs[:-1]])

    tm = pl.BlockSpec((tile, DA), lambda i: (i, 0))
    par = pl.BlockSpec((1, DA), lambda i: (0, 0))
    return pl.pallas_call(
        body, name="rwkv_post_fwd", grid=(s // tile,), in_specs=[tm] * 5 + [par] * 3,
        out_specs=tm, out_shape=jax.ShapeDtypeStruct((s, DA), F32),
        compiler_params=_params("arbitrary"))(y, r, k2, v, g, *post_params)


def _rwkv_post_bwd(y, r, k2, v, g, post_params, dya, slabs, lo):
    s = y.shape[0]
    tile = HEAD_TILE

    def body(y_ref, r_ref, k_ref, v_ref, g_ref, w_ref, b_ref, rk_ref, dya_ref, s_ref, *refs):
        d_refs, p_ref = refs[:8], refs[8]
        start, wait = _pair_swap_ops(s_ref, p_ref, lo, refs[9:])

        @pl.when(pl.program_id(0) == 0)
        def _():
            for ref in d_refs[5:]:
                ref[...] = jnp.zeros_like(ref)
            start()

        _, vjp = jax.vjp(_rwkv_post, *[ref[...] for ref in (y_ref, r_ref, k_ref, v_ref, g_ref, w_ref, b_ref, rk_ref)])
        grads = vjp(dya_ref[...])
        for ref, val in zip(d_refs[:5], grads[:5]):
            ref[...] = val
        for ref, val in zip(d_refs[5:], grads[5:]):
            ref[...] += val

        @pl.when(pl.program_id(0) == s // tile - 1)
        def _():
            wait()

    tm = pl.BlockSpec((tile, DA), lambda i: (i, 0))
    par = pl.BlockSpec((1, DA), lambda i: (0, 0))
    return pl.pallas_call(
        body, name="rwkv_post_bwd", grid=(s // tile,),
        in_specs=[tm] * 5 + [par] * 3 + [tm] + _hbm_specs(1),
        out_specs=[tm] * 5 + [par] * 3 + _hbm_specs(1),
        out_shape=[jax.ShapeDtypeStruct((s, DA), F32)] * 5 + [jax.ShapeDtypeStruct((1, DA), F32)] * 3
        + [jax.ShapeDtypeStruct(slabs.shape, slabs.dtype)],
        scratch_shapes=_pair_swap_scratch(slabs.shape[0]),
        compiler_params=_params("arbitrary"))(y, r, k2, v, g, *post_params, dya, slabs)


def _tri(t):
    return (lax.broadcasted_iota(jnp.int32, (t, t), 0) >= lax.broadcasted_iota(jnp.int32, (t, t), 1)).astype(F32)


def _fox_pre_fwd(ub, uf, q_g, k_g, f_b):
    s = ub.shape[0]
    tile = HEAD_TILE

    def body(ub_ref, uf_ref, qg_ref, kg_ref, fb_ref, q_ref, k_ref, v_ref, cum_ref, carry):
        @pl.when(pl.program_id(0) == 0)
        def _():
            carry[...] = jnp.zeros_like(carry)

        qn, kn, logf = _fox_pre(ub_ref[:, :DA], ub_ref[:, DA:2 * DA], uf_ref[...], qg_ref[...], kg_ref[...],
                                fb_ref[...])
        for h, (q_col, k_col) in enumerate(zip(_to_heads(qn), _to_heads(kn))):
            q_ref[h] = q_col
            k_ref[h] = k_col
        v_ref[...] = _heads(ub_ref, 2 * DA)
        cum = jnp.dot(_tri(tile), logf, precision=HI, preferred_element_type=F32) + carry[...]
        cum_ref[...] = cum
        carry[...] = cum[tile - 1:tile, :]

    hm = pl.BlockSpec((H, tile, N), lambda i: (0, i, 0))
    fixed = lambda shape: pl.BlockSpec(shape, lambda i: (0,) * len(shape))
    return pl.pallas_call(
        body, name="fox_pre_fwd", grid=(s // tile,),
        in_specs=[pl.BlockSpec((tile, NB), lambda i: (i, 0)), pl.BlockSpec((tile, NF), lambda i: (i, 0)),
                  fixed((1, DA)), fixed((1, DA)), fixed((1, NF))],
        out_specs=[hm] * 3 + [pl.BlockSpec((tile, NF), lambda i: (i, 0))],
        out_shape=[jax.ShapeDtypeStruct((H, s, N), F32)] * 3 + [jax.ShapeDtypeStruct((s, NF), F32)],
        scratch_shapes=[pltpu.VMEM((1, NF), F32)], compiler_params=_params("arbitrary"))(ub, uf, q_g, k_g, f_b)


def _fox_pre_bwd(ub, uf, q_g, k_g, f_b, dqn, dkn, dvf, dgate, dcum_q, dcum_k):
    s = ub.shape[0]
    tile = HEAD_TILE
    nt = s // tile

    def body(ub_ref, uf_ref, qg_ref, kg_ref, fb_ref, dq_ref, dk_ref, dv_ref, dgate_ref, dcq_ref, dck_ref,
             dub_ref, duf_ref, dqg_ref, dkg_ref, dfb_ref, carry):
        @pl.when(pl.program_id(0) == 0)
        def _():
            carry[...] = jnp.zeros_like(carry)
            for ref in (dqg_ref, dkg_ref, dfb_ref):
                ref[...] = jnp.zeros_like(ref)

        dcum = dcq_ref[...] + dck_ref[...]
        dlogf = lax.dot_general(_tri(tile), dcum, (((0,), (0,)), ((), ())), precision=HI,
                                preferred_element_type=F32) + carry[...]
        carry[...] = dlogf[0:1, :]
        _, vjp = jax.vjp(_fox_pre, ub_ref[:, :DA], ub_ref[:, DA:2 * DA], uf_ref[...], qg_ref[...], kg_ref[...],
                         fb_ref[...])
        d_q, d_k, d_f, d_qg, d_kg, d_fb = vjp((_from_heads(dq_ref), _from_heads(dk_ref), dlogf))
        dub_ref[...] = jnp.concatenate([d_q, d_k, _from_heads(dv_ref), dgate_ref[...]], axis=-1).astype(BF16)
        duf_ref[...] = d_f.astype(BF16)
        dqg_ref[...] += functools.reduce(jnp.add, _to_heads(d_qg))
        dkg_ref[...] += functools.reduce(jnp.add, _to_heads(d_kg))
        dfb_ref[...] += d_fb

    rev = lambda i: nt - 1 - i
    hm = pl.BlockSpec((H, tile, N), lambda i: (0, rev(i), 0))
    tok = lambda n: pl.BlockSpec((tile, n), lambda i: (rev(i), 0))
    fixed = lambda shape: pl.BlockSpec(shape, lambda i: (0,) * len(shape))
    return pl.pallas_call(
        body, name="fox_pre_bwd", grid=(nt,),
        in_specs=[tok(NB), tok(NF), fixed((1, DA)), fixed((1, DA)), fixed((1, NF)), hm, hm, hm, tok(DA), tok(NF),
                  tok(NF)],
        out_specs=[tok(NB), tok(NF), fixed((1, N)), fixed((1, N)), fixed((1, NF))],
        out_shape=[jax.ShapeDtypeStruct((s, NB), BF16), jax.ShapeDtypeStruct((s, NF), BF16),
                   jax.ShapeDtypeStruct((1, N), F32), jax.ShapeDtypeStruct((1, N), F32),
                   jax.ShapeDtypeStruct((1, NF), F32)],
        scratch_shapes=[pltpu.VMEM((1, NF), F32)],
        compiler_params=_params("arbitrary"))(ub, uf, q_g, k_g, f_b, dqn, dkn, dvf, dgate, dcum_q, dcum_k)


def _att_groups(s):
    blocks = s // ATT_TILE
    per = max(1, blocks // ATT_GROUPS)
    return per, blocks // per


def _att_parts(n, width):
    return ([(0, n - width, False)] if n > width else []) + [(n - width, n, True)]


def _att_scores(q_bf, k_ref, ck_ref, lo, hi, masked, row_offset):
    scores = _bdot_nt(q_bf, k_ref[0, lo:hi, :]) - ck_ref[0, :, lo:hi]
    if masked:
        rows = row_offset + lax.broadcasted_iota(jnp.int32, scores.shape, 0)
        scores = jnp.where(rows >= lax.broadcasted_iota(jnp.int32, scores.shape, 1), scores, -1e30)
    return scores


def _fox_attn_fwd(q, k, v, cum_q, cum_k):
    s = q.shape[1]
    t = ATT_TILE
    per, groups = _att_groups(s)

    def body(q_ref, k_ref, v_ref, cq_ref, ck_ref, o_ref, lse_ref):
        qi = pl.program_id(1)
        for g in range(groups):
            @pl.when(qi // per == g)
            def _(g=g):
                q_bf = (q_ref[0] * ATT_SCALE).astype(BF16)
                parts = _att_parts((g + 1) * per * t, per * t)
                scores = [_att_scores(q_bf, k_ref, ck_ref, lo, hi, masked, (qi - g * per) * t)
                          for lo, hi, masked in parts]
                m = functools.reduce(jnp.maximum, [jnp.max(sc, axis=-1, keepdims=True) for sc in scores])
                l, acc = 0.0, 0.0
                for sc, (lo, hi, _) in zip(scores, parts):
                    p = jnp.exp(sc - m)
                    l += jnp.sum(p, axis=-1, keepdims=True)
                    acc += _bdot(p, v_ref[0, lo:hi, :])
                o_ref[0] = acc / l
                lse_ref[0] = m + jnp.log(l) + cq_ref[0]

    qb = pl.BlockSpec((1, t, N), lambda h, i: (h, i, 0))
    kb = pl.BlockSpec((1, s, N), lambda h, i: (h, 0, 0))
    return pl.pallas_call(
        body, name="fox_attn_fwd", grid=(H, s // t),
        in_specs=[qb, kb, kb, pl.BlockSpec((1, t, 1), lambda h, i: (h, i, 0)),
                  pl.BlockSpec((1, 1, s), lambda h, i: (h, 0, 0))],
        out_specs=[qb, pl.BlockSpec((1, t, 1), lambda h, i: (h, i, 0))],
        out_shape=[jax.ShapeDtypeStruct((H, s, N), F32), jax.ShapeDtypeStruct((H, s, 1), F32)],
        compiler_params=_params("arbitrary", "arbitrary"))(q, k, v, cum_q, cum_k)


def _fox_attn_bwd(q, k, v, cum_q, cum_k, o, lse, do, slabs, owners):
    s = q.shape[1]
    t = ATT_TILE
    per, groups = _att_groups(s)
    nx = len(slabs)

    def body(q_ref, k_ref, v_ref, cq_ref, ck_ref, o_ref, lse_ref, do_ref, *refs):
        src_refs, (dq_ref, dk_ref, dv_ref, dcq_ref, dck_ref) = refs[:nx], refs[nx:nx + 5]
        start, wait = _exchange_ops(src_refs, refs[nx + 5:2 * nx + 5], owners, refs[2 * nx + 5:])
        qi = pl.program_id(1)

        @pl.when((pl.program_id(0) == 0) & (qi == 0))
        def _():
            start()

        @pl.when(qi == 0)
        def _():
            for ref in (dk_ref, dv_ref, dck_ref):
                ref[...] = jnp.zeros_like(ref)

        for g in range(groups):
            @pl.when(qi // per == g)
            def _(g=g):
                q_bf, do_bf = (q_ref[0] * ATT_SCALE).astype(BF16), do_ref[0].astype(BF16)
                row_term = cq_ref[0] - lse_ref[0]
                delta = jnp.sum(do_ref[0] * o_ref[0], axis=-1, keepdims=True)
                dq, dcq = 0.0, 0.0
                for lo, hi, masked in _att_parts((g + 1) * per * t, per * t):
                    p = jnp.exp(_att_scores(q_bf, k_ref, ck_ref, lo, hi, masked, (qi - g * per) * t) + row_term)
                    ds = p * (_bdot_nt(do_bf, v_ref[0, lo:hi, :]) - delta)
                    dq += _bdot(ds, k_ref[0, lo:hi, :])
                    dcq += jnp.sum(ds, axis=-1, keepdims=True)
                    dk_ref[0, lo:hi, :] += _bdot_tn(ds, q_bf)
                    dv_ref[0, lo:hi, :] += _bdot_tn(p, do_bf)
                    dck_ref[0, :, lo:hi] -= jnp.sum(ds, axis=0, keepdims=True)
                dq_ref[0] = dq * ATT_SCALE
                dcq_ref[0] = dcq

        @pl.when((pl.program_id(0) == H - 1) & (qi == s // t - 1))
        def _():
            wait()

    qb = pl.BlockSpec((1, t, N), lambda h, i: (h, i, 0))
    kb = pl.BlockSpec((1, s, N), lambda h, i: (h, 0, 0))
    cqb = pl.BlockSpec((1, t, 1), lambda h, i: (h, i, 0))
    ckb = pl.BlockSpec((1, 1, s), lambda h, i: (h, 0, 0))
    f32 = lambda *shape: jax.ShapeDtypeStruct(shape, F32)
    out = pl.pallas_call(
        body, name="fox_attn_bwd", grid=(H, s // t),
        in_specs=[qb, kb, kb, cqb, ckb, qb, cqb, qb] + _hbm_specs(nx), out_specs=[qb, kb, kb, cqb, ckb] + _hbm_specs(nx),
        out_shape=[f32(H, s, N), f32(H, s, N), f32(H, s, N), f32(H, s, 1), f32(H, 1, s)]
        + _received_shapes(slabs, owners),
        scratch_shapes=_exchange_scratch(nx),
        compiler_params=_params("arbitrary", "arbitrary"))(q, k, v, cum_q, cum_k, o, lse, do, *slabs)
    return out[:5], out[5:]


def _local_step(x, target, w, p):
    mu = p["shift_mu"]
    lora_matrix = lambda a: jnp.moveaxis(a, 0, 1).reshape(RANK, DA).astype(F32)
    pre_params = (mu[:, 0:DA], mu[:, DA:2 * DA], mu[:, 2 * DA:3 * DA], mu[:, 3 * DA + 2 * RANK:],
                  mu[:, 3 * DA:3 * DA + RANK], mu[:, 3 * DA + RANK:3 * DA + 2 * RANK],
                  lora_matrix(w["w_lora_up"]), p["w0"], lora_matrix(w["a_lora_up"]), p["a0"], p["k_k"], p["k_a"])
    post_params = (p["lnx_w"], p["lnx_b"], p["r_k"])
    q_g, k_g = jnp.tile(p["q_norm_g"], (1, H)), jnp.tile(p["k_norm_g"], (1, H))
    f_b = jnp.pad(p["f_bias"], ((0, 0), (0, NF - H)))
    fg = p["final_norm_g"].reshape(1, D)

    h, (ua, ub, ug, uf) = _norm_proj(x, p["norm_g"], (w["in_a"], w["in_b"], w["in_g"], w["in_f"]))
    r, lw, cl, k2, v, av, bv, gg = _rwkv_pre_fwd(ua, pre_params)
    y, ckpt, pinv = _wkv_fwd((r, lw, cl, k2, v, av, bv))
    ya = _rwkv_post_fwd(y, r, k2, v, gg, post_params)
    qn, kn, vf, cum = _fox_pre_fwd(ub, uf, q_g, k_g, f_b)
    cum_t = cum[:, :H].T
    cum_q, cum_k = cum_t[:, :, None], cum_t[:, None, :]
    o, lse = _fox_attn_fwd(qn, kn, vf, cum_q, cum_k)

    (loss, dfg, dwo, dwoa, dwob, dx2, dya, do, dgate_b, dug) = _tail(
        x, target, ya, o, ub, ug, w["w_out_a"], w["w_out_b"], w["w_out"], fg)
    everyone = (0, N_DEV)
    (dqn, dkn, dvf, dcq, dck), (recv_woa, recv_wob, recv_wo) = _fox_attn_bwd(
        qn, kn, vf, cum_q, cum_k, o, lse, do,
        (_col_slabs(dwoa), _col_slabs(dwob), dwo.astype(BF16).reshape(N_DEV, D // N_DEV, D)), (everyone,) * 3)
    pad_f = lambda a: jnp.pad(a.T, ((0, 0), (0, NF - H)))
    dub, duf, dqg, dkg, dfb = _fox_pre_bwd(ub, uf, q_g, k_g, f_b, dqn, dkn, dvf, dgate_b,
                                           pad_f(dcq[:, :, 0]), pad_f(dck.reshape(H, -1)))
    spill = EARLY_FROM * COLS_PER_DEV - NA
    early, dwt_b_head = _proj_wgrad_early(h, dub, dug, duf, -(-spill // 8) * 8)
    dy, dr_p, dk_p, dv_p, dgg, dlnw, dlnb, drk, handed = _rwkv_post_bwd(y, r, k2, v, gg, post_params, dya, early,
                                                                          EARLY_FROM)
    early = _chip_sums(early, handed, EARLY_FROM, "chip_sums_early")
    (dr_s, dlw, dcl, dk_s, dv_s, dav, dbv), (recv_early,) = _wkv_bwd(
        (r, lw, cl, k2, v, av, bv), ckpt, pinv, dy, (early,), ((EARLY_FROM, N_DEV, "chips"),))
    pre_out = _rwkv_pre_bwd(ua, pre_params, (dr_s, dr_p, dlw, dcl, dk_s, dk_p, dv_s, dv_p, dav, dbv, dgg))
    dua, dpre = pre_out[0], pre_out[1:]
    late = _proj_wgrad_late(h, dua, dwt_b_head)

    flat = lambda a: a.reshape(1, -1)
    small = {
        "final_norm_g": dfg, "w0": dpre[7], "a0": dpre[9], "k_k": dpre[10], "k_a": dpre[11], "r_k": drk, "lnx_w": dlnw,
        "lnx_b": dlnb, "q_norm_g": dqg, "k_norm_g": dkg, "f_bias": dfb[:, :H],
        "shift_mu": jnp.concatenate([flat(dpre[0]), flat(dpre[1]), flat(dpre[2]), dpre[4], dpre[5], flat(dpre[3])], axis=1),
    }
    late = _chip_sums(late, _pair_swap(late, 0, "pair_swap_late"), 0, "chip_sums_late")
    by_head = lambda a: jnp.moveaxis(a.reshape(RANK, H, N), 1, 0)
    loras = jnp.stack([by_head(dpre[6]), by_head(dpre[8])], axis=1).astype(BF16)
    dx, dng, (recv_late, recv_lora, recv_small) = _proj_xgrad(
        x, p["norm_g"], dx2, (dua, dub, dug, duf), (w["in_a"], w["in_b"], w["in_g"], w["in_f"]),
        (late, loras, _pack_small(small, loss)), ((0, EARLY_FROM, "chips"), everyone, everyone))
    return dx, dng, (recv_early, recv_late), (recv_woa, recv_wob, recv_wo, recv_lora), recv_small


def _position():
    return lax.axis_index("x"), lax.axis_index("y"), lax.axis_index("c")


def _hbm_specs(n):
    return [pl.BlockSpec(memory_space=pl.ANY)] * n


BIG_GATHER_COPIES = 13
GATHER_ROW_CUT = 400


def _all_gather(big, blocks, name):
    n = len(blocks)

    def body(*refs):
        big_ref, x_refs = refs[0], refs[1:1 + n]
        big_out, out_refs = refs[1 + n], refs[2 + n:2 + 2 * n]
        send_sems, recv_sems, local_sems = refs[2 + 2 * n:]
        x, y, c = _position()
        me, sibling = (x, y, c), (x, y, 1 - c)
        chips = [(1 - x, y), (x, 1 - y), (1 - x, 1 - y)]
        x_nbr, y_nbr, diag = chips
        rows = big_ref.shape[0]
        cut = GATHER_ROW_CUT

        def part(ref, h):
            return ref if h is None else ref.at[pl.ds(0, cut)] if h == 0 else ref.at[pl.ds(cut, rows - cut)]

        def landed(chip, core, h):
            return part(big_out.at[4 * chip[0] + 2 * chip[1] + core], h)

        def big_copy(k, src, dst, to):
            return pltpu.make_async_remote_copy(src_ref=src, dst_ref=dst, send_sem=send_sems.at[7 * n + k],
                                                recv_sem=recv_sems.at[7 * n + k], device_id=to, device_id_type=MESH)

        def arrival(k, chip, core, h):
            dst = landed(chip, core, h)
            return big_copy(k, dst, dst, me)

        def pass_on(k, chip, h, to):
            src = landed(chip, c, h)
            return big_copy(k, src, src, to)

        big_mine = pltpu.make_async_copy(big_ref, landed((x, y), c, None), local_sems.at[n])
        big_mine.start()
        here = (x, y)
        big_sent = [big_copy(0, big_ref, landed(here, c, None), sibling),
                    big_copy(1, part(big_ref, 0), landed(here, c, 0), (*x_nbr, c)),
                    big_copy(2, part(big_ref, 1), landed(here, c, 1), (*y_nbr, c)),
                    big_copy(3, part(big_ref, 1), landed(here, c, 1), (*x_nbr, c)),
                    big_copy(4, part(big_ref, 0), landed(here, c, 0), (*y_nbr, c))]
        for cp in big_sent:
            cp.start()

        def copy(a, k, blk, to, own=False):
            dst = out_refs[a].at[4 * blk[0] + 2 * blk[1] + blk[2]]
            return pltpu.make_async_remote_copy(
                src_ref=x_refs[a] if own else dst, dst_ref=dst, send_sem=send_sems.at[7 * a + k],
                recv_sem=recv_sems.at[7 * a + k], device_id=to, device_id_type=MESH)

        mine = [pltpu.make_async_copy(x_refs[a], out_refs[a].at[4 * x + 2 * y + c], local_sems.at[a]) for a in range(n)]
        for cp in mine:
            cp.start()
        first = []
        for a in range(n):
            first.append(copy(a, 0, me, sibling, own=True))
            first += [copy(a, 1 + j, me, (*chip, c), own=True) for j, chip in enumerate(chips)]
        for cp in first:
            cp.start()

        big_steps = [(1, x_nbr, 0, (*y_nbr, c), 5, 7), (2, y_nbr, 1, (*x_nbr, c), 6, 8), (3, x_nbr, 1, None, None, 9),
                     (4, y_nbr, 0, None, None, 10), (5, diag, 0, None, None, 11), (6, diag, 1, None, None, 12)]
        for k, chip, h, onward, k_onward, k_sibling in big_steps:
            arrival(k, chip, c, h).wait_recv()
            if onward is not None:
                big_sent.append(pass_on(k_onward, chip, h, onward))
                big_sent[-1].start()
            big_sent.append(pass_on(k_sibling, chip, h, sibling))
            big_sent[-1].start()

        passed = []
        for j, chip in enumerate(chips):
            for a in range(n):
                copy(a, 1 + j, (*chip, c), me).wait_recv()
                passed.append(copy(a, 4 + j, (*chip, c), sibling))
                passed[-1].start()
        for a in range(n):
            copy(a, 0, sibling, me).wait_recv()
        for j, chip in enumerate(chips):
            for a in range(n):
                copy(a, 4 + j, (*chip, 1 - c), me).wait_recv()
        arrival(0, here, 1 - c, None).wait_recv()
        for k, chip, h, _, _, k_sibling in big_steps:
            arrival(k_sibling, chip, 1 - c, h).wait_recv()
        for cp in first + passed + big_sent:
            cp.wait_send()
        for cp in mine + [big_mine]:
            cp.wait()

    everything = [big] + list(blocks)
    return pl.pallas_call(
        body, name=name, out_shape=[jax.ShapeDtypeStruct((N_DEV,) + b.shape, b.dtype) for b in everything],
        in_specs=_hbm_specs(n + 1), out_specs=_hbm_specs(n + 1),
        scratch_shapes=[pltpu.SemaphoreType.DMA((7 * n + BIG_GATHER_COPIES,)),
                        pltpu.SemaphoreType.DMA((7 * n + BIG_GATHER_COPIES,)), pltpu.SemaphoreType.DMA((n + 1,))],
    )(*everything)


def _received_shapes(slabs, owners):
    return [jax.ShapeDtypeStruct((N_DEV // 2 if len(o) == 3 else N_DEV,) + s.shape[1:], s.dtype)
            for s, o in zip(slabs, owners)]


def _pair_swap_scratch(n):
    return [pltpu.SemaphoreType.DMA((n,)), pltpu.SemaphoreType.DMA((n,))]


def _pair_swap_ops(s_ref, p_ref, lo, sems):
    send_sems, recv_sems = sems
    n = s_ref.shape[0]

    def run(sending):
        x, y, c = _position()
        for side in (0, 1):
            mine = [pltpu.make_async_remote_copy(src_ref=s_ref.at[i], dst_ref=p_ref.at[i], send_sem=send_sems.at[i],
                                                 recv_sem=recv_sems.at[i], device_id=(x, y, 1 - c), device_id_type=MESH)
                    for i in range(n) if (lo + i) % 2 == side]

            @pl.when(c != side)
            def _():
                for cp in mine:
                    cp.start() if sending else cp.wait_send()

            if not sending:
                @pl.when(c == side)
                def _():
                    for cp in mine:
                        cp.wait_recv()

    return functools.partial(run, True), functools.partial(run, False)


def _pair_swap(slabs, lo, name):
    n = slabs.shape[0]

    def body(s_ref, p_ref, *sems):
        start, wait = _pair_swap_ops(s_ref, p_ref, lo, sems)
        start()
        wait()

    return pl.pallas_call(
        body, name=name, out_shape=jax.ShapeDtypeStruct(slabs.shape, slabs.dtype),
        in_specs=_hbm_specs(1), out_specs=_hbm_specs(1)[0], scratch_shapes=_pair_swap_scratch(n))(slabs)


def _chip_sums(slabs, swapped, lo, name):
    n, rows, cols = slabs.shape
    tile = W_IN_COL_TILE

    def body(s_ref, p_ref, o_ref):
        c = lax.axis_index("c")
        for i in range(n):
            @pl.when(c == (lo + i) % 2)
            def _(i=i):
                o_ref[i] = (s_ref[i].astype(F32) + p_ref[i].astype(F32)).astype(BF16)

    blk = pl.BlockSpec((n, rows, tile), lambda j: (0, 0, j))
    return pl.pallas_call(
        body, name=name, grid=(cols // tile,), in_specs=[blk, blk], out_specs=blk,
        out_shape=jax.ShapeDtypeStruct(slabs.shape, BF16), compiler_params=_params("arbitrary"))(slabs, swapped)


def _exchange_scratch(n):
    return [pltpu.SemaphoreType.DMA((7 * n,)), pltpu.SemaphoreType.DMA((7 * n,)), pltpu.SemaphoreType.DMA((n,))]


def _exchange_ops(src_refs, dst_refs, owners, sems):
    send_sems, recv_sems, local_sems = sems
    n = len(src_refs)

    def guarded(a, dev, fn):
        lo, hi = owners[a][:2]
        if (lo, hi) == (0, N_DEV):
            fn()
        else:
            pl.when((dev >= lo) & (dev < hi))(fn)

    def src(a, dev):
        ref = src_refs[a]
        return ref.at[0] if ref.shape[0] == 1 else ref.at[dev - owners[a][0]]

    def run(sending, waiting):
        x, y, c = _position()
        me = 4 * x + 2 * y + c
        for a in range(n):
            by_chip = len(owners[a]) == 3
            slot = (lambda qx, qy, qc: 2 * qx + qy) if by_chip else (lambda qx, qy, qc: 4 * qx + 2 * qy + qc)
            mine = slot(x, y, c)
            local = lambda a=a, mine=mine: pltpu.make_async_copy(src(a, me), dst_refs[a].at[mine], local_sems.at[a])
            if sending:
                guarded(a, me, lambda local=local: local().start())
            for m in range(2, N_DEV, 2) if by_chip else range(1, N_DEV):
                px, py, pc = x ^ (m >> 2), y ^ ((m >> 1) & 1), c ^ (m & 1)
                peer = 4 * px + 2 * py + pc
                theirs = slot(px, py, pc)
                sem = dict(send_sem=send_sems.at[7 * a + m - 1], recv_sem=recv_sems.at[7 * a + m - 1],
                           device_id=(px, py, pc), device_id_type=MESH)
                send = lambda a=a, peer=peer, sem=sem, mine=mine: pltpu.make_async_remote_copy(
                    src_ref=src(a, peer), dst_ref=dst_refs[a].at[mine], **sem)
                recv = lambda a=a, sem=sem, theirs=theirs: pltpu.make_async_remote_copy(
                    src_ref=src(a, me), dst_ref=dst_refs[a].at[theirs], **sem)
                if sending:
                    guarded(a, peer, lambda send=send: send().start())
                if waiting:
                    guarded(a, me, lambda recv=recv: recv().wait_recv())
                    guarded(a, peer, lambda send=send: send().wait_send())
            if waiting:
                guarded(a, me, lambda local=local: local().wait())

    return functools.partial(run, True, False), functools.partial(run, False, True)


def _sum_slabs(r_ref):
    g = r_ref[0].astype(F32)
    for k in range(1, r_ref.shape[0]):
        g = g + r_ref[k].astype(F32)
    return g


def _adamw(g, w, m, v):
    m_new = ADAM_B1 * m + (1.0 - ADAM_B1) * g
    v_new = ADAM_B2 * v + (1.0 - ADAM_B2) * (g * g)
    m_hat = m_new / (1.0 - ADAM_B1 ** ADAM_STEP)
    v_hat = v_new / (1.0 - ADAM_B2 ** ADAM_STEP)
    return g, -ADAM_LR * (m_hat / (jnp.sqrt(v_hat) + ADAM_EPS) + ADAM_WD * w), m_new, v_new


def _adamw_w_in(recv_early, recv_late, w, m, v, slabs, owners):
    rows, cols = w.shape
    tile = W_IN_COL_TILE
    nx = len(slabs)

    def body(early_ref, late_ref, w_ref, m_ref, v_ref, *refs):
        src_refs, o_refs, dst_refs = refs[:nx], refs[nx:nx + 4], refs[nx + 4:2 * nx + 4]
        start, wait = _exchange_ops(src_refs, dst_refs, owners, refs[2 * nx + 4:])
        x, y, c = _position()
        early_owner = 4 * x + 2 * y + c >= EARLY_FROM

        @pl.when(pl.program_id(0) == 0)
        def _():
            start()

        def update(g):
            for o_ref, val in zip(o_refs, _adamw(g, w_ref[...], m_ref[...], v_ref[...])):
                o_ref[...] = val

        pl.when(early_owner)(lambda: update(_sum_slabs(early_ref)))
        pl.when(jnp.logical_not(early_owner))(lambda: update(_sum_slabs(late_ref)))

        @pl.when(pl.program_id(0) == cols // tile - 1)
        def _():
            wait()

    blk = pl.BlockSpec((rows, tile), lambda i: (0, i))
    slots = lambda r: pl.BlockSpec((r.shape[0], rows, tile), lambda i: (0, 0, i))
    out = pl.pallas_call(
        body, name="adamw_w_in", grid=(cols // tile,),
        in_specs=[slots(recv_early), slots(recv_late), blk, blk, blk] + _hbm_specs(nx),
        out_specs=[blk] * 4 + _hbm_specs(nx),
        out_shape=[jax.ShapeDtypeStruct((rows, cols), F32)] * 4 + _received_shapes(slabs, owners),
        scratch_shapes=_exchange_scratch(nx),
        compiler_params=_params("arbitrary"))(recv_early, recv_late, w, m, v, *slabs)
    return out[:4], out[4:]


def _adamw_misc(recvs, recv_small, recv_norm, params):
    names = list(params)
    flat = [a for n in names for a in params[n]]

    def body(woa_ref, wob_ref, wo_ref, lora_ref, small_ref, norm_ref, *refs):
        p_refs, o_refs = refs[:len(flat)], refs[len(flat):]
        g_small = _sum_slabs(small_ref)
        g_lora = _sum_slabs(lora_ref)
        grads = {"w_out_a": _sum_slabs(woa_ref), "w_out_b": _sum_slabs(wob_ref), "w_out": _sum_slabs(wo_ref),
                 "w_lora_up": g_lora[0], "a_lora_up": g_lora[1], "norm_g": _sum_slabs(norm_ref)}
        for n, (off, size) in SMALL_SLOTS.items():
            grads[n] = g_small[:, off:off + size]
        for i, n in enumerate(names):
            w_ref, m_ref, v_ref = p_refs[3 * i:3 * i + 3]
            for o_ref, val in zip(o_refs[4 * i:4 * i + 4], _adamw(grads[n], w_ref[...], m_ref[...], v_ref[...])):
                o_ref[...] = val
        o_refs[-1][...] = g_small[:, LOSS_SLOT:LOSS_SLOT + 1]

    out = pl.pallas_call(
        body, name="adamw_misc",
        out_shape=[jax.ShapeDtypeStruct(params[n][0].shape, F32) for n in names for _ in range(4)]
        + [jax.ShapeDtypeStruct((1, 1), F32)],
        compiler_params=_params())(*recvs, recv_small, recv_norm, *flat)
    return {n: out[4 * i:4 * i + 4] for i, n in enumerate(names)}, out[-1]


_WT_SEGMENTS = ((0, NA), (NA, NB), (NA + NB + H, NG), (NA + NB, H))


def _split_wt(gathered):
    tile = W_IN_COL_TILE

    def body(g_ref, *o_refs):
        full = jnp.concatenate([g_ref[j] for j in range(N_DEV)], axis=0)
        for o_ref, (row, n) in zip(o_refs, _WT_SEGMENTS):
            seg = full[row:row + n]
            if n < o_ref.shape[0]:
                seg = jnp.concatenate([seg, jnp.zeros((o_ref.shape[0] - n, tile), BF16)], axis=0)
            o_ref[...] = seg

    sizes = (NA, NB, NG, NF)
    return pl.pallas_call(
        body, name="split_wt", grid=(D // tile,),
        in_specs=[pl.BlockSpec((N_DEV, COLS_PER_DEV, tile), lambda i: (0, 0, i))],
        out_specs=[pl.BlockSpec((n, tile), lambda i: (0, i)) for n in sizes],
        out_shape=[jax.ShapeDtypeStruct((n, D), BF16) for n in sizes],
        compiler_params=_params("arbitrary"))(gathered)


def _by_cols(a):
    return jnp.moveaxis(a, 0, 1).reshape(a.shape[1], -1)


def _col_slabs(a):
    return jnp.moveaxis(a.reshape(a.shape[0], N_DEV, -1), 1, 0).astype(BF16)


def _pack_small(grads, loss):
    pieces, at = [], 0
    for n, (off, size) in list(SMALL_SLOTS.items()) + [("loss", (LOSS_SLOT, 1))]:
        pieces += [jnp.zeros((off - at,), F32), (loss if n == "loss" else grads[n]).reshape(-1)]
        at = off + size
    return jnp.concatenate(pieces + [jnp.zeros((SMALL_LEN - at,), F32)]).reshape(1, 1, SMALL_LEN)


def _gather_weights(t):
    cast = lambda a: a.astype(BF16)
    loras = jnp.stack([t["w_lora_up"][0], t["a_lora_up"][0]])
    wt, woa, wob, wo, lora = _all_gather(
        cast(t["w_in"][0].T), [cast(t["w_out_a"][0]), cast(t["w_out_b"][0]), cast(t["w_out"][0]), cast(loras)],
        "weight_gather")
    in_a, in_b, in_g, in_f = _split_wt(wt)
    return {"in_a": in_a, "in_b": in_b, "in_g": in_g, "in_f": in_f, "w_out_a": _by_cols(woa), "w_out_b": _by_cols(wob),
            "w_out": wo.reshape(D, D), "w_lora_up": lora[:, 0], "a_lora_up": lora[:, 1]}


def kernel(x, norm_g, w_in, shift_mu, w_lora_up, w0, a_lora_up, a0, k_k, k_a, r_k, lnx_w, lnx_b, f_bias, q_norm_g, k_norm_g, w_out_a, w_out_b, w_out, final_norm_g, loss_target, m_norm_g, m_w_in, m_shift_mu, m_w_lora_up, m_w0, m_a_lora_up, m_a0, m_k_k, m_k_a, m_r_k, m_lnx_w, m_lnx_b, m_f_bias, m_q_norm_g, m_k_norm_g, m_w_out_a, m_w_out_b, m_w_out, m_final_norm_g, v_norm_g, v_w_in, v_shift_mu, v_w_lora_up, v_w0, v_a_lora_up, v_a0, v_k_k, v_k_a, v_r_k, v_lnx_w, v_lnx_b, v_f_bias, v_q_norm_g, v_k_norm_g, v_w_out_a, v_w_out_b, v_w_out, v_final_norm_g):
    names = ("norm_g", "w_in", "shift_mu", "w_lora_up", "w0", "a_lora_up", "a0", "k_k", "k_a", "r_k", "lnx_w", "lnx_b",
             "f_bias", "q_norm_g", "k_norm_g", "w_out_a", "w_out_b", "w_out", "final_norm_g")
    weights = dict(zip(names, (norm_g, w_in, shift_mu, w_lora_up, w0, a_lora_up, a0, k_k, k_a, r_k, lnx_w, lnx_b,
                               f_bias, q_norm_g, k_norm_g, w_out_a, w_out_b, w_out, final_norm_g)))
    m_in = dict(zip(names, (m_norm_g, m_w_in, m_shift_mu, m_w_lora_up, m_w0, m_a_lora_up, m_a0, m_k_k, m_k_a, m_r_k,
                            m_lnx_w, m_lnx_b, m_f_bias, m_q_norm_g, m_k_norm_g, m_w_out_a, m_w_out_b, m_w_out,
                            m_final_norm_g)))
    v_in = dict(zip(names, (v_norm_g, v_w_in, v_shift_mu, v_w_lora_up, v_w0, v_a_lora_up, v_a0, v_k_k, v_k_a, v_r_k,
                            v_lnx_w, v_lnx_b, v_f_bias, v_q_norm_g, v_k_norm_g, v_w_out_a, v_w_out_b, v_w_out,
                            v_final_norm_g)))

    matrices = ("w_out_a", "w_out_b", "w_out", "w_lora_up", "a_lora_up")
    as_2d = lambda n, a: a[0] if n in matrices else a.reshape(1, -1)

    full = _gather_weights(weights)
    dx, dng, recv_wt, recvs, recv_small = _local_step(
        x[0], loss_target[0], full, {n: as_2d(n, weights[n]) for n in ("norm_g",) + tuple(SMALL_SLOTS)})

    res, (recv_norm,) = _adamw_w_in(*recv_wt, w_in[0].T, m_w_in[0].T, v_w_in[0].T, (dng[None],), ((0, N_DEV),))
    outs = {"w_in": [r.T[None] for r in res]}
    misc = [n for n in names if n != "w_in"]
    res, loss_sum = _adamw_misc(recvs, recv_small, recv_norm,
                                {n: tuple(as_2d(n, t[n]) for t in (weights, m_in, v_in)) for n in misc})
    for n in misc:
        outs[n] = [r.reshape(weights[n].shape) for r in res[n]]
    return (loss_sum.reshape(()), dx[None], *[outs[n][i] for i in range(4) for n in names])
```

```python
import functools
import math

import jax
import jax.numpy as jnp
from jax import lax
from jax.experimental import pallas as pl
from jax.experimental.pallas import tpu as pltpu

F32 = jnp.float32
BF16 = jnp.bfloat16
HI = lax.Precision.HIGHEST
MESH = pl.DeviceIdType.MESH

N_DEV = 8
D = 1024
H = 8
N = 64
DA = H * N
RANK = 64
NA = 4 * DA + 2 * RANK
NB = 4 * DA
NG = 2 * D
NF = 128
IN_COLS = NA + NB + H + NG
COLS_PER_DEV = IN_COLS // N_DEV
RMS_EPS = 1e-6
LNX_EPS = 64e-5
ATT_SCALE = N ** -0.5

ADAM_LR = 0.001
ADAM_B1 = 0.9
ADAM_B2 = 0.999
ADAM_EPS = 1e-08
ADAM_WD = 0.01
ADAM_STEP = 10

LANES = 128
WKV_CHUNK = 64
WKV_STEP_CHUNKS = 2
TOK_TILE = 256
HEAD_TILE = 256
XGRAD_TILE = 128
WGRAD_TILE = 512
ATT_TILE = 256
ATT_GROUPS = 8
VMEM_LIMIT = 56 * 1024 * 1024


def _lane_tile_slots(sizes):
    slots, at = {}, 0
    for name, size in sizes:
        slots[name] = (at, size)
        at += -(-size // LANES) * LANES
    return slots, at


SMALL_SLOTS, LOSS_SLOT = _lane_tile_slots((
    ("final_norm_g", D), ("shift_mu", NA), ("w0", DA), ("a0", DA), ("k_k", DA), ("k_a", DA), ("r_k", DA), ("lnx_w", DA),
    ("lnx_b", DA), ("q_norm_g", N), ("k_norm_g", N), ("f_bias", H)))
SMALL_LEN = LOSS_SLOT + LANES
W_IN_COL_TILE = 512
EARLY_FROM = -(-NA // COLS_PER_DEV)


def _params(*sem):
    return pltpu.CompilerParams(dimension_semantics=sem or None, vmem_limit_bytes=VMEM_LIMIT)


def _bdot(a, b):
    return jnp.dot(a.astype(BF16), b.astype(BF16), preferred_element_type=F32)


def _bdot_nt(a, b):
    return lax.dot_general(a.astype(BF16), b.astype(BF16), (((1,), (1,)), ((), ())), preferred_element_type=F32)


def _bdot_tn(a, b):
    return lax.dot_general(a.astype(BF16), b.astype(BF16), (((0,), (0,)), ((), ())), preferred_element_type=F32)


def _sigmoid(x):
    return 1.0 / (1.0 + jnp.exp(-x))


def _softplus(x):
    return jnp.maximum(x, 0.0) + jnp.log(1.0 + jnp.exp(-jnp.abs(x)))


def _heads(ref, col0):
    return jnp.stack([ref[:, col0 + N * h:col0 + N * (h + 1)] for h in range(H)])


def _lerp(c, s, mu):
    return c + (s - c) * mu


def _head_sums(x):
    low = lax.broadcasted_iota(jnp.int32, (x.shape[0], LANES), 1) < N
    out = []
    for p in range(x.shape[1] // LANES):
        pair = x[:, LANES * p:LANES * (p + 1)]
        first = jnp.sum(jnp.where(low, pair, 0.0), axis=-1, keepdims=True)
        second = jnp.sum(jnp.where(low, 0.0, pair), axis=-1, keepdims=True)
        out.append(jnp.where(low, first, second))
    return jnp.concatenate(out, axis=-1)


def _to_heads(x):
    return [x[:, N * h:N * (h + 1)] for h in range(H)]


def _from_heads(ref):
    return jnp.concatenate([ref[h] for h in range(H)], axis=-1)


def _rwkv_pre(rc, rs, kc, ks, vc, vs, gc, gs, wdc, wds, adc, ads,
              mu_r, mu_k, mu_v, mu_g, mu_wd, mu_ad, w_up, w0, a_up, a0, k_k, k_a):
    r = _lerp(rc, rs, mu_r)
    k = _lerp(kc, ks, mu_k)
    v = _lerp(vc, vs, mu_v)
    g = _lerp(gc, gs, mu_g)
    wd = _lerp(wdc, wds, mu_wd)
    ad = _lerp(adc, ads, mu_ad)
    t = wd.shape[0]
    w_raw = -_softplus(-(w0 + _bdot(jnp.tanh(wd), w_up))) - 0.5
    lw = -jnp.exp(w_raw)
    row = lax.broadcasted_iota(jnp.int32, (t, t), 0)
    col = lax.broadcasted_iota(jnp.int32, (t, t), 1)
    same_chunk = ((row >= col) & (row // WKV_CHUNK == col // WKV_CHUNK)).astype(F32)
    cl = jnp.dot(same_chunk, lw, precision=HI, preferred_element_type=F32)
    alr = _sigmoid(a0 + _bdot(ad, a_up))
    kk = k * k_k
    kk = kk / jnp.maximum(jnp.sqrt(_head_sums(kk * kk)), 1e-12)
    k2 = k * (1.0 + (alr - 1.0) * k_a)
    return r, lw, cl, k2, v, -kk, kk * alr, g


_MM_DIMS = {"nn": (((2,), (1,)), ((0,), (0,))), "nt": (((2,), (2,)), ((0,), (0,))), "tn": (((1,), (1,)), ((0,), (0,)))}


def _dot1(a, b, kind):
    return lax.dot_general(a.astype(BF16), b.astype(BF16), dimension_numbers=_MM_DIMS[kind], preferred_element_type=F32)


@functools.partial(jax.custom_vjp, nondiff_argnums=(2,))
def _mm(a, b, kind):
    return _dot1(a, b, kind)


def _mm_fwd(a, b, kind):
    return _dot1(a, b, kind), (a, b)


def _mm_bwd(kind, res, ct):
    a, b = res
    if kind == "nn":
        return _dot1(ct, b, "nt"), _dot1(a, ct, "tn")
    if kind == "nt":
        return _dot1(ct, b, "nn"), _dot1(ct, a, "tn")
    return _dot1(b, ct, "nt"), _dot1(a, ct, "nn")


_mm.defvjp(_mm_fwd, _mm_bwd)


def _chunk_masks(c):
    row = lax.broadcasted_iota(jnp.int32, (c, c), 0)
    col = lax.broadcasted_iota(jnp.int32, (c, c), 1)
    return (row >= col)[None], (row > col)[None], (row == col).astype(F32)[None]


def _wkv_aab(lw, cl, a, b):
    _, strict, _ = _chunk_masks(a.shape[1])
    return jnp.where(strict, _mm(a * jnp.exp(cl - lw), b * jnp.exp(-cl), "nt"), 0.0)


def _tri_inverse(x):
    c = x.shape[1]
    p = _chunk_masks(c)[2] + x
    for _ in range(int(math.log2(c)) - 1):
        x = _dot1(x, x, "nn")
        p = p + _dot1(p, x, "nn")
    return p


def _wkv_apply(s0, r, lw, cl, k, v, a, b, p):
    c = r.shape[1]
    incl, strict, _ = _chunk_masks(c)
    gi = jnp.exp(-cl)
    left = jnp.concatenate([a * jnp.exp(cl - lw), r * jnp.exp(cl)], axis=1)
    right = jnp.concatenate([b * gi, k * gi], axis=1)
    m = _mm(left, right, "nt")
    z0 = _mm(left, s0, "nt")
    a_ak = jnp.where(strict, m[:, :c, c:], 0.0)
    row = lax.broadcasted_iota(jnp.int32, (c, 2 * c), 0)
    col = lax.broadcasted_iota(jnp.int32, (c, 2 * c), 1)
    a_r = jnp.where((row >= col % c)[None], m[:, c:, :], 0.0)
    sa = _mm(p, z0[:, :c] + _mm(a_ak, v, "nn"), "nn")
    sa_v = jnp.concatenate([sa, v], axis=1)
    y = z0[:, c:] + _mm(a_r, sa_v, "nn")
    s1 = (s0 + _mm(sa_v, right, "tn")) * jnp.exp(cl[:, c - 1:c, :])
    return y, s1


def _rwkv_post(y, r, k2, v, g, lnx_w, lnx_b, r_k):
    yc = y - _head_sums(y) * (1.0 / N)
    var = _head_sums(yc * yc) * (1.0 / N)
    yn = yc * lax.rsqrt(var + LNX_EPS) * lnx_w + lnx_b
    bonus = _head_sums(r * k2 * r_k) * v
    return (yn + bonus) * (g * _sigmoid(g))


def _fox_pre(q, k, f, q_g, k_g, f_b):
    qn = q * lax.rsqrt(_head_sums(q * q) * (1.0 / N) + RMS_EPS) * q_g
    kn = k * lax.rsqrt(_head_sums(k * k) * (1.0 / N) + RMS_EPS) * k_g
    x = f + f_b
    return qn, kn, jnp.minimum(x, 0.0) - jnp.log(1.0 + jnp.exp(-jnp.abs(x)))


def _norm_proj(x, g, wts):
    s = x.shape[0]
    k = len(wts)

    def body(x_ref, g_ref, *refs):
        w_refs, h_ref, o_refs = refs[:k], refs[k], refs[k + 1:]
        xv = x_ref[...]
        h = (xv * lax.rsqrt(jnp.mean(xv * xv, axis=-1, keepdims=True) + RMS_EPS) * g_ref[...]).astype(BF16)
        h_ref[...] = h
        for w_ref, o_ref in zip(w_refs, o_refs):
            o_ref[...] = _bdot_nt(h, w_ref[...])

    tok = lambda n: pl.BlockSpec((TOK_TILE, n), lambda i: (i, 0))
    out = pl.pallas_call(
        body, name="norm_proj", grid=(s // TOK_TILE,),
        in_specs=[tok(D), pl.BlockSpec((1, D), lambda i: (0, 0))] + [pl.BlockSpec(w.shape, lambda i: (0, 0)) for w in wts],
        out_specs=[tok(D)] + [tok(w.shape[0]) for w in wts],
        out_shape=[jax.ShapeDtypeStruct((s, D), BF16)] + [jax.ShapeDtypeStruct((s, w.shape[0]), F32) for w in wts],
        compiler_params=_params("arbitrary"))(x, g, *wts)
    return out[0], out[1:]


def _proj_wgrad_early(h, dub, dug, duf, head_rows):
    s = dub.shape[0]
    steps = s // WGRAD_TILE
    seg_rows = (_WT_SEGMENTS[1], _WT_SEGMENTS[2], _WT_SEGMENTS[3])

    def body(h_ref, b_ref, g_ref, f_ref, o_ref, head_ref, *accs):
        @pl.when(pl.program_id(0) == 0)
        def _():
            for acc in accs:
                acc[...] = jnp.zeros_like(acc)

        h = h_ref[...]
        for acc, du_ref in zip(accs, (b_ref, g_ref, f_ref)):
            acc[...] += _bdot_tn(du_ref[...], h)

        @pl.when(pl.program_id(0) == steps - 1)
        def _():
            head_ref[...] = accs[0][:head_rows, :]
            for j in range(EARLY_FROM, N_DEV):
                lo, hi = COLS_PER_DEV * j, COLS_PER_DEV * (j + 1)
                parts = []
                for acc, (row, n) in sorted(zip(accs, seg_rows), key=lambda t: t[1][0]):
                    first, last = max(lo, row), min(hi, row + n)
                    if first < last:
                        parts.append(acc[first - row:last - row, :])
                o_ref[j - EARLY_FROM] = (parts[0] if len(parts) == 1 else jnp.concatenate(parts, axis=0)).astype(BF16)

    tok = lambda n: pl.BlockSpec((WGRAD_TILE, n), lambda i: (i, 0))
    n_early = N_DEV - EARLY_FROM
    return pl.pallas_call(
        body, name="wgrad_bgf", grid=(steps,), in_specs=[tok(D), tok(NB), tok(NG), tok(NF)],
        out_specs=[pl.BlockSpec((n_early, COLS_PER_DEV, D), lambda i: (0, 0, 0)),
                   pl.BlockSpec((head_rows, D), lambda i: (0, 0))],
        out_shape=[jax.ShapeDtypeStruct((n_early, COLS_PER_DEV, D), BF16), jax.ShapeDtypeStruct((head_rows, D), F32)],
        scratch_shapes=[pltpu.VMEM((n, D), F32) for n in (NB, NG, NF)],
        compiler_params=_params("arbitrary"))(h, dub, dug, duf)


def _proj_wgrad_late(h, dua, dwt_b_head):
    s = dua.shape[0]
    steps = s // WGRAD_TILE

    def body(h_ref, du_ref, b_ref, o_ref, acc):
        @pl.when(pl.program_id(0) == 0)
        def _():
            acc[...] = jnp.zeros_like(acc)

        acc[...] += _bdot_tn(du_ref[...], h_ref[...])

        @pl.when(pl.program_id(0) == steps - 1)
        def _():
            for j in range(EARLY_FROM):
                lo, hi = COLS_PER_DEV * j, COLS_PER_DEV * (j + 1)
                parts = [acc[lo:min(hi, NA), :]] + ([b_ref[:hi - NA, :]] if hi > NA else [])
                o_ref[j] = (parts[0] if len(parts) == 1 else jnp.concatenate(parts, axis=0)).astype(BF16)

    return pl.pallas_call(
        body, name="wgrad_a", grid=(steps,),
        in_specs=[pl.BlockSpec((WGRAD_TILE, D), lambda i: (i, 0)), pl.BlockSpec((WGRAD_TILE, NA), lambda i: (i, 0)),
                  pl.BlockSpec(dwt_b_head.shape, lambda i: (0, 0))],
        out_specs=pl.BlockSpec((EARLY_FROM, COLS_PER_DEV, D), lambda i: (0, 0, 0)),
        out_shape=jax.ShapeDtypeStruct((EARLY_FROM, COLS_PER_DEV, D), BF16),
        scratch_shapes=[pltpu.VMEM((NA, D), F32)], compiler_params=_params("arbitrary"))(h, dua, dwt_b_head)


def _proj_xgrad(x, g, dx2, dus, ws, slabs, owners):
    s = x.shape[0]
    tile = XGRAD_TILE
    k = len(dus)
    nx = len(slabs)
    n_in = 3 + 2 * k + nx

    def body(*refs):
        x_ref, g_ref, dx2_ref = refs[:3]
        du_refs, w_refs = refs[3:3 + k], refs[3 + k:3 + 2 * k]
        src_refs = refs[3 + 2 * k:3 + 2 * k + nx]
        dx_ref, dg_ref = refs[n_in:n_in + 2]
        dst_refs = refs[n_in + 2:n_in + 2 + nx]
        start, wait = _exchange_ops(src_refs, dst_refs, owners, refs[n_in + 2 + nx:])

        @pl.when(pl.program_id(0) == 0)
        def _():
            dg_ref[...] = jnp.zeros_like(dg_ref)
            start()

        dh = _bdot(du_refs[0][...], w_refs[0][...])
        for du_ref, w_ref in zip(du_refs[1:], w_refs[1:]):
            dh += _bdot(du_ref[...], w_ref[...])
        xv = x_ref[...]
        rs = lax.rsqrt(jnp.mean(xv * xv, axis=-1, keepdims=True) + RMS_EPS)
        xn = xv * rs
        dg_ref[...] += jnp.sum(dh * xn, axis=0, keepdims=True)
        dxn = dh * g_ref[...]
        dx_ref[...] = rs * (dxn - xn * jnp.mean(dxn * xn, axis=-1, keepdims=True)) + dx2_ref[...]

        @pl.when(pl.program_id(0) == s // tile - 1)
        def _():
            wait()

    tok = lambda n: pl.BlockSpec((tile, n), lambda i: (i, 0))
    fixed = lambda a: pl.BlockSpec(a.shape, lambda i: (0,) * a.ndim)
    out = pl.pallas_call(
        body, name="proj_xgrad", grid=(s // tile,),
        in_specs=([tok(D), fixed(g), tok(D)] + [tok(du.shape[1]) for du in dus] + [fixed(w) for w in ws]
                  + _hbm_specs(nx)),
        out_specs=[tok(D), pl.BlockSpec((1, D), lambda i: (0, 0))] + _hbm_specs(nx),
        out_shape=[jax.ShapeDtypeStruct((s, D), F32), jax.ShapeDtypeStruct((1, D), F32)] + _received_shapes(slabs, owners),
        scratch_shapes=_exchange_scratch(nx),
        compiler_params=_params("arbitrary"))(x, g, dx2, *dus, *ws, *slabs)
    return out[0], out[1], out[2:]


def _tail(x, target, ya, o, ub, ug, w_oa, w_ob, w_o, fg):
    s = x.shape[0]
    tile = TOK_TILE

    def body(x_ref, t_ref, ya_ref, o_ref, gb_ref, ug_ref, woa_ref, wob_ref, wo_ref, fg_ref,
             loss_ref, dfg_ref, dwo_ref, dwoa_ref, dwob_ref, dx2_ref, dya_ref, do_ref, dgb_ref, dug_ref):
        @pl.when(pl.program_id(0) == 0)
        def _():
            for r in (loss_ref, dfg_ref, dwo_ref, dwoa_ref, dwob_ref):
                r[...] = jnp.zeros_like(r)

        ya_v = ya_ref[...]
        gate_b = gb_ref[...]
        sg_b = _sigmoid(gate_b)
        silu_b = gate_b * sg_b
        o_v = jnp.concatenate([o_ref[h] for h in range(H)], axis=-1)
        yb_v = o_v * silu_b
        big_a = _bdot(ya_v, woa_ref[...])
        big_b = _bdot(yb_v, wob_ref[...])
        sa = _sigmoid(ug_ref[:, :D])
        sb = _sigmoid(ug_ref[:, D:])
        merged = sa * big_a + sb * big_b
        x2 = x_ref[...] + _bdot(merged, wo_ref[...])
        rs = lax.rsqrt(jnp.mean(x2 * x2, axis=-1, keepdims=True) + RMS_EPS)
        xn = x2 * rs
        err = xn * fg_ref[...] - t_ref[...]
        loss_ref[...] += (0.5 / D) * jnp.sum(err * err)
        dout = err * (1.0 / D)
        dfg_ref[...] += jnp.sum(dout * xn, axis=0, keepdims=True)
        dxn = dout * fg_ref[...]
        dx2 = rs * (dxn - xn * jnp.mean(dxn * xn, axis=-1, keepdims=True))
        dx2_ref[...] = dx2
        dwo_ref[...] += _bdot_tn(merged, dx2)
        dmerged = _bdot_nt(dx2, wo_ref[...])
        dbig_a = dmerged * sa
        dbig_b = dmerged * sb
        dug_ref[:, :D] = (dmerged * big_a * sa * (1.0 - sa)).astype(BF16)
        dug_ref[:, D:] = (dmerged * big_b * sb * (1.0 - sb)).astype(BF16)
        dwoa_ref[...] += _bdot_tn(ya_v, dbig_a)
        dwob_ref[...] += _bdot_tn(yb_v, dbig_b)
        dya_ref[...] = _bdot_nt(dbig_a, woa_ref[...])
        dyb = _bdot_nt(dbig_b, wob_ref[...])
        dgb_ref[...] = dyb * o_v * (sg_b * (1.0 + gate_b * (1.0 - sg_b)))
        _dov = dyb * silu_b
        for h in range(H):
            do_ref[h] = _dov[:, N * h:N * (h + 1)]

    tok = lambda n: pl.BlockSpec((tile, n), lambda i: (i, 0))
    hm = pl.BlockSpec((H, tile, N), lambda i: (0, i, 0))
    fixed = lambda shape: pl.BlockSpec(shape, lambda i: (0,) * len(shape))
    f32 = lambda *shape: jax.ShapeDtypeStruct(shape, F32)
    return pl.pallas_call(
        body, name="tail", grid=(s // tile,),
        in_specs=[tok(D), tok(D), tok(DA), hm, pl.BlockSpec((tile, DA), lambda i: (i, 3)), tok(NG),
                  fixed((DA, D)), fixed((DA, D)), fixed((D, D)), fixed((1, D))],
        out_specs=[fixed((1, 1)), fixed((1, D)), fixed((D, D)), fixed((DA, D)), fixed((DA, D)),
                   tok(D), tok(DA), hm, tok(DA), tok(NG)],
        out_shape=[f32(1, 1), f32(1, D), f32(D, D), f32(DA, D), f32(DA, D),
                   f32(s, D), f32(s, DA), f32(H, s, N), f32(s, DA), jax.ShapeDtypeStruct((s, NG), BF16)],
        compiler_params=_params("arbitrary"))(x, target, ya, o, ub, ug, w_oa, w_ob, w_o, fg)


def _pre_operands(ua_ref, prev_ref, first):
    cur = ua_ref[...]
    t = cur.shape[0]
    prev_row = jnp.where(first, 0.0, prev_ref[7:8, :])
    rows = lax.broadcasted_iota(jnp.int32, cur.shape, 0)
    sh = jnp.where(rows == 0, prev_row, pltpu.roll(cur, 1, axis=0))
    ops = []
    for c0, n in ((0, DA), (DA, DA), (2 * DA, DA), (3 * DA + 2 * RANK, DA), (3 * DA, RANK), (3 * DA + RANK, RANK)):
        ops += [cur[:, c0:c0 + n], sh[:, c0:c0 + n]]
    del t
    return ops


def _ua_specs(tile, order):
    blocks = tile // 8
    return [pl.BlockSpec((tile, NA), lambda i: (order(i), 0)),
            pl.BlockSpec((8, NA), lambda i: (jnp.maximum(order(i) * blocks - 1, 0), 0))]


def _rwkv_pre_fwd(ua, pre_params):
    s = ua.shape[0]
    tile = HEAD_TILE

    def body(ua_ref, prev_ref, *refs):
        p_refs, o_refs = refs[:len(pre_params)], refs[len(pre_params):]
        ops = _pre_operands(ua_ref, prev_ref, pl.program_id(0) == 0)
        outs = _rwkv_pre(*ops, *[p[...] for p in p_refs])
        for o_ref, val in zip(o_refs, outs):
            o_ref[...] = val

    tm = pl.BlockSpec((tile, DA), lambda i: (i, 0))
    return pl.pallas_call(
        body, name="rwkv_pre_fwd", grid=(s // tile,),
        in_specs=_ua_specs(tile, lambda i: i) + [pl.BlockSpec(p.shape, lambda i, nd=p.ndim: (0,) * nd) for p in pre_params],
        out_specs=[tm] * 8, out_shape=[jax.ShapeDtypeStruct((s, DA), F32)] * 8,
        compiler_params=_params("arbitrary"))(ua, ua, *pre_params)


def _rwkv_pre_bwd(ua, pre_params, cots):
    s = ua.shape[0]
    tile = HEAD_TILE
    nt = s // tile
    n_p = len(pre_params)

    def body(ua_ref, prev_ref, *refs):
        p_refs, c_refs = refs[:n_p], refs[n_p:n_p + 11]
        dua_ref = refs[n_p + 11]
        dp_refs = refs[n_p + 12:n_p + 12 + n_p]
        carry_ref = refs[-1]
        i = pl.program_id(0)

        @pl.when(i == 0)
        def _():
            carry_ref[...] = jnp.zeros_like(carry_ref)
            for r in dp_refs:
                r[...] = jnp.zeros_like(r)

        ops = _pre_operands(ua_ref, prev_ref, i == nt - 1)
        _, vjp = jax.vjp(_rwkv_pre, *ops, *[p[...] for p in p_refs])
        c = [r[...] for r in c_refs]
        grads = vjp((c[0] + c[1], c[2], c[3], c[4] + c[5], c[6] + c[7], c[8], c[9], c[10]))
        d_ops, d_par = grads[:12], grads[12:]
        for r, val in zip(dp_refs, d_par):
            r[...] += val
        d_cur = jnp.concatenate([d_ops[0], d_ops[2], d_ops[4], d_ops[8], d_ops[10], d_ops[6]], axis=-1)
        d_sh = jnp.concatenate([d_ops[1], d_ops[3], d_ops[5], d_ops[9], d_ops[11], d_ops[7]], axis=-1)
        rows = lax.broadcasted_iota(jnp.int32, d_sh.shape, 0)
        dua = d_cur + jnp.where(rows == tile - 1, carry_ref[...], pltpu.roll(d_sh, tile - 1, axis=0))
        dua_ref[...] = dua.astype(BF16)
        carry_ref[...] = d_sh[0:1, :]

    rev = lambda i: nt - 1 - i
    tm = pl.BlockSpec((tile, DA), lambda i: (rev(i), 0))
    fixed = [pl.BlockSpec(p.shape, lambda i, nd=p.ndim: (0,) * nd) for p in pre_params]
    return pl.pallas_call(
        body, name="rwkv_pre_bwd", grid=(nt,),
        in_specs=_ua_specs(tile, rev) + fixed + [tm] * 11,
        out_specs=[pl.BlockSpec((tile, NA), lambda i: (rev(i), 0))] + fixed,
        out_shape=[jax.ShapeDtypeStruct((s, NA), BF16)] + [jax.ShapeDtypeStruct(p.shape, F32) for p in pre_params],
        scratch_shapes=[pltpu.VMEM((1, NA), F32)],
        compiler_params=_params("arbitrary"))(ua, ua, *pre_params, *cots)


def _wkv_fwd(seq):
    s = seq[0].shape[0]
    nc = s // WKV_CHUNK

    def body(r_ref, lw_ref, cl_ref, k_ref, v_ref, a_ref, b_ref, y_ref, ck_ref, p_ref, state):
        @pl.when(pl.program_id(0) == 0)
        def _():
            state[...] = jnp.zeros_like(state)

        s0 = state[...]
        for i in range(WKV_STEP_CHUNKS):
            rows = slice(i * WKV_CHUNK, (i + 1) * WKV_CHUNK)
            r, lw, cl, k, v, a, b = (jnp.stack(_to_heads(ref[rows, :])) for ref in (r_ref, lw_ref, cl_ref, k_ref, v_ref,
                                                                                     a_ref, b_ref))
            ck_ref[i] = s0
            p = _tri_inverse(_wkv_aab(lw, cl, a, b))
            p_ref[i] = p
            y, s0 = _wkv_apply(s0, r, lw, cl, k, v, a, b, p)
            y_ref[rows, :] = jnp.concatenate([y[h] for h in range(H)], axis=-1)
        state[...] = s0

    tm = pl.BlockSpec((WKV_STEP_CHUNKS * WKV_CHUNK, DA), lambda c: (c, 0))
    per_chunk = lambda m: pl.BlockSpec((WKV_STEP_CHUNKS, H, m, m), lambda c: (c, 0, 0, 0))
    return pl.pallas_call(
        body, name="wkv_fwd", grid=(nc // WKV_STEP_CHUNKS,), in_specs=[tm] * 7,
        out_specs=[tm, per_chunk(N), per_chunk(WKV_CHUNK)],
        out_shape=[jax.ShapeDtypeStruct((s, DA), F32), jax.ShapeDtypeStruct((nc, H, N, N), F32),
                   jax.ShapeDtypeStruct((nc, H, WKV_CHUNK, WKV_CHUNK), F32)],
        scratch_shapes=[pltpu.VMEM((H, N, N), F32)], compiler_params=_params("arbitrary"))(*seq)


def _wkv_bwd(seq, ckpt, pinv, dy, slabs, owners):
    s = seq[0].shape[0]
    nc = s // WKV_CHUNK
    nx = len(slabs)

    def body(r_ref, lw_ref, cl_ref, k_ref, v_ref, a_ref, b_ref, ck_ref, p_ref, dy_ref, *refs):
        src_refs, d_refs, dst_refs = refs[:nx], refs[nx:nx + 7], refs[nx + 7:2 * nx + 7]
        dstate = refs[2 * nx + 7]
        start, wait = _exchange_ops(src_refs, dst_refs, owners, refs[2 * nx + 8:])

        @pl.when(pl.program_id(0) == 0)
        def _():
            dstate[...] = jnp.zeros_like(dstate)
            start()

        p = p_ref[0]
        r, lw, cl, k, v, a, b, dy = (jnp.stack(_to_heads(ref[...])) for ref in (r_ref, lw_ref, cl_ref, k_ref, v_ref,
                                                                                 a_ref, b_ref, dy_ref))
        _, vjp = jax.vjp(_wkv_apply, ck_ref[0], r, lw, cl, k, v, a, b, p)
        ds0, dr, dlw, dcl, dk, dv, da, db, dp = vjp((dy, dstate[...]))
        dstate[...] = ds0
        _, vjp_x = jax.vjp(_wkv_aab, lw, cl, a, b)
        dlw2, dcl2, da2, db2 = vjp_x(_dot1(_dot1(p, dp, "tn"), p, "nt"))
        for d_ref, val in zip(d_refs, (dr, dlw + dlw2, dcl + dcl2, dk, dv, da + da2, db + db2)):
            d_ref[...] = jnp.concatenate([val[h] for h in range(H)], axis=-1)

        @pl.when(pl.program_id(0) == nc - 1)
        def _():
            wait()

    tm = pl.BlockSpec((WKV_CHUNK, DA), lambda c: (nc - 1 - c, 0))
    per_chunk = lambda m: pl.BlockSpec((1, H, m, m), lambda c: (nc - 1 - c, 0, 0, 0))
    out = pl.pallas_call(
        body, name="wkv_bwd", grid=(nc,),
        in_specs=[tm] * 7 + [per_chunk(N), per_chunk(WKV_CHUNK), tm] + _hbm_specs(nx),
        out_specs=[tm] * 7 + _hbm_specs(nx),
        out_shape=[jax.ShapeDtypeStruct((s, DA), F32)] * 7 + _received_shapes(slabs, owners),
        scratch_shapes=[pltpu.VMEM((H, N, N), F32)] + _exchange_scratch(nx),
        compiler_params=_params("arbitrary"))(*seq, ckpt, pinv, dy, *slabs)
    return out[:7], out[7:]


def _rwkv_post_fwd(y, r, k2, v, g, post_params):
    s = y.shape[0]
    tile = TOK_TILE

    def body(*refs):
        refs[-1][...] = _rwkv_post(*[ref[...] for ref in refs[:-1]])

    tm = pl.BlockSpec((tile, DA), lambda i: (i, 0))
    par = pl.BlockSpec((1, DA), lambda i: (0, 0))
    return pl.pallas_call(
        body, name="rwkv_post_fwd", grid=(s // tile,), in_specs=[tm] * 5 + [par] * 3,
        out_specs=tm, out_shape=jax.ShapeDtypeStruct((s, DA), F32),
        compiler_params=_params("arbitrary"))(y, r, k2, v, g, *post_params)


def _rwkv_post_bwd(y, r, k2, v, g, post_params, dya, slabs, lo):
    s = y.shape[0]
    tile = HEAD_TILE

    def body(y_ref, r_ref, k_ref, v_ref, g_ref, w_ref, b_ref, rk_ref, dya_ref, s_ref, *refs):
        d_refs, p_ref = refs[:8], refs[8]
        start, wait = _pair_swap_ops(s_ref, p_ref, lo, refs[9:])

        @pl.when(pl.program_id(0) == 0)
        def _():
            for ref in d_refs[5:]:
                ref[...] = jnp.zeros_like(ref)
            start()

        _, vjp = jax.vjp(_rwkv_post, *[ref[...] for ref in (y_ref, r_ref, k_ref, v_ref, g_ref, w_ref, b_ref, rk_ref)])
        grads = vjp(dya_ref[...])
        for ref, val in zip(d_refs[:5], grads[:5]):
            ref[...] = val
        for ref, val in zip(d_refs[5:], grads[5:]):
            ref[...] += val

        @pl.when(pl.program_id(0) == s // tile - 1)
        def _():
            wait()

    tm = pl.BlockSpec((tile, DA), lambda i: (i, 0))
    par = pl.BlockSpec((1, DA), lambda i: (0, 0))
    return pl.pallas_call(
        body, name="rwkv_post_bwd", grid=(s // tile,),
        in_specs=[tm] * 5 + [par] * 3 + [tm] + _hbm_specs(1),
        out_specs=[tm] * 5 + [par] * 3 + _hbm_specs(1),
        out_shape=[jax.ShapeDtypeStruct((s, DA), F32)] * 5 + [jax.ShapeDtypeStruct((1, DA), F32)] * 3
        + [jax.ShapeDtypeStruct(slabs.shape, slabs.dtype)],
        scratch_shapes=_pair_swap_scratch(slabs.shape[0]),
        compiler_params=_params("arbitrary"))(y, r, k2, v, g, *post_params, dya, slabs)


def _tri(t):
    return (lax.broadcasted_iota(jnp.int32, (t, t), 0) >= lax.broadcasted_iota(jnp.int32, (t, t), 1)).astype(F32)


def _fox_pre_fwd(ub, uf, q_g, k_g, f_b):
    s = ub.shape[0]
    tile = HEAD_TILE

    def body(ub_ref, uf_ref, qg_ref, kg_ref, fb_ref, q_ref, k_ref, v_ref, cum_ref, carry):
        @pl.when(pl.program_id(0) == 0)
        def _():
            carry[...] = jnp.zeros_like(carry)

        qn, kn, logf = _fox_pre(ub_ref[:, :DA], ub_ref[:, DA:2 * DA], uf_ref[...], qg_ref[...], kg_ref[...],
                                fb_ref[...])
        for h, (q_col, k_col) in enumerate(zip(_to_heads(qn), _to_heads(kn))):
            q_ref[h] = q_col
            k_ref[h] = k_col
        v_ref[...] = _heads(ub_ref, 2 * DA)
        cum = jnp.dot(_tri(tile), logf, precision=HI, preferred_element_type=F32) + carry[...]
        cum_ref[...] = cum
        carry[...] = cum[tile - 1:tile, :]

    hm = pl.BlockSpec((H, tile, N), lambda i: (0, i, 0))
    fixed = lambda shape: pl.BlockSpec(shape, lambda i: (0,) * len(shape))
    return pl.pallas_call(
        body, name="fox_pre_fwd", grid=(s // tile,),
        in_specs=[pl.BlockSpec((tile, NB), lambda i: (i, 0)), pl.BlockSpec((tile, NF), lambda i: (i, 0)),
                  fixed((1, DA)), fixed((1, DA)), fixed((1, NF))],
        out_specs=[hm] * 3 + [pl.BlockSpec((tile, NF), lambda i: (i, 0))],
        out_shape=[jax.ShapeDtypeStruct((H, s, N), F32)] * 3 + [jax.ShapeDtypeStruct((s, NF), F32)],
        scratch_shapes=[pltpu.VMEM((1, NF), F32)], compiler_params=_params("arbitrary"))(ub, uf, q_g, k_g, f_b)


def _fox_pre_bwd(ub, uf, q_g, k_g, f_b, dqn, dkn, dvf, dgate, dcum_q, dcum_k):
    s = ub.shape[0]
    tile = HEAD_TILE
    nt = s // tile

    def body(ub_ref, uf_ref, qg_ref, kg_ref, fb_ref, dq_ref, dk_ref, dv_ref, dgate_ref, dcq_ref, dck_ref,
             dub_ref, duf_ref, dqg_ref, dkg_ref, dfb_ref, carry):
        @pl.when(pl.program_id(0) == 0)
        def _():
            carry[...] = jnp.zeros_like(carry)
            for ref in (dqg_ref, dkg_ref, dfb_ref):
                ref[...] = jnp.zeros_like(ref)

        dcum = dcq_ref[...] + dck_ref[...]
        dlogf = lax.dot_general(_tri(tile), dcum, (((0,), (0,)), ((), ())), precision=HI,
                                preferred_element_type=F32) + carry[...]
        carry[...] = dlogf[0:1, :]
        _, vjp = jax.vjp(_fox_pre, ub_ref[:, :DA], ub_ref[:, DA:2 * DA], uf_ref[...], qg_ref[...], kg_ref[...],
                         fb_ref[...])
        d_q, d_k, d_f, d_qg, d_kg, d_fb = vjp((_from_heads(dq_ref), _from_heads(dk_ref), dlogf))
        dub_ref[...] = jnp.concatenate([d_q, d_k, _from_heads(dv_ref), dgate_ref[...]], axis=-1).astype(BF16)
        duf_ref[...] = d_f.astype(BF16)
        dqg_ref[...] += functools.reduce(jnp.add, _to_heads(d_qg))
        dkg_ref[...] += functools.reduce(jnp.add, _to_heads(d_kg))
        dfb_ref[...] += d_fb

    rev = lambda i: nt - 1 - i
    hm = pl.BlockSpec((H, tile, N), lambda i: (0, rev(i), 0))
    tok = lambda n: pl.BlockSpec((tile, n), lambda i: (rev(i), 0))
    fixed = lambda shape: pl.BlockSpec(shape, lambda i: (0,) * len(shape))
    return pl.pallas_call(
        body, name="fox_pre_bwd", grid=(nt,),
        in_specs=[tok(NB), tok(NF), fixed((1, DA)), fixed((1, DA)), fixed((1, NF)), hm, hm, hm, tok(DA), tok(NF),
                  tok(NF)],
        out_specs=[tok(NB), tok(NF), fixed((1, N)), fixed((1, N)), fixed((1, NF))],
        out_shape=[jax.ShapeDtypeStruct((s, NB), BF16), jax.ShapeDtypeStruct((s, NF), BF16),
                   jax.ShapeDtypeStruct((1, N), F32), jax.ShapeDtypeStruct((1, N), F32),
                   jax.ShapeDtypeStruct((1, NF), F32)],
        scratch_shapes=[pltpu.VMEM((1, NF), F32)],
        compiler_params=_params("arbitrary"))(ub, uf, q_g, k_g, f_b, dqn, dkn, dvf, dgate, dcum_q, dcum_k)


def _att_groups(s):
    blocks = s // ATT_TILE
    per = max(1, blocks // ATT_GROUPS)
    return per, blocks // per


def _att_parts(n, width):
    return ([(0, n - width, False)] if n > width else []) + [(n - width, n, True)]


def _att_scores(q_bf, k_ref, ck_ref, lo, hi, masked, row_offset):
    scores = _bdot_nt(q_bf, k_ref[0, lo:hi, :]) - ck_ref[0, :, lo:hi]
    if masked:
        rows = row_offset + lax.broadcasted_iota(jnp.int32, scores.shape, 0)
        scores = jnp.where(rows >= lax.broadcasted_iota(jnp.int32, scores.shape, 1), scores, -1e30)
    return scores


def _fox_attn_fwd(q, k, v, cum_q, cum_k):
    s = q.shape[1]
    t = ATT_TILE
    per, groups = _att_groups(s)

    def body(q_ref, k_ref, v_ref, cq_ref, ck_ref, o_ref, lse_ref):
        qi = pl.program_id(1)
        for g in range(groups):
            @pl.when(qi // per == g)
            def _(g=g):
                q_bf = (q_ref[0] * ATT_SCALE).astype(BF16)
                parts = _att_parts((g + 1) * per * t, per * t)
                scores = [_att_scores(q_bf, k_ref, ck_ref, lo, hi, masked, (qi - g * per) * t)
                          for lo, hi, masked in parts]
                m = functools.reduce(jnp.maximum, [jnp.max(sc, axis=-1, keepdims=True) for sc in scores])
                l, acc = 0.0, 0.0
                for sc, (lo, hi, _) in zip(scores, parts):
                    p = jnp.exp(sc - m)
                    l += jnp.sum(p, axis=-1, keepdims=True)
                    acc += _bdot(p, v_ref[0, lo:hi, :])
                o_ref[0] = acc / l
                lse_ref[0] = m + jnp.log(l) + cq_ref[0]

    qb = pl.BlockSpec((1, t, N), lambda h, i: (h, i, 0))
    kb = pl.BlockSpec((1, s, N), lambda h, i: (h, 0, 0))
    return pl.pallas_call(
        body, name="fox_attn_fwd", grid=(H, s // t),
        in_specs=[qb, kb, kb, pl.BlockSpec((1, t, 1), lambda h, i: (h, i, 0)),
                  pl.BlockSpec((1, 1, s), lambda h, i: (h, 0, 0))],
        out_specs=[qb, pl.BlockSpec((1, t, 1), lambda h, i: (h, i, 0))],
        out_shape=[jax.ShapeDtypeStruct((H, s, N), F32), jax.ShapeDtypeStruct((H, s, 1), F32)],
        compiler_params=_params("arbitrary", "arbitrary"))(q, k, v, cum_q, cum_k)


def _fox_attn_bwd(q, k, v, cum_q, cum_k, o, lse, do, slabs, owners):
    s = q.shape[1]
    t = ATT_TILE
    per, groups = _att_groups(s)
    nx = len(slabs)

    def body(q_ref, k_ref, v_ref, cq_ref, ck_ref, o_ref, lse_ref, do_ref, *refs):
        src_refs, (dq_ref, dk_ref, dv_ref, dcq_ref, dck_ref) = refs[:nx], refs[nx:nx + 5]
        start, wait = _exchange_ops(src_refs, refs[nx + 5:2 * nx + 5], owners, refs[2 * nx + 5:])
        qi = pl.program_id(1)

        @pl.when((pl.program_id(0) == 0) & (qi == 0))
        def _():
            start()

        @pl.when(qi == 0)
        def _():
            for ref in (dk_ref, dv_ref, dck_ref):
                ref[...] = jnp.zeros_like(ref)

        for g in range(groups):
            @pl.when(qi // per == g)
            def _(g=g):
                q_bf, do_bf = (q_ref[0] * ATT_SCALE).astype(BF16), do_ref[0].astype(BF16)
                row_term = cq_ref[0] - lse_ref[0]
                delta = jnp.sum(do_ref[0] * o_ref[0], axis=-1, keepdims=True)
                dq, dcq = 0.0, 0.0
                for lo, hi, masked in _att_parts((g + 1) * per * t, per * t):
                    p = jnp.exp(_att_scores(q_bf, k_ref, ck_ref, lo, hi, masked, (qi - g * per) * t) + row_term)
                    ds = p * (_bdot_nt(do_bf, v_ref[0, lo:hi, :]) - delta)
                    dq += _bdot(ds, k_ref[0, lo:hi, :])
                    dcq += jnp.sum(ds, axis=-1, keepdims=True)
                    dk_ref[0, lo:hi, :] += _bdot_tn(ds, q_bf)
                    dv_ref[0, lo:hi, :] += _bdot_tn(p, do_bf)
                    dck_ref[0, :, lo:hi] -= jnp.sum(ds, axis=0, keepdims=True)
                dq_ref[0] = dq * ATT_SCALE
                dcq_ref[0] = dcq

        @pl.when((pl.program_id(0) == H - 1) & (qi == s // t - 1))
        def _():
            wait()

    qb = pl.BlockSpec((1, t, N), lambda h, i: (h, i, 0))
    kb = pl.BlockSpec((1, s, N), lambda h, i: (h, 0, 0))
    cqb = pl.BlockSpec((1, t, 1), lambda h, i: (h, i, 0))
    ckb = pl.BlockSpec((1, 1, s), lambda h, i: (h, 0, 0))
    f32 = lambda *shape: jax.ShapeDtypeStruct(shape, F32)
    out = pl.pallas_call(
        body, name="fox_attn_bwd", grid=(H, s // t),
        in_specs=[qb, kb, kb, cqb, ckb, qb, cqb, qb] + _hbm_specs(nx), out_specs=[qb, kb, kb, cqb, ckb] + _hbm_specs(nx),
        out_shape=[f32(H, s, N), f32(H, s, N), f32(H, s, N), f32(H, s, 1), f32(H, 1, s)]
        + _received_shapes(slabs, owners),
        scratch_shapes=_exchange_scratch(nx),
        compiler_params=_params("arbitrary", "arbitrary"))(q, k, v, cum_q, cum_k, o, lse, do, *slabs)
    return out[:5], out[5:]


def _local_step(x, target, w, p):
    mu = p["shift_mu"]
    lora_matrix = lambda a: jnp.moveaxis(a, 0, 1).reshape(RANK, DA).astype(F32)
    pre_params = (mu[:, 0:DA], mu[:, DA:2 * DA], mu[:, 2 * DA:3 * DA], mu[:, 3 * DA + 2 * RANK:],
                  mu[:, 3 * DA:3 * DA + RANK], mu[:, 3 * DA + RANK:3 * DA + 2 * RANK],
                  lora_matrix(w["w_lora_up"]), p["w0"], lora_matrix(w["a_lora_up"]), p["a0"], p["k_k"], p["k_a"])
    post_params = (p["lnx_w"], p["lnx_b"], p["r_k"])
    q_g, k_g = jnp.tile(p["q_norm_g"], (1, H)), jnp.tile(p["k_norm_g"], (1, H))
    f_b = jnp.pad(p["f_bias"], ((0, 0), (0, NF - H)))
    fg = p["final_norm_g"].reshape(1, D)

    h, (ua, ub, ug, uf) = _norm_proj(x, p["norm_g"], (w["in_a"], w["in_b"], w["in_g"], w["in_f"]))
    r, lw, cl, k2, v, av, bv, gg = _rwkv_pre_fwd(ua, pre_params)
    y, ckpt, pinv = _wkv_fwd((r, lw, cl, k2, v, av, bv))
    ya = _rwkv_post_fwd(y, r, k2, v, gg, post_params)
    qn, kn, vf, cum = _fox_pre_fwd(ub, uf, q_g, k_g, f_b)
    cum_t = cum[:, :H].T
    cum_q, cum_k = cum_t[:, :, None], cum_t[:, None, :]
    o, lse = _fox_attn_fwd(qn, kn, vf, cum_q, cum_k)

    (loss, dfg, dwo, dwoa, dwob, dx2, dya, do, dgate_b, dug) = _tail(
        x, target, ya, o, ub, ug, w["w_out_a"], w["w_out_b"], w["w_out"], fg)
    everyone = (0, N_DEV)
    (dqn, dkn, dvf, dcq, dck), (recv_woa, recv_wob, recv_wo) = _fox_attn_bwd(
        qn, kn, vf, cum_q, cum_k, o, lse, do,
        (_col_slabs(dwoa), _col_slabs(dwob), dwo.astype(BF16).reshape(N_DEV, D // N_DEV, D)), (everyone,) * 3)
    pad_f = lambda a: jnp.pad(a.T, ((0, 0), (0, NF - H)))
    dub, duf, dqg, dkg, dfb = _fox_pre_bwd(ub, uf, q_g, k_g, f_b, dqn, dkn, dvf, dgate_b,
                                           pad_f(dcq[:, :, 0]), pad_f(dck.reshape(H, -1)))
    spill = EARLY_FROM * COLS_PER_DEV - NA
    early, dwt_b_head = _proj_wgrad_early(h, dub, dug, duf, -(-spill // 8) * 8)
    dy, dr_p, dk_p, dv_p, dgg, dlnw, dlnb, drk, handed = _rwkv_post_bwd(y, r, k2, v, gg, post_params, dya, early,
                                                                          EARLY_FROM)
    early = _chip_sums(early, handed, EARLY_FROM, "chip_sums_early")
    (dr_s, dlw, dcl, dk_s, dv_s, dav, dbv), (recv_early,) = _wkv_bwd(
        (r, lw, cl, k2, v, av, bv), ckpt, pinv, dy, (early,), ((EARLY_FROM, N_DEV, "chips"),))
    pre_out = _rwkv_pre_bwd(ua, pre_params, (dr_s, dr_p, dlw, dcl, dk_s, dk_p, dv_s, dv_p, dav, dbv, dgg))
    dua, dpre = pre_out[0], pre_out[1:]
    late = _proj_wgrad_late(h, dua, dwt_b_head)

    flat = lambda a: a.reshape(1, -1)
    small = {
        "final_norm_g": dfg, "w0": dpre[7], "a0": dpre[9], "k_k": dpre[10], "k_a": dpre[11], "r_k": drk, "lnx_w": dlnw,
        "lnx_b": dlnb, "q_norm_g": dqg, "k_norm_g": dkg, "f_bias": dfb[:, :H],
        "shift_mu": jnp.concatenate([flat(dpre[0]), flat(dpre[1]), flat(dpre[2]), dpre[4], dpre[5], flat(dpre[3])], axis=1),
    }
    late = _chip_sums(late, _pair_swap(late, 0, "pair_swap_late"), 0, "chip_sums_late")
    by_head = lambda a: jnp.moveaxis(a.reshape(RANK, H, N), 1, 0)
    loras = jnp.stack([by_head(dpre[6]), by_head(dpre[8])], axis=1).astype(BF16)
    dx, dng, (recv_late, recv_lora, recv_small) = _proj_xgrad(
        x, p["norm_g"], dx2, (dua, dub, dug, duf), (w["in_a"], w["in_b"], w["in_g"], w["in_f"]),
        (late, loras, _pack_small(small, loss)), ((0, EARLY_FROM, "chips"), everyone, everyone))
    return dx, dng, (recv_early, recv_late), (recv_woa, recv_wob, recv_wo, recv_lora), recv_small


def _position():
    return lax.axis_index("x"), lax.axis_index("y"), lax.axis_index("c")


def _hbm_specs(n):
    return [pl.BlockSpec(memory_space=pl.ANY)] * n


BIG_GATHER_COPIES = 13
GATHER_ROW_CUT = 400


def _all_gather(big, blocks, name):
    n = len(blocks)

    def body(*refs):
        big_ref, x_refs = refs[0], refs[1:1 + n]
        big_out, out_refs = refs[1 + n], refs[2 + n:2 + 2 * n]
        send_sems, recv_sems, local_sems = refs[2 + 2 * n:]
        x, y, c = _position()
        me, sibling = (x, y, c), (x, y, 1 - c)
        chips = [(1 - x, y), (x, 1 - y), (1 - x, 1 - y)]
        x_nbr, y_nbr, diag = chips
        rows = big_ref.shape[0]
        cut = GATHER_ROW_CUT

        def part(ref, h):
            return ref if h is None else ref.at[pl.ds(0, cut)] if h == 0 else ref.at[pl.ds(cut, rows - cut)]

        def landed(chip, core, h):
            return part(big_out.at[4 * chip[0] + 2 * chip[1] + core], h)

        def big_copy(k, src, dst, to):
            return pltpu.make_async_remote_copy(src_ref=src, dst_ref=dst, send_sem=send_sems.at[7 * n + k],
                                                recv_sem=recv_sems.at[7 * n + k], device_id=to, device_id_type=MESH)

        def arrival(k, chip, core, h):
            dst = landed(chip, core, h)
            return big_copy(k, dst, dst, me)

        def pass_on(k, chip, h, to):
            src = landed(chip, c, h)
            return big_copy(k, src, src, to)

        big_mine = pltpu.make_async_copy(big_ref, landed((x, y), c, None), local_sems.at[n])
        big_mine.start()
        here = (x, y)
        big_sent = [big_copy(0, big_ref, landed(here, c, None), sibling),
                    big_copy(1, part(big_ref, 0), landed(here, c, 0), (*x_nbr, c)),
                    big_copy(2, part(big_ref, 1), landed(here, c, 1), (*y_nbr, c)),
                    big_copy(3, part(big_ref, 1), landed(here, c, 1), (*x_nbr, c)),
                    big_copy(4, part(big_ref, 0), landed(here, c, 0), (*y_nbr, c))]
        for cp in big_sent:
            cp.start()

        def copy(a, k, blk, to, own=False):
            dst = out_refs[a].at[4 * blk[0] + 2 * blk[1] + blk[2]]
            return pltpu.make_async_remote_copy(
                src_ref=x_refs[a] if own else dst, dst_ref=dst, send_sem=send_sems.at[7 * a + k],
                recv_sem=recv_sems.at[7 * a + k], device_id=to, device_id_type=MESH)

        mine = [pltpu.make_async_copy(x_refs[a], out_refs[a].at[4 * x + 2 * y + c], local_sems.at[a]) for a in range(n)]
        for cp in mine:
            cp.start()
        first = []
        for a in range(n):
            first.append(copy(a, 0, me, sibling, own=True))
            first += [copy(a, 1 + j, me, (*chip, c), own=True) for j, chip in enumerate(chips)]
        for cp in first:
            cp.start()

        big_steps = [(1, x_nbr, 0, (*y_nbr, c), 5, 7), (2, y_nbr, 1, (*x_nbr, c), 6, 8), (3, x_nbr, 1, None, None, 9),
                     (4, y_nbr, 0, None, None, 10), (5, diag, 0, None, None, 11), (6, diag, 1, None, None, 12)]
        for k, chip, h, onward, k_onward, k_sibling in big_steps:
            arrival(k, chip, c, h).wait_recv()
            if onward is not None:
                big_sent.append(pass_on(k_onward, chip, h, onward))
                big_sent[-1].start()
            big_sent.append(pass_on(k_sibling, chip, h, sibling))
            big_sent[-1].start()

        passed = []
        for j, chip in enumerate(chips):
            for a in range(n):
                copy(a, 1 + j, (*chip, c), me).wait_recv()
                passed.append(copy(a, 4 + j, (*chip, c), sibling))
                passed[-1].start()
        for a in range(n):
            copy(a, 0, sibling, me).wait_recv()
        for j, chip in enumerate(chips):
            for a in range(n):
                copy(a, 4 + j, (*chip, 1 - c), me).wait_recv()
        arrival(0, here, 1 - c, None).wait_recv()
        for k, chip, h, _, _, k_sibling in big_steps:
            arrival(k_sibling, chip, 1 - c, h).wait_recv()
        for cp in first + passed + big_sent:
            cp.wait_send()
        for cp in mine + [big_mine]:
            cp.wait()

    everything = [big] + list(blocks)
    return pl.pallas_call(
        body, name=name, out_shape=[jax.ShapeDtypeStruct((N_DEV,) + b.shape, b.dtype) for b in everything],
        in_specs=_hbm_specs(n + 1), out_specs=_hbm_specs(n + 1),
        scratch_shapes=[pltpu.SemaphoreType.DMA((7 * n + BIG_GATHER_COPIES,)),
                        pltpu.SemaphoreType.DMA((7 * n + BIG_GATHER_COPIES,)), pltpu.SemaphoreType.DMA((n + 1,))],
    )(*everything)


def _received_shapes(slabs, owners):
    return [jax.ShapeDtypeStruct((N_DEV // 2 if len(o) == 3 else N_DEV,) + s.shape[1:], s.dtype)
            for s, o in zip(slabs, owners)]


def _pair_swap_scratch(n):
    return [pltpu.SemaphoreType.DMA((n,)), pltpu.SemaphoreType.DMA((n,))]


def _pair_swap_ops(s_ref, p_ref, lo, sems):
    send_sems, recv_sems = sems
    n = s_ref.shape[0]

    def run(sending):
        x, y, c = _position()
        for side in (0, 1):
            mine = [pltpu.make_async_remote_copy(src_ref=s_ref.at[i], dst_ref=p_ref.at[i], send_sem=send_sems.at[i],
                                                 recv_sem=recv_sems.at[i], device_id=(x, y, 1 - c), device_id_type=MESH)
                    for i in range(n) if (lo + i) % 2 == side]

            @pl.when(c != side)
            def _():
                for cp in mine:
                    cp.start() if sending else cp.wait_send()

            if not sending:
                @pl.when(c == side)
                def _():
                    for cp in mine:
                        cp.wait_recv()

    return functools.partial(run, True), functools.partial(run, False)


def _pair_swap(slabs, lo, name):
    n = slabs.shape[0]

    def body(s_ref, p_ref, *sems):
        start, wait = _pair_swap_ops(s_ref, p_ref, lo, sems)
        start()
        wait()

    return pl.pallas_call(
        body, name=name, out_shape=jax.ShapeDtypeStruct(slabs.shape, slabs.dtype),
        in_specs=_hbm_specs(1), out_specs=_hbm_specs(1)[0], scratch_shapes=_pair_swap_scratch(n))(slabs)


def _chip_sums(slabs, swapped, lo, name):
    n, rows, cols = slabs.shape
    tile = W_IN_COL_TILE

    def body(s_ref, p_ref, o_ref):
        c = lax.axis_index("c")
        for i in range(n):
            @pl.when(c == (lo + i) % 2)
            def _(i=i):
                o_ref[i] = (s_ref[i].astype(F32) + p_ref[i].astype(F32)).astype(BF16)

    blk = pl.BlockSpec((n, rows, tile), lambda j: (0, 0, j))
    return pl.pallas_call(
        body, name=name, grid=(cols // tile,), in_specs=[blk, blk], out_specs=blk,
        out_shape=jax.ShapeDtypeStruct(slabs.shape, BF16), compiler_params=_params("arbitrary"))(slabs, swapped)


def _exchange_scratch(n):
    return [pltpu.SemaphoreType.DMA((7 * n,)), pltpu.SemaphoreType.DMA((7 * n,)), pltpu.SemaphoreType.DMA((n,))]


def _exchange_ops(src_refs, dst_refs, owners, sems):
    send_sems, recv_sems, local_sems = sems
    n = len(src_refs)

    def guarded(a, dev, fn):
        lo, hi = owners[a][:2]
        if (lo, hi) == (0, N_DEV):
            fn()
        else:
            pl.when((dev >= lo) & (dev < hi))(fn)

    def src(a, dev):
        ref = src_refs[a]
        return ref.at[0] if ref.shape[0] == 1 else ref.at[dev - owners[a][0]]

    def run(sending, waiting):
        x, y, c = _position()
        me = 4 * x + 2 * y + c
        for a in range(n):
            by_chip = len(owners[a]) == 3
            slot = (lambda qx, qy, qc: 2 * qx + qy) if by_chip else (lambda qx, qy, qc: 4 * qx + 2 * qy + qc)
            mine = slot(x, y, c)
            local = lambda a=a, mine=mine: pltpu.make_async_copy(src(a, me), dst_refs[a].at[mine], local_sems.at[a])
            if sending:
                guarded(a, me, lambda local=local: local().start())
            for m in range(2, N_DEV, 2) if by_chip else range(1, N_DEV):
                px, py, pc = x ^ (m >> 2), y ^ ((m >> 1) & 1), c ^ (m & 1)
                peer = 4 * px + 2 * py + pc
                theirs = slot(px, py, pc)
                sem = dict(send_sem=send_sems.at[7 * a + m - 1], recv_sem=recv_sems.at[7 * a + m - 1],
                           device_id=(px, py, pc), device_id_type=MESH)
                send = lambda a=a, peer=peer, sem=sem, mine=mine: pltpu.make_async_remote_copy(
                    src_ref=src(a, peer), dst_ref=dst_refs[a].at[mine], **sem)
                recv = lambda a=a, sem=sem, theirs=theirs: pltpu.make_async_remote_copy(
                    src_ref=src(a, me), dst_ref=dst_refs[a].at[theirs], **sem)
                if sending:
                    guarded(a, peer, lambda send=send: send().start())
                if waiting:
                    guarded(a, me, lambda recv=recv: recv().wait_recv())
                    guarded(a, peer, lambda send=send: send().wait_send())
            if waiting:
                guarded(a, me, lambda local=local: local().wait())

    return functools.partial(run, True, False), functools.partial(run, False, True)


def _sum_slabs(r_ref):
    g = r_ref[0].astype(F32)
    for k in range(1, r_ref.shape[0]):
        g = g + r_ref[k].astype(F32)
    return g


def _adamw(g, w, m, v):
    m_new = ADAM_B1 * m + (1.0 - ADAM_B1) * g
    v_new = ADAM_B2 * v + (1.0 - ADAM_B2) * (g * g)
    m_hat = m_new / (1.0 - ADAM_B1 ** ADAM_STEP)
    v_hat = v_new / (1.0 - ADAM_B2 ** ADAM_STEP)
    return g, -ADAM_LR * (m_hat / (jnp.sqrt(v_hat) + ADAM_EPS) + ADAM_WD * w), m_new, v_new


def _adamw_w_in(recv_early, recv_late, w, m, v, slabs, owners):
    rows, cols = w.shape
    tile = W_IN_COL_TILE
    nx = len(slabs)

    def body(early_ref, late_ref, w_ref, m_ref, v_ref, *refs):
        src_refs, o_refs, dst_refs = refs[:nx], refs[nx:nx + 4], refs[nx + 4:2 * nx + 4]
        start, wait = _exchange_ops(src_refs, dst_refs, owners, refs[2 * nx + 4:])
        x, y, c = _position()
        early_owner = 4 * x + 2 * y + c >= EARLY_FROM

        @pl.when(pl.program_id(0) == 0)
        def _():
            start()

        def update(g):
            for o_ref, val in zip(o_refs, _adamw(g, w_ref[...], m_ref[...], v_ref[...])):
                o_ref[...] = val

        pl.when(early_owner)(lambda: update(_sum_slabs(early_ref)))
        pl.when(jnp.logical_not(early_owner))(lambda: update(_sum_slabs(late_ref)))

        @pl.when(pl.program_id(0) == cols // tile - 1)
        def _():
            wait()

    blk = pl.BlockSpec((rows, tile), lambda i: (0, i))
    slots = lambda r: pl.BlockSpec((r.shape[0], rows, tile), lambda i: (0, 0, i))
    out = pl.pallas_call(
        body, name="adamw_w_in", grid=(cols // tile,),
        in_specs=[slots(recv_early), slots(recv_late), blk, blk, blk] + _hbm_specs(nx),
        out_specs=[blk] * 4 + _hbm_specs(nx),
        out_shape=[jax.ShapeDtypeStruct((rows, cols), F32)] * 4 + _received_shapes(slabs, owners),
        scratch_shapes=_exchange_scratch(nx),
        compiler_params=_params("arbitrary"))(recv_early, recv_late, w, m, v, *slabs)
    return out[:4], out[4:]


def _adamw_misc(recvs, recv_small, recv_norm, params):
    names = list(params)
    flat = [a for n in names for a in params[n]]

    def body(woa_ref, wob_ref, wo_ref, lora_ref, small_ref, norm_ref, *refs):
        p_refs, o_refs = refs[:len(flat)], refs[len(flat):]
        g_small = _sum_slabs(small_ref)
        g_lora = _sum_slabs(lora_ref)
        grads = {"w_out_a": _sum_slabs(woa_ref), "w_out_b": _sum_slabs(wob_ref), "w_out": _sum_slabs(wo_ref),
                 "w_lora_up": g_lora[0], "a_lora_up": g_lora[1], "norm_g": _sum_slabs(norm_ref)}
        for n, (off, size) in SMALL_SLOTS.items():
            grads[n] = g_small[:, off:off + size]
        for i, n in enumerate(names):
            w_ref, m_ref, v_ref = p_refs[3 * i:3 * i + 3]
            for o_ref, val in zip(o_refs[4 * i:4 * i + 4], _adamw(grads[n], w_ref[...], m_ref[...], v_ref[...])):
                o_ref[...] = val
        o_refs[-1][...] = g_small[:, LOSS_SLOT:LOSS_SLOT + 1]

    out = pl.pallas_call(
        body, name="adamw_misc",
        out_shape=[jax.ShapeDtypeStruct(params[n][0].shape, F32) for n in names for _ in range(4)]
        + [jax.ShapeDtypeStruct((1, 1), F32)],
        compiler_params=_params())(*recvs, recv_small, recv_norm, *flat)
    return {n: out[4 * i:4 * i + 4] for i, n in enumerate(names)}, out[-1]


_WT_SEGMENTS = ((0, NA), (NA, NB), (NA + NB + H, NG), (NA + NB, H))


def _split_wt(gathered):
    tile = W_IN_COL_TILE

    def body(g_ref, *o_refs):
        full = jnp.concatenate([g_ref[j] for j in range(N_DEV)], axis=0)
        for o_ref, (row, n) in zip(o_refs, _WT_SEGMENTS):
            seg = full[row:row + n]
            if n < o_ref.shape[0]:
                seg = jnp.concatenate([seg, jnp.zeros((o_ref.shape[0] - n, tile), BF16)], axis=0)
            o_ref[...] = seg

    sizes = (NA, NB, NG, NF)
    return pl.pallas_call(
        body, name="split_wt", grid=(D // tile,),
        in_specs=[pl.BlockSpec((N_DEV, COLS_PER_DEV, tile), lambda i: (0, 0, i))],
        out_specs=[pl.BlockSpec((n, tile), lambda i: (0, i)) for n in sizes],
        out_shape=[jax.ShapeDtypeStruct((n, D), BF16) for n in sizes],
        compiler_params=_params("arbitrary"))(gathered)


def _by_cols(a):
    return jnp.moveaxis(a, 0, 1).reshape(a.shape[1], -1)


def _col_slabs(a):
    return jnp.moveaxis(a.reshape(a.shape[0], N_DEV, -1), 1, 0).astype(BF16)


def _pack_small(grads, loss):
    pieces, at = [], 0
    for n, (off, size) in list(SMALL_SLOTS.items()) + [("loss", (LOSS_SLOT, 1))]:
        pieces += [jnp.zeros((off - at,), F32), (loss if n == "loss" else grads[n]).reshape(-1)]
        at = off + size
    return jnp.concatenate(pieces + [jnp.zeros((SMALL_LEN - at,), F32)]).reshape(1, 1, SMALL_LEN)


def _gather_weights(t):
    cast = lambda a: a.astype(BF16)
    loras = jnp.stack([t["w_lora_up"][0], t["a_lora_up"][0]])
    wt, woa, wob, wo, lora = _all_gather(
        cast(t["w_in"][0].T), [cast(t["w_out_a"][0]), cast(t["w_out_b"][0]), cast(t["w_out"][0]), cast(loras)],
        "weight_gather")
    in_a, in_b, in_g, in_f = _split_wt(wt)
    return {"in_a": in_a, "in_b": in_b, "in_g": in_g, "in_f": in_f, "w_out_a": _by_cols(woa), "w_out_b": _by_cols(wob),
            "w_out": wo.reshape(D, D), "w_lora_up": lora[:, 0], "a_lora_up": lora[:, 1]}


def kernel(x, norm_g, w_in, shift_mu, w_lora_up, w0, a_lora_up, a0, k_k, k_a, r_k, lnx_w, lnx_b, f_bias, q_norm_g, k_norm_g, w_out_a, w_out_b, w_out, final_norm_g, loss_target, m_norm_g, m_w_in, m_shift_mu, m_w_lora_up, m_w0, m_a_lora_up, m_a0, m_k_k, m_k_a, m_r_k, m_lnx_w, m_lnx_b, m_f_bias, m_q_norm_g, m_k_norm_g, m_w_out_a, m_w_out_b, m_w_out, m_final_norm_g, v_norm_g, v_w_in, v_shift_mu, v_w_lora_up, v_w0, v_a_lora_up, v_a0, v_k_k, v_k_a, v_r_k, v_lnx_w, v_lnx_b, v_f_bias, v_q_norm_g, v_k_norm_g, v_w_out_a, v_w_out_b, v_w_out, v_final_norm_g):
    names = ("norm_g", "w_in", "shift_mu", "w_lora_up", "w0", "a_lora_up", "a0", "k_k", "k_a", "r_k", "lnx_w", "lnx_b",
             "f_bias", "q_norm_g", "k_norm_g", "w_out_a", "w_out_b", "w_out", "final_norm_g")
    weights = dict(zip(names, (norm_g, w_in, shift_mu, w_lora_up, w0, a_lora_up, a0, k_k, k_a, r_k, lnx_w, lnx_b,
                               f_bias, q_norm_g, k_norm_g, w_out_a, w_out_b, w_out, final_norm_g)))
    m_in = dict(zip(names, (m_norm_g, m_w_in, m_shift_mu, m_w_lora_up, m_w0, m_a_lora_up, m_a0, m_k_k, m_k_a, m_r_k,
                            m_lnx_w, m_lnx_b, m_f_bias, m_q_norm_g, m_k_norm_g, m_w_out_a, m_w_out_b, m_w_out,
                            m_final_norm_g)))
    v_in = dict(zip(names, (v_norm_g, v_w_in, v_shift_mu, v_w_lora_up, v_w0, v_a_lora_up, v_a0, v_k_k, v_k_a, v_r_k,
                            v_lnx_w, v_lnx_b, v_f_bias, v_q_norm_g, v_k_norm_g, v_w_out_a, v_w_out_b, v_w_out,
                            v_final_norm_g)))

    matrices = ("w_out_a", "w_out_b", "w_out", "w_lora_up", "a_lora_up")
    as_2d = lambda n, a: a[0] if n in matrices else a.reshape(1, -1)

    full = _gather_weights(weights)
    dx, dng, recv_wt, recvs, recv_small = _local_step(
        x[0], loss_target[0], full, {n: as_2d(n, weights[n]) for n in ("norm_g",) + tuple(SMALL_SLOTS)})

    res, (recv_norm,) = _adamw_w_in(*recv_wt, w_in[0].T, m_w_in[0].T, v_w_in[0].T, (dng[None],), ((0, N_DEV),))
    outs = {"w_in": [r.T[None] for r in res]}
    misc = [n for n in names if n != "w_in"]
    res, loss_sum = _adamw_misc(recvs, recv_small, recv_norm,
                                {n: tuple(as_2d(n, t[n]) for t in (weights, m_in, v_in)) for n in misc})
    for n in misc:
        outs[n] = [r.reshape(weights[n].shape) for r in res[n]]
    return (loss_sum.reshape(()), dx[None], *[outs[n][i] for i in range(4) for n in names])
```

```python
import functools
import math

import jax
import jax.numpy as jnp
from jax import lax
from jax.experimental import pallas as pl
from jax.experimental.pallas import tpu as pltpu

F32 = jnp.float32
BF16 = jnp.bfloat16
HI = lax.Precision.HIGHEST
MESH = pl.DeviceIdType.MESH

N_DEV = 8
D = 1024
H = 8
N = 64
DA = H * N
RANK = 64
NA = 4 * DA + 2 * RANK
NB = 4 * DA
NG = 2 * D
NF = 128
IN_COLS = NA + NB + H + NG
COLS_PER_DEV = IN_COLS // N_DEV
RMS_EPS = 1e-6
LNX_EPS = 64e-5
ATT_SCALE = N ** -0.5

ADAM_LR = 0.001
ADAM_B1 = 0.9
ADAM_B2 = 0.999
ADAM_EPS = 1e-08
ADAM_WD = 0.01
ADAM_STEP = 10

LANES = 128
WKV_CHUNK = 64
WKV_STEP_CHUNKS = 2
TOK_TILE = 256
HEAD_TILE = 256
XGRAD_TILE = 128
WGRAD_TILE = 512
ATT_TILE = 256
ATT_GROUPS = 8
VMEM_LIMIT = 56 * 1024 * 1024


def _lane_tile_slots(sizes):
    slots, at = {}, 0
    for name, size in sizes:
        slots[name] = (at, size)
        at += -(-size // LANES) * LANES
    return slots, at


SMALL_SLOTS, LOSS_SLOT = _lane_tile_slots((
    ("final_norm_g", D), ("shift_mu", NA), ("w0", DA), ("a0", DA), ("k_k", DA), ("k_a", DA), ("r_k", DA), ("lnx_w", DA),
    ("lnx_b", DA), ("q_norm_g", N), ("k_norm_g", N), ("f_bias", H)))
SMALL_LEN = LOSS_SLOT + LANES
W_IN_COL_TILE = 512
EARLY_FROM = -(-NA // COLS_PER_DEV)


def _params(*sem):
    return pltpu.CompilerParams(dimension_semantics=sem or None, vmem_limit_bytes=VMEM_LIMIT)


def _bdot(a, b):
    return jnp.dot(a.astype(BF16), b.astype(BF16), preferred_element_type=F32)


def _bdot_nt(a, b):
    return lax.dot_general(a.astype(BF16), b.astype(BF16), (((1,), (1,)), ((), ())), preferred_element_type=F32)


def _bdot_tn(a, b):
    return lax.dot_general(a.astype(BF16), b.astype(BF16), (((0,), (0,)), ((), ())), preferred_element_type=F32)


def _sigmoid(x):
    return 1.0 / (1.0 + jnp.exp(-x))


def _softplus(x):
    return jnp.maximum(x, 0.0) + jnp.log(1.0 + jnp.exp(-jnp.abs(x)))


def _heads(ref, col0):
    return jnp.stack([ref[:, col0 + N * h:col0 + N * (h + 1)] for h in range(H)])


def _lerp(c, s, mu):
    return c + (s - c) * mu


def _head_sums(x):
    low = lax.broadcasted_iota(jnp.int32, (x.shape[0], LANES), 1) < N
    out = []
    for p in range(x.shape[1] // LANES):
        pair = x[:, LANES * p:LANES * (p + 1)]
        first = jnp.sum(jnp.where(low, pair, 0.0), axis=-1, keepdims=True)
        second = jnp.sum(jnp.where(low, 0.0, pair), axis=-1, keepdims=True)
        out.append(jnp.where(low, first, second))
    return jnp.concatenate(out, axis=-1)


def _to_heads(x):
    return [x[:, N * h:N * (h + 1)] for h in range(H)]


def _from_heads(ref):
    return jnp.concatenate([ref[h] for h in range(H)], axis=-1)


def _rwkv_pre(rc, rs, kc, ks, vc, vs, gc, gs, wdc, wds, adc, ads,
              mu_r, mu_k, mu_v, mu_g, mu_wd, mu_ad, w_up, w0, a_up, a0, k_k, k_a):
    r = _lerp(rc, rs, mu_r)
    k = _lerp(kc, ks, mu_k)
    v = _lerp(vc, vs, mu_v)
    g = _lerp(gc, gs, mu_g)
    wd = _lerp(wdc, wds, mu_wd)
    ad = _lerp(adc, ads, mu_ad)
    t = wd.shape[0]
    w_raw = -_softplus(-(w0 + _bdot(jnp.tanh(wd), w_up))) - 0.5
    lw = -jnp.exp(w_raw)
    row = lax.broadcasted_iota(jnp.int32, (t, t), 0)
    col = lax.broadcasted_iota(jnp.int32, (t, t), 1)
    same_chunk = ((row >= col) & (row // WKV_CHUNK == col // WKV_CHUNK)).astype(F32)
    cl = jnp.dot(same_chunk, lw, precision=HI, preferred_element_type=F32)
    alr = _sigmoid(a0 + _bdot(ad, a_up))
    kk = k * k_k
    kk = kk / jnp.maximum(jnp.sqrt(_head_sums(kk * kk)), 1e-12)
    k2 = k * (1.0 + (alr - 1.0) * k_a)
    return r, lw, cl, k2, v, -kk, kk * alr, g


_MM_DIMS = {"nn": (((2,), (1,)), ((0,), (0,))), "nt": (((2,), (2,)), ((0,), (0,))), "tn": (((1,), (1,)), ((0,), (0,)))}


def _dot1(a, b, kind):
    return lax.dot_general(a.astype(BF16), b.astype(BF16), dimension_numbers=_MM_DIMS[kind], preferred_element_type=F32)


@functools.partial(jax.custom_vjp, nondiff_argnums=(2,))
def _mm(a, b, kind):
    return _dot1(a, b, kind)


def _mm_fwd(a, b, kind):
    return _dot1(a, b, kind), (a, b)


def _mm_bwd(kind, res, ct):
    a, b = res
    if kind == "nn":
        return _dot1(ct, b, "nt"), _dot1(a, ct, "tn")
    if kind == "nt":
        return _dot1(ct, b, "nn"), _dot1(ct, a, "tn")
    return _dot1(b, ct, "nt"), _dot1(a, ct, "nn")


_mm.defvjp(_mm_fwd, _mm_bwd)


def _chunk_masks(c):
    row = lax.broadcasted_iota(jnp.int32, (c, c), 0)
    col = lax.broadcasted_iota(jnp.int32, (c, c), 1)
    return (row >= col)[None], (row > col)[None], (row == col).astype(F32)[None]


def _wkv_aab(lw, cl, a, b):
    _, strict, _ = _chunk_masks(a.shape[1])
    return jnp.where(strict, _mm(a * jnp.exp(cl - lw), b * jnp.exp(-cl), "nt"), 0.0)


def _tri_inverse(x):
    c = x.shape[1]
    p = _chunk_masks(c)[2] + x
    for _ in range(int(math.log2(c)) - 1):
        x = _dot1(x, x, "nn")
        p = p + _dot1(p, x, "nn")
    return p


def _wkv_apply(s0, r, lw, cl, k, v, a, b, p):
    c = r.shape[1]
    incl, strict, _ = _chunk_masks(c)
    gi = jnp.exp(-cl)
    left = jnp.concatenate([a * jnp.exp(cl - lw), r * jnp.exp(cl)], axis=1)
    right = jnp.concatenate([b * gi, k * gi], axis=1)
    m = _mm(left, right, "nt")
    z0 = _mm(left, s0, "nt")
    a_ak = jnp.where(strict, m[:, :c, c:], 0.0)
    row = lax.broadcasted_iota(jnp.int32, (c, 2 * c), 0)
    col = lax.broadcasted_iota(jnp.int32, (c, 2 * c), 1)
    a_r = jnp.where((row >= col % c)[None], m[:, c:, :], 0.0)
    sa = _mm(p, z0[:, :c] + _mm(a_ak, v, "nn"), "nn")
    sa_v = jnp.concatenate([sa, v], axis=1)
    y = z0[:, c:] + _mm(a_r, sa_v, "nn")
    s1 = (s0 + _mm(sa_v, right, "tn")) * jnp.exp(cl[:, c - 1:c, :])
    return y, s1


def _rwkv_post(y, r, k2, v, g, lnx_w, lnx_b, r_k):
    yc = y - _head_sums(y) * (1.0 / N)
    var = _head_sums(yc * yc) * (1.0 / N)
    yn = yc * lax.rsqrt(var + LNX_EPS) * lnx_w + lnx_b
    bonus = _head_sums(r * k2 * r_k) * v
    return (yn + bonus) * (g * _sigmoid(g))


def _fox_pre(q, k, f, q_g, k_g, f_b):
    qn = q * lax.rsqrt(_head_sums(q * q) * (1.0 / N) + RMS_EPS) * q_g
    kn = k * lax.rsqrt(_head_sums(k * k) * (1.0 / N) + RMS_EPS) * k_g
    x = f + f_b
    return qn, kn, jnp.minimum(x, 0.0) - jnp.log(1.0 + jnp.exp(-jnp.abs(x)))


def _norm_proj(x, g, wts):
    s = x.shape[0]
    k = len(wts)

    def body(x_ref, g_ref, *refs):
        w_refs, h_ref, o_refs = refs[:k], refs[k], refs[k + 1:]
        xv = x_ref[...]
        h = (xv * lax.rsqrt(jnp.mean(xv * xv, axis=-1, keepdims=True) + RMS_EPS) * g_ref[...]).astype(BF16)
        h_ref[...] = h
        for w_ref, o_ref in zip(w_refs, o_refs):
            o_ref[...] = _bdot_nt(h, w_ref[...])

    tok = lambda n: pl.BlockSpec((TOK_TILE, n), lambda i: (i, 0))
    out = pl.pallas_call(
        body, name="norm_proj", grid=(s // TOK_TILE,),
        in_specs=[tok(D), pl.BlockSpec((1, D), lambda i: (0, 0))] + [pl.BlockSpec(w.shape, lambda i: (0, 0)) for w in wts],
        out_specs=[tok(D)] + [tok(w.shape[0]) for w in wts],
        out_shape=[jax.ShapeDtypeStruct((s, D), BF16)] + [jax.ShapeDtypeStruct((s, w.shape[0]), F32) for w in wts],
        compiler_params=_params("arbitrary"))(x, g, *wts)
    return out[0], out[1:]


def _proj_wgrad_early(h, dub, dug, duf, head_rows):
    s = dub.shape[0]
    steps = s // WGRAD_TILE
    seg_rows = (_WT_SEGMENTS[1], _WT_SEGMENTS[2], _WT_SEGMENTS[3])

    def body(h_ref, b_ref, g_ref, f_ref, o_ref, head_ref, *accs):
        @pl.when(pl.program_id(0) == 0)
        def _():
            for acc in accs:
                acc[...] = jnp.zeros_like(acc)

        h = h_ref[...]
        for acc, du_ref in zip(accs, (b_ref, g_ref, f_ref)):
            acc[...] += _bdot_tn(du_ref[...], h)

        @pl.when(pl.program_id(0) == steps - 1)
        def _():
            head_ref[...] = accs[0][:head_rows, :]
            for j in range(EARLY_FROM, N_DEV):
                lo, hi = COLS_PER_DEV * j, COLS_PER_DEV * (j + 1)
                parts = []
                for acc, (row, n) in sorted(zip(accs, seg_rows), key=lambda t: t[1][0]):
                    first, last = max(lo, row), min(hi, row + n)
                    if first < last:
                        parts.append(acc[first - row:last - row, :])
                o_ref[j - EARLY_FROM] = (parts[0] if len(parts) == 1 else jnp.concatenate(parts, axis=0)).astype(BF16)

    tok = lambda n: pl.BlockSpec((WGRAD_TILE, n), lambda i: (i, 0))
    n_early = N_DEV - EARLY_FROM
    return pl.pallas_call(
        body, name="wgrad_bgf", grid=(steps,), in_specs=[tok(D), tok(NB), tok(NG), tok(NF)],
        out_specs=[pl.BlockSpec((n_early, COLS_PER_DEV, D), lambda i: (0, 0, 0)),
                   pl.BlockSpec((head_rows, D), lambda i: (0, 0))],
        out_shape=[jax.ShapeDtypeStruct((n_early, COLS_PER_DEV, D), BF16), jax.ShapeDtypeStruct((head_rows, D), F32)],
        scratch_shapes=[pltpu.VMEM((n, D), F32) for n in (NB, NG, NF)],
        compiler_params=_params("arbitrary"))(h, dub, dug, duf)


def _proj_wgrad_late(h, dua, dwt_b_head):
    s = dua.shape[0]
    steps = s // WGRAD_TILE

    def body(h_ref, du_ref, b_ref, o_ref, acc):
        @pl.when(pl.program_id(0) == 0)
        def _():
            acc[...] = jnp.zeros_like(acc)

        acc[...] += _bdot_tn(du_ref[...], h_ref[...])

        @pl.when(pl.program_id(0) == steps - 1)
        def _():
            for j in range(EARLY_FROM):
                lo, hi = COLS_PER_DEV * j, COLS_PER_DEV * (j + 1)
                parts = [acc[lo:min(hi, NA), :]] + ([b_ref[:hi - NA, :]] if hi > NA else [])
                o_ref[j] = (parts[0] if len(parts) == 1 else jnp.concatenate(parts, axis=0)).astype(BF16)

    return pl.pallas_call(
        body, name="wgrad_a", grid=(steps,),
        in_specs=[pl.BlockSpec((WGRAD_TILE, D), lambda i: (i, 0)), pl.BlockSpec((WGRAD_TILE, NA), lambda i: (i, 0)),
                  pl.BlockSpec(dwt_b_head.shape, lambda i: (0, 0))],
        out_specs=pl.BlockSpec((EARLY_FROM, COLS_PER_DEV, D), lambda i: (0, 0, 0)),
        out_shape=jax.ShapeDtypeStruct((EARLY_FROM, COLS_PER_DEV, D), BF16),
        scratch_shapes=[pltpu.VMEM((NA, D), F32)], compiler_params=_params("arbitrary"))(h, dua, dwt_b_head)


def _proj_xgrad(x, g, dx2, dus, ws, slabs, owners):
    s = x.shape[0]
    tile = XGRAD_TILE
    k = len(dus)
    nx = len(slabs)
    n_in = 3 + 2 * k + nx

    def body(*refs):
        x_ref, g_ref, dx2_ref = refs[:3]
        du_refs, w_refs = refs[3:3 + k], refs[3 + k:3 + 2 * k]
        src_refs = refs[3 + 2 * k:3 + 2 * k + nx]
        dx_ref, dg_ref = refs[n_in:n_in + 2]
        dst_refs = refs[n_in + 2:n_in + 2 + nx]
        start, wait = _exchange_ops(src_refs, dst_refs, owners, refs[n_in + 2 + nx:])

        @pl.when(pl.program_id(0) == 0)
        def _():
            dg_ref[...] = jnp.zeros_like(dg_ref)
            start()

        dh = _bdot(du_refs[0][...], w_refs[0][...])
        for du_ref, w_ref in zip(du_refs[1:], w_refs[1:]):
            dh += _bdot(du_ref[...], w_ref[...])
        xv = x_ref[...]
        rs = lax.rsqrt(jnp.mean(xv * xv, axis=-1, keepdims=True) + RMS_EPS)
        xn = xv * rs
        dg_ref[...] += jnp.sum(dh * xn, axis=0, keepdims=True)
        dxn = dh * g_ref[...]
        dx_ref[...] = rs * (dxn - xn * jnp.mean(dxn * xn, axis=-1, keepdims=True)) + dx2_ref[...]

        @pl.when(pl.program_id(0) == s // tile - 1)
        def _():
            wait()

    tok = lambda n: pl.BlockSpec((tile, n), lambda i: (i, 0))
    fixed = lambda a: pl.BlockSpec(a.shape, lambda i: (0,) * a.ndim)
    out = pl.pallas_call(
        body, name="proj_xgrad", grid=(s // tile,),
        in_specs=([tok(D), fixed(g), tok(D)] + [tok(du.shape[1]) for du in dus] + [fixed(w) for w in ws]
                  + _hbm_specs(nx)),
        out_specs=[tok(D), pl.BlockSpec((1, D), lambda i: (0, 0))] + _hbm_specs(nx),
        out_shape=[jax.ShapeDtypeStruct((s, D), F32), jax.ShapeDtypeStruct((1, D), F32)] + _received_shapes(slabs, owners),
        scratch_shapes=_exchange_scratch(nx),
        compiler_params=_params("arbitrary"))(x, g, dx2, *dus, *ws, *slabs)
    return out[0], out[1], out[2:]


def _tail(x, target, ya, o, ub, ug, w_oa, w_ob, w_o, fg):
    s = x.shape[0]
    tile = TOK_TILE

    def body(x_ref, t_ref, ya_ref, o_ref, gb_ref, ug_ref, woa_ref, wob_ref, wo_ref, fg_ref,
             loss_ref, dfg_ref, dwo_ref, dwoa_ref, dwob_ref, dx2_ref, dya_ref, do_ref, dgb_ref, dug_ref):
        @pl.when(pl.program_id(0) == 0)
        def _():
            for r in (loss_ref, dfg_ref, dwo_ref, dwoa_ref, dwob_ref):
                r[...] = jnp.zeros_like(r)

        ya_v = ya_ref[...]
        gate_b = gb_ref[...]
        sg_b = _sigmoid(gate_b)
        silu_b = gate_b * sg_b
        o_v = jnp.concatenate([o_ref[h] for h in range(H)], axis=-1)
        yb_v = o_v * silu_b
        big_a = _bdot(ya_v, woa_ref[...])
        big_b = _bdot(yb_v, wob_ref[...])
        sa = _sigmoid(ug_ref[:, :D])
        sb = _sigmoid(ug_ref[:, D:])
        merged = sa * big_a + sb * big_b
        x2 = x_ref[...] + _bdot(merged, wo_ref[...])
        rs = lax.rsqrt(jnp.mean(x2 * x2, axis=-1, keepdims=True) + RMS_EPS)
        xn = x2 * rs
        err = xn * fg_ref[...] - t_ref[...]
        loss_ref[...] += (0.5 / D) * jnp.sum(err * err)
        dout = err * (1.0 / D)
        dfg_ref[...] += jnp.sum(dout * xn, axis=0, keepdims=True)
        dxn = dout * fg_ref[...]
        dx2 = rs * (dxn - xn * jnp.mean(dxn * xn, axis=-1, keepdims=True))
        dx2_ref[...] = dx2
        dwo_ref[...] += _bdot_tn(merged, dx2)
        dmerged = _bdot_nt(dx2, wo_ref[...])
        dbig_a = dmerged * sa
        dbig_b = dmerged * sb
        dug_ref[:, :D] = (dmerged * big_a * sa * (1.0 - sa)).astype(BF16)
        dug_ref[:, D:] = (dmerged * big_b * sb * (1.0 - sb)).astype(BF16)
        dwoa_ref[...] += _bdot_tn(ya_v, dbig_a)
        dwob_ref[...] += _bdot_tn(yb_v, dbig_b)
        dya_ref[...] = _bdot_nt(dbig_a, woa_ref[...])
        dyb = _bdot_nt(dbig_b, wob_ref[...])
        dgb_ref[...] = dyb * o_v * (sg_b * (1.0 + gate_b * (1.0 - sg_b)))
        _dov = dyb * silu_b
        for h in range(H):
            do_ref[h] = _dov[:, N * h:N * (h + 1)]

    tok = lambda n: pl.BlockSpec((tile, n), lambda i: (i, 0))
    hm = pl.BlockSpec((H, tile, N), lambda i: (0, i, 0))
    fixed = lambda shape: pl.BlockSpec(shape, lambda i: (0,) * len(shape))
    f32 = lambda *shape: jax.ShapeDtypeStruct(shape, F32)
    return pl.pallas_call(
        body, name="tail", grid=(s // tile,),
        in_specs=[tok(D), tok(D), tok(DA), hm, pl.BlockSpec((tile, DA), lambda i: (i, 3)), tok(NG),
                  fixed((DA, D)), fixed((DA, D)), fixed((D, D)), fixed((1, D))],
        out_specs=[fixed((1, 1)), fixed((1, D)), fixed((D, D)), fixed((DA, D)), fixed((DA, D)),
                   tok(D), tok(DA), hm, tok(DA), tok(NG)],
        out_shape=[f32(1, 1), f32(1, D), f32(D, D), f32(DA, D), f32(DA, D),
                   f32(s, D), f32(s, DA), f32(H, s, N), f32(s, DA), jax.ShapeDtypeStruct((s, NG), BF16)],
        compiler_params=_params("arbitrary"))(x, target, ya, o, ub, ug, w_oa, w_ob, w_o, fg)


def _pre_operands(ua_ref, prev_ref, first):
    cur = ua_ref[...]
    t = cur.shape[0]
    prev_row = jnp.where(first, 0.0, prev_ref[7:8, :])
    rows = lax.broadcasted_iota(jnp.int32, cur.shape, 0)
    sh = jnp.where(rows == 0, prev_row, pltpu.roll(cur, 1, axis=0))
    ops = []
    for c0, n in ((0, DA), (DA, DA), (2 * DA, DA), (3 * DA + 2 * RANK, DA), (3 * DA, RANK), (3 * DA + RANK, RANK)):
        ops += [cur[:, c0:c0 + n], sh[:, c0:c0 + n]]
    del t
    return ops


def _ua_specs(tile, order):
    blocks = tile // 8
    return [pl.BlockSpec((tile, NA), lambda i: (order(i), 0)),
            pl.BlockSpec((8, NA), lambda i: (jnp.maximum(order(i) * blocks - 1, 0), 0))]


def _rwkv_pre_fwd(ua, pre_params):
    s = ua.shape[0]
    tile = HEAD_TILE

    def body(ua_ref, prev_ref, *refs):
        p_refs, o_refs = refs[:len(pre_params)], refs[len(pre_params):]
        ops = _pre_operands(ua_ref, prev_ref, pl.program_id(0) == 0)
        outs = _rwkv_pre(*ops, *[p[...] for p in p_refs])
        for o_ref, val in zip(o_refs, outs):
            o_ref[...] = val

    tm = pl.BlockSpec((tile, DA), lambda i: (i, 0))
    return pl.pallas_call(
        body, name="rwkv_pre_fwd", grid=(s // tile,),
        in_specs=_ua_specs(tile, lambda i: i) + [pl.BlockSpec(p.shape, lambda i, nd=p.ndim: (0,) * nd) for p in pre_params],
        out_specs=[tm] * 8, out_shape=[jax.ShapeDtypeStruct((s, DA), F32)] * 8,
        compiler_params=_params("arbitrary"))(ua, ua, *pre_params)


def _rwkv_pre_bwd(ua, pre_params, cots):
    s = ua.shape[0]
    tile = HEAD_TILE
    nt = s // tile
    n_p = len(pre_params)

    def body(ua_ref, prev_ref, *refs):
        p_refs, c_refs = refs[:n_p], refs[n_p:n_p + 11]
        dua_ref = refs[n_p + 11]
        dp_refs = refs[n_p + 12:n_p + 12 + n_p]
        carry_ref = refs[-1]
        i = pl.program_id(0)

        @pl.when(i == 0)
        def _():
            carry_ref[...] = jnp.zeros_like(carry_ref)
            for r in dp_refs:
                r[...] = jnp.zeros_like(r)

        ops = _pre_operands(ua_ref, prev_ref, i == nt - 1)
        _, vjp = jax.vjp(_rwkv_pre, *ops, *[p[...] for p in p_refs])
        c = [r[...] for r in c_refs]
        grads = vjp((c[0] + c[1], c[2], c[3], c[4] + c[5], c[6] + c[7], c[8], c[9], c[10]))
        d_ops, d_par = grads[:12], grads[12:]
        for r, val in zip(dp_refs, d_par):
            r[...] += val
        d_cur = jnp.concatenate([d_ops[0], d_ops[2], d_ops[4], d_ops[8], d_ops[10], d_ops[6]], axis=-1)
        d_sh = jnp.concatenate([d_ops[1], d_ops[3], d_ops[5], d_ops[9], d_ops[11], d_ops[7]], axis=-1)
        rows = lax.broadcasted_iota(jnp.int32, d_sh.shape, 0)
        dua = d_cur + jnp.where(rows == tile - 1, carry_ref[...], pltpu.roll(d_sh, tile - 1, axis=0))
        dua_ref[...] = dua.astype(BF16)
        carry_ref[...] = d_sh[0:1, :]

    rev = lambda i: nt - 1 - i
    tm = pl.BlockSpec((tile, DA), lambda i: (rev(i), 0))
    fixed = [pl.BlockSpec(p.shape, lambda i, nd=p.ndim: (0,) * nd) for p in pre_params]
    return pl.pallas_call(
        body, name="rwkv_pre_bwd", grid=(nt,),
        in_specs=_ua_specs(tile, rev) + fixed + [tm] * 11,
        out_specs=[pl.BlockSpec((tile, NA), lambda i: (rev(i), 0))] + fixed,
        out_shape=[jax.ShapeDtypeStruct((s, NA), BF16)] + [jax.ShapeDtypeStruct(p.shape, F32) for p in pre_params],
        scratch_shapes=[pltpu.VMEM((1, NA), F32)],
        compiler_params=_params("arbitrary"))(ua, ua, *pre_params, *cots)


def _wkv_fwd(seq):
    s = seq[0].shape[0]
    nc = s // WKV_CHUNK

    def body(r_ref, lw_ref, cl_ref, k_ref, v_ref, a_ref, b_ref, y_ref, ck_ref, p_ref, state):
        @pl.when(pl.program_id(0) == 0)
        def _():
            state[...] = jnp.zeros_like(state)

        s0 = state[...]
        for i in range(WKV_STEP_CHUNKS):
            rows = slice(i * WKV_CHUNK, (i + 1) * WKV_CHUNK)
            r, lw, cl, k, v, a, b = (jnp.stack(_to_heads(ref[rows, :])) for ref in (r_ref, lw_ref, cl_ref, k_ref, v_ref,
                                                                                     a_ref, b_ref))
            ck_ref[i] = s0
            p = _tri_inverse(_wkv_aab(lw, cl, a, b))
            p_ref[i] = p
            y, s0 = _wkv_apply(s0, r, lw, cl, k, v, a, b, p)
            y_ref[rows, :] = jnp.concatenate([y[h] for h in range(H)], axis=-1)
        state[...] = s0

    tm = pl.BlockSpec((WKV_STEP_CHUNKS * WKV_CHUNK, DA), lambda c: (c, 0))
    per_chunk = lambda m: pl.BlockSpec((WKV_STEP_CHUNKS, H, m, m), lambda c: (c, 0, 0, 0))
    return pl.pallas_call(
        body, name="wkv_fwd", grid=(nc // WKV_STEP_CHUNKS,), in_specs=[tm] * 7,
        out_specs=[tm, per_chunk(N), per_chunk(WKV_CHUNK)],
        out_shape=[jax.ShapeDtypeStruct((s, DA), F32), jax.ShapeDtypeStruct((nc, H, N, N), F32),
                   jax.ShapeDtypeStruct((nc, H, WKV_CHUNK, WKV_CHUNK), F32)],
        scratch_shapes=[pltpu.VMEM((H, N, N), F32)], compiler_params=_params("arbitrary"))(*seq)


def _wkv_bwd(seq, ckpt, pinv, dy, slabs, owners):
    s = seq[0].shape[0]
    nc = s // WKV_CHUNK
    nx = len(slabs)

    def body(r_ref, lw_ref, cl_ref, k_ref, v_ref, a_ref, b_ref, ck_ref, p_ref, dy_ref, *refs):
        src_refs, d_refs, dst_refs = refs[:nx], refs[nx:nx + 7], refs[nx + 7:2 * nx + 7]
        dstate = refs[2 * nx + 7]
        start, wait = _exchange_ops(src_refs, dst_refs, owners, refs[2 * nx + 8:])

        @pl.when(pl.program_id(0) == 0)
        def _():
            dstate[...] = jnp.zeros_like(dstate)
            start()

        ds = dstate[...]
        for i in reversed(range(WKV_STEP_CHUNKS)):
            rows = slice(i * WKV_CHUNK, (i + 1) * WKV_CHUNK)
            p = p_ref[i]
            r, lw, cl, k, v, a, b, dy = (jnp.stack(_to_heads(ref[rows, :])) for ref in (r_ref, lw_ref, cl_ref, k_ref,
                                                                                         v_ref, a_ref, b_ref, dy_ref))
            _, vjp = jax.vjp(_wkv_apply, ck_ref[i], r, lw, cl, k, v, a, b, p)
            ds, dr, dlw, dcl, dk, dv, da, db, dp = vjp((dy, ds))
            _, vjp_x = jax.vjp(_wkv_aab, lw, cl, a, b)
            dlw2, dcl2, da2, db2 = vjp_x(_dot1(_dot1(p, dp, "tn"), p, "nt"))
            for d_ref, val in zip(d_refs, (dr, dlw + dlw2, dcl + dcl2, dk, dv, da + da2, db + db2)):
                d_ref[rows, :] = jnp.concatenate([val[h] for h in range(H)], axis=-1)
        dstate[...] = ds

        @pl.when(pl.program_id(0) == steps - 1)
        def _():
            wait()

    steps = nc // WKV_STEP_CHUNKS
    tm = pl.BlockSpec((WKV_STEP_CHUNKS * WKV_CHUNK, DA), lambda c: (steps - 1 - c, 0))
    per_chunk = lambda m: pl.BlockSpec((WKV_STEP_CHUNKS, H, m, m), lambda c: (steps - 1 - c, 0, 0, 0))
    out = pl.pallas_call(
        body, name="wkv_bwd", grid=(steps,),
        in_specs=[tm] * 7 + [per_chunk(N), per_chunk(WKV_CHUNK), tm] + _hbm_specs(nx),
        out_specs=[tm] * 7 + _hbm_specs(nx),
        out_shape=[jax.ShapeDtypeStruct((s, DA), F32)] * 7 + _received_shapes(slabs, owners),
        scratch_shapes=[pltpu.VMEM((H, N, N), F32)] + _exchange_scratch(nx),
        compiler_params=_params("arbitrary"))(*seq, ckpt, pinv, dy, *slabs)
    return out[:7], out[7:]


def _rwkv_post_fwd(y, r, k2, v, g, post_params):
    s = y.shape[0]
    tile = TOK_TILE

    def body(*refs):
        refs[-1][...] = _rwkv_post(*[ref[...] for ref in refs[:-1]])

    tm = pl.BlockSpec((tile, DA), lambda i: (i, 0))
    par = pl.BlockSpec((1, DA), lambda i: (0, 0))
    return pl.pallas_call(
        body, name="rwkv_post_fwd", grid=(s // tile,), in_specs=[tm] * 5 + [par] * 3,
        out_specs=tm, out_shape=jax.ShapeDtypeStruct((s, DA), F32),
        compiler_params=_params("arbitrary"))(y, r, k2, v, g, *post_params)


def _rwkv_post_bwd(y, r, k2, v, g, post_params, dya, slabs, lo):
    s = y.shape[0]
    tile = HEAD_TILE

    def body(y_ref, r_ref, k_ref, v_ref, g_ref, w_ref, b_ref, rk_ref, dya_ref, s_ref, *refs):
        d_refs, p_ref = refs[:8], refs[8]
        start, wait = _pair_swap_ops(s_ref, p_ref, lo, refs[9:])

        @pl.when(pl.program_id(0) == 0)
        def _():
            for ref in d_refs[5:]:
                ref[...] = jnp.zeros_like(ref)
            start()

        _, vjp = jax.vjp(_rwkv_post, *[ref[...] for ref in (y_ref, r_ref, k_ref, v_ref, g_ref, w_ref, b_ref, rk_ref)])
        grads = vjp(dya_ref[...])
        for ref, val in zip(d_refs[:5], grads[:5]):
            ref[...] = val
        for ref, val in zip(d_refs[5:], grads[5:]):
            ref[...] += val

        @pl.when(pl.program_id(0) == s // tile - 1)
        def _():
            wait()

    tm = pl.BlockSpec((tile, DA), lambda i: (i, 0))
    par = pl.BlockSpec((1, DA), lambda i: (0, 0))
    return pl.pallas_call(
        body, name="rwkv_post_bwd", grid=(s // tile,),
        in_specs=[tm] * 5 + [par] * 3 + [tm] + _hbm_specs(1),
        out_specs=[tm] * 5 + [par] * 3 + _hbm_specs(1),
        out_shape=[jax.ShapeDtypeStruct((s, DA), F32)] * 5 + [jax.ShapeDtypeStruct((1, DA), F32)] * 3
        + [jax.ShapeDtypeStruct(slabs.shape, slabs.dtype)],
        scratch_shapes=_pair_swap_scratch(slabs.shape[0]),
        compiler_params=_params("arbitrary"))(y, r, k2, v, g, *post_params, dya, slabs)


def _tri(t):
    return (lax.broadcasted_iota(jnp.int32, (t, t), 0) >= lax.broadcasted_iota(jnp.int32, (t, t), 1)).astype(F32)


def _fox_pre_fwd(ub, uf, q_g, k_g, f_b):
    s = ub.shape[0]
    tile = HEAD_TILE

    def body(ub_ref, uf_ref, qg_ref, kg_ref, fb_ref, q_ref, k_ref, v_ref, cum_ref, carry):
        @pl.when(pl.program_id(0) == 0)
        def _():
            carry[...] = jnp.zeros_like(carry)

        qn, kn, logf = _fox_pre(ub_ref[:, :DA], ub_ref[:, DA:2 * DA], uf_ref[...], qg_ref[...], kg_ref[...],
                                fb_ref[...])
        for h, (q_col, k_col) in enumerate(zip(_to_heads(qn), _to_heads(kn))):
            q_ref[h] = q_col
            k_ref[h] = k_col
        v_ref[...] = _heads(ub_ref, 2 * DA)
        cum = jnp.dot(_tri(tile), logf, precision=HI, preferred_element_type=F32) + carry[...]
        cum_ref[...] = cum
        carry[...] = cum[tile - 1:tile, :]

    hm = pl.BlockSpec((H, tile, N), lambda i: (0, i, 0))
    fixed = lambda shape: pl.BlockSpec(shape, lambda i: (0,) * len(shape))
    return pl.pallas_call(
        body, name="fox_pre_fwd", grid=(s // tile,),
        in_specs=[pl.BlockSpec((tile, NB), lambda i: (i, 0)), pl.BlockSpec((tile, NF), lambda i: (i, 0)),
                  fixed((1, DA)), fixed((1, DA)), fixed((1, NF))],
        out_specs=[hm] * 3 + [pl.BlockSpec((tile, NF), lambda i: (i, 0))],
        out_shape=[jax.ShapeDtypeStruct((H, s, N), F32)] * 3 + [jax.ShapeDtypeStruct((s, NF), F32)],
        scratch_shapes=[pltpu.VMEM((1, NF), F32)], compiler_params=_params("arbitrary"))(ub, uf, q_g, k_g, f_b)


def _fox_pre_bwd(ub, uf, q_g, k_g, f_b, dqn, dkn, dvf, dgate, dcum_q, dcum_k):
    s = ub.shape[0]
    tile = HEAD_TILE
    nt = s // tile

    def body(ub_ref, uf_ref, qg_ref, kg_ref, fb_ref, dq_ref, dk_ref, dv_ref, dgate_ref, dcq_ref, dck_ref,
             dub_ref, duf_ref, dqg_ref, dkg_ref, dfb_ref, carry):
        @pl.when(pl.program_id(0) == 0)
        def _():
            carry[...] = jnp.zeros_like(carry)
            for ref in (dqg_ref, dkg_ref, dfb_ref):
                ref[...] = jnp.zeros_like(ref)

        dcum = dcq_ref[...] + dck_ref[...]
        dlogf = lax.dot_general(_tri(tile), dcum, (((0,), (0,)), ((), ())), precision=HI,
                                preferred_element_type=F32) + carry[...]
        carry[...] = dlogf[0:1, :]
        _, vjp = jax.vjp(_fox_pre, ub_ref[:, :DA], ub_ref[:, DA:2 * DA], uf_ref[...], qg_ref[...], kg_ref[...],
                         fb_ref[...])
        d_q, d_k, d_f, d_qg, d_kg, d_fb = vjp((_from_heads(dq_ref), _from_heads(dk_ref), dlogf))
        dub_ref[...] = jnp.concatenate([d_q, d_k, _from_heads(dv_ref), dgate_ref[...]], axis=-1).astype(BF16)
        duf_ref[...] = d_f.astype(BF16)
        dqg_ref[...] += functools.reduce(jnp.add, _to_heads(d_qg))
        dkg_ref[...] += functools.reduce(jnp.add, _to_heads(d_kg))
        dfb_ref[...] += d_fb

    rev = lambda i: nt - 1 - i
    hm = pl.BlockSpec((H, tile, N), lambda i: (0, rev(i), 0))
    tok = lambda n: pl.BlockSpec((tile, n), lambda i: (rev(i), 0))
    fixed = lambda shape: pl.BlockSpec(shape, lambda i: (0,) * len(shape))
    return pl.pallas_call(
        body, name="fox_pre_bwd", grid=(nt,),
        in_specs=[tok(NB), tok(NF), fixed((1, DA)), fixed((1, DA)), fixed((1, NF)), hm, hm, hm, tok(DA), tok(NF),
                  tok(NF)],
        out_specs=[tok(NB), tok(NF), fixed((1, N)), fixed((1, N)), fixed((1, NF))],
        out_shape=[jax.ShapeDtypeStruct((s, NB), BF16), jax.ShapeDtypeStruct((s, NF), BF16),
                   jax.ShapeDtypeStruct((1, N), F32), jax.ShapeDtypeStruct((1, N), F32),
                   jax.ShapeDtypeStruct((1, NF), F32)],
        scratch_shapes=[pltpu.VMEM((1, NF), F32)],
        compiler_params=_params("arbitrary"))(ub, uf, q_g, k_g, f_b, dqn, dkn, dvf, dgate, dcum_q, dcum_k)


def _att_groups(s):
    blocks = s // ATT_TILE
    per = max(1, blocks // ATT_GROUPS)
    return per, blocks // per


def _att_parts(n, width):
    return ([(0, n - width, False)] if n > width else []) + [(n - width, n, True)]


def _att_scores(q_bf, k_ref, ck_ref, lo, hi, masked, row_offset):
    scores = _bdot_nt(q_bf, k_ref[0, lo:hi, :]) - ck_ref[0, :, lo:hi]
    if masked:
        rows = row_offset + lax.broadcasted_iota(jnp.int32, scores.shape, 0)
        scores = jnp.where(rows >= lax.broadcasted_iota(jnp.int32, scores.shape, 1), scores, -1e30)
    return scores


def _fox_attn_fwd(q, k, v, cum_q, cum_k):
    s = q.shape[1]
    t = ATT_TILE
    per, groups = _att_groups(s)

    def body(q_ref, k_ref, v_ref, cq_ref, ck_ref, o_ref, lse_ref):
        qi = pl.program_id(1)
        for g in range(groups):
            @pl.when(qi // per == g)
            def _(g=g):
                q_bf = (q_ref[0] * ATT_SCALE).astype(BF16)
                parts = _att_parts((g + 1) * per * t, per * t)
                scores = [_att_scores(q_bf, k_ref, ck_ref, lo, hi, masked, (qi - g * per) * t)
                          for lo, hi, masked in parts]
                m = functools.reduce(jnp.maximum, [jnp.max(sc, axis=-1, keepdims=True) for sc in scores])
                l, acc = 0.0, 0.0
                for sc, (lo, hi, _) in zip(scores, parts):
                    p = jnp.exp(sc - m)
                    l += jnp.sum(p, axis=-1, keepdims=True)
                    acc += _bdot(p, v_ref[0, lo:hi, :])
                o_ref[0] = acc / l
                lse_ref[0] = m + jnp.log(l) + cq_ref[0]

    qb = pl.BlockSpec((1, t, N), lambda h, i: (h, i, 0))
    kb = pl.BlockSpec((1, s, N), lambda h, i: (h, 0, 0))
    return pl.pallas_call(
        body, name="fox_attn_fwd", grid=(H, s // t),
        in_specs=[qb, kb, kb, pl.BlockSpec((1, t, 1), lambda h, i: (h, i, 0)),
                  pl.BlockSpec((1, 1, s), lambda h, i: (h, 0, 0))],
        out_specs=[qb, pl.BlockSpec((1, t, 1), lambda h, i: (h, i, 0))],
        out_shape=[jax.ShapeDtypeStruct((H, s, N), F32), jax.ShapeDtypeStruct((H, s, 1), F32)],
        compiler_params=_params("arbitrary", "arbitrary"))(q, k, v, cum_q, cum_k)


def _fox_attn_bwd(q, k, v, cum_q, cum_k, o, lse, do, slabs, owners):
    s = q.shape[1]
    t = ATT_TILE
    per, groups = _att_groups(s)
    nx = len(slabs)

    def body(q_ref, k_ref, v_ref, cq_ref, ck_ref, o_ref, lse_ref, do_ref, *refs):
        src_refs, (dq_ref, dk_ref, dv_ref, dcq_ref, dck_ref) = refs[:nx], refs[nx:nx + 5]
        start, wait = _exchange_ops(src_refs, refs[nx + 5:2 * nx + 5], owners, refs[2 * nx + 5:])
        qi = pl.program_id(1)

        @pl.when((pl.program_id(0) == 0) & (qi == 0))
        def _():
            start()

        @pl.when(qi == 0)
        def _():
            for ref in (dk_ref, dv_ref, dck_ref):
                ref[...] = jnp.zeros_like(ref)

        for g in range(groups):
            @pl.when(qi // per == g)
            def _(g=g):
                q_bf, do_bf = (q_ref[0] * ATT_SCALE).astype(BF16), do_ref[0].astype(BF16)
                row_term = cq_ref[0] - lse_ref[0]
                delta = jnp.sum(do_ref[0] * o_ref[0], axis=-1, keepdims=True)
                dq, dcq = 0.0, 0.0
                for lo, hi, masked in _att_parts((g + 1) * per * t, per * t):
                    p = jnp.exp(_att_scores(q_bf, k_ref, ck_ref, lo, hi, masked, (qi - g * per) * t) + row_term)
                    ds = p * (_bdot_nt(do_bf, v_ref[0, lo:hi, :]) - delta)
                    dq += _bdot(ds, k_ref[0, lo:hi, :])
                    dcq += jnp.sum(ds, axis=-1, keepdims=True)
                    dk_ref[0, lo:hi, :] += _bdot_tn(ds, q_bf)
                    dv_ref[0, lo:hi, :] += _bdot_tn(p, do_bf)
                    dck_ref[0, :, lo:hi] -= jnp.sum(ds, axis=0, keepdims=True)
                dq_ref[0] = dq * ATT_SCALE
                dcq_ref[0] = dcq

        @pl.when((pl.program_id(0) == H - 1) & (qi == s // t - 1))
        def _():
            wait()

    qb = pl.BlockSpec((1, t, N), lambda h, i: (h, i, 0))
    kb = pl.BlockSpec((1, s, N), lambda h, i: (h, 0, 0))
    cqb = pl.BlockSpec((1, t, 1), lambda h, i: (h, i, 0))
    ckb = pl.BlockSpec((1, 1, s), lambda h, i: (h, 0, 0))
    f32 = lambda *shape: jax.ShapeDtypeStruct(shape, F32)
    out = pl.pallas_call(
        body, name="fox_attn_bwd", grid=(H, s // t),
        in_specs=[qb, kb, kb, cqb, ckb, qb, cqb, qb] + _hbm_specs(nx), out_specs=[qb, kb, kb, cqb, ckb] + _hbm_specs(nx),
        out_shape=[f32(H, s, N), f32(H, s, N), f32(H, s, N), f32(H, s, 1), f32(H, 1, s)]
        + _received_shapes(slabs, owners),
        scratch_shapes=_exchange_scratch(nx),
        compiler_params=_params("arbitrary", "arbitrary"))(q, k, v, cum_q, cum_k, o, lse, do, *slabs)
    return out[:5], out[5:]


def _local_step(x, target, w, p):
    mu = p["shift_mu"]
    lora_matrix = lambda a: jnp.moveaxis(a, 0, 1).reshape(RANK, DA).astype(F32)
    pre_params = (mu[:, 0:DA], mu[:, DA:2 * DA], mu[:, 2 * DA:3 * DA], mu[:, 3 * DA + 2 * RANK:],
                  mu[:, 3 * DA:3 * DA + RANK], mu[:, 3 * DA + RANK:3 * DA + 2 * RANK],
                  lora_matrix(w["w_lora_up"]), p["w0"], lora_matrix(w["a_lora_up"]), p["a0"], p["k_k"], p["k_a"])
    post_params = (p["lnx_w"], p["lnx_b"], p["r_k"])
    q_g, k_g = jnp.tile(p["q_norm_g"], (1, H)), jnp.tile(p["k_norm_g"], (1, H))
    f_b = jnp.pad(p["f_bias"], ((0, 0), (0, NF - H)))
    fg = p["final_norm_g"].reshape(1, D)

    h, (ua, ub, ug, uf) = _norm_proj(x, p["norm_g"], (w["in_a"], w["in_b"], w["in_g"], w["in_f"]))
    r, lw, cl, k2, v, av, bv, gg = _rwkv_pre_fwd(ua, pre_params)
    y, ckpt, pinv = _wkv_fwd((r, lw, cl, k2, v, av, bv))
    ya = _rwkv_post_fwd(y, r, k2, v, gg, post_params)
    qn, kn, vf, cum = _fox_pre_fwd(ub, uf, q_g, k_g, f_b)
    cum_t = cum[:, :H].T
    cum_q, cum_k = cum_t[:, :, None], cum_t[:, None, :]
    o, lse = _fox_attn_fwd(qn, kn, vf, cum_q, cum_k)

    (loss, dfg, dwo, dwoa, dwob, dx2, dya, do, dgate_b, dug) = _tail(
        x, target, ya, o, ub, ug, w["w_out_a"], w["w_out_b"], w["w_out"], fg)
    everyone = (0, N_DEV)
    (dqn, dkn, dvf, dcq, dck), (recv_woa, recv_wob, recv_wo) = _fox_attn_bwd(
        qn, kn, vf, cum_q, cum_k, o, lse, do,
        (_col_slabs(dwoa), _col_slabs(dwob), dwo.astype(BF16).reshape(N_DEV, D // N_DEV, D)), (everyone,) * 3)
    pad_f = lambda a: jnp.pad(a.T, ((0, 0), (0, NF - H)))
    dub, duf, dqg, dkg, dfb = _fox_pre_bwd(ub, uf, q_g, k_g, f_b, dqn, dkn, dvf, dgate_b,
                                           pad_f(dcq[:, :, 0]), pad_f(dck.reshape(H, -1)))
    spill = EARLY_FROM * COLS_PER_DEV - NA
    early, dwt_b_head = _proj_wgrad_early(h, dub, dug, duf, -(-spill // 8) * 8)
    dy, dr_p, dk_p, dv_p, dgg, dlnw, dlnb, drk, handed = _rwkv_post_bwd(y, r, k2, v, gg, post_params, dya, early,
                                                                          EARLY_FROM)
    early = _chip_sums(early, handed, EARLY_FROM, "chip_sums_early")
    (dr_s, dlw, dcl, dk_s, dv_s, dav, dbv), (recv_early,) = _wkv_bwd(
        (r, lw, cl, k2, v, av, bv), ckpt, pinv, dy, (early,), ((EARLY_FROM, N_DEV, "chips"),))
    pre_out = _rwkv_pre_bwd(ua, pre_params, (dr_s, dr_p, dlw, dcl, dk_s, dk_p, dv_s, dv_p, dav, dbv, dgg))
    dua, dpre = pre_out[0], pre_out[1:]
    late = _proj_wgrad_late(h, dua, dwt_b_head)

    flat = lambda a: a.reshape(1, -1)
    small = {
        "final_norm_g": dfg, "w0": dpre[7], "a0": dpre[9], "k_k": dpre[10], "k_a": dpre[11], "r_k": drk, "lnx_w": dlnw,
        "lnx_b": dlnb, "q_norm_g": dqg, "k_norm_g": dkg, "f_bias": dfb[:, :H],
        "shift_mu": jnp.concatenate([flat(dpre[0]), flat(dpre[1]), flat(dpre[2]), dpre[4], dpre[5], flat(dpre[3])], axis=1),
    }
    late = _chip_sums(late, _pair_swap(late, 0, "pair_swap_late"), 0, "chip_sums_late")
    by_head = lambda a: jnp.moveaxis(a.reshape(RANK, H, N), 1, 0)
    loras = jnp.stack([by_head(dpre[6]), by_head(dpre[8])], axis=1).astype(BF16)
    dx, dng, (recv_late, recv_lora, recv_small) = _proj_xgrad(
        x, p["norm_g"], dx2, (dua, dub, dug, duf), (w["in_a"], w["in_b"], w["in_g"], w["in_f"]),
        (late, loras, _pack_small(small, loss)), ((0, EARLY_FROM, "chips"), everyone, everyone))
    return dx, dng, (recv_early, recv_late), (recv_woa, recv_wob, recv_wo, recv_lora), recv_small


def _position():
    return lax.axis_index("x"), lax.axis_index("y"), lax.axis_index("c")


def _hbm_specs(n):
    return [pl.BlockSpec(memory_space=pl.ANY)] * n


BIG_GATHER_COPIES = 13
GATHER_ROW_CUT = 400


def _all_gather(big, blocks, name):
    n = len(blocks)

    def body(*refs):
        big_ref, x_refs = refs[0], refs[1:1 + n]
        big_out, out_refs = refs[1 + n], refs[2 + n:2 + 2 * n]
        send_sems, recv_sems, local_sems = refs[2 + 2 * n:]
        x, y, c = _position()
        me, sibling = (x, y, c), (x, y, 1 - c)
        chips = [(1 - x, y), (x, 1 - y), (1 - x, 1 - y)]
        x_nbr, y_nbr, diag = chips
        rows = big_ref.shape[0]
        cut = GATHER_ROW_CUT

        def part(ref, h):
            return ref if h is None else ref.at[pl.ds(0, cut)] if h == 0 else ref.at[pl.ds(cut, rows - cut)]

        def landed(chip, core, h):
            return part(big_out.at[4 * chip[0] + 2 * chip[1] + core], h)

        def big_copy(k, src, dst, to):
            return pltpu.make_async_remote_copy(src_ref=src, dst_ref=dst, send_sem=send_sems.at[7 * n + k],
                                                recv_sem=recv_sems.at[7 * n + k], device_id=to, device_id_type=MESH)

        def arrival(k, chip, core, h):
            dst = landed(chip, core, h)
            return big_copy(k, dst, dst, me)

        def pass_on(k, chip, h, to):
            src = landed(chip, c, h)
            return big_copy(k, src, src, to)

        big_mine = pltpu.make_async_copy(big_ref, landed((x, y), c, None), local_sems.at[n])
        big_mine.start()
        here = (x, y)
        big_sent = [big_copy(0, big_ref, landed(here, c, None), sibling),
                    big_copy(1, part(big_ref, 0), landed(here, c, 0), (*x_nbr, c)),
                    big_copy(2, part(big_ref, 1), landed(here, c, 1), (*y_nbr, c)),
                    big_copy(3, part(big_ref, 1), landed(here, c, 1), (*x_nbr, c)),
                    big_copy(4, part(big_ref, 0), landed(here, c, 0), (*y_nbr, c))]
        for cp in big_sent:
            cp.start()

        def copy(a, k, blk, to, own=False):
            dst = out_refs[a].at[4 * blk[0] + 2 * blk[1] + blk[2]]
            return pltpu.make_async_remote_copy(
                src_ref=x_refs[a] if own else dst, dst_ref=dst, send_sem=send_sems.at[7 * a + k],
                recv_sem=recv_sems.at[7 * a + k], device_id=to, device_id_type=MESH)

        mine = [pltpu.make_async_copy(x_refs[a], out_refs[a].at[4 * x + 2 * y + c], local_sems.at[a]) for a in range(n)]
        for cp in mine:
            cp.start()
        first = []
        for a in range(n):
            first.append(copy(a, 0, me, sibling, own=True))
            first += [copy(a, 1 + j, me, (*chip, c), own=True) for j, chip in enumerate(chips)]
        for cp in first:
            cp.start()

        big_steps = [(1, x_nbr, 0, (*y_nbr, c), 5, 7), (2, y_nbr, 1, (*x_nbr, c), 6, 8), (3, x_nbr, 1, None, None, 9),
                     (4, y_nbr, 0, None, None, 10), (5, diag, 0, None, None, 11), (6, diag, 1, None, None, 12)]
        for k, chip, h, onward, k_onward, k_sibling in big_steps:
            arrival(k, chip, c, h).wait_recv()
            if onward is not None:
                big_sent.append(pass_on(k_onward, chip, h, onward))
                big_sent[-1].start()
            big_sent.append(pass_on(k_sibling, chip, h, sibling))
            big_sent[-1].start()

        passed = []
        for j, chip in enumerate(chips):
            for a in range(n):
                copy(a, 1 + j, (*chip, c), me).wait_recv()
                passed.append(copy(a, 4 + j, (*chip, c), sibling))
                passed[-1].start()
        for a in range(n):
            copy(a, 0, sibling, me).wait_recv()
        for j, chip in enumerate(chips):
            for a in range(n):
                copy(a, 4 + j, (*chip, 1 - c), me).wait_recv()
        arrival(0, here, 1 - c, None).wait_recv()
        for k, chip, h, _, _, k_sibling in big_steps:
            arrival(k_sibling, chip, 1 - c, h).wait_recv()
        for cp in first + passed + big_sent:
            cp.wait_send()
        for cp in mine + [big_mine]:
            cp.wait()

    everything = [big] + list(blocks)
    return pl.pallas_call(
        body, name=name, out_shape=[jax.ShapeDtypeStruct((N_DEV,) + b.shape, b.dtype) for b in everything],
        in_specs=_hbm_specs(n + 1), out_specs=_hbm_specs(n + 1),
        scratch_shapes=[pltpu.SemaphoreType.DMA((7 * n + BIG_GATHER_COPIES,)),
                        pltpu.SemaphoreType.DMA((7 * n + BIG_GATHER_COPIES,)), pltpu.SemaphoreType.DMA((n + 1,))],
    )(*everything)


def _received_shapes(slabs, owners):
    return [jax.ShapeDtypeStruct((N_DEV // 2 if len(o) == 3 else N_DEV,) + s.shape[1:], s.dtype)
            for s, o in zip(slabs, owners)]


def _pair_swap_scratch(n):
    return [pltpu.SemaphoreType.DMA((n,)), pltpu.SemaphoreType.DMA((n,))]


def _pair_swap_ops(s_ref, p_ref, lo, sems):
    send_sems, recv_sems = sems
    n = s_ref.shape[0]

    def run(sending):
        x, y, c = _position()
        for side in (0, 1):
            mine = [pltpu.make_async_remote_copy(src_ref=s_ref.at[i], dst_ref=p_ref.at[i], send_sem=send_sems.at[i],
                                                 recv_sem=recv_sems.at[i], device_id=(x, y, 1 - c), device_id_type=MESH)
                    for i in range(n) if (lo + i) % 2 == side]

            @pl.when(c != side)
            def _():
                for cp in mine:
                    cp.start() if sending else cp.wait_send()

            if not sending:
                @pl.when(c == side)
                def _():
                    for cp in mine:
                        cp.wait_recv()

    return functools.partial(run, True), functools.partial(run, False)


def _pair_swap(slabs, lo, name):
    n = slabs.shape[0]

    def body(s_ref, p_ref, *sems):
        start, wait = _pair_swap_ops(s_ref, p_ref, lo, sems)
        start()
        wait()

    return pl.pallas_call(
        body, name=name, out_shape=jax.ShapeDtypeStruct(slabs.shape, slabs.dtype),
        in_specs=_hbm_specs(1), out_specs=_hbm_specs(1)[0], scratch_shapes=_pair_swap_scratch(n))(slabs)


def _chip_sums(slabs, swapped, lo, name):
    n, rows, cols = slabs.shape
    tile = W_IN_COL_TILE

    def body(s_ref, p_ref, o_ref):
        c = lax.axis_index("c")
        for i in range(n):
            @pl.when(c == (lo + i) % 2)
            def _(i=i):
                o_ref[i] = (s_ref[i].astype(F32) + p_ref[i].astype(F32)).astype(BF16)

    blk = pl.BlockSpec((n, rows, tile), lambda j: (0, 0, j))
    return pl.pallas_call(
        body, name=name, grid=(cols // tile,), in_specs=[blk, blk], out_specs=blk,
        out_shape=jax.ShapeDtypeStruct(slabs.shape, BF16), compiler_params=_params("arbitrary"))(slabs, swapped)


def _exchange_scratch(n):
    return [pltpu.SemaphoreType.DMA((7 * n,)), pltpu.SemaphoreType.DMA((7 * n,)), pltpu.SemaphoreType.DMA((n,))]


def _exchange_ops(src_refs, dst_refs, owners, sems):
    send_sems, recv_sems, local_sems = sems
    n = len(src_refs)

    def guarded(a, dev, fn):
        lo, hi = owners[a][:2]
        if (lo, hi) == (0, N_DEV):
            fn()
        else:
            pl.when((dev >= lo) & (dev < hi))(fn)

    def src(a, dev):
        ref = src_refs[a]
        return ref.at[0] if ref.shape[0] == 1 else ref.at[dev - owners[a][0]]

    def run(sending, waiting):
        x, y, c = _position()
        me = 4 * x + 2 * y + c
        for a in range(n):
            by_chip = len(owners[a]) == 3
            slot = (lambda qx, qy, qc: 2 * qx + qy) if by_chip else (lambda qx, qy, qc: 4 * qx + 2 * qy + qc)
            mine = slot(x, y, c)
            local = lambda a=a, mine=mine: pltpu.make_async_copy(src(a, me), dst_refs[a].at[mine], local_sems.at[a])
            if sending:
                guarded(a, me, lambda local=local: local().start())
            for m in range(2, N_DEV, 2) if by_chip else range(1, N_DEV):
                px, py, pc = x ^ (m >> 2), y ^ ((m >> 1) & 1), c ^ (m & 1)
                peer = 4 * px + 2 * py + pc
                theirs = slot(px, py, pc)
                sem = dict(send_sem=send_sems.at[7 * a + m - 1], recv_sem=recv_sems.at[7 * a + m - 1],
                           device_id=(px, py, pc), device_id_type=MESH)
                send = lambda a=a, peer=peer, sem=sem, mine=mine: pltpu.make_async_remote_copy(
                    src_ref=src(a, peer), dst_ref=dst_refs[a].at[mine], **sem)
                recv = lambda a=a, sem=sem, theirs=theirs: pltpu.make_async_remote_copy(
                    src_ref=src(a, me), dst_ref=dst_refs[a].at[theirs], **sem)
                if sending:
                    guarded(a, peer, lambda send=send: send().start())
                if waiting:
                    guarded(a, me, lambda recv=recv: recv().wait_recv())
                    guarded(a, peer, lambda send=send: send().wait_send())
            if waiting:
                guarded(a, me, lambda local=local: local().wait())

    return functools.partial(run, True, False), functools.partial(run, False, True)


def _sum_slabs(r_ref):
    g = r_ref[0].astype(F32)
    for k in range(1, r_ref.shape[0]):
        g = g + r_ref[k].astype(F32)
    return g


def _adamw(g, w, m, v):
    m_new = ADAM_B1 * m + (1.0 - ADAM_B1) * g
    v_new = ADAM_B2 * v + (1.0 - ADAM_B2) * (g * g)
    m_hat = m_new / (1.0 - ADAM_B1 ** ADAM_STEP)
    v_hat = v_new / (1.0 - ADAM_B2 ** ADAM_STEP)
    return g, -ADAM_LR * (m_hat / (jnp.sqrt(v_hat) + ADAM_EPS) + ADAM_WD * w), m_new, v_new


def _adamw_w_in(recv_early, recv_late, w, m, v, slabs, owners):
    rows, cols = w.shape
    tile = W_IN_COL_TILE
    nx = len(slabs)

    def body(early_ref, late_ref, w_ref, m_ref, v_ref, *refs):
        src_refs, o_refs, dst_refs = refs[:nx], refs[nx:nx + 4], refs[nx + 4:2 * nx + 4]
        start, wait = _exchange_ops(src_refs, dst_refs, owners, refs[2 * nx + 4:])
        x, y, c = _position()
        early_owner = 4 * x + 2 * y + c >= EARLY_FROM

        @pl.when(pl.program_id(0) == 0)
        def _():
            start()

        def update(g):
            for o_ref, val in zip(o_refs, _adamw(g, w_ref[...], m_ref[...], v_ref[...])):
                o_ref[...] = val

        pl.when(early_owner)(lambda: update(_sum_slabs(early_ref)))
        pl.when(jnp.logical_not(early_owner))(lambda: update(_sum_slabs(late_ref)))

        @pl.when(pl.program_id(0) == cols // tile - 1)
        def _():
            wait()

    blk = pl.BlockSpec((rows, tile), lambda i: (0, i))
    slots = lambda r: pl.BlockSpec((r.shape[0], rows, tile), lambda i: (0, 0, i))
    out = pl.pallas_call(
        body, name="adamw_w_in", grid=(cols // tile,),
        in_specs=[slots(recv_early), slots(recv_late), blk, blk, blk] + _hbm_specs(nx),
        out_specs=[blk] * 4 + _hbm_specs(nx),
        out_shape=[jax.ShapeDtypeStruct((rows, cols), F32)] * 4 + _received_shapes(slabs, owners),
        scratch_shapes=_exchange_scratch(nx),
        compiler_params=_params("arbitrary"))(recv_early, recv_late, w, m, v, *slabs)
    return out[:4], out[4:]


def _adamw_misc(recvs, recv_small, recv_norm, params):
    names = list(params)
    flat = [a for n in names for a in params[n]]

    def body(woa_ref, wob_ref, wo_ref, lora_ref, small_ref, norm_ref, *refs):
        p_refs, o_refs = refs[:len(flat)], refs[len(flat):]
        g_small = _sum_slabs(small_ref)
        g_lora = _sum_slabs(lora_ref)
        grads = {"w_out_a": _sum_slabs(woa_ref), "w_out_b": _sum_slabs(wob_ref), "w_out": _sum_slabs(wo_ref),
                 "w_lora_up": g_lora[0], "a_lora_up": g_lora[1], "norm_g": _sum_slabs(norm_ref)}
        for n, (off, size) in SMALL_SLOTS.items():
            grads[n] = g_small[:, off:off + size]
        for i, n in enumerate(names):
            w_ref, m_ref, v_ref = p_refs[3 * i:3 * i + 3]
            for o_ref, val in zip(o_refs[4 * i:4 * i + 4], _adamw(grads[n], w_ref[...], m_ref[...], v_ref[...])):
                o_ref[...] = val
        o_refs[-1][...] = g_small[:, LOSS_SLOT:LOSS_SLOT + 1]

    out = pl.pallas_call(
        body, name="adamw_misc",
        out_shape=[jax.ShapeDtypeStruct(params[n][0].shape, F32) for n in names for _ in range(4)]
        + [jax.ShapeDtypeStruct((1, 1), F32)],
        compiler_params=_params())(*recvs, recv_small, recv_norm, *flat)
    return {n: out[4 * i:4 * i + 4] for i, n in enumerate(names)}, out[-1]


_WT_SEGMENTS = ((0, NA), (NA, NB), (NA + NB + H, NG), (NA + NB, H))


def _split_wt(gathered):
    tile = W_IN_COL_TILE

    def body(g_ref, *o_refs):
        full = jnp.concatenate([g_ref[j] for j in range(N_DEV)], axis=0)
        for o_ref, (row, n) in zip(o_refs, _WT_SEGMENTS):
            seg = full[row:row + n]
            if n < o_ref.shape[0]:
                seg = jnp.concatenate([seg, jnp.zeros((o_ref.shape[0] - n, tile), BF16)], axis=0)
            o_ref[...] = seg

    sizes = (NA, NB, NG, NF)
    return pl.pallas_call(
        body, name="split_wt", grid=(D // tile,),
        in_specs=[pl.BlockSpec((N_DEV, COLS_PER_DEV, tile), lambda i: (0, 0, i))],
        out_specs=[pl.BlockSpec((n, tile), lambda i: (0, i)) for n in sizes],
        out_shape=[jax.ShapeDtypeStruct((n, D), BF16) for n in sizes],
        compiler_params=_params("arbitrary"))(gathered)


def _by_cols(a):
    return jnp.moveaxis(a, 0, 1).reshape(a.shape[1], -1)


def _col_slabs(a):
    return jnp.moveaxis(a.reshape(a.shape[0], N_DEV, -1), 1, 0).astype(BF16)


def _pack_small(grads, loss):
    pieces, at = [], 0
    for n, (off, size) in list(SMALL_SLOTS.items()) + [("loss", (LOSS_SLOT, 1))]:
        pieces += [jnp.zeros((off - at,), F32), (loss if n == "loss" else grads[n]).reshape(-1)]
        at = off + size
    return jnp.concatenate(pieces + [jnp.zeros((SMALL_LEN - at,), F32)]).reshape(1, 1, SMALL_LEN)


def _gather_weights(t):
    cast = lambda a: a.astype(BF16)
    loras = jnp.stack([t["w_lora_up"][0], t["a_lora_up"][0]])
    wt, woa, wob, wo, lora = _all_gather(
        cast(t["w_in"][0].T), [cast(t["w_out_a"][0]), cast(t["w_out_b"][0]), cast(t["w_out"][0]), cast(loras)],
        "weight_gather")
    in_a, in_b, in_g, in_f = _split_wt(wt)
    return {"in_a": in_a, "in_b": in_b, "in_g": in_g, "in_f": in_f, "w_out_a": _by_cols(woa), "w_out_b": _by_cols(wob),
            "w_out": wo.reshape(D, D), "w_lora_up": lora[:, 0], "a_lora_up": lora[:, 1]}


def kernel(x, norm_g, w_in, shift_mu, w_lora_up, w0, a_lora_up, a0, k_k, k_a, r_k, lnx_w, lnx_b, f_bias, q_norm_g, k_norm_g, w_out_a, w_out_b, w_out, final_norm_g, loss_target, m_norm_g, m_w_in, m_shift_mu, m_w_lora_up, m_w0, m_a_lora_up, m_a0, m_k_k, m_k_a, m_r_k, m_lnx_w, m_lnx_b, m_f_bias, m_q_norm_g, m_k_norm_g, m_w_out_a, m_w_out_b, m_w_out, m_final_norm_g, v_norm_g, v_w_in, v_shift_mu, v_w_lora_up, v_w0, v_a_lora_up, v_a0, v_k_k, v_k_a, v_r_k, v_lnx_w, v_lnx_b, v_f_bias, v_q_norm_g, v_k_norm_g, v_w_out_a, v_w_out_b, v_w_out, v_final_norm_g):
    names = ("norm_g", "w_in", "shift_mu", "w_lora_up", "w0", "a_lora_up", "a0", "k_k", "k_a", "r_k", "lnx_w", "lnx_b",
             "f_bias", "q_norm_g", "k_norm_g", "w_out_a", "w_out_b", "w_out", "final_norm_g")
    weights = dict(zip(names, (norm_g, w_in, shift_mu, w_lora_up, w0, a_lora_up, a0, k_k, k_a, r_k, lnx_w, lnx_b,
                               f_bias, q_norm_g, k_norm_g, w_out_a, w_out_b, w_out, final_norm_g)))
    m_in = dict(zip(names, (m_norm_g, m_w_in, m_shift_mu, m_w_lora_up, m_w0, m_a_lora_up, m_a0, m_k_k, m_k_a, m_r_k,
                            m_lnx_w, m_lnx_b, m_f_bias, m_q_norm_g, m_k_norm_g, m_w_out_a, m_w_out_b, m_w_out,
                            m_final_norm_g)))
    v_in = dict(zip(names, (v_norm_g, v_w_in, v_shift_mu, v_w_lora_up, v_w0, v_a_lora_up, v_a0, v_k_k, v_k_a, v_r_k,
                            v_lnx_w, v_lnx_b, v_f_bias, v_q_norm_g, v_k_norm_g, v_w_out_a, v_w_out_b, v_w_out,
                            v_final_norm_g)))

    matrices = ("w_out_a", "w_out_b", "w_out", "w_lora_up", "a_lora_up")
    as_2d = lambda n, a: a[0] if n in matrices else a.reshape(1, -1)

    full = _gather_weights(weights)
    dx, dng, recv_wt, recvs, recv_small = _local_step(
        x[0], loss_target[0], full, {n: as_2d(n, weights[n]) for n in ("norm_g",) + tuple(SMALL_SLOTS)})

    res, (recv_norm,) = _adamw_w_in(*recv_wt, w_in[0].T, m_w_in[0].T, v_w_in[0].T, (dng[None],), ((0, N_DEV),))
    outs = {"w_in": [r.T[None] for r in res]}
    misc = [n for n in names if n != "w_in"]
    res, loss_sum = _adamw_misc(recvs, recv_small, recv_norm,
                                {n: tuple(as_2d(n, t[n]) for t in (weights, m_in, v_in)) for n in misc})
    for n in misc:
        outs[n] = [r.reshape(weights[n].shape) for r in res[n]]
    return (loss_sum.reshape(()), dx[None], *[outs[n][i] for i in range(4) for n in names])
```

```python
import functools
import math

import jax
import jax.numpy as jnp
from jax import lax
from jax.experimental import pallas as pl
from jax.experimental.pallas import tpu as pltpu

F32 = jnp.float32
BF16 = jnp.bfloat16
HI = lax.Precision.HIGHEST
MESH = pl.DeviceIdType.MESH

N_DEV = 8
D = 1024
H = 8
N = 64
DA = H * N
RANK = 64
NA = 4 * DA + 2 * RANK
NB = 4 * DA
NG = 2 * D
NF = 128
IN_COLS = NA + NB + H + NG
COLS_PER_DEV = IN_COLS // N_DEV
RMS_EPS = 1e-6
LNX_EPS = 64e-5
ATT_SCALE = N ** -0.5

ADAM_LR = 0.001
ADAM_B1 = 0.9
ADAM_B2 = 0.999
ADAM_EPS = 1e-08
ADAM_WD = 0.01
ADAM_STEP = 10

LANES = 128
WKV_CHUNK = 64
WKV_STEP_CHUNKS = 4
TOK_TILE = 256
HEAD_TILE = 256
XGRAD_TILE = 128
WGRAD_TILE = 512
ATT_TILE = 256
ATT_GROUPS = 8
VMEM_LIMIT = 56 * 1024 * 1024


def _lane_tile_slots(sizes):
    slots, at = {}, 0
    for name, size in sizes:
        slots[name] = (at, size)
        at += -(-size // LANES) * LANES
    return slots, at


SMALL_SLOTS, LOSS_SLOT = _lane_tile_slots((
    ("final_norm_g", D), ("shift_mu", NA), ("w0", DA), ("a0", DA), ("k_k", DA), ("k_a", DA), ("r_k", DA), ("lnx_w", DA),
    ("lnx_b", DA), ("q_norm_g", N), ("k_norm_g", N), ("f_bias", H)))
SMALL_LEN = LOSS_SLOT + LANES
W_IN_COL_TILE = 512
EARLY_FROM = -(-NA // COLS_PER_DEV)


def _params(*sem):
    return pltpu.CompilerParams(dimension_semantics=sem or None, vmem_limit_bytes=VMEM_LIMIT)


def _bdot(a, b):
    return jnp.dot(a.astype(BF16), b.astype(BF16), preferred_element_type=F32)


def _bdot_nt(a, b):
    return lax.dot_general(a.astype(BF16), b.astype(BF16), (((1,), (1,)), ((), ())), preferred_element_type=F32)


def _bdot_tn(a, b):
    return lax.dot_general(a.astype(BF16), b.astype(BF16), (((0,), (0,)), ((), ())), preferred_element_type=F32)


def _sigmoid(x):
    return 1.0 / (1.0 + jnp.exp(-x))


def _softplus(x):
    return jnp.maximum(x, 0.0) + jnp.log(1.0 + jnp.exp(-jnp.abs(x)))


def _heads(ref, col0):
    return jnp.stack([ref[:, col0 + N * h:col0 + N * (h + 1)] for h in range(H)])


def _lerp(c, s, mu):
    return c + (s - c) * mu


def _head_sums(x):
    low = lax.broadcasted_iota(jnp.int32, (x.shape[0], LANES), 1) < N
    out = []
    for p in range(x.shape[1] // LANES):
        pair = x[:, LANES * p:LANES * (p + 1)]
        first = jnp.sum(jnp.where(low, pair, 0.0), axis=-1, keepdims=True)
        second = jnp.sum(jnp.where(low, 0.0, pair), axis=-1, keepdims=True)
        out.append(jnp.where(low, first, second))
    return jnp.concatenate(out, axis=-1)


def _to_heads(x):
    return [x[:, N * h:N * (h + 1)] for h in range(H)]


def _from_heads(ref):
    return jnp.concatenate([ref[h] for h in range(H)], axis=-1)


def _rwkv_pre(rc, rs, kc, ks, vc, vs, gc, gs, wdc, wds, adc, ads,
              mu_r, mu_k, mu_v, mu_g, mu_wd, mu_ad, w_up, w0, a_up, a0, k_k, k_a):
    r = _lerp(rc, rs, mu_r)
    k = _lerp(kc, ks, mu_k)
    v = _lerp(vc, vs, mu_v)
    g = _lerp(gc, gs, mu_g)
    wd = _lerp(wdc, wds, mu_wd)
    ad = _lerp(adc, ads, mu_ad)
    t = wd.shape[0]
    w_raw = -_softplus(-(w0 + _bdot(jnp.tanh(wd), w_up))) - 0.5
    lw = -jnp.exp(w_raw)
    row = lax.broadcasted_iota(jnp.int32, (t, t), 0)
    col = lax.broadcasted_iota(jnp.int32, (t, t), 1)
    same_chunk = ((row >= col) & (row // WKV_CHUNK == col // WKV_CHUNK)).astype(F32)
    cl = jnp.dot(same_chunk, lw, precision=HI, preferred_element_type=F32)
    alr = _sigmoid(a0 + _bdot(ad, a_up))
    kk = k * k_k
    kk = kk / jnp.maximum(jnp.sqrt(_head_sums(kk * kk)), 1e-12)
    k2 = k * (1.0 + (alr - 1.0) * k_a)
    return r, lw, cl, k2, v, -kk, kk * alr, g


_MM_DIMS = {"nn": (((2,), (1,)), ((0,), (0,))), "nt": (((2,), (2,)), ((0,), (0,))), "tn": (((1,), (1,)), ((0,), (0,)))}


def _dot1(a, b, kind):
    return lax.dot_general(a.astype(BF16), b.astype(BF16), dimension_numbers=_MM_DIMS[kind], preferred_element_type=F32)


@functools.partial(jax.custom_vjp, nondiff_argnums=(2,))
def _mm(a, b, kind):
    return _dot1(a, b, kind)


def _mm_fwd(a, b, kind):
    return _dot1(a, b, kind), (a, b)


def _mm_bwd(kind, res, ct):
    a, b = res
    if kind == "nn":
        return _dot1(ct, b, "nt"), _dot1(a, ct, "tn")
    if kind == "nt":
        return _dot1(ct, b, "nn"), _dot1(ct, a, "tn")
    return _dot1(b, ct, "nt"), _dot1(a, ct, "nn")


_mm.defvjp(_mm_fwd, _mm_bwd)


def _chunk_masks(c):
    row = lax.broadcasted_iota(jnp.int32, (c, c), 0)
    col = lax.broadcasted_iota(jnp.int32, (c, c), 1)
    return (row >= col)[None], (row > col)[None], (row == col).astype(F32)[None]


def _wkv_aab(lw, cl, a, b):
    _, strict, _ = _chunk_masks(a.shape[1])
    return jnp.where(strict, _mm(a * jnp.exp(cl - lw), b * jnp.exp(-cl), "nt"), 0.0)


def _tri_inverse(x):
    c = x.shape[1]
    p = _chunk_masks(c)[2] + x
    for _ in range(int(math.log2(c)) - 1):
        x = _dot1(x, x, "nn")
        p = p + _dot1(p, x, "nn")
    return p


def _wkv_apply(s0, r, lw, cl, k, v, a, b, p):
    c = r.shape[1]
    incl, strict, _ = _chunk_masks(c)
    gi = jnp.exp(-cl)
    left = jnp.concatenate([a * jnp.exp(cl - lw), r * jnp.exp(cl)], axis=1)
    right = jnp.concatenate([b * gi, k * gi], axis=1)
    m = _mm(left, right, "nt")
    z0 = _mm(left, s0, "nt")
    a_ak = jnp.where(strict, m[:, :c, c:], 0.0)
    row = lax.broadcasted_iota(jnp.int32, (c, 2 * c), 0)
    col = lax.broadcasted_iota(jnp.int32, (c, 2 * c), 1)
    a_r = jnp.where((row >= col % c)[None], m[:, c:, :], 0.0)
    sa = _mm(p, z0[:, :c] + _mm(a_ak, v, "nn"), "nn")
    sa_v = jnp.concatenate([sa, v], axis=1)
    y = z0[:, c:] + _mm(a_r, sa_v, "nn")
    s1 = (s0 + _mm(sa_v, right, "tn")) * jnp.exp(cl[:, c - 1:c, :])
    return y, s1


def _rwkv_post(y, r, k2, v, g, lnx_w, lnx_b, r_k):
    yc = y - _head_sums(y) * (1.0 / N)
    var = _head_sums(yc * yc) * (1.0 / N)
    yn = yc * lax.rsqrt(var + LNX_EPS) * lnx_w + lnx_b
    bonus = _head_sums(r * k2 * r_k) * v
    return (yn + bonus) * (g * _sigmoid(g))


def _fox_pre(q, k, f, q_g, k_g, f_b):
    qn = q * lax.rsqrt(_head_sums(q * q) * (1.0 / N) + RMS_EPS) * q_g
    kn = k * lax.rsqrt(_head_sums(k * k) * (1.0 / N) + RMS_EPS) * k_g
    x = f + f_b
    return qn, kn, jnp.minimum(x, 0.0) - jnp.log(1.0 + jnp.exp(-jnp.abs(x)))


def _norm_proj(x, g, wts):
    s = x.shape[0]
    k = len(wts)

    def body(x_ref, g_ref, *refs):
        w_refs, h_ref, o_refs = refs[:k], refs[k], refs[k + 1:]
        xv = x_ref[...]
        h = (xv * lax.rsqrt(jnp.mean(xv * xv, axis=-1, keepdims=True) + RMS_EPS) * g_ref[...]).astype(BF16)
        h_ref[...] = h
        for w_ref, o_ref in zip(w_refs, o_refs):
            o_ref[...] = _bdot_nt(h, w_ref[...])

    tok = lambda n: pl.BlockSpec((TOK_TILE, n), lambda i: (i, 0))
    out = pl.pallas_call(
        body, name="norm_proj", grid=(s // TOK_TILE,),
        in_specs=[tok(D), pl.BlockSpec((1, D), lambda i: (0, 0))] + [pl.BlockSpec(w.shape, lambda i: (0, 0)) for w in wts],
        out_specs=[tok(D)] + [tok(w.shape[0]) for w in wts],
        out_shape=[jax.ShapeDtypeStruct((s, D), BF16)] + [jax.ShapeDtypeStruct((s, w.shape[0]), F32) for w in wts],
        compiler_params=_params("arbitrary"))(x, g, *wts)
    return out[0], out[1:]


def _proj_wgrad_early(h, dub, dug, duf, head_rows):
    s = dub.shape[0]
    steps = s // WGRAD_TILE
    seg_rows = (_WT_SEGMENTS[1], _WT_SEGMENTS[2], _WT_SEGMENTS[3])

    def body(h_ref, b_ref, g_ref, f_ref, o_ref, head_ref, *accs):
        @pl.when(pl.program_id(0) == 0)
        def _():
            for acc in accs:
                acc[...] = jnp.zeros_like(acc)

        h = h_ref[...]
        for acc, du_ref in zip(accs, (b_ref, g_ref, f_ref)):
            acc[...] += _bdot_tn(du_ref[...], h)

        @pl.when(pl.program_id(0) == steps - 1)
        def _():
            head_ref[...] = accs[0][:head_rows, :]
            for j in range(EARLY_FROM, N_DEV):
                lo, hi = COLS_PER_DEV * j, COLS_PER_DEV * (j + 1)
                parts = []
                for acc, (row, n) in sorted(zip(accs, seg_rows), key=lambda t: t[1][0]):
                    first, last = max(lo, row), min(hi, row + n)
                    if first < last:
                        parts.append(acc[first - row:last - row, :])
                o_ref[j - EARLY_FROM] = (parts[0] if len(parts) == 1 else jnp.concatenate(parts, axis=0)).astype(BF16)

    tok = lambda n: pl.BlockSpec((WGRAD_TILE, n), lambda i: (i, 0))
    n_early = N_DEV - EARLY_FROM
    return pl.pallas_call(
        body, name="wgrad_bgf", grid=(steps,), in_specs=[tok(D), tok(NB), tok(NG), tok(NF)],
        out_specs=[pl.BlockSpec((n_early, COLS_PER_DEV, D), lambda i: (0, 0, 0)),
                   pl.BlockSpec((head_rows, D), lambda i: (0, 0))],
        out_shape=[jax.ShapeDtypeStruct((n_early, COLS_PER_DEV, D), BF16), jax.ShapeDtypeStruct((head_rows, D), F32)],
        scratch_shapes=[pltpu.VMEM((n, D), F32) for n in (NB, NG, NF)],
        compiler_params=_params("arbitrary"))(h, dub, dug, duf)


def _proj_wgrad_late(h, dua, dwt_b_head):
    s = dua.shape[0]
    steps = s // WGRAD_TILE

    def body(h_ref, du_ref, b_ref, o_ref, acc):
        @pl.when(pl.program_id(0) == 0)
        def _():
            acc[...] = jnp.zeros_like(acc)

        acc[...] += _bdot_tn(du_ref[...], h_ref[...])

        @pl.when(pl.program_id(0) == steps - 1)
        def _():
            for j in range(EARLY_FROM):
                lo, hi = COLS_PER_DEV * j, COLS_PER_DEV * (j + 1)
                parts = [acc[lo:min(hi, NA), :]] + ([b_ref[:hi - NA, :]] if hi > NA else [])
                o_ref[j] = (parts[0] if len(parts) == 1 else jnp.concatenate(parts, axis=0)).astype(BF16)

    return pl.pallas_call(
        body, name="wgrad_a", grid=(steps,),
        in_specs=[pl.BlockSpec((WGRAD_TILE, D), lambda i: (i, 0)), pl.BlockSpec((WGRAD_TILE, NA), lambda i: (i, 0)),
                  pl.BlockSpec(dwt_b_head.shape, lambda i: (0, 0))],
        out_specs=pl.BlockSpec((EARLY_FROM, COLS_PER_DEV, D), lambda i: (0, 0, 0)),
        out_shape=jax.ShapeDtypeStruct((EARLY_FROM, COLS_PER_DEV, D), BF16),
        scratch_shapes=[pltpu.VMEM((NA, D), F32)], compiler_params=_params("arbitrary"))(h, dua, dwt_b_head)


def _proj_xgrad(x, g, dx2, dus, ws, slabs, owners):
    s = x.shape[0]
    tile = XGRAD_TILE
    k = len(dus)
    nx = len(slabs)
    n_in = 3 + 2 * k + nx

    def body(*refs):
        x_ref, g_ref, dx2_ref = refs[:3]
        du_refs, w_refs = refs[3:3 + k], refs[3 + k:3 + 2 * k]
        src_refs = refs[3 + 2 * k:3 + 2 * k + nx]
        dx_ref, dg_ref = refs[n_in:n_in + 2]
        dst_refs = refs[n_in + 2:n_in + 2 + nx]
        start, wait = _exchange_ops(src_refs, dst_refs, owners, refs[n_in + 2 + nx:])

        @pl.when(pl.program_id(0) == 0)
        def _():
            dg_ref[...] = jnp.zeros_like(dg_ref)
            start()

        dh = _bdot(du_refs[0][...], w_refs[0][...])
        for du_ref, w_ref in zip(du_refs[1:], w_refs[1:]):
            dh += _bdot(du_ref[...], w_ref[...])
        xv = x_ref[...]
        rs = lax.rsqrt(jnp.mean(xv * xv, axis=-1, keepdims=True) + RMS_EPS)
        xn = xv * rs
        dg_ref[...] += jnp.sum(dh * xn, axis=0, keepdims=True)
        dxn = dh * g_ref[...]
        dx_ref[...] = rs * (dxn - xn * jnp.mean(dxn * xn, axis=-1, keepdims=True)) + dx2_ref[...]

        @pl.when(pl.program_id(0) == s // tile - 1)
        def _():
            wait()

    tok = lambda n: pl.BlockSpec((tile, n), lambda i: (i, 0))
    fixed = lambda a: pl.BlockSpec(a.shape, lambda i: (0,) * a.ndim)
    out = pl.pallas_call(
        body, name="proj_xgrad", grid=(s // tile,),
        in_specs=([tok(D), fixed(g), tok(D)] + [tok(du.shape[1]) for du in dus] + [fixed(w) for w in ws]
                  + _hbm_specs(nx)),
        out_specs=[tok(D), pl.BlockSpec((1, D), lambda i: (0, 0))] + _hbm_specs(nx),
        out_shape=[jax.ShapeDtypeStruct((s, D), F32), jax.ShapeDtypeStruct((1, D), F32)] + _received_shapes(slabs, owners),
        scratch_shapes=_exchange_scratch(nx),
        compiler_params=_params("arbitrary"))(x, g, dx2, *dus, *ws, *slabs)
    return out[0], out[1], out[2:]


def _tail(x, target, ya, o, ub, ug, w_oa, w_ob, w_o, fg):
    s = x.shape[0]
    tile = TOK_TILE

    def body(x_ref, t_ref, ya_ref, o_ref, gb_ref, ug_ref, woa_ref, wob_ref, wo_ref, fg_ref,
             loss_ref, dfg_ref, dwo_ref, dwoa_ref, dwob_ref, dx2_ref, dya_ref, do_ref, dgb_ref, dug_ref):
        @pl.when(pl.program_id(0) == 0)
        def _():
            for r in (loss_ref, dfg_ref, dwo_ref, dwoa_ref, dwob_ref):
                r[...] = jnp.zeros_like(r)

        ya_v = ya_ref[...]
        gate_b = gb_ref[...]
        sg_b = _sigmoid(gate_b)
        silu_b = gate_b * sg_b
        o_v = jnp.concatenate([o_ref[h] for h in range(H)], axis=-1)
        yb_v = o_v * silu_b
        big_a = _bdot(ya_v, woa_ref[...])
        big_b = _bdot(yb_v, wob_ref[...])
        sa = _sigmoid(ug_ref[:, :D])
        sb = _sigmoid(ug_ref[:, D:])
        merged = sa * big_a + sb * big_b
        x2 = x_ref[...] + _bdot(merged, wo_ref[...])
        rs = lax.rsqrt(jnp.mean(x2 * x2, axis=-1, keepdims=True) + RMS_EPS)
        xn = x2 * rs
        err = xn * fg_ref[...] - t_ref[...]
        loss_ref[...] += (0.5 / D) * jnp.sum(err * err)
        dout = err * (1.0 / D)
        dfg_ref[...] += jnp.sum(dout * xn, axis=0, keepdims=True)
        dxn = dout * fg_ref[...]
        dx2 = rs * (dxn - xn * jnp.mean(dxn * xn, axis=-1, keepdims=True))
        dx2_ref[...] = dx2
        dwo_ref[...] += _bdot_tn(merged, dx2)
        dmerged = _bdot_nt(dx2, wo_ref[...])
        dbig_a = dmerged * sa
        dbig_b = dmerged * sb
        dug_ref[:, :D] = (dmerged * big_a * sa * (1.0 - sa)).astype(BF16)
        dug_ref[:, D:] = (dmerged * big_b * sb * (1.0 - sb)).astype(BF16)
        dwoa_ref[...] += _bdot_tn(ya_v, dbig_a)
        dwob_ref[...] += _bdot_tn(yb_v, dbig_b)
        dya_ref[...] = _bdot_nt(dbig_a, woa_ref[...])
        dyb = _bdot_nt(dbig_b, wob_ref[...])
        dgb_ref[...] = dyb * o_v * (sg_b * (1.0 + gate_b * (1.0 - sg_b)))
        _dov = dyb * silu_b
        for h in range(H):
            do_ref[h] = _dov[:, N * h:N * (h + 1)]

    tok = lambda n: pl.BlockSpec((tile, n), lambda i: (i, 0))
    hm = pl.BlockSpec((H, tile, N), lambda i: (0, i, 0))
    fixed = lambda shape: pl.BlockSpec(shape, lambda i: (0,) * len(shape))
    f32 = lambda *shape: jax.ShapeDtypeStruct(shape, F32)
    return pl.pallas_call(
        body, name="tail", grid=(s // tile,),
        in_specs=[tok(D), tok(D), tok(DA), hm, pl.BlockSpec((tile, DA), lambda i: (i, 3)), tok(NG),
                  fixed((DA, D)), fixed((DA, D)), fixed((D, D)), fixed((1, D))],
        out_specs=[fixed((1, 1)), fixed((1, D)), fixed((D, D)), fixed((DA, D)), fixed((DA, D)),
                   tok(D), tok(DA), hm, tok(DA), tok(NG)],
        out_shape=[f32(1, 1), f32(1, D), f32(D, D), f32(DA, D), f32(DA, D),
                   f32(s, D), f32(s, DA), f32(H, s, N), f32(s, DA), jax.ShapeDtypeStruct((s, NG), BF16)],
        compiler_params=_params("arbitrary"))(x, target, ya, o, ub, ug, w_oa, w_ob, w_o, fg)


def _pre_operands(ua_ref, prev_ref, first):
    cur = ua_ref[...]
    t = cur.shape[0]
    prev_row = jnp.where(first, 0.0, prev_ref[7:8, :])
    rows = lax.broadcasted_iota(jnp.int32, cur.shape, 0)
    sh = jnp.where(rows == 0, prev_row, pltpu.roll(cur, 1, axis=0))
    ops = []
    for c0, n in ((0, DA), (DA, DA), (2 * DA, DA), (3 * DA + 2 * RANK, DA), (3 * DA, RANK), (3 * DA + RANK, RANK)):
        ops += [cur[:, c0:c0 + n], sh[:, c0:c0 + n]]
    del t
    return ops


def _ua_specs(tile, order):
    blocks = tile // 8
    return [pl.BlockSpec((tile, NA), lambda i: (order(i), 0)),
            pl.BlockSpec((8, NA), lambda i: (jnp.maximum(order(i) * blocks - 1, 0), 0))]


def _rwkv_pre_fwd(ua, pre_params):
    s = ua.shape[0]
    tile = HEAD_TILE

    def body(ua_ref, prev_ref, *refs):
        p_refs, o_refs = refs[:len(pre_params)], refs[len(pre_params):]
        ops = _pre_operands(ua_ref, prev_ref, pl.program_id(0) == 0)
        outs = _rwkv_pre(*ops, *[p[...] for p in p_refs])
        for o_ref, val in zip(o_refs, outs):
            o_ref[...] = val

    tm = pl.BlockSpec((tile, DA), lambda i: (i, 0))
    return pl.pallas_call(
        body, name="rwkv_pre_fwd", grid=(s // tile,),
        in_specs=_ua_specs(tile, lambda i: i) + [pl.BlockSpec(p.shape, lambda i, nd=p.ndim: (0,) * nd) for p in pre_params],
        out_specs=[tm] * 8, out_shape=[jax.ShapeDtypeStruct((s, DA), F32)] * 8,
        compiler_params=_params("arbitrary"))(ua, ua, *pre_params)


def _rwkv_pre_bwd(ua, pre_params, cots):
    s = ua.shape[0]
    tile = HEAD_TILE
    nt = s // tile
    n_p = len(pre_params)

    def body(ua_ref, prev_ref, *refs):
        p_refs, c_refs = refs[:n_p], refs[n_p:n_p + 11]
        dua_ref = refs[n_p + 11]
        dp_refs = refs[n_p + 12:n_p + 12 + n_p]
        carry_ref = refs[-1]
        i = pl.program_id(0)

        @pl.when(i == 0)
        def _():
            carry_ref[...] = jnp.zeros_like(carry_ref)
            for r in dp_refs:
                r[...] = jnp.zeros_like(r)

        ops = _pre_operands(ua_ref, prev_ref, i == nt - 1)
        _, vjp = jax.vjp(_rwkv_pre, *ops, *[p[...] for p in p_refs])
        c = [r[...] for r in c_refs]
        grads = vjp((c[0] + c[1], c[2], c[3], c[4] + c[5], c[6] + c[7], c[8], c[9], c[10]))
        d_ops, d_par = grads[:12], grads[12:]
        for r, val in zip(dp_refs, d_par):
            r[...] += val
        d_cur = jnp.concatenate([d_ops[0], d_ops[2], d_ops[4], d_ops[8], d_ops[10], d_ops[6]], axis=-1)
        d_sh = jnp.concatenate([d_ops[1], d_ops[3], d_ops[5], d_ops[9], d_ops[11], d_ops[7]], axis=-1)
        rows = lax.broadcasted_iota(jnp.int32, d_sh.shape, 0)
        dua = d_cur + jnp.where(rows == tile - 1, carry_ref[...], pltpu.roll(d_sh, tile - 1, axis=0))
        dua_ref[...] = dua.astype(BF16)
        carry_ref[...] = d_sh[0:1, :]

    rev = lambda i: nt - 1 - i
    tm = pl.BlockSpec((tile, DA), lambda i: (rev(i), 0))
    fixed = [pl.BlockSpec(p.shape, lambda i, nd=p.ndim: (0,) * nd) for p in pre_params]
    return pl.pallas_call(
        body, name="rwkv_pre_bwd", grid=(nt,),
        in_specs=_ua_specs(tile, rev) + fixed + [tm] * 11,
        out_specs=[pl.BlockSpec((tile, NA), lambda i: (rev(i), 0))] + fixed,
        out_shape=[jax.ShapeDtypeStruct((s, NA), BF16)] + [jax.ShapeDtypeStruct(p.shape, F32) for p in pre_params],
        scratch_shapes=[pltpu.VMEM((1, NA), F32)],
        compiler_params=_params("arbitrary"))(ua, ua, *pre_params, *cots)


def _wkv_fwd(seq):
    s = seq[0].shape[0]
    nc = s // WKV_CHUNK

    def body(r_ref, lw_ref, cl_ref, k_ref, v_ref, a_ref, b_ref, y_ref, ck_ref, p_ref, state):
        @pl.when(pl.program_id(0) == 0)
        def _():
            state[...] = jnp.zeros_like(state)

        s0 = state[...]
        for i in range(WKV_STEP_CHUNKS):
            rows = slice(i * WKV_CHUNK, (i + 1) * WKV_CHUNK)
            r, lw, cl, k, v, a, b = (jnp.stack(_to_heads(ref[rows, :])) for ref in (r_ref, lw_ref, cl_ref, k_ref, v_ref,
                                                                                     a_ref, b_ref))
            ck_ref[i] = s0
            p = _tri_inverse(_wkv_aab(lw, cl, a, b))
            p_ref[i] = p
            y, s0 = _wkv_apply(s0, r, lw, cl, k, v, a, b, p)
            y_ref[rows, :] = jnp.concatenate([y[h] for h in range(H)], axis=-1)
        state[...] = s0

    tm = pl.BlockSpec((WKV_STEP_CHUNKS * WKV_CHUNK, DA), lambda c: (c, 0))
    per_chunk = lambda m: pl.BlockSpec((WKV_STEP_CHUNKS, H, m, m), lambda c: (c, 0, 0, 0))
    return pl.pallas_call(
        body, name="wkv_fwd", grid=(nc // WKV_STEP_CHUNKS,), in_specs=[tm] * 7,
        out_specs=[tm, per_chunk(N), per_chunk(WKV_CHUNK)],
        out_shape=[jax.ShapeDtypeStruct((s, DA), F32), jax.ShapeDtypeStruct((nc, H, N, N), F32),
                   jax.ShapeDtypeStruct((nc, H, WKV_CHUNK, WKV_CHUNK), F32)],
        scratch_shapes=[pltpu.VMEM((H, N, N), F32)], compiler_params=_params("arbitrary"))(*seq)


def _wkv_bwd(seq, ckpt, pinv, dy, slabs, owners):
    s = seq[0].shape[0]
    nc = s // WKV_CHUNK
    nx = len(slabs)

    def body(r_ref, lw_ref, cl_ref, k_ref, v_ref, a_ref, b_ref, ck_ref, p_ref, dy_ref, *refs):
        src_refs, d_refs, dst_refs = refs[:nx], refs[nx:nx + 7], refs[nx + 7:2 * nx + 7]
        dstate = refs[2 * nx + 7]
        start, wait = _exchange_ops(src_refs, dst_refs, owners, refs[2 * nx + 8:])

        @pl.when(pl.program_id(0) == 0)
        def _():
            dstate[...] = jnp.zeros_like(dstate)
            start()

        ds = dstate[...]
        for i in reversed(range(WKV_STEP_CHUNKS)):
            rows = slice(i * WKV_CHUNK, (i + 1) * WKV_CHUNK)
            p = p_ref[i]
            r, lw, cl, k, v, a, b, dy = (jnp.stack(_to_heads(ref[rows, :])) for ref in (r_ref, lw_ref, cl_ref, k_ref,
                                                                                         v_ref, a_ref, b_ref, dy_ref))
            _, vjp = jax.vjp(_wkv_apply, ck_ref[i], r, lw, cl, k, v, a, b, p)
            ds, dr, dlw, dcl, dk, dv, da, db, dp = vjp((dy, ds))
            _, vjp_x = jax.vjp(_wkv_aab, lw, cl, a, b)
            dlw2, dcl2, da2, db2 = vjp_x(_dot1(_dot1(p, dp, "tn"), p, "nt"))
            for d_ref, val in zip(d_refs, (dr, dlw + dlw2, dcl + dcl2, dk, dv, da + da2, db + db2)):
                d_ref[rows, :] = jnp.concatenate([val[h] for h in range(H)], axis=-1)
        dstate[...] = ds

        @pl.when(pl.program_id(0) == steps - 1)
        def _():
            wait()

    steps = nc // WKV_STEP_CHUNKS
    tm = pl.BlockSpec((WKV_STEP_CHUNKS * WKV_CHUNK, DA), lambda c: (steps - 1 - c, 0))
    per_chunk = lambda m: pl.BlockSpec((WKV_STEP_CHUNKS, H, m, m), lambda c: (steps - 1 - c, 0, 0, 0))
    out = pl.pallas_call(
        body, name="wkv_bwd", grid=(steps,),
        in_specs=[tm] * 7 + [per_chunk(N), per_chunk(WKV_CHUNK), tm] + _hbm_specs(nx),
        out_specs=[tm] * 7 + _hbm_specs(nx),
        out_shape=[jax.ShapeDtypeStruct((s, DA), F32)] * 7 + _received_shapes(slabs, owners),
        scratch_shapes=[pltpu.VMEM((H, N, N), F32)] + _exchange_scratch(nx),
        compiler_params=_params("arbitrary"))(*seq, ckpt, pinv, dy, *slabs)
    return out[:7], out[7:]


def _rwkv_post_fwd(y, r, k2, v, g, post_params):
    s = y.shape[0]
    tile = TOK_TILE

    def body(*refs):
        refs[-1][...] = _rwkv_post(*[ref[...] for ref in refs[:-1]])

    tm = pl.BlockSpec((tile, DA), lambda i: (i, 0))
    par = pl.BlockSpec((1, DA), lambda i: (0, 0))
    return pl.pallas_call(
        body, name="rwkv_post_fwd", grid=(s // tile,), in_specs=[tm] * 5 + [par] * 3,
        out_specs=tm, out_shape=jax.ShapeDtypeStruct((s, DA), F32),
        compiler_params=_params("arbitrary"))(y, r, k2, v, g, *post_params)


def _rwkv_post_bwd(y, r, k2, v, g, post_params, dya, slabs, lo):
    s = y.shape[0]
    tile = HEAD_TILE

    def body(y_ref, r_ref, k_ref, v_ref, g_ref, w_ref, b_ref, rk_ref, dya_ref, s_ref, *refs):
        d_refs, p_ref = refs[:8], refs[8]
        start, wait = _pair_swap_ops(s_ref, p_ref, lo, refs[9:])

        @pl.when(pl.program_id(0) == 0)
        def _():
            for ref in d_refs[5:]:
                ref[...] = jnp.zeros_like(ref)
            start()

        _, vjp = jax.vjp(_rwkv_post, *[ref[...] for ref in (y_ref, r_ref, k_ref, v_ref, g_ref, w_ref, b_ref, rk_ref)])
        grads = vjp(dya_ref[...])
        for ref, val in zip(d_refs[:5], grads[:5]):
            ref[...] = val
        for ref, val in zip(d_refs[5:], grads[5:]):
            ref[...] += val

        @pl.when(pl.program_id(0) == s // tile - 1)
        def _():
            wait()

    tm = pl.BlockSpec((tile, DA), lambda i: (i, 0))
    par = pl.BlockSpec((1, DA), lambda i: (0, 0))
    return pl.pallas_call(
        body, name="rwkv_post_bwd", grid=(s // tile,),
        in_specs=[tm] * 5 + [par] * 3 + [tm] + _hbm_specs(1),
        out_specs=[tm] * 5 + [par] * 3 + _hbm_specs(1),
        out_shape=[jax.ShapeDtypeStruct((s, DA), F32)] * 5 + [jax.ShapeDtypeStruct((1, DA), F32)] * 3
        + [jax.ShapeDtypeStruct(slabs.shape, slabs.dtype)],
        scratch_shapes=_pair_swap_scratch(slabs.shape[0]),
        compiler_params=_params("arbitrary"))(y, r, k2, v, g, *post_params, dya, slabs)


def _tri(t):
    return (lax.broadcasted_iota(jnp.int32, (t, t), 0) >= lax.broadcasted_iota(jnp.int32, (t, t), 1)).astype(F32)


def _fox_pre_fwd(ub, uf, q_g, k_g, f_b):
    s = ub.shape[0]
    tile = HEAD_TILE

    def body(ub_ref, uf_ref, qg_ref, kg_ref, fb_ref, q_ref, k_ref, v_ref, cum_ref, carry):
        @pl.when(pl.program_id(0) == 0)
        def _():
            carry[...] = jnp.zeros_like(carry)

        qn, kn, logf = _fox_pre(ub_ref[:, :DA], ub_ref[:, DA:2 * DA], uf_ref[...], qg_ref[...], kg_ref[...],
                                fb_ref[...])
        for h, (q_col, k_col) in enumerate(zip(_to_heads(qn), _to_heads(kn))):
            q_ref[h] = q_col
            k_ref[h] = k_col
        v_ref[...] = _heads(ub_ref, 2 * DA)
        cum = jnp.dot(_tri(tile), logf, precision=HI, preferred_element_type=F32) + carry[...]
        cum_ref[...] = cum
        carry[...] = cum[tile - 1:tile, :]

    hm = pl.BlockSpec((H, tile, N), lambda i: (0, i, 0))
    fixed = lambda shape: pl.BlockSpec(shape, lambda i: (0,) * len(shape))
    return pl.pallas_call(
        body, name="fox_pre_fwd", grid=(s // tile,),
        in_specs=[pl.BlockSpec((tile, NB), lambda i: (i, 0)), pl.BlockSpec((tile, NF), lambda i: (i, 0)),
                  fixed((1, DA)), fixed((1, DA)), fixed((1, NF))],
        out_specs=[hm] * 3 + [pl.BlockSpec((tile, NF), lambda i: (i, 0))],
        out_shape=[jax.ShapeDtypeStruct((H, s, N), F32)] * 3 + [jax.ShapeDtypeStruct((s, NF), F32)],
        scratch_shapes=[pltpu.VMEM((1, NF), F32)], compiler_params=_params("arbitrary"))(ub, uf, q_g, k_g, f_b)


def _fox_pre_bwd(ub, uf, q_g, k_g, f_b, dqn, dkn, dvf, dgate, dcum_q, dcum_k):
    s = ub.shape[0]
    tile = HEAD_TILE
    nt = s // tile

    def body(ub_ref, uf_ref, qg_ref, kg_ref, fb_ref, dq_ref, dk_ref, dv_ref, dgate_ref, dcq_ref, dck_ref,
             dub_ref, duf_ref, dqg_ref, dkg_ref, dfb_ref, carry):
        @pl.when(pl.program_id(0) == 0)
        def _():
            carry[...] = jnp.zeros_like(carry)
            for ref in (dqg_ref, dkg_ref, dfb_ref):
                ref[...] = jnp.zeros_like(ref)

        dcum = dcq_ref[...] + dck_ref[...]
        dlogf = lax.dot_general(_tri(tile), dcum, (((0,), (0,)), ((), ())), precision=HI,
                                preferred_element_type=F32) + carry[...]
        carry[...] = dlogf[0:1, :]
        _, vjp = jax.vjp(_fox_pre, ub_ref[:, :DA], ub_ref[:, DA:2 * DA], uf_ref[...], qg_ref[...], kg_ref[...],
                         fb_ref[...])
        d_q, d_k, d_f, d_qg, d_kg, d_fb = vjp((_from_heads(dq_ref), _from_heads(dk_ref), dlogf))
        dub_ref[...] = jnp.concatenate([d_q, d_k, _from_heads(dv_ref), dgate_ref[...]], axis=-1).astype(BF16)
        duf_ref[...] = d_f.astype(BF16)
        dqg_ref[...] += functools.reduce(jnp.add, _to_heads(d_qg))
        dkg_ref[...] += functools.reduce(jnp.add, _to_heads(d_kg))
        dfb_ref[...] += d_fb

    rev = lambda i: nt - 1 - i
    hm = pl.BlockSpec((H, tile, N), lambda i: (0, rev(i), 0))
    tok = lambda n: pl.BlockSpec((tile, n), lambda i: (rev(i), 0))
    fixed = lambda shape: pl.BlockSpec(shape, lambda i: (0,) * len(shape))
    return pl.pallas_call(
        body, name="fox_pre_bwd", grid=(nt,),
        in_specs=[tok(NB), tok(NF), fixed((1, DA)), fixed((1, DA)), fixed((1, NF)), hm, hm, hm, tok(DA), tok(NF),
                  tok(NF)],
        out_specs=[tok(NB), tok(NF), fixed((1, N)), fixed((1, N)), fixed((1, NF))],
        out_shape=[jax.ShapeDtypeStruct((s, NB), BF16), jax.ShapeDtypeStruct((s, NF), BF16),
                   jax.ShapeDtypeStruct((1, N), F32), jax.ShapeDtypeStruct((1, N), F32),
                   jax.ShapeDtypeStruct((1, NF), F32)],
        scratch_shapes=[pltpu.VMEM((1, NF), F32)],
        compiler_params=_params("arbitrary"))(ub, uf, q_g, k_g, f_b, dqn, dkn, dvf, dgate, dcum_q, dcum_k)


def _att_groups(s):
    blocks = s // ATT_TILE
    per = max(1, blocks // ATT_GROUPS)
    return per, blocks // per


def _att_parts(n, width):
    return ([(0, n - width, False)] if n > width else []) + [(n - width, n, True)]


def _att_scores(q_bf, k_ref, ck_ref, lo, hi, masked, row_offset):
    scores = _bdot_nt(q_bf, k_ref[0, lo:hi, :]) - ck_ref[0, :, lo:hi]
    if masked:
        rows = row_offset + lax.broadcasted_iota(jnp.int32, scores.shape, 0)
        scores = jnp.where(rows >= lax.broadcasted_iota(jnp.int32, scores.shape, 1), scores, -1e30)
    return scores


def _fox_attn_fwd(q, k, v, cum_q, cum_k):
    s = q.shape[1]
    t = ATT_TILE
    per, groups = _att_groups(s)

    def body(q_ref, k_ref, v_ref, cq_ref, ck_ref, o_ref, lse_ref):
        qi = pl.program_id(1)
        for g in range(groups):
            @pl.when(qi // per == g)
            def _(g=g):
                q_bf = (q_ref[0] * ATT_SCALE).astype(BF16)
                parts = _att_parts((g + 1) * per * t, per * t)
                scores = [_att_scores(q_bf, k_ref, ck_ref, lo, hi, masked, (qi - g * per) * t)
                          for lo, hi, masked in parts]
                m = functools.reduce(jnp.maximum, [jnp.max(sc, axis=-1, keepdims=True) for sc in scores])
                l, acc = 0.0, 0.0
                for sc, (lo, hi, _) in zip(scores, parts):
                    p = jnp.exp(sc - m)
                    l += jnp.sum(p, axis=-1, keepdims=True)
                    acc += _bdot(p, v_ref[0, lo:hi, :])
                o_ref[0] = acc / l
                lse_ref[0] = m + jnp.log(l) + cq_ref[0]

    qb = pl.BlockSpec((1, t, N), lambda h, i: (h, i, 0))
    kb = pl.BlockSpec((1, s, N), lambda h, i: (h, 0, 0))
    return pl.pallas_call(
        body, name="fox_attn_fwd", grid=(H, s // t),
        in_specs=[qb, kb, kb, pl.BlockSpec((1, t, 1), lambda h, i: (h, i, 0)),
                  pl.BlockSpec((1, 1, s), lambda h, i: (h, 0, 0))],
        out_specs=[qb, pl.BlockSpec((1, t, 1), lambda h, i: (h, i, 0))],
        out_shape=[jax.ShapeDtypeStruct((H, s, N), F32), jax.ShapeDtypeStruct((H, s, 1), F32)],
        compiler_params=_params("arbitrary", "arbitrary"))(q, k, v, cum_q, cum_k)


def _fox_attn_bwd(q, k, v, cum_q, cum_k, o, lse, do, slabs, owners):
    s = q.shape[1]
    t = ATT_TILE
    per, groups = _att_groups(s)
    nx = len(slabs)

    def body(q_ref, k_ref, v_ref, cq_ref, ck_ref, o_ref, lse_ref, do_ref, *refs):
        src_refs, (dq_ref, dk_ref, dv_ref, dcq_ref, dck_ref) = refs[:nx], refs[nx:nx + 5]
        start, wait = _exchange_ops(src_refs, refs[nx + 5:2 * nx + 5], owners, refs[2 * nx + 5:])
        qi = pl.program_id(1)

        @pl.when((pl.program_id(0) == 0) & (qi == 0))
        def _():
            start()

        @pl.when(qi == 0)
        def _():
            for ref in (dk_ref, dv_ref, dck_ref):
                ref[...] = jnp.zeros_like(ref)

        for g in range(groups):
            @pl.when(qi // per == g)
            def _(g=g):
                q_bf, do_bf = (q_ref[0] * ATT_SCALE).astype(BF16), do_ref[0].astype(BF16)
                row_term = cq_ref[0] - lse_ref[0]
                delta = jnp.sum(do_ref[0] * o_ref[0], axis=-1, keepdims=True)
                dq, dcq = 0.0, 0.0
                for lo, hi, masked in _att_parts((g + 1) * per * t, per * t):
                    p = jnp.exp(_att_scores(q_bf, k_ref, ck_ref, lo, hi, masked, (qi - g * per) * t) + row_term)
                    ds = p * (_bdot_nt(do_bf, v_ref[0, lo:hi, :]) - delta)
                    dq += _bdot(ds, k_ref[0, lo:hi, :])
                    dcq += jnp.sum(ds, axis=-1, keepdims=True)
                    dk_ref[0, lo:hi, :] += _bdot_tn(ds, q_bf)
                    dv_ref[0, lo:hi, :] += _bdot_tn(p, do_bf)
                    dck_ref[0, :, lo:hi] -= jnp.sum(ds, axis=0, keepdims=True)
                dq_ref[0] = dq * ATT_SCALE
                dcq_ref[0] = dcq

        @pl.when((pl.program_id(0) == H - 1) & (qi == s // t - 1))
        def _():
            wait()

    qb = pl.BlockSpec((1, t, N), lambda h, i: (h, i, 0))
    kb = pl.BlockSpec((1, s, N), lambda h, i: (h, 0, 0))
    cqb = pl.BlockSpec((1, t, 1), lambda h, i: (h, i, 0))
    ckb = pl.BlockSpec((1, 1, s), lambda h, i: (h, 0, 0))
    f32 = lambda *shape: jax.ShapeDtypeStruct(shape, F32)
    out = pl.pallas_call(
        body, name="fox_attn_bwd", grid=(H, s // t),
        in_specs=[qb, kb, kb, cqb, ckb, qb, cqb, qb] + _hbm_specs(nx), out_specs=[qb, kb, kb, cqb, ckb] + _hbm_specs(nx),
        out_shape=[f32(H, s, N), f32(H, s, N), f32(H, s, N), f32(H, s, 1), f32(H, 1, s)]
        + _received_shapes(slabs, owners),
        scratch_shapes=_exchange_scratch(nx),
        compiler_params=_params("arbitrary", "arbitrary"))(q, k, v, cum_q, cum_k, o, lse, do, *slabs)
    return out[:5], out[5:]


def _local_step(x, target, w, p):
    mu = p["shift_mu"]
    lora_matrix = lambda a: jnp.moveaxis(a, 0, 1).reshape(RANK, DA).astype(F32)
    pre_params = (mu[:, 0:DA], mu[:, DA:2 * DA], mu[:, 2 * DA:3 * DA], mu[:, 3 * DA + 2 * RANK:],
                  mu[:, 3 * DA:3 * DA + RANK], mu[:, 3 * DA + RANK:3 * DA + 2 * RANK],
                  lora_matrix(w["w_lora_up"]), p["w0"], lora_matrix(w["a_lora_up"]), p["a0"], p["k_k"], p["k_a"])
    post_params = (p["lnx_w"], p["lnx_b"], p["r_k"])
    q_g, k_g = jnp.tile(p["q_norm_g"], (1, H)), jnp.tile(p["k_norm_g"], (1, H))
    f_b = jnp.pad(p["f_bias"], ((0, 0), (0, NF - H)))
    fg = p["final_norm_g"].reshape(1, D)

    h, (ua, ub, ug, uf) = _norm_proj(x, p["norm_g"], (w["in_a"], w["in_b"], w["in_g"], w["in_f"]))
    r, lw, cl, k2, v, av, bv, gg = _rwkv_pre_fwd(ua, pre_params)
    y, ckpt, pinv = _wkv_fwd((r, lw, cl, k2, v, av, bv))
    ya = _rwkv_post_fwd(y, r, k2, v, gg, post_params)
    qn, kn, vf, cum = _fox_pre_fwd(ub, uf, q_g, k_g, f_b)
    cum_t = cum[:, :H].T
    cum_q, cum_k = cum_t[:, :, None], cum_t[:, None, :]
    o, lse = _fox_attn_fwd(qn, kn, vf, cum_q, cum_k)

    (loss, dfg, dwo, dwoa, dwob, dx2, dya, do, dgate_b, dug) = _tail(
        x, target, ya, o, ub, ug, w["w_out_a"], w["w_out_b"], w["w_out"], fg)
    everyone = (0, N_DEV)
    (dqn, dkn, dvf, dcq, dck), (recv_woa, recv_wob, recv_wo) = _fox_attn_bwd(
        qn, kn, vf, cum_q, cum_k, o, lse, do,
        (_col_slabs(dwoa), _col_slabs(dwob), dwo.astype(BF16).reshape(N_DEV, D // N_DEV, D)), (everyone,) * 3)
    pad_f = lambda a: jnp.pad(a.T, ((0, 0), (0, NF - H)))
    dub, duf, dqg, dkg, dfb = _fox_pre_bwd(ub, uf, q_g, k_g, f_b, dqn, dkn, dvf, dgate_b,
                                           pad_f(dcq[:, :, 0]), pad_f(dck.reshape(H, -1)))
    spill = EARLY_FROM * COLS_PER_DEV - NA
    early, dwt_b_head = _proj_wgrad_early(h, dub, dug, duf, -(-spill // 8) * 8)
    dy, dr_p, dk_p, dv_p, dgg, dlnw, dlnb, drk, handed = _rwkv_post_bwd(y, r, k2, v, gg, post_params, dya, early,
                                                                          EARLY_FROM)
    early = _chip_sums(early, handed, EARLY_FROM, "chip_sums_early")
    (dr_s, dlw, dcl, dk_s, dv_s, dav, dbv), (recv_early,) = _wkv_bwd(
        (r, lw, cl, k2, v, av, bv), ckpt, pinv, dy, (early,), ((EARLY_FROM, N_DEV, "chips"),))
    pre_out = _rwkv_pre_bwd(ua, pre_params, (dr_s, dr_p, dlw, dcl, dk_s, dk_p, dv_s, dv_p, dav, dbv, dgg))
    dua, dpre = pre_out[0], pre_out[1:]
    late = _proj_wgrad_late(h, dua, dwt_b_head)

    flat = lambda a: a.reshape(1, -1)
    small = {
        "final_norm_g": dfg, "w0": dpre[7], "a0": dpre[9], "k_k": dpre[10], "k_a": dpre[11], "r_k": drk, "lnx_w": dlnw,
        "lnx_b": dlnb, "q_norm_g": dqg, "k_norm_g": dkg, "f_bias": dfb[:, :H],
        "shift_mu": jnp.concatenate([flat(dpre[0]), flat(dpre[1]), flat(dpre[2]), dpre[4], dpre[5], flat(dpre[3])], axis=1),
    }
    late = _chip_sums(late, _pair_swap(late, 0, "pair_swap_late"), 0, "chip_sums_late")
    by_head = lambda a: jnp.moveaxis(a.reshape(RANK, H, N), 1, 0)
    loras = jnp.stack([by_head(dpre[6]), by_head(dpre[8])], axis=1).astype(BF16)
    dx, dng, (recv_late, recv_lora, recv_small) = _proj_xgrad(
        x, p["norm_g"], dx2, (dua, dub, dug, duf), (w["in_a"], w["in_b"], w["in_g"], w["in_f"]),
        (late, loras, _pack_small(small, loss)), ((0, EARLY_FROM, "chips"), everyone, everyone))
    return dx, dng, (recv_early, recv_late), (recv_woa, recv_wob, recv_wo, recv_lora), recv_small


def _position():
    return lax.axis_index("x"), lax.axis_index("y"), lax.axis_index("c")


def _hbm_specs(n):
    return [pl.BlockSpec(memory_space=pl.ANY)] * n


BIG_GATHER_COPIES = 13
GATHER_ROW_CUT = 400


def _all_gather(big, blocks, name):
    n = len(blocks)

    def body(*refs):
        big_ref, x_refs = refs[0], refs[1:1 + n]
        big_out, out_refs = refs[1 + n], refs[2 + n:2 + 2 * n]
        send_sems, recv_sems, local_sems = refs[2 + 2 * n:]
        x, y, c = _position()
        me, sibling = (x, y, c), (x, y, 1 - c)
        chips = [(1 - x, y), (x, 1 - y), (1 - x, 1 - y)]
        x_nbr, y_nbr, diag = chips
        rows = big_ref.shape[0]
        cut = GATHER_ROW_CUT

        def part(ref, h):
            return ref if h is None else ref.at[pl.ds(0, cut)] if h == 0 else ref.at[pl.ds(cut, rows - cut)]

        def landed(chip, core, h):
            return part(big_out.at[4 * chip[0] + 2 * chip[1] + core], h)

        def big_copy(k, src, dst, to):
            return pltpu.make_async_remote_copy(src_ref=src, dst_ref=dst, send_sem=send_sems.at[7 * n + k],
                                                recv_sem=recv_sems.at[7 * n + k], device_id=to, device_id_type=MESH)

        def arrival(k, chip, core, h):
            dst = landed(chip, core, h)
            return big_copy(k, dst, dst, me)

        def pass_on(k, chip, h, to):
            src = landed(chip, c, h)
            return big_copy(k, src, src, to)

        big_mine = pltpu.make_async_copy(big_ref, landed((x, y), c, None), local_sems.at[n])
        big_mine.start()
        here = (x, y)
        big_sent = [big_copy(0, big_ref, landed(here, c, None), sibling),
                    big_copy(1, part(big_ref, 0), landed(here, c, 0), (*x_nbr, c)),
                    big_copy(2, part(big_ref, 1), landed(here, c, 1), (*y_nbr, c)),
                    big_copy(3, part(big_ref, 1), landed(here, c, 1), (*x_nbr, c)),
                    big_copy(4, part(big_ref, 0), landed(here, c, 0), (*y_nbr, c))]
        for cp in big_sent:
            cp.start()

        def copy(a, k, blk, to, own=False):
            dst = out_refs[a].at[4 * blk[0] + 2 * blk[1] + blk[2]]
            return pltpu.make_async_remote_copy(
                src_ref=x_refs[a] if own else dst, dst_ref=dst, send_sem=send_sems.at[7 * a + k],
                recv_sem=recv_sems.at[7 * a + k], device_id=to, device_id_type=MESH)

        mine = [pltpu.make_async_copy(x_refs[a], out_refs[a].at[4 * x + 2 * y + c], local_sems.at[a]) for a in range(n)]
        for cp in mine:
            cp.start()
        first = []
        for a in range(n):
            first.append(copy(a, 0, me, sibling, own=True))
            first += [copy(a, 1 + j, me, (*chip, c), own=True) for j, chip in enumerate(chips)]
        for cp in first:
            cp.start()

        big_steps = [(1, x_nbr, 0, (*y_nbr, c), 5, 7), (2, y_nbr, 1, (*x_nbr, c), 6, 8), (3, x_nbr, 1, None, None, 9),
                     (4, y_nbr, 0, None, None, 10), (5, diag, 0, None, None, 11), (6, diag, 1, None, None, 12)]
        for k, chip, h, onward, k_onward, k_sibling in big_steps:
            arrival(k, chip, c, h).wait_recv()
            if onward is not None:
                big_sent.append(pass_on(k_onward, chip, h, onward))
                big_sent[-1].start()
            big_sent.append(pass_on(k_sibling, chip, h, sibling))
            big_sent[-1].start()

        passed = []
        for j, chip in enumerate(chips):
            for a in range(n):
                copy(a, 1 + j, (*chip, c), me).wait_recv()
                passed.append(copy(a, 4 + j, (*chip, c), sibling))
                passed[-1].start()
        for a in range(n):
            copy(a, 0, sibling, me).wait_recv()
        for j, chip in enumerate(chips):
            for a in range(n):
                copy(a, 4 + j, (*chip, 1 - c), me).wait_recv()
        arrival(0, here, 1 - c, None).wait_recv()
        for k, chip, h, _, _, k_sibling in big_steps:
            arrival(k_sibling, chip, 1 - c, h).wait_recv()
        for cp in first + passed + big_sent:
            cp.wait_send()
        for cp in mine + [big_mine]:
            cp.wait()

    everything = [big] + list(blocks)
    return pl.pallas_call(
        body, name=name, out_shape=[jax.ShapeDtypeStruct((N_DEV,) + b.shape, b.dtype) for b in everything],
        in_specs=_hbm_specs(n + 1), out_specs=_hbm_specs(n + 1),
        scratch_shapes=[pltpu.SemaphoreType.DMA((7 * n + BIG_GATHER_COPIES,)),
                        pltpu.SemaphoreType.DMA((7 * n + BIG_GATHER_COPIES,)), pltpu.SemaphoreType.DMA((n + 1,))],
    )(*everything)


def _received_shapes(slabs, owners):
    return [jax.ShapeDtypeStruct((N_DEV // 2 if len(o) == 3 else N_DEV,) + s.shape[1:], s.dtype)
            for s, o in zip(slabs, owners)]


def _pair_swap_scratch(n):
    return [pltpu.SemaphoreType.DMA((n,)), pltpu.SemaphoreType.DMA((n,))]


def _pair_swap_ops(s_ref, p_ref, lo, sems):
    send_sems, recv_sems = sems
    n = s_ref.shape[0]

    def run(sending):
        x, y, c = _position()
        for side in (0, 1):
            mine = [pltpu.make_async_remote_copy(src_ref=s_ref.at[i], dst_ref=p_ref.at[i], send_sem=send_sems.at[i],
                                                 recv_sem=recv_sems.at[i], device_id=(x, y, 1 - c), device_id_type=MESH)
                    for i in range(n) if (lo + i) % 2 == side]

            @pl.when(c != side)
            def _():
                for cp in mine:
                    cp.start() if sending else cp.wait_send()

            if not sending:
                @pl.when(c == side)
                def _():
                    for cp in mine:
                        cp.wait_recv()

    return functools.partial(run, True), functools.partial(run, False)


def _pair_swap(slabs, lo, name):
    n = slabs.shape[0]

    def body(s_ref, p_ref, *sems):
        start, wait = _pair_swap_ops(s_ref, p_ref, lo, sems)
        start()
        wait()

    return pl.pallas_call(
        body, name=name, out_shape=jax.ShapeDtypeStruct(slabs.shape, slabs.dtype),
        in_specs=_hbm_specs(1), out_specs=_hbm_specs(1)[0], scratch_shapes=_pair_swap_scratch(n))(slabs)


def _chip_sums(slabs, swapped, lo, name):
    n, rows, cols = slabs.shape
    tile = W_IN_COL_TILE

    def body(s_ref, p_ref, o_ref):
        c = lax.axis_index("c")
        for i in range(n):
            @pl.when(c == (lo + i) % 2)
            def _(i=i):
                o_ref[i] = (s_ref[i].astype(F32) + p_ref[i].astype(F32)).astype(BF16)

    blk = pl.BlockSpec((n, rows, tile), lambda j: (0, 0, j))
    return pl.pallas_call(
        body, name=name, grid=(cols // tile,), in_specs=[blk, blk], out_specs=blk,
        out_shape=jax.ShapeDtypeStruct(slabs.shape, BF16), compiler_params=_params("arbitrary"))(slabs, swapped)


def _exchange_scratch(n):
    return [pltpu.SemaphoreType.DMA((7 * n,)), pltpu.SemaphoreType.DMA((7 * n,)), pltpu.SemaphoreType.DMA((n,))]


def _exchange_ops(src_refs, dst_refs, owners, sems):
    send_sems, recv_sems, local_sems = sems
    n = len(src_refs)

    def guarded(a, dev, fn):
        lo, hi = owners[a][:2]
        if (lo, hi) == (0, N_DEV):
            fn()
        else:
            pl.when((dev >= lo) & (dev < hi))(fn)

    def src(a, dev):
        ref = src_refs[a]
        return ref.at[0] if ref.shape[0] == 1 else ref.at[dev - owners[a][0]]

    def run(sending, waiting):
        x, y, c = _position()
        me = 4 * x + 2 * y + c
        for a in range(n):
            by_chip = len(owners[a]) == 3
            slot = (lambda qx, qy, qc: 2 * qx + qy) if by_chip else (lambda qx, qy, qc: 4 * qx + 2 * qy + qc)
            mine = slot(x, y, c)
            local = lambda a=a, mine=mine: pltpu.make_async_copy(src(a, me), dst_refs[a].at[mine], local_sems.at[a])
            if sending:
                guarded(a, me, lambda local=local: local().start())
            for m in range(2, N_DEV, 2) if by_chip else range(1, N_DEV):
                px, py, pc = x ^ (m >> 2), y ^ ((m >> 1) & 1), c ^ (m & 1)
                peer = 4 * px + 2 * py + pc
                theirs = slot(px, py, pc)
                sem = dict(send_sem=send_sems.at[7 * a + m - 1], recv_sem=recv_sems.at[7 * a + m - 1],
                           device_id=(px, py, pc), device_id_type=MESH)
                send = lambda a=a, peer=peer, sem=sem, mine=mine: pltpu.make_async_remote_copy(
                    src_ref=src(a, peer), dst_ref=dst_refs[a].at[mine], **sem)
                recv = lambda a=a, sem=sem, theirs=theirs: pltpu.make_async_remote_copy(
                    src_ref=src(a, me), dst_ref=dst_refs[a].at[theirs], **sem)
                if sending:
                    guarded(a, peer, lambda send=send: send().start())
                if waiting:
                    guarded(a, me, lambda recv=recv: recv().wait_recv())
                    guarded(a, peer, lambda send=send: send().wait_send())
            if waiting:
                guarded(a, me, lambda local=local: local().wait())

    return functools.partial(run, True, False), functools.partial(run, False, True)


def _sum_slabs(r_ref):
    g = r_ref[0].astype(F32)
    for k in range(1, r_ref.shape[0]):
        g = g + r_ref[k].astype(F32)
    return g


def _adamw(g, w, m, v):
    m_new = ADAM_B1 * m + (1.0 - ADAM_B1) * g
    v_new = ADAM_B2 * v + (1.0 - ADAM_B2) * (g * g)
    m_hat = m_new / (1.0 - ADAM_B1 ** ADAM_STEP)
    v_hat = v_new / (1.0 - ADAM_B2 ** ADAM_STEP)
    return g, -ADAM_LR * (m_hat / (jnp.sqrt(v_hat) + ADAM_EPS) + ADAM_WD * w), m_new, v_new


def _adamw_w_in(recv_early, recv_late, w, m, v, slabs, owners):
    rows, cols = w.shape
    tile = W_IN_COL_TILE
    nx = len(slabs)

    def body(early_ref, late_ref, w_ref, m_ref, v_ref, *refs):
        src_refs, o_refs, dst_refs = refs[:nx], refs[nx:nx + 4], refs[nx + 4:2 * nx + 4]
        start, wait = _exchange_ops(src_refs, dst_refs, owners, refs[2 * nx + 4:])
        x, y, c = _position()
        early_owner = 4 * x + 2 * y + c >= EARLY_FROM

        @pl.when(pl.program_id(0) == 0)
        def _():
            start()

        def update(g):
            for o_ref, val in zip(o_refs, _adamw(g, w_ref[...], m_ref[...], v_ref[...])):
                o_ref[...] = val

        pl.when(early_owner)(lambda: update(_sum_slabs(early_ref)))
        pl.when(jnp.logical_not(early_owner))(lambda: update(_sum_slabs(late_ref)))

        @pl.when(pl.program_id(0) == cols // tile - 1)
        def _():
            wait()

    blk = pl.BlockSpec((rows, tile), lambda i: (0, i))
    slots = lambda r: pl.BlockSpec((r.shape[0], rows, tile), lambda i: (0, 0, i))
    out = pl.pallas_call(
        body, name="adamw_w_in", grid=(cols // tile,),
        in_specs=[slots(recv_early), slots(recv_late), blk, blk, blk] + _hbm_specs(nx),
        out_specs=[blk] * 4 + _hbm_specs(nx),
        out_shape=[jax.ShapeDtypeStruct((rows, cols), F32)] * 4 + _received_shapes(slabs, owners),
        scratch_shapes=_exchange_scratch(nx),
        compiler_params=_params("arbitrary"))(recv_early, recv_late, w, m, v, *slabs)
    return out[:4], out[4:]


def _adamw_misc(recvs, recv_small, recv_norm, params):
    names = list(params)
    flat = [a for n in names for a in params[n]]

    def body(woa_ref, wob_ref, wo_ref, lora_ref, small_ref, norm_ref, *refs):
        p_refs, o_refs = refs[:len(flat)], refs[len(flat):]
        g_small = _sum_slabs(small_ref)
        g_lora = _sum_slabs(lora_ref)
        grads = {"w_out_a": _sum_slabs(woa_ref), "w_out_b": _sum_slabs(wob_ref), "w_out": _sum_slabs(wo_ref),
                 "w_lora_up": g_lora[0], "a_lora_up": g_lora[1], "norm_g": _sum_slabs(norm_ref)}
        for n, (off, size) in SMALL_SLOTS.items():
            grads[n] = g_small[:, off:off + size]
        for i, n in enumerate(names):
            w_ref, m_ref, v_ref = p_refs[3 * i:3 * i + 3]
            for o_ref, val in zip(o_refs[4 * i:4 * i + 4], _adamw(grads[n], w_ref[...], m_ref[...], v_ref[...])):
                o_ref[...] = val
        o_refs[-1][...] = g_small[:, LOSS_SLOT:LOSS_SLOT + 1]

    out = pl.pallas_call(
        body, name="adamw_misc",
        out_shape=[jax.ShapeDtypeStruct(params[n][0].shape, F32) for n in names for _ in range(4)]
        + [jax.ShapeDtypeStruct((1, 1), F32)],
        compiler_params=_params())(*recvs, recv_small, recv_norm, *flat)
    return {n: out[4 * i:4 * i + 4] for i, n in enumerate(names)}, out[-1]


_WT_SEGMENTS = ((0, NA), (NA, NB), (NA + NB + H, NG), (NA + NB, H))


def _split_wt(gathered):
    tile = W_IN_COL_TILE

    def body(g_ref, *o_refs):
        full = jnp.concatenate([g_ref[j] for j in range(N_DEV)], axis=0)
        for o_ref, (row, n) in zip(o_refs, _WT_SEGMENTS):
            seg = full[row:row + n]
            if n < o_ref.shape[0]:
                seg = jnp.concatenate([seg, jnp.zeros((o_ref.shape[0] - n, tile), BF16)], axis=0)
            o_ref[...] = seg

    sizes = (NA, NB, NG, NF)
    return pl.pallas_call(
        body, name="split_wt", grid=(D // tile,),
        in_specs=[pl.BlockSpec((N_DEV, COLS_PER_DEV, tile), lambda i: (0, 0, i))],
        out_specs=[pl.BlockSpec((n, tile), lambda i: (0, i)) for n in sizes],
        out_shape=[jax.ShapeDtypeStruct((n, D), BF16) for n in sizes],
        compiler_params=_params("arbitrary"))(gathered)


def _by_cols(a):
    return jnp.moveaxis(a, 0, 1).reshape(a.shape[1], -1)


def _col_slabs(a):
    return jnp.moveaxis(a.reshape(a.shape[0], N_DEV, -1), 1, 0).astype(BF16)


def _pack_small(grads, loss):
    pieces, at = [], 0
    for n, (off, size) in list(SMALL_SLOTS.items()) + [("loss", (LOSS_SLOT, 1))]:
        pieces += [jnp.zeros((off - at,), F32), (loss if n == "loss" else grads[n]).reshape(-1)]
        at = off + size
    return jnp.concatenate(pieces + [jnp.zeros((SMALL_LEN - at,), F32)]).reshape(1, 1, SMALL_LEN)


def _gather_weights(t):
    cast = lambda a: a.astype(BF16)
    loras = jnp.stack([t["w_lora_up"][0], t["a_lora_up"][0]])
    wt, woa, wob, wo, lora = _all_gather(
        cast(t["w_in"][0].T), [cast(t["w_out_a"][0]), cast(t["w_out_b"][0]), cast(t["w_out"][0]), cast(loras)],
        "weight_gather")
    in_a, in_b, in_g, in_f = _split_wt(wt)
    return {"in_a": in_a, "in_b": in_b, "in_g": in_g, "in_f": in_f, "w_out_a": _by_cols(woa), "w_out_b": _by_cols(wob),
            "w_out": wo.reshape(D, D), "w_lora_up": lora[:, 0], "a_lora_up": lora[:, 1]}


def kernel(x, norm_g, w_in, shift_mu, w_lora_up, w0, a_lora_up, a0, k_k, k_a, r_k, lnx_w, lnx_b, f_bias, q_norm_g, k_norm_g, w_out_a, w_out_b, w_out, final_norm_g, loss_target, m_norm_g, m_w_in, m_shift_mu, m_w_lora_up, m_w0, m_a_lora_up, m_a0, m_k_k, m_k_a, m_r_k, m_lnx_w, m_lnx_b, m_f_bias, m_q_norm_g, m_k_norm_g, m_w_out_a, m_w_out_b, m_w_out, m_final_norm_g, v_norm_g, v_w_in, v_shift_mu, v_w_lora_up, v_w0, v_a_lora_up, v_a0, v_k_k, v_k_a, v_r_k, v_lnx_w, v_lnx_b, v_f_bias, v_q_norm_g, v_k_norm_g, v_w_out_a, v_w_out_b, v_w_out, v_final_norm_g):
    names = ("norm_g", "w_in", "shift_mu", "w_lora_up", "w0", "a_lora_up", "a0", "k_k", "k_a", "r_k", "lnx_w", "lnx_b",
             "f_bias", "q_norm_g", "k_norm_g", "w_out_a", "w_out_b", "w_out", "final_norm_g")
    weights = dict(zip(names, (norm_g, w_in, shift_mu, w_lora_up, w0, a_lora_up, a0, k_k, k_a, r_k, lnx_w, lnx_b,
                               f_bias, q_norm_g, k_norm_g, w_out_a, w_out_b, w_out, final_norm_g)))
    m_in = dict(zip(names, (m_norm_g, m_w_in, m_shift_mu, m_w_lora_up, m_w0, m_a_lora_up, m_a0, m_k_k, m_k_a, m_r_k,
                            m_lnx_w, m_lnx_b, m_f_bias, m_q_norm_g, m_k_norm_g, m_w_out_a, m_w_out_b, m_w_out,
                            m_final_norm_g)))
    v_in = dict(zip(names, (v_norm_g, v_w_in, v_shift_mu, v_w_lora_up, v_w0, v_a_lora_up, v_a0, v_k_k, v_k_a, v_r_k,
                            v_lnx_w, v_lnx_b, v_f_bias, v_q_norm_g, v_k_norm_g, v_w_out_a, v_w_out_b, v_w_out,
                            v_final_norm_g)))

    matrices = ("w_out_a", "w_out_b", "w_out", "w_lora_up", "a_lora_up")
    as_2d = lambda n, a: a[0] if n in matrices else a.reshape(1, -1)

    full = _gather_weights(weights)
    dx, dng, recv_wt, recvs, recv_small = _local_step(
        x[0], loss_target[0], full, {n: as_2d(n, weights[n]) for n in ("norm_g",) + tuple(SMALL_SLOTS)})

    res, (recv_norm,) = _adamw_w_in(*recv_wt, w_in[0].T, m_w_in[0].T, v_w_in[0].T, (dng[None],), ((0, N_DEV),))
    outs = {"w_in": [r.T[None] for r in res]}
    misc = [n for n in names if n != "w_in"]
    res, loss_sum = _adamw_misc(recvs, recv_small, recv_norm,
                                {n: tuple(as_2d(n, t[n]) for t in (weights, m_in, v_in)) for n in misc})
    for n in misc:
        outs[n] = [r.reshape(weights[n].shape) for r in res[n]]
    return (loss_sum.reshape(()), dx[None], *[outs[n][i] for i in range(4) for n in names])
```
